```python
import jax, jax.numpy as jnp
from jax import lax
import numpy as np

D_MODEL = 1024
BATCH = 16
SEQ = 2048
DEPTH = 2

N_MIXERS = 2
D_RNN = 1280
LRU_BLOCKS = 16
LRU_BLOCK_W = D_RNN // LRU_BLOCKS
CONV_W = 4
LRU_C = 8.0
N_HEADS = 16
N_KV_HEADS = 4
HEAD_DIM = 64
GROUP = N_HEADS // N_KV_HEADS
WINDOW = 128
ATT_BLOCK = WINDOW
Q_DIM = N_HEADS * HEAD_DIM
KV_DIM = N_KV_HEADS * HEAD_DIM
QKV_DIM = Q_DIM + 2 * KV_DIM
N_GROUPS = 4
EXPERTS_PER_GROUP = 8
N_EXPERTS = N_GROUPS * EXPERTS_PER_GROUP
TOP_K = 2
D_EXPERT = 512
MOE_BLOCK = 128
ALPHA = (2 * DEPTH) ** 0.25
BETA = (8 * DEPTH) ** -0.25
LN_EPS = 1e-5
N_REC_LAYERS = (DEPTH + N_MIXERS - 1) // N_MIXERS
N_ATT_LAYERS = DEPTH // N_MIXERS

kernel_name = 'hybrid_rglru_swa_sink_alibi_hmoe'


def layer_norm(x, g, b):
    xf = x.astype(jnp.float32)
    mu = jnp.mean(xf, axis=-1, keepdims=True)
    xc = xf - mu
    var = jnp.mean(xc * xc, axis=-1, keepdims=True)
    y = xc * lax.rsqrt(var + LN_EPS) * g.astype(jnp.float32) + b.astype(jnp.float32)
    return y.astype(x.dtype)


def _linear_recurrence_combine(c1, c2):
    a1, b1 = c1
    a2, b2 = c2
    return a1 * a2, a2 * b1 + b2


def rglru_block(x, w_in, conv_w, conv_b, w_r, b_r, w_i, b_i, lam, w_out):
    B, S, _ = x.shape
    gate, xr = jnp.split(x @ w_in, 2, axis=-1)
    xp = jnp.pad(xr, ((0, 0), (CONV_W - 1, 0), (0, 0)))
    xc = conv_b + sum(xp[:, k:k + S] * conv_w[k] for k in range(CONV_W))
    xb = xc.reshape(B, S, LRU_BLOCKS, LRU_BLOCK_W)
    r = jax.nn.sigmoid(jnp.einsum('bsnk,nkj->bsnj', xb, w_r).reshape(B, S, D_RNN) + b_r)
    i = jax.nn.sigmoid(jnp.einsum('bsnk,nkj->bsnj', xb, w_i).reshape(B, S, D_RNN) + b_i)
    log_a = -LRU_C * r.astype(jnp.float32) * jax.nn.softplus(-lam.astype(jnp.float32))
    a = jnp.exp(log_a)
    u = jnp.sqrt(-jnp.expm1(2.0 * log_a)) * (i * xc).astype(jnp.float32)
    _, h = lax.associative_scan(_linear_recurrence_combine, (a, u), axis=1)
    y = h.astype(x.dtype) * jax.nn.gelu(gate)
    return y @ w_out


def alibi_slopes():
    return 2.0 ** (-8.0 * jnp.arange(1, N_HEADS + 1, dtype=jnp.float32) / N_HEADS)


def _with_prev_block(t):
    prev = jnp.concatenate([jnp.zeros_like(t[:, :1]), t[:, :-1]], axis=1)
    return jnp.concatenate([prev, t], axis=2)


def swa_sink_alibi(x, w_qkv, sinks, w_o):
    B, S, _ = x.shape
    NB = S // ATT_BLOCK
    qkv = x @ w_qkv
    q = qkv[..., :Q_DIM].reshape(B, NB, ATT_BLOCK, N_KV_HEADS, GROUP, HEAD_DIM) * (HEAD_DIM ** -0.5)
    k = qkv[..., Q_DIM:Q_DIM + KV_DIM].reshape(B, NB, ATT_BLOCK, N_KV_HEADS, HEAD_DIM)
    v = qkv[..., Q_DIM + KV_DIM:].reshape(B, NB, ATT_BLOCK, N_KV_HEADS, HEAD_DIM)
    kk = _with_prev_block(k)
    vv = _with_prev_block(v)
    s = jnp.einsum('bnqkgd,bnskd->bnkgqs', q, kk).astype(jnp.float32)
    qi = jnp.arange(ATT_BLOCK)[:, None]
    sj = jnp.arange(2 * ATT_BLOCK)[None, :]
    dist = qi - sj + ATT_BLOCK
    valid = (dist >= 0) & (dist < WINDOW)
    not_before_start = (jnp.arange(NB)[:, None, None] > 0) | (sj[None] >= ATT_BLOCK)
    mask = valid[None] & not_before_start
    slopes = alibi_slopes().reshape(N_KV_HEADS, GROUP)
    s = s - slopes[:, :, None, None] * dist.astype(jnp.float32)
    s = jnp.where(mask[None, :, None, None], s, -jnp.inf)
    sink = sinks.astype(jnp.float32).reshape(N_KV_HEADS, GROUP)[None, None, :, :, None, None]
    m = jnp.maximum(jnp.max(s, axis=-1, keepdims=True), sink)
    p = jnp.exp(s - m)
    denom = jnp.sum(p, axis=-1, keepdims=True) + jnp.exp(sink - m)
    o = jnp.einsum('bnkgqs,bnskd->bnqkgd', (p / denom).astype(x.dtype), vv)
    return o.reshape(B, S, Q_DIM) @ w_o


def hier_moe(x, w_rg, b_rg, w_re, b_re, w1, w3, w2):
    B, S, D = x.shape
    T = B * S
    TK = T * TOP_K
    xf = x.reshape(T, D)
    g_logits = (xf @ w_rg + b_rg).astype(jnp.float32)
    g_idx = jnp.argmax(g_logits, axis=-1)
    g_gate = jnp.take_along_axis(jax.nn.softmax(g_logits, axis=-1), g_idx[:, None], axis=1)
    e_logits = (xf @ w_re + b_re).astype(jnp.float32).reshape(T, N_GROUPS, EXPERTS_PER_GROUP)
    e_in_group = jnp.take_along_axis(e_logits, g_idx[:, None, None], axis=1)[:, 0]
    top_v, top_i = lax.top_k(e_in_group, TOP_K)
    gates = jax.nn.softmax(top_v, axis=-1) * g_gate
    expert = g_idx[:, None] * EXPERTS_PER_GROUP + top_i
    flat_e = expert.reshape(TK)
    order = jnp.argsort(flat_e)
    sorted_e = flat_e[order]
    counts = jnp.bincount(flat_e, length=N_EXPERTS)
    padded = ((counts + MOE_BLOCK - 1) // MOE_BLOCK) * MOE_BLOCK
    starts = jnp.cumsum(counts) - counts
    pends = jnp.cumsum(padded)
    pstarts = pends - padded
    dest = pstarts[sorted_e] + (jnp.arange(TK) - starts[sorted_e])
    slot_dest = jnp.zeros_like(dest).at[order].set(dest)
    n_blocks = -(-(TK + N_EXPERTS * (MOE_BLOCK - 1)) // MOE_BLOCK)
    n_rows = n_blocks * MOE_BLOCK
    xbuf = jnp.zeros((n_rows, D), x.dtype).at[slot_dest].set(jnp.repeat(xf, TOP_K, axis=0))
    block_start = jnp.arange(n_blocks) * MOE_BLOCK
    block_e = jnp.minimum(jnp.sum(block_start[:, None] >= pends[None, :], axis=1), N_EXPERTS - 1)

    def expert_block(args):
        xb, e = args
        hdn = jax.nn.silu(xb @ w1[e]) * (xb @ w3[e])
        return hdn @ w2[e]

    ybuf = lax.map(expert_block, (xbuf.reshape(n_blocks, MOE_BLOCK, D), block_e)).reshape(n_rows, D)
    y_slots = ybuf[slot_dest].reshape(T, TOP_K, D)
    out = jnp.einsum('tk,tkd->td', gates.astype(x.dtype), y_slots)
    return out.reshape(B, S, D)


def setup_inputs(seed: int = 0) -> dict:
    key = jax.random.key(seed)
    ks = jax.random.split(key, 24)

    def nrm(k, shape, scale):
        return jax.random.normal(k, shape, jnp.float32) * scale

    x = nrm(ks[0], (BATCH, SEQ, D_MODEL), 1.0)
    rec_w_in = nrm(ks[1], (N_REC_LAYERS, D_MODEL, 2 * D_RNN), D_MODEL ** -0.5)
    rec_conv_w = nrm(ks[2], (N_REC_LAYERS, CONV_W, D_RNN), CONV_W ** -0.5)
    rec_conv_b = nrm(ks[3], (N_REC_LAYERS, D_RNN), 0.01)
    rec_w_r = nrm(ks[4], (N_REC_LAYERS, LRU_BLOCKS, LRU_BLOCK_W, LRU_BLOCK_W), LRU_BLOCK_W ** -0.5)
    rec_b_r = nrm(ks[5], (N_REC_LAYERS, D_RNN), 0.01)
    rec_w_i = nrm(ks[6], (N_REC_LAYERS, LRU_BLOCKS, LRU_BLOCK_W, LRU_BLOCK_W), LRU_BLOCK_W ** -0.5)
    rec_b_i = nrm(ks[7], (N_REC_LAYERS, D_RNN), 0.01)
    a_c = jax.random.uniform(ks[8], (N_REC_LAYERS, D_RNN), jnp.float32, minval=0.9, maxval=0.999)
    a0 = a_c ** (1.0 / LRU_C)
    rec_lambda = jnp.log(a0) - jnp.log1p(-a0)
    rec_w_out = nrm(ks[9], (N_REC_LAYERS, D_RNN, D_MODEL), BETA * D_RNN ** -0.5)
    col_scale = jnp.concatenate([jnp.ones((Q_DIM + KV_DIM,), jnp.float32), jnp.full((KV_DIM,), BETA, jnp.float32)])
    att_w_qkv = nrm(ks[10], (N_ATT_LAYERS, D_MODEL, QKV_DIM), D_MODEL ** -0.5) * col_scale
    att_sinks = nrm(ks[11], (N_ATT_LAYERS, N_HEADS), 0.5)
    att_w_o = nrm(ks[12], (N_ATT_LAYERS, Q_DIM, D_MODEL), BETA * Q_DIM ** -0.5)
    moe_w_group = nrm(ks[13], (DEPTH, D_MODEL, N_GROUPS), D_MODEL ** -0.5)
    moe_b_group = nrm(ks[14], (DEPTH, N_GROUPS), 0.01)
    moe_w_expert = nrm(ks[15], (DEPTH, D_MODEL, N_EXPERTS), D_MODEL ** -0.5)
    moe_b_expert = nrm(ks[16], (DEPTH, N_EXPERTS), 0.01)
    moe_w1 = nrm(ks[17], (DEPTH, N_EXPERTS, D_MODEL, D_EXPERT), D_MODEL ** -0.5)
    moe_w3 = nrm(ks[18], (DEPTH, N_EXPERTS, D_MODEL, D_EXPERT), D_MODEL ** -0.5)
    moe_w2 = nrm(ks[19], (DEPTH, N_EXPERTS, D_EXPERT, D_MODEL), BETA * D_EXPERT ** -0.5)
    ln_g = 1.0 + nrm(ks[20], (DEPTH, 2, D_MODEL), 0.02)
    ln_b = nrm(ks[21], (DEPTH, 2, D_MODEL), 0.02)
    return {'x': x, 'rec_w_in': rec_w_in, 'rec_conv_w': rec_conv_w, 'rec_conv_b': rec_conv_b,
            'rec_w_r': rec_w_r, 'rec_b_r': rec_b_r, 'rec_w_i': rec_w_i, 'rec_b_i': rec_b_i,
            'rec_lambda': rec_lambda, 'rec_w_out': rec_w_out, 'att_w_qkv': att_w_qkv,
            'att_sinks': att_sinks, 'att_w_o': att_w_o, 'moe_w_group': moe_w_group,
            'moe_b_group': moe_b_group, 'moe_w_expert': moe_w_expert, 'moe_b_expert': moe_b_expert,
            'moe_w1': moe_w1, 'moe_w3': moe_w3, 'moe_w2': moe_w2, 'ln_g': ln_g, 'ln_b': ln_b}


def reference(x, rec_w_in, rec_conv_w, rec_conv_b, rec_w_r, rec_b_r, rec_w_i, rec_b_i, rec_lambda,
              rec_w_out, att_w_qkv, att_sinks, att_w_o, moe_w_group, moe_b_group, moe_w_expert,
              moe_b_expert, moe_w1, moe_w3, moe_w2, ln_g, ln_b):
    for layer in range(DEPTH):
        j = layer // N_MIXERS
        if layer % N_MIXERS == 0:
            h = rglru_block(x, rec_w_in[j], rec_conv_w[j], rec_conv_b[j], rec_w_r[j], rec_b_r[j],
                            rec_w_i[j], rec_b_i[j], rec_lambda[j], rec_w_out[j])
        else:
            h = swa_sink_alibi(x, att_w_qkv[j], att_sinks[j], att_w_o[j])
        x = layer_norm(ALPHA * x + h, ln_g[layer, 0], ln_b[layer, 0])
        f = hier_moe(x, moe_w_group[layer], moe_b_group[layer], moe_w_expert[layer], moe_b_expert[layer],
                     moe_w1[layer], moe_w3[layer], moe_w2[layer])
        x = layer_norm(ALPHA * x + f, ln_g[layer, 1], ln_b[layer, 1])
    return x
```

```python
import functools

import jax
import jax.numpy as jnp
import numpy as np
from jax import lax
from jax.experimental import pallas as pl
from jax.experimental.pallas import tpu as pltpu

F32 = jnp.float32
BF16 = jnp.bfloat16

D_MODEL = 1024
DEPTH = 2
D_RNN = 1280
LRU_BLOCKS = 16
LRU_BLOCK_W = D_RNN // LRU_BLOCKS
CONV_W = 4
LRU_C = 8.0
N_HEADS = 16
N_KV_HEADS = 4
HEAD_DIM = 64
WINDOW = 128
Q_DIM = N_HEADS * HEAD_DIM
KV_DIM = N_KV_HEADS * HEAD_DIM
N_GROUPS = 4
EXPERTS_PER_GROUP = 8
N_EXPERTS = N_GROUPS * EXPERTS_PER_GROUP
TOP_K = 2
D_EXPERT = 512
ALPHA = (2 * DEPTH) ** 0.25
LN_EPS = 1e-5

LANES = 128
SUBLANES = 8
VMEM_LIMIT = 56 * 1024 * 1024

REC_TS = 256
GATE_TILE = 256
GATE_WIN = 512
GATE_WIN_STARTS = (0, 128, 384, 640, 768)
N_GATE_TILES = D_RNN // GATE_TILE

ROUTE_T = 512
ROUTE_ROWS = 40

MOE_BM = 256

ATT_TQ = 512
ATT_NB = ATT_TQ // WINDOW

COMB_T = 512


def _layer_norm_rows(z, g, b):
    mu = jnp.mean(z, axis=-1, keepdims=True)
    zc = z - mu
    var = jnp.mean(zc * zc, axis=-1, keepdims=True)
    return zc * lax.rsqrt(var + LN_EPS) * g + b


def _rglru_kernel(x_ref, w_in_ref, convw_ref, convb_ref, wg_ref, br_ref, bi_ref, lam_ref, w_out_ref,
                  g_ref, b_ref, o_ref, xr_ext, a_sc, u_sc, h_carry):
    s = pl.program_id(1)
    ts = REC_TS

    @pl.when(s == 0)
    def _():
        xr_ext[0:SUBLANES, :] = jnp.zeros((SUBLANES, D_RNN), F32)
        h_carry[...] = jnp.zeros((1, D_RNN), F32)

    x = x_ref[0]
    proj = jnp.dot(x.astype(BF16), w_in_ref[...], preferred_element_type=F32)
    gate = proj[:, :D_RNN]
    xr = proj[:, D_RNN:]
    xr_ext[SUBLANES:SUBLANES + ts, :] = xr
    xc = convb_ref[...] + convw_ref[3:4, :] * xr
    for k in range(CONV_W - 1):
        shift = CONV_W - 1 - k
        xc = xc + convw_ref[k:k + 1, :] * xr_ext[SUBLANES - shift:SUBLANES - shift + ts, :]
    xr_ext[0:SUBLANES, :] = xr_ext[ts:ts + SUBLANES, :]

    xcb = xc.astype(BF16)
    nlam = -lam_ref[...]
    sp = jnp.maximum(nlam, 0.0) + jnp.log1p(jnp.exp(-jnp.abs(nlam)))
    for j in range(N_GATE_TILES):
        ws = GATE_WIN_STARTS[j]
        cs = j * GATE_TILE
        pre = jnp.dot(xcb[:, ws:ws + GATE_WIN], wg_ref[j], preferred_element_type=F32)
        r = jax.nn.sigmoid(pre[:, :GATE_TILE] + br_ref[:, cs:cs + GATE_TILE])
        i = jax.nn.sigmoid(pre[:, GATE_TILE:] + bi_ref[:, cs:cs + GATE_TILE])
        log_a = (-LRU_C) * r * sp[:, cs:cs + GATE_TILE]
        a = jnp.exp(log_a)
        u = jnp.sqrt(1.0 - a * a) * (i * xc[:, cs:cs + GATE_TILE])
        a_sc[:, cs:cs + GATE_TILE] = a
        u_sc[:, cs:cs + GATE_TILE] = u

    row = lax.broadcasted_iota(jnp.int32, (SUBLANES, D_RNN), 0)

    def scan_body(gidx, hprev):
        r0 = pl.multiple_of(gidx * SUBLANES, SUBLANES)
        a8 = a_sc[pl.ds(r0, SUBLANES), :]
        u8 = u_sc[pl.ds(r0, SUBLANES), :]
        for d in (1, 2, 4):
            keep = row >= d
            a_sh = jnp.where(keep, pltpu.roll(a8, d, axis=0), 1.0)
            u_sh = jnp.where(keep, pltpu.roll(u8, d, axis=0), 0.0)
            u8 = a8 * u_sh + u8
            a8 = a8 * a_sh
        h8 = u8 + a8 * hprev
        u_sc[pl.ds(r0, SUBLANES), :] = h8
        return h8[SUBLANES - 1:SUBLANES, :]

    h_last = lax.fori_loop(0, ts // SUBLANES, scan_body, h_carry[...])
    h_carry[...] = h_last

    y = u_sc[...] * jax.nn.gelu(gate)
    out = jnp.dot(y.astype(BF16), w_out_ref[...], preferred_element_type=F32)
    z = ALPHA * x + out
    o_ref[0] = _layer_norm_rows(z, g_ref[...], b_ref[...])


def _band_gate_weights(w_r, w_i):
    eye = jnp.eye(LRU_BLOCKS, dtype=F32)

    def dense(w):
        return (w[:, :, None, :] * eye[:, None, :, None]).reshape(D_RNN, D_RNN)

    wr, wi = dense(w_r), dense(w_i)
    tiles = []
    for j in range(N_GATE_TILES):
        ws = GATE_WIN_STARTS[j]
        cs = j * GATE_TILE
        lo_blk = cs // LRU_BLOCK_W
        hi_blk = (cs + GATE_TILE - 1) // LRU_BLOCK_W
        assert ws <= lo_blk * LRU_BLOCK_W and (hi_blk + 1) * LRU_BLOCK_W <= ws + GATE_WIN
        tiles.append(jnp.concatenate([wr[ws:ws + GATE_WIN, cs:cs + GATE_TILE],
                                      wi[ws:ws + GATE_WIN, cs:cs + GATE_TILE]], axis=1))
    return jnp.stack(tiles).astype(BF16)


def _rglru_layer(x, w_in, conv_w, conv_b, w_r, b_r, w_i, b_i, lam, w_out, ln_g, ln_b):
    B, S, D = x.shape
    wg = _band_gate_weights(w_r, w_i)
    row = lambda v: v.reshape(1, -1)
    const = lambda shape: pl.BlockSpec(shape, lambda b, s: (0,) * len(shape))
    return pl.pallas_call(
        _rglru_kernel,
        grid=(B, S // REC_TS),
        in_specs=[
            pl.BlockSpec((1, REC_TS, D), lambda b, s: (b, s, 0)),
            const((D, 2 * D_RNN)),
            const((CONV_W, D_RNN)),
            const((1, D_RNN)),
            const((N_GATE_TILES, GATE_WIN, 2 * GATE_TILE)),
            const((1, D_RNN)),
            const((1, D_RNN)),
            const((1, D_RNN)),
            const((D_RNN, D)),
            const((1, D)),
            const((1, D)),
        ],
        out_specs=pl.BlockSpec((1, REC_TS, D), lambda b, s: (b, s, 0)),
        out_shape=jax.ShapeDtypeStruct((B, S, D), F32),
        scratch_shapes=[
            pltpu.VMEM((REC_TS + SUBLANES, D_RNN), F32),
            pltpu.VMEM((REC_TS, D_RNN), F32),
            pltpu.VMEM((REC_TS, D_RNN), F32),
            pltpu.VMEM((1, D_RNN), F32),
        ],
        compiler_params=pltpu.CompilerParams(
            dimension_semantics=("arbitrary", "arbitrary"), vmem_limit_bytes=VMEM_LIMIT),
        name="rglru_ln",
    )(x, w_in.astype(BF16), conv_w, row(conv_b), wg, row(b_r), row(b_i), row(lam), w_out.astype(BF16),
      row(ln_g), row(ln_b))


def _router_kernel(x_ref, whi_ref, wlo_ref, bias_ref, tri_ref, idx_ref, gate_ref, cnt_ref, base_sc):
    step = pl.program_id(0)
    tr = ROUTE_T

    @pl.when(step == 0)
    def _():
        base_sc[...] = jnp.zeros((N_EXPERTS, 1), F32)

    x = x_ref[...]
    xhi = x.astype(BF16)
    xlo = (x - xhi.astype(F32)).astype(BF16)
    nt = (((1,), (1,)), ((), ()))
    whi = whi_ref[...]
    logits = (lax.dot_general(whi, xhi, nt, preferred_element_type=F32)
              + lax.dot_general(whi, xlo, nt, preferred_element_type=F32)
              + lax.dot_general(wlo_ref[...], xhi, nt, preferred_element_type=F32))
    logits = logits + bias_ref[...]

    row8 = lax.broadcasted_iota(jnp.int32, (SUBLANES, tr), 0).astype(F32)
    neg_inf = -jnp.inf
    g = jnp.where(row8 < N_GROUPS, logits[N_EXPERTS:N_EXPERTS + SUBLANES, :], neg_inf)
    gmax = jnp.max(g, axis=0, keepdims=True)
    gidx = jnp.min(jnp.where(g == gmax, row8, SUBLANES), axis=0, keepdims=True)
    g_gate = 1.0 / jnp.sum(jnp.exp(g - gmax), axis=0, keepdims=True)

    esel = logits[0:EXPERTS_PER_GROUP, :]
    for grp in range(1, N_GROUPS):
        esel = jnp.where(gidx == grp, logits[grp * EXPERTS_PER_GROUP:(grp + 1) * EXPERTS_PER_GROUP, :], esel)
    v1 = jnp.max(esel, axis=0, keepdims=True)
    i1 = jnp.min(jnp.where(esel == v1, row8, SUBLANES), axis=0, keepdims=True)
    esel2 = jnp.where(row8 == i1, neg_inf, esel)
    v2 = jnp.max(esel2, axis=0, keepdims=True)
    i2 = jnp.min(jnp.where(esel2 == v2, row8, SUBLANES), axis=0, keepdims=True)
    e21 = jnp.exp(v2 - v1)
    inv = 1.0 / (1.0 + e21)
    gate1 = inv * g_gate
    gate2 = e21 * inv * g_gate
    e1 = gidx * EXPERTS_PER_GROUP + i1
    e2 = gidx * EXPERTS_PER_GROUP + i2

    rowe = lax.broadcasted_iota(jnp.int32, (N_EXPERTS, tr), 0).astype(F32)
    hit1 = rowe == e1
    hit2 = rowe == e2
    member = jnp.where(hit1, 1.0, jnp.where(hit2, 1.0, 0.0))
    before = jnp.dot(member.astype(BF16), tri_ref[...], preferred_element_type=F32) + base_sc[...]
    rank1 = jnp.sum(jnp.where(hit1, before, 0.0), axis=0, keepdims=True)
    rank2 = jnp.sum(jnp.where(hit2, before, 0.0), axis=0, keepdims=True)
    base_sc[...] = base_sc[...] + jnp.sum(member, axis=1, keepdims=True)

    zi = jnp.zeros((1, tr), jnp.int32)
    idx_ref[...] = jnp.concatenate(
        [e1.astype(jnp.int32), e2.astype(jnp.int32), rank1.astype(jnp.int32), rank2.astype(jnp.int32),
         zi, zi, zi, zi], axis=0)
    zf = jnp.zeros((1, tr), F32)
    gate_ref[...] = jnp.concatenate([gate1, gate2, zf, zf, zf, zf, zf, zf], axis=0)
    cnt_ref[...] = jnp.broadcast_to(base_sc[...], (N_EXPERTS, LANES)).astype(jnp.int32)


def _router(xf, w_rg, b_rg, w_re, b_re):
    T, D = xf.shape
    pad_rows = ROUTE_ROWS - N_EXPERTS - N_GROUPS
    w = jnp.concatenate([w_re.T, w_rg.T, jnp.zeros((pad_rows, D), F32)], axis=0)
    whi = w.astype(BF16)
    wlo = (w - whi.astype(F32)).astype(BF16)
    bias = jnp.concatenate([b_re, b_rg, jnp.zeros((pad_rows,), F32)]).reshape(ROUTE_ROWS, 1)
    tri = jnp.asarray(np.triu(np.ones((ROUTE_T, ROUTE_T), np.float32), 1), BF16)
    const = lambda shape: pl.BlockSpec(shape, lambda i: (0,) * len(shape))
    return pl.pallas_call(
        _router_kernel,
        grid=(T // ROUTE_T,),
        in_specs=[
            pl.BlockSpec((ROUTE_T, D), lambda i: (i, 0)),
            const((ROUTE_ROWS, D)),
            const((ROUTE_ROWS, D)),
            const((ROUTE_ROWS, 1)),
            const((ROUTE_T, ROUTE_T)),
        ],
        out_specs=[
            pl.BlockSpec((SUBLANES, ROUTE_T), lambda i: (0, i)),
            pl.BlockSpec((SUBLANES, ROUTE_T), lambda i: (0, i)),
            const((N_EXPERTS, LANES)),
        ],
        out_shape=[
            jax.ShapeDtypeStruct((SUBLANES, T), jnp.int32),
            jax.ShapeDtypeStruct((SUBLANES, T), F32),
            jax.ShapeDtypeStruct((N_EXPERTS, LANES), jnp.int32),
        ],
        scratch_shapes=[pltpu.VMEM((N_EXPERTS, 1), F32)],
        compiler_params=pltpu.CompilerParams(
            dimension_semantics=("arbitrary",), vmem_limit_bytes=VMEM_LIMIT),
        name="router",
    )(xf, whi, wlo, bias, tri)


def _moe_kernel(be_ref, nused_ref, x_ref, w1_ref, w3_ref, w2_ref, o_ref):
    i = pl.program_id(0)

    @pl.when(i < nused_ref[0])
    def _():
        xb = x_ref[...]
        h1 = jnp.dot(xb, w1_ref[0], preferred_element_type=F32)
        h3 = jnp.dot(xb, w3_ref[0], preferred_element_type=F32)
        hdn = (jax.nn.silu(h1) * h3).astype(BF16)
        o_ref[...] = jnp.dot(hdn, w2_ref[0], preferred_element_type=F32).astype(o_ref.dtype)

    @pl.when(i >= nused_ref[0])
    def _():
        o_ref[...] = jnp.zeros(o_ref.shape, o_ref.dtype)


def _moe_blocks(xbuf, block_e, n_used, w1, w3, w2):
    n_rows, D = xbuf.shape
    n_blocks = n_rows // MOE_BM
    grid_spec = pltpu.PrefetchScalarGridSpec(
        num_scalar_prefetch=2,
        grid=(n_blocks,),
        in_specs=[
            pl.BlockSpec((MOE_BM, D), lambda i, be, nu: (i, 0)),
            pl.BlockSpec((1, D, D_EXPERT), lambda i, be, nu: (be[i], 0, 0)),
            pl.BlockSpec((1, D, D_EXPERT), lambda i, be, nu: (be[i], 0, 0)),
            pl.BlockSpec((1, D_EXPERT, D), lambda i, be, nu: (be[i], 0, 0)),
        ],
        out_specs=pl.BlockSpec((MOE_BM, D), lambda i, be, nu: (i, 0)),
    )
    return pl.pallas_call(
        _moe_kernel,
        grid_spec=grid_spec,
        out_shape=jax.ShapeDtypeStruct((n_rows, D), BF16),
        compiler_params=pltpu.CompilerParams(
            dimension_semantics=("arbitrary",), vmem_limit_bytes=VMEM_LIMIT),
        name="moe_experts",
    )(block_e, n_used, xbuf, w1, w3, w2)


def _combine_kernel(x_ref, y1_ref, y2_ref, gates_ref, g_ref, b_ref, o_ref):
    x = x_ref[...]
    gates = gates_ref[...]
    f = gates[:, 0:1] * y1_ref[...].astype(F32) + gates[:, 1:2] * y2_ref[...].astype(F32)
    z = ALPHA * x + f
    o_ref[...] = _layer_norm_rows(z, g_ref[...], b_ref[...])


def _combine_ln(xf, y1, y2, gates, ln_g, ln_b):
    T, D = xf.shape
    const = lambda shape: pl.BlockSpec(shape, lambda i: (0,) * len(shape))
    rows = lambda w: pl.BlockSpec((COMB_T, w), lambda i: (i, 0))
    return pl.pallas_call(
        _combine_kernel,
        grid=(T // COMB_T,),
        in_specs=[rows(D), rows(D), rows(D), rows(TOP_K), const((1, D)), const((1, D))],
        out_specs=rows(D),
        out_shape=jax.ShapeDtypeStruct((T, D), F32),
        compiler_params=pltpu.CompilerParams(
            dimension_semantics=("arbitrary",), vmem_limit_bytes=VMEM_LIMIT),
        name="moe_combine_ln",
    )(xf, y1, y2, gates, ln_g.reshape(1, D), ln_b.reshape(1, D))


def _hier_moe_ln(x, w_rg, b_rg, w_re, b_re, w1, w3, w2, ln_g, ln_b):
    B, S, D = x.shape
    T = B * S
    xf = x.reshape(T, D)
    idx, gates, cnt = _router(xf, w_rg, b_rg, w_re, b_re)
    counts = cnt[:, 0]
    padded = ((counts + MOE_BM - 1) // MOE_BM) * MOE_BM
    pends = jnp.cumsum(padded)
    pstarts = pends - padded
    dest1 = pstarts[idx[0]] + idx[2]
    dest2 = pstarts[idx[1]] + idx[3]
    n_blocks = -(-(T * TOP_K + N_EXPERTS * (MOE_BM - 1)) // MOE_BM)
    n_rows = n_blocks * MOE_BM
    block_start = jnp.arange(n_blocks, dtype=jnp.int32) * MOE_BM
    block_e = jnp.minimum(jnp.sum(block_start[:, None] >= pends[None, :], axis=1), N_EXPERTS - 1).astype(jnp.int32)
    n_used = (pends[-1] // MOE_BM).astype(jnp.int32).reshape(1)
    xb = xf.astype(BF16)
    xbuf = jnp.zeros((n_rows, D), BF16).at[jnp.concatenate([dest1, dest2])].set(jnp.concatenate([xb, xb]))
    ybuf = _moe_blocks(xbuf, block_e, n_used, w1.astype(BF16), w3.astype(BF16), w2.astype(BF16))
    y1 = ybuf[dest1]
    y2 = ybuf[dest2]
    out = _combine_ln(xf, y1, y2, gates[:TOP_K].T, ln_g, ln_b)
    return out.reshape(B, S, D)


def _att_head_order():
    order = []
    for p in range(N_HEADS // 2):
        jj, m = divmod(p, 4)
        order += [8 * jj + m, 8 * jj + 4 + m]
    return order


ATT_HEAD_ORDER = _att_head_order()


def _attn_kernel(x_ref, wqkv_ref, bias_ref, sink_ref, wo_ref, g_ref, b_ref, o_ref, kv_ext, o_sc):
    s = pl.program_id(1)
    tq = ATT_TQ

    @pl.when(s == 0)
    def _():
        kv_ext[0:WINDOW, :] = jnp.zeros((WINDOW, 2 * KV_DIM), BF16)

    x = x_ref[0]
    qkv = jnp.dot(x.astype(BF16), wqkv_ref[...], preferred_element_type=F32)
    q = (qkv[:, :Q_DIM] * (HEAD_DIM ** -0.5)).astype(BF16)
    kv_ext[WINDOW:WINDOW + tq, :] = qkv[:, Q_DIM:].astype(BF16)

    lane = lax.broadcasted_iota(jnp.int32, (WINDOW, LANES), 1)
    low = lane < HEAD_DIM
    first = jnp.where(s == 0, 1, 0)
    nt = (((1,), (1,)), ((), ()))
    zero = jnp.zeros((), BF16)

    for n in range(ATT_NB):
        r0 = n * WINDOW
        for j in range(2):
            k_tile = kv_ext[r0:r0 + 2 * WINDOW, j * LANES:(j + 1) * LANES]
            v_tile = kv_ext[r0:r0 + 2 * WINDOW, KV_DIM + j * LANES:KV_DIM + (j + 1) * LANES]
            parts = []
            for m in range(4):
                p = 4 * j + m
                qt = q[r0:r0 + WINDOW, p * LANES:(p + 1) * LANES]
                parts.append(jnp.where(low, qt, zero))
                parts.append(jnp.where(low, zero, qt))
            qs = jnp.concatenate(parts, axis=0)
            sc = lax.dot_general(qs, k_tile, nt, preferred_element_type=F32)
            sc = sc.reshape(8, WINDOW, 2 * WINDOW)
            hb = bias_ref[first if n == 0 else 0, 8 * j:8 * j + 8]
            sc = sc + hb
            sink = sink_ref[8 * j:8 * j + 8]
            mx = jnp.maximum(jnp.max(sc, axis=-1, keepdims=True), sink)
            pr = jnp.exp(sc - mx)
            denom = jnp.sum(pr, axis=-1, keepdims=True) + jnp.exp(sink - mx)
            pn = (pr / denom).astype(BF16).reshape(8 * WINDOW, 2 * WINDOW)
            ov = jnp.dot(pn, v_tile, preferred_element_type=F32)
            for m in range(4):
                p = 4 * j + m
                o_even = ov[(2 * m) * WINDOW:(2 * m + 1) * WINDOW, :]
                o_odd = ov[(2 * m + 1) * WINDOW:(2 * m + 2) * WINDOW, :]
                o_sc[r0:r0 + WINDOW, p * LANES:(p + 1) * LANES] = jnp.where(low, o_even, o_odd).astype(BF16)

    kv_ext[0:WINDOW, :] = kv_ext[tq:tq + WINDOW, :]
    out = jnp.dot(o_sc[...], wo_ref[...], preferred_element_type=F32)
    z = ALPHA * x + out
    o_ref[0] = _layer_norm_rows(z, g_ref[...], b_ref[...])


def _attn_bias():
    qi = np.arange(WINDOW)[:, None]
    sj = np.arange(2 * WINDOW)[None, :]
    dist = qi - sj + WINDOW
    valid = (dist >= 0) & (dist < WINDOW)
    slopes = 2.0 ** (-8.0 * np.arange(1, N_HEADS + 1, dtype=np.float32) / N_HEADS)
    slopes = slopes.astype(np.float32)[ATT_HEAD_ORDER]
    sb = -(slopes[:, None, None] * dist.astype(np.float32)[None])
    later = np.where(valid[None], sb, -np.inf)
    first = np.where((valid & (sj >= WINDOW))[None], sb, -np.inf)
    return jnp.asarray(np.stack([later, first]).astype(np.float32))


def _attn_layer(x, w_qkv, sinks, w_o, ln_g, ln_b):
    B, S, D = x.shape
    cols = np.concatenate([np.arange(h * HEAD_DIM, (h + 1) * HEAD_DIM) for h in ATT_HEAD_ORDER])
    wqkv = jnp.concatenate([w_qkv[:, cols], w_qkv[:, Q_DIM:]], axis=1).astype(BF16)
    wo = w_o[cols, :].astype(BF16)
    sink = sinks[np.asarray(ATT_HEAD_ORDER)].reshape(N_HEADS, 1, 1)
    bias = _attn_bias()
    const = lambda shape: pl.BlockSpec(shape, lambda b, s: (0,) * len(shape))
    return pl.pallas_call(
        _attn_kernel,
        grid=(B, S // ATT_TQ),
        in_specs=[
            pl.BlockSpec((1, ATT_TQ, D), lambda b, s: (b, s, 0)),
            const((D, Q_DIM + 2 * KV_DIM)),
            const((2, N_HEADS, WINDOW, 2 * WINDOW)),
            const((N_HEADS, 1, 1)),
            const((Q_DIM, D)),
            const((1, D)),
            const((1, D)),
        ],
        out_specs=pl.BlockSpec((1, ATT_TQ, D), lambda b, s: (b, s, 0)),
        out_shape=jax.ShapeDtypeStruct((B, S, D), F32),
        scratch_shapes=[
            pltpu.VMEM((ATT_TQ + WINDOW, 2 * KV_DIM), BF16),
            pltpu.VMEM((ATT_TQ, Q_DIM), BF16),
        ],
        compiler_params=pltpu.CompilerParams(
            dimension_semantics=("arbitrary", "arbitrary"), vmem_limit_bytes=VMEM_LIMIT),
        name="swa_attn_ln",
    )(x, wqkv, bias, sink, wo, ln_g.reshape(1, D), ln_b.reshape(1, D))


def kernel(x, rec_w_in, rec_conv_w, rec_conv_b, rec_w_r, rec_b_r, rec_w_i, rec_b_i, rec_lambda, rec_w_out,
           att_w_qkv, att_sinks, att_w_o, moe_w_group, moe_b_group, moe_w_expert, moe_b_expert,
           moe_w1, moe_w3, moe_w2, ln_g, ln_b):
    for layer in range(DEPTH):
        j = layer // 2
        if layer % 2 == 0:
            x = _rglru_layer(x, rec_w_in[j], rec_conv_w[j], rec_conv_b[j], rec_w_r[j], rec_b_r[j], rec_w_i[j],
                             rec_b_i[j], rec_lambda[j], rec_w_out[j], ln_g[layer, 0], ln_b[layer, 0])
        else:
            x = _attn_layer(x, att_w_qkv[j], att_sinks[j], att_w_o[j], ln_g[layer, 0], ln_b[layer, 0])
        x = _hier_moe_ln(x, moe_w_group[layer], moe_b_group[layer], moe_w_expert[layer], moe_b_expert[layer],
                         moe_w1[layer], moe_w3[layer], moe_w2[layer], ln_g[layer, 1], ln_b[layer, 1])
    return x
```

```python
import functools

import jax
import jax.numpy as jnp
import numpy as np
from jax import lax
from jax.experimental import pallas as pl
from jax.experimental.pallas import tpu as pltpu
from jax.experimental.pallas import tpu_sc as plsc

F32 = jnp.float32
BF16 = jnp.bfloat16
U32 = jnp.uint32

D_MODEL = 1024
DEPTH = 2
D_RNN = 1280
LRU_BLOCKS = 16
LRU_BLOCK_W = D_RNN // LRU_BLOCKS
CONV_W = 4
LRU_C = 8.0
N_HEADS = 16
N_KV_HEADS = 4
HEAD_DIM = 64
WINDOW = 128
Q_DIM = N_HEADS * HEAD_DIM
KV_DIM = N_KV_HEADS * HEAD_DIM
N_GROUPS = 4
EXPERTS_PER_GROUP = 8
N_EXPERTS = N_GROUPS * EXPERTS_PER_GROUP
TOP_K = 2
D_EXPERT = 512
ALPHA = (2 * DEPTH) ** 0.25
LN_EPS = 1e-5

LANES = 128
SUBLANES = 8
VMEM_LIMIT = 56 * 1024 * 1024

REC_TS = 256
GATE_TILE = 256
GATE_WIN = 512
GATE_WIN_STARTS = (0, 128, 384, 640, 768)
N_GATE_TILES = D_RNN // GATE_TILE

ROUTE_T = 512
ROUTE_ROWS = 40

MOE_BM = 256

ATT_TQ = 512
ATT_NB = ATT_TQ // WINDOW

COMB_T = 512

D_HALF = D_MODEL // 2

SC_CORES = 2
SC_SUBCORES = 16
SC_WORKERS = SC_CORES * SC_SUBCORES
SC_WIN = 64


def _layer_norm_rows(z, g, b):
    mu = jnp.mean(z, axis=-1, keepdims=True)
    zc = z - mu
    var = jnp.mean(zc * zc, axis=-1, keepdims=True)
    return zc * lax.rsqrt(var + LN_EPS) * g + b


def _pack_bf16_pair(a, b):
    ua = lax.bitcast_convert_type(a.astype(BF16).astype(F32), U32)
    ub = lax.bitcast_convert_type(b.astype(BF16).astype(F32), U32)
    return (ua >> 16) | (ub & jnp.uint32(0xFFFF0000))


def _unpack_bf16_pair(w):
    a = lax.bitcast_convert_type(w << 16, F32)
    b = lax.bitcast_convert_type(w & jnp.uint32(0xFFFF0000), F32)
    return a, b


def _pack_row_halves(x):
    return _pack_bf16_pair(x[:, :D_HALF], x[:, D_HALF:])


def _rglru_kernel(x_ref, w_in_ref, convw_ref, convb_ref, wg_ref, br_ref, bi_ref, lam_ref, w_out_ref,
                  g_ref, b_ref, o_ref, opk_ref, xr_ext, a_sc, u_sc, h_carry):
    s = pl.program_id(1)
    ts = REC_TS

    @pl.when(s == 0)
    def _():
        xr_ext[0:SUBLANES, :] = jnp.zeros((SUBLANES, D_RNN), F32)
        h_carry[...] = jnp.zeros((1, D_RNN), F32)

    x = x_ref[0]
    proj = jnp.dot(x.astype(BF16), w_in_ref[...], preferred_element_type=F32)
    gate = proj[:, :D_RNN]
    xr = proj[:, D_RNN:]
    xr_ext[SUBLANES:SUBLANES + ts, :] = xr
    xc = convb_ref[...] + convw_ref[3:4, :] * xr
    for k in range(CONV_W - 1):
        shift = CONV_W - 1 - k
        xc = xc + convw_ref[k:k + 1, :] * xr_ext[SUBLANES - shift:SUBLANES - shift + ts, :]
    xr_ext[0:SUBLANES, :] = xr_ext[ts:ts + SUBLANES, :]

    xcb = xc.astype(BF16)
    nlam = -lam_ref[...]
    sp = jnp.maximum(nlam, 0.0) + jnp.log1p(jnp.exp(-jnp.abs(nlam)))
    for j in range(N_GATE_TILES):
        ws = GATE_WIN_STARTS[j]
        cs = j * GATE_TILE
        pre = jnp.dot(xcb[:, ws:ws + GATE_WIN], wg_ref[j], preferred_element_type=F32)
        r = jax.nn.sigmoid(pre[:, :GATE_TILE] + br_ref[:, cs:cs + GATE_TILE])
        i = jax.nn.sigmoid(pre[:, GATE_TILE:] + bi_ref[:, cs:cs + GATE_TILE])
        log_a = (-LRU_C) * r * sp[:, cs:cs + GATE_TILE]
        a = jnp.exp(log_a)
        u = jnp.sqrt(1.0 - a * a) * (i * xc[:, cs:cs + GATE_TILE])
        a_sc[:, cs:cs + GATE_TILE] = a
        u_sc[:, cs:cs + GATE_TILE] = u

    row = lax.broadcasted_iota(jnp.int32, (SUBLANES, D_RNN), 0)

    def scan_body(gidx, hprev):
        r0 = pl.multiple_of(gidx * SUBLANES, SUBLANES)
        a8 = a_sc[pl.ds(r0, SUBLANES), :]
        u8 = u_sc[pl.ds(r0, SUBLANES), :]
        for d in (1, 2, 4):
            keep = row >= d
            a_sh = jnp.where(keep, pltpu.roll(a8, d, axis=0), 1.0)
            u_sh = jnp.where(keep, pltpu.roll(u8, d, axis=0), 0.0)
            u8 = a8 * u_sh + u8
            a8 = a8 * a_sh
        h8 = u8 + a8 * hprev
        u_sc[pl.ds(r0, SUBLANES), :] = h8
        return h8[SUBLANES - 1:SUBLANES, :]

    h_last = lax.fori_loop(0, ts // SUBLANES, scan_body, h_carry[...])
    h_carry[...] = h_last

    y = u_sc[...] * jax.nn.gelu(gate)
    out = jnp.dot(y.astype(BF16), w_out_ref[...], preferred_element_type=F32)
    z = ALPHA * x + out
    xn = _layer_norm_rows(z, g_ref[...], b_ref[...])
    o_ref[0] = xn
    opk_ref[0] = _pack_row_halves(xn)


def _band_gate_weights(w_r, w_i):
    eye = jnp.eye(LRU_BLOCKS, dtype=F32)

    def dense(w):
        return (w[:, :, None, :] * eye[:, None, :, None]).reshape(D_RNN, D_RNN)

    wr, wi = dense(w_r), dense(w_i)
    tiles = []
    for j in range(N_GATE_TILES):
        ws = GATE_WIN_STARTS[j]
        cs = j * GATE_TILE
        lo_blk = cs // LRU_BLOCK_W
        hi_blk = (cs + GATE_TILE - 1) // LRU_BLOCK_W
        assert ws <= lo_blk * LRU_BLOCK_W and (hi_blk + 1) * LRU_BLOCK_W <= ws + GATE_WIN
        tiles.append(jnp.concatenate([wr[ws:ws + GATE_WIN, cs:cs + GATE_TILE],
                                      wi[ws:ws + GATE_WIN, cs:cs + GATE_TILE]], axis=1))
    return jnp.stack(tiles).astype(BF16)


def _rglru_layer(x, w_in, conv_w, conv_b, w_r, b_r, w_i, b_i, lam, w_out, ln_g, ln_b):
    B, S, D = x.shape
    wg = _band_gate_weights(w_r, w_i)
    row = lambda v: v.reshape(1, -1)
    const = lambda shape: pl.BlockSpec(shape, lambda b, s: (0,) * len(shape))
    tile = lambda w: pl.BlockSpec((1, REC_TS, w), lambda b, s: (b, s, 0))
    return pl.pallas_call(
        _rglru_kernel,
        grid=(B, S // REC_TS),
        in_specs=[
            tile(D),
            const((D, 2 * D_RNN)),
            const((CONV_W, D_RNN)),
            const((1, D_RNN)),
            const((N_GATE_TILES, GATE_WIN, 2 * GATE_TILE)),
            const((1, D_RNN)),
            const((1, D_RNN)),
            const((1, D_RNN)),
            const((D_RNN, D)),
            const((1, D)),
            const((1, D)),
        ],
        out_specs=[tile(D), tile(D_HALF)],
        out_shape=[jax.ShapeDtypeStruct((B, S, D), F32), jax.ShapeDtypeStruct((B, S, D_HALF), U32)],
        scratch_shapes=[
            pltpu.VMEM((REC_TS + SUBLANES, D_RNN), F32),
            pltpu.VMEM((REC_TS, D_RNN), F32),
            pltpu.VMEM((REC_TS, D_RNN), F32),
            pltpu.VMEM((1, D_RNN), F32),
        ],
        compiler_params=pltpu.CompilerParams(
            dimension_semantics=("arbitrary", "arbitrary"), vmem_limit_bytes=VMEM_LIMIT),
        name="rglru_ln",
    )(x, w_in.astype(BF16), conv_w, row(conv_b), wg, row(b_r), row(b_i), row(lam), w_out.astype(BF16),
      row(ln_g), row(ln_b))


def _router_kernel(x_ref, whi_ref, wlo_ref, bias_ref, tri_ref, idx_ref, gate_ref, cnt_ref, base_sc):
    step = pl.program_id(0)
    tr = ROUTE_T

    @pl.when(step == 0)
    def _():
        base_sc[...] = jnp.zeros((N_EXPERTS, 1), F32)

    x = x_ref[...]
    xhi = x.astype(BF16)
    xlo = (x - xhi.astype(F32)).astype(BF16)
    nt = (((1,), (1,)), ((), ()))
    whi = whi_ref[...]
    logits = (lax.dot_general(whi, xhi, nt, preferred_element_type=F32)
              + lax.dot_general(whi, xlo, nt, preferred_element_type=F32)
              + lax.dot_general(wlo_ref[...], xhi, nt, preferred_element_type=F32))
    logits = logits + bias_ref[...]

    row8 = lax.broadcasted_iota(jnp.int32, (SUBLANES, tr), 0).astype(F32)
    neg_inf = -jnp.inf
    g = jnp.where(row8 < N_GROUPS, logits[N_EXPERTS:N_EXPERTS + SUBLANES, :], neg_inf)
    gmax = jnp.max(g, axis=0, keepdims=True)
    gidx = jnp.min(jnp.where(g == gmax, row8, SUBLANES), axis=0, keepdims=True)
    g_gate = 1.0 / jnp.sum(jnp.exp(g - gmax), axis=0, keepdims=True)

    esel = logits[0:EXPERTS_PER_GROUP, :]
    for grp in range(1, N_GROUPS):
        esel = jnp.where(gidx == grp, logits[grp * EXPERTS_PER_GROUP:(grp + 1) * EXPERTS_PER_GROUP, :], esel)
    v1 = jnp.max(esel, axis=0, keepdims=True)
    i1 = jnp.min(jnp.where(esel == v1, row8, SUBLANES), axis=0, keepdims=True)
    esel2 = jnp.where(row8 == i1, neg_inf, esel)
    v2 = jnp.max(esel2, axis=0, keepdims=True)
    i2 = jnp.min(jnp.where(esel2 == v2, row8, SUBLANES), axis=0, keepdims=True)
    e21 = jnp.exp(v2 - v1)
    inv = 1.0 / (1.0 + e21)
    gate1 = inv * g_gate
    gate2 = e21 * inv * g_gate
    e1 = gidx * EXPERTS_PER_GROUP + i1
    e2 = gidx * EXPERTS_PER_GROUP + i2

    rowe = lax.broadcasted_iota(jnp.int32, (N_EXPERTS, tr), 0).astype(F32)
    hit1 = rowe == e1
    hit2 = rowe == e2
    member = jnp.where(hit1, 1.0, jnp.where(hit2, 1.0, 0.0))
    before = jnp.dot(member.astype(BF16), tri_ref[...], preferred_element_type=F32) + base_sc[...]
    rank1 = jnp.sum(jnp.where(hit1, before, 0.0), axis=0, keepdims=True)
    rank2 = jnp.sum(jnp.where(hit2, before, 0.0), axis=0, keepdims=True)
    base_sc[...] = base_sc[...] + jnp.sum(member, axis=1, keepdims=True)

    zi = jnp.zeros((1, tr), jnp.int32)
    idx_ref[...] = jnp.concatenate(
        [e1.astype(jnp.int32), e2.astype(jnp.int32), rank1.astype(jnp.int32), rank2.astype(jnp.int32),
         zi, zi, zi, zi], axis=0)
    zf = jnp.zeros((1, tr), F32)
    gate_ref[...] = jnp.concatenate([gate1, gate2, zf, zf, zf, zf, zf, zf], axis=0)
    cnt_ref[...] = jnp.broadcast_to(base_sc[...], (N_EXPERTS, LANES)).astype(jnp.int32)


def _router(xf, w_rg, b_rg, w_re, b_re):
    T, D = xf.shape
    pad_rows = ROUTE_ROWS - N_EXPERTS - N_GROUPS
    w = jnp.concatenate([w_re.T, w_rg.T, jnp.zeros((pad_rows, D), F32)], axis=0)
    whi = w.astype(BF16)
    wlo = (w - whi.astype(F32)).astype(BF16)
    bias = jnp.concatenate([b_re, b_rg, jnp.zeros((pad_rows,), F32)]).reshape(ROUTE_ROWS, 1)
    tri = jnp.asarray(np.triu(np.ones((ROUTE_T, ROUTE_T), np.float32), 1), BF16)
    const = lambda shape: pl.BlockSpec(shape, lambda i: (0,) * len(shape))
    return pl.pallas_call(
        _router_kernel,
        grid=(T // ROUTE_T,),
        in_specs=[
            pl.BlockSpec((ROUTE_T, D), lambda i: (i, 0)),
            const((ROUTE_ROWS, D)),
            const((ROUTE_ROWS, D)),
            const((ROUTE_ROWS, 1)),
            const((ROUTE_T, ROUTE_T)),
        ],
        out_specs=[
            pl.BlockSpec((SUBLANES, ROUTE_T), lambda i: (0, i)),
            pl.BlockSpec((SUBLANES, ROUTE_T), lambda i: (0, i)),
            const((N_EXPERTS, LANES)),
        ],
        out_shape=[
            jax.ShapeDtypeStruct((SUBLANES, T), jnp.int32),
            jax.ShapeDtypeStruct((SUBLANES, T), F32),
            jax.ShapeDtypeStruct((N_EXPERTS, LANES), jnp.int32),
        ],
        scratch_shapes=[pltpu.VMEM((N_EXPERTS, 1), F32)],
        compiler_params=pltpu.CompilerParams(
            dimension_semantics=("arbitrary",), vmem_limit_bytes=VMEM_LIMIT),
        name="router",
    )(xf, whi, wlo, bias, tri)


def _sc_mesh():
    return plsc.VectorSubcoreMesh(core_axis_name="c", subcore_axis_name="s",
                                  num_cores=SC_CORES, num_subcores=SC_SUBCORES)


def _sc_worker_id():
    return lax.axis_index("s") * SC_CORES + lax.axis_index("c")


def _sc_scratch(n_win, width):
    return [
        pltpu.VMEM((n_win, SC_WIN), jnp.int32),
        pltpu.VMEM((n_win, SC_WIN), jnp.int32),
        pltpu.VMEM((2, SC_WIN, width), U32),
        pltpu.SemaphoreType.DMA((2,)),
        pltpu.SemaphoreType.DMA((2,)),
    ]


def _sc_dispatch(rows, idx1, idx2, n_rows):
    _, width = rows.shape
    _, n_win, _ = idx1.shape

    @functools.partial(
        pl.kernel, mesh=_sc_mesh(), out_type=jax.ShapeDtypeStruct((n_rows, width), rows.dtype),
        scratch_types=_sc_scratch(n_win, width), name="sc_dispatch")
    def run(rows_hbm, i1_hbm, i2_hbm, o_hbm, i1_v, i2_v, buf, rsem, wsem):
        wid = _sc_worker_id()
        base = wid * (n_win * SC_WIN)
        pltpu.sync_copy(i1_hbm.at[wid], i1_v)
        pltpu.sync_copy(i2_hbm.at[wid], i2_v)

        def read(j):
            return pltpu.async_copy(rows_hbm.at[pl.ds(base + j * SC_WIN, SC_WIN)], buf.at[j % 2], rsem.at[j % 2])

        reads = {0: read(0)}
        writes = {}
        for j in range(n_win):
            if j + 1 < n_win:
                for d in writes.pop(j - 1, ()):
                    d.wait()
                reads[j + 1] = read(j + 1)
            reads.pop(j).wait()
            writes[j] = (pltpu.async_copy(buf.at[j % 2], o_hbm.at[i1_v.at[j]], wsem.at[j % 2]),
                         pltpu.async_copy(buf.at[j % 2], o_hbm.at[i2_v.at[j]], wsem.at[j % 2]))
        for j in sorted(writes):
            for d in writes[j]:
                d.wait()

    return run(rows, idx1, idx2)


def _sc_gather_pair(table, idx1, idx2):
    _, width = table.shape
    _, n_win, _ = idx1.shape
    n_tok = SC_WORKERS * n_win * SC_WIN
    out_t = jax.ShapeDtypeStruct((n_tok, width), table.dtype)

    @functools.partial(
        pl.kernel, mesh=_sc_mesh(), out_type=(out_t, out_t),
        scratch_types=_sc_scratch(n_win, width), name="sc_combine_gather")
    def run(table_hbm, i1_hbm, i2_hbm, o1_hbm, o2_hbm, i1_v, i2_v, buf, gsem, wsem):
        wid = _sc_worker_id()
        base = wid * (n_win * SC_WIN)
        pltpu.sync_copy(i1_hbm.at[wid], i1_v)
        pltpu.sync_copy(i2_hbm.at[wid], i2_v)
        work = [(i1_v, o1_hbm, j) for j in range(n_win)] + [(i2_v, o2_hbm, j) for j in range(n_win)]

        def gather(t):
            iv, _, j = work[t]
            return pltpu.async_copy(table_hbm.at[iv.at[j]], buf.at[t % 2], gsem.at[t % 2])

        def put(t):
            _, oh, j = work[t]
            return pltpu.async_copy(buf.at[t % 2], oh.at[pl.ds(base + j * SC_WIN, SC_WIN)], wsem.at[t % 2])

        gathers = {0: gather(0)}
        puts = {}
        for t in range(len(work)):
            if t + 1 < len(work):
                if t - 1 in puts:
                    puts.pop(t - 1).wait()
                gathers[t + 1] = gather(t + 1)
            gathers.pop(t).wait()
            puts[t] = put(t)
        for t in sorted(puts):
            puts[t].wait()

    return run(table, idx1, idx2)


def _moe_kernel(be_ref, nused_ref, x_ref, w1_ref, w3_ref, w2_ref, o_ref, w1_sc, w3_sc, w2_sc):
    i = pl.program_id(0)
    new_expert = jnp.logical_or(i == 0, be_ref[i] != be_ref[jnp.maximum(i - 1, 0)])

    @pl.when(new_expert)
    def _():
        w1_sc[...] = w1_ref[0, 0].astype(BF16)
        w3_sc[...] = w3_ref[0, 0].astype(BF16)
        w2_sc[...] = w2_ref[0, 0].astype(BF16)

    @pl.when(i < nused_ref[0])
    def _():
        xa, xb = _unpack_bf16_pair(x_ref[...])
        xa = xa.astype(BF16)
        xb = xb.astype(BF16)
        h1 = (jnp.dot(xa, w1_sc[:D_HALF], preferred_element_type=F32)
              + jnp.dot(xb, w1_sc[D_HALF:], preferred_element_type=F32))
        h3 = (jnp.dot(xa, w3_sc[:D_HALF], preferred_element_type=F32)
              + jnp.dot(xb, w3_sc[D_HALF:], preferred_element_type=F32))
        hdn = (jax.nn.silu(h1) * h3).astype(BF16)
        y = jnp.dot(hdn, w2_sc[...], preferred_element_type=F32)
        o_ref[...] = _pack_row_halves(y)

    @pl.when(i >= nused_ref[0])
    def _():
        o_ref[...] = jnp.zeros(o_ref.shape, o_ref.dtype)


def _moe_blocks(xbuf, block_e, n_used, w1, w3, w2, layer):
    n_rows, _ = xbuf.shape
    D = D_MODEL
    n_blocks = n_rows // MOE_BM
    grid_spec = pltpu.PrefetchScalarGridSpec(
        num_scalar_prefetch=2,
        grid=(n_blocks,),
        in_specs=[
            pl.BlockSpec((MOE_BM, D_HALF), lambda i, be, nu: (i, 0)),
            pl.BlockSpec((1, 1, D, D_EXPERT), lambda i, be, nu: (layer, be[i], 0, 0)),
            pl.BlockSpec((1, 1, D, D_EXPERT), lambda i, be, nu: (layer, be[i], 0, 0)),
            pl.BlockSpec((1, 1, D_EXPERT, D), lambda i, be, nu: (layer, be[i], 0, 0)),
        ],
        out_specs=pl.BlockSpec((MOE_BM, D_HALF), lambda i, be, nu: (i, 0)),
        scratch_shapes=[
            pltpu.VMEM((D, D_EXPERT), BF16),
            pltpu.VMEM((D, D_EXPERT), BF16),
            pltpu.VMEM((D_EXPERT, D), BF16),
        ],
    )
    return pl.pallas_call(
        _moe_kernel,
        grid_spec=grid_spec,
        out_shape=jax.ShapeDtypeStruct((n_rows, D_HALF), U32),
        compiler_params=pltpu.CompilerParams(
            dimension_semantics=("arbitrary",), vmem_limit_bytes=VMEM_LIMIT),
        name="moe_experts",
    )(block_e, n_used, xbuf, w1, w3, w2)


def _combine_kernel(x_ref, y1_ref, y2_ref, gates_ref, g_ref, b_ref, o_ref):
    x = x_ref[...]
    gates = gates_ref[...]
    g1 = gates[:, 0:1]
    g2 = gates[:, 1:2]
    a1, b1 = _unpack_bf16_pair(y1_ref[...])
    a2, b2 = _unpack_bf16_pair(y2_ref[...])
    f = jnp.concatenate([g1 * a1 + g2 * a2, g1 * b1 + g2 * b2], axis=1)
    z = ALPHA * x + f
    o_ref[...] = _layer_norm_rows(z, g_ref[...], b_ref[...])


def _combine_ln(xf, y1, y2, gates, ln_g, ln_b):
    T, D = xf.shape
    const = lambda shape: pl.BlockSpec(shape, lambda i: (0,) * len(shape))
    rows = lambda w: pl.BlockSpec((COMB_T, w), lambda i: (i, 0))
    return pl.pallas_call(
        _combine_kernel,
        grid=(T // COMB_T,),
        in_specs=[rows(D), rows(D_HALF), rows(D_HALF), rows(TOP_K), const((1, D)), const((1, D))],
        out_specs=rows(D),
        out_shape=jax.ShapeDtypeStruct((T, D), F32),
        compiler_params=pltpu.CompilerParams(
            dimension_semantics=("arbitrary",), vmem_limit_bytes=VMEM_LIMIT),
        name="moe_combine_ln",
    )(xf, y1, y2, gates, ln_g.reshape(1, D), ln_b.reshape(1, D))


def _hier_moe_ln(x, xpk, w_rg, b_rg, w_re, b_re, w1, w3, w2, layer, ln_g, ln_b):
    B, S, D = x.shape
    T = B * S
    xf = x.reshape(T, D)
    idx, gates, cnt = _router(xf, w_rg, b_rg, w_re, b_re)
    counts = cnt[:, 0]
    padded = ((counts + MOE_BM - 1) // MOE_BM) * MOE_BM
    pends = jnp.cumsum(padded)
    pstarts = pends - padded
    experts = jnp.arange(N_EXPERTS, dtype=jnp.int32)[:, None]

    def dest(e_row, rank_row):
        return jnp.sum(jnp.where(e_row[None, :] == experts, pstarts[:, None], 0), axis=0) + rank_row

    n_win = T // (SC_WORKERS * SC_WIN)
    dest1 = dest(idx[0], idx[2]).reshape(SC_WORKERS, n_win, SC_WIN)
    dest2 = dest(idx[1], idx[3]).reshape(SC_WORKERS, n_win, SC_WIN)
    n_blocks = -(-(T * TOP_K + N_EXPERTS * (MOE_BM - 1)) // MOE_BM)
    n_rows = n_blocks * MOE_BM
    block_start = jnp.arange(n_blocks, dtype=jnp.int32) * MOE_BM
    block_e = jnp.minimum(jnp.sum(block_start[:, None] >= pends[None, :], axis=1), N_EXPERTS - 1).astype(jnp.int32)
    n_used = (pends[-1] // MOE_BM).astype(jnp.int32).reshape(1)
    xbuf = _sc_dispatch(xpk.reshape(T, D_HALF), dest1, dest2, n_rows)
    ybuf = _moe_blocks(xbuf, block_e, n_used, w1, w3, w2, layer)
    y1, y2 = _sc_gather_pair(ybuf, dest1, dest2)
    out = _combine_ln(xf, y1, y2, gates[:TOP_K].T, ln_g, ln_b)
    return out.reshape(B, S, D)


def _att_head_order():
    order = []
    for p in range(N_HEADS // 2):
        jj, m = divmod(p, 4)
        order += [8 * jj + m, 8 * jj + 4 + m]
    return order


ATT_HEAD_ORDER = _att_head_order()


def _attn_kernel(x_ref, wqkv_ref, bias_ref, sink_ref, wo_ref, g_ref, b_ref, o_ref, opk_ref, kv_ext, o_sc):
    s = pl.program_id(1)
    tq = ATT_TQ

    @pl.when(s == 0)
    def _():
        kv_ext[0:WINDOW, :] = jnp.zeros((WINDOW, 2 * KV_DIM), BF16)

    x = x_ref[0]
    qkv = jnp.dot(x.astype(BF16), wqkv_ref[...], preferred_element_type=F32)
    q = (qkv[:, :Q_DIM] * (HEAD_DIM ** -0.5)).astype(BF16)
    kv_ext[WINDOW:WINDOW + tq, :] = qkv[:, Q_DIM:].astype(BF16)

    lane = lax.broadcasted_iota(jnp.int32, (WINDOW, LANES), 1)
    low = lane < HEAD_DIM
    first = jnp.where(s == 0, 1, 0)
    nt = (((1,), (1,)), ((), ()))
    zero = jnp.zeros((), BF16)

    for n in range(ATT_NB):
        r0 = n * WINDOW
        for j in range(2):
            k_tile = kv_ext[r0:r0 + 2 * WINDOW, j * LANES:(j + 1) * LANES]
            v_tile = kv_ext[r0:r0 + 2 * WINDOW, KV_DIM + j * LANES:KV_DIM + (j + 1) * LANES]
            parts = []
            for m in range(4):
                p = 4 * j + m
                qt = q[r0:r0 + WINDOW, p * LANES:(p + 1) * LANES]
                parts.append(jnp.where(low, qt, zero))
                parts.append(jnp.where(low, zero, qt))
            qs = jnp.concatenate(parts, axis=0)
            sc = lax.dot_general(qs, k_tile, nt, preferred_element_type=F32)
            sc = sc.reshape(8, WINDOW, 2 * WINDOW)
            hb = bias_ref[first if n == 0 else 0, 8 * j:8 * j + 8]
            sc = sc + hb
            sink = sink_ref[8 * j:8 * j + 8]
            mx = jnp.maximum(jnp.max(sc, axis=-1, keepdims=True), sink)
            pr = jnp.exp(sc - mx)
            denom = jnp.sum(pr, axis=-1, keepdims=True) + jnp.exp(sink - mx)
            pn = (pr / denom).astype(BF16).reshape(8 * WINDOW, 2 * WINDOW)
            ov = jnp.dot(pn, v_tile, preferred_element_type=F32)
            for m in range(4):
                p = 4 * j + m
                o_even = ov[(2 * m) * WINDOW:(2 * m + 1) * WINDOW, :]
                o_odd = ov[(2 * m + 1) * WINDOW:(2 * m + 2) * WINDOW, :]
                o_sc[r0:r0 + WINDOW, p * LANES:(p + 1) * LANES] = jnp.where(low, o_even, o_odd).astype(BF16)

    kv_ext[0:WINDOW, :] = kv_ext[tq:tq + WINDOW, :]
    out = jnp.dot(o_sc[...], wo_ref[...], preferred_element_type=F32)
    z = ALPHA * x + out
    xn = _layer_norm_rows(z, g_ref[...], b_ref[...])
    o_ref[0] = xn
    opk_ref[0] = _pack_row_halves(xn)


def _attn_bias():
    qi = np.arange(WINDOW)[:, None]
    sj = np.arange(2 * WINDOW)[None, :]
    dist = qi - sj + WINDOW
    valid = (dist >= 0) & (dist < WINDOW)
    slopes = 2.0 ** (-8.0 * np.arange(1, N_HEADS + 1, dtype=np.float32) / N_HEADS)
    slopes = slopes.astype(np.float32)[ATT_HEAD_ORDER]
    sb = -(slopes[:, None, None] * dist.astype(np.float32)[None])
    later = np.where(valid[None], sb, -np.inf)
    first = np.where((valid & (sj >= WINDOW))[None], sb, -np.inf)
    return jnp.asarray(np.stack([later, first]).astype(np.float32))


def _attn_layer(x, w_qkv, sinks, w_o, ln_g, ln_b):
    B, S, D = x.shape
    cols = np.concatenate([np.arange(h * HEAD_DIM, (h + 1) * HEAD_DIM) for h in ATT_HEAD_ORDER])
    wqkv = jnp.concatenate([w_qkv[:, cols], w_qkv[:, Q_DIM:]], axis=1).astype(BF16)
    wo = w_o[cols, :].astype(BF16)
    sink = sinks[np.asarray(ATT_HEAD_ORDER)].reshape(N_HEADS, 1, 1)
    bias = _attn_bias()
    const = lambda shape: pl.BlockSpec(shape, lambda b, s: (0,) * len(shape))
    tile = lambda w: pl.BlockSpec((1, ATT_TQ, w), lambda b, s: (b, s, 0))
    return pl.pallas_call(
        _attn_kernel,
        grid=(B, S // ATT_TQ),
        in_specs=[
            tile(D),
            const((D, Q_DIM + 2 * KV_DIM)),
            const((2, N_HEADS, WINDOW, 2 * WINDOW)),
            const((N_HEADS, 1, 1)),
            const((Q_DIM, D)),
            const((1, D)),
            const((1, D)),
        ],
        out_specs=[tile(D), tile(D_HALF)],
        out_shape=[jax.ShapeDtypeStruct((B, S, D), F32), jax.ShapeDtypeStruct((B, S, D_HALF), U32)],
        scratch_shapes=[
            pltpu.VMEM((ATT_TQ + WINDOW, 2 * KV_DIM), BF16),
            pltpu.VMEM((ATT_TQ, Q_DIM), BF16),
        ],
        compiler_params=pltpu.CompilerParams(
            dimension_semantics=("arbitrary", "arbitrary"), vmem_limit_bytes=VMEM_LIMIT),
        name="swa_attn_ln",
    )(x, wqkv, bias, sink, wo, ln_g.reshape(1, D), ln_b.reshape(1, D))


def kernel(x, rec_w_in, rec_conv_w, rec_conv_b, rec_w_r, rec_b_r, rec_w_i, rec_b_i, rec_lambda, rec_w_out,
           att_w_qkv, att_sinks, att_w_o, moe_w_group, moe_b_group, moe_w_expert, moe_b_expert,
           moe_w1, moe_w3, moe_w2, ln_g, ln_b):
    for layer in range(DEPTH):
        j = layer // 2
        if layer % 2 == 0:
            x, xpk = _rglru_layer(x, rec_w_in[j], rec_conv_w[j], rec_conv_b[j], rec_w_r[j], rec_b_r[j],
                                  rec_w_i[j], rec_b_i[j], rec_lambda[j], rec_w_out[j],
                                  ln_g[layer, 0], ln_b[layer, 0])
        else:
            x, xpk = _attn_layer(x, att_w_qkv[j], att_sinks[j], att_w_o[j], ln_g[layer, 0], ln_b[layer, 0])
        x = _hier_moe_ln(x, xpk, moe_w_group[layer], moe_b_group[layer], moe_w_expert[layer],
                         moe_b_expert[layer], moe_w1, moe_w3, moe_w2, layer,
                         ln_g[layer, 1], ln_b[layer, 1])
    return x
```

```python
import functools

import jax
import jax.numpy as jnp
import numpy as np
from jax import lax
from jax.experimental import pallas as pl
from jax.experimental.pallas import tpu as pltpu
from jax.experimental.pallas import tpu_sc as plsc

F32 = jnp.float32
BF16 = jnp.bfloat16
U32 = jnp.uint32

D_MODEL = 1024
DEPTH = 2
D_RNN = 1280
LRU_BLOCKS = 16
LRU_BLOCK_W = D_RNN // LRU_BLOCKS
CONV_W = 4
LRU_C = 8.0
N_HEADS = 16
N_KV_HEADS = 4
HEAD_DIM = 64
WINDOW = 128
Q_DIM = N_HEADS * HEAD_DIM
KV_DIM = N_KV_HEADS * HEAD_DIM
N_GROUPS = 4
EXPERTS_PER_GROUP = 8
N_EXPERTS = N_GROUPS * EXPERTS_PER_GROUP
TOP_K = 2
D_EXPERT = 512
ALPHA = (2 * DEPTH) ** 0.25
LN_EPS = 1e-5

LANES = 128
SUBLANES = 8
VMEM_LIMIT = 56 * 1024 * 1024

REC_TS = 256
GATE_TILE = 256
GATE_WIN = 512
GATE_WIN_STARTS = (0, 128, 384, 640, 768)
N_GATE_TILES = D_RNN // GATE_TILE

ROUTE_T = 512
ROUTE_ROWS = 40

MOE_BM = 256

ATT_TQ = 512
ATT_NB = ATT_TQ // WINDOW

COMB_T = 512

D_HALF = D_MODEL // 2

SC_CORES = 2
SC_SUBCORES = 16
SC_WORKERS = SC_CORES * SC_SUBCORES
SC_WIN = 64


def _layer_norm_rows(z, g, b):
    mu = jnp.mean(z, axis=-1, keepdims=True)
    zc = z - mu
    var = jnp.mean(zc * zc, axis=-1, keepdims=True)
    return zc * lax.rsqrt(var + LN_EPS) * g + b


def _pack_bf16_pair(a, b):
    ua = lax.bitcast_convert_type(a.astype(BF16).astype(F32), U32)
    ub = lax.bitcast_convert_type(b.astype(BF16).astype(F32), U32)
    return (ua >> 16) | (ub & jnp.uint32(0xFFFF0000))


def _unpack_bf16_pair(w):
    a = lax.bitcast_convert_type(w << 16, F32)
    b = lax.bitcast_convert_type(w & jnp.uint32(0xFFFF0000), F32)
    return a, b


def _pack_row_halves(x):
    return _pack_bf16_pair(x[:, :D_HALF], x[:, D_HALF:])


def _rglru_kernel(x_ref, w_in_ref, convw_ref, convb_ref, wg_ref, br_ref, bi_ref, lam_ref, w_out_ref,
                  g_ref, b_ref, o_ref, opk_ref, xr_ext, a_sc, u_sc, h_carry):
    s = pl.program_id(1)
    ts = REC_TS

    @pl.when(s == 0)
    def _():
        xr_ext[0:SUBLANES, :] = jnp.zeros((SUBLANES, D_RNN), F32)
        h_carry[...] = jnp.zeros((1, D_RNN), F32)

    x = x_ref[0]
    proj = jnp.dot(x.astype(BF16), w_in_ref[...], preferred_element_type=F32)
    gate = proj[:, :D_RNN]
    xr = proj[:, D_RNN:]
    xr_ext[SUBLANES:SUBLANES + ts, :] = xr
    xc = convb_ref[...] + convw_ref[3:4, :] * xr
    for k in range(CONV_W - 1):
        shift = CONV_W - 1 - k
        xc = xc + convw_ref[k:k + 1, :] * xr_ext[SUBLANES - shift:SUBLANES - shift + ts, :]
    xr_ext[0:SUBLANES, :] = xr_ext[ts:ts + SUBLANES, :]

    xcb = xc.astype(BF16)
    nlam = -lam_ref[...]
    sp = jnp.maximum(nlam, 0.0) + jnp.log1p(jnp.exp(-jnp.abs(nlam)))
    for j in range(N_GATE_TILES):
        ws = GATE_WIN_STARTS[j]
        cs = j * GATE_TILE
        pre = jnp.dot(xcb[:, ws:ws + GATE_WIN], wg_ref[j], preferred_element_type=F32)
        r = jax.nn.sigmoid(pre[:, :GATE_TILE] + br_ref[:, cs:cs + GATE_TILE])
        i = jax.nn.sigmoid(pre[:, GATE_TILE:] + bi_ref[:, cs:cs + GATE_TILE])
        log_a = (-LRU_C) * r * sp[:, cs:cs + GATE_TILE]
        a = jnp.exp(log_a)
        u = jnp.sqrt(1.0 - a * a) * (i * xc[:, cs:cs + GATE_TILE])
        a_sc[:, cs:cs + GATE_TILE] = a
        u_sc[:, cs:cs + GATE_TILE] = u

    row = lax.broadcasted_iota(jnp.int32, (SUBLANES, D_RNN), 0)

    def scan_body(gidx, hprev):
        r0 = pl.multiple_of(gidx * SUBLANES, SUBLANES)
        a8 = a_sc[pl.ds(r0, SUBLANES), :]
        u8 = u_sc[pl.ds(r0, SUBLANES), :]
        for d in (1, 2, 4):
            keep = row >= d
            a_sh = jnp.where(keep, pltpu.roll(a8, d, axis=0), 1.0)
            u_sh = jnp.where(keep, pltpu.roll(u8, d, axis=0), 0.0)
            u8 = a8 * u_sh + u8
            a8 = a8 * a_sh
        h8 = u8 + a8 * hprev
        u_sc[pl.ds(r0, SUBLANES), :] = h8
        return h8[SUBLANES - 1:SUBLANES, :]

    h_last = lax.fori_loop(0, ts // SUBLANES, scan_body, h_carry[...])
    h_carry[...] = h_last

    y = u_sc[...] * jax.nn.gelu(gate)
    out = jnp.dot(y.astype(BF16), w_out_ref[...], preferred_element_type=F32)
    z = ALPHA * x + out
    xn = _layer_norm_rows(z, g_ref[...], b_ref[...])
    o_ref[0] = xn
    opk_ref[0] = _pack_row_halves(xn)


def _band_gate_weights(w_r, w_i):
    eye = jnp.eye(LRU_BLOCKS, dtype=F32)

    def dense(w):
        return (w[:, :, None, :] * eye[:, None, :, None]).reshape(D_RNN, D_RNN)

    wr, wi = dense(w_r), dense(w_i)
    tiles = []
    for j in range(N_GATE_TILES):
        ws = GATE_WIN_STARTS[j]
        cs = j * GATE_TILE
        lo_blk = cs // LRU_BLOCK_W
        hi_blk = (cs + GATE_TILE - 1) // LRU_BLOCK_W
        assert ws <= lo_blk * LRU_BLOCK_W and (hi_blk + 1) * LRU_BLOCK_W <= ws + GATE_WIN
        tiles.append(jnp.concatenate([wr[ws:ws + GATE_WIN, cs:cs + GATE_TILE],
                                      wi[ws:ws + GATE_WIN, cs:cs + GATE_TILE]], axis=1))
    return jnp.stack(tiles).astype(BF16)


def _rglru_layer(x, w_in, conv_w, conv_b, w_r, b_r, w_i, b_i, lam, w_out, ln_g, ln_b):
    B, S, D = x.shape
    wg = _band_gate_weights(w_r, w_i)
    row = lambda v: v.reshape(1, -1)
    const = lambda shape: pl.BlockSpec(shape, lambda b, s: (0,) * len(shape))
    tile = lambda w: pl.BlockSpec((1, REC_TS, w), lambda b, s: (b, s, 0))
    return pl.pallas_call(
        _rglru_kernel,
        grid=(B, S // REC_TS),
        in_specs=[
            tile(D),
            const((D, 2 * D_RNN)),
            const((CONV_W, D_RNN)),
            const((1, D_RNN)),
            const((N_GATE_TILES, GATE_WIN, 2 * GATE_TILE)),
            const((1, D_RNN)),
            const((1, D_RNN)),
            const((1, D_RNN)),
            const((D_RNN, D)),
            const((1, D)),
            const((1, D)),
        ],
        out_specs=[tile(D), tile(D_HALF)],
        out_shape=[jax.ShapeDtypeStruct((B, S, D), F32), jax.ShapeDtypeStruct((B, S, D_HALF), U32)],
        scratch_shapes=[
            pltpu.VMEM((REC_TS + SUBLANES, D_RNN), F32),
            pltpu.VMEM((REC_TS, D_RNN), F32),
            pltpu.VMEM((REC_TS, D_RNN), F32),
            pltpu.VMEM((1, D_RNN), F32),
        ],
        compiler_params=pltpu.CompilerParams(
            dimension_semantics=("arbitrary", "arbitrary"), vmem_limit_bytes=VMEM_LIMIT),
        name="rglru_ln",
    )(x, w_in.astype(BF16), conv_w, row(conv_b), wg, row(b_r), row(b_i), row(lam), w_out.astype(BF16),
      row(ln_g), row(ln_b))


def _router_kernel(x_ref, whi_ref, wlo_ref, bias_ref, tri_ref, idx_ref, gate_ref, cnt_ref, base_sc):
    step = pl.program_id(0)
    tr = ROUTE_T

    @pl.when(step == 0)
    def _():
        base_sc[...] = jnp.zeros((N_EXPERTS, 1), F32)

    x = x_ref[...]
    xhi = x.astype(BF16)
    xlo = (x - xhi.astype(F32)).astype(BF16)
    nt = (((1,), (1,)), ((), ()))
    whi = whi_ref[...]
    logits = (lax.dot_general(whi, xhi, nt, preferred_element_type=F32)
              + lax.dot_general(whi, xlo, nt, preferred_element_type=F32)
              + lax.dot_general(wlo_ref[...], xhi, nt, preferred_element_type=F32))
    logits = logits + bias_ref[...]

    row8 = lax.broadcasted_iota(jnp.int32, (SUBLANES, tr), 0).astype(F32)
    neg_inf = -jnp.inf
    g = jnp.where(row8 < N_GROUPS, logits[N_EXPERTS:N_EXPERTS + SUBLANES, :], neg_inf)
    gmax = jnp.max(g, axis=0, keepdims=True)
    gidx = jnp.min(jnp.where(g == gmax, row8, SUBLANES), axis=0, keepdims=True)
    g_gate = 1.0 / jnp.sum(jnp.exp(g - gmax), axis=0, keepdims=True)

    esel = logits[0:EXPERTS_PER_GROUP, :]
    for grp in range(1, N_GROUPS):
        esel = jnp.where(gidx == grp, logits[grp * EXPERTS_PER_GROUP:(grp + 1) * EXPERTS_PER_GROUP, :], esel)
    v1 = jnp.max(esel, axis=0, keepdims=True)
    i1 = jnp.min(jnp.where(esel == v1, row8, SUBLANES), axis=0, keepdims=True)
    esel2 = jnp.where(row8 == i1, neg_inf, esel)
    v2 = jnp.max(esel2, axis=0, keepdims=True)
    i2 = jnp.min(jnp.where(esel2 == v2, row8, SUBLANES), axis=0, keepdims=True)
    e21 = jnp.exp(v2 - v1)
    inv = 1.0 / (1.0 + e21)
    gate1 = inv * g_gate
    gate2 = e21 * inv * g_gate
    e1 = gidx * EXPERTS_PER_GROUP + i1
    e2 = gidx * EXPERTS_PER_GROUP + i2

    rowe = lax.broadcasted_iota(jnp.int32, (N_EXPERTS, tr), 0).astype(F32)
    hit1 = rowe == e1
    hit2 = rowe == e2
    member = jnp.where(hit1, 1.0, jnp.where(hit2, 1.0, 0.0))
    before = jnp.dot(member.astype(BF16), tri_ref[...], preferred_element_type=F32) + base_sc[...]
    rank1 = jnp.sum(jnp.where(hit1, before, 0.0), axis=0, keepdims=True)
    rank2 = jnp.sum(jnp.where(hit2, before, 0.0), axis=0, keepdims=True)
    base_sc[...] = base_sc[...] + jnp.sum(member, axis=1, keepdims=True)

    zi = jnp.zeros((1, tr), jnp.int32)
    idx_ref[...] = jnp.concatenate(
        [e1.astype(jnp.int32), e2.astype(jnp.int32), rank1.astype(jnp.int32), rank2.astype(jnp.int32),
         zi, zi, zi, zi], axis=0)
    zf = jnp.zeros((1, tr), F32)
    gate_ref[...] = jnp.concatenate([gate1, gate2, zf, zf, zf, zf, zf, zf], axis=0)
    cnt_ref[...] = jnp.broadcast_to(base_sc[...], (N_EXPERTS, LANES)).astype(jnp.int32)


def _router(xf, w_rg, b_rg, w_re, b_re):
    T, D = xf.shape
    pad_rows = ROUTE_ROWS - N_EXPERTS - N_GROUPS
    w = jnp.concatenate([w_re.T, w_rg.T, jnp.zeros((pad_rows, D), F32)], axis=0)
    whi = w.astype(BF16)
    wlo = (w - whi.astype(F32)).astype(BF16)
    bias = jnp.concatenate([b_re, b_rg, jnp.zeros((pad_rows,), F32)]).reshape(ROUTE_ROWS, 1)
    tri = jnp.asarray(np.triu(np.ones((ROUTE_T, ROUTE_T), np.float32), 1), BF16)
    const = lambda shape: pl.BlockSpec(shape, lambda i: (0,) * len(shape))
    return pl.pallas_call(
        _router_kernel,
        grid=(T // ROUTE_T,),
        in_specs=[
            pl.BlockSpec((ROUTE_T, D), lambda i: (i, 0)),
            const((ROUTE_ROWS, D)),
            const((ROUTE_ROWS, D)),
            const((ROUTE_ROWS, 1)),
            const((ROUTE_T, ROUTE_T)),
        ],
        out_specs=[
            pl.BlockSpec((SUBLANES, ROUTE_T), lambda i: (0, i)),
            pl.BlockSpec((SUBLANES, ROUTE_T), lambda i: (0, i)),
            const((N_EXPERTS, LANES)),
        ],
        out_shape=[
            jax.ShapeDtypeStruct((SUBLANES, T), jnp.int32),
            jax.ShapeDtypeStruct((SUBLANES, T), F32),
            jax.ShapeDtypeStruct((N_EXPERTS, LANES), jnp.int32),
        ],
        scratch_shapes=[pltpu.VMEM((N_EXPERTS, 1), F32)],
        compiler_params=pltpu.CompilerParams(
            dimension_semantics=("arbitrary",), vmem_limit_bytes=VMEM_LIMIT),
        name="router",
    )(xf, whi, wlo, bias, tri)


def _sc_mesh():
    return plsc.VectorSubcoreMesh(core_axis_name="c", subcore_axis_name="s",
                                  num_cores=SC_CORES, num_subcores=SC_SUBCORES)


def _sc_worker_id():
    return lax.axis_index("s") * SC_CORES + lax.axis_index("c")


def _sc_scratch(n_win, width):
    return [
        pltpu.VMEM((n_win, SC_WIN), jnp.int32),
        pltpu.VMEM((n_win, SC_WIN), jnp.int32),
        pltpu.VMEM((2, SC_WIN, width), U32),
        pltpu.SemaphoreType.DMA((2,)),
        pltpu.SemaphoreType.DMA((2,)),
    ]


def _sc_dispatch(rows, idx1, idx2, n_rows):
    _, width = rows.shape
    _, n_win, _ = idx1.shape

    @functools.partial(
        pl.kernel, mesh=_sc_mesh(), out_type=jax.ShapeDtypeStruct((n_rows, width), rows.dtype),
        scratch_types=_sc_scratch(n_win, width), name="sc_dispatch")
    def run(rows_hbm, i1_hbm, i2_hbm, o_hbm, i1_v, i2_v, buf, rsem, wsem):
        wid = _sc_worker_id()
        base = wid * (n_win * SC_WIN)
        pltpu.sync_copy(i1_hbm.at[wid], i1_v)
        pltpu.sync_copy(i2_hbm.at[wid], i2_v)

        def read(j):
            return pltpu.async_copy(rows_hbm.at[pl.ds(base + j * SC_WIN, SC_WIN)], buf.at[j % 2], rsem.at[j % 2])

        reads = {0: read(0)}
        writes = {}
        for j in range(n_win):
            if j + 1 < n_win:
                for d in writes.pop(j - 1, ()):
                    d.wait()
                reads[j + 1] = read(j + 1)
            reads.pop(j).wait()
            writes[j] = (pltpu.async_copy(buf.at[j % 2], o_hbm.at[i1_v.at[j]], wsem.at[j % 2]),
                         pltpu.async_copy(buf.at[j % 2], o_hbm.at[i2_v.at[j]], wsem.at[j % 2]))
        for j in sorted(writes):
            for d in writes[j]:
                d.wait()

    return run(rows, idx1, idx2)


def _sc_gather_pair(table, idx1, idx2):
    _, width = table.shape
    _, n_win, _ = idx1.shape
    n_tok = SC_WORKERS * n_win * SC_WIN
    out_t = jax.ShapeDtypeStruct((n_tok, width), table.dtype)

    @functools.partial(
        pl.kernel, mesh=_sc_mesh(), out_type=(out_t, out_t),
        scratch_types=_sc_scratch(n_win, width), name="sc_combine_gather")
    def run(table_hbm, i1_hbm, i2_hbm, o1_hbm, o2_hbm, i1_v, i2_v, buf, gsem, wsem):
        wid = _sc_worker_id()
        base = wid * (n_win * SC_WIN)
        pltpu.sync_copy(i1_hbm.at[wid], i1_v)
        pltpu.sync_copy(i2_hbm.at[wid], i2_v)
        work = [(i1_v, o1_hbm, j) for j in range(n_win)] + [(i2_v, o2_hbm, j) for j in range(n_win)]

        def gather(t):
            iv, _, j = work[t]
            return pltpu.async_copy(table_hbm.at[iv.at[j]], buf.at[t % 2], gsem.at[t % 2])

        def put(t):
            _, oh, j = work[t]
            return pltpu.async_copy(buf.at[t % 2], oh.at[pl.ds(base + j * SC_WIN, SC_WIN)], wsem.at[t % 2])

        gathers = {0: gather(0)}
        puts = {}
        for t in range(len(work)):
            if t + 1 < len(work):
                if t - 1 in puts:
                    puts.pop(t - 1).wait()
                gathers[t + 1] = gather(t + 1)
            gathers.pop(t).wait()
            puts[t] = put(t)
        for t in sorted(puts):
            puts[t].wait()

    return run(table, idx1, idx2)


def _moe_kernel(be_ref, nused_ref, x_ref, w1_ref, w3_ref, w2_ref, o_ref, w1_sc, w3_sc, w2_sc):
    i = pl.program_id(0)
    new_expert = jnp.logical_or(i == 0, be_ref[i] != be_ref[jnp.maximum(i - 1, 0)])

    @pl.when(new_expert)
    def _():
        w1_sc[...] = w1_ref[0, 0].astype(BF16)
        w3_sc[...] = w3_ref[0, 0].astype(BF16)
        w2_sc[...] = w2_ref[0, 0].astype(BF16)

    @pl.when(i < nused_ref[0])
    def _():
        xa, xb = _unpack_bf16_pair(x_ref[...])
        xa = xa.astype(BF16)
        xb = xb.astype(BF16)
        h1 = (jnp.dot(xa, w1_sc[:D_HALF], preferred_element_type=F32)
              + jnp.dot(xb, w1_sc[D_HALF:], preferred_element_type=F32))
        h3 = (jnp.dot(xa, w3_sc[:D_HALF], preferred_element_type=F32)
              + jnp.dot(xb, w3_sc[D_HALF:], preferred_element_type=F32))
        hdn = (jax.nn.silu(h1) * h3).astype(BF16)
        y = jnp.dot(hdn, w2_sc[...], preferred_element_type=F32)
        o_ref[...] = _pack_row_halves(y)

    @pl.when(i >= nused_ref[0])
    def _():
        o_ref[...] = jnp.zeros(o_ref.shape, o_ref.dtype)


def _moe_blocks(xbuf, block_e, n_used, w1, w3, w2, layer):
    n_rows, _ = xbuf.shape
    D = D_MODEL
    n_blocks = n_rows // MOE_BM
    grid_spec = pltpu.PrefetchScalarGridSpec(
        num_scalar_prefetch=2,
        grid=(n_blocks,),
        in_specs=[
            pl.BlockSpec((MOE_BM, D_HALF), lambda i, be, nu: (i, 0)),
            pl.BlockSpec((1, 1, D, D_EXPERT), lambda i, be, nu: (layer, be[i], 0, 0)),
            pl.BlockSpec((1, 1, D, D_EXPERT), lambda i, be, nu: (layer, be[i], 0, 0)),
            pl.BlockSpec((1, 1, D_EXPERT, D), lambda i, be, nu: (layer, be[i], 0, 0)),
        ],
        out_specs=pl.BlockSpec((MOE_BM, D_HALF), lambda i, be, nu: (i, 0)),
        scratch_shapes=[
            pltpu.VMEM((D, D_EXPERT), BF16),
            pltpu.VMEM((D, D_EXPERT), BF16),
            pltpu.VMEM((D_EXPERT, D), BF16),
        ],
    )
    return pl.pallas_call(
        _moe_kernel,
        grid_spec=grid_spec,
        out_shape=jax.ShapeDtypeStruct((n_rows, D_HALF), U32),
        compiler_params=pltpu.CompilerParams(
            dimension_semantics=("arbitrary",), vmem_limit_bytes=VMEM_LIMIT),
        name="moe_experts",
    )(block_e, n_used, xbuf, w1, w3, w2)


def _combine_kernel(x_ref, y1_ref, y2_ref, gates_ref, g_ref, b_ref, o_ref):
    x = x_ref[...]
    gates = gates_ref[...]
    g1 = gates[:, 0:1]
    g2 = gates[:, 1:2]
    a1, b1 = _unpack_bf16_pair(y1_ref[...])
    a2, b2 = _unpack_bf16_pair(y2_ref[...])
    f = jnp.concatenate([g1 * a1 + g2 * a2, g1 * b1 + g2 * b2], axis=1)
    z = ALPHA * x + f
    o_ref[...] = _layer_norm_rows(z, g_ref[...], b_ref[...])


def _combine_ln(xf, y1, y2, gates, ln_g, ln_b):
    T, D = xf.shape
    const = lambda shape: pl.BlockSpec(shape, lambda i: (0,) * len(shape))
    rows = lambda w: pl.BlockSpec((COMB_T, w), lambda i: (i, 0))
    return pl.pallas_call(
        _combine_kernel,
        grid=(T // COMB_T,),
        in_specs=[rows(D), rows(D_HALF), rows(D_HALF), rows(TOP_K), const((1, D)), const((1, D))],
        out_specs=rows(D),
        out_shape=jax.ShapeDtypeStruct((T, D), F32),
        compiler_params=pltpu.CompilerParams(
            dimension_semantics=("arbitrary",), vmem_limit_bytes=VMEM_LIMIT),
        name="moe_combine_ln",
    )(xf, y1, y2, gates, ln_g.reshape(1, D), ln_b.reshape(1, D))


def _hier_moe_ln(x, xpk, w_rg, b_rg, w_re, b_re, w1, w3, w2, layer, ln_g, ln_b):
    B, S, D = x.shape
    T = B * S
    xf = x.reshape(T, D)
    idx, gates, cnt = _router(xf, w_rg, b_rg, w_re, b_re)
    counts = cnt[:, 0]
    padded = ((counts + MOE_BM - 1) // MOE_BM) * MOE_BM
    pends = jnp.cumsum(padded)
    pstarts = pends - padded
    experts = jnp.arange(N_EXPERTS, dtype=jnp.int32)[:, None]

    def dest(e_row, rank_row):
        return jnp.sum(jnp.where(e_row[None, :] == experts, pstarts[:, None], 0), axis=0) + rank_row

    n_win = T // (SC_WORKERS * SC_WIN)
    dest1 = dest(idx[0], idx[2]).reshape(SC_WORKERS, n_win, SC_WIN)
    dest2 = dest(idx[1], idx[3]).reshape(SC_WORKERS, n_win, SC_WIN)
    n_blocks = -(-(T * TOP_K + N_EXPERTS * (MOE_BM - 1)) // MOE_BM)
    n_rows = n_blocks * MOE_BM
    block_start = jnp.arange(n_blocks, dtype=jnp.int32) * MOE_BM
    block_e = jnp.minimum(jnp.sum(block_start[:, None] >= pends[None, :], axis=1), N_EXPERTS - 1).astype(jnp.int32)
    n_used = (pends[-1] // MOE_BM).astype(jnp.int32).reshape(1)
    xbuf = _sc_dispatch(xpk.reshape(T, D_HALF), dest1, dest2, n_rows)
    ybuf = _moe_blocks(xbuf, block_e, n_used, w1, w3, w2, layer)
    y1, y2 = _sc_gather_pair(ybuf, dest1, dest2)
    out = _combine_ln(xf, y1, y2, gates[:TOP_K].T, ln_g, ln_b)
    return out.reshape(B, S, D)


def _att_head_order():
    order = []
    for p in range(N_HEADS // 2):
        jj, m = divmod(p, 4)
        order += [8 * jj + m, 8 * jj + 4 + m]
    return order


ATT_HEAD_ORDER = _att_head_order()


def _attn_kernel(x_ref, wqkv_ref, bias_ref, sink_ref, wo_ref, g_ref, b_ref, o_ref, opk_ref, kv_ext, o_sc,
                 s_sc, p_sc):
    s = pl.program_id(1)
    tq = ATT_TQ

    @pl.when(s == 0)
    def _():
        kv_ext[0:WINDOW, :] = jnp.zeros((WINDOW, 2 * KV_DIM), BF16)

    x = x_ref[0]
    qkv = jnp.dot(x.astype(BF16), wqkv_ref[...], preferred_element_type=F32)
    q = (qkv[:, :Q_DIM] * (HEAD_DIM ** -0.5)).astype(BF16)
    kv_ext[WINDOW:WINDOW + tq, :] = qkv[:, Q_DIM:].astype(BF16)

    lane = lax.broadcasted_iota(jnp.int32, (WINDOW, LANES), 1)
    low = lane < HEAD_DIM
    sub = lax.broadcasted_iota(jnp.int32, (LANES, WINDOW), 0)
    top = sub < HEAD_DIM
    first = jnp.where(s == 0, 1, 0)
    nt = (((1,), (1,)), ((), ()))
    zero = jnp.zeros((), BF16)

    for n in range(ATT_NB):
        r0 = n * WINDOW
        for j in range(2):
            k_tile = kv_ext[r0:r0 + 2 * WINDOW, j * LANES:(j + 1) * LANES]
            v_tile = kv_ext[r0:r0 + 2 * WINDOW, KV_DIM + j * LANES:KV_DIM + (j + 1) * LANES]
            v_t = v_tile.astype(F32).T.astype(BF16)
            parts = []
            for m in range(4):
                p = 4 * j + m
                qt = q[r0:r0 + WINDOW, p * LANES:(p + 1) * LANES]
                parts.append(jnp.where(low, qt, zero))
                parts.append(jnp.where(low, zero, qt))
            qs = jnp.concatenate(parts, axis=0)
            slot = (2 * n + j) % 2
            s_sc[slot] = lax.dot_general(k_tile, qs, nt, preferred_element_type=F32)
            bias_sel = first if n == 0 else 0
            for h in range(8):
                hc = slice(h * WINDOW, (h + 1) * WINDOW)
                sc = s_sc[slot, :, hc] + bias_ref[bias_sel, 8 * j + h]
                sink = sink_ref[8 * j + h]
                mx = jnp.maximum(jnp.max(sc, axis=0, keepdims=True), sink)
                pr = jnp.exp(sc - mx)
                denom = jnp.sum(pr, axis=0, keepdims=True) + jnp.exp(sink - mx)
                p_sc[slot, :, hc] = (pr * (1.0 / denom)).astype(BF16)
            ov = jnp.dot(v_t, p_sc[slot], preferred_element_type=F32)
            for m in range(4):
                p = 4 * j + m
                o_even = ov[:, (2 * m) * WINDOW:(2 * m + 1) * WINDOW]
                o_odd = ov[:, (2 * m + 1) * WINDOW:(2 * m + 2) * WINDOW]
                o_sc[p * LANES:(p + 1) * LANES, r0:r0 + WINDOW] = jnp.where(top, o_even, o_odd).astype(BF16)

    kv_ext[0:WINDOW, :] = kv_ext[tq:tq + WINDOW, :]
    out_t = jnp.dot(wo_ref[...], o_sc[...], preferred_element_type=F32)
    z = ALPHA * x + out_t.T
    xn = _layer_norm_rows(z, g_ref[...], b_ref[...])
    o_ref[0] = xn
    opk_ref[0] = _pack_row_halves(xn)


def _attn_bias():
    qi = np.arange(WINDOW)[:, None]
    sj = np.arange(2 * WINDOW)[None, :]
    dist = qi - sj + WINDOW
    valid = (dist >= 0) & (dist < WINDOW)
    slopes = 2.0 ** (-8.0 * np.arange(1, N_HEADS + 1, dtype=np.float32) / N_HEADS)
    slopes = slopes.astype(np.float32)[ATT_HEAD_ORDER]
    sb = -(slopes[:, None, None] * dist.astype(np.float32)[None])
    later = np.where(valid[None], sb, -np.inf)
    first = np.where((valid & (sj >= WINDOW))[None], sb, -np.inf)
    bias = np.stack([later, first]).astype(np.float32)
    return jnp.asarray(np.ascontiguousarray(np.swapaxes(bias, 2, 3)))


def _attn_layer(x, w_qkv, sinks, w_o, ln_g, ln_b):
    B, S, D = x.shape
    cols = np.concatenate([np.arange(h * HEAD_DIM, (h + 1) * HEAD_DIM) for h in ATT_HEAD_ORDER])
    wqkv = jnp.concatenate([w_qkv[:, cols], w_qkv[:, Q_DIM:]], axis=1).astype(BF16)
    wo_t = w_o[cols, :].T.astype(BF16)
    sink = sinks[np.asarray(ATT_HEAD_ORDER)].reshape(N_HEADS, 1, 1)
    bias = _attn_bias()
    const = lambda shape: pl.BlockSpec(shape, lambda b, s: (0,) * len(shape))
    tile = lambda w: pl.BlockSpec((1, ATT_TQ, w), lambda b, s: (b, s, 0))
    return pl.pallas_call(
        _attn_kernel,
        grid=(B, S // ATT_TQ),
        in_specs=[
            tile(D),
            const((D, Q_DIM + 2 * KV_DIM)),
            const((2, N_HEADS, 2 * WINDOW, WINDOW)),
            const((N_HEADS, 1, 1)),
            const((D, Q_DIM)),
            const((1, D)),
            const((1, D)),
        ],
        out_specs=[tile(D), tile(D_HALF)],
        out_shape=[jax.ShapeDtypeStruct((B, S, D), F32), jax.ShapeDtypeStruct((B, S, D_HALF), U32)],
        scratch_shapes=[
            pltpu.VMEM((ATT_TQ + WINDOW, 2 * KV_DIM), BF16),
            pltpu.VMEM((Q_DIM, ATT_TQ), BF16),
            pltpu.VMEM((2, 2 * WINDOW, 8 * WINDOW), F32),
            pltpu.VMEM((2, 2 * WINDOW, 8 * WINDOW), BF16),
        ],
        compiler_params=pltpu.CompilerParams(
            dimension_semantics=("arbitrary", "arbitrary"), vmem_limit_bytes=VMEM_LIMIT),
        name="swa_attn_ln",
    )(x, wqkv, bias, sink, wo_t, ln_g.reshape(1, D), ln_b.reshape(1, D))


def kernel(x, rec_w_in, rec_conv_w, rec_conv_b, rec_w_r, rec_b_r, rec_w_i, rec_b_i, rec_lambda, rec_w_out,
           att_w_qkv, att_sinks, att_w_o, moe_w_group, moe_b_group, moe_w_expert, moe_b_expert,
           moe_w1, moe_w3, moe_w2, ln_g, ln_b):
    for layer in range(DEPTH):
        j = layer // 2
        if layer % 2 == 0:
            x, xpk = _rglru_layer(x, rec_w_in[j], rec_conv_w[j], rec_conv_b[j], rec_w_r[j], rec_b_r[j],
                                  rec_w_i[j], rec_b_i[j], rec_lambda[j], rec_w_out[j],
                                  ln_g[layer, 0], ln_b[layer, 0])
        else:
            x, xpk = _attn_layer(x, att_w_qkv[j], att_sinks[j], att_w_o[j], ln_g[layer, 0], ln_b[layer, 0])
        x = _hier_moe_ln(x, xpk, moe_w_group[layer], moe_b_group[layer], moe_w_expert[layer],
                         moe_b_expert[layer], moe_w1, moe_w3, moe_w2, layer,
                         ln_g[layer, 1], ln_b[layer, 1])
    return x
```

```python
import functools

import jax
import jax.numpy as jnp
import numpy as np
from jax import lax
from jax.experimental import pallas as pl
from jax.experimental.pallas import tpu as pltpu
from jax.experimental.pallas import tpu_sc as plsc

F32 = jnp.float32
BF16 = jnp.bfloat16
U32 = jnp.uint32

D_MODEL = 1024
DEPTH = 2
D_RNN = 1280
LRU_BLOCKS = 16
LRU_BLOCK_W = D_RNN // LRU_BLOCKS
CONV_W = 4
LRU_C = 8.0
N_HEADS = 16
N_KV_HEADS = 4
HEAD_DIM = 64
WINDOW = 128
Q_DIM = N_HEADS * HEAD_DIM
KV_DIM = N_KV_HEADS * HEAD_DIM
N_GROUPS = 4
EXPERTS_PER_GROUP = 8
N_EXPERTS = N_GROUPS * EXPERTS_PER_GROUP
TOP_K = 2
D_EXPERT = 512
ALPHA = (2 * DEPTH) ** 0.25
LN_EPS = 1e-5

LANES = 128
SUBLANES = 8
VMEM_LIMIT = 56 * 1024 * 1024

REC_TS = 256
REC_GROUPS = REC_TS // SUBLANES
GATE_TILE = 256
GATE_WIN = 512
GATE_WIN_STARTS = (0, 128, 384, 640, 768)
N_GATE_TILES = D_RNN // GATE_TILE

ROUTE_T = 512
ROUTE_ROWS = 40

MOE_BM = 256

ATT_TQ = 512
ATT_NB = ATT_TQ // WINDOW

COMB_T = 512

D_HALF = D_MODEL // 2

SC_CORES = 2
SC_SUBCORES = 16
SC_WORKERS = SC_CORES * SC_SUBCORES
SC_WIN = 64


def _layer_norm_rows(z, g, b):
    mu = jnp.mean(z, axis=-1, keepdims=True)
    zc = z - mu
    var = jnp.mean(zc * zc, axis=-1, keepdims=True)
    return zc * lax.rsqrt(var + LN_EPS) * g + b


def _pack_bf16_pair(a, b):
    ua = lax.bitcast_convert_type(a.astype(BF16).astype(F32), U32)
    ub = lax.bitcast_convert_type(b.astype(BF16).astype(F32), U32)
    return (ua >> 16) | (ub & jnp.uint32(0xFFFF0000))


def _unpack_bf16_pair(w):
    a = lax.bitcast_convert_type(w << 16, F32)
    b = lax.bitcast_convert_type(w & jnp.uint32(0xFFFF0000), F32)
    return a, b


def _pack_row_halves(x):
    return _pack_bf16_pair(x[:, :D_HALF], x[:, D_HALF:])


def _rglru_kernel(x_ref, perm_ref, perm_t_ref, w_in_ref, convw_ref, convb_ref, wg_ref, br_ref, bi_ref, lam_ref,
                  w_out_ref, g_ref, b_ref, o_ref, opk_ref, xr_ext, tail_sc, a_sc, u_sc, h_carry):
    s = pl.program_id(1)
    ts = REC_TS
    halo = (CONV_W - 1) * SUBLANES

    @pl.when(s == 0)
    def _():
        tail_sc[...] = jnp.zeros((halo, D_RNN), F32)
        h_carry[...] = jnp.zeros((1, D_RNN), F32)

    x = x_ref[0]
    xp = jnp.dot(perm_ref[...], x.astype(BF16), preferred_element_type=F32).astype(BF16)
    proj = jnp.dot(xp, w_in_ref[...], preferred_element_type=F32)
    gate = proj[:, :D_RNN]
    xr = proj[:, D_RNN:]
    row = lax.broadcasted_iota(jnp.int32, (SUBLANES, D_RNN), 0)
    for k in range(CONV_W - 1):
        r0 = ts - halo + k * SUBLANES
        cur = xr[r0:r0 + SUBLANES, :]
        prev = tail_sc[k * SUBLANES:(k + 1) * SUBLANES, :]
        xr_ext[k * SUBLANES:(k + 1) * SUBLANES, :] = jnp.where(
            row == 0, pltpu.roll(prev, 1, axis=0), pltpu.roll(cur, 1, axis=0))
    tail_sc[...] = xr[ts - halo:, :]
    xr_ext[halo:halo + ts, :] = xr
    xc = convb_ref[...] + convw_ref[CONV_W - 1:CONV_W, :] * xr
    for k in range(CONV_W - 1):
        xc = xc + convw_ref[k:k + 1, :] * xr_ext[k * SUBLANES:k * SUBLANES + ts, :]

    xcb = xc.astype(BF16)
    nlam = -lam_ref[...]
    sp = jnp.maximum(nlam, 0.0) + jnp.log1p(jnp.exp(-jnp.abs(nlam)))
    for j in range(N_GATE_TILES):
        ws = GATE_WIN_STARTS[j]
        cs = j * GATE_TILE
        pre = jnp.dot(xcb[:, ws:ws + GATE_WIN], wg_ref[j], preferred_element_type=F32)
        r = jax.nn.sigmoid(pre[:, :GATE_TILE] + br_ref[:, cs:cs + GATE_TILE])
        i = jax.nn.sigmoid(pre[:, GATE_TILE:] + bi_ref[:, cs:cs + GATE_TILE])
        log_a = (-LRU_C) * r * sp[:, cs:cs + GATE_TILE]
        a = jnp.exp(log_a)
        u = jnp.sqrt(1.0 - a * a) * (i * xc[:, cs:cs + GATE_TILE])
        a_sc[:, cs:cs + GATE_TILE] = a
        u_sc[:, cs:cs + GATE_TILE] = u

    def scan_body(gidx, carry):
        h, prod = carry
        r0 = pl.multiple_of(gidx * SUBLANES, SUBLANES)
        a8 = a_sc[pl.ds(r0, SUBLANES), :]
        h = a8 * h + u_sc[pl.ds(r0, SUBLANES), :]
        prod = a8 * prod
        u_sc[pl.ds(r0, SUBLANES), :] = h
        a_sc[pl.ds(r0, SUBLANES), :] = prod
        return h, prod

    seg_h, seg_a = lax.fori_loop(
        0, REC_GROUPS, scan_body,
        (jnp.zeros((SUBLANES, D_RNN), F32), jnp.ones((SUBLANES, D_RNN), F32)), unroll=2)
    for d in (1, 2, 4):
        keep = row >= d
        a_sh = jnp.where(keep, pltpu.roll(seg_a, d, axis=0), 1.0)
        h_sh = jnp.where(keep, pltpu.roll(seg_h, d, axis=0), 0.0)
        seg_h = seg_a * h_sh + seg_h
        seg_a = seg_a * a_sh
    h_in = h_carry[...]
    after = seg_a * h_in + seg_h
    enter = jnp.where(row == 0, h_in, pltpu.roll(after, 1, axis=0))
    h_carry[...] = after[SUBLANES - 1:SUBLANES, :]
    hs = (u_sc[...].reshape(REC_GROUPS, SUBLANES, D_RNN)
          + a_sc[...].reshape(REC_GROUPS, SUBLANES, D_RNN) * enter[None]).reshape(ts, D_RNN)

    y = hs * jax.nn.gelu(gate)
    y_t = jnp.dot(perm_t_ref[...], y.astype(BF16), preferred_element_type=F32).astype(BF16)
    out = jnp.dot(y_t, w_out_ref[...], preferred_element_type=F32)
    z = ALPHA * x + out
    xn = _layer_norm_rows(z, g_ref[...], b_ref[...])
    o_ref[0] = xn
    opk_ref[0] = _pack_row_halves(xn)


def _band_gate_weights(w_r, w_i):
    eye = jnp.eye(LRU_BLOCKS, dtype=F32)

    def dense(w):
        return (w[:, :, None, :] * eye[:, None, :, None]).reshape(D_RNN, D_RNN)

    wr, wi = dense(w_r), dense(w_i)
    tiles = []
    for j in range(N_GATE_TILES):
        ws = GATE_WIN_STARTS[j]
        cs = j * GATE_TILE
        lo_blk = cs // LRU_BLOCK_W
        hi_blk = (cs + GATE_TILE - 1) // LRU_BLOCK_W
        assert ws <= lo_blk * LRU_BLOCK_W and (hi_blk + 1) * LRU_BLOCK_W <= ws + GATE_WIN
        tiles.append(jnp.concatenate([wr[ws:ws + GATE_WIN, cs:cs + GATE_TILE],
                                      wi[ws:ws + GATE_WIN, cs:cs + GATE_TILE]], axis=1))
    return jnp.stack(tiles).astype(BF16)


def _rglru_layer(x, w_in, conv_w, conv_b, w_r, b_r, w_i, b_i, lam, w_out, ln_g, ln_b):
    B, S, D = x.shape
    wg = _band_gate_weights(w_r, w_i)
    rho = np.arange(REC_TS)
    perm_np = np.zeros((REC_TS, REC_TS), np.float32)
    perm_np[rho, (rho % SUBLANES) * REC_GROUPS + rho // SUBLANES] = 1.0
    perm = jnp.asarray(perm_np, BF16)
    perm_t = jnp.asarray(perm_np.T, BF16)
    row = lambda v: v.reshape(1, -1)
    const = lambda shape: pl.BlockSpec(shape, lambda b, s: (0,) * len(shape))
    tile = lambda w: pl.BlockSpec((1, REC_TS, w), lambda b, s: (b, s, 0))
    halo = (CONV_W - 1) * SUBLANES
    return pl.pallas_call(
        _rglru_kernel,
        grid=(B, S // REC_TS),
        in_specs=[
            tile(D),
            const((REC_TS, REC_TS)),
            const((REC_TS, REC_TS)),
            const((D, 2 * D_RNN)),
            const((CONV_W, D_RNN)),
            const((1, D_RNN)),
            const((N_GATE_TILES, GATE_WIN, 2 * GATE_TILE)),
            const((1, D_RNN)),
            const((1, D_RNN)),
            const((1, D_RNN)),
            const((D_RNN, D)),
            const((1, D)),
            const((1, D)),
        ],
        out_specs=[tile(D), tile(D_HALF)],
        out_shape=[jax.ShapeDtypeStruct((B, S, D), F32), jax.ShapeDtypeStruct((B, S, D_HALF), U32)],
        scratch_shapes=[
            pltpu.VMEM((halo + REC_TS, D_RNN), F32),
            pltpu.VMEM((halo, D_RNN), F32),
            pltpu.VMEM((REC_TS, D_RNN), F32),
            pltpu.VMEM((REC_TS, D_RNN), F32),
            pltpu.VMEM((1, D_RNN), F32),
        ],
        compiler_params=pltpu.CompilerParams(
            dimension_semantics=("arbitrary", "arbitrary"), vmem_limit_bytes=VMEM_LIMIT),
        name="rglru_ln",
    )(x, perm, perm_t, w_in.astype(BF16), conv_w, row(conv_b), wg, row(b_r), row(b_i), row(lam), w_out.astype(BF16),
      row(ln_g), row(ln_b))


def _router_kernel(x_ref, whi_ref, wlo_ref, bias_ref, tri_ref, idx_ref, gate_ref, cnt_ref, base_sc):
    step = pl.program_id(0)
    tr = ROUTE_T

    @pl.when(step == 0)
    def _():
        base_sc[...] = jnp.zeros((N_EXPERTS, 1), F32)

    x = x_ref[...]
    xhi = x.astype(BF16)
    xlo = (x - xhi.astype(F32)).astype(BF16)
    nt = (((1,), (1,)), ((), ()))
    whi = whi_ref[...]
    logits = (lax.dot_general(whi, xhi, nt, preferred_element_type=F32)
              + lax.dot_general(whi, xlo, nt, preferred_element_type=F32)
              + lax.dot_general(wlo_ref[...], xhi, nt, preferred_element_type=F32))
    logits = logits + bias_ref[...]

    row8 = lax.broadcasted_iota(jnp.int32, (SUBLANES, tr), 0).astype(F32)
    neg_inf = -jnp.inf
    g = jnp.where(row8 < N_GROUPS, logits[N_EXPERTS:N_EXPERTS + SUBLANES, :], neg_inf)
    gmax = jnp.max(g, axis=0, keepdims=True)
    gidx = jnp.min(jnp.where(g == gmax, row8, SUBLANES), axis=0, keepdims=True)
    g_gate = 1.0 / jnp.sum(jnp.exp(g - gmax), axis=0, keepdims=True)

    esel = logits[0:EXPERTS_PER_GROUP, :]
    for grp in range(1, N_GROUPS):
        esel = jnp.where(gidx == grp, logits[grp * EXPERTS_PER_GROUP:(grp + 1) * EXPERTS_PER_GROUP, :], esel)
    v1 = jnp.max(esel, axis=0, keepdims=True)
    i1 = jnp.min(jnp.where(esel == v1, row8, SUBLANES), axis=0, keepdims=True)
    esel2 = jnp.where(row8 == i1, neg_inf, esel)
    v2 = jnp.max(esel2, axis=0, keepdims=True)
    i2 = jnp.min(jnp.where(esel2 == v2, row8, SUBLANES), axis=0, keepdims=True)
    e21 = jnp.exp(v2 - v1)
    inv = 1.0 / (1.0 + e21)
    gate1 = inv * g_gate
    gate2 = e21 * inv * g_gate
    e1 = gidx * EXPERTS_PER_GROUP + i1
    e2 = gidx * EXPERTS_PER_GROUP + i2

    rowe = lax.broadcasted_iota(jnp.int32, (N_EXPERTS, tr), 0).astype(F32)
    hit1 = rowe == e1
    hit2 = rowe == e2
    member = jnp.where(hit1, 1.0, jnp.where(hit2, 1.0, 0.0))
    before = jnp.dot(member.astype(BF16), tri_ref[...], preferred_element_type=F32) + base_sc[...]
    rank1 = jnp.sum(jnp.where(hit1, before, 0.0), axis=0, keepdims=True)
    rank2 = jnp.sum(jnp.where(hit2, before, 0.0), axis=0, keepdims=True)
    base_sc[...] = base_sc[...] + jnp.sum(member, axis=1, keepdims=True)

    zi = jnp.zeros((1, tr), jnp.int32)
    idx_ref[...] = jnp.concatenate(
        [e1.astype(jnp.int32), e2.astype(jnp.int32), rank1.astype(jnp.int32), rank2.astype(jnp.int32),
         zi, zi, zi, zi], axis=0)
    zf = jnp.zeros((1, tr), F32)
    gate_ref[...] = jnp.concatenate([gate1, gate2, zf, zf, zf, zf, zf, zf], axis=0)
    cnt_ref[...] = jnp.broadcast_to(base_sc[...], (N_EXPERTS, LANES)).astype(jnp.int32)


def _router(xf, w_rg, b_rg, w_re, b_re):
    T, D = xf.shape
    pad_rows = ROUTE_ROWS - N_EXPERTS - N_GROUPS
    w = jnp.concatenate([w_re.T, w_rg.T, jnp.zeros((pad_rows, D), F32)], axis=0)
    whi = w.astype(BF16)
    wlo = (w - whi.astype(F32)).astype(BF16)
    bias = jnp.concatenate([b_re, b_rg, jnp.zeros((pad_rows,), F32)]).reshape(ROUTE_ROWS, 1)
    tri = jnp.asarray(np.triu(np.ones((ROUTE_T, ROUTE_T), np.float32), 1), BF16)
    const = lambda shape: pl.BlockSpec(shape, lambda i: (0,) * len(shape))
    return pl.pallas_call(
        _router_kernel,
        grid=(T // ROUTE_T,),
        in_specs=[
            pl.BlockSpec((ROUTE_T, D), lambda i: (i, 0)),
            const((ROUTE_ROWS, D)),
            const((ROUTE_ROWS, D)),
            const((ROUTE_ROWS, 1)),
            const((ROUTE_T, ROUTE_T)),
        ],
        out_specs=[
            pl.BlockSpec((SUBLANES, ROUTE_T), lambda i: (0, i)),
            pl.BlockSpec((SUBLANES, ROUTE_T), lambda i: (0, i)),
            const((N_EXPERTS, LANES)),
        ],
        out_shape=[
            jax.ShapeDtypeStruct((SUBLANES, T), jnp.int32),
            jax.ShapeDtypeStruct((SUBLANES, T), F32),
            jax.ShapeDtypeStruct((N_EXPERTS, LANES), jnp.int32),
        ],
        scratch_shapes=[pltpu.VMEM((N_EXPERTS, 1), F32)],
        compiler_params=pltpu.CompilerParams(
            dimension_semantics=("arbitrary",), vmem_limit_bytes=VMEM_LIMIT),
        name="router",
    )(xf, whi, wlo, bias, tri)


def _sc_mesh():
    return plsc.VectorSubcoreMesh(core_axis_name="c", subcore_axis_name="s",
                                  num_cores=SC_CORES, num_subcores=SC_SUBCORES)


def _sc_worker_id():
    return lax.axis_index("s") * SC_CORES + lax.axis_index("c")


def _sc_scratch(n_win, width):
    return [
        pltpu.VMEM((n_win, SC_WIN), jnp.int32),
        pltpu.VMEM((n_win, SC_WIN), jnp.int32),
        pltpu.VMEM((2, SC_WIN, width), U32),
        pltpu.SemaphoreType.DMA((2,)),
        pltpu.SemaphoreType.DMA((2,)),
    ]


def _sc_dispatch(rows, idx1, idx2, n_rows):
    _, width = rows.shape
    _, n_win, _ = idx1.shape

    @functools.partial(
        pl.kernel, mesh=_sc_mesh(), out_type=jax.ShapeDtypeStruct((n_rows, width), rows.dtype),
        scratch_types=_sc_scratch(n_win, width), name="sc_dispatch")
    def run(rows_hbm, i1_hbm, i2_hbm, o_hbm, i1_v, i2_v, buf, rsem, wsem):
        wid = _sc_worker_id()
        base = wid * (n_win * SC_WIN)
        pltpu.sync_copy(i1_hbm.at[wid], i1_v)
        pltpu.sync_copy(i2_hbm.at[wid], i2_v)

        def read(j):
            return pltpu.async_copy(rows_hbm.at[pl.ds(base + j * SC_WIN, SC_WIN)], buf.at[j % 2], rsem.at[j % 2])

        reads = {0: read(0)}
        writes = {}
        for j in range(n_win):
            if j + 1 < n_win:
                for d in writes.pop(j - 1, ()):
                    d.wait()
                reads[j + 1] = read(j + 1)
            reads.pop(j).wait()
            writes[j] = (pltpu.async_copy(buf.at[j % 2], o_hbm.at[i1_v.at[j]], wsem.at[j % 2]),
                         pltpu.async_copy(buf.at[j % 2], o_hbm.at[i2_v.at[j]], wsem.at[j % 2]))
        for j in sorted(writes):
            for d in writes[j]:
                d.wait()

    return run(rows, idx1, idx2)


def _sc_gather_pair(table, idx1, idx2):
    _, width = table.shape
    _, n_win, _ = idx1.shape
    n_tok = SC_WORKERS * n_win * SC_WIN
    out_t = jax.ShapeDtypeStruct((n_tok, width), table.dtype)

    @functools.partial(
        pl.kernel, mesh=_sc_mesh(), out_type=(out_t, out_t),
        scratch_types=_sc_scratch(n_win, width), name="sc_combine_gather")
    def run(table_hbm, i1_hbm, i2_hbm, o1_hbm, o2_hbm, i1_v, i2_v, buf, gsem, wsem):
        wid = _sc_worker_id()
        base = wid * (n_win * SC_WIN)
        pltpu.sync_copy(i1_hbm.at[wid], i1_v)
        pltpu.sync_copy(i2_hbm.at[wid], i2_v)
        work = [(i1_v, o1_hbm, j) for j in range(n_win)] + [(i2_v, o2_hbm, j) for j in range(n_win)]

        def gather(t):
            iv, _, j = work[t]
            return pltpu.async_copy(table_hbm.at[iv.at[j]], buf.at[t % 2], gsem.at[t % 2])

        def put(t):
            _, oh, j = work[t]
            return pltpu.async_copy(buf.at[t % 2], oh.at[pl.ds(base + j * SC_WIN, SC_WIN)], wsem.at[t % 2])

        gathers = {0: gather(0)}
        puts = {}
        for t in range(len(work)):
            if t + 1 < len(work):
                if t - 1 in puts:
                    puts.pop(t - 1).wait()
                gathers[t + 1] = gather(t + 1)
            gathers.pop(t).wait()
            puts[t] = put(t)
        for t in sorted(puts):
            puts[t].wait()

    return run(table, idx1, idx2)


def _moe_kernel(be_ref, nused_ref, x_ref, w1_ref, w3_ref, w2_ref, o_ref, w1_sc, w3_sc, w2_sc):
    i = pl.program_id(0)
    new_expert = jnp.logical_or(i == 0, be_ref[i] != be_ref[jnp.maximum(i - 1, 0)])

    @pl.when(new_expert)
    def _():
        w1_sc[...] = w1_ref[0, 0].astype(BF16)
        w3_sc[...] = w3_ref[0, 0].astype(BF16)
        w2_sc[...] = w2_ref[0, 0].astype(BF16)

    @pl.when(i < nused_ref[0])
    def _():
        xa, xb = _unpack_bf16_pair(x_ref[...])
        xa = xa.astype(BF16)
        xb = xb.astype(BF16)
        h1 = (jnp.dot(xa, w1_sc[:D_HALF], preferred_element_type=F32)
              + jnp.dot(xb, w1_sc[D_HALF:], preferred_element_type=F32))
        h3 = (jnp.dot(xa, w3_sc[:D_HALF], preferred_element_type=F32)
              + jnp.dot(xb, w3_sc[D_HALF:], preferred_element_type=F32))
        hdn = (jax.nn.silu(h1) * h3).astype(BF16)
        y = jnp.dot(hdn, w2_sc[...], preferred_element_type=F32)
        o_ref[...] = _pack_row_halves(y)

    @pl.when(i >= nused_ref[0])
    def _():
        o_ref[...] = jnp.zeros(o_ref.shape, o_ref.dtype)


def _moe_blocks(xbuf, block_e, n_used, w1, w3, w2, layer):
    n_rows, _ = xbuf.shape
    D = D_MODEL
    n_blocks = n_rows // MOE_BM
    grid_spec = pltpu.PrefetchScalarGridSpec(
        num_scalar_prefetch=2,
        grid=(n_blocks,),
        in_specs=[
            pl.BlockSpec((MOE_BM, D_HALF), lambda i, be, nu: (i, 0)),
            pl.BlockSpec((1, 1, D, D_EXPERT), lambda i, be, nu: (layer, be[i], 0, 0)),
            pl.BlockSpec((1, 1, D, D_EXPERT), lambda i, be, nu: (layer, be[i], 0, 0)),
            pl.BlockSpec((1, 1, D_EXPERT, D), lambda i, be, nu: (layer, be[i], 0, 0)),
        ],
        out_specs=pl.BlockSpec((MOE_BM, D_HALF), lambda i, be, nu: (i, 0)),
        scratch_shapes=[
            pltpu.VMEM((D, D_EXPERT), BF16),
            pltpu.VMEM((D, D_EXPERT), BF16),
            pltpu.VMEM((D_EXPERT, D), BF16),
        ],
    )
    return pl.pallas_call(
        _moe_kernel,
        grid_spec=grid_spec,
        out_shape=jax.ShapeDtypeStruct((n_rows, D_HALF), U32),
        compiler_params=pltpu.CompilerParams(
            dimension_semantics=("arbitrary",), vmem_limit_bytes=VMEM_LIMIT),
        name="moe_experts",
    )(block_e, n_used, xbuf, w1, w3, w2)


def _combine_kernel(x_ref, y1_ref, y2_ref, gates_ref, g_ref, b_ref, o_ref):
    x = x_ref[...]
    gates = gates_ref[...]
    g1 = gates[:, 0:1]
    g2 = gates[:, 1:2]
    a1, b1 = _unpack_bf16_pair(y1_ref[...])
    a2, b2 = _unpack_bf16_pair(y2_ref[...])
    f = jnp.concatenate([g1 * a1 + g2 * a2, g1 * b1 + g2 * b2], axis=1)
    z = ALPHA * x + f
    o_ref[...] = _layer_norm_rows(z, g_ref[...], b_ref[...])


def _combine_ln(xf, y1, y2, gates, ln_g, ln_b):
    T, D = xf.shape
    const = lambda shape: pl.BlockSpec(shape, lambda i: (0,) * len(shape))
    rows = lambda w: pl.BlockSpec((COMB_T, w), lambda i: (i, 0))
    return pl.pallas_call(
        _combine_kernel,
        grid=(T // COMB_T,),
        in_specs=[rows(D), rows(D_HALF), rows(D_HALF), rows(TOP_K), const((1, D)), const((1, D))],
        out_specs=rows(D),
        out_shape=jax.ShapeDtypeStruct((T, D), F32),
        compiler_params=pltpu.CompilerParams(
            dimension_semantics=("arbitrary",), vmem_limit_bytes=VMEM_LIMIT),
        name="moe_combine_ln",
    )(xf, y1, y2, gates, ln_g.reshape(1, D), ln_b.reshape(1, D))


def _hier_moe_ln(x, xpk, w_rg, b_rg, w_re, b_re, w1, w3, w2, layer, ln_g, ln_b):
    B, S, D = x.shape
    T = B * S
    xf = x.reshape(T, D)
    idx, gates, cnt = _router(xf, w_rg, b_rg, w_re, b_re)
    counts = cnt[:, 0]
    padded = ((counts + MOE_BM - 1) // MOE_BM) * MOE_BM
    pends = jnp.cumsum(padded)
    pstarts = pends - padded
    experts = jnp.arange(N_EXPERTS, dtype=jnp.int32)[:, None]

    def dest(e_row, rank_row):
        return jnp.sum(jnp.where(e_row[None, :] == experts, pstarts[:, None], 0), axis=0) + rank_row

    n_win = T // (SC_WORKERS * SC_WIN)
    dest1 = dest(idx[0], idx[2]).reshape(SC_WORKERS, n_win, SC_WIN)
    dest2 = dest(idx[1], idx[3]).reshape(SC_WORKERS, n_win, SC_WIN)
    n_blocks = -(-(T * TOP_K + N_EXPERTS * (MOE_BM - 1)) // MOE_BM)
    n_rows = n_blocks * MOE_BM
    block_start = jnp.arange(n_blocks, dtype=jnp.int32) * MOE_BM
    block_e = jnp.minimum(jnp.sum(block_start[:, None] >= pends[None, :], axis=1), N_EXPERTS - 1).astype(jnp.int32)
    n_used = (pends[-1] // MOE_BM).astype(jnp.int32).reshape(1)
    xbuf = _sc_dispatch(xpk.reshape(T, D_HALF), dest1, dest2, n_rows)
    ybuf = _moe_blocks(xbuf, block_e, n_used, w1, w3, w2, layer)
    y1, y2 = _sc_gather_pair(ybuf, dest1, dest2)
    out = _combine_ln(xf, y1, y2, gates[:TOP_K].T, ln_g, ln_b)
    return out.reshape(B, S, D)


def _att_head_order():
    order = []
    for p in range(N_HEADS // 2):
        jj, m = divmod(p, 4)
        order += [8 * jj + m, 8 * jj + 4 + m]
    return order


ATT_HEAD_ORDER = _att_head_order()


def _attn_kernel(x_ref, wqkv_ref, bias_ref, sink_ref, wo_ref, g_ref, b_ref, o_ref, opk_ref, kv_ext, o_sc,
                 s_sc, p_sc):
    s = pl.program_id(1)
    tq = ATT_TQ

    @pl.when(s == 0)
    def _():
        kv_ext[0:WINDOW, :] = jnp.zeros((WINDOW, 2 * KV_DIM), BF16)

    x = x_ref[0]
    qkv = jnp.dot(x.astype(BF16), wqkv_ref[...], preferred_element_type=F32)
    q = (qkv[:, :Q_DIM] * (HEAD_DIM ** -0.5)).astype(BF16)
    kv_ext[WINDOW:WINDOW + tq, :] = qkv[:, Q_DIM:].astype(BF16)

    lane = lax.broadcasted_iota(jnp.int32, (WINDOW, LANES), 1)
    low = lane < HEAD_DIM
    sub = lax.broadcasted_iota(jnp.int32, (LANES, WINDOW), 0)
    top = sub < HEAD_DIM
    first = jnp.where(s == 0, 1, 0)
    nt = (((1,), (1,)), ((), ()))
    zero = jnp.zeros((), BF16)

    for n in range(ATT_NB):
        r0 = n * WINDOW
        for j in range(2):
            k_tile = kv_ext[r0:r0 + 2 * WINDOW, j * LANES:(j + 1) * LANES]
            v_tile = kv_ext[r0:r0 + 2 * WINDOW, KV_DIM + j * LANES:KV_DIM + (j + 1) * LANES]
            v_t = v_tile.astype(F32).T.astype(BF16)
            parts = []
            for m in range(4):
                p = 4 * j + m
                qt = q[r0:r0 + WINDOW, p * LANES:(p + 1) * LANES]
                parts.append(jnp.where(low, qt, zero))
                parts.append(jnp.where(low, zero, qt))
            qs = jnp.concatenate(parts, axis=0)
            slot = (2 * n + j) % 2
            s_sc[slot] = lax.dot_general(k_tile, qs, nt, preferred_element_type=F32)
            bias_sel = first if n == 0 else 0
            for h in range(8):
                hc = slice(h * WINDOW, (h + 1) * WINDOW)
                sc = s_sc[slot, :, hc] + bias_ref[bias_sel, 8 * j + h]
                sink = sink_ref[8 * j + h]
                mx = jnp.maximum(jnp.max(sc, axis=0, keepdims=True), sink)
                pr = jnp.exp(sc - mx)
                denom = jnp.sum(pr, axis=0, keepdims=True) + jnp.exp(sink - mx)
                p_sc[slot, :, hc] = (pr * (1.0 / denom)).astype(BF16)
            ov = jnp.dot(v_t, p_sc[slot], preferred_element_type=F32)
            for m in range(4):
                p = 4 * j + m
                o_even = ov[:, (2 * m) * WINDOW:(2 * m + 1) * WINDOW]
                o_odd = ov[:, (2 * m + 1) * WINDOW:(2 * m + 2) * WINDOW]
                o_sc[p * LANES:(p + 1) * LANES, r0:r0 + WINDOW] = jnp.where(top, o_even, o_odd).astype(BF16)

    kv_ext[0:WINDOW, :] = kv_ext[tq:tq + WINDOW, :]
    out_t = jnp.dot(wo_ref[...], o_sc[...], preferred_element_type=F32)
    z = ALPHA * x + out_t.T
    xn = _layer_norm_rows(z, g_ref[...], b_ref[...])
    o_ref[0] = xn
    opk_ref[0] = _pack_row_halves(xn)


def _attn_bias():
    qi = np.arange(WINDOW)[:, None]
    sj = np.arange(2 * WINDOW)[None, :]
    dist = qi - sj + WINDOW
    valid = (dist >= 0) & (dist < WINDOW)
    slopes = 2.0 ** (-8.0 * np.arange(1, N_HEADS + 1, dtype=np.float32) / N_HEADS)
    slopes = slopes.astype(np.float32)[ATT_HEAD_ORDER]
    sb = -(slopes[:, None, None] * dist.astype(np.float32)[None])
    later = np.where(valid[None], sb, -np.inf)
    first = np.where((valid & (sj >= WINDOW))[None], sb, -np.inf)
    bias = np.stack([later, first]).astype(np.float32)
    return jnp.asarray(np.ascontiguousarray(np.swapaxes(bias, 2, 3)))


def _attn_layer(x, w_qkv, sinks, w_o, ln_g, ln_b):
    B, S, D = x.shape
    cols = np.concatenate([np.arange(h * HEAD_DIM, (h + 1) * HEAD_DIM) for h in ATT_HEAD_ORDER])
    wqkv = jnp.concatenate([w_qkv[:, cols], w_qkv[:, Q_DIM:]], axis=1).astype(BF16)
    wo_t = w_o[cols, :].T.astype(BF16)
    sink = sinks[np.asarray(ATT_HEAD_ORDER)].reshape(N_HEADS, 1, 1)
    bias = _attn_bias()
    const = lambda shape: pl.BlockSpec(shape, lambda b, s: (0,) * len(shape))
    tile = lambda w: pl.BlockSpec((1, ATT_TQ, w), lambda b, s: (b, s, 0))
    return pl.pallas_call(
        _attn_kernel,
        grid=(B, S // ATT_TQ),
        in_specs=[
            tile(D),
            const((D, Q_DIM + 2 * KV_DIM)),
            const((2, N_HEADS, 2 * WINDOW, WINDOW)),
            const((N_HEADS, 1, 1)),
            const((D, Q_DIM)),
            const((1, D)),
            const((1, D)),
        ],
        out_specs=[tile(D), tile(D_HALF)],
        out_shape=[jax.ShapeDtypeStruct((B, S, D), F32), jax.ShapeDtypeStruct((B, S, D_HALF), U32)],
        scratch_shapes=[
            pltpu.VMEM((ATT_TQ + WINDOW, 2 * KV_DIM), BF16),
            pltpu.VMEM((Q_DIM, ATT_TQ), BF16),
            pltpu.VMEM((2, 2 * WINDOW, 8 * WINDOW), F32),
            pltpu.VMEM((2, 2 * WINDOW, 8 * WINDOW), BF16),
        ],
        compiler_params=pltpu.CompilerParams(
            dimension_semantics=("arbitrary", "arbitrary"), vmem_limit_bytes=VMEM_LIMIT),
        name="swa_attn_ln",
    )(x, wqkv, bias, sink, wo_t, ln_g.reshape(1, D), ln_b.reshape(1, D))


def kernel(x, rec_w_in, rec_conv_w, rec_conv_b, rec_w_r, rec_b_r, rec_w_i, rec_b_i, rec_lambda, rec_w_out,
           att_w_qkv, att_sinks, att_w_o, moe_w_group, moe_b_group, moe_w_expert, moe_b_expert,
           moe_w1, moe_w3, moe_w2, ln_g, ln_b):
    for layer in range(DEPTH):
        j = layer // 2
        if layer % 2 == 0:
            x, xpk = _rglru_layer(x, rec_w_in[j], rec_conv_w[j], rec_conv_b[j], rec_w_r[j], rec_b_r[j],
                                  rec_w_i[j], rec_b_i[j], rec_lambda[j], rec_w_out[j],
                                  ln_g[layer, 0], ln_b[layer, 0])
        else:
            x, xpk = _attn_layer(x, att_w_qkv[j], att_sinks[j], att_w_o[j], ln_g[layer, 0], ln_b[layer, 0])
        x = _hier_moe_ln(x, xpk, moe_w_group[layer], moe_b_group[layer], moe_w_expert[layer],
                         moe_b_expert[layer], moe_w1, moe_w3, moe_w2, layer,
                         ln_g[layer, 1], ln_b[layer, 1])
    return x
```

```python
import functools

import jax
import jax.numpy as jnp
import numpy as np
from jax import lax
from jax.experimental import pallas as pl
from jax.experimental.pallas import tpu as pltpu
from jax.experimental.pallas import tpu_sc as plsc

F32 = jnp.float32
BF16 = jnp.bfloat16
U32 = jnp.uint32

D_MODEL = 1024
DEPTH = 2
D_RNN = 1280
LRU_BLOCKS = 16
LRU_BLOCK_W = D_RNN // LRU_BLOCKS
CONV_W = 4
LRU_C = 8.0
N_HEADS = 16
N_KV_HEADS = 4
HEAD_DIM = 64
WINDOW = 128
Q_DIM = N_HEADS * HEAD_DIM
KV_DIM = N_KV_HEADS * HEAD_DIM
N_GROUPS = 4
EXPERTS_PER_GROUP = 8
N_EXPERTS = N_GROUPS * EXPERTS_PER_GROUP
TOP_K = 2
D_EXPERT = 512
ALPHA = (2 * DEPTH) ** 0.25
LN_EPS = 1e-5

LANES = 128
SUBLANES = 8
VMEM_LIMIT = 56 * 1024 * 1024

REC_TS = 256
REC_GROUPS = REC_TS // SUBLANES
GATE_TILE = 256
GATE_WIN = 512
GATE_WIN_STARTS = (0, 128, 384, 640, 768)
N_GATE_TILES = D_RNN // GATE_TILE

ROUTE_T = 512
ROUTE_ROWS = 40

MOE_BM = 256

ATT_TQ = 512
ATT_NB = ATT_TQ // WINDOW

COMB_T = 512

D_HALF = D_MODEL // 2

SC_CORES = 2
SC_SUBCORES = 16
SC_WORKERS = SC_CORES * SC_SUBCORES
SC_WIN = 64


def _layer_norm_rows(z, g, b):
    mu = jnp.mean(z, axis=-1, keepdims=True)
    zc = z - mu
    var = jnp.mean(zc * zc, axis=-1, keepdims=True)
    return zc * lax.rsqrt(var + LN_EPS) * g + b


def _pack_bf16_pair(a, b):
    ua = lax.bitcast_convert_type(a.astype(BF16).astype(F32), U32)
    ub = lax.bitcast_convert_type(b.astype(BF16).astype(F32), U32)
    return (ua >> 16) | (ub & jnp.uint32(0xFFFF0000))


def _unpack_bf16_pair(w):
    a = lax.bitcast_convert_type(w << 16, F32)
    b = lax.bitcast_convert_type(w & jnp.uint32(0xFFFF0000), F32)
    return a, b


def _pack_row_halves(x):
    return _pack_bf16_pair(x[:, :D_HALF], x[:, D_HALF:])


def _rglru_kernel(x_ref, perm_ref, perm_t_ref, w_in_ref, convw_ref, convb_ref, wg_ref, br_ref, bi_ref, lam_ref,
                  w_out_ref, g_ref, b_ref, o_ref, opk_ref, xr_ext, tail_sc, a_sc, u_sc, h_carry):
    s = pl.program_id(1)
    ts = REC_TS
    halo = (CONV_W - 1) * SUBLANES

    @pl.when(s == 0)
    def _():
        tail_sc[...] = jnp.zeros((halo, D_RNN), F32)
        h_carry[...] = jnp.zeros((1, D_RNN), F32)

    x = x_ref[0]
    xp = jnp.dot(perm_ref[...], x.astype(BF16), preferred_element_type=F32).astype(BF16)
    proj = jnp.dot(xp, w_in_ref[...], preferred_element_type=F32)
    gate = proj[:, :D_RNN]
    xr = proj[:, D_RNN:]
    row = lax.broadcasted_iota(jnp.int32, (SUBLANES, D_RNN), 0)
    for k in range(CONV_W - 1):
        r0 = ts - halo + k * SUBLANES
        cur = xr[r0:r0 + SUBLANES, :]
        prev = tail_sc[k * SUBLANES:(k + 1) * SUBLANES, :]
        xr_ext[k * SUBLANES:(k + 1) * SUBLANES, :] = jnp.where(
            row == 0, pltpu.roll(prev, 1, axis=0), pltpu.roll(cur, 1, axis=0))
    tail_sc[...] = xr[ts - halo:, :]
    xr_ext[halo:halo + ts, :] = xr
    xc = convb_ref[...] + convw_ref[CONV_W - 1:CONV_W, :] * xr
    for k in range(CONV_W - 1):
        xc = xc + convw_ref[k:k + 1, :] * xr_ext[k * SUBLANES:k * SUBLANES + ts, :]

    xcb = xc.astype(BF16)
    nlam = -lam_ref[...]
    sp = jnp.maximum(nlam, 0.0) + jnp.log1p(jnp.exp(-jnp.abs(nlam)))
    for j in range(N_GATE_TILES):
        ws = GATE_WIN_STARTS[j]
        cs = j * GATE_TILE
        pre = jnp.dot(xcb[:, ws:ws + GATE_WIN], wg_ref[j], preferred_element_type=F32)
        r = jax.nn.sigmoid(pre[:, :GATE_TILE] + br_ref[:, cs:cs + GATE_TILE])
        i = jax.nn.sigmoid(pre[:, GATE_TILE:] + bi_ref[:, cs:cs + GATE_TILE])
        log_a = (-LRU_C) * r * sp[:, cs:cs + GATE_TILE]
        a = jnp.exp(log_a)
        u = jnp.sqrt(1.0 - a * a) * (i * xc[:, cs:cs + GATE_TILE])
        a_sc[:, cs:cs + GATE_TILE] = a
        u_sc[:, cs:cs + GATE_TILE] = u

    def scan_body(gidx, carry):
        h, prod = carry
        r0 = pl.multiple_of(gidx * SUBLANES, SUBLANES)
        a8 = a_sc[pl.ds(r0, SUBLANES), :]
        h = a8 * h + u_sc[pl.ds(r0, SUBLANES), :]
        prod = a8 * prod
        u_sc[pl.ds(r0, SUBLANES), :] = h
        a_sc[pl.ds(r0, SUBLANES), :] = prod
        return h, prod

    seg_h, seg_a = lax.fori_loop(
        0, REC_GROUPS, scan_body,
        (jnp.zeros((SUBLANES, D_RNN), F32), jnp.ones((SUBLANES, D_RNN), F32)), unroll=2)
    for d in (1, 2, 4):
        keep = row >= d
        a_sh = jnp.where(keep, pltpu.roll(seg_a, d, axis=0), 1.0)
        h_sh = jnp.where(keep, pltpu.roll(seg_h, d, axis=0), 0.0)
        seg_h = seg_a * h_sh + seg_h
        seg_a = seg_a * a_sh
    h_in = h_carry[...]
    after = seg_a * h_in + seg_h
    enter = jnp.where(row == 0, h_in, pltpu.roll(after, 1, axis=0))
    h_carry[...] = after[SUBLANES - 1:SUBLANES, :]
    hs = (u_sc[...].reshape(REC_GROUPS, SUBLANES, D_RNN)
          + a_sc[...].reshape(REC_GROUPS, SUBLANES, D_RNN) * enter[None]).reshape(ts, D_RNN)

    y = hs * jax.nn.gelu(gate)
    y_t = jnp.dot(perm_t_ref[...], y.astype(BF16), preferred_element_type=F32).astype(BF16)
    out = jnp.dot(y_t, w_out_ref[...], preferred_element_type=F32)
    z = ALPHA * x + out
    xn = _layer_norm_rows(z, g_ref[...], b_ref[...])
    o_ref[0] = xn
    opk_ref[0] = _pack_row_halves(xn)


def _band_gate_weights(w_r, w_i):
    eye = jnp.eye(LRU_BLOCKS, dtype=F32)

    def dense(w):
        return (w[:, :, None, :] * eye[:, None, :, None]).reshape(D_RNN, D_RNN)

    wr, wi = dense(w_r), dense(w_i)
    tiles = []
    for j in range(N_GATE_TILES):
        ws = GATE_WIN_STARTS[j]
        cs = j * GATE_TILE
        lo_blk = cs // LRU_BLOCK_W
        hi_blk = (cs + GATE_TILE - 1) // LRU_BLOCK_W
        assert ws <= lo_blk * LRU_BLOCK_W and (hi_blk + 1) * LRU_BLOCK_W <= ws + GATE_WIN
        tiles.append(jnp.concatenate([wr[ws:ws + GATE_WIN, cs:cs + GATE_TILE],
                                      wi[ws:ws + GATE_WIN, cs:cs + GATE_TILE]], axis=1))
    return jnp.stack(tiles).astype(BF16)


def _rglru_layer(x, w_in, conv_w, conv_b, w_r, b_r, w_i, b_i, lam, w_out, ln_g, ln_b):
    B, S, D = x.shape
    wg = _band_gate_weights(w_r, w_i)
    rho = np.arange(REC_TS)
    perm_np = np.zeros((REC_TS, REC_TS), np.float32)
    perm_np[rho, (rho % SUBLANES) * REC_GROUPS + rho // SUBLANES] = 1.0
    perm = jnp.asarray(perm_np, BF16)
    perm_t = jnp.asarray(perm_np.T, BF16)
    row = lambda v: v.reshape(1, -1)
    const = lambda shape: pl.BlockSpec(shape, lambda b, s: (0,) * len(shape))
    tile = lambda w: pl.BlockSpec((1, REC_TS, w), lambda b, s: (b, s, 0))
    halo = (CONV_W - 1) * SUBLANES
    return pl.pallas_call(
        _rglru_kernel,
        grid=(B, S // REC_TS),
        in_specs=[
            tile(D),
            const((REC_TS, REC_TS)),
            const((REC_TS, REC_TS)),
            const((D, 2 * D_RNN)),
            const((CONV_W, D_RNN)),
            const((1, D_RNN)),
            const((N_GATE_TILES, GATE_WIN, 2 * GATE_TILE)),
            const((1, D_RNN)),
            const((1, D_RNN)),
            const((1, D_RNN)),
            const((D_RNN, D)),
            const((1, D)),
            const((1, D)),
        ],
        out_specs=[tile(D), tile(D_HALF)],
        out_shape=[jax.ShapeDtypeStruct((B, S, D), F32), jax.ShapeDtypeStruct((B, S, D_HALF), U32)],
        scratch_shapes=[
            pltpu.VMEM((halo + REC_TS, D_RNN), F32),
            pltpu.VMEM((halo, D_RNN), F32),
            pltpu.VMEM((REC_TS, D_RNN), F32),
            pltpu.VMEM((REC_TS, D_RNN), F32),
            pltpu.VMEM((1, D_RNN), F32),
        ],
        compiler_params=pltpu.CompilerParams(
            dimension_semantics=("arbitrary", "arbitrary"), vmem_limit_bytes=VMEM_LIMIT),
        name="rglru_ln",
    )(x, perm, perm_t, w_in.astype(BF16), conv_w, row(conv_b), wg, row(b_r), row(b_i), row(lam), w_out.astype(BF16),
      row(ln_g), row(ln_b))


def _router_kernel(x_ref, whi_ref, wlo_ref, bias_ref, tri_ref, idx_ref, gate_ref, cnt_ref, base_sc):
    step = pl.program_id(0)
    tr = ROUTE_T

    @pl.when(step == 0)
    def _():
        base_sc[...] = jnp.zeros((N_EXPERTS, 1), F32)

    x = x_ref[...]
    xhi = x.astype(BF16)
    xlo = (x - xhi.astype(F32)).astype(BF16)
    nt = (((1,), (1,)), ((), ()))
    whi = whi_ref[...]
    logits = (lax.dot_general(whi, xhi, nt, preferred_element_type=F32)
              + lax.dot_general(whi, xlo, nt, preferred_element_type=F32)
              + lax.dot_general(wlo_ref[...], xhi, nt, preferred_element_type=F32))
    logits = logits + bias_ref[...]

    row8 = lax.broadcasted_iota(jnp.int32, (SUBLANES, tr), 0).astype(F32)
    neg_inf = -jnp.inf
    g = jnp.where(row8 < N_GROUPS, logits[N_EXPERTS:N_EXPERTS + SUBLANES, :], neg_inf)
    gmax = jnp.max(g, axis=0, keepdims=True)
    gidx = jnp.min(jnp.where(g == gmax, row8, SUBLANES), axis=0, keepdims=True)
    g_gate = 1.0 / jnp.sum(jnp.exp(g - gmax), axis=0, keepdims=True)

    esel = logits[0:EXPERTS_PER_GROUP, :]
    for grp in range(1, N_GROUPS):
        esel = jnp.where(gidx == grp, logits[grp * EXPERTS_PER_GROUP:(grp + 1) * EXPERTS_PER_GROUP, :], esel)
    v1 = jnp.max(esel, axis=0, keepdims=True)
    i1 = jnp.min(jnp.where(esel == v1, row8, SUBLANES), axis=0, keepdims=True)
    esel2 = jnp.where(row8 == i1, neg_inf, esel)
    v2 = jnp.max(esel2, axis=0, keepdims=True)
    i2 = jnp.min(jnp.where(esel2 == v2, row8, SUBLANES), axis=0, keepdims=True)
    e21 = jnp.exp(v2 - v1)
    inv = 1.0 / (1.0 + e21)
    gate1 = inv * g_gate
    gate2 = e21 * inv * g_gate
    e1 = gidx * EXPERTS_PER_GROUP + i1
    e2 = gidx * EXPERTS_PER_GROUP + i2

    rowe = lax.broadcasted_iota(jnp.int32, (N_EXPERTS, tr), 0).astype(F32)
    hit1 = rowe == e1
    hit2 = rowe == e2
    member = jnp.where(hit1, 1.0, jnp.where(hit2, 1.0, 0.0))
    before = jnp.dot(member.astype(BF16), tri_ref[...], preferred_element_type=F32) + base_sc[...]
    rank1 = jnp.sum(jnp.where(hit1, before, 0.0), axis=0, keepdims=True)
    rank2 = jnp.sum(jnp.where(hit2, before, 0.0), axis=0, keepdims=True)
    base_sc[...] = base_sc[...] + jnp.sum(member, axis=1, keepdims=True)

    zi = jnp.zeros((1, tr), jnp.int32)
    idx_ref[...] = jnp.concatenate(
        [e1.astype(jnp.int32), e2.astype(jnp.int32), rank1.astype(jnp.int32), rank2.astype(jnp.int32),
         zi, zi, zi, zi], axis=0)
    zf = jnp.zeros((1, tr), F32)
    gate_ref[...] = jnp.concatenate([gate1, gate2, zf, zf, zf, zf, zf, zf], axis=0)
    cnt_ref[...] = jnp.broadcast_to(base_sc[...], (N_EXPERTS, LANES)).astype(jnp.int32)


def _router(xf, w_rg, b_rg, w_re, b_re):
    T, D = xf.shape
    pad_rows = ROUTE_ROWS - N_EXPERTS - N_GROUPS
    w = jnp.concatenate([w_re.T, w_rg.T, jnp.zeros((pad_rows, D), F32)], axis=0)
    whi = w.astype(BF16)
    wlo = (w - whi.astype(F32)).astype(BF16)
    bias = jnp.concatenate([b_re, b_rg, jnp.zeros((pad_rows,), F32)]).reshape(ROUTE_ROWS, 1)
    tri = jnp.asarray(np.triu(np.ones((ROUTE_T, ROUTE_T), np.float32), 1), BF16)
    const = lambda shape: pl.BlockSpec(shape, lambda i: (0,) * len(shape))
    return pl.pallas_call(
        _router_kernel,
        grid=(T // ROUTE_T,),
        in_specs=[
            pl.BlockSpec((ROUTE_T, D), lambda i: (i, 0)),
            const((ROUTE_ROWS, D)),
            const((ROUTE_ROWS, D)),
            const((ROUTE_ROWS, 1)),
            const((ROUTE_T, ROUTE_T)),
        ],
        out_specs=[
            pl.BlockSpec((SUBLANES, ROUTE_T), lambda i: (0, i)),
            pl.BlockSpec((SUBLANES, ROUTE_T), lambda i: (0, i)),
            const((N_EXPERTS, LANES)),
        ],
        out_shape=[
            jax.ShapeDtypeStruct((SUBLANES, T), jnp.int32),
            jax.ShapeDtypeStruct((SUBLANES, T), F32),
            jax.ShapeDtypeStruct((N_EXPERTS, LANES), jnp.int32),
        ],
        scratch_shapes=[pltpu.VMEM((N_EXPERTS, 1), F32)],
        compiler_params=pltpu.CompilerParams(
            dimension_semantics=("arbitrary",), vmem_limit_bytes=VMEM_LIMIT),
        name="router",
    )(xf, whi, wlo, bias, tri)


def _sc_mesh():
    return plsc.VectorSubcoreMesh(core_axis_name="c", subcore_axis_name="s",
                                  num_cores=SC_CORES, num_subcores=SC_SUBCORES)


def _sc_worker_id():
    return lax.axis_index("s") * SC_CORES + lax.axis_index("c")


def _sc_scratch(n_win, width):
    return [
        pltpu.VMEM((n_win, SC_WIN), jnp.int32),
        pltpu.VMEM((n_win, SC_WIN), jnp.int32),
        pltpu.VMEM((2, SC_WIN, width), U32),
        pltpu.SemaphoreType.DMA((2,)),
        pltpu.SemaphoreType.DMA((2,)),
    ]


def _sc_dispatch(rows, idx1, idx2, n_rows):
    _, width = rows.shape
    _, n_win, _ = idx1.shape

    @functools.partial(
        pl.kernel, mesh=_sc_mesh(), out_type=jax.ShapeDtypeStruct((n_rows, width), rows.dtype),
        scratch_types=_sc_scratch(n_win, width), name="sc_dispatch")
    def run(rows_hbm, i1_hbm, i2_hbm, o_hbm, i1_v, i2_v, buf, rsem, wsem):
        wid = _sc_worker_id()
        base = wid * (n_win * SC_WIN)
        pltpu.sync_copy(i1_hbm.at[wid], i1_v)
        pltpu.sync_copy(i2_hbm.at[wid], i2_v)

        def read(j):
            return pltpu.async_copy(rows_hbm.at[pl.ds(base + j * SC_WIN, SC_WIN)], buf.at[j % 2], rsem.at[j % 2])

        reads = {0: read(0)}
        writes = {}
        for j in range(n_win):
            if j + 1 < n_win:
                for d in writes.pop(j - 1, ()):
                    d.wait()
                reads[j + 1] = read(j + 1)
            reads.pop(j).wait()
            writes[j] = (pltpu.async_copy(buf.at[j % 2], o_hbm.at[i1_v.at[j]], wsem.at[j % 2]),
                         pltpu.async_copy(buf.at[j % 2], o_hbm.at[i2_v.at[j]], wsem.at[j % 2]))
        for j in sorted(writes):
            for d in writes[j]:
                d.wait()

    return run(rows, idx1, idx2)


def _sc_gather_pair(table, idx1, idx2):
    _, width = table.shape
    _, n_win, _ = idx1.shape
    n_tok = SC_WORKERS * n_win * SC_WIN
    out_t = jax.ShapeDtypeStruct((n_tok, width), table.dtype)

    @functools.partial(
        pl.kernel, mesh=_sc_mesh(), out_type=(out_t, out_t),
        scratch_types=_sc_scratch(n_win, width), name="sc_combine_gather")
    def run(table_hbm, i1_hbm, i2_hbm, o1_hbm, o2_hbm, i1_v, i2_v, buf, gsem, wsem):
        wid = _sc_worker_id()
        base = wid * (n_win * SC_WIN)
        pltpu.sync_copy(i1_hbm.at[wid], i1_v)
        pltpu.sync_copy(i2_hbm.at[wid], i2_v)
        work = [(i1_v, o1_hbm, j) for j in range(n_win)] + [(i2_v, o2_hbm, j) for j in range(n_win)]

        def gather(t):
            iv, _, j = work[t]
            return pltpu.async_copy(table_hbm.at[iv.at[j]], buf.at[t % 2], gsem.at[t % 2])

        def put(t):
            _, oh, j = work[t]
            return pltpu.async_copy(buf.at[t % 2], oh.at[pl.ds(base + j * SC_WIN, SC_WIN)], wsem.at[t % 2])

        gathers = {0: gather(0)}
        puts = {}
        for t in range(len(work)):
            if t + 1 < len(work):
                if t - 1 in puts:
                    puts.pop(t - 1).wait()
                gathers[t + 1] = gather(t + 1)
            gathers.pop(t).wait()
            puts[t] = put(t)
        for t in sorted(puts):
            puts[t].wait()

    return run(table, idx1, idx2)


def _moe_kernel(layer, be_ref, slot_ref, nxt_ref, nused_ref, x_ref, w1_hbm, w3_hbm, w2_hbm, o_ref,
                w1_st, w3_st, w2_st, w1_sc, w3_sc, w2_sc, sems):
    i = pl.program_id(0)
    expert = be_ref[i]
    slot = slot_ref[i]
    new_expert = jnp.logical_or(i == 0, expert != be_ref[jnp.maximum(i - 1, 0)])

    def weight_copies(e, sl):
        return [pltpu.make_async_copy(hbm.at[layer, e], stage.at[sl], sems.at[k, sl])
                for k, (hbm, stage) in enumerate(((w1_hbm, w1_st), (w3_hbm, w3_st), (w2_hbm, w2_st)))]

    @pl.when(i == 0)
    def _():
        for cp in weight_copies(expert, slot):
            cp.start()

    @pl.when(new_expert)
    def _():
        for cp in weight_copies(expert, slot):
            cp.wait()
        w1_sc[...] = w1_st[slot].astype(BF16)
        w3_sc[...] = w3_st[slot].astype(BF16)
        w2_sc[...] = w2_st[slot].astype(BF16)
        nxt = nxt_ref[i]

        @pl.when(nxt >= 0)
        def _():
            for cp in weight_copies(nxt, 1 - slot):
                cp.start()

    @pl.when(i < nused_ref[0])
    def _():
        xa, xb = _unpack_bf16_pair(x_ref[...])
        xa = xa.astype(BF16)
        xb = xb.astype(BF16)
        h1 = (jnp.dot(xa, w1_sc[:D_HALF], preferred_element_type=F32)
              + jnp.dot(xb, w1_sc[D_HALF:], preferred_element_type=F32))
        h3 = (jnp.dot(xa, w3_sc[:D_HALF], preferred_element_type=F32)
              + jnp.dot(xb, w3_sc[D_HALF:], preferred_element_type=F32))
        hdn = (jax.nn.silu(h1) * h3).astype(BF16)
        y = jnp.dot(hdn, w2_sc[...], preferred_element_type=F32)
        o_ref[...] = _pack_row_halves(y)

    @pl.when(i >= nused_ref[0])
    def _():
        o_ref[...] = jnp.zeros(o_ref.shape, o_ref.dtype)


def _moe_blocks(xbuf, block_e, n_used, w1, w3, w2, layer):
    n_rows, _ = xbuf.shape
    D = D_MODEL
    n_blocks = n_rows // MOE_BM
    pos = jnp.arange(n_blocks, dtype=jnp.int32)
    is_new = jnp.concatenate([jnp.ones((1,), bool), block_e[1:] != block_e[:-1]])
    slot = ((jnp.cumsum(is_new.astype(jnp.int32)) - 1) % 2).astype(jnp.int32)
    change_pos = jnp.where(is_new, pos, n_blocks)
    next_change = jnp.concatenate([lax.cummin(change_pos, reverse=True)[1:], jnp.full((1,), n_blocks, jnp.int32)])
    nxt = jnp.where(next_change < n_blocks, block_e[jnp.minimum(next_change, n_blocks - 1)], -1).astype(jnp.int32)
    rows = lambda i, be, sl, nx, nu: (i, 0)
    grid_spec = pltpu.PrefetchScalarGridSpec(
        num_scalar_prefetch=4,
        grid=(n_blocks,),
        in_specs=[
            pl.BlockSpec((MOE_BM, D_HALF), rows),
            pl.BlockSpec(memory_space=pl.ANY),
            pl.BlockSpec(memory_space=pl.ANY),
            pl.BlockSpec(memory_space=pl.ANY),
        ],
        out_specs=pl.BlockSpec((MOE_BM, D_HALF), rows),
        scratch_shapes=[
            pltpu.VMEM((2, D, D_EXPERT), F32),
            pltpu.VMEM((2, D, D_EXPERT), F32),
            pltpu.VMEM((2, D_EXPERT, D), F32),
            pltpu.VMEM((D, D_EXPERT), BF16),
            pltpu.VMEM((D, D_EXPERT), BF16),
            pltpu.VMEM((D_EXPERT, D), BF16),
            pltpu.SemaphoreType.DMA((3, 2)),
        ],
    )
    return pl.pallas_call(
        functools.partial(_moe_kernel, layer),
        grid_spec=grid_spec,
        out_shape=jax.ShapeDtypeStruct((n_rows, D_HALF), U32),
        compiler_params=pltpu.CompilerParams(
            dimension_semantics=("arbitrary",), vmem_limit_bytes=VMEM_LIMIT),
        name="moe_experts",
    )(block_e, slot, nxt, n_used, xbuf, w1, w3, w2)


def _combine_kernel(x_ref, y1_ref, y2_ref, gates_ref, g_ref, b_ref, o_ref):
    x = x_ref[...]
    gates = gates_ref[...]
    g1 = gates[:, 0:1]
    g2 = gates[:, 1:2]
    a1, b1 = _unpack_bf16_pair(y1_ref[...])
    a2, b2 = _unpack_bf16_pair(y2_ref[...])
    f = jnp.concatenate([g1 * a1 + g2 * a2, g1 * b1 + g2 * b2], axis=1)
    z = ALPHA * x + f
    o_ref[...] = _layer_norm_rows(z, g_ref[...], b_ref[...])


def _combine_ln(xf, y1, y2, gates, ln_g, ln_b):
    T, D = xf.shape
    const = lambda shape: pl.BlockSpec(shape, lambda i: (0,) * len(shape))
    rows = lambda w: pl.BlockSpec((COMB_T, w), lambda i: (i, 0))
    return pl.pallas_call(
        _combine_kernel,
        grid=(T // COMB_T,),
        in_specs=[rows(D), rows(D_HALF), rows(D_HALF), rows(TOP_K), const((1, D)), const((1, D))],
        out_specs=rows(D),
        out_shape=jax.ShapeDtypeStruct((T, D), F32),
        compiler_params=pltpu.CompilerParams(
            dimension_semantics=("arbitrary",), vmem_limit_bytes=VMEM_LIMIT),
        name="moe_combine_ln",
    )(xf, y1, y2, gates, ln_g.reshape(1, D), ln_b.reshape(1, D))


def _hier_moe_ln(x, xpk, w_rg, b_rg, w_re, b_re, w1, w3, w2, layer, ln_g, ln_b):
    B, S, D = x.shape
    T = B * S
    xf = x.reshape(T, D)
    idx, gates, cnt = _router(xf, w_rg, b_rg, w_re, b_re)
    counts = cnt[:, 0]
    padded = ((counts + MOE_BM - 1) // MOE_BM) * MOE_BM
    pends = jnp.cumsum(padded)
    pstarts = pends - padded
    experts = jnp.arange(N_EXPERTS, dtype=jnp.int32)[:, None]

    def dest(e_row, rank_row):
        return jnp.sum(jnp.where(e_row[None, :] == experts, pstarts[:, None], 0), axis=0) + rank_row

    n_win = T // (SC_WORKERS * SC_WIN)
    dest1 = dest(idx[0], idx[2]).reshape(SC_WORKERS, n_win, SC_WIN)
    dest2 = dest(idx[1], idx[3]).reshape(SC_WORKERS, n_win, SC_WIN)
    n_blocks = -(-(T * TOP_K + N_EXPERTS * (MOE_BM - 1)) // MOE_BM)
    n_rows = n_blocks * MOE_BM
    block_start = jnp.arange(n_blocks, dtype=jnp.int32) * MOE_BM
    block_e = jnp.minimum(jnp.sum(block_start[:, None] >= pends[None, :], axis=1), N_EXPERTS - 1).astype(jnp.int32)
    n_used = (pends[-1] // MOE_BM).astype(jnp.int32).reshape(1)
    xbuf = _sc_dispatch(xpk.reshape(T, D_HALF), dest1, dest2, n_rows)
    ybuf = _moe_blocks(xbuf, block_e, n_used, w1, w3, w2, layer)
    y1, y2 = _sc_gather_pair(ybuf, dest1, dest2)
    out = _combine_ln(xf, y1, y2, gates[:TOP_K].T, ln_g, ln_b)
    return out.reshape(B, S, D)


def _att_head_order():
    order = []
    for p in range(N_HEADS // 2):
        jj, m = divmod(p, 4)
        order += [8 * jj + m, 8 * jj + 4 + m]
    return order


ATT_HEAD_ORDER = _att_head_order()


def _attn_kernel(x_ref, wqkv_ref, bias_ref, sink_ref, wo_ref, g_ref, b_ref, o_ref, opk_ref, kv_ext, o_sc,
                 s_sc, p_sc):
    s = pl.program_id(1)
    tq = ATT_TQ

    @pl.when(s == 0)
    def _():
        kv_ext[0:WINDOW, :] = jnp.zeros((WINDOW, 2 * KV_DIM), BF16)

    x = x_ref[0]
    qkv = jnp.dot(x.astype(BF16), wqkv_ref[...], preferred_element_type=F32)
    q = (qkv[:, :Q_DIM] * (HEAD_DIM ** -0.5)).astype(BF16)
    kv_ext[WINDOW:WINDOW + tq, :] = qkv[:, Q_DIM:].astype(BF16)

    lane = lax.broadcasted_iota(jnp.int32, (WINDOW, LANES), 1)
    low = lane < HEAD_DIM
    sub = lax.broadcasted_iota(jnp.int32, (LANES, WINDOW), 0)
    top = sub < HEAD_DIM
    first = jnp.where(s == 0, 1, 0)
    nt = (((1,), (1,)), ((), ()))
    zero = jnp.zeros((), BF16)

    for n in range(ATT_NB):
        r0 = n * WINDOW
        for j in range(2):
            k_tile = kv_ext[r0:r0 + 2 * WINDOW, j * LANES:(j + 1) * LANES]
            v_tile = kv_ext[r0:r0 + 2 * WINDOW, KV_DIM + j * LANES:KV_DIM + (j + 1) * LANES]
            v_t = v_tile.astype(F32).T.astype(BF16)
            parts = []
            for m in range(4):
                p = 4 * j + m
                qt = q[r0:r0 + WINDOW, p * LANES:(p + 1) * LANES]
                parts.append(jnp.where(low, qt, zero))
                parts.append(jnp.where(low, zero, qt))
            qs = jnp.concatenate(parts, axis=0)
            slot = (2 * n + j) % 2
            s_sc[slot] = lax.dot_general(k_tile, qs, nt, preferred_element_type=F32)
            bias_sel = first if n == 0 else 0
            for h in range(8):
                hc = slice(h * WINDOW, (h + 1) * WINDOW)
                sc = s_sc[slot, :, hc] + bias_ref[bias_sel, 8 * j + h]
                sink = sink_ref[8 * j + h]
                mx = jnp.maximum(jnp.max(sc, axis=0, keepdims=True), sink)
                pr = jnp.exp(sc - mx)
                denom = jnp.sum(pr, axis=0, keepdims=True) + jnp.exp(sink - mx)
                p_sc[slot, :, hc] = (pr * (1.0 / denom)).astype(BF16)
            ov = jnp.dot(v_t, p_sc[slot], preferred_element_type=F32)
            for m in range(4):
                p = 4 * j + m
                o_even = ov[:, (2 * m) * WINDOW:(2 * m + 1) * WINDOW]
                o_odd = ov[:, (2 * m + 1) * WINDOW:(2 * m + 2) * WINDOW]
                o_sc[p * LANES:(p + 1) * LANES, r0:r0 + WINDOW] = jnp.where(top, o_even, o_odd).astype(BF16)

    kv_ext[0:WINDOW, :] = kv_ext[tq:tq + WINDOW, :]
    out_t = jnp.dot(wo_ref[...], o_sc[...], preferred_element_type=F32)
    z = ALPHA * x + out_t.T
    xn = _layer_norm_rows(z, g_ref[...], b_ref[...])
    o_ref[0] = xn
    opk_ref[0] = _pack_row_halves(xn)


def _attn_bias():
    qi = np.arange(WINDOW)[:, None]
    sj = np.arange(2 * WINDOW)[None, :]
    dist = qi - sj + WINDOW
    valid = (dist >= 0) & (dist < WINDOW)
    slopes = 2.0 ** (-8.0 * np.arange(1, N_HEADS + 1, dtype=np.float32) / N_HEADS)
    slopes = slopes.astype(np.float32)[ATT_HEAD_ORDER]
    sb = -(slopes[:, None, None] * dist.astype(np.float32)[None])
    later = np.where(valid[None], sb, -np.inf)
    first = np.where((valid & (sj >= WINDOW))[None], sb, -np.inf)
    bias = np.stack([later, first]).astype(np.float32)
    return jnp.asarray(np.ascontiguousarray(np.swapaxes(bias, 2, 3)))


def _attn_layer(x, w_qkv, sinks, w_o, ln_g, ln_b):
    B, S, D = x.shape
    cols = np.concatenate([np.arange(h * HEAD_DIM, (h + 1) * HEAD_DIM) for h in ATT_HEAD_ORDER])
    wqkv = jnp.concatenate([w_qkv[:, cols], w_qkv[:, Q_DIM:]], axis=1).astype(BF16)
    wo_t = w_o[cols, :].T.astype(BF16)
    sink = sinks[np.asarray(ATT_HEAD_ORDER)].reshape(N_HEADS, 1, 1)
    bias = _attn_bias()
    const = lambda shape: pl.BlockSpec(shape, lambda b, s: (0,) * len(shape))
    tile = lambda w: pl.BlockSpec((1, ATT_TQ, w), lambda b, s: (b, s, 0))
    return pl.pallas_call(
        _attn_kernel,
        grid=(B, S // ATT_TQ),
        in_specs=[
            tile(D),
            const((D, Q_DIM + 2 * KV_DIM)),
            const((2, N_HEADS, 2 * WINDOW, WINDOW)),
            const((N_HEADS, 1, 1)),
            const((D, Q_DIM)),
            const((1, D)),
            const((1, D)),
        ],
        out_specs=[tile(D), tile(D_HALF)],
        out_shape=[jax.ShapeDtypeStruct((B, S, D), F32), jax.ShapeDtypeStruct((B, S, D_HALF), U32)],
        scratch_shapes=[
            pltpu.VMEM((ATT_TQ + WINDOW, 2 * KV_DIM), BF16),
            pltpu.VMEM((Q_DIM, ATT_TQ), BF16),
            pltpu.VMEM((2, 2 * WINDOW, 8 * WINDOW), F32),
            pltpu.VMEM((2, 2 * WINDOW, 8 * WINDOW), BF16),
        ],
        compiler_params=pltpu.CompilerParams(
            dimension_semantics=("arbitrary", "arbitrary"), vmem_limit_bytes=VMEM_LIMIT),
        name="swa_attn_ln",
    )(x, wqkv, bias, sink, wo_t, ln_g.reshape(1, D), ln_b.reshape(1, D))


def kernel(x, rec_w_in, rec_conv_w, rec_conv_b, rec_w_r, rec_b_r, rec_w_i, rec_b_i, rec_lambda, rec_w_out,
           att_w_qkv, att_sinks, att_w_o, moe_w_group, moe_b_group, moe_w_expert, moe_b_expert,
           moe_w1, moe_w3, moe_w2, ln_g, ln_b):
    for layer in range(DEPTH):
        j = layer // 2
        if layer % 2 == 0:
            x, xpk = _rglru_layer(x, rec_w_in[j], rec_conv_w[j], rec_conv_b[j], rec_w_r[j], rec_b_r[j],
                                  rec_w_i[j], rec_b_i[j], rec_lambda[j], rec_w_out[j],
                                  ln_g[layer, 0], ln_b[layer, 0])
        else:
            x, xpk = _attn_layer(x, att_w_qkv[j], att_sinks[j], att_w_o[j], ln_g[layer, 0], ln_b[layer, 0])
        x = _hier_moe_ln(x, xpk, moe_w_group[layer], moe_b_group[layer], moe_w_expert[layer],
                         moe_b_expert[layer], moe_w1, moe_w3, moe_w2, layer,
                         ln_g[layer, 1], ln_b[layer, 1])
    return x
```

```python
import functools

import jax
import jax.numpy as jnp
import numpy as np
from jax import lax
from jax.experimental import pallas as pl
from jax.experimental.pallas import tpu as pltpu
from jax.experimental.pallas import tpu_sc as plsc

F32 = jnp.float32
BF16 = jnp.bfloat16
U32 = jnp.uint32

D_MODEL = 1024
DEPTH = 2
D_RNN = 1280
LRU_BLOCKS = 16
LRU_BLOCK_W = D_RNN // LRU_BLOCKS
CONV_W = 4
LRU_C = 8.0
N_HEADS = 16
N_KV_HEADS = 4
HEAD_DIM = 64
WINDOW = 128
Q_DIM = N_HEADS * HEAD_DIM
KV_DIM = N_KV_HEADS * HEAD_DIM
N_GROUPS = 4
EXPERTS_PER_GROUP = 8
N_EXPERTS = N_GROUPS * EXPERTS_PER_GROUP
TOP_K = 2
D_EXPERT = 512
ALPHA = (2 * DEPTH) ** 0.25
LN_EPS = 1e-5
LOG2E = 1.4426950408889634

LANES = 128
SUBLANES = 8
VMEM_LIMIT = 56 * 1024 * 1024

REC_TS = 256
REC_GROUPS = REC_TS // SUBLANES
GATE_TILE = 256
GATE_WIN = 512
GATE_WIN_STARTS = (0, 128, 384, 640, 768)
N_GATE_TILES = D_RNN // GATE_TILE

ROUTE_T = 512
ROUTE_ROWS = 40

MOE_BM = 256

ATT_TQ = 512
ATT_NB = ATT_TQ // WINDOW

COMB_T = 512

D_HALF = D_MODEL // 2

SC_CORES = 2
SC_SUBCORES = 16
SC_WORKERS = SC_CORES * SC_SUBCORES
SC_WIN = 64


def _layer_norm_rows(z, g, b):
    mu = jnp.mean(z, axis=-1, keepdims=True)
    zc = z - mu
    var = jnp.mean(zc * zc, axis=-1, keepdims=True)
    return zc * lax.rsqrt(var + LN_EPS) * g + b


def _pack_bf16_pair(a, b):
    ua = lax.bitcast_convert_type(a.astype(BF16).astype(F32), U32)
    ub = lax.bitcast_convert_type(b.astype(BF16).astype(F32), U32)
    return (ua >> 16) | (ub & jnp.uint32(0xFFFF0000))


def _unpack_bf16_pair(w):
    a = lax.bitcast_convert_type(w << 16, F32)
    b = lax.bitcast_convert_type(w & jnp.uint32(0xFFFF0000), F32)
    return a, b


def _pack_row_halves(x):
    return _pack_bf16_pair(x[:, :D_HALF], x[:, D_HALF:])


def _rglru_kernel(x_ref, perm_ref, perm_t_ref, w_in_ref, convw_ref, convb_ref, wg_ref, br_ref, bi_ref, lam_ref,
                  w_out_ref, g_ref, b_ref, o_ref, opk_ref, xr_ext, tail_sc, a_sc, u_sc, h_carry):
    s = pl.program_id(1)
    ts = REC_TS
    halo = (CONV_W - 1) * SUBLANES

    @pl.when(s == 0)
    def _():
        tail_sc[...] = jnp.zeros((halo, D_RNN), F32)
        h_carry[...] = jnp.zeros((1, D_RNN), F32)

    x = x_ref[0]
    xp = jnp.dot(perm_ref[...], x.astype(BF16), preferred_element_type=F32).astype(BF16)
    proj = jnp.dot(xp, w_in_ref[...], preferred_element_type=F32)
    gate = proj[:, :D_RNN]
    xr = proj[:, D_RNN:]
    row = lax.broadcasted_iota(jnp.int32, (SUBLANES, D_RNN), 0)
    for k in range(CONV_W - 1):
        r0 = ts - halo + k * SUBLANES
        cur = xr[r0:r0 + SUBLANES, :]
        prev = tail_sc[k * SUBLANES:(k + 1) * SUBLANES, :]
        xr_ext[k * SUBLANES:(k + 1) * SUBLANES, :] = jnp.where(
            row == 0, pltpu.roll(prev, 1, axis=0), pltpu.roll(cur, 1, axis=0))
    tail_sc[...] = xr[ts - halo:, :]
    xr_ext[halo:halo + ts, :] = xr
    xc = convb_ref[...] + convw_ref[CONV_W - 1:CONV_W, :] * xr
    for k in range(CONV_W - 1):
        xc = xc + convw_ref[k:k + 1, :] * xr_ext[k * SUBLANES:k * SUBLANES + ts, :]

    xcb = xc.astype(BF16)
    nlam = -lam_ref[...]
    sp = jnp.maximum(nlam, 0.0) + jnp.log1p(jnp.exp(-jnp.abs(nlam)))
    for j in range(N_GATE_TILES):
        ws = GATE_WIN_STARTS[j]
        cs = j * GATE_TILE
        pre = jnp.dot(xcb[:, ws:ws + GATE_WIN], wg_ref[j], preferred_element_type=F32)
        r = jax.nn.sigmoid(pre[:, :GATE_TILE] + br_ref[:, cs:cs + GATE_TILE])
        i = jax.nn.sigmoid(pre[:, GATE_TILE:] + bi_ref[:, cs:cs + GATE_TILE])
        log_a = (-LRU_C) * r * sp[:, cs:cs + GATE_TILE]
        a = jnp.exp(log_a)
        u = jnp.sqrt(1.0 - a * a) * (i * xc[:, cs:cs + GATE_TILE])
        a_sc[:, cs:cs + GATE_TILE] = a
        u_sc[:, cs:cs + GATE_TILE] = u

    def scan_body(gidx, carry):
        h, prod = carry
        r0 = pl.multiple_of(gidx * SUBLANES, SUBLANES)
        a8 = a_sc[pl.ds(r0, SUBLANES), :]
        h = a8 * h + u_sc[pl.ds(r0, SUBLANES), :]
        prod = a8 * prod
        u_sc[pl.ds(r0, SUBLANES), :] = h
        a_sc[pl.ds(r0, SUBLANES), :] = prod
        return h, prod

    seg_h, seg_a = lax.fori_loop(
        0, REC_GROUPS, scan_body,
        (jnp.zeros((SUBLANES, D_RNN), F32), jnp.ones((SUBLANES, D_RNN), F32)), unroll=2)
    for d in (1, 2, 4):
        keep = row >= d
        a_sh = jnp.where(keep, pltpu.roll(seg_a, d, axis=0), 1.0)
        h_sh = jnp.where(keep, pltpu.roll(seg_h, d, axis=0), 0.0)
        seg_h = seg_a * h_sh + seg_h
        seg_a = seg_a * a_sh
    h_in = h_carry[...]
    after = seg_a * h_in + seg_h
    enter = jnp.where(row == 0, h_in, pltpu.roll(after, 1, axis=0))
    h_carry[...] = after[SUBLANES - 1:SUBLANES, :]
    hs = (u_sc[...].reshape(REC_GROUPS, SUBLANES, D_RNN)
          + a_sc[...].reshape(REC_GROUPS, SUBLANES, D_RNN) * enter[None]).reshape(ts, D_RNN)

    y = hs * jax.nn.gelu(gate)
    y_t = jnp.dot(perm_t_ref[...], y.astype(BF16), preferred_element_type=F32).astype(BF16)
    out = jnp.dot(y_t, w_out_ref[...], preferred_element_type=F32)
    z = ALPHA * x + out
    xn = _layer_norm_rows(z, g_ref[...], b_ref[...])
    o_ref[0] = xn
    opk_ref[0] = _pack_row_halves(xn)


def _band_gate_weights(w_r, w_i):
    eye = jnp.eye(LRU_BLOCKS, dtype=F32)

    def dense(w):
        return (w[:, :, None, :] * eye[:, None, :, None]).reshape(D_RNN, D_RNN)

    wr, wi = dense(w_r), dense(w_i)
    tiles = []
    for j in range(N_GATE_TILES):
        ws = GATE_WIN_STARTS[j]
        cs = j * GATE_TILE
        lo_blk = cs // LRU_BLOCK_W
        hi_blk = (cs + GATE_TILE - 1) // LRU_BLOCK_W
        assert ws <= lo_blk * LRU_BLOCK_W and (hi_blk + 1) * LRU_BLOCK_W <= ws + GATE_WIN
        tiles.append(jnp.concatenate([wr[ws:ws + GATE_WIN, cs:cs + GATE_TILE],
                                      wi[ws:ws + GATE_WIN, cs:cs + GATE_TILE]], axis=1))
    return jnp.stack(tiles).astype(BF16)


def _rglru_layer(x, w_in, conv_w, conv_b, w_r, b_r, w_i, b_i, lam, w_out, ln_g, ln_b):
    B, S, D = x.shape
    wg = _band_gate_weights(w_r, w_i)
    rho = np.arange(REC_TS)
    perm_np = np.zeros((REC_TS, REC_TS), np.float32)
    perm_np[rho, (rho % SUBLANES) * REC_GROUPS + rho // SUBLANES] = 1.0
    perm = jnp.asarray(perm_np, BF16)
    perm_t = jnp.asarray(perm_np.T, BF16)
    row = lambda v: v.reshape(1, -1)
    const = lambda shape: pl.BlockSpec(shape, lambda b, s: (0,) * len(shape))
    tile = lambda w: pl.BlockSpec((1, REC_TS, w), lambda b, s: (b, s, 0))
    halo = (CONV_W - 1) * SUBLANES
    return pl.pallas_call(
        _rglru_kernel,
        grid=(B, S // REC_TS),
        in_specs=[
            tile(D),
            const((REC_TS, REC_TS)),
            const((REC_TS, REC_TS)),
            const((D, 2 * D_RNN)),
            const((CONV_W, D_RNN)),
            const((1, D_RNN)),
            const((N_GATE_TILES, GATE_WIN, 2 * GATE_TILE)),
            const((1, D_RNN)),
            const((1, D_RNN)),
            const((1, D_RNN)),
            const((D_RNN, D)),
            const((1, D)),
            const((1, D)),
        ],
        out_specs=[tile(D), tile(D_HALF)],
        out_shape=[jax.ShapeDtypeStruct((B, S, D), F32), jax.ShapeDtypeStruct((B, S, D_HALF), U32)],
        scratch_shapes=[
            pltpu.VMEM((halo + REC_TS, D_RNN), F32),
            pltpu.VMEM((halo, D_RNN), F32),
            pltpu.VMEM((REC_TS, D_RNN), F32),
            pltpu.VMEM((REC_TS, D_RNN), F32),
            pltpu.VMEM((1, D_RNN), F32),
        ],
        compiler_params=pltpu.CompilerParams(
            dimension_semantics=("arbitrary", "arbitrary"), vmem_limit_bytes=VMEM_LIMIT),
        name="rglru_ln",
    )(x, perm, perm_t, w_in.astype(BF16), conv_w, row(conv_b), wg, row(b_r), row(b_i), row(lam), w_out.astype(BF16),
      row(ln_g), row(ln_b))


def _router_kernel(x_ref, whi_ref, bias_ref, tri_ref, idx_ref, gate_ref, cnt_ref, base_sc):
    step = pl.program_id(0)
    tr = ROUTE_T

    @pl.when(step == 0)
    def _():
        base_sc[...] = jnp.zeros((N_EXPERTS, 1), F32)

    x = x_ref[...]
    xhi = x.astype(BF16)
    xlo = (x - xhi.astype(F32)).astype(BF16)
    nt = (((1,), (1,)), ((), ()))
    both = lax.dot_general(whi_ref[...], xhi, nt, preferred_element_type=F32)
    logits = (both[:ROUTE_ROWS] + both[ROUTE_ROWS:]
              + lax.dot_general(whi_ref[:ROUTE_ROWS], xlo, nt, preferred_element_type=F32))
    logits = logits + bias_ref[...]

    row8 = lax.broadcasted_iota(jnp.int32, (SUBLANES, tr), 0).astype(F32)
    neg_inf = -jnp.inf
    g = jnp.where(row8 < N_GROUPS, logits[N_EXPERTS:N_EXPERTS + SUBLANES, :], neg_inf)
    gmax = jnp.max(g, axis=0, keepdims=True)
    gidx = jnp.min(jnp.where(g == gmax, row8, SUBLANES), axis=0, keepdims=True)
    g_gate = 1.0 / jnp.sum(jnp.exp(g - gmax), axis=0, keepdims=True)

    esel = logits[0:EXPERTS_PER_GROUP, :]
    for grp in range(1, N_GROUPS):
        esel = jnp.where(gidx == grp, logits[grp * EXPERTS_PER_GROUP:(grp + 1) * EXPERTS_PER_GROUP, :], esel)
    v1 = jnp.max(esel, axis=0, keepdims=True)
    i1 = jnp.min(jnp.where(esel == v1, row8, SUBLANES), axis=0, keepdims=True)
    esel2 = jnp.where(row8 == i1, neg_inf, esel)
    v2 = jnp.max(esel2, axis=0, keepdims=True)
    i2 = jnp.min(jnp.where(esel2 == v2, row8, SUBLANES), axis=0, keepdims=True)
    e21 = jnp.exp(v2 - v1)
    inv = 1.0 / (1.0 + e21)
    gate1 = inv * g_gate
    gate2 = e21 * inv * g_gate
    e1 = gidx * EXPERTS_PER_GROUP + i1
    e2 = gidx * EXPERTS_PER_GROUP + i2

    rowe = lax.broadcasted_iota(jnp.int32, (N_EXPERTS, tr), 0).astype(F32)
    hit1 = rowe == e1
    hit2 = rowe == e2
    member = jnp.where(hit1, 1.0, jnp.where(hit2, 1.0, 0.0))
    before = jnp.dot(member.astype(BF16), tri_ref[...], preferred_element_type=F32) + base_sc[...]
    rank1 = jnp.sum(jnp.where(hit1, before, 0.0), axis=0, keepdims=True)
    rank2 = jnp.sum(jnp.where(hit2, before, 0.0), axis=0, keepdims=True)
    base_sc[...] = base_sc[...] + jnp.sum(member, axis=1, keepdims=True)

    zi = jnp.zeros((1, tr), jnp.int32)
    idx_ref[...] = jnp.concatenate(
        [e1.astype(jnp.int32), e2.astype(jnp.int32), rank1.astype(jnp.int32), rank2.astype(jnp.int32),
         zi, zi, zi, zi], axis=0)
    zf = jnp.zeros((1, tr), F32)
    gate_ref[...] = jnp.concatenate([gate1, gate2, zf, zf, zf, zf, zf, zf], axis=0)
    cnt_ref[...] = jnp.broadcast_to(base_sc[...], (N_EXPERTS, LANES)).astype(jnp.int32)


def _router(xf, w_rg, b_rg, w_re, b_re):
    T, D = xf.shape
    pad_rows = ROUTE_ROWS - N_EXPERTS - N_GROUPS
    w = jnp.concatenate([w_re.T, w_rg.T, jnp.zeros((pad_rows, D), F32)], axis=0)
    whi = w.astype(BF16)
    wlo = (w - whi.astype(F32)).astype(BF16)
    w_split = jnp.concatenate([whi, wlo], axis=0)
    bias = jnp.concatenate([b_re, b_rg, jnp.zeros((pad_rows,), F32)]).reshape(ROUTE_ROWS, 1)
    tri = jnp.asarray(np.triu(np.ones((ROUTE_T, ROUTE_T), np.float32), 1), BF16)
    const = lambda shape: pl.BlockSpec(shape, lambda i: (0,) * len(shape))
    return pl.pallas_call(
        _router_kernel,
        grid=(T // ROUTE_T,),
        in_specs=[
            pl.BlockSpec((ROUTE_T, D), lambda i: (i, 0)),
            const((2 * ROUTE_ROWS, D)),
            const((ROUTE_ROWS, 1)),
            const((ROUTE_T, ROUTE_T)),
        ],
        out_specs=[
            pl.BlockSpec((SUBLANES, ROUTE_T), lambda i: (0, i)),
            pl.BlockSpec((SUBLANES, ROUTE_T), lambda i: (0, i)),
            const((N_EXPERTS, LANES)),
        ],
        out_shape=[
            jax.ShapeDtypeStruct((SUBLANES, T), jnp.int32),
            jax.ShapeDtypeStruct((SUBLANES, T), F32),
            jax.ShapeDtypeStruct((N_EXPERTS, LANES), jnp.int32),
        ],
        scratch_shapes=[pltpu.VMEM((N_EXPERTS, 1), F32)],
        compiler_params=pltpu.CompilerParams(
            dimension_semantics=("arbitrary",), vmem_limit_bytes=VMEM_LIMIT),
        name="router",
    )(xf, w_split, bias, tri)


def _sc_mesh():
    return plsc.VectorSubcoreMesh(core_axis_name="c", subcore_axis_name="s",
                                  num_cores=SC_CORES, num_subcores=SC_SUBCORES)


def _sc_worker_id():
    return lax.axis_index("s") * SC_CORES + lax.axis_index("c")


def _sc_scratch(n_win, width):
    return [
        pltpu.VMEM((n_win, SC_WIN), jnp.int32),
        pltpu.VMEM((n_win, SC_WIN), jnp.int32),
        pltpu.VMEM((2, SC_WIN, width), U32),
        pltpu.SemaphoreType.DMA((2,)),
        pltpu.SemaphoreType.DMA((2,)),
    ]


def _sc_dispatch(rows, idx1, idx2, n_rows):
    _, width = rows.shape
    _, n_win, _ = idx1.shape

    @functools.partial(
        pl.kernel, mesh=_sc_mesh(), out_type=jax.ShapeDtypeStruct((n_rows, width), rows.dtype),
        scratch_types=_sc_scratch(n_win, width), name="sc_dispatch")
    def run(rows_hbm, i1_hbm, i2_hbm, o_hbm, i1_v, i2_v, buf, rsem, wsem):
        wid = _sc_worker_id()
        base = wid * (n_win * SC_WIN)
        pltpu.sync_copy(i1_hbm.at[wid], i1_v)
        pltpu.sync_copy(i2_hbm.at[wid], i2_v)

        def read(j):
            return pltpu.async_copy(rows_hbm.at[pl.ds(base + j * SC_WIN, SC_WIN)], buf.at[j % 2], rsem.at[j % 2])

        reads = {0: read(0)}
        writes = {}
        for j in range(n_win):
            if j + 1 < n_win:
                for d in writes.pop(j - 1, ()):
                    d.wait()
                reads[j + 1] = read(j + 1)
            reads.pop(j).wait()
            writes[j] = (pltpu.async_copy(buf.at[j % 2], o_hbm.at[i1_v.at[j]], wsem.at[j % 2]),
                         pltpu.async_copy(buf.at[j % 2], o_hbm.at[i2_v.at[j]], wsem.at[j % 2]))
        for j in sorted(writes):
            for d in writes[j]:
                d.wait()

    return run(rows, idx1, idx2)


def _sc_gather_pair(table, idx1, idx2):
    _, width = table.shape
    _, n_win, _ = idx1.shape
    n_tok = SC_WORKERS * n_win * SC_WIN
    out_t = jax.ShapeDtypeStruct((n_tok, width), table.dtype)

    @functools.partial(
        pl.kernel, mesh=_sc_mesh(), out_type=(out_t, out_t),
        scratch_types=_sc_scratch(n_win, width), name="sc_combine_gather")
    def run(table_hbm, i1_hbm, i2_hbm, o1_hbm, o2_hbm, i1_v, i2_v, buf, gsem, wsem):
        wid = _sc_worker_id()
        base = wid * (n_win * SC_WIN)
        pltpu.sync_copy(i1_hbm.at[wid], i1_v)
        pltpu.sync_copy(i2_hbm.at[wid], i2_v)
        work = [(i1_v, o1_hbm, j) for j in range(n_win)] + [(i2_v, o2_hbm, j) for j in range(n_win)]

        def gather(t):
            iv, _, j = work[t]
            return pltpu.async_copy(table_hbm.at[iv.at[j]], buf.at[t % 2], gsem.at[t % 2])

        def put(t):
            _, oh, j = work[t]
            return pltpu.async_copy(buf.at[t % 2], oh.at[pl.ds(base + j * SC_WIN, SC_WIN)], wsem.at[t % 2])

        gathers = {0: gather(0)}
        puts = {}
        for t in range(len(work)):
            if t + 1 < len(work):
                if t - 1 in puts:
                    puts.pop(t - 1).wait()
                gathers[t + 1] = gather(t + 1)
            gathers.pop(t).wait()
            puts[t] = put(t)
        for t in sorted(puts):
            puts[t].wait()

    return run(table, idx1, idx2)


def _moe_kernel(layer, be_ref, slot_ref, nxt_ref, nused_ref, x_ref, w1_hbm, w3_hbm, w2_hbm, o_ref,
                w1_st, w3_st, w2_st, w1_sc, w3_sc, w2_sc, sems):
    i = pl.program_id(0)
    expert = be_ref[i]
    slot = slot_ref[i]
    new_expert = jnp.logical_or(i == 0, expert != be_ref[jnp.maximum(i - 1, 0)])

    def weight_copies(e, sl):
        return [pltpu.make_async_copy(hbm.at[layer, e], stage.at[sl], sems.at[k, sl])
                for k, (hbm, stage) in enumerate(((w1_hbm, w1_st), (w3_hbm, w3_st), (w2_hbm, w2_st)))]

    @pl.when(i == 0)
    def _():
        for cp in weight_copies(expert, slot):
            cp.start()

    @pl.when(new_expert)
    def _():
        for cp in weight_copies(expert, slot):
            cp.wait()
        w1_sc[...] = w1_st[slot].astype(BF16)
        w3_sc[...] = w3_st[slot].astype(BF16)
        w2_sc[...] = w2_st[slot].astype(BF16)
        nxt = nxt_ref[i]

        @pl.when(nxt >= 0)
        def _():
            for cp in weight_copies(nxt, 1 - slot):
                cp.start()

    @pl.when(i < nused_ref[0])
    def _():
        xa, xb = _unpack_bf16_pair(x_ref[...])
        xa = xa.astype(BF16)
        xb = xb.astype(BF16)
        h1 = (jnp.dot(xa, w1_sc[:D_HALF], preferred_element_type=F32)
              + jnp.dot(xb, w1_sc[D_HALF:], preferred_element_type=F32))
        h3 = (jnp.dot(xa, w3_sc[:D_HALF], preferred_element_type=F32)
              + jnp.dot(xb, w3_sc[D_HALF:], preferred_element_type=F32))
        hdn = (jax.nn.silu(h1) * h3).astype(BF16)
        y = jnp.dot(hdn, w2_sc[...], preferred_element_type=F32)
        o_ref[...] = _pack_row_halves(y)

    @pl.when(i >= nused_ref[0])
    def _():
        o_ref[...] = jnp.zeros(o_ref.shape, o_ref.dtype)


def _moe_blocks(xbuf, block_e, n_used, w1, w3, w2, layer):
    n_rows, _ = xbuf.shape
    D = D_MODEL
    n_blocks = n_rows // MOE_BM
    pos = jnp.arange(n_blocks, dtype=jnp.int32)
    is_new = jnp.concatenate([jnp.ones((1,), bool), block_e[1:] != block_e[:-1]])
    slot = ((jnp.cumsum(is_new.astype(jnp.int32)) - 1) % 2).astype(jnp.int32)
    change_pos = jnp.where(is_new, pos, n_blocks)
    next_change = jnp.concatenate([lax.cummin(change_pos, reverse=True)[1:], jnp.full((1,), n_blocks, jnp.int32)])
    nxt = jnp.where(next_change < n_blocks, block_e[jnp.minimum(next_change, n_blocks - 1)], -1).astype(jnp.int32)
    rows = lambda i, be, sl, nx, nu: (i, 0)
    grid_spec = pltpu.PrefetchScalarGridSpec(
        num_scalar_prefetch=4,
        grid=(n_blocks,),
        in_specs=[
            pl.BlockSpec((MOE_BM, D_HALF), rows),
            pl.BlockSpec(memory_space=pl.ANY),
            pl.BlockSpec(memory_space=pl.ANY),
            pl.BlockSpec(memory_space=pl.ANY),
        ],
        out_specs=pl.BlockSpec((MOE_BM, D_HALF), rows),
        scratch_shapes=[
            pltpu.VMEM((2, D, D_EXPERT), F32),
            pltpu.VMEM((2, D, D_EXPERT), F32),
            pltpu.VMEM((2, D_EXPERT, D), F32),
            pltpu.VMEM((D, D_EXPERT), BF16),
            pltpu.VMEM((D, D_EXPERT), BF16),
            pltpu.VMEM((D_EXPERT, D), BF16),
            pltpu.SemaphoreType.DMA((3, 2)),
        ],
    )
    return pl.pallas_call(
        functools.partial(_moe_kernel, layer),
        grid_spec=grid_spec,
        out_shape=jax.ShapeDtypeStruct((n_rows, D_HALF), U32),
        compiler_params=pltpu.CompilerParams(
            dimension_semantics=("arbitrary",), vmem_limit_bytes=VMEM_LIMIT),
        name="moe_experts",
    )(block_e, slot, nxt, n_used, xbuf, w1, w3, w2)


def _combine_kernel(x_ref, y1_ref, y2_ref, gates_ref, g_ref, b_ref, o_ref):
    x = x_ref[...]
    gates = gates_ref[...]
    g1 = gates[:, 0:1]
    g2 = gates[:, 1:2]
    a1, b1 = _unpack_bf16_pair(y1_ref[...])
    a2, b2 = _unpack_bf16_pair(y2_ref[...])
    f = jnp.concatenate([g1 * a1 + g2 * a2, g1 * b1 + g2 * b2], axis=1)
    z = ALPHA * x + f
    o_ref[...] = _layer_norm_rows(z, g_ref[...], b_ref[...])


def _combine_ln(xf, y1, y2, gates, ln_g, ln_b):
    T, D = xf.shape
    const = lambda shape: pl.BlockSpec(shape, lambda i: (0,) * len(shape))
    rows = lambda w: pl.BlockSpec((COMB_T, w), lambda i: (i, 0))
    return pl.pallas_call(
        _combine_kernel,
        grid=(T // COMB_T,),
        in_specs=[rows(D), rows(D_HALF), rows(D_HALF), rows(TOP_K), const((1, D)), const((1, D))],
        out_specs=rows(D),
        out_shape=jax.ShapeDtypeStruct((T, D), F32),
        compiler_params=pltpu.CompilerParams(
            dimension_semantics=("arbitrary",), vmem_limit_bytes=VMEM_LIMIT),
        name="moe_combine_ln",
    )(xf, y1, y2, gates, ln_g.reshape(1, D), ln_b.reshape(1, D))


def _hier_moe_ln(x, xpk, w_rg, b_rg, w_re, b_re, w1, w3, w2, layer, ln_g, ln_b):
    B, S, D = x.shape
    T = B * S
    xf = x.reshape(T, D)
    idx, gates, cnt = _router(xf, w_rg, b_rg, w_re, b_re)
    counts = cnt[:, 0]
    padded = ((counts + MOE_BM - 1) // MOE_BM) * MOE_BM
    pends = jnp.cumsum(padded)
    pstarts = pends - padded
    experts = jnp.arange(N_EXPERTS, dtype=jnp.int32)[:, None]

    def dest(e_row, rank_row):
        return jnp.sum(jnp.where(e_row[None, :] == experts, pstarts[:, None], 0), axis=0) + rank_row

    n_win = T // (SC_WORKERS * SC_WIN)
    dest1 = dest(idx[0], idx[2]).reshape(SC_WORKERS, n_win, SC_WIN)
    dest2 = dest(idx[1], idx[3]).reshape(SC_WORKERS, n_win, SC_WIN)
    n_blocks = -(-(T * TOP_K + N_EXPERTS * (MOE_BM - 1)) // MOE_BM)
    n_rows = n_blocks * MOE_BM
    block_start = jnp.arange(n_blocks, dtype=jnp.int32) * MOE_BM
    block_e = jnp.minimum(jnp.sum(block_start[:, None] >= pends[None, :], axis=1), N_EXPERTS - 1).astype(jnp.int32)
    n_used = (pends[-1] // MOE_BM).astype(jnp.int32).reshape(1)
    xbuf = _sc_dispatch(xpk.reshape(T, D_HALF), dest1, dest2, n_rows)
    ybuf = _moe_blocks(xbuf, block_e, n_used, w1, w3, w2, layer)
    y1, y2 = _sc_gather_pair(ybuf, dest1, dest2)
    out = _combine_ln(xf, y1, y2, gates[:TOP_K].T, ln_g, ln_b)
    return out.reshape(B, S, D)


def _att_head_order():
    order = []
    for p in range(N_HEADS // 2):
        jj, m = divmod(p, 4)
        order += [8 * jj + m, 8 * jj + 4 + m]
    return order


ATT_HEAD_ORDER = _att_head_order()


def _attn_kernel(x_ref, wqkv_ref, bias_ref, sink_ref, wo_ref, g_ref, b_ref, o_ref, opk_ref, kv_ext, o_sc,
                 s_sc0, s_sc1, p_sc0, p_sc1):
    s = pl.program_id(1)
    tq = ATT_TQ
    s_bufs = (s_sc0, s_sc1)
    p_bufs = (p_sc0, p_sc1)

    @pl.when(s == 0)
    def _():
        kv_ext[0:WINDOW, :] = jnp.zeros((WINDOW, 2 * KV_DIM), BF16)

    x = x_ref[0]
    qkv = jnp.dot(x.astype(BF16), wqkv_ref[...], preferred_element_type=F32)
    q = (qkv[:, :Q_DIM] * (HEAD_DIM ** -0.5 * LOG2E)).astype(BF16)
    kv_ext[WINDOW:WINDOW + tq, :] = qkv[:, Q_DIM:].astype(BF16)

    lane = lax.broadcasted_iota(jnp.int32, (WINDOW, LANES), 1)
    low = lane < HEAD_DIM
    sub = lax.broadcasted_iota(jnp.int32, (LANES, WINDOW), 0)
    top = sub < HEAD_DIM
    first = jnp.where(s == 0, 1, 0)
    nt = (((1,), (1,)), ((), ()))
    zero = jnp.zeros((), BF16)

    tiles = [(n, j) for n in range(ATT_NB) for j in range(2)]

    def scores(t):
        n, j = tiles[t]
        r0 = n * WINDOW
        k_tile = kv_ext[r0:r0 + 2 * WINDOW, j * LANES:(j + 1) * LANES]
        parts = []
        for m in range(4):
            p = 4 * j + m
            qt = q[r0:r0 + WINDOW, p * LANES:(p + 1) * LANES]
            parts.append(jnp.where(low, qt, zero))
            parts.append(jnp.where(low, zero, qt))
        qs = jnp.concatenate(parts, axis=0)
        bias_sel = first if n == 0 else 0
        s_bufs[t % 2][...] = (lax.dot_general(k_tile, qs, nt, preferred_element_type=F32)
                              + bias_ref[bias_sel, j])

    def softmax_pv(t):
        n, j = tiles[t]
        r0 = n * WINDOW
        s_sc = s_bufs[t % 2]
        p_sc = p_bufs[t % 2]
        inv_l = []
        for h in range(8):
            hc = slice(h * WINDOW, (h + 1) * WINDOW)
            sink = sink_ref[8 * j + h] * LOG2E
            mx = jnp.maximum(jnp.max(s_sc[:, hc], axis=0, keepdims=True), sink)
            pr = jnp.exp2(s_sc[:, hc] - mx)
            p_sc[:, hc] = pr.astype(BF16)
            inv_l.append(1.0 / (jnp.sum(pr, axis=0, keepdims=True) + jnp.exp2(sink - mx)))
        v_tile = kv_ext[r0:r0 + 2 * WINDOW, KV_DIM + j * LANES:KV_DIM + (j + 1) * LANES]
        v_t = v_tile.astype(F32).T.astype(BF16)
        ov = jnp.dot(v_t, p_sc[...], preferred_element_type=F32)
        for m in range(4):
            p = 4 * j + m
            o_even = ov[:, (2 * m) * WINDOW:(2 * m + 1) * WINDOW] * inv_l[2 * m]
            o_odd = ov[:, (2 * m + 1) * WINDOW:(2 * m + 2) * WINDOW] * inv_l[2 * m + 1]
            o_sc[p * LANES:(p + 1) * LANES, r0:r0 + WINDOW] = jnp.where(top, o_even, o_odd).astype(BF16)

    scores(0)
    for t in range(len(tiles)):
        if t + 1 < len(tiles):
            scores(t + 1)
        softmax_pv(t)
    kv_ext[0:WINDOW, :] = kv_ext[tq:tq + WINDOW, :]

    halves = [slice(hf * (tq // 2), (hf + 1) * (tq // 2)) for hf in range(2)]
    outs_t = [jnp.dot(wo_ref[...], o_sc[:, rows], preferred_element_type=F32) for rows in halves]
    for rows, out_t in zip(halves, outs_t):
        z = ALPHA * x[rows, :] + out_t.T
        xn = _layer_norm_rows(z, g_ref[...], b_ref[...])
        o_ref[0, rows, :] = xn
        opk_ref[0, rows, :] = _pack_row_halves(xn)


def _attn_bias():
    qi = np.arange(WINDOW)[:, None]
    sj = np.arange(2 * WINDOW)[None, :]
    dist = qi - sj + WINDOW
    valid = (dist >= 0) & (dist < WINDOW)
    slopes = 2.0 ** (-8.0 * np.arange(1, N_HEADS + 1, dtype=np.float32) / N_HEADS)
    slopes = slopes.astype(np.float32)[ATT_HEAD_ORDER]
    sb = -(slopes[:, None, None] * dist.astype(np.float32)[None])
    later = np.where(valid[None], sb, -np.inf)
    first = np.where((valid & (sj >= WINDOW))[None], sb, -np.inf)
    bias = np.stack([later, first]).astype(np.float32) * np.float32(LOG2E)
    bias = bias.reshape(2, 2, 8, WINDOW, 2 * WINDOW).transpose(0, 1, 4, 2, 3).reshape(2, 2, 2 * WINDOW, 8 * WINDOW)
    return jnp.asarray(np.ascontiguousarray(bias))


def _attn_layer(x, w_qkv, sinks, w_o, ln_g, ln_b):
    B, S, D = x.shape
    cols = np.concatenate([np.arange(h * HEAD_DIM, (h + 1) * HEAD_DIM) for h in ATT_HEAD_ORDER])
    wqkv = jnp.concatenate([w_qkv[:, cols], w_qkv[:, Q_DIM:]], axis=1).astype(BF16)
    wo_t = w_o[cols, :].T.astype(BF16)
    sink = sinks[np.asarray(ATT_HEAD_ORDER)].reshape(N_HEADS, 1, 1)
    bias = _attn_bias()
    const = lambda shape: pl.BlockSpec(shape, lambda b, s: (0,) * len(shape))
    tile = lambda w: pl.BlockSpec((1, ATT_TQ, w), lambda b, s: (b, s, 0))
    return pl.pallas_call(
        _attn_kernel,
        grid=(B, S // ATT_TQ),
        in_specs=[
            tile(D),
            const((D, Q_DIM + 2 * KV_DIM)),
            const((2, 2, 2 * WINDOW, 8 * WINDOW)),
            const((N_HEADS, 1, 1)),
            const((D, Q_DIM)),
            const((1, D)),
            const((1, D)),
        ],
        out_specs=[tile(D), tile(D_HALF)],
        out_shape=[jax.ShapeDtypeStruct((B, S, D), F32), jax.ShapeDtypeStruct((B, S, D_HALF), U32)],
        scratch_shapes=[
            pltpu.VMEM((ATT_TQ + WINDOW, 2 * KV_DIM), BF16),
            pltpu.VMEM((Q_DIM, ATT_TQ), BF16),
            pltpu.VMEM((2 * WINDOW, 8 * WINDOW), F32),
            pltpu.VMEM((2 * WINDOW, 8 * WINDOW), F32),
            pltpu.VMEM((2 * WINDOW, 8 * WINDOW), BF16),
            pltpu.VMEM((2 * WINDOW, 8 * WINDOW), BF16),
        ],
        compiler_params=pltpu.CompilerParams(
            dimension_semantics=("arbitrary", "arbitrary"), vmem_limit_bytes=VMEM_LIMIT),
        name="swa_attn_ln",
    )(x, wqkv, bias, sink, wo_t, ln_g.reshape(1, D), ln_b.reshape(1, D))


def kernel(x, rec_w_in, rec_conv_w, rec_conv_b, rec_w_r, rec_b_r, rec_w_i, rec_b_i, rec_lambda, rec_w_out,
           att_w_qkv, att_sinks, att_w_o, moe_w_group, moe_b_group, moe_w_expert, moe_b_expert,
           moe_w1, moe_w3, moe_w2, ln_g, ln_b):
    for layer in range(DEPTH):
        j = layer // 2
        if layer % 2 == 0:
            x, xpk = _rglru_layer(x, rec_w_in[j], rec_conv_w[j], rec_conv_b[j], rec_w_r[j], rec_b_r[j],
                                  rec_w_i[j], rec_b_i[j], rec_lambda[j], rec_w_out[j],
                                  ln_g[layer, 0], ln_b[layer, 0])
        else:
            x, xpk = _attn_layer(x, att_w_qkv[j], att_sinks[j], att_w_o[j], ln_g[layer, 0], ln_b[layer, 0])
        x = _hier_moe_ln(x, xpk, moe_w_group[layer], moe_b_group[layer], moe_w_expert[layer],
                         moe_b_expert[layer], moe_w1, moe_w3, moe_w2, layer,
                         ln_g[layer, 1], ln_b[layer, 1])
    return x
```

```python
import functools

import jax
import jax.numpy as jnp
import numpy as np
from jax import lax
from jax.experimental import pallas as pl
from jax.experimental.pallas import tpu as pltpu
from jax.experimental.pallas import tpu_sc as plsc

F32 = jnp.float32
BF16 = jnp.bfloat16
U32 = jnp.uint32

D_MODEL = 1024
DEPTH = 2
D_RNN = 1280
LRU_BLOCKS = 16
LRU_BLOCK_W = D_RNN // LRU_BLOCKS
CONV_W = 4
LRU_C = 8.0
N_HEADS = 16
N_KV_HEADS = 4
HEAD_DIM = 64
WINDOW = 128
Q_DIM = N_HEADS * HEAD_DIM
KV_DIM = N_KV_HEADS * HEAD_DIM
N_GROUPS = 4
EXPERTS_PER_GROUP = 8
N_EXPERTS = N_GROUPS * EXPERTS_PER_GROUP
TOP_K = 2
D_EXPERT = 512
ALPHA = (2 * DEPTH) ** 0.25
LN_EPS = 1e-5
LOG2E = 1.4426950408889634

LANES = 128
SUBLANES = 8
VMEM_LIMIT = 56 * 1024 * 1024

REC_TS = 256
REC_GROUPS = REC_TS // SUBLANES
GATE_TILE = 256
GATE_WIN = 512
GATE_WIN_STARTS = (0, 128, 384, 640, 768)
N_GATE_TILES = D_RNN // GATE_TILE

ROUTE_T = 512
ROUTE_ROWS = 40

MOE_BM = 512
MOE_SUB = 256

ATT_TQ = 512
ATT_NB = ATT_TQ // WINDOW

COMB_T = 512

D_HALF = D_MODEL // 2

SC_CORES = 2
SC_SUBCORES = 16
SC_WORKERS = SC_CORES * SC_SUBCORES
SC_WIN = 64


def _layer_norm_rows(z, g, b):
    mu = jnp.mean(z, axis=-1, keepdims=True)
    zc = z - mu
    var = jnp.mean(zc * zc, axis=-1, keepdims=True)
    return zc * lax.rsqrt(var + LN_EPS) * g + b


def _pack_bf16_pair(a, b):
    ua = lax.bitcast_convert_type(a.astype(BF16).astype(F32), U32)
    ub = lax.bitcast_convert_type(b.astype(BF16).astype(F32), U32)
    return (ua >> 16) | (ub & jnp.uint32(0xFFFF0000))


def _unpack_bf16_pair(w):
    a = lax.bitcast_convert_type(w << 16, F32)
    b = lax.bitcast_convert_type(w & jnp.uint32(0xFFFF0000), F32)
    return a, b


def _pack_row_halves(x):
    return _pack_bf16_pair(x[:, :D_HALF], x[:, D_HALF:])


def _rglru_kernel(x_ref, perm_ref, perm_t_ref, w_in_ref, convw_ref, convb_ref, wg_ref, br_ref, bi_ref, lam_ref,
                  w_out_ref, g_ref, b_ref, o_ref, opk_ref, xr_ext, tail_sc, a_sc, u_sc, h_carry):
    s = pl.program_id(1)
    ts = REC_TS
    halo = (CONV_W - 1) * SUBLANES

    @pl.when(s == 0)
    def _():
        tail_sc[...] = jnp.zeros((halo, D_RNN), F32)
        h_carry[...] = jnp.zeros((1, D_RNN), F32)

    x = x_ref[0]
    xp = jnp.dot(perm_ref[...], x.astype(BF16), preferred_element_type=F32).astype(BF16)
    proj = jnp.dot(xp, w_in_ref[...], preferred_element_type=F32)
    gate = proj[:, :D_RNN]
    xr = proj[:, D_RNN:]
    row = lax.broadcasted_iota(jnp.int32, (SUBLANES, D_RNN), 0)
    for k in range(CONV_W - 1):
        r0 = ts - halo + k * SUBLANES
        cur = xr[r0:r0 + SUBLANES, :]
        prev = tail_sc[k * SUBLANES:(k + 1) * SUBLANES, :]
        xr_ext[k * SUBLANES:(k + 1) * SUBLANES, :] = jnp.where(
            row == 0, pltpu.roll(prev, 1, axis=0), pltpu.roll(cur, 1, axis=0))
    tail_sc[...] = xr[ts - halo:, :]
    xr_ext[halo:halo + ts, :] = xr
    xc = convb_ref[...] + convw_ref[CONV_W - 1:CONV_W, :] * xr
    for k in range(CONV_W - 1):
        xc = xc + convw_ref[k:k + 1, :] * xr_ext[k * SUBLANES:k * SUBLANES + ts, :]

    xcb = xc.astype(BF16)
    nlam = -lam_ref[...]
    sp = jnp.maximum(nlam, 0.0) + jnp.log1p(jnp.exp(-jnp.abs(nlam)))
    for j in range(N_GATE_TILES):
        ws = GATE_WIN_STARTS[j]
        cs = j * GATE_TILE
        pre = jnp.dot(xcb[:, ws:ws + GATE_WIN], wg_ref[j], preferred_element_type=F32)
        r = jax.nn.sigmoid(pre[:, :GATE_TILE] + br_ref[:, cs:cs + GATE_TILE])
        i = jax.nn.sigmoid(pre[:, GATE_TILE:] + bi_ref[:, cs:cs + GATE_TILE])
        log_a = (-LRU_C) * r * sp[:, cs:cs + GATE_TILE]
        a = jnp.exp(log_a)
        u = jnp.sqrt(1.0 - a * a) * (i * xc[:, cs:cs + GATE_TILE])
        a_sc[:, cs:cs + GATE_TILE] = a
        u_sc[:, cs:cs + GATE_TILE] = u

    def scan_body(gidx, carry):
        h, prod = carry
        r0 = pl.multiple_of(gidx * SUBLANES, SUBLANES)
        a8 = a_sc[pl.ds(r0, SUBLANES), :]
        h = a8 * h + u_sc[pl.ds(r0, SUBLANES), :]
        prod = a8 * prod
        u_sc[pl.ds(r0, SUBLANES), :] = h
        a_sc[pl.ds(r0, SUBLANES), :] = prod
        return h, prod

    seg_h, seg_a = lax.fori_loop(
        0, REC_GROUPS, scan_body,
        (jnp.zeros((SUBLANES, D_RNN), F32), jnp.ones((SUBLANES, D_RNN), F32)), unroll=2)
    for d in (1, 2, 4):
        keep = row >= d
        a_sh = jnp.where(keep, pltpu.roll(seg_a, d, axis=0), 1.0)
        h_sh = jnp.where(keep, pltpu.roll(seg_h, d, axis=0), 0.0)
        seg_h = seg_a * h_sh + seg_h
        seg_a = seg_a * a_sh
    h_in = h_carry[...]
    after = seg_a * h_in + seg_h
    enter = jnp.where(row == 0, h_in, pltpu.roll(after, 1, axis=0))
    h_carry[...] = after[SUBLANES - 1:SUBLANES, :]
    hs = (u_sc[...].reshape(REC_GROUPS, SUBLANES, D_RNN)
          + a_sc[...].reshape(REC_GROUPS, SUBLANES, D_RNN) * enter[None]).reshape(ts, D_RNN)

    y = hs * jax.nn.gelu(gate)
    y_t = jnp.dot(perm_t_ref[...], y.astype(BF16), preferred_element_type=F32).astype(BF16)
    out = jnp.dot(y_t, w_out_ref[...], preferred_element_type=F32)
    z = ALPHA * x + out
    xn = _layer_norm_rows(z, g_ref[...], b_ref[...])
    o_ref[0] = xn
    opk_ref[0] = _pack_row_halves(xn)


def _band_gate_weights(w_r, w_i):
    eye = jnp.eye(LRU_BLOCKS, dtype=F32)

    def dense(w):
        return (w[:, :, None, :] * eye[:, None, :, None]).reshape(D_RNN, D_RNN)

    wr, wi = dense(w_r), dense(w_i)
    tiles = []
    for j in range(N_GATE_TILES):
        ws = GATE_WIN_STARTS[j]
        cs = j * GATE_TILE
        lo_blk = cs // LRU_BLOCK_W
        hi_blk = (cs + GATE_TILE - 1) // LRU_BLOCK_W
        assert ws <= lo_blk * LRU_BLOCK_W and (hi_blk + 1) * LRU_BLOCK_W <= ws + GATE_WIN
        tiles.append(jnp.concatenate([wr[ws:ws + GATE_WIN, cs:cs + GATE_TILE],
                                      wi[ws:ws + GATE_WIN, cs:cs + GATE_TILE]], axis=1))
    return jnp.stack(tiles).astype(BF16)


def _rglru_layer(x, w_in, conv_w, conv_b, w_r, b_r, w_i, b_i, lam, w_out, ln_g, ln_b):
    B, S, D = x.shape
    wg = _band_gate_weights(w_r, w_i)
    rho = np.arange(REC_TS)
    perm_np = np.zeros((REC_TS, REC_TS), np.float32)
    perm_np[rho, (rho % SUBLANES) * REC_GROUPS + rho // SUBLANES] = 1.0
    perm = jnp.asarray(perm_np, BF16)
    perm_t = jnp.asarray(perm_np.T, BF16)
    row = lambda v: v.reshape(1, -1)
    const = lambda shape: pl.BlockSpec(shape, lambda b, s: (0,) * len(shape))
    tile = lambda w: pl.BlockSpec((1, REC_TS, w), lambda b, s: (b, s, 0))
    halo = (CONV_W - 1) * SUBLANES
    return pl.pallas_call(
        _rglru_kernel,
        grid=(B, S // REC_TS),
        in_specs=[
            tile(D),
            const((REC_TS, REC_TS)),
            const((REC_TS, REC_TS)),
            const((D, 2 * D_RNN)),
            const((CONV_W, D_RNN)),
            const((1, D_RNN)),
            const((N_GATE_TILES, GATE_WIN, 2 * GATE_TILE)),
            const((1, D_RNN)),
            const((1, D_RNN)),
            const((1, D_RNN)),
            const((D_RNN, D)),
            const((1, D)),
            const((1, D)),
        ],
        out_specs=[tile(D), tile(D_HALF)],
        out_shape=[jax.ShapeDtypeStruct((B, S, D), F32), jax.ShapeDtypeStruct((B, S, D_HALF), U32)],
        scratch_shapes=[
            pltpu.VMEM((halo + REC_TS, D_RNN), F32),
            pltpu.VMEM((halo, D_RNN), F32),
            pltpu.VMEM((REC_TS, D_RNN), F32),
            pltpu.VMEM((REC_TS, D_RNN), F32),
            pltpu.VMEM((1, D_RNN), F32),
        ],
        compiler_params=pltpu.CompilerParams(
            dimension_semantics=("arbitrary", "arbitrary"), vmem_limit_bytes=VMEM_LIMIT),
        name="rglru_ln",
    )(x, perm, perm_t, w_in.astype(BF16), conv_w, row(conv_b), wg, row(b_r), row(b_i), row(lam), w_out.astype(BF16),
      row(ln_g), row(ln_b))


def _router_kernel(x_ref, whi_ref, bias_ref, tri_ref, idx_ref, gate_ref, cnt_ref, base_sc):
    step = pl.program_id(0)
    tr = ROUTE_T

    @pl.when(step == 0)
    def _():
        base_sc[...] = jnp.zeros((N_EXPERTS, 1), F32)

    x = x_ref[...]
    xhi = x.astype(BF16)
    xlo = (x - xhi.astype(F32)).astype(BF16)
    nt = (((1,), (1,)), ((), ()))
    both = lax.dot_general(whi_ref[...], xhi, nt, preferred_element_type=F32)
    logits = (both[:ROUTE_ROWS] + both[ROUTE_ROWS:]
              + lax.dot_general(whi_ref[:ROUTE_ROWS], xlo, nt, preferred_element_type=F32))
    logits = logits + bias_ref[...]

    row8 = lax.broadcasted_iota(jnp.int32, (SUBLANES, tr), 0).astype(F32)
    neg_inf = -jnp.inf
    g = jnp.where(row8 < N_GROUPS, logits[N_EXPERTS:N_EXPERTS + SUBLANES, :], neg_inf)
    gmax = jnp.max(g, axis=0, keepdims=True)
    gidx = jnp.min(jnp.where(g == gmax, row8, SUBLANES), axis=0, keepdims=True)
    g_gate = 1.0 / jnp.sum(jnp.exp(g - gmax), axis=0, keepdims=True)

    esel = logits[0:EXPERTS_PER_GROUP, :]
    for grp in range(1, N_GROUPS):
        esel = jnp.where(gidx == grp, logits[grp * EXPERTS_PER_GROUP:(grp + 1) * EXPERTS_PER_GROUP, :], esel)
    v1 = jnp.max(esel, axis=0, keepdims=True)
    i1 = jnp.min(jnp.where(esel == v1, row8, SUBLANES), axis=0, keepdims=True)
    esel2 = jnp.where(row8 == i1, neg_inf, esel)
    v2 = jnp.max(esel2, axis=0, keepdims=True)
    i2 = jnp.min(jnp.where(esel2 == v2, row8, SUBLANES), axis=0, keepdims=True)
    e21 = jnp.exp(v2 - v1)
    inv = 1.0 / (1.0 + e21)
    gate1 = inv * g_gate
    gate2 = e21 * inv * g_gate
    e1 = gidx * EXPERTS_PER_GROUP + i1
    e2 = gidx * EXPERTS_PER_GROUP + i2

    rowe = lax.broadcasted_iota(jnp.int32, (N_EXPERTS, tr), 0).astype(F32)
    hit1 = rowe == e1
    hit2 = rowe == e2
    member = jnp.where(hit1, 1.0, jnp.where(hit2, 1.0, 0.0))
    before = jnp.dot(member.astype(BF16), tri_ref[...], preferred_element_type=F32) + base_sc[...]
    rank1 = jnp.sum(jnp.where(hit1, before, 0.0), axis=0, keepdims=True)
    rank2 = jnp.sum(jnp.where(hit2, before, 0.0), axis=0, keepdims=True)
    base_sc[...] = base_sc[...] + jnp.sum(member, axis=1, keepdims=True)

    zi = jnp.zeros((1, tr), jnp.int32)
    idx_ref[...] = jnp.concatenate(
        [e1.astype(jnp.int32), e2.astype(jnp.int32), rank1.astype(jnp.int32), rank2.astype(jnp.int32),
         zi, zi, zi, zi], axis=0)
    zf = jnp.zeros((1, tr), F32)
    gate_ref[...] = jnp.concatenate([gate1, gate2, zf, zf, zf, zf, zf, zf], axis=0)
    cnt_ref[...] = jnp.broadcast_to(base_sc[...], (N_EXPERTS, LANES)).astype(jnp.int32)


def _router(xf, w_rg, b_rg, w_re, b_re):
    T, D = xf.shape
    pad_rows = ROUTE_ROWS - N_EXPERTS - N_GROUPS
    w = jnp.concatenate([w_re.T, w_rg.T, jnp.zeros((pad_rows, D), F32)], axis=0)
    whi = w.astype(BF16)
    wlo = (w - whi.astype(F32)).astype(BF16)
    w_split = jnp.concatenate([whi, wlo], axis=0)
    bias = jnp.concatenate([b_re, b_rg, jnp.zeros((pad_rows,), F32)]).reshape(ROUTE_ROWS, 1)
    tri = jnp.asarray(np.triu(np.ones((ROUTE_T, ROUTE_T), np.float32), 1), BF16)
    const = lambda shape: pl.BlockSpec(shape, lambda i: (0,) * len(shape))
    return pl.pallas_call(
        _router_kernel,
        grid=(T // ROUTE_T,),
        in_specs=[
            pl.BlockSpec((ROUTE_T, D), lambda i: (i, 0)),
            const((2 * ROUTE_ROWS, D)),
            const((ROUTE_ROWS, 1)),
            const((ROUTE_T, ROUTE_T)),
        ],
        out_specs=[
            pl.BlockSpec((SUBLANES, ROUTE_T), lambda i: (0, i)),
            pl.BlockSpec((SUBLANES, ROUTE_T), lambda i: (0, i)),
            const((N_EXPERTS, LANES)),
        ],
        out_shape=[
            jax.ShapeDtypeStruct((SUBLANES, T), jnp.int32),
            jax.ShapeDtypeStruct((SUBLANES, T), F32),
            jax.ShapeDtypeStruct((N_EXPERTS, LANES), jnp.int32),
        ],
        scratch_shapes=[pltpu.VMEM((N_EXPERTS, 1), F32)],
        compiler_params=pltpu.CompilerParams(
            dimension_semantics=("arbitrary",), vmem_limit_bytes=VMEM_LIMIT),
        name="router",
    )(xf, w_split, bias, tri)


def _sc_mesh():
    return plsc.VectorSubcoreMesh(core_axis_name="c", subcore_axis_name="s",
                                  num_cores=SC_CORES, num_subcores=SC_SUBCORES)


def _sc_worker_id():
    return lax.axis_index("s") * SC_CORES + lax.axis_index("c")


def _sc_scratch(n_win, width):
    return [
        pltpu.VMEM((n_win, SC_WIN), jnp.int32),
        pltpu.VMEM((n_win, SC_WIN), jnp.int32),
        pltpu.VMEM((2, SC_WIN, width), U32),
        pltpu.SemaphoreType.DMA((2,)),
        pltpu.SemaphoreType.DMA((2,)),
    ]


def _sc_dispatch(rows, idx1, idx2, n_rows):
    _, width = rows.shape
    _, n_win, _ = idx1.shape

    @functools.partial(
        pl.kernel, mesh=_sc_mesh(), out_type=jax.ShapeDtypeStruct((n_rows, width), rows.dtype),
        scratch_types=_sc_scratch(n_win, width), name="sc_dispatch")
    def run(rows_hbm, i1_hbm, i2_hbm, o_hbm, i1_v, i2_v, buf, rsem, wsem):
        wid = _sc_worker_id()
        base = wid * (n_win * SC_WIN)
        pltpu.sync_copy(i1_hbm.at[wid], i1_v)
        pltpu.sync_copy(i2_hbm.at[wid], i2_v)

        def read(j):
            return pltpu.async_copy(rows_hbm.at[pl.ds(base + j * SC_WIN, SC_WIN)], buf.at[j % 2], rsem.at[j % 2])

        reads = {0: read(0)}
        writes = {}
        for j in range(n_win):
            if j + 1 < n_win:
                for d in writes.pop(j - 1, ()):
                    d.wait()
                reads[j + 1] = read(j + 1)
            reads.pop(j).wait()
            writes[j] = (pltpu.async_copy(buf.at[j % 2], o_hbm.at[i1_v.at[j]], wsem.at[j % 2]),
                         pltpu.async_copy(buf.at[j % 2], o_hbm.at[i2_v.at[j]], wsem.at[j % 2]))
        for j in sorted(writes):
            for d in writes[j]:
                d.wait()

    return run(rows, idx1, idx2)


def _sc_gather_pair(table, idx1, idx2):
    _, width = table.shape
    _, n_win, _ = idx1.shape
    n_tok = SC_WORKERS * n_win * SC_WIN
    out_t = jax.ShapeDtypeStruct((n_tok, width), table.dtype)

    @functools.partial(
        pl.kernel, mesh=_sc_mesh(), out_type=(out_t, out_t),
        scratch_types=_sc_scratch(n_win, width), name="sc_combine_gather")
    def run(table_hbm, i1_hbm, i2_hbm, o1_hbm, o2_hbm, i1_v, i2_v, buf, gsem, wsem):
        wid = _sc_worker_id()
        base = wid * (n_win * SC_WIN)
        pltpu.sync_copy(i1_hbm.at[wid], i1_v)
        pltpu.sync_copy(i2_hbm.at[wid], i2_v)
        work = [(i1_v, o1_hbm, j) for j in range(n_win)] + [(i2_v, o2_hbm, j) for j in range(n_win)]

        def gather(t):
            iv, _, j = work[t]
            return pltpu.async_copy(table_hbm.at[iv.at[j]], buf.at[t % 2], gsem.at[t % 2])

        def put(t):
            _, oh, j = work[t]
            return pltpu.async_copy(buf.at[t % 2], oh.at[pl.ds(base + j * SC_WIN, SC_WIN)], wsem.at[t % 2])

        gathers = {0: gather(0)}
        puts = {}
        for t in range(len(work)):
            if t + 1 < len(work):
                if t - 1 in puts:
                    puts.pop(t - 1).wait()
                gathers[t + 1] = gather(t + 1)
            gathers.pop(t).wait()
            puts[t] = put(t)
        for t in sorted(puts):
            puts[t].wait()

    return run(table, idx1, idx2)


def _moe_kernel(layer, be_ref, slot_ref, nxt_ref, nused_ref, x_ref, w1_hbm, w3_hbm, w2_hbm, o_ref,
                w1_st, w3_st, w2_st, w1_sc, w3_sc, w2_sc, sems):
    i = pl.program_id(0)
    expert = be_ref[i]
    slot = slot_ref[i]
    new_expert = jnp.logical_or(i == 0, expert != be_ref[jnp.maximum(i - 1, 0)])

    def weight_copies(e, sl):
        return [pltpu.make_async_copy(hbm.at[layer, e], stage.at[sl], sems.at[k, sl])
                for k, (hbm, stage) in enumerate(((w1_hbm, w1_st), (w3_hbm, w3_st), (w2_hbm, w2_st)))]

    @pl.when(i == 0)
    def _():
        for cp in weight_copies(expert, slot):
            cp.start()

    @pl.when(new_expert)
    def _():
        for cp in weight_copies(expert, slot):
            cp.wait()
        w1_sc[...] = w1_st[slot].astype(BF16)
        w3_sc[...] = w3_st[slot].astype(BF16)
        w2_sc[...] = w2_st[slot].astype(BF16)
        nxt = nxt_ref[i]

        @pl.when(nxt >= 0)
        def _():
            for cp in weight_copies(nxt, 1 - slot):
                cp.start()

    @pl.when(i < nused_ref[0])
    def _():
        def up(rows):
            xa, xb = _unpack_bf16_pair(x_ref[rows, :])
            xa = xa.astype(BF16)
            xb = xb.astype(BF16)
            h1 = (jnp.dot(xa, w1_sc[:D_HALF], preferred_element_type=F32)
                  + jnp.dot(xb, w1_sc[D_HALF:], preferred_element_type=F32))
            h3 = (jnp.dot(xa, w3_sc[:D_HALF], preferred_element_type=F32)
                  + jnp.dot(xb, w3_sc[D_HALF:], preferred_element_type=F32))
            return h1, h3

        def down(rows, h1, h3):
            hdn = (jax.nn.silu(h1) * h3).astype(BF16)
            y = jnp.dot(hdn, w2_sc[...], preferred_element_type=F32)
            o_ref[rows, :] = _pack_row_halves(y)

        subs = [slice(k * MOE_SUB, (k + 1) * MOE_SUB) for k in range(MOE_BM // MOE_SUB)]
        ups = [up(rows) for rows in subs]
        for rows, (h1, h3) in zip(subs, ups):
            down(rows, h1, h3)

    @pl.when(i >= nused_ref[0])
    def _():
        o_ref[...] = jnp.zeros(o_ref.shape, o_ref.dtype)


def _moe_blocks(xbuf, block_e, n_used, w1, w3, w2, layer):
    n_rows, _ = xbuf.shape
    D = D_MODEL
    n_blocks = n_rows // MOE_BM
    pos = jnp.arange(n_blocks, dtype=jnp.int32)
    is_new = jnp.concatenate([jnp.ones((1,), bool), block_e[1:] != block_e[:-1]])
    slot = ((jnp.cumsum(is_new.astype(jnp.int32)) - 1) % 2).astype(jnp.int32)
    change_pos = jnp.where(is_new, pos, n_blocks)
    next_change = jnp.concatenate([lax.cummin(change_pos, reverse=True)[1:], jnp.full((1,), n_blocks, jnp.int32)])
    nxt = jnp.where(next_change < n_blocks, block_e[jnp.minimum(next_change, n_blocks - 1)], -1).astype(jnp.int32)
    rows = lambda i, be, sl, nx, nu: (i, 0)
    grid_spec = pltpu.PrefetchScalarGridSpec(
        num_scalar_prefetch=4,
        grid=(n_blocks,),
        in_specs=[
            pl.BlockSpec((MOE_BM, D_HALF), rows),
            pl.BlockSpec(memory_space=pl.ANY),
            pl.BlockSpec(memory_space=pl.ANY),
            pl.BlockSpec(memory_space=pl.ANY),
        ],
        out_specs=pl.BlockSpec((MOE_BM, D_HALF), rows),
        scratch_shapes=[
            pltpu.VMEM((2, D, D_EXPERT), F32),
            pltpu.VMEM((2, D, D_EXPERT), F32),
            pltpu.VMEM((2, D_EXPERT, D), F32),
            pltpu.VMEM((D, D_EXPERT), BF16),
            pltpu.VMEM((D, D_EXPERT), BF16),
            pltpu.VMEM((D_EXPERT, D), BF16),
            pltpu.SemaphoreType.DMA((3, 2)),
        ],
    )
    return pl.pallas_call(
        functools.partial(_moe_kernel, layer),
        grid_spec=grid_spec,
        out_shape=jax.ShapeDtypeStruct((n_rows, D_HALF), U32),
        compiler_params=pltpu.CompilerParams(
            dimension_semantics=("arbitrary",), vmem_limit_bytes=VMEM_LIMIT),
        name="moe_experts",
    )(block_e, slot, nxt, n_used, xbuf, w1, w3, w2)


def _combine_kernel(x_ref, y1_ref, y2_ref, gates_ref, g_ref, b_ref, o_ref):
    x = x_ref[...]
    gates = gates_ref[...]
    g1 = gates[:, 0:1]
    g2 = gates[:, 1:2]
    a1, b1 = _unpack_bf16_pair(y1_ref[...])
    a2, b2 = _unpack_bf16_pair(y2_ref[...])
    f = jnp.concatenate([g1 * a1 + g2 * a2, g1 * b1 + g2 * b2], axis=1)
    z = ALPHA * x + f
    o_ref[...] = _layer_norm_rows(z, g_ref[...], b_ref[...])


def _combine_ln(xf, y1, y2, gates, ln_g, ln_b):
    T, D = xf.shape
    const = lambda shape: pl.BlockSpec(shape, lambda i: (0,) * len(shape))
    rows = lambda w: pl.BlockSpec((COMB_T, w), lambda i: (i, 0))
    return pl.pallas_call(
        _combine_kernel,
        grid=(T // COMB_T,),
        in_specs=[rows(D), rows(D_HALF), rows(D_HALF), rows(TOP_K), const((1, D)), const((1, D))],
        out_specs=rows(D),
        out_shape=jax.ShapeDtypeStruct((T, D), F32),
        compiler_params=pltpu.CompilerParams(
            dimension_semantics=("arbitrary",), vmem_limit_bytes=VMEM_LIMIT),
        name="moe_combine_ln",
    )(xf, y1, y2, gates, ln_g.reshape(1, D), ln_b.reshape(1, D))


def _hier_moe_ln(x, xpk, w_rg, b_rg, w_re, b_re, w1, w3, w2, layer, ln_g, ln_b):
    B, S, D = x.shape
    T = B * S
    xf = x.reshape(T, D)
    idx, gates, cnt = _router(xf, w_rg, b_rg, w_re, b_re)
    counts = cnt[:, 0]
    padded = ((counts + MOE_BM - 1) // MOE_BM) * MOE_BM
    pends = jnp.cumsum(padded)
    pstarts = pends - padded
    experts = jnp.arange(N_EXPERTS, dtype=jnp.int32)[:, None]

    def dest(e_row, rank_row):
        return jnp.sum(jnp.where(e_row[None, :] == experts, pstarts[:, None], 0), axis=0) + rank_row

    n_win = T // (SC_WORKERS * SC_WIN)
    dest1 = dest(idx[0], idx[2]).reshape(SC_WORKERS, n_win, SC_WIN)
    dest2 = dest(idx[1], idx[3]).reshape(SC_WORKERS, n_win, SC_WIN)
    n_blocks = -(-(T * TOP_K + N_EXPERTS * (MOE_BM - 1)) // MOE_BM)
    n_rows = n_blocks * MOE_BM
    block_start = jnp.arange(n_blocks, dtype=jnp.int32) * MOE_BM
    block_e = jnp.minimum(jnp.sum(block_start[:, None] >= pends[None, :], axis=1), N_EXPERTS - 1).astype(jnp.int32)
    n_used = (pends[-1] // MOE_BM).astype(jnp.int32).reshape(1)
    xbuf = _sc_dispatch(xpk.reshape(T, D_HALF), dest1, dest2, n_rows)
    ybuf = _moe_blocks(xbuf, block_e, n_used, w1, w3, w2, layer)
    y1, y2 = _sc_gather_pair(ybuf, dest1, dest2)
    out = _combine_ln(xf, y1, y2, gates[:TOP_K].T, ln_g, ln_b)
    return out.reshape(B, S, D)


def _att_head_order():
    order = []
    for p in range(N_HEADS // 2):
        jj, m = divmod(p, 4)
        order += [8 * jj + m, 8 * jj + 4 + m]
    return order


ATT_HEAD_ORDER = _att_head_order()


def _attn_kernel(x_ref, wqkv_ref, bias_ref, sink_ref, wo_ref, g_ref, b_ref, o_ref, opk_ref, kv_ext, o_sc,
                 s_sc0, s_sc1, p_sc0, p_sc1):
    s = pl.program_id(1)
    tq = ATT_TQ
    s_bufs = (s_sc0, s_sc1)
    p_bufs = (p_sc0, p_sc1)

    @pl.when(s == 0)
    def _():
        kv_ext[0:WINDOW, :] = jnp.zeros((WINDOW, 2 * KV_DIM), BF16)

    x = x_ref[0]
    qkv = jnp.dot(x.astype(BF16), wqkv_ref[...], preferred_element_type=F32)
    q = (qkv[:, :Q_DIM] * (HEAD_DIM ** -0.5 * LOG2E)).astype(BF16)
    kv_ext[WINDOW:WINDOW + tq, :] = qkv[:, Q_DIM:].astype(BF16)

    lane = lax.broadcasted_iota(jnp.int32, (WINDOW, LANES), 1)
    low = lane < HEAD_DIM
    sub = lax.broadcasted_iota(jnp.int32, (LANES, WINDOW), 0)
    top = sub < HEAD_DIM
    first = jnp.where(s == 0, 1, 0)
    nt = (((1,), (1,)), ((), ()))
    zero = jnp.zeros((), BF16)

    tiles = [(n, j) for n in range(ATT_NB) for j in range(2)]

    def scores(t):
        n, j = tiles[t]
        r0 = n * WINDOW
        k_tile = kv_ext[r0:r0 + 2 * WINDOW, j * LANES:(j + 1) * LANES]
        parts = []
        for m in range(4):
            p = 4 * j + m
            qt = q[r0:r0 + WINDOW, p * LANES:(p + 1) * LANES]
            parts.append(jnp.where(low, qt, zero))
            parts.append(jnp.where(low, zero, qt))
        qs = jnp.concatenate(parts, axis=0)
        bias_sel = first if n == 0 else 0
        s_bufs[t % 2][...] = (lax.dot_general(k_tile, qs, nt, preferred_element_type=F32)
                              + bias_ref[bias_sel, j])

    def softmax_pv(t):
        n, j = tiles[t]
        r0 = n * WINDOW
        s_sc = s_bufs[t % 2]
        p_sc = p_bufs[t % 2]
        inv_l = []
        for h in range(8):
            hc = slice(h * WINDOW, (h + 1) * WINDOW)
            sink = sink_ref[8 * j + h] * LOG2E
            mx = jnp.maximum(jnp.max(s_sc[:, hc], axis=0, keepdims=True), sink)
            pr = jnp.exp2(s_sc[:, hc] - mx)
            p_sc[:, hc] = pr.astype(BF16)
            inv_l.append(1.0 / (jnp.sum(pr, axis=0, keepdims=True) + jnp.exp2(sink - mx)))
        v_tile = kv_ext[r0:r0 + 2 * WINDOW, KV_DIM + j * LANES:KV_DIM + (j + 1) * LANES]
        v_t = v_tile.astype(F32).T.astype(BF16)
        ov = jnp.dot(v_t, p_sc[...], preferred_element_type=F32)
        for m in range(4):
            p = 4 * j + m
            o_even = ov[:, (2 * m) * WINDOW:(2 * m + 1) * WINDOW] * inv_l[2 * m]
            o_odd = ov[:, (2 * m + 1) * WINDOW:(2 * m + 2) * WINDOW] * inv_l[2 * m + 1]
            o_sc[p * LANES:(p + 1) * LANES, r0:r0 + WINDOW] = jnp.where(top, o_even, o_odd).astype(BF16)

    scores(0)
    for t in range(len(tiles)):
        if t + 1 < len(tiles):
            scores(t + 1)
        softmax_pv(t)
    kv_ext[0:WINDOW, :] = kv_ext[tq:tq + WINDOW, :]

    halves = [slice(hf * (tq // 2), (hf + 1) * (tq // 2)) for hf in range(2)]
    outs_t = [jnp.dot(wo_ref[...], o_sc[:, rows], preferred_element_type=F32) for rows in halves]
    for rows, out_t in zip(halves, outs_t):
        z = ALPHA * x[rows, :] + out_t.T
        xn = _layer_norm_rows(z, g_ref[...], b_ref[...])
        o_ref[0, rows, :] = xn
        opk_ref[0, rows, :] = _pack_row_halves(xn)


def _attn_bias():
    qi = np.arange(WINDOW)[:, None]
    sj = np.arange(2 * WINDOW)[None, :]
    dist = qi - sj + WINDOW
    valid = (dist >= 0) & (dist < WINDOW)
    slopes = 2.0 ** (-8.0 * np.arange(1, N_HEADS + 1, dtype=np.float32) / N_HEADS)
    slopes = slopes.astype(np.float32)[ATT_HEAD_ORDER]
    sb = -(slopes[:, None, None] * dist.astype(np.float32)[None])
    later = np.where(valid[None], sb, -np.inf)
    first = np.where((valid & (sj >= WINDOW))[None], sb, -np.inf)
    bias = np.stack([later, first]).astype(np.float32) * np.float32(LOG2E)
    bias = bias.reshape(2, 2, 8, WINDOW, 2 * WINDOW).transpose(0, 1, 4, 2, 3).reshape(2, 2, 2 * WINDOW, 8 * WINDOW)
    return jnp.asarray(np.ascontiguousarray(bias))


def _attn_layer(x, w_qkv, sinks, w_o, ln_g, ln_b):
    B, S, D = x.shape
    cols = np.concatenate([np.arange(h * HEAD_DIM, (h + 1) * HEAD_DIM) for h in ATT_HEAD_ORDER])
    wqkv = jnp.concatenate([w_qkv[:, cols], w_qkv[:, Q_DIM:]], axis=1).astype(BF16)
    wo_t = w_o[cols, :].T.astype(BF16)
    sink = sinks[np.asarray(ATT_HEAD_ORDER)].reshape(N_HEADS, 1, 1)
    bias = _attn_bias()
    const = lambda shape: pl.BlockSpec(shape, lambda b, s: (0,) * len(shape))
    tile = lambda w: pl.BlockSpec((1, ATT_TQ, w), lambda b, s: (b, s, 0))
    return pl.pallas_call(
        _attn_kernel,
        grid=(B, S // ATT_TQ),
        in_specs=[
            tile(D),
            const((D, Q_DIM + 2 * KV_DIM)),
            const((2, 2, 2 * WINDOW, 8 * WINDOW)),
            const((N_HEADS, 1, 1)),
            const((D, Q_DIM)),
            const((1, D)),
            const((1, D)),
        ],
        out_specs=[tile(D), tile(D_HALF)],
        out_shape=[jax.ShapeDtypeStruct((B, S, D), F32), jax.ShapeDtypeStruct((B, S, D_HALF), U32)],
        scratch_shapes=[
            pltpu.VMEM((ATT_TQ + WINDOW, 2 * KV_DIM), BF16),
            pltpu.VMEM((Q_DIM, ATT_TQ), BF16),
            pltpu.VMEM((2 * WINDOW, 8 * WINDOW), F32),
            pltpu.VMEM((2 * WINDOW, 8 * WINDOW), F32),
            pltpu.VMEM((2 * WINDOW, 8 * WINDOW), BF16),
            pltpu.VMEM((2 * WINDOW, 8 * WINDOW), BF16),
        ],
        compiler_params=pltpu.CompilerParams(
            dimension_semantics=("arbitrary", "arbitrary"), vmem_limit_bytes=VMEM_LIMIT),
        name="swa_attn_ln",
    )(x, wqkv, bias, sink, wo_t, ln_g.reshape(1, D), ln_b.reshape(1, D))


def kernel(x, rec_w_in, rec_conv_w, rec_conv_b, rec_w_r, rec_b_r, rec_w_i, rec_b_i, rec_lambda, rec_w_out,
           att_w_qkv, att_sinks, att_w_o, moe_w_group, moe_b_group, moe_w_expert, moe_b_expert,
           moe_w1, moe_w3, moe_w2, ln_g, ln_b):
    for layer in range(DEPTH):
        j = layer // 2
        if layer % 2 == 0:
            x, xpk = _rglru_layer(x, rec_w_in[j], rec_conv_w[j], rec_conv_b[j], rec_w_r[j], rec_b_r[j],
                                  rec_w_i[j], rec_b_i[j], rec_lambda[j], rec_w_out[j],
                                  ln_g[layer, 0], ln_b[layer, 0])
        else:
            x, xpk = _attn_layer(x, att_w_qkv[j], att_sinks[j], att_w_o[j], ln_g[layer, 0], ln_b[layer, 0])
        x = _hier_moe_ln(x, xpk, moe_w_group[layer], moe_b_group[layer], moe_w_expert[layer],
                         moe_b_expert[layer], moe_w1, moe_w3, moe_w2, layer,
                         ln_g[layer, 1], ln_b[layer, 1])
    return x
```

```python
import functools

import jax
import jax.numpy as jnp
import numpy as np
from jax import lax
from jax.experimental import pallas as pl
from jax.experimental.pallas import tpu as pltpu
from jax.experimental.pallas import tpu_sc as plsc

F32 = jnp.float32
BF16 = jnp.bfloat16
U32 = jnp.uint32

D_MODEL = 1024
DEPTH = 2
D_RNN = 1280
LRU_BLOCKS = 16
LRU_BLOCK_W = D_RNN // LRU_BLOCKS
CONV_W = 4
LRU_C = 8.0
N_HEADS = 16
N_KV_HEADS = 4
HEAD_DIM = 64
WINDOW = 128
Q_DIM = N_HEADS * HEAD_DIM
KV_DIM = N_KV_HEADS * HEAD_DIM
N_GROUPS = 4
EXPERTS_PER_GROUP = 8
N_EXPERTS = N_GROUPS * EXPERTS_PER_GROUP
TOP_K = 2
D_EXPERT = 512
ALPHA = (2 * DEPTH) ** 0.25
LN_EPS = 1e-5
LOG2E = 1.4426950408889634

LANES = 128
SUBLANES = 8
VMEM_LIMIT = 56 * 1024 * 1024

REC_TS = 256
REC_GROUPS = REC_TS // SUBLANES
REC_NSEQ = 2
GATE_TILE = 256
GATE_WIN = 512
GATE_WIN_STARTS = (0, 128, 384, 640, 768)
N_GATE_TILES = D_RNN // GATE_TILE

ROUTE_T = 512
ROUTE_ROWS = 40

MOE_BM = 512
MOE_SUB = 256

ATT_TQ = 512
ATT_NB = ATT_TQ // WINDOW

COMB_T = 512

D_HALF = D_MODEL // 2

SC_CORES = 2
SC_SUBCORES = 16
SC_WORKERS = SC_CORES * SC_SUBCORES
SC_WIN = 64


def _layer_norm_rows(z, g, b):
    mu = jnp.mean(z, axis=-1, keepdims=True)
    zc = z - mu
    var = jnp.mean(zc * zc, axis=-1, keepdims=True)
    return zc * lax.rsqrt(var + LN_EPS) * g + b


def _pack_bf16_pair(a, b):
    ua = lax.bitcast_convert_type(a.astype(BF16).astype(F32), U32)
    ub = lax.bitcast_convert_type(b.astype(BF16).astype(F32), U32)
    return (ua >> 16) | (ub & jnp.uint32(0xFFFF0000))


def _unpack_bf16_pair(w):
    a = lax.bitcast_convert_type(w << 16, F32)
    b = lax.bitcast_convert_type(w & jnp.uint32(0xFFFF0000), F32)
    return a, b


def _pack_row_halves(x):
    return _pack_bf16_pair(x[:, :D_HALF], x[:, D_HALF:])


def _rglru_kernel(x_ref, perm_ref, perm_t_ref, w_in_ref, convw_ref, convb_ref, wg_ref, br_ref, bi_ref, lam_ref,
                  w_out_ref, g_ref, b_ref, o_ref, opk_ref, xr_ext, tail_sc, a_sc, u_sc, h_carry):
    s = pl.program_id(1)
    ts = REC_TS
    halo = (CONV_W - 1) * SUBLANES
    seqs = range(REC_NSEQ)

    @pl.when(s == 0)
    def _():
        tail_sc[...] = jnp.zeros((REC_NSEQ, halo, D_RNN), F32)
        h_carry[...] = jnp.zeros((REC_NSEQ, 1, D_RNN), F32)

    row = lax.broadcasted_iota(jnp.int32, (SUBLANES, D_RNN), 0)
    nlam = -lam_ref[...]
    sp = jnp.maximum(nlam, 0.0) + jnp.log1p(jnp.exp(-jnp.abs(nlam)))

    def project(q):
        xp = jnp.dot(perm_ref[...], x_ref[q].astype(BF16), preferred_element_type=F32).astype(BF16)
        return jnp.dot(xp, w_in_ref[...], preferred_element_type=F32)

    def conv_gates(q, proj):
        xr = proj[:, D_RNN:]
        for k in range(CONV_W - 1):
            r0 = ts - halo + k * SUBLANES
            cur = xr[r0:r0 + SUBLANES, :]
            prev = tail_sc[q, k * SUBLANES:(k + 1) * SUBLANES, :]
            xr_ext[q, k * SUBLANES:(k + 1) * SUBLANES, :] = jnp.where(
                row == 0, pltpu.roll(prev, 1, axis=0), pltpu.roll(cur, 1, axis=0))
        tail_sc[q] = xr[ts - halo:, :]
        xr_ext[q, halo:halo + ts, :] = xr
        xc = convb_ref[...] + convw_ref[CONV_W - 1:CONV_W, :] * xr
        for k in range(CONV_W - 1):
            xc = xc + convw_ref[k:k + 1, :] * xr_ext[q, k * SUBLANES:k * SUBLANES + ts, :]
        xcb = xc.astype(BF16)
        pres = [jnp.dot(xcb[:, GATE_WIN_STARTS[j]:GATE_WIN_STARTS[j] + GATE_WIN], wg_ref[j],
                        preferred_element_type=F32) for j in range(N_GATE_TILES)]
        for j, pre in enumerate(pres):
            cs = j * GATE_TILE
            r = jax.nn.sigmoid(pre[:, :GATE_TILE] + br_ref[:, cs:cs + GATE_TILE])
            i = jax.nn.sigmoid(pre[:, GATE_TILE:] + bi_ref[:, cs:cs + GATE_TILE])
            log_a = (-LRU_C) * r * sp[:, cs:cs + GATE_TILE]
            a = jnp.exp(log_a)
            u = jnp.sqrt(1.0 - a * a) * (i * xc[:, cs:cs + GATE_TILE])
            a_sc[q, :, cs:cs + GATE_TILE] = a
            u_sc[q, :, cs:cs + GATE_TILE] = u

    projs = [project(q) for q in seqs]
    for q in seqs:
        conv_gates(q, projs[q])

    def scan_body(gidx, carry):
        r0 = pl.multiple_of(gidx * SUBLANES, SUBLANES)
        new = []
        for q in seqs:
            h, prod = carry[q]
            a8 = a_sc[q, pl.ds(r0, SUBLANES), :]
            h = a8 * h + u_sc[q, pl.ds(r0, SUBLANES), :]
            prod = a8 * prod
            u_sc[q, pl.ds(r0, SUBLANES), :] = h
            a_sc[q, pl.ds(r0, SUBLANES), :] = prod
            new.append((h, prod))
        return tuple(new)

    init = tuple((jnp.zeros((SUBLANES, D_RNN), F32), jnp.ones((SUBLANES, D_RNN), F32)) for _ in seqs)
    segs = lax.fori_loop(0, REC_GROUPS, scan_body, init, unroll=2)

    def recur_out(q):
        seg_h, seg_a = segs[q]
        for d in (1, 2, 4):
            keep = row >= d
            a_sh = jnp.where(keep, pltpu.roll(seg_a, d, axis=0), 1.0)
            h_sh = jnp.where(keep, pltpu.roll(seg_h, d, axis=0), 0.0)
            seg_h = seg_a * h_sh + seg_h
            seg_a = seg_a * a_sh
        h_in = h_carry[q]
        after = seg_a * h_in + seg_h
        enter = jnp.where(row == 0, h_in, pltpu.roll(after, 1, axis=0))
        h_carry[q] = after[SUBLANES - 1:SUBLANES, :]
        hs = (u_sc[q].reshape(REC_GROUPS, SUBLANES, D_RNN)
              + a_sc[q].reshape(REC_GROUPS, SUBLANES, D_RNN) * enter[None]).reshape(ts, D_RNN)
        y = hs * jax.nn.gelu(projs[q][:, :D_RNN])
        y_t = jnp.dot(perm_t_ref[...], y.astype(BF16), preferred_element_type=F32).astype(BF16)
        return jnp.dot(y_t, w_out_ref[...], preferred_element_type=F32)

    outs = [recur_out(q) for q in seqs]
    for q in seqs:
        z = ALPHA * x_ref[q] + outs[q]
        xn = _layer_norm_rows(z, g_ref[...], b_ref[...])
        o_ref[q] = xn
        opk_ref[q] = _pack_row_halves(xn)


def _band_gate_weights(w_r, w_i):
    spread = jnp.asarray(np.tile(np.eye(LRU_BLOCK_W, dtype=np.float32), (1, LRU_BLOCKS)), BF16)
    blk = np.arange(D_RNN) // LRU_BLOCK_W
    on_diag = jnp.asarray(blk[:, None] == blk[None, :])

    def dense(w):
        rows = w.reshape(D_RNN, LRU_BLOCK_W).astype(BF16)
        return jnp.where(on_diag, jnp.dot(rows, spread, preferred_element_type=F32), 0.0)

    wr, wi = dense(w_r), dense(w_i)
    tiles = []
    for j in range(N_GATE_TILES):
        ws = GATE_WIN_STARTS[j]
        cs = j * GATE_TILE
        lo_blk = cs // LRU_BLOCK_W
        hi_blk = (cs + GATE_TILE - 1) // LRU_BLOCK_W
        assert ws <= lo_blk * LRU_BLOCK_W and (hi_blk + 1) * LRU_BLOCK_W <= ws + GATE_WIN
        tiles.append(jnp.concatenate([wr[ws:ws + GATE_WIN, cs:cs + GATE_TILE],
                                      wi[ws:ws + GATE_WIN, cs:cs + GATE_TILE]], axis=1))
    return jnp.stack(tiles).astype(BF16)


def _rglru_layer(x, w_in, conv_w, conv_b, w_r, b_r, w_i, b_i, lam, w_out, ln_g, ln_b):
    B, S, D = x.shape
    wg = _band_gate_weights(w_r, w_i)
    rho = np.arange(REC_TS)
    perm_np = np.zeros((REC_TS, REC_TS), np.float32)
    perm_np[rho, (rho % SUBLANES) * REC_GROUPS + rho // SUBLANES] = 1.0
    perm = jnp.asarray(perm_np, BF16)
    perm_t = jnp.asarray(perm_np.T, BF16)
    row = lambda v: v.reshape(1, -1)
    const = lambda shape: pl.BlockSpec(shape, lambda b, s: (0,) * len(shape))
    tile = lambda w: pl.BlockSpec((REC_NSEQ, REC_TS, w), lambda b, s: (b, s, 0))
    halo = (CONV_W - 1) * SUBLANES
    return pl.pallas_call(
        _rglru_kernel,
        grid=(B // REC_NSEQ, S // REC_TS),
        in_specs=[
            tile(D),
            const((REC_TS, REC_TS)),
            const((REC_TS, REC_TS)),
            const((D, 2 * D_RNN)),
            const((CONV_W, D_RNN)),
            const((1, D_RNN)),
            const((N_GATE_TILES, GATE_WIN, 2 * GATE_TILE)),
            const((1, D_RNN)),
            const((1, D_RNN)),
            const((1, D_RNN)),
            const((D_RNN, D)),
            const((1, D)),
            const((1, D)),
        ],
        out_specs=[tile(D), tile(D_HALF)],
        out_shape=[jax.ShapeDtypeStruct((B, S, D), F32), jax.ShapeDtypeStruct((B, S, D_HALF), U32)],
        scratch_shapes=[
            pltpu.VMEM((REC_NSEQ, halo + REC_TS, D_RNN), F32),
            pltpu.VMEM((REC_NSEQ, halo, D_RNN), F32),
            pltpu.VMEM((REC_NSEQ, REC_TS, D_RNN), F32),
            pltpu.VMEM((REC_NSEQ, REC_TS, D_RNN), F32),
            pltpu.VMEM((REC_NSEQ, 1, D_RNN), F32),
        ],
        compiler_params=pltpu.CompilerParams(
            dimension_semantics=("arbitrary", "arbitrary"), vmem_limit_bytes=VMEM_LIMIT),
        name="rglru_ln",
    )(x, perm, perm_t, w_in.astype(BF16), conv_w, row(conv_b), wg, row(b_r), row(b_i), row(lam), w_out.astype(BF16),
      row(ln_g), row(ln_b))


def _router_kernel(x_ref, whi_ref, bias_ref, tri_ref, idx_ref, gate_ref, cnt_ref, base_sc):
    step = pl.program_id(0)
    tr = ROUTE_T

    @pl.when(step == 0)
    def _():
        base_sc[...] = jnp.zeros((N_EXPERTS, 1), F32)

    x = x_ref[...]
    xhi = x.astype(BF16)
    xlo = (x - xhi.astype(F32)).astype(BF16)
    nt = (((1,), (1,)), ((), ()))
    both = lax.dot_general(whi_ref[...], xhi, nt, preferred_element_type=F32)
    logits = (both[:ROUTE_ROWS] + both[ROUTE_ROWS:]
              + lax.dot_general(whi_ref[:ROUTE_ROWS], xlo, nt, preferred_element_type=F32))
    logits = logits + bias_ref[...]

    row8 = lax.broadcasted_iota(jnp.int32, (SUBLANES, tr), 0).astype(F32)
    neg_inf = -jnp.inf
    g = jnp.where(row8 < N_GROUPS, logits[N_EXPERTS:N_EXPERTS + SUBLANES, :], neg_inf)
    gmax = jnp.max(g, axis=0, keepdims=True)
    gidx = jnp.min(jnp.where(g == gmax, row8, SUBLANES), axis=0, keepdims=True)
    g_gate = 1.0 / jnp.sum(jnp.exp(g - gmax), axis=0, keepdims=True)

    esel = logits[0:EXPERTS_PER_GROUP, :]
    for grp in range(1, N_GROUPS):
        esel = jnp.where(gidx == grp, logits[grp * EXPERTS_PER_GROUP:(grp + 1) * EXPERTS_PER_GROUP, :], esel)
    v1 = jnp.max(esel, axis=0, keepdims=True)
    i1 = jnp.min(jnp.where(esel == v1, row8, SUBLANES), axis=0, keepdims=True)
    esel2 = jnp.where(row8 == i1, neg_inf, esel)
    v2 = jnp.max(esel2, axis=0, keepdims=True)
    i2 = jnp.min(jnp.where(esel2 == v2, row8, SUBLANES), axis=0, keepdims=True)
    e21 = jnp.exp(v2 - v1)
    inv = 1.0 / (1.0 + e21)
    gate1 = inv * g_gate
    gate2 = e21 * inv * g_gate
    e1 = gidx * EXPERTS_PER_GROUP + i1
    e2 = gidx * EXPERTS_PER_GROUP + i2

    rowe = lax.broadcasted_iota(jnp.int32, (N_EXPERTS, tr), 0).astype(F32)
    hit1 = rowe == e1
    hit2 = rowe == e2
    member = jnp.where(hit1, 1.0, jnp.where(hit2, 1.0, 0.0))
    before = jnp.dot(member.astype(BF16), tri_ref[...], preferred_element_type=F32) + base_sc[...]
    rank1 = jnp.sum(jnp.where(hit1, before, 0.0), axis=0, keepdims=True)
    rank2 = jnp.sum(jnp.where(hit2, before, 0.0), axis=0, keepdims=True)
    base_sc[...] = base_sc[...] + jnp.sum(member, axis=1, keepdims=True)

    zi = jnp.zeros((1, tr), jnp.int32)
    idx_ref[...] = jnp.concatenate(
        [e1.astype(jnp.int32), e2.astype(jnp.int32), rank1.astype(jnp.int32), rank2.astype(jnp.int32),
         zi, zi, zi, zi], axis=0)
    zf = jnp.zeros((1, tr), F32)
    gate_ref[...] = jnp.concatenate([gate1, gate2, zf, zf, zf, zf, zf, zf], axis=0)
    cnt_ref[...] = jnp.broadcast_to(base_sc[...], (N_EXPERTS, LANES)).astype(jnp.int32)


def _router(xf, w_rg, b_rg, w_re, b_re):
    T, D = xf.shape
    pad_rows = ROUTE_ROWS - N_EXPERTS - N_GROUPS
    w = jnp.concatenate([w_re.T, w_rg.T, jnp.zeros((pad_rows, D), F32)], axis=0)
    whi = w.astype(BF16)
    wlo = (w - whi.astype(F32)).astype(BF16)
    w_split = jnp.concatenate([whi, wlo], axis=0)
    bias = jnp.concatenate([b_re, b_rg, jnp.zeros((pad_rows,), F32)]).reshape(ROUTE_ROWS, 1)
    tri = jnp.asarray(np.triu(np.ones((ROUTE_T, ROUTE_T), np.float32), 1), BF16)
    const = lambda shape: pl.BlockSpec(shape, lambda i: (0,) * len(shape))
    return pl.pallas_call(
        _router_kernel,
        grid=(T // ROUTE_T,),
        in_specs=[
            pl.BlockSpec((ROUTE_T, D), lambda i: (i, 0)),
            const((2 * ROUTE_ROWS, D)),
            const((ROUTE_ROWS, 1)),
            const((ROUTE_T, ROUTE_T)),
        ],
        out_specs=[
            pl.BlockSpec((SUBLANES, ROUTE_T), lambda i: (0, i)),
            pl.BlockSpec((SUBLANES, ROUTE_T), lambda i: (0, i)),
            const((N_EXPERTS, LANES)),
        ],
        out_shape=[
            jax.ShapeDtypeStruct((SUBLANES, T), jnp.int32),
            jax.ShapeDtypeStruct((SUBLANES, T), F32),
            jax.ShapeDtypeStruct((N_EXPERTS, LANES), jnp.int32),
        ],
        scratch_shapes=[pltpu.VMEM((N_EXPERTS, 1), F32)],
        compiler_params=pltpu.CompilerParams(
            dimension_semantics=("arbitrary",), vmem_limit_bytes=VMEM_LIMIT),
        name="router",
    )(xf, w_split, bias, tri)


def _sc_mesh():
    return plsc.VectorSubcoreMesh(core_axis_name="c", subcore_axis_name="s",
                                  num_cores=SC_CORES, num_subcores=SC_SUBCORES)


def _sc_worker_id():
    return lax.axis_index("s") * SC_CORES + lax.axis_index("c")


def _sc_scratch(n_win, width):
    return [
        pltpu.VMEM((n_win, SC_WIN), jnp.int32),
        pltpu.VMEM((n_win, SC_WIN), jnp.int32),
        pltpu.VMEM((2, SC_WIN, width), U32),
        pltpu.SemaphoreType.DMA((2,)),
        pltpu.SemaphoreType.DMA((2,)),
    ]


def _sc_dispatch(rows, idx1, idx2, n_rows):
    _, width = rows.shape
    _, n_win, _ = idx1.shape

    @functools.partial(
        pl.kernel, mesh=_sc_mesh(), out_type=jax.ShapeDtypeStruct((n_rows, width), rows.dtype),
        scratch_types=_sc_scratch(n_win, width), name="sc_dispatch")
    def run(rows_hbm, i1_hbm, i2_hbm, o_hbm, i1_v, i2_v, buf, rsem, wsem):
        wid = _sc_worker_id()
        base = wid * (n_win * SC_WIN)
        pltpu.sync_copy(i1_hbm.at[wid], i1_v)
        pltpu.sync_copy(i2_hbm.at[wid], i2_v)

        def read(j):
            return pltpu.async_copy(rows_hbm.at[pl.ds(base + j * SC_WIN, SC_WIN)], buf.at[j % 2], rsem.at[j % 2])

        reads = {0: read(0)}
        writes = {}
        for j in range(n_win):
            if j + 1 < n_win:
                for d in writes.pop(j - 1, ()):
                    d.wait()
                reads[j + 1] = read(j + 1)
            reads.pop(j).wait()
            writes[j] = (pltpu.async_copy(buf.at[j % 2], o_hbm.at[i1_v.at[j]], wsem.at[j % 2]),
                         pltpu.async_copy(buf.at[j % 2], o_hbm.at[i2_v.at[j]], wsem.at[j % 2]))
        for j in sorted(writes):
            for d in writes[j]:
                d.wait()

    return run(rows, idx1, idx2)


def _sc_gather_pair(table, idx1, idx2):
    _, width = table.shape
    _, n_win, _ = idx1.shape
    n_tok = SC_WORKERS * n_win * SC_WIN
    out_t = jax.ShapeDtypeStruct((n_tok, width), table.dtype)

    @functools.partial(
        pl.kernel, mesh=_sc_mesh(), out_type=(out_t, out_t),
        scratch_types=_sc_scratch(n_win, width), name="sc_combine_gather")
    def run(table_hbm, i1_hbm, i2_hbm, o1_hbm, o2_hbm, i1_v, i2_v, buf, gsem, wsem):
        wid = _sc_worker_id()
        base = wid * (n_win * SC_WIN)
        pltpu.sync_copy(i1_hbm.at[wid], i1_v)
        pltpu.sync_copy(i2_hbm.at[wid], i2_v)
        work = [(i1_v, o1_hbm, j) for j in range(n_win)] + [(i2_v, o2_hbm, j) for j in range(n_win)]

        def gather(t):
            iv, _, j = work[t]
            return pltpu.async_copy(table_hbm.at[iv.at[j]], buf.at[t % 2], gsem.at[t % 2])

        def put(t):
            _, oh, j = work[t]
            return pltpu.async_copy(buf.at[t % 2], oh.at[pl.ds(base + j * SC_WIN, SC_WIN)], wsem.at[t % 2])

        gathers = {0: gather(0)}
        puts = {}
        for t in range(len(work)):
            if t + 1 < len(work):
                if t - 1 in puts:
                    puts.pop(t - 1).wait()
                gathers[t + 1] = gather(t + 1)
            gathers.pop(t).wait()
            puts[t] = put(t)
        for t in sorted(puts):
            puts[t].wait()

    return run(table, idx1, idx2)


def _moe_kernel(layer, be_ref, slot_ref, nxt_ref, nused_ref, x_ref, w1_hbm, w3_hbm, w2_hbm, o_ref,
                w1_st, w3_st, w2_st, w1_sc, w3_sc, w2_sc, sems):
    i = pl.program_id(0)
    expert = be_ref[i]
    slot = slot_ref[i]
    new_expert = jnp.logical_or(i == 0, expert != be_ref[jnp.maximum(i - 1, 0)])

    def weight_copies(e, sl):
        return [pltpu.make_async_copy(hbm.at[layer, e], stage.at[sl], sems.at[k, sl])
                for k, (hbm, stage) in enumerate(((w1_hbm, w1_st), (w3_hbm, w3_st), (w2_hbm, w2_st)))]

    @pl.when(i == 0)
    def _():
        for cp in weight_copies(expert, slot):
            cp.start()

    @pl.when(new_expert)
    def _():
        for cp in weight_copies(expert, slot):
            cp.wait()
        w1_sc[...] = w1_st[slot].astype(BF16)
        w3_sc[...] = w3_st[slot].astype(BF16)
        w2_sc[...] = w2_st[slot].astype(BF16)
        nxt = nxt_ref[i]

        @pl.when(nxt >= 0)
        def _():
            for cp in weight_copies(nxt, 1 - slot):
                cp.start()

    @pl.when(i < nused_ref[0])
    def _():
        def up(rows):
            xa, xb = _unpack_bf16_pair(x_ref[rows, :])
            xa = xa.astype(BF16)
            xb = xb.astype(BF16)
            h1 = (jnp.dot(xa, w1_sc[:D_HALF], preferred_element_type=F32)
                  + jnp.dot(xb, w1_sc[D_HALF:], preferred_element_type=F32))
            h3 = (jnp.dot(xa, w3_sc[:D_HALF], preferred_element_type=F32)
                  + jnp.dot(xb, w3_sc[D_HALF:], preferred_element_type=F32))
            return h1, h3

        def down(rows, h1, h3):
            hdn = (jax.nn.silu(h1) * h3).astype(BF16)
            y = jnp.dot(hdn, w2_sc[...], preferred_element_type=F32)
            o_ref[rows, :] = _pack_row_halves(y)

        subs = [slice(k * MOE_SUB, (k + 1) * MOE_SUB) for k in range(MOE_BM // MOE_SUB)]
        ups = [up(rows) for rows in subs]
        for rows, (h1, h3) in zip(subs, ups):
            down(rows, h1, h3)

    @pl.when(i >= nused_ref[0])
    def _():
        o_ref[...] = jnp.zeros(o_ref.shape, o_ref.dtype)


def _moe_blocks(xbuf, block_e, n_used, w1, w3, w2, layer):
    n_rows, _ = xbuf.shape
    D = D_MODEL
    n_blocks = n_rows // MOE_BM
    pos = jnp.arange(n_blocks, dtype=jnp.int32)
    is_new = jnp.concatenate([jnp.ones((1,), bool), block_e[1:] != block_e[:-1]])
    slot = ((jnp.cumsum(is_new.astype(jnp.int32)) - 1) % 2).astype(jnp.int32)
    change_pos = jnp.where(is_new, pos, n_blocks)
    next_change = jnp.concatenate([lax.cummin(change_pos, reverse=True)[1:], jnp.full((1,), n_blocks, jnp.int32)])
    nxt = jnp.where(next_change < n_blocks, block_e[jnp.minimum(next_change, n_blocks - 1)], -1).astype(jnp.int32)
    rows = lambda i, be, sl, nx, nu: (i, 0)
    grid_spec = pltpu.PrefetchScalarGridSpec(
        num_scalar_prefetch=4,
        grid=(n_blocks,),
        in_specs=[
            pl.BlockSpec((MOE_BM, D_HALF), rows),
            pl.BlockSpec(memory_space=pl.ANY),
            pl.BlockSpec(memory_space=pl.ANY),
            pl.BlockSpec(memory_space=pl.ANY),
        ],
        out_specs=pl.BlockSpec((MOE_BM, D_HALF), rows),
        scratch_shapes=[
            pltpu.VMEM((2, D, D_EXPERT), F32),
            pltpu.VMEM((2, D, D_EXPERT), F32),
            pltpu.VMEM((2, D_EXPERT, D), F32),
            pltpu.VMEM((D, D_EXPERT), BF16),
            pltpu.VMEM((D, D_EXPERT), BF16),
            pltpu.VMEM((D_EXPERT, D), BF16),
            pltpu.SemaphoreType.DMA((3, 2)),
        ],
    )
    return pl.pallas_call(
        functools.partial(_moe_kernel, layer),
        grid_spec=grid_spec,
        out_shape=jax.ShapeDtypeStruct((n_rows, D_HALF), U32),
        compiler_params=pltpu.CompilerParams(
            dimension_semantics=("arbitrary",), vmem_limit_bytes=VMEM_LIMIT),
        name="moe_experts",
    )(block_e, slot, nxt, n_used, xbuf, w1, w3, w2)


def _combine_kernel(x_ref, y1_ref, y2_ref, gates_ref, g_ref, b_ref, o_ref):
    x = x_ref[...]
    pad = jnp.zeros((LANES - SUBLANES, COMB_T), F32)
    gates = jnp.concatenate([gates_ref[...], pad], axis=0).T
    g1 = gates[:, 0:1]
    g2 = gates[:, 1:2]
    a1, b1 = _unpack_bf16_pair(y1_ref[...])
    a2, b2 = _unpack_bf16_pair(y2_ref[...])
    f = jnp.concatenate([g1 * a1 + g2 * a2, g1 * b1 + g2 * b2], axis=1)
    z = ALPHA * x + f
    o_ref[...] = _layer_norm_rows(z, g_ref[...], b_ref[...])


def _combine_ln(xf, y1, y2, gates, ln_g, ln_b):
    T, D = xf.shape
    const = lambda shape: pl.BlockSpec(shape, lambda i: (0,) * len(shape))
    rows = lambda w: pl.BlockSpec((COMB_T, w), lambda i: (i, 0))
    return pl.pallas_call(
        _combine_kernel,
        grid=(T // COMB_T,),
        in_specs=[rows(D), rows(D_HALF), rows(D_HALF), pl.BlockSpec((SUBLANES, COMB_T), lambda i: (0, i)),
                  const((1, D)), const((1, D))],
        out_specs=rows(D),
        out_shape=jax.ShapeDtypeStruct((T, D), F32),
        compiler_params=pltpu.CompilerParams(
            dimension_semantics=("arbitrary",), vmem_limit_bytes=VMEM_LIMIT),
        name="moe_combine_ln",
    )(xf, y1, y2, gates, ln_g.reshape(1, D), ln_b.reshape(1, D))


def _hier_moe_ln(x, xpk, w_rg, b_rg, w_re, b_re, w1, w3, w2, layer, ln_g, ln_b):
    B, S, D = x.shape
    T = B * S
    xf = x.reshape(T, D)
    idx, gates, cnt = _router(xf, w_rg, b_rg, w_re, b_re)
    counts = cnt[:, 0]
    padded = ((counts + MOE_BM - 1) // MOE_BM) * MOE_BM
    pends = jnp.cumsum(padded)
    pstarts = pends - padded
    experts = jnp.arange(N_EXPERTS, dtype=jnp.int32)[:, None]

    def dest(e_row, rank_row):
        return jnp.sum(jnp.where(e_row[None, :] == experts, pstarts[:, None], 0), axis=0) + rank_row

    n_win = T // (SC_WORKERS * SC_WIN)
    dest1 = dest(idx[0], idx[2]).reshape(SC_WORKERS, n_win, SC_WIN)
    dest2 = dest(idx[1], idx[3]).reshape(SC_WORKERS, n_win, SC_WIN)
    n_blocks = -(-(T * TOP_K + N_EXPERTS * (MOE_BM - 1)) // MOE_BM)
    n_rows = n_blocks * MOE_BM
    block_start = jnp.arange(n_blocks, dtype=jnp.int32) * MOE_BM
    block_e = jnp.minimum(jnp.sum(block_start[:, None] >= pends[None, :], axis=1), N_EXPERTS - 1).astype(jnp.int32)
    n_used = (pends[-1] // MOE_BM).astype(jnp.int32).reshape(1)
    xbuf = _sc_dispatch(xpk.reshape(T, D_HALF), dest1, dest2, n_rows)
    ybuf = _moe_blocks(xbuf, block_e, n_used, w1, w3, w2, layer)
    y1, y2 = _sc_gather_pair(ybuf, dest1, dest2)
    out = _combine_ln(xf, y1, y2, gates, ln_g, ln_b)
    return out.reshape(B, S, D)


def _att_head_order():
    order = []
    for p in range(N_HEADS // 2):
        jj, m = divmod(p, 4)
        order += [8 * jj + m, 8 * jj + 4 + m]
    return order


ATT_HEAD_ORDER = _att_head_order()


def _attn_kernel(x_ref, wqkv_ref, bias_ref, sink_ref, wo_ref, g_ref, b_ref, o_ref, opk_ref, kv_ext, o_sc,
                 s_sc0, s_sc1, p_sc0, p_sc1):
    s = pl.program_id(1)
    tq = ATT_TQ
    s_bufs = (s_sc0, s_sc1)
    p_bufs = (p_sc0, p_sc1)

    @pl.when(s == 0)
    def _():
        kv_ext[0:WINDOW, :] = jnp.zeros((WINDOW, 2 * KV_DIM), BF16)

    x = x_ref[0]
    qkv = jnp.dot(x.astype(BF16), wqkv_ref[...], preferred_element_type=F32)
    q = (qkv[:, :Q_DIM] * (HEAD_DIM ** -0.5 * LOG2E)).astype(BF16)
    kv_ext[WINDOW:WINDOW + tq, :] = qkv[:, Q_DIM:].astype(BF16)

    lane = lax.broadcasted_iota(jnp.int32, (WINDOW, LANES), 1)
    low = lane < HEAD_DIM
    sub = lax.broadcasted_iota(jnp.int32, (LANES, WINDOW), 0)
    top = sub < HEAD_DIM
    first = jnp.where(s == 0, 1, 0)
    nt = (((1,), (1,)), ((), ()))
    zero = jnp.zeros((), BF16)

    tiles = [(n, j) for n in range(ATT_NB) for j in range(2)]

    def scores(t):
        n, j = tiles[t]
        r0 = n * WINDOW
        k_tile = kv_ext[r0:r0 + 2 * WINDOW, j * LANES:(j + 1) * LANES]
        parts = []
        for m in range(4):
            p = 4 * j + m
            qt = q[r0:r0 + WINDOW, p * LANES:(p + 1) * LANES]
            parts.append(jnp.where(low, qt, zero))
            parts.append(jnp.where(low, zero, qt))
        qs = jnp.concatenate(parts, axis=0)
        bias_sel = first if n == 0 else 0
        s_bufs[t % 2][...] = (lax.dot_general(k_tile, qs, nt, preferred_element_type=F32)
                              + bias_ref[bias_sel, j])

    def softmax_pv(t):
        n, j = tiles[t]
        r0 = n * WINDOW
        s_sc = s_bufs[t % 2]
        p_sc = p_bufs[t % 2]
        inv_l = []
        for h in range(8):
            hc = slice(h * WINDOW, (h + 1) * WINDOW)
            sink = sink_ref[8 * j + h] * LOG2E
            mx = jnp.maximum(jnp.max(s_sc[:, hc], axis=0, keepdims=True), sink)
            pr = jnp.exp2(s_sc[:, hc] - mx)
            p_sc[:, hc] = pr.astype(BF16)
            inv_l.append(1.0 / (jnp.sum(pr, axis=0, keepdims=True) + jnp.exp2(sink - mx)))
        v_tile = kv_ext[r0:r0 + 2 * WINDOW, KV_DIM + j * LANES:KV_DIM + (j + 1) * LANES]
        v_t = v_tile.astype(F32).T.astype(BF16)
        ov = jnp.dot(v_t, p_sc[...], preferred_element_type=F32)
        for m in range(4):
            p = 4 * j + m
            o_even = ov[:, (2 * m) * WINDOW:(2 * m + 1) * WINDOW] * inv_l[2 * m]
            o_odd = ov[:, (2 * m + 1) * WINDOW:(2 * m + 2) * WINDOW] * inv_l[2 * m + 1]
            o_sc[p * LANES:(p + 1) * LANES, r0:r0 + WINDOW] = jnp.where(top, o_even, o_odd).astype(BF16)

    scores(0)
    for t in range(len(tiles)):
        if t + 1 < len(tiles):
            scores(t + 1)
        softmax_pv(t)
    kv_ext[0:WINDOW, :] = kv_ext[tq:tq + WINDOW, :]

    halves = [slice(hf * (tq // 2), (hf + 1) * (tq // 2)) for hf in range(2)]
    outs_t = [jnp.dot(wo_ref[...], o_sc[:, rows], preferred_element_type=F32) for rows in halves]
    for rows, out_t in zip(halves, outs_t):
        z = ALPHA * x[rows, :] + out_t.T
        xn = _layer_norm_rows(z, g_ref[...], b_ref[...])
        o_ref[0, rows, :] = xn
        opk_ref[0, rows, :] = _pack_row_halves(xn)


def _attn_bias():
    qi = np.arange(WINDOW)[:, None]
    sj = np.arange(2 * WINDOW)[None, :]
    dist = qi - sj + WINDOW
    valid = (dist >= 0) & (dist < WINDOW)
    slopes = 2.0 ** (-8.0 * np.arange(1, N_HEADS + 1, dtype=np.float32) / N_HEADS)
    slopes = slopes.astype(np.float32)[ATT_HEAD_ORDER]
    sb = -(slopes[:, None, None] * dist.astype(np.float32)[None])
    later = np.where(valid[None], sb, -np.inf)
    first = np.where((valid & (sj >= WINDOW))[None], sb, -np.inf)
    bias = np.stack([later, first]).astype(np.float32) * np.float32(LOG2E)
    bias = bias.reshape(2, 2, 8, WINDOW, 2 * WINDOW).transpose(0, 1, 4, 2, 3).reshape(2, 2, 2 * WINDOW, 8 * WINDOW)
    return jnp.asarray(np.ascontiguousarray(bias))


def _attn_layer(x, w_qkv, sinks, w_o, ln_g, ln_b):
    B, S, D = x.shape
    assert ATT_HEAD_ORDER == list(np.arange(N_HEADS).reshape(2, 2, 4).transpose(0, 2, 1).reshape(-1))
    wq = w_qkv[:, :Q_DIM].reshape(D, 2, 2, 4, HEAD_DIM).transpose(0, 1, 3, 2, 4).reshape(D, Q_DIM)
    wqkv = jnp.concatenate([wq, w_qkv[:, Q_DIM:]], axis=1).astype(BF16)
    wo_t = w_o.reshape(2, 2, 4, HEAD_DIM, D).transpose(0, 2, 1, 3, 4).reshape(Q_DIM, D).T.astype(BF16)
    sink = sinks.reshape(2, 2, 4).transpose(0, 2, 1).reshape(N_HEADS, 1, 1)
    bias = _attn_bias()
    const = lambda shape: pl.BlockSpec(shape, lambda b, s: (0,) * len(shape))
    tile = lambda w: pl.BlockSpec((1, ATT_TQ, w), lambda b, s: (b, s, 0))
    return pl.pallas_call(
        _attn_kernel,
        grid=(B, S // ATT_TQ),
        in_specs=[
            tile(D),
            const((D, Q_DIM + 2 * KV_DIM)),
            const((2, 2, 2 * WINDOW, 8 * WINDOW)),
            const((N_HEADS, 1, 1)),
            const((D, Q_DIM)),
            const((1, D)),
            const((1, D)),
        ],
        out_specs=[tile(D), tile(D_HALF)],
        out_shape=[jax.ShapeDtypeStruct((B, S, D), F32), jax.ShapeDtypeStruct((B, S, D_HALF), U32)],
        scratch_shapes=[
            pltpu.VMEM((ATT_TQ + WINDOW, 2 * KV_DIM), BF16),
            pltpu.VMEM((Q_DIM, ATT_TQ), BF16),
            pltpu.VMEM((2 * WINDOW, 8 * WINDOW), F32),
            pltpu.VMEM((2 * WINDOW, 8 * WINDOW), F32),
            pltpu.VMEM((2 * WINDOW, 8 * WINDOW), BF16),
            pltpu.VMEM((2 * WINDOW, 8 * WINDOW), BF16),
        ],
        compiler_params=pltpu.CompilerParams(
            dimension_semantics=("arbitrary", "arbitrary"), vmem_limit_bytes=VMEM_LIMIT),
        name="swa_attn_ln",
    )(x, wqkv, bias, sink, wo_t, ln_g.reshape(1, D), ln_b.reshape(1, D))


def kernel(x, rec_w_in, rec_conv_w, rec_conv_b, rec_w_r, rec_b_r, rec_w_i, rec_b_i, rec_lambda, rec_w_out,
           att_w_qkv, att_sinks, att_w_o, moe_w_group, moe_b_group, moe_w_expert, moe_b_expert,
           moe_w1, moe_w3, moe_w2, ln_g, ln_b):
    for layer in range(DEPTH):
        j = layer // 2
        if layer % 2 == 0:
            x, xpk = _rglru_layer(x, rec_w_in[j], rec_conv_w[j], rec_conv_b[j], rec_w_r[j], rec_b_r[j],
                                  rec_w_i[j], rec_b_i[j], rec_lambda[j], rec_w_out[j],
                                  ln_g[layer, 0], ln_b[layer, 0])
        else:
            x, xpk = _attn_layer(x, att_w_qkv[j], att_sinks[j], att_w_o[j], ln_g[layer, 0], ln_b[layer, 0])
        x = _hier_moe_ln(x, xpk, moe_w_group[layer], moe_b_group[layer], moe_w_expert[layer],
                         moe_b_expert[layer], moe_w1, moe_w3, moe_w2, layer,
                         ln_g[layer, 1], ln_b[layer, 1])
    return x
```

```python
import functools

import jax
import jax.numpy as jnp
import numpy as np
from jax import lax
from jax.experimental import pallas as pl
from jax.experimental.pallas import tpu as pltpu
from jax.experimental.pallas import tpu_sc as plsc

F32 = jnp.float32
BF16 = jnp.bfloat16
U32 = jnp.uint32

D_MODEL = 1024
DEPTH = 2
D_RNN = 1280
LRU_BLOCKS = 16
LRU_BLOCK_W = D_RNN // LRU_BLOCKS
CONV_W = 4
LRU_C = 8.0
N_HEADS = 16
N_KV_HEADS = 4
HEAD_DIM = 64
WINDOW = 128
Q_DIM = N_HEADS * HEAD_DIM
KV_DIM = N_KV_HEADS * HEAD_DIM
N_GROUPS = 4
EXPERTS_PER_GROUP = 8
N_EXPERTS = N_GROUPS * EXPERTS_PER_GROUP
TOP_K = 2
D_EXPERT = 512
ALPHA = (2 * DEPTH) ** 0.25
LN_EPS = 1e-5
LOG2E = 1.4426950408889634

LANES = 128
SUBLANES = 8
VMEM_LIMIT = 56 * 1024 * 1024

REC_TS = 256
REC_GROUPS = REC_TS // SUBLANES
REC_NSEQ = 2
GATE_TILE = 256
GATE_WIN = 512
GATE_WIN_STARTS = (0, 128, 384, 640, 768)
N_GATE_TILES = D_RNN // GATE_TILE

ROUTE_T = 512
ROUTE_ROWS = 40

MOE_BM = 512
MOE_SUB = 256

ATT_TQ = 512
ATT_NB = ATT_TQ // WINDOW

COMB_T = 512

D_HALF = D_MODEL // 2

SC_CORES = 2
SC_SUBCORES = 16
SC_WORKERS = SC_CORES * SC_SUBCORES
SC_WIN = 64


def _layer_norm_rows(z, g, b):
    mu = jnp.mean(z, axis=-1, keepdims=True)
    zc = z - mu
    var = jnp.mean(zc * zc, axis=-1, keepdims=True)
    return zc * lax.rsqrt(var + LN_EPS) * g + b


def _pack_bf16_pair(a, b):
    ua = lax.bitcast_convert_type(a.astype(BF16).astype(F32), U32)
    ub = lax.bitcast_convert_type(b.astype(BF16).astype(F32), U32)
    return (ua >> 16) | (ub & jnp.uint32(0xFFFF0000))


def _unpack_bf16_pair(w):
    a = lax.bitcast_convert_type(w << 16, F32)
    b = lax.bitcast_convert_type(w & jnp.uint32(0xFFFF0000), F32)
    return a, b


def _pack_row_halves(x):
    return _pack_bf16_pair(x[:, :D_HALF], x[:, D_HALF:])


def _rglru_kernel(x_ref, perm_ref, perm_t_ref, w_in_ref, convw_ref, convb_ref, wg_ref, br_ref, bi_ref, lam_ref,
                  w_out_ref, g_ref, b_ref, o_ref, opk_ref, xr_ext, tail_sc, a_sc, u_sc, h_carry):
    s = pl.program_id(1)
    ts = REC_TS
    halo = (CONV_W - 1) * SUBLANES
    seqs = range(REC_NSEQ)

    @pl.when(s == 0)
    def _():
        tail_sc[...] = jnp.zeros((REC_NSEQ, halo, D_RNN), F32)
        h_carry[...] = jnp.zeros((REC_NSEQ, 1, D_RNN), F32)

    row = lax.broadcasted_iota(jnp.int32, (SUBLANES, D_RNN), 0)
    nlam = -lam_ref[...]
    sp = jnp.maximum(nlam, 0.0) + jnp.log1p(jnp.exp(-jnp.abs(nlam)))

    def project(q):
        xp = jnp.dot(perm_ref[...], x_ref[q].astype(BF16), preferred_element_type=F32).astype(BF16)
        return jnp.dot(xp, w_in_ref[...], preferred_element_type=F32)

    def conv_gates(q, proj):
        xr = proj[:, D_RNN:]
        for k in range(CONV_W - 1):
            r0 = ts - halo + k * SUBLANES
            cur = xr[r0:r0 + SUBLANES, :]
            prev = tail_sc[q, k * SUBLANES:(k + 1) * SUBLANES, :]
            xr_ext[q, k * SUBLANES:(k + 1) * SUBLANES, :] = jnp.where(
                row == 0, pltpu.roll(prev, 1, axis=0), pltpu.roll(cur, 1, axis=0))
        tail_sc[q] = xr[ts - halo:, :]
        xr_ext[q, halo:halo + ts, :] = xr
        xc = convb_ref[...] + convw_ref[CONV_W - 1:CONV_W, :] * xr
        for k in range(CONV_W - 1):
            xc = xc + convw_ref[k:k + 1, :] * xr_ext[q, k * SUBLANES:k * SUBLANES + ts, :]
        xcb = xc.astype(BF16)
        pres = [jnp.dot(xcb[:, GATE_WIN_STARTS[j]:GATE_WIN_STARTS[j] + GATE_WIN], wg_ref[j],
                        preferred_element_type=F32) for j in range(N_GATE_TILES)]
        for j, pre in enumerate(pres):
            cs = j * GATE_TILE
            r = jax.nn.sigmoid(pre[:, :GATE_TILE] + br_ref[:, cs:cs + GATE_TILE])
            i = jax.nn.sigmoid(pre[:, GATE_TILE:] + bi_ref[:, cs:cs + GATE_TILE])
            log_a = (-LRU_C) * r * sp[:, cs:cs + GATE_TILE]
            a = jnp.exp(log_a)
            u = jnp.sqrt(1.0 - a * a) * (i * xc[:, cs:cs + GATE_TILE])
            a_sc[q, :, cs:cs + GATE_TILE] = a
            u_sc[q, :, cs:cs + GATE_TILE] = u

    projs = [project(q) for q in seqs]
    for q in seqs:
        conv_gates(q, projs[q])

    def scan_body(gidx, carry):
        r0 = pl.multiple_of(gidx * SUBLANES, SUBLANES)
        new = []
        for q in seqs:
            h, prod = carry[q]
            a8 = a_sc[q, pl.ds(r0, SUBLANES), :]
            h = a8 * h + u_sc[q, pl.ds(r0, SUBLANES), :]
            prod = a8 * prod
            u_sc[q, pl.ds(r0, SUBLANES), :] = h
            a_sc[q, pl.ds(r0, SUBLANES), :] = prod
            new.append((h, prod))
        return tuple(new)

    init = tuple((jnp.zeros((SUBLANES, D_RNN), F32), jnp.ones((SUBLANES, D_RNN), F32)) for _ in seqs)
    segs = lax.fori_loop(0, REC_GROUPS, scan_body, init, unroll=2)

    def recur_out(q):
        seg_h, seg_a = segs[q]
        for d in (1, 2, 4):
            keep = row >= d
            a_sh = jnp.where(keep, pltpu.roll(seg_a, d, axis=0), 1.0)
            h_sh = jnp.where(keep, pltpu.roll(seg_h, d, axis=0), 0.0)
            seg_h = seg_a * h_sh + seg_h
            seg_a = seg_a * a_sh
        h_in = h_carry[q]
        after = seg_a * h_in + seg_h
        enter = jnp.where(row == 0, h_in, pltpu.roll(after, 1, axis=0))
        h_carry[q] = after[SUBLANES - 1:SUBLANES, :]
        hs = (u_sc[q].reshape(REC_GROUPS, SUBLANES, D_RNN)
              + a_sc[q].reshape(REC_GROUPS, SUBLANES, D_RNN) * enter[None]).reshape(ts, D_RNN)
        y = hs * jax.nn.gelu(projs[q][:, :D_RNN])
        y_t = jnp.dot(perm_t_ref[...], y.astype(BF16), preferred_element_type=F32).astype(BF16)
        return jnp.dot(y_t, w_out_ref[...], preferred_element_type=F32)

    outs = [recur_out(q) for q in seqs]
    for q in seqs:
        z = ALPHA * x_ref[q] + outs[q]
        xn = _layer_norm_rows(z, g_ref[...], b_ref[...])
        o_ref[q] = xn
        opk_ref[q] = _pack_row_halves(xn)


def _band_gate_weights(w_r, w_i):
    spread = jnp.asarray(np.tile(np.eye(LRU_BLOCK_W, dtype=np.float32), (1, LRU_BLOCKS)), BF16)
    blk = np.arange(D_RNN) // LRU_BLOCK_W
    on_diag = jnp.asarray(blk[:, None] == blk[None, :])

    def dense(w):
        rows = w.reshape(D_RNN, LRU_BLOCK_W).astype(BF16)
        return jnp.where(on_diag, jnp.dot(rows, spread, preferred_element_type=F32), 0.0)

    wr, wi = dense(w_r), dense(w_i)
    tiles = []
    for j in range(N_GATE_TILES):
        ws = GATE_WIN_STARTS[j]
        cs = j * GATE_TILE
        lo_blk = cs // LRU_BLOCK_W
        hi_blk = (cs + GATE_TILE - 1) // LRU_BLOCK_W
        assert ws <= lo_blk * LRU_BLOCK_W and (hi_blk + 1) * LRU_BLOCK_W <= ws + GATE_WIN
        tiles.append(jnp.concatenate([wr[ws:ws + GATE_WIN, cs:cs + GATE_TILE],
                                      wi[ws:ws + GATE_WIN, cs:cs + GATE_TILE]], axis=1))
    return jnp.stack(tiles).astype(BF16)


def _rglru_layer(x, w_in, conv_w, conv_b, w_r, b_r, w_i, b_i, lam, w_out, ln_g, ln_b):
    B, S, D = x.shape
    wg = _band_gate_weights(w_r, w_i)
    rho = np.arange(REC_TS)
    perm_np = np.zeros((REC_TS, REC_TS), np.float32)
    perm_np[rho, (rho % SUBLANES) * REC_GROUPS + rho // SUBLANES] = 1.0
    perm = jnp.asarray(perm_np, BF16)
    perm_t = jnp.asarray(perm_np.T, BF16)
    row = lambda v: v.reshape(1, -1)
    const = lambda shape: pl.BlockSpec(shape, lambda b, s: (0,) * len(shape))
    tile = lambda w: pl.BlockSpec((REC_NSEQ, REC_TS, w), lambda b, s: (b, s, 0))
    halo = (CONV_W - 1) * SUBLANES
    return pl.pallas_call(
        _rglru_kernel,
        grid=(B // REC_NSEQ, S // REC_TS),
        in_specs=[
            tile(D),
            const((REC_TS, REC_TS)),
            const((REC_TS, REC_TS)),
            const((D, 2 * D_RNN)),
            const((CONV_W, D_RNN)),
            const((1, D_RNN)),
            const((N_GATE_TILES, GATE_WIN, 2 * GATE_TILE)),
            const((1, D_RNN)),
            const((1, D_RNN)),
            const((1, D_RNN)),
            const((D_RNN, D)),
            const((1, D)),
            const((1, D)),
        ],
        out_specs=[tile(D), tile(D_HALF)],
        out_shape=[jax.ShapeDtypeStruct((B, S, D), F32), jax.ShapeDtypeStruct((B, S, D_HALF), U32)],
        scratch_shapes=[
            pltpu.VMEM((REC_NSEQ, halo + REC_TS, D_RNN), F32),
            pltpu.VMEM((REC_NSEQ, halo, D_RNN), F32),
            pltpu.VMEM((REC_NSEQ, REC_TS, D_RNN), F32),
            pltpu.VMEM((REC_NSEQ, REC_TS, D_RNN), F32),
            pltpu.VMEM((REC_NSEQ, 1, D_RNN), F32),
        ],
        compiler_params=pltpu.CompilerParams(
            dimension_semantics=("arbitrary", "arbitrary"), vmem_limit_bytes=VMEM_LIMIT),
        name="rglru_ln",
    )(x, perm, perm_t, w_in.astype(BF16), conv_w, row(conv_b), wg, row(b_r), row(b_i), row(lam), w_out.astype(BF16),
      row(ln_g), row(ln_b))


def _router_kernel(x_ref, whi_ref, bias_ref, tri_ref, idx_ref, gate_ref, cnt_ref, base_sc):
    step = pl.program_id(0)
    tr = ROUTE_T

    @pl.when(step == 0)
    def _():
        base_sc[...] = jnp.zeros((N_EXPERTS, 1), F32)

    x = x_ref[...]
    xhi = x.astype(BF16)
    xlo = (x - xhi.astype(F32)).astype(BF16)
    nt = (((1,), (1,)), ((), ()))
    both = lax.dot_general(whi_ref[...], xhi, nt, preferred_element_type=F32)
    logits = (both[:ROUTE_ROWS] + both[ROUTE_ROWS:]
              + lax.dot_general(whi_ref[:ROUTE_ROWS], xlo, nt, preferred_element_type=F32))
    logits = logits + bias_ref[...]

    row8 = lax.broadcasted_iota(jnp.int32, (SUBLANES, tr), 0).astype(F32)
    neg_inf = -jnp.inf
    g = jnp.where(row8 < N_GROUPS, logits[N_EXPERTS:N_EXPERTS + SUBLANES, :], neg_inf)
    gmax = jnp.max(g, axis=0, keepdims=True)
    gidx = jnp.min(jnp.where(g == gmax, row8, SUBLANES), axis=0, keepdims=True)
    g_gate = 1.0 / jnp.sum(jnp.exp(g - gmax), axis=0, keepdims=True)

    esel = logits[0:EXPERTS_PER_GROUP, :]
    for grp in range(1, N_GROUPS):
        esel = jnp.where(gidx == grp, logits[grp * EXPERTS_PER_GROUP:(grp + 1) * EXPERTS_PER_GROUP, :], esel)
    v1 = jnp.max(esel, axis=0, keepdims=True)
    i1 = jnp.min(jnp.where(esel == v1, row8, SUBLANES), axis=0, keepdims=True)
    esel2 = jnp.where(row8 == i1, neg_inf, esel)
    v2 = jnp.max(esel2, axis=0, keepdims=True)
    i2 = jnp.min(jnp.where(esel2 == v2, row8, SUBLANES), axis=0, keepdims=True)
    e21 = jnp.exp(v2 - v1)
    inv = 1.0 / (1.0 + e21)
    gate1 = inv * g_gate
    gate2 = e21 * inv * g_gate
    e1 = gidx * EXPERTS_PER_GROUP + i1
    e2 = gidx * EXPERTS_PER_GROUP + i2

    rowe = lax.broadcasted_iota(jnp.int32, (N_EXPERTS, tr), 0).astype(F32)
    hit1 = rowe == e1
    hit2 = rowe == e2
    member = jnp.where(hit1, 1.0, jnp.where(hit2, 1.0, 0.0))
    before = jnp.dot(member.astype(BF16), tri_ref[...], preferred_element_type=F32) + base_sc[...]
    rank1 = jnp.sum(jnp.where(hit1, before, 0.0), axis=0, keepdims=True)
    rank2 = jnp.sum(jnp.where(hit2, before, 0.0), axis=0, keepdims=True)
    base_sc[...] = base_sc[...] + jnp.sum(member, axis=1, keepdims=True)

    zi = jnp.zeros((1, tr), jnp.int32)
    idx_ref[...] = jnp.concatenate(
        [e1.astype(jnp.int32), e2.astype(jnp.int32), rank1.astype(jnp.int32), rank2.astype(jnp.int32),
         zi, zi, zi, zi], axis=0)
    zf = jnp.zeros((1, tr), F32)
    gate_ref[...] = jnp.concatenate([gate1, gate2, zf, zf, zf, zf, zf, zf], axis=0)
    cnt_ref[...] = jnp.broadcast_to(base_sc[...], (N_EXPERTS, LANES)).astype(jnp.int32)


def _router(xf, w_rg, b_rg, w_re, b_re):
    T, D = xf.shape
    pad_rows = ROUTE_ROWS - N_EXPERTS - N_GROUPS
    w = jnp.concatenate([w_re.T, w_rg.T, jnp.zeros((pad_rows, D), F32)], axis=0)
    whi = w.astype(BF16)
    wlo = (w - whi.astype(F32)).astype(BF16)
    w_split = jnp.concatenate([whi, wlo], axis=0)
    bias = jnp.concatenate([b_re, b_rg, jnp.zeros((pad_rows,), F32)]).reshape(ROUTE_ROWS, 1)
    tri = jnp.asarray(np.triu(np.ones((ROUTE_T, ROUTE_T), np.float32), 1), BF16)
    const = lambda shape: pl.BlockSpec(shape, lambda i: (0,) * len(shape))
    return pl.pallas_call(
        _router_kernel,
        grid=(T // ROUTE_T,),
        in_specs=[
            pl.BlockSpec((ROUTE_T, D), lambda i: (i, 0)),
            const((2 * ROUTE_ROWS, D)),
            const((ROUTE_ROWS, 1)),
            const((ROUTE_T, ROUTE_T)),
        ],
        out_specs=[
            pl.BlockSpec((SUBLANES, ROUTE_T), lambda i: (0, i)),
            pl.BlockSpec((SUBLANES, ROUTE_T), lambda i: (0, i)),
            const((N_EXPERTS, LANES)),
        ],
        out_shape=[
            jax.ShapeDtypeStruct((SUBLANES, T), jnp.int32),
            jax.ShapeDtypeStruct((SUBLANES, T), F32),
            jax.ShapeDtypeStruct((N_EXPERTS, LANES), jnp.int32),
        ],
        scratch_shapes=[pltpu.VMEM((N_EXPERTS, 1), F32)],
        compiler_params=pltpu.CompilerParams(
            dimension_semantics=("arbitrary",), vmem_limit_bytes=VMEM_LIMIT),
        name="router",
    )(xf, w_split, bias, tri)


def _sc_mesh():
    return plsc.VectorSubcoreMesh(core_axis_name="c", subcore_axis_name="s",
                                  num_cores=SC_CORES, num_subcores=SC_SUBCORES)


def _sc_worker_id():
    return lax.axis_index("s") * SC_CORES + lax.axis_index("c")


def _sc_scratch(n_win, width):
    return [
        pltpu.VMEM((n_win, SC_WIN), jnp.int32),
        pltpu.VMEM((n_win, SC_WIN), jnp.int32),
        pltpu.VMEM((2, SC_WIN, width), U32),
        pltpu.SemaphoreType.DMA((2,)),
        pltpu.SemaphoreType.DMA((2,)),
    ]


def _sc_dispatch(rows, idx1, idx2, n_rows):
    _, width = rows.shape
    _, n_win, _ = idx1.shape

    @functools.partial(
        pl.kernel, mesh=_sc_mesh(), out_type=jax.ShapeDtypeStruct((n_rows, width), rows.dtype),
        scratch_types=_sc_scratch(n_win, width), name="sc_dispatch")
    def run(rows_hbm, i1_hbm, i2_hbm, o_hbm, i1_v, i2_v, buf, rsem, wsem):
        wid = _sc_worker_id()
        base = wid * (n_win * SC_WIN)
        pltpu.sync_copy(i1_hbm.at[wid], i1_v)
        pltpu.sync_copy(i2_hbm.at[wid], i2_v)

        def read(j):
            return pltpu.async_copy(rows_hbm.at[pl.ds(base + j * SC_WIN, SC_WIN)], buf.at[j % 2], rsem.at[j % 2])

        reads = {0: read(0)}
        writes = {}
        for j in range(n_win):
            if j + 1 < n_win:
                for d in writes.pop(j - 1, ()):
                    d.wait()
                reads[j + 1] = read(j + 1)
            reads.pop(j).wait()
            writes[j] = (pltpu.async_copy(buf.at[j % 2], o_hbm.at[i1_v.at[j]], wsem.at[j % 2]),
                         pltpu.async_copy(buf.at[j % 2], o_hbm.at[i2_v.at[j]], wsem.at[j % 2]))
        for j in sorted(writes):
            for d in writes[j]:
                d.wait()

    return run(rows, idx1, idx2)


def _sc_gather_pair(table, idx1, idx2):
    _, width = table.shape
    _, n_win, _ = idx1.shape
    n_tok = SC_WORKERS * n_win * SC_WIN
    out_t = jax.ShapeDtypeStruct((n_tok, width), table.dtype)

    @functools.partial(
        pl.kernel, mesh=_sc_mesh(), out_type=(out_t, out_t),
        scratch_types=_sc_scratch(n_win, width), name="sc_combine_gather")
    def run(table_hbm, i1_hbm, i2_hbm, o1_hbm, o2_hbm, i1_v, i2_v, buf, gsem, wsem):
        wid = _sc_worker_id()
        base = wid * (n_win * SC_WIN)
        pltpu.sync_copy(i1_hbm.at[wid], i1_v)
        pltpu.sync_copy(i2_hbm.at[wid], i2_v)
        work = [(i1_v, o1_hbm, j) for j in range(n_win)] + [(i2_v, o2_hbm, j) for j in range(n_win)]

        def gather(t):
            iv, _, j = work[t]
            return pltpu.async_copy(table_hbm.at[iv.at[j]], buf.at[t % 2], gsem.at[t % 2])

        def put(t):
            _, oh, j = work[t]
            return pltpu.async_copy(buf.at[t % 2], oh.at[pl.ds(base + j * SC_WIN, SC_WIN)], wsem.at[t % 2])

        gathers = {0: gather(0)}
        puts = {}
        for t in range(len(work)):
            if t + 1 < len(work):
                if t - 1 in puts:
                    puts.pop(t - 1).wait()
                gathers[t + 1] = gather(t + 1)
            gathers.pop(t).wait()
            puts[t] = put(t)
        for t in sorted(puts):
            puts[t].wait()

    return run(table, idx1, idx2)


def _moe_kernel(layer, be_ref, slot_ref, nxt_ref, nused_ref, x_ref, w1_hbm, w3_hbm, w2_hbm, o_ref,
                w1_st, w3_st, w2_st, w1_sc, w3_sc, w2_sc, sems):
    i = pl.program_id(0)
    expert = be_ref[i]
    slot = slot_ref[i]
    new_expert = jnp.logical_or(i == 0, expert != be_ref[jnp.maximum(i - 1, 0)])

    def weight_copies(e, sl):
        return [pltpu.make_async_copy(hbm.at[layer, e], stage.at[sl], sems.at[k, sl])
                for k, (hbm, stage) in enumerate(((w1_hbm, w1_st), (w3_hbm, w3_st), (w2_hbm, w2_st)))]

    @pl.when(i == 0)
    def _():
        for cp in weight_copies(expert, slot):
            cp.start()

    @pl.when(new_expert)
    def _():
        for cp in weight_copies(expert, slot):
            cp.wait()
        w1_sc[...] = w1_st[slot].astype(BF16)
        w3_sc[...] = w3_st[slot].astype(BF16)
        w2_sc[...] = w2_st[slot].astype(BF16)
        nxt = nxt_ref[i]

        @pl.when(nxt >= 0)
        def _():
            for cp in weight_copies(nxt, 1 - slot):
                cp.start()

    @pl.when(i < nused_ref[0])
    def _():
        def up(rows):
            xa, xb = _unpack_bf16_pair(x_ref[rows, :])
            xa = xa.astype(BF16)
            xb = xb.astype(BF16)
            h1 = (jnp.dot(xa, w1_sc[:D_HALF], preferred_element_type=F32)
                  + jnp.dot(xb, w1_sc[D_HALF:], preferred_element_type=F32))
            h3 = (jnp.dot(xa, w3_sc[:D_HALF], preferred_element_type=F32)
                  + jnp.dot(xb, w3_sc[D_HALF:], preferred_element_type=F32))
            return h1, h3

        def down(rows, h1, h3):
            hdn = (jax.nn.silu(h1) * h3).astype(BF16)
            y = jnp.dot(hdn, w2_sc[...], preferred_element_type=F32)
            o_ref[rows, :] = _pack_row_halves(y)

        subs = [slice(k * MOE_SUB, (k + 1) * MOE_SUB) for k in range(MOE_BM // MOE_SUB)]
        ups = [up(rows) for rows in subs]
        for rows, (h1, h3) in zip(subs, ups):
            down(rows, h1, h3)

    @pl.when(i >= nused_ref[0])
    def _():
        o_ref[...] = jnp.zeros(o_ref.shape, o_ref.dtype)


def _moe_blocks(xbuf, block_e, n_used, w1, w3, w2, layer):
    n_rows, _ = xbuf.shape
    D = D_MODEL
    n_blocks = n_rows // MOE_BM
    pos = jnp.arange(n_blocks, dtype=jnp.int32)
    is_new = jnp.concatenate([jnp.ones((1,), bool), block_e[1:] != block_e[:-1]])
    slot = ((jnp.cumsum(is_new.astype(jnp.int32)) - 1) % 2).astype(jnp.int32)
    change_pos = jnp.where(is_new, pos, n_blocks)
    next_change = jnp.concatenate([lax.cummin(change_pos, reverse=True)[1:], jnp.full((1,), n_blocks, jnp.int32)])
    nxt = jnp.where(next_change < n_blocks, block_e[jnp.minimum(next_change, n_blocks - 1)], -1).astype(jnp.int32)
    rows = lambda i, be, sl, nx, nu: (i, 0)
    grid_spec = pltpu.PrefetchScalarGridSpec(
        num_scalar_prefetch=4,
        grid=(n_blocks,),
        in_specs=[
            pl.BlockSpec((MOE_BM, D_HALF), rows),
            pl.BlockSpec(memory_space=pl.ANY),
            pl.BlockSpec(memory_space=pl.ANY),
            pl.BlockSpec(memory_space=pl.ANY),
        ],
        out_specs=pl.BlockSpec((MOE_BM, D_HALF), rows),
        scratch_shapes=[
            pltpu.VMEM((2, D, D_EXPERT), F32),
            pltpu.VMEM((2, D, D_EXPERT), F32),
            pltpu.VMEM((2, D_EXPERT, D), F32),
            pltpu.VMEM((D, D_EXPERT), BF16),
            pltpu.VMEM((D, D_EXPERT), BF16),
            pltpu.VMEM((D_EXPERT, D), BF16),
            pltpu.SemaphoreType.DMA((3, 2)),
        ],
    )
    return pl.pallas_call(
        functools.partial(_moe_kernel, layer),
        grid_spec=grid_spec,
        out_shape=jax.ShapeDtypeStruct((n_rows, D_HALF), U32),
        compiler_params=pltpu.CompilerParams(
            dimension_semantics=("arbitrary",), vmem_limit_bytes=VMEM_LIMIT),
        name="moe_experts",
    )(block_e, slot, nxt, n_used, xbuf, w1, w3, w2)


def _moe_combine_norm(x, y1, y2, gate_rows, g, b):
    n = x.shape[0]
    pad = jnp.zeros((LANES - SUBLANES, n), F32)
    gates = jnp.concatenate([gate_rows, pad], axis=0).T
    g1 = gates[:, 0:1]
    g2 = gates[:, 1:2]
    a1, b1 = _unpack_bf16_pair(y1)
    a2, b2 = _unpack_bf16_pair(y2)
    f = jnp.concatenate([g1 * a1 + g2 * a2, g1 * b1 + g2 * b2], axis=1)
    return _layer_norm_rows(ALPHA * x + f, g, b)


def _combine_kernel(x_ref, y1_ref, y2_ref, gates_ref, g_ref, b_ref, o_ref):
    o_ref[...] = _moe_combine_norm(x_ref[...], y1_ref[...], y2_ref[...], gates_ref[...], g_ref[...], b_ref[...])


def _combine_ln(xf, y1, y2, gates, ln_g, ln_b):
    T, D = xf.shape
    const = lambda shape: pl.BlockSpec(shape, lambda i: (0,) * len(shape))
    rows = lambda w: pl.BlockSpec((COMB_T, w), lambda i: (i, 0))
    return pl.pallas_call(
        _combine_kernel,
        grid=(T // COMB_T,),
        in_specs=[rows(D), rows(D_HALF), rows(D_HALF), pl.BlockSpec((SUBLANES, COMB_T), lambda i: (0, i)),
                  const((1, D)), const((1, D))],
        out_specs=rows(D),
        out_shape=jax.ShapeDtypeStruct((T, D), F32),
        compiler_params=pltpu.CompilerParams(
            dimension_semantics=("arbitrary",), vmem_limit_bytes=VMEM_LIMIT),
        name="moe_combine_ln",
    )(xf, y1, y2, gates, ln_g.reshape(1, D), ln_b.reshape(1, D))


def _hier_moe(x, xpk, w_rg, b_rg, w_re, b_re, w1, w3, w2, layer):
    B, S, D = x.shape
    T = B * S
    xf = x.reshape(T, D)
    idx, gates, cnt = _router(xf, w_rg, b_rg, w_re, b_re)
    counts = cnt[:, 0]
    padded = ((counts + MOE_BM - 1) // MOE_BM) * MOE_BM
    pends = jnp.cumsum(padded)
    pstarts = pends - padded
    experts = jnp.arange(N_EXPERTS, dtype=jnp.int32)[:, None]

    def dest(e_row, rank_row):
        return jnp.sum(jnp.where(e_row[None, :] == experts, pstarts[:, None], 0), axis=0) + rank_row

    n_win = T // (SC_WORKERS * SC_WIN)
    dest1 = dest(idx[0], idx[2]).reshape(SC_WORKERS, n_win, SC_WIN)
    dest2 = dest(idx[1], idx[3]).reshape(SC_WORKERS, n_win, SC_WIN)
    n_blocks = -(-(T * TOP_K + N_EXPERTS * (MOE_BM - 1)) // MOE_BM)
    n_rows = n_blocks * MOE_BM
    block_start = jnp.arange(n_blocks, dtype=jnp.int32) * MOE_BM
    block_e = jnp.minimum(jnp.sum(block_start[:, None] >= pends[None, :], axis=1), N_EXPERTS - 1).astype(jnp.int32)
    n_used = (pends[-1] // MOE_BM).astype(jnp.int32).reshape(1)
    xbuf = _sc_dispatch(xpk.reshape(T, D_HALF), dest1, dest2, n_rows)
    ybuf = _moe_blocks(xbuf, block_e, n_used, w1, w3, w2, layer)
    y1, y2 = _sc_gather_pair(ybuf, dest1, dest2)
    return y1, y2, gates


def _att_head_order():
    order = []
    for p in range(N_HEADS // 2):
        jj, m = divmod(p, 4)
        order += [8 * jj + m, 8 * jj + 4 + m]
    return order


ATT_HEAD_ORDER = _att_head_order()


def _attn_kernel(xprev_ref, y1_ref, y2_ref, gates_ref, g_prev_ref, b_prev_ref, wqkv_ref, bias_ref, sink_ref, wo_ref,
                 g_ref, b_ref, o_ref, opk_ref, kv_ext, o_sc, s_sc0, s_sc1, p_sc0, p_sc1):
    s = pl.program_id(1)
    tq = ATT_TQ
    s_bufs = (s_sc0, s_sc1)
    p_bufs = (p_sc0, p_sc1)

    @pl.when(s == 0)
    def _():
        kv_ext[0:WINDOW, :] = jnp.zeros((WINDOW, 2 * KV_DIM), BF16)

    x = _moe_combine_norm(xprev_ref[0], y1_ref[0], y2_ref[0], gates_ref[...], g_prev_ref[...], b_prev_ref[...])
    qkv = jnp.dot(x.astype(BF16), wqkv_ref[...], preferred_element_type=F32)
    q = (qkv[:, :Q_DIM] * (HEAD_DIM ** -0.5 * LOG2E)).astype(BF16)
    kv_ext[WINDOW:WINDOW + tq, :] = qkv[:, Q_DIM:].astype(BF16)

    lane = lax.broadcasted_iota(jnp.int32, (WINDOW, LANES), 1)
    low = lane < HEAD_DIM
    sub = lax.broadcasted_iota(jnp.int32, (LANES, WINDOW), 0)
    top = sub < HEAD_DIM
    first = jnp.where(s == 0, 1, 0)
    nt = (((1,), (1,)), ((), ()))
    zero = jnp.zeros((), BF16)

    tiles = [(n, j) for n in range(ATT_NB) for j in range(2)]

    def scores(t):
        n, j = tiles[t]
        r0 = n * WINDOW
        k_tile = kv_ext[r0:r0 + 2 * WINDOW, j * LANES:(j + 1) * LANES]
        parts = []
        for m in range(4):
            p = 4 * j + m
            qt = q[r0:r0 + WINDOW, p * LANES:(p + 1) * LANES]
            parts.append(jnp.where(low, qt, zero))
            parts.append(jnp.where(low, zero, qt))
        qs = jnp.concatenate(parts, axis=0)
        bias_sel = first if n == 0 else 0
        s_bufs[t % 2][...] = (lax.dot_general(k_tile, qs, nt, preferred_element_type=F32)
                              + bias_ref[bias_sel, j])

    def softmax_pv(t):
        n, j = tiles[t]
        r0 = n * WINDOW
        s_sc = s_bufs[t % 2]
        p_sc = p_bufs[t % 2]
        inv_l = []
        for h in range(8):
            hc = slice(h * WINDOW, (h + 1) * WINDOW)
            sink = sink_ref[8 * j + h] * LOG2E
            mx = jnp.maximum(jnp.max(s_sc[:, hc], axis=0, keepdims=True), sink)
            pr = jnp.exp2(s_sc[:, hc] - mx)
            p_sc[:, hc] = pr.astype(BF16)
            inv_l.append(1.0 / (jnp.sum(pr, axis=0, keepdims=True) + jnp.exp2(sink - mx)))
        v_tile = kv_ext[r0:r0 + 2 * WINDOW, KV_DIM + j * LANES:KV_DIM + (j + 1) * LANES]
        v_t = v_tile.astype(F32).T.astype(BF16)
        ov = jnp.dot(v_t, p_sc[...], preferred_element_type=F32)
        for m in range(4):
            p = 4 * j + m
            o_even = ov[:, (2 * m) * WINDOW:(2 * m + 1) * WINDOW] * inv_l[2 * m]
            o_odd = ov[:, (2 * m + 1) * WINDOW:(2 * m + 2) * WINDOW] * inv_l[2 * m + 1]
            o_sc[p * LANES:(p + 1) * LANES, r0:r0 + WINDOW] = jnp.where(top, o_even, o_odd).astype(BF16)

    scores(0)
    for t in range(len(tiles)):
        if t + 1 < len(tiles):
            scores(t + 1)
        softmax_pv(t)
    kv_ext[0:WINDOW, :] = kv_ext[tq:tq + WINDOW, :]

    halves = [slice(hf * (tq // 2), (hf + 1) * (tq // 2)) for hf in range(2)]
    outs_t = [jnp.dot(wo_ref[...], o_sc[:, rows], preferred_element_type=F32) for rows in halves]
    for rows, out_t in zip(halves, outs_t):
        z = ALPHA * x[rows, :] + out_t.T
        xn = _layer_norm_rows(z, g_ref[...], b_ref[...])
        o_ref[0, rows, :] = xn
        opk_ref[0, rows, :] = _pack_row_halves(xn)


def _attn_bias():
    qi = np.arange(WINDOW)[:, None]
    sj = np.arange(2 * WINDOW)[None, :]
    dist = qi - sj + WINDOW
    valid = (dist >= 0) & (dist < WINDOW)
    slopes = 2.0 ** (-8.0 * np.arange(1, N_HEADS + 1, dtype=np.float32) / N_HEADS)
    slopes = slopes.astype(np.float32)[ATT_HEAD_ORDER]
    sb = -(slopes[:, None, None] * dist.astype(np.float32)[None])
    later = np.where(valid[None], sb, -np.inf)
    first = np.where((valid & (sj >= WINDOW))[None], sb, -np.inf)
    bias = np.stack([later, first]).astype(np.float32) * np.float32(LOG2E)
    bias = bias.reshape(2, 2, 8, WINDOW, 2 * WINDOW).transpose(0, 1, 4, 2, 3).reshape(2, 2, 2 * WINDOW, 8 * WINDOW)
    return jnp.asarray(np.ascontiguousarray(bias))


def _attn_layer(x_prev, y1, y2, gates, g_prev, b_prev, w_qkv, sinks, w_o, ln_g, ln_b):
    B, S, D = x_prev.shape
    steps = S // ATT_TQ
    assert ATT_HEAD_ORDER == list(np.arange(N_HEADS).reshape(2, 2, 4).transpose(0, 2, 1).reshape(-1))
    wq = w_qkv[:, :Q_DIM].reshape(D, 2, 2, 4, HEAD_DIM).transpose(0, 1, 3, 2, 4).reshape(D, Q_DIM)
    wqkv = jnp.concatenate([wq, w_qkv[:, Q_DIM:]], axis=1).astype(BF16)
    wo_t = w_o.reshape(2, 2, 4, HEAD_DIM, D).transpose(0, 2, 1, 3, 4).reshape(Q_DIM, D).T.astype(BF16)
    sink = sinks.reshape(2, 2, 4).transpose(0, 2, 1).reshape(N_HEADS, 1, 1)
    bias = _attn_bias()
    const = lambda shape: pl.BlockSpec(shape, lambda b, s: (0,) * len(shape))
    tile = lambda w: pl.BlockSpec((1, ATT_TQ, w), lambda b, s: (b, s, 0))
    return pl.pallas_call(
        _attn_kernel,
        grid=(B, S // ATT_TQ),
        in_specs=[
            tile(D),
            tile(D_HALF),
            tile(D_HALF),
            pl.BlockSpec((SUBLANES, ATT_TQ), lambda b, s: (0, b * steps + s)),
            const((1, D)),
            const((1, D)),
            const((D, Q_DIM + 2 * KV_DIM)),
            const((2, 2, 2 * WINDOW, 8 * WINDOW)),
            const((N_HEADS, 1, 1)),
            const((D, Q_DIM)),
            const((1, D)),
            const((1, D)),
        ],
        out_specs=[tile(D), tile(D_HALF)],
        out_shape=[jax.ShapeDtypeStruct((B, S, D), F32), jax.ShapeDtypeStruct((B, S, D_HALF), U32)],
        scratch_shapes=[
            pltpu.VMEM((ATT_TQ + WINDOW, 2 * KV_DIM), BF16),
            pltpu.VMEM((Q_DIM, ATT_TQ), BF16),
            pltpu.VMEM((2 * WINDOW, 8 * WINDOW), F32),
            pltpu.VMEM((2 * WINDOW, 8 * WINDOW), F32),
            pltpu.VMEM((2 * WINDOW, 8 * WINDOW), BF16),
            pltpu.VMEM((2 * WINDOW, 8 * WINDOW), BF16),
        ],
        compiler_params=pltpu.CompilerParams(
            dimension_semantics=("arbitrary", "arbitrary"), vmem_limit_bytes=VMEM_LIMIT),
        name="swa_attn_ln",
    )(x_prev, y1.reshape(B, S, D_HALF), y2.reshape(B, S, D_HALF), gates, g_prev.reshape(1, D), b_prev.reshape(1, D),
      wqkv, bias, sink, wo_t, ln_g.reshape(1, D), ln_b.reshape(1, D))


def kernel(x, rec_w_in, rec_conv_w, rec_conv_b, rec_w_r, rec_b_r, rec_w_i, rec_b_i, rec_lambda, rec_w_out,
           att_w_qkv, att_sinks, att_w_o, moe_w_group, moe_b_group, moe_w_expert, moe_b_expert,
           moe_w1, moe_w3, moe_w2, ln_g, ln_b):
    assert DEPTH == 2
    B, S, D = x.shape

    def moe(layer, xin, xin_pk):
        return _hier_moe(xin, xin_pk, moe_w_group[layer], moe_b_group[layer], moe_w_expert[layer],
                         moe_b_expert[layer], moe_w1, moe_w3, moe_w2, layer)

    x1, x1_pk = _rglru_layer(x, rec_w_in[0], rec_conv_w[0], rec_conv_b[0], rec_w_r[0], rec_b_r[0], rec_w_i[0],
                             rec_b_i[0], rec_lambda[0], rec_w_out[0], ln_g[0, 0], ln_b[0, 0])
    y1, y2, gates = moe(0, x1, x1_pk)
    x3, x3_pk = _attn_layer(x1, y1, y2, gates, ln_g[0, 1], ln_b[0, 1], att_w_qkv[0], att_sinks[0], att_w_o[0],
                            ln_g[1, 0], ln_b[1, 0])
    y1, y2, gates = moe(1, x3, x3_pk)
    out = _combine_ln(x3.reshape(B * S, D), y1, y2, gates, ln_g[1, 1], ln_b[1, 1])
    return out.reshape(B, S, D)
```

```python
import functools

import jax
import jax.numpy as jnp
import numpy as np
from jax import lax
from jax.experimental import pallas as pl
from jax.experimental.pallas import tpu as pltpu
from jax.experimental.pallas import tpu_sc as plsc

F32 = jnp.float32
BF16 = jnp.bfloat16
U32 = jnp.uint32

D_MODEL = 1024
DEPTH = 2
D_RNN = 1280
LRU_BLOCKS = 16
LRU_BLOCK_W = D_RNN // LRU_BLOCKS
CONV_W = 4
LRU_C = 8.0
N_HEADS = 16
N_KV_HEADS = 4
HEAD_DIM = 64
WINDOW = 128
Q_DIM = N_HEADS * HEAD_DIM
KV_DIM = N_KV_HEADS * HEAD_DIM
N_GROUPS = 4
EXPERTS_PER_GROUP = 8
N_EXPERTS = N_GROUPS * EXPERTS_PER_GROUP
TOP_K = 2
D_EXPERT = 512
ALPHA = (2 * DEPTH) ** 0.25
LN_EPS = 1e-5
LOG2E = 1.4426950408889634

LANES = 128
SUBLANES = 8
VMEM_LIMIT = 56 * 1024 * 1024

REC_TS = 256
REC_GROUPS = REC_TS // SUBLANES
REC_NSEQ = 2
GATE_TILE = 256
GATE_WIN = 512
GATE_WIN_STARTS = (0, 128, 384, 640, 768)
N_GATE_TILES = D_RNN // GATE_TILE

ROUTE_T = 512
ROUTE_ROWS = 40

MOE_BM = 512
MOE_SUB = 256

ATT_TQ = 256
ATT_NB = ATT_TQ // WINDOW
ATT_NSEQ = 2

COMB_T = 1024

D_HALF = D_MODEL // 2

SC_CORES = 2
SC_SUBCORES = 16
SC_WORKERS = SC_CORES * SC_SUBCORES
SC_WIN = 64


def _layer_norm_rows(z, g, b):
    mu = jnp.mean(z, axis=-1, keepdims=True)
    zc = z - mu
    var = jnp.mean(zc * zc, axis=-1, keepdims=True)
    return zc * lax.rsqrt(var + LN_EPS) * g + b


def _pack_bf16_pair(a, b):
    ua = lax.bitcast_convert_type(a.astype(BF16).astype(F32), U32)
    ub = lax.bitcast_convert_type(b.astype(BF16).astype(F32), U32)
    return (ua >> 16) | (ub & jnp.uint32(0xFFFF0000))


def _unpack_bf16_pair(w):
    a = lax.bitcast_convert_type(w << 16, F32)
    b = lax.bitcast_convert_type(w & jnp.uint32(0xFFFF0000), F32)
    return a, b


def _pack_row_halves(x):
    return _pack_bf16_pair(x[:, :D_HALF], x[:, D_HALF:])


def _rglru_kernel(x_ref, perm_ref, perm_t_ref, w_in_ref, convw_ref, convb_ref, wg_ref, br_ref, bi_ref, lam_ref,
                  w_out_ref, g_ref, b_ref, o_ref, opk_ref, xr_ext, tail_sc, a_sc, u_sc, h_carry):
    s = pl.program_id(1)
    ts = REC_TS
    halo = (CONV_W - 1) * SUBLANES
    seqs = range(REC_NSEQ)

    @pl.when(s == 0)
    def _():
        tail_sc[...] = jnp.zeros((REC_NSEQ, halo, D_RNN), F32)
        h_carry[...] = jnp.zeros((REC_NSEQ, 1, D_RNN), F32)

    row = lax.broadcasted_iota(jnp.int32, (SUBLANES, D_RNN), 0)
    nlam = -lam_ref[...]
    sp = jnp.maximum(nlam, 0.0) + jnp.log1p(jnp.exp(-jnp.abs(nlam)))

    def project(q):
        xp = jnp.dot(perm_ref[...], x_ref[q].astype(BF16), preferred_element_type=F32).astype(BF16)
        return jnp.dot(xp, w_in_ref[...], preferred_element_type=F32)

    def conv_gates(q, proj):
        xr = proj[:, D_RNN:]
        for k in range(CONV_W - 1):
            r0 = ts - halo + k * SUBLANES
            cur = xr[r0:r0 + SUBLANES, :]
            prev = tail_sc[q, k * SUBLANES:(k + 1) * SUBLANES, :]
            xr_ext[q, k * SUBLANES:(k + 1) * SUBLANES, :] = jnp.where(
                row == 0, pltpu.roll(prev, 1, axis=0), pltpu.roll(cur, 1, axis=0))
        tail_sc[q] = xr[ts - halo:, :]
        xr_ext[q, halo:halo + ts, :] = xr
        xc = convb_ref[...] + convw_ref[CONV_W - 1:CONV_W, :] * xr
        for k in range(CONV_W - 1):
            xc = xc + convw_ref[k:k + 1, :] * xr_ext[q, k * SUBLANES:k * SUBLANES + ts, :]
        xcb = xc.astype(BF16)
        pres = [jnp.dot(xcb[:, GATE_WIN_STARTS[j]:GATE_WIN_STARTS[j] + GATE_WIN], wg_ref[j],
                        preferred_element_type=F32) for j in range(N_GATE_TILES)]
        for j, pre in enumerate(pres):
            cs = j * GATE_TILE
            r = jax.nn.sigmoid(pre[:, :GATE_TILE] + br_ref[:, cs:cs + GATE_TILE])
            i = jax.nn.sigmoid(pre[:, GATE_TILE:] + bi_ref[:, cs:cs + GATE_TILE])
            log_a = (-LRU_C) * r * sp[:, cs:cs + GATE_TILE]
            a = jnp.exp(log_a)
            u = jnp.sqrt(1.0 - a * a) * (i * xc[:, cs:cs + GATE_TILE])
            a_sc[q, :, cs:cs + GATE_TILE] = a
            u_sc[q, :, cs:cs + GATE_TILE] = u

    def segment_scan(q):
        h = jnp.zeros((SUBLANES, D_RNN), F32)
        prod = jnp.ones((SUBLANES, D_RNN), F32)
        for gidx in range(REC_GROUPS):
            rows = slice(gidx * SUBLANES, (gidx + 1) * SUBLANES)
            a8 = a_sc[q, rows, :]
            h = a8 * h + u_sc[q, rows, :]
            prod = a8 * prod
            u_sc[q, rows, :] = h
            a_sc[q, rows, :] = prod
        return h, prod

    def recur_out(q, proj, seg):
        seg_h, seg_a = seg
        for d in (1, 2, 4):
            keep = row >= d
            a_sh = jnp.where(keep, pltpu.roll(seg_a, d, axis=0), 1.0)
            h_sh = jnp.where(keep, pltpu.roll(seg_h, d, axis=0), 0.0)
            seg_h = seg_a * h_sh + seg_h
            seg_a = seg_a * a_sh
        h_in = h_carry[q]
        after = seg_a * h_in + seg_h
        enter = jnp.where(row == 0, h_in, pltpu.roll(after, 1, axis=0))
        h_carry[q] = after[SUBLANES - 1:SUBLANES, :]
        hs = (u_sc[q].reshape(REC_GROUPS, SUBLANES, D_RNN)
              + a_sc[q].reshape(REC_GROUPS, SUBLANES, D_RNN) * enter[None]).reshape(ts, D_RNN)
        y = hs * jax.nn.gelu(proj[:, :D_RNN])
        y_t = jnp.dot(perm_t_ref[...], y.astype(BF16), preferred_element_type=F32).astype(BF16)
        return jnp.dot(y_t, w_out_ref[...], preferred_element_type=F32)

    def finish(q, out):
        z = ALPHA * x_ref[q] + out
        xn = _layer_norm_rows(z, g_ref[...], b_ref[...])
        o_ref[q] = xn
        opk_ref[q] = _pack_row_halves(xn)

    assert REC_NSEQ == 2
    proj_a = project(0)
    conv_gates(0, proj_a)
    proj_b = project(1)
    out_a = recur_out(0, proj_a, segment_scan(0))
    conv_gates(1, proj_b)
    finish(0, out_a)
    out_b = recur_out(1, proj_b, segment_scan(1))
    finish(1, out_b)


def _band_gate_weights(w_r, w_i):
    spread = jnp.asarray(np.tile(np.eye(LRU_BLOCK_W, dtype=np.float32), (1, LRU_BLOCKS)), BF16)
    blk = np.arange(D_RNN) // LRU_BLOCK_W
    on_diag = jnp.asarray(blk[:, None] == blk[None, :])

    def dense(w):
        rows = w.reshape(D_RNN, LRU_BLOCK_W).astype(BF16)
        return jnp.where(on_diag, jnp.dot(rows, spread, preferred_element_type=F32), 0.0)

    wr, wi = dense(w_r), dense(w_i)
    tiles = []
    for j in range(N_GATE_TILES):
        ws = GATE_WIN_STARTS[j]
        cs = j * GATE_TILE
        lo_blk = cs // LRU_BLOCK_W
        hi_blk = (cs + GATE_TILE - 1) // LRU_BLOCK_W
        assert ws <= lo_blk * LRU_BLOCK_W and (hi_blk + 1) * LRU_BLOCK_W <= ws + GATE_WIN
        tiles.append(jnp.concatenate([wr[ws:ws + GATE_WIN, cs:cs + GATE_TILE],
                                      wi[ws:ws + GATE_WIN, cs:cs + GATE_TILE]], axis=1))
    return jnp.stack(tiles).astype(BF16)


def _rglru_layer(x, w_in, conv_w, conv_b, w_r, b_r, w_i, b_i, lam, w_out, ln_g, ln_b):
    B, S, D = x.shape
    wg = _band_gate_weights(w_r, w_i)
    rho = np.arange(REC_TS)
    perm_np = np.zeros((REC_TS, REC_TS), np.float32)
    perm_np[rho, (rho % SUBLANES) * REC_GROUPS + rho // SUBLANES] = 1.0
    perm = jnp.asarray(perm_np, BF16)
    perm_t = jnp.asarray(perm_np.T, BF16)
    row = lambda v: v.reshape(1, -1)
    const = lambda shape: pl.BlockSpec(shape, lambda b, s: (0,) * len(shape))
    tile = lambda w: pl.BlockSpec((REC_NSEQ, REC_TS, w), lambda b, s: (b, s, 0))
    halo = (CONV_W - 1) * SUBLANES
    return pl.pallas_call(
        _rglru_kernel,
        grid=(B // REC_NSEQ, S // REC_TS),
        in_specs=[
            tile(D),
            const((REC_TS, REC_TS)),
            const((REC_TS, REC_TS)),
            const((D, 2 * D_RNN)),
            const((CONV_W, D_RNN)),
            const((1, D_RNN)),
            const((N_GATE_TILES, GATE_WIN, 2 * GATE_TILE)),
            const((1, D_RNN)),
            const((1, D_RNN)),
            const((1, D_RNN)),
            const((D_RNN, D)),
            const((1, D)),
            const((1, D)),
        ],
        out_specs=[tile(D), tile(D_HALF)],
        out_shape=[jax.ShapeDtypeStruct((B, S, D), F32), jax.ShapeDtypeStruct((B, S, D_HALF), U32)],
        scratch_shapes=[
            pltpu.VMEM((REC_NSEQ, halo + REC_TS, D_RNN), F32),
            pltpu.VMEM((REC_NSEQ, halo, D_RNN), F32),
            pltpu.VMEM((REC_NSEQ, REC_TS, D_RNN), F32),
            pltpu.VMEM((REC_NSEQ, REC_TS, D_RNN), F32),
            pltpu.VMEM((REC_NSEQ, 1, D_RNN), F32),
        ],
        compiler_params=pltpu.CompilerParams(
            dimension_semantics=("arbitrary", "arbitrary"), vmem_limit_bytes=VMEM_LIMIT),
        name="rglru_ln",
    )(x, perm, perm_t, w_in.astype(BF16), conv_w, row(conv_b), wg, row(b_r), row(b_i), row(lam), w_out.astype(BF16),
      row(ln_g), row(ln_b))


def _router_kernel(x_ref, whi_ref, bias_ref, tri_ref, idx_ref, gate_ref, cnt_ref, base_sc):
    step = pl.program_id(0)
    tr = ROUTE_T

    @pl.when(step == 0)
    def _():
        base_sc[...] = jnp.zeros((N_EXPERTS, 1), F32)

    x = x_ref[...]
    xhi = x.astype(BF16)
    xlo = (x - xhi.astype(F32)).astype(BF16)
    nt = (((1,), (1,)), ((), ()))
    both = lax.dot_general(whi_ref[...], xhi, nt, preferred_element_type=F32)
    logits = (both[:ROUTE_ROWS] + both[ROUTE_ROWS:]
              + lax.dot_general(whi_ref[:ROUTE_ROWS], xlo, nt, preferred_element_type=F32))
    logits = logits + bias_ref[...]

    row8 = lax.broadcasted_iota(jnp.int32, (SUBLANES, tr), 0).astype(F32)
    neg_inf = -jnp.inf
    g = jnp.where(row8 < N_GROUPS, logits[N_EXPERTS:N_EXPERTS + SUBLANES, :], neg_inf)
    gmax = jnp.max(g, axis=0, keepdims=True)
    gidx = jnp.min(jnp.where(g == gmax, row8, SUBLANES), axis=0, keepdims=True)
    g_gate = 1.0 / jnp.sum(jnp.exp(g - gmax), axis=0, keepdims=True)

    esel = logits[0:EXPERTS_PER_GROUP, :]
    for grp in range(1, N_GROUPS):
        esel = jnp.where(gidx == grp, logits[grp * EXPERTS_PER_GROUP:(grp + 1) * EXPERTS_PER_GROUP, :], esel)
    v1 = jnp.max(esel, axis=0, keepdims=True)
    i1 = jnp.min(jnp.where(esel == v1, row8, SUBLANES), axis=0, keepdims=True)
    esel2 = jnp.where(row8 == i1, neg_inf, esel)
    v2 = jnp.max(esel2, axis=0, keepdims=True)
    i2 = jnp.min(jnp.where(esel2 == v2, row8, SUBLANES), axis=0, keepdims=True)
    e21 = jnp.exp(v2 - v1)
    inv = 1.0 / (1.0 + e21)
    gate1 = inv * g_gate
    gate2 = e21 * inv * g_gate
    e1 = gidx * EXPERTS_PER_GROUP + i1
    e2 = gidx * EXPERTS_PER_GROUP + i2

    rowe = lax.broadcasted_iota(jnp.int32, (N_EXPERTS, tr), 0).astype(F32)
    hit1 = rowe == e1
    hit2 = rowe == e2
    member = jnp.where(hit1, 1.0, jnp.where(hit2, 1.0, 0.0))
    before = jnp.dot(member.astype(BF16), tri_ref[...], preferred_element_type=F32) + base_sc[...]
    rank1 = jnp.sum(jnp.where(hit1, before, 0.0), axis=0, keepdims=True)
    rank2 = jnp.sum(jnp.where(hit2, before, 0.0), axis=0, keepdims=True)
    base_sc[...] = base_sc[...] + jnp.sum(member, axis=1, keepdims=True)

    zi = jnp.zeros((1, tr), jnp.int32)
    idx_ref[...] = jnp.concatenate(
        [e1.astype(jnp.int32), e2.astype(jnp.int32), rank1.astype(jnp.int32), rank2.astype(jnp.int32),
         zi, zi, zi, zi], axis=0)
    zf = jnp.zeros((1, tr), F32)
    gate_ref[...] = jnp.concatenate([gate1, gate2, zf, zf, zf, zf, zf, zf], axis=0)
    cnt_ref[...] = jnp.broadcast_to(base_sc[...], (N_EXPERTS, LANES)).astype(jnp.int32)


def _router(xf, w_rg, b_rg, w_re, b_re):
    T, D = xf.shape
    pad_rows = ROUTE_ROWS - N_EXPERTS - N_GROUPS
    w = jnp.concatenate([w_re.T, w_rg.T, jnp.zeros((pad_rows, D), F32)], axis=0)
    whi = w.astype(BF16)
    wlo = (w - whi.astype(F32)).astype(BF16)
    w_split = jnp.concatenate([whi, wlo], axis=0)
    bias = jnp.concatenate([b_re, b_rg, jnp.zeros((pad_rows,), F32)]).reshape(ROUTE_ROWS, 1)
    tri = jnp.asarray(np.triu(np.ones((ROUTE_T, ROUTE_T), np.float32), 1), BF16)
    const = lambda shape: pl.BlockSpec(shape, lambda i: (0,) * len(shape))
    return pl.pallas_call(
        _router_kernel,
        grid=(T // ROUTE_T,),
        in_specs=[
            pl.BlockSpec((ROUTE_T, D), lambda i: (i, 0)),
            const((2 * ROUTE_ROWS, D)),
            const((ROUTE_ROWS, 1)),
            const((ROUTE_T, ROUTE_T)),
        ],
        out_specs=[
            pl.BlockSpec((SUBLANES, ROUTE_T), lambda i: (0, i)),
            pl.BlockSpec((SUBLANES, ROUTE_T), lambda i: (0, i)),
            const((N_EXPERTS, LANES)),
        ],
        out_shape=[
            jax.ShapeDtypeStruct((SUBLANES, T), jnp.int32),
            jax.ShapeDtypeStruct((SUBLANES, T), F32),
            jax.ShapeDtypeStruct((N_EXPERTS, LANES), jnp.int32),
        ],
        scratch_shapes=[pltpu.VMEM((N_EXPERTS, 1), F32)],
        compiler_params=pltpu.CompilerParams(
            dimension_semantics=("arbitrary",), vmem_limit_bytes=VMEM_LIMIT),
        name="router",
    )(xf, w_split, bias, tri)


def _sc_mesh():
    return plsc.VectorSubcoreMesh(core_axis_name="c", subcore_axis_name="s",
                                  num_cores=SC_CORES, num_subcores=SC_SUBCORES)


def _sc_worker_id():
    return lax.axis_index("s") * SC_CORES + lax.axis_index("c")


def _sc_scratch(n_win, width):
    return [
        pltpu.VMEM((n_win, SC_WIN), jnp.int32),
        pltpu.VMEM((n_win, SC_WIN), jnp.int32),
        pltpu.VMEM((2, SC_WIN, width), U32),
        pltpu.SemaphoreType.DMA((2,)),
        pltpu.SemaphoreType.DMA((2,)),
    ]


def _sc_dispatch(rows, idx1, idx2, n_rows):
    _, width = rows.shape
    _, n_win, _ = idx1.shape

    @functools.partial(
        pl.kernel, mesh=_sc_mesh(), out_type=jax.ShapeDtypeStruct((n_rows, width), rows.dtype),
        scratch_types=_sc_scratch(n_win, width), name="sc_dispatch")
    def run(rows_hbm, i1_hbm, i2_hbm, o_hbm, i1_v, i2_v, buf, rsem, wsem):
        wid = _sc_worker_id()
        base = wid * (n_win * SC_WIN)
        pltpu.sync_copy(i1_hbm.at[wid], i1_v)
        pltpu.sync_copy(i2_hbm.at[wid], i2_v)

        def read(j):
            return pltpu.async_copy(rows_hbm.at[pl.ds(base + j * SC_WIN, SC_WIN)], buf.at[j % 2], rsem.at[j % 2])

        reads = {0: read(0)}
        writes = {}
        for j in range(n_win):
            if j + 1 < n_win:
                for d in writes.pop(j - 1, ()):
                    d.wait()
                reads[j + 1] = read(j + 1)
            reads.pop(j).wait()
            writes[j] = (pltpu.async_copy(buf.at[j % 2], o_hbm.at[i1_v.at[j]], wsem.at[j % 2]),
                         pltpu.async_copy(buf.at[j % 2], o_hbm.at[i2_v.at[j]], wsem.at[j % 2]))
        for j in sorted(writes):
            for d in writes[j]:
                d.wait()

    return run(rows, idx1, idx2)


def _sc_gather_pair(table, idx1, idx2):
    _, width = table.shape
    _, n_win, _ = idx1.shape
    n_tok = SC_WORKERS * n_win * SC_WIN
    out_t = jax.ShapeDtypeStruct((n_tok, width), table.dtype)

    @functools.partial(
        pl.kernel, mesh=_sc_mesh(), out_type=(out_t, out_t),
        scratch_types=_sc_scratch(n_win, width), name="sc_combine_gather")
    def run(table_hbm, i1_hbm, i2_hbm, o1_hbm, o2_hbm, i1_v, i2_v, buf, gsem, wsem):
        wid = _sc_worker_id()
        base = wid * (n_win * SC_WIN)
        pltpu.sync_copy(i1_hbm.at[wid], i1_v)
        pltpu.sync_copy(i2_hbm.at[wid], i2_v)
        work = [(i1_v, o1_hbm, j) for j in range(n_win)] + [(i2_v, o2_hbm, j) for j in range(n_win)]

        def gather(t):
            iv, _, j = work[t]
            return pltpu.async_copy(table_hbm.at[iv.at[j]], buf.at[t % 2], gsem.at[t % 2])

        def put(t):
            _, oh, j = work[t]
            return pltpu.async_copy(buf.at[t % 2], oh.at[pl.ds(base + j * SC_WIN, SC_WIN)], wsem.at[t % 2])

        gathers = {0: gather(0)}
        puts = {}
        for t in range(len(work)):
            if t + 1 < len(work):
                if t - 1 in puts:
                    puts.pop(t - 1).wait()
                gathers[t + 1] = gather(t + 1)
            gathers.pop(t).wait()
            puts[t] = put(t)
        for t in sorted(puts):
            puts[t].wait()

    return run(table, idx1, idx2)


def _moe_kernel(layer, be_ref, slot_ref, nxt_ref, nused_ref, x_ref, w1_hbm, w3_hbm, w2_hbm, o_ref,
                w1_st, w3_st, w2_st, w1_sc, w3_sc, w2_sc, sems):
    i = pl.program_id(0)
    expert = be_ref[i]
    slot = slot_ref[i]
    new_expert = jnp.logical_or(i == 0, expert != be_ref[jnp.maximum(i - 1, 0)])

    def weight_copies(e, sl):
        return [pltpu.make_async_copy(hbm.at[layer, e], stage.at[sl], sems.at[k, sl])
                for k, (hbm, stage) in enumerate(((w1_hbm, w1_st), (w3_hbm, w3_st), (w2_hbm, w2_st)))]

    @pl.when(i == 0)
    def _():
        for cp in weight_copies(expert, slot):
            cp.start()

    @pl.when(new_expert)
    def _():
        for cp in weight_copies(expert, slot):
            cp.wait()
        w1_sc[...] = w1_st[slot].astype(BF16)
        w3_sc[...] = w3_st[slot].astype(BF16)
        w2_sc[...] = w2_st[slot].astype(BF16)
        nxt = nxt_ref[i]

        @pl.when(nxt >= 0)
        def _():
            for cp in weight_copies(nxt, 1 - slot):
                cp.start()

    @pl.when(i < nused_ref[0])
    def _():
        def up(rows):
            xa, xb = _unpack_bf16_pair(x_ref[rows, :])
            xa = xa.astype(BF16)
            xb = xb.astype(BF16)
            h1 = (jnp.dot(xa, w1_sc[:D_HALF], preferred_element_type=F32)
                  + jnp.dot(xb, w1_sc[D_HALF:], preferred_element_type=F32))
            h3 = (jnp.dot(xa, w3_sc[:D_HALF], preferred_element_type=F32)
                  + jnp.dot(xb, w3_sc[D_HALF:], preferred_element_type=F32))
            return h1, h3

        def down(rows, h1, h3):
            hdn = (jax.nn.silu(h1) * h3).astype(BF16)
            y = jnp.dot(hdn, w2_sc[...], preferred_element_type=F32)
            o_ref[rows, :] = _pack_row_halves(y)

        subs = [slice(k * MOE_SUB, (k + 1) * MOE_SUB) for k in range(MOE_BM // MOE_SUB)]
        ups = [up(rows) for rows in subs]
        for rows, (h1, h3) in zip(subs, ups):
            down(rows, h1, h3)

    @pl.when(i >= nused_ref[0])
    def _():
        o_ref[...] = jnp.zeros(o_ref.shape, o_ref.dtype)


def _moe_blocks(xbuf, block_e, n_used, w1, w3, w2, layer):
    n_rows, _ = xbuf.shape
    D = D_MODEL
    n_blocks = n_rows // MOE_BM
    pos = jnp.arange(n_blocks, dtype=jnp.int32)
    is_new = jnp.concatenate([jnp.ones((1,), bool), block_e[1:] != block_e[:-1]])
    slot = ((jnp.cumsum(is_new.astype(jnp.int32)) - 1) % 2).astype(jnp.int32)
    change_pos = jnp.where(is_new, pos, n_blocks)
    next_change = jnp.concatenate([lax.cummin(change_pos, reverse=True)[1:], jnp.full((1,), n_blocks, jnp.int32)])
    nxt = jnp.where(next_change < n_blocks, block_e[jnp.minimum(next_change, n_blocks - 1)], -1).astype(jnp.int32)
    rows = lambda i, be, sl, nx, nu: (i, 0)
    grid_spec = pltpu.PrefetchScalarGridSpec(
        num_scalar_prefetch=4,
        grid=(n_blocks,),
        in_specs=[
            pl.BlockSpec((MOE_BM, D_HALF), rows),
            pl.BlockSpec(memory_space=pl.ANY),
            pl.BlockSpec(memory_space=pl.ANY),
            pl.BlockSpec(memory_space=pl.ANY),
        ],
        out_specs=pl.BlockSpec((MOE_BM, D_HALF), rows),
        scratch_shapes=[
            pltpu.VMEM((2, D, D_EXPERT), F32),
            pltpu.VMEM((2, D, D_EXPERT), F32),
            pltpu.VMEM((2, D_EXPERT, D), F32),
            pltpu.VMEM((D, D_EXPERT), BF16),
            pltpu.VMEM((D, D_EXPERT), BF16),
            pltpu.VMEM((D_EXPERT, D), BF16),
            pltpu.SemaphoreType.DMA((3, 2)),
        ],
    )
    return pl.pallas_call(
        functools.partial(_moe_kernel, layer),
        grid_spec=grid_spec,
        out_shape=jax.ShapeDtypeStruct((n_rows, D_HALF), U32),
        compiler_params=pltpu.CompilerParams(
            dimension_semantics=("arbitrary",), vmem_limit_bytes=VMEM_LIMIT),
        name="moe_experts",
    )(block_e, slot, nxt, n_used, xbuf, w1, w3, w2)


def _moe_combine_norm(x, y1, y2, gate_rows, g, b):
    n = x.shape[0]
    pad = jnp.zeros((LANES - SUBLANES, n), F32)
    gates = jnp.concatenate([gate_rows, pad], axis=0).T
    g1 = gates[:, 0:1]
    g2 = gates[:, 1:2]
    a1, b1 = _unpack_bf16_pair(y1)
    a2, b2 = _unpack_bf16_pair(y2)
    f = jnp.concatenate([g1 * a1 + g2 * a2, g1 * b1 + g2 * b2], axis=1)
    return _layer_norm_rows(ALPHA * x + f, g, b)


def _combine_kernel(x_ref, y1_ref, y2_ref, gates_ref, g_ref, b_ref, o_ref):
    o_ref[...] = _moe_combine_norm(x_ref[...], y1_ref[...], y2_ref[...], gates_ref[...], g_ref[...], b_ref[...])


def _combine_ln(xf, y1, y2, gates, ln_g, ln_b):
    T, D = xf.shape
    const = lambda shape: pl.BlockSpec(shape, lambda i: (0,) * len(shape))
    rows = lambda w: pl.BlockSpec((COMB_T, w), lambda i: (i, 0))
    return pl.pallas_call(
        _combine_kernel,
        grid=(T // COMB_T,),
        in_specs=[rows(D), rows(D_HALF), rows(D_HALF), pl.BlockSpec((SUBLANES, COMB_T), lambda i: (0, i)),
                  const((1, D)), const((1, D))],
        out_specs=rows(D),
        out_shape=jax.ShapeDtypeStruct((T, D), F32),
        compiler_params=pltpu.CompilerParams(
            dimension_semantics=("arbitrary",), vmem_limit_bytes=VMEM_LIMIT),
        name="moe_combine_ln",
    )(xf, y1, y2, gates, ln_g.reshape(1, D), ln_b.reshape(1, D))


def _hier_moe(x, xpk, w_rg, b_rg, w_re, b_re, w1, w3, w2, layer):
    B, S, D = x.shape
    T = B * S
    xf = x.reshape(T, D)
    idx, gates, cnt = _router(xf, w_rg, b_rg, w_re, b_re)
    counts = cnt[:, 0]
    padded = ((counts + MOE_BM - 1) // MOE_BM) * MOE_BM
    pends = jnp.cumsum(padded)
    pstarts = pends - padded
    experts = jnp.arange(N_EXPERTS, dtype=jnp.int32)[:, None]

    def dest(e_row, rank_row):
        return jnp.sum(jnp.where(e_row[None, :] == experts, pstarts[:, None], 0), axis=0) + rank_row

    n_win = T // (SC_WORKERS * SC_WIN)
    dest1 = dest(idx[0], idx[2]).reshape(SC_WORKERS, n_win, SC_WIN)
    dest2 = dest(idx[1], idx[3]).reshape(SC_WORKERS, n_win, SC_WIN)
    n_blocks = -(-(T * TOP_K + N_EXPERTS * (MOE_BM - 1)) // MOE_BM)
    n_rows = n_blocks * MOE_BM
    block_start = jnp.arange(n_blocks, dtype=jnp.int32) * MOE_BM
    block_e = jnp.minimum(jnp.sum(block_start[:, None] >= pends[None, :], axis=1), N_EXPERTS - 1).astype(jnp.int32)
    n_used = (pends[-1] // MOE_BM).astype(jnp.int32).reshape(1)
    xbuf = _sc_dispatch(xpk.reshape(T, D_HALF), dest1, dest2, n_rows)
    ybuf = _moe_blocks(xbuf, block_e, n_used, w1, w3, w2, layer)
    y1, y2 = _sc_gather_pair(ybuf, dest1, dest2)
    return y1, y2, gates


def _att_head_order():
    order = []
    for p in range(N_HEADS // 2):
        jj, m = divmod(p, 4)
        order += [8 * jj + m, 8 * jj + 4 + m]
    return order


ATT_HEAD_ORDER = _att_head_order()


def _attn_kernel(xprev_ref, y1_ref, y2_ref, gates_a_ref, gates_b_ref, g_prev_ref, b_prev_ref, wqkv_ref, bias_ref,
                 sink_ref, wo_ref, g_ref, b_ref, o_ref, opk_ref, kv_ext, o_sc, s_sc0, s_sc1, p_sc0, p_sc1):
    s = pl.program_id(1)
    tq = ATT_TQ
    s_bufs = (s_sc0, s_sc1)
    p_bufs = (p_sc0, p_sc1)
    gate_refs = (gates_a_ref, gates_b_ref)
    assert ATT_NSEQ == len(gate_refs)

    @pl.when(s == 0)
    def _():
        kv_ext[:, 0:WINDOW, :] = jnp.zeros((ATT_NSEQ, WINDOW, 2 * KV_DIM), BF16)

    def layer_input(sq):
        return _moe_combine_norm(xprev_ref[sq], y1_ref[sq], y2_ref[sq], gate_refs[sq][...],
                                 g_prev_ref[...], b_prev_ref[...])

    def project_qkv(sq, x):
        qkv = jnp.dot(x.astype(BF16), wqkv_ref[...], preferred_element_type=F32)
        kv_ext[sq, WINDOW:WINDOW + tq, :] = qkv[:, Q_DIM:].astype(BF16)
        return (qkv[:, :Q_DIM] * (HEAD_DIM ** -0.5 * LOG2E)).astype(BF16)

    lane = lax.broadcasted_iota(jnp.int32, (WINDOW, LANES), 1)
    low = lane < HEAD_DIM
    sub = lax.broadcasted_iota(jnp.int32, (LANES, WINDOW), 0)
    top = sub < HEAD_DIM
    first = jnp.where(s == 0, 1, 0)
    nt = (((1,), (1,)), ((), ()))
    zero = jnp.zeros((), BF16)

    tiles = [(sq, n, j) for n in range(ATT_NB) for j in range(2) for sq in range(ATT_NSEQ)]

    def scores(t, qs_all):
        sq, n, j = tiles[t]
        q = qs_all[sq]
        r0 = n * WINDOW
        k_tile = kv_ext[sq, r0:r0 + 2 * WINDOW, j * LANES:(j + 1) * LANES]
        parts = []
        for m in range(4):
            p = 4 * j + m
            qt = q[r0:r0 + WINDOW, p * LANES:(p + 1) * LANES]
            parts.append(jnp.where(low, qt, zero))
            parts.append(jnp.where(low, zero, qt))
        qs = jnp.concatenate(parts, axis=0)
        bias_sel = first if n == 0 else 0
        s_bufs[t % 2][...] = (lax.dot_general(k_tile, qs, nt, preferred_element_type=F32)
                              + bias_ref[bias_sel, j])

    def softmax_pv(t):
        sq, n, j = tiles[t]
        r0 = n * WINDOW
        s_sc = s_bufs[t % 2]
        p_sc = p_bufs[t % 2]
        inv_l = []
        for h in range(8):
            hc = slice(h * WINDOW, (h + 1) * WINDOW)
            sink = sink_ref[8 * j + h] * LOG2E
            mx = jnp.maximum(jnp.max(s_sc[:, hc], axis=0, keepdims=True), sink)
            pr = jnp.exp2(s_sc[:, hc] - mx)
            p_sc[:, hc] = pr.astype(BF16)
            inv_l.append(1.0 / (jnp.sum(pr, axis=0, keepdims=True) + jnp.exp2(sink - mx)))
        v_tile = kv_ext[sq, r0:r0 + 2 * WINDOW, KV_DIM + j * LANES:KV_DIM + (j + 1) * LANES]
        v_t = v_tile.astype(F32).T.astype(BF16)
        ov = jnp.dot(v_t, p_sc[...], preferred_element_type=F32)
        for m in range(4):
            p = 4 * j + m
            o_even = ov[:, (2 * m) * WINDOW:(2 * m + 1) * WINDOW] * inv_l[2 * m]
            o_odd = ov[:, (2 * m + 1) * WINDOW:(2 * m + 2) * WINDOW] * inv_l[2 * m + 1]
            o_sc[sq, p * LANES:(p + 1) * LANES, r0:r0 + WINDOW] = jnp.where(top, o_even, o_odd).astype(BF16)

    def project_out(sq):
        return jnp.dot(wo_ref[...], o_sc[sq], preferred_element_type=F32)

    def finish(sq, x, out_t):
        xn = _layer_norm_rows(ALPHA * x + out_t.T, g_ref[...], b_ref[...])
        o_ref[sq] = xn
        opk_ref[sq] = _pack_row_halves(xn)

    xs = [layer_input(0)]
    qs_all = [project_qkv(0, xs[0])]
    xs.append(layer_input(1))
    qs_all.append(project_qkv(1, xs[1]))
    scores(0, qs_all)
    for t in range(len(tiles)):
        if t + 1 < len(tiles):
            scores(t + 1, qs_all)
        softmax_pv(t)
    kv_ext[:, 0:WINDOW, :] = kv_ext[:, tq:tq + WINDOW, :]
    outs_t = [project_out(sq) for sq in range(ATT_NSEQ)]
    for sq in range(ATT_NSEQ):
        finish(sq, xs[sq], outs_t[sq])


def _attn_bias():
    qi = np.arange(WINDOW)[:, None]
    sj = np.arange(2 * WINDOW)[None, :]
    dist = qi - sj + WINDOW
    valid = (dist >= 0) & (dist < WINDOW)
    slopes = 2.0 ** (-8.0 * np.arange(1, N_HEADS + 1, dtype=np.float32) / N_HEADS)
    slopes = slopes.astype(np.float32)[ATT_HEAD_ORDER]
    sb = -(slopes[:, None, None] * dist.astype(np.float32)[None])
    later = np.where(valid[None], sb, -np.inf)
    first = np.where((valid & (sj >= WINDOW))[None], sb, -np.inf)
    bias = np.stack([later, first]).astype(np.float32) * np.float32(LOG2E)
    bias = bias.reshape(2, 2, 8, WINDOW, 2 * WINDOW).transpose(0, 1, 4, 2, 3).reshape(2, 2, 2 * WINDOW, 8 * WINDOW)
    return jnp.asarray(np.ascontiguousarray(bias))


def _attn_layer(x_prev, y1, y2, gates, g_prev, b_prev, w_qkv, sinks, w_o, ln_g, ln_b):
    B, S, D = x_prev.shape
    steps = S // ATT_TQ
    assert ATT_HEAD_ORDER == list(np.arange(N_HEADS).reshape(2, 2, 4).transpose(0, 2, 1).reshape(-1))
    wq = w_qkv[:, :Q_DIM].reshape(D, 2, 2, 4, HEAD_DIM).transpose(0, 1, 3, 2, 4).reshape(D, Q_DIM)
    wqkv = jnp.concatenate([wq, w_qkv[:, Q_DIM:]], axis=1).astype(BF16)
    wo_t = w_o.reshape(2, 2, 4, HEAD_DIM, D).transpose(0, 2, 1, 3, 4).reshape(Q_DIM, D).T.astype(BF16)
    sink = sinks.reshape(2, 2, 4).transpose(0, 2, 1).reshape(N_HEADS, 1, 1)
    bias = _attn_bias()
    const = lambda shape: pl.BlockSpec(shape, lambda b, s: (0,) * len(shape))
    tile = lambda w: pl.BlockSpec((ATT_NSEQ, ATT_TQ, w), lambda b, s: (b, s, 0))
    gate_rows = lambda sq: pl.BlockSpec((SUBLANES, ATT_TQ), lambda b, s: (0, (ATT_NSEQ * b + sq) * steps + s))
    return pl.pallas_call(
        _attn_kernel,
        grid=(B // ATT_NSEQ, steps),
        in_specs=[
            tile(D),
            tile(D_HALF),
            tile(D_HALF),
            gate_rows(0),
            gate_rows(1),
            const((1, D)),
            const((1, D)),
            const((D, Q_DIM + 2 * KV_DIM)),
            const((2, 2, 2 * WINDOW, 8 * WINDOW)),
            const((N_HEADS, 1, 1)),
            const((D, Q_DIM)),
            const((1, D)),
            const((1, D)),
        ],
        out_specs=[tile(D), tile(D_HALF)],
        out_shape=[jax.ShapeDtypeStruct((B, S, D), F32), jax.ShapeDtypeStruct((B, S, D_HALF), U32)],
        scratch_shapes=[
            pltpu.VMEM((ATT_NSEQ, ATT_TQ + WINDOW, 2 * KV_DIM), BF16),
            pltpu.VMEM((ATT_NSEQ, Q_DIM, ATT_TQ), BF16),
            pltpu.VMEM((2 * WINDOW, 8 * WINDOW), F32),
            pltpu.VMEM((2 * WINDOW, 8 * WINDOW), F32),
            pltpu.VMEM((2 * WINDOW, 8 * WINDOW), BF16),
            pltpu.VMEM((2 * WINDOW, 8 * WINDOW), BF16),
        ],
        compiler_params=pltpu.CompilerParams(
            dimension_semantics=("arbitrary", "arbitrary"), vmem_limit_bytes=VMEM_LIMIT),
        name="swa_attn_ln",
    )(x_prev, y1.reshape(B, S, D_HALF), y2.reshape(B, S, D_HALF), gates, gates, g_prev.reshape(1, D),
      b_prev.reshape(1, D),
      wqkv, bias, sink, wo_t, ln_g.reshape(1, D), ln_b.reshape(1, D))


def kernel(x, rec_w_in, rec_conv_w, rec_conv_b, rec_w_r, rec_b_r, rec_w_i, rec_b_i, rec_lambda, rec_w_out,
           att_w_qkv, att_sinks, att_w_o, moe_w_group, moe_b_group, moe_w_expert, moe_b_expert,
           moe_w1, moe_w3, moe_w2, ln_g, ln_b):
    assert DEPTH == 2
    B, S, D = x.shape

    def moe(layer, xin, xin_pk):
        return _hier_moe(xin, xin_pk, moe_w_group[layer], moe_b_group[layer], moe_w_expert[layer],
                         moe_b_expert[layer], moe_w1, moe_w3, moe_w2, layer)

    x1, x1_pk = _rglru_layer(x, rec_w_in[0], rec_conv_w[0], rec_conv_b[0], rec_w_r[0], rec_b_r[0], rec_w_i[0],
                             rec_b_i[0], rec_lambda[0], rec_w_out[0], ln_g[0, 0], ln_b[0, 0])
    y1, y2, gates = moe(0, x1, x1_pk)
    x3, x3_pk = _attn_layer(x1, y1, y2, gates, ln_g[0, 1], ln_b[0, 1], att_w_qkv[0], att_sinks[0], att_w_o[0],
                            ln_g[1, 0], ln_b[1, 0])
    y1, y2, gates = moe(1, x3, x3_pk)
    out = _combine_ln(x3.reshape(B * S, D), y1, y2, gates, ln_g[1, 1], ln_b[1, 1])
    return out.reshape(B, S, D)
```

```python
import functools

import jax
import jax.numpy as jnp
import numpy as np
from jax import lax
from jax.experimental import pallas as pl
from jax.experimental.pallas import tpu as pltpu
from jax.experimental.pallas import tpu_sc as plsc

F32 = jnp.float32
BF16 = jnp.bfloat16
U32 = jnp.uint32

D_MODEL = 1024
DEPTH = 2
D_RNN = 1280
LRU_BLOCKS = 16
LRU_BLOCK_W = D_RNN // LRU_BLOCKS
CONV_W = 4
LRU_C = 8.0
N_HEADS = 16
N_KV_HEADS = 4
HEAD_DIM = 64
WINDOW = 128
Q_DIM = N_HEADS * HEAD_DIM
KV_DIM = N_KV_HEADS * HEAD_DIM
N_GROUPS = 4
EXPERTS_PER_GROUP = 8
N_EXPERTS = N_GROUPS * EXPERTS_PER_GROUP
TOP_K = 2
D_EXPERT = 512
ALPHA = (2 * DEPTH) ** 0.25
LN_EPS = 1e-5
LOG2E = 1.4426950408889634

LANES = 128
SUBLANES = 8
VMEM_LIMIT = 56 * 1024 * 1024

REC_TS = 256
REC_GROUPS = REC_TS // SUBLANES
REC_NSEQ = 2
GATE_TILE = 256
GATE_WIN = 512
GATE_WIN_STARTS = (0, 128, 384, 640, 768)
N_GATE_TILES = D_RNN // GATE_TILE

ROUTE_T = 512
ROUTE_ROWS = 40

MOE_BM = 512
MOE_SUB = 256

ATT_TQ = 256
ATT_NB = ATT_TQ // WINDOW
ATT_NSEQ = 2

COMB_T = 1024

D_HALF = D_MODEL // 2

SC_CORES = 2
SC_SUBCORES = 16
SC_WORKERS = SC_CORES * SC_SUBCORES
SC_WIN = 64


def _layer_norm_rows(z, g, b):
    mu = jnp.mean(z, axis=-1, keepdims=True)
    zc = z - mu
    var = jnp.mean(zc * zc, axis=-1, keepdims=True)
    return zc * lax.rsqrt(var + LN_EPS) * g + b


def _pack_bf16_pair(a, b):
    ua = lax.bitcast_convert_type(a.astype(BF16).astype(F32), U32)
    ub = lax.bitcast_convert_type(b.astype(BF16).astype(F32), U32)
    return (ua >> 16) | (ub & jnp.uint32(0xFFFF0000))


def _unpack_bf16_pair(w):
    a = lax.bitcast_convert_type(w << 16, F32)
    b = lax.bitcast_convert_type(w & jnp.uint32(0xFFFF0000), F32)
    return a, b


def _pack_row_halves(x):
    return _pack_bf16_pair(x[:, :D_HALF], x[:, D_HALF:])


def _rglru_kernel(x_ref, perm_ref, perm_t_ref, w_in_ref, convw_ref, convb_ref, wg_ref, br_ref, bi_ref, lam_ref,
                  w_out_ref, g_ref, b_ref, o_ref, opk_ref, xr_ext, tail_sc, a_sc, u_sc, h_carry, gate_sc):
    s = pl.program_id(1)
    ts = REC_TS
    halo = (CONV_W - 1) * SUBLANES
    seqs = range(REC_NSEQ)

    @pl.when(s == 0)
    def _():
        tail_sc[...] = jnp.zeros((REC_NSEQ, halo, D_RNN), F32)
        h_carry[...] = jnp.zeros((REC_NSEQ, 1, D_RNN), F32)

    row = lax.broadcasted_iota(jnp.int32, (SUBLANES, D_RNN), 0)
    nlam = -lam_ref[...]
    sp = jnp.maximum(nlam, 0.0) + jnp.log1p(jnp.exp(-jnp.abs(nlam)))
    log2a_scale = (-LRU_C * LOG2E) * sp

    def project(q):
        xp = jnp.dot(perm_ref[...], x_ref[q].astype(BF16), preferred_element_type=F32).astype(BF16)
        proj = jnp.dot(xp, w_in_ref[...], preferred_element_type=F32)
        gate_sc[q] = proj[:, :D_RNN]
        return proj[:, D_RNN:]

    def conv_gates(q, xr):
        for k in range(CONV_W - 1):
            r0 = ts - halo + k * SUBLANES
            cur = xr[r0:r0 + SUBLANES, :]
            prev = tail_sc[q, k * SUBLANES:(k + 1) * SUBLANES, :]
            xr_ext[q, k * SUBLANES:(k + 1) * SUBLANES, :] = jnp.where(
                row == 0, pltpu.roll(prev, 1, axis=0), pltpu.roll(cur, 1, axis=0))
        tail_sc[q] = xr[ts - halo:, :]
        xr_ext[q, halo:halo + ts, :] = xr
        xc = convb_ref[...] + convw_ref[CONV_W - 1:CONV_W, :] * xr
        for k in range(CONV_W - 1):
            xc = xc + convw_ref[k:k + 1, :] * xr_ext[q, k * SUBLANES:k * SUBLANES + ts, :]
        xcb = xc.astype(BF16)
        pres = [jnp.dot(xcb[:, GATE_WIN_STARTS[j]:GATE_WIN_STARTS[j] + GATE_WIN], wg_ref[j],
                        preferred_element_type=F32) for j in range(N_GATE_TILES)]
        for j, pre in enumerate(pres):
            cs = j * GATE_TILE
            r = jax.nn.sigmoid(pre[:, :GATE_TILE] + br_ref[:, cs:cs + GATE_TILE])
            i = jax.nn.sigmoid(pre[:, GATE_TILE:] + bi_ref[:, cs:cs + GATE_TILE])
            a = jnp.exp2(r * log2a_scale[:, cs:cs + GATE_TILE])
            s1 = 1.0 - a * a
            mult = jnp.where(s1 > 0.0, s1 * lax.rsqrt(s1), 0.0)
            u = mult * (i * xc[:, cs:cs + GATE_TILE])
            a_sc[q, :, cs:cs + GATE_TILE] = a
            u_sc[q, :, cs:cs + GATE_TILE] = u

    def segment_scan(q):
        h = jnp.zeros((SUBLANES, D_RNN), F32)
        prod = jnp.ones((SUBLANES, D_RNN), F32)
        for gidx in range(REC_GROUPS):
            rows = slice(gidx * SUBLANES, (gidx + 1) * SUBLANES)
            a8 = a_sc[q, rows, :]
            h = a8 * h + u_sc[q, rows, :]
            prod = a8 * prod
            u_sc[q, rows, :] = h
            a_sc[q, rows, :] = prod
        return h, prod

    def recur_out(q, seg):
        seg_h, seg_a = seg
        for d in (1, 2, 4):
            keep = row >= d
            a_sh = jnp.where(keep, pltpu.roll(seg_a, d, axis=0), 1.0)
            h_sh = jnp.where(keep, pltpu.roll(seg_h, d, axis=0), 0.0)
            seg_h = seg_a * h_sh + seg_h
            seg_a = seg_a * a_sh
        h_in = h_carry[q]
        after = seg_a * h_in + seg_h
        enter = jnp.where(row == 0, h_in, pltpu.roll(after, 1, axis=0))
        h_carry[q] = after[SUBLANES - 1:SUBLANES, :]
        hs = (u_sc[q].reshape(REC_GROUPS, SUBLANES, D_RNN)
              + a_sc[q].reshape(REC_GROUPS, SUBLANES, D_RNN) * enter[None]).reshape(ts, D_RNN)
        y = hs * jax.nn.gelu(gate_sc[q])
        y_t = jnp.dot(perm_t_ref[...], y.astype(BF16), preferred_element_type=F32).astype(BF16)
        return jnp.dot(y_t, w_out_ref[...], preferred_element_type=F32)

    def finish(q, out):
        z = ALPHA * x_ref[q] + out
        xn = _layer_norm_rows(z, g_ref[...], b_ref[...])
        o_ref[q] = xn
        opk_ref[q] = _pack_row_halves(xn)

    assert REC_NSEQ == 2
    xr_a = project(0)
    conv_gates(0, xr_a)
    xr_b = project(1)
    out_a = recur_out(0, segment_scan(0))
    conv_gates(1, xr_b)
    finish(0, out_a)
    out_b = recur_out(1, segment_scan(1))
    finish(1, out_b)


def _band_gate_weights(w_r, w_i):
    spread = jnp.asarray(np.tile(np.eye(LRU_BLOCK_W, dtype=np.float32), (1, LRU_BLOCKS)), BF16)
    blk = np.arange(D_RNN) // LRU_BLOCK_W
    on_diag = jnp.asarray(blk[:, None] == blk[None, :])

    def dense(w):
        rows = w.reshape(D_RNN, LRU_BLOCK_W).astype(BF16)
        return jnp.where(on_diag, jnp.dot(rows, spread, preferred_element_type=F32), 0.0)

    wr, wi = dense(w_r), dense(w_i)
    tiles = []
    for j in range(N_GATE_TILES):
        ws = GATE_WIN_STARTS[j]
        cs = j * GATE_TILE
        lo_blk = cs // LRU_BLOCK_W
        hi_blk = (cs + GATE_TILE - 1) // LRU_BLOCK_W
        assert ws <= lo_blk * LRU_BLOCK_W and (hi_blk + 1) * LRU_BLOCK_W <= ws + GATE_WIN
        tiles.append(jnp.concatenate([wr[ws:ws + GATE_WIN, cs:cs + GATE_TILE],
                                      wi[ws:ws + GATE_WIN, cs:cs + GATE_TILE]], axis=1))
    return jnp.stack(tiles).astype(BF16)


def _rglru_layer(x, w_in, conv_w, conv_b, w_r, b_r, w_i, b_i, lam, w_out, ln_g, ln_b):
    B, S, D = x.shape
    wg = _band_gate_weights(w_r, w_i)
    rho = np.arange(REC_TS)
    perm_np = np.zeros((REC_TS, REC_TS), np.float32)
    perm_np[rho, (rho % SUBLANES) * REC_GROUPS + rho // SUBLANES] = 1.0
    perm = jnp.asarray(perm_np, BF16)
    perm_t = jnp.asarray(perm_np.T, BF16)
    row = lambda v: v.reshape(1, -1)
    const = lambda shape: pl.BlockSpec(shape, lambda b, s: (0,) * len(shape))
    tile = lambda w: pl.BlockSpec((REC_NSEQ, REC_TS, w), lambda b, s: (b, s, 0))
    halo = (CONV_W - 1) * SUBLANES
    return pl.pallas_call(
        _rglru_kernel,
        grid=(B // REC_NSEQ, S // REC_TS),
        in_specs=[
            tile(D),
            const((REC_TS, REC_TS)),
            const((REC_TS, REC_TS)),
            const((D, 2 * D_RNN)),
            const((CONV_W, D_RNN)),
            const((1, D_RNN)),
            const((N_GATE_TILES, GATE_WIN, 2 * GATE_TILE)),
            const((1, D_RNN)),
            const((1, D_RNN)),
            const((1, D_RNN)),
            const((D_RNN, D)),
            const((1, D)),
            const((1, D)),
        ],
        out_specs=[tile(D), tile(D_HALF)],
        out_shape=[jax.ShapeDtypeStruct((B, S, D), F32), jax.ShapeDtypeStruct((B, S, D_HALF), U32)],
        scratch_shapes=[
            pltpu.VMEM((REC_NSEQ, halo + REC_TS, D_RNN), F32),
            pltpu.VMEM((REC_NSEQ, halo, D_RNN), F32),
            pltpu.VMEM((REC_NSEQ, REC_TS, D_RNN), F32),
            pltpu.VMEM((REC_NSEQ, REC_TS, D_RNN), F32),
            pltpu.VMEM((REC_NSEQ, 1, D_RNN), F32),
            pltpu.VMEM((REC_NSEQ, REC_TS, D_RNN), F32),
        ],
        compiler_params=pltpu.CompilerParams(
            dimension_semantics=("arbitrary", "arbitrary"), vmem_limit_bytes=VMEM_LIMIT),
        name="rglru_ln",
    )(x, perm, perm_t, w_in.astype(BF16), conv_w, row(conv_b), wg, row(b_r), row(b_i), row(lam), w_out.astype(BF16),
      row(ln_g), row(ln_b))


def _router_kernel(x_ref, whi_ref, bias_ref, tri_ref, idx_ref, gate_ref, cnt_ref, base_sc):
    step = pl.program_id(0)
    tr = ROUTE_T

    @pl.when(step == 0)
    def _():
        base_sc[...] = jnp.zeros((N_EXPERTS, 1), F32)

    x = x_ref[...]
    xhi = x.astype(BF16)
    xlo = (x - xhi.astype(F32)).astype(BF16)
    nt = (((1,), (1,)), ((), ()))
    both = lax.dot_general(whi_ref[...], xhi, nt, preferred_element_type=F32)
    logits = (both[:ROUTE_ROWS] + both[ROUTE_ROWS:]
              + lax.dot_general(whi_ref[:ROUTE_ROWS], xlo, nt, preferred_element_type=F32))
    logits = logits + bias_ref[...]

    row8 = lax.broadcasted_iota(jnp.int32, (SUBLANES, tr), 0).astype(F32)
    neg_inf = -jnp.inf
    g = jnp.where(row8 < N_GROUPS, logits[N_EXPERTS:N_EXPERTS + SUBLANES, :], neg_inf)
    gmax = jnp.max(g, axis=0, keepdims=True)
    gidx = jnp.min(jnp.where(g == gmax, row8, SUBLANES), axis=0, keepdims=True)
    g_gate = 1.0 / jnp.sum(jnp.exp(g - gmax), axis=0, keepdims=True)

    esel = logits[0:EXPERTS_PER_GROUP, :]
    for grp in range(1, N_GROUPS):
        esel = jnp.where(gidx == grp, logits[grp * EXPERTS_PER_GROUP:(grp + 1) * EXPERTS_PER_GROUP, :], esel)
    v1 = jnp.max(esel, axis=0, keepdims=True)
    i1 = jnp.min(jnp.where(esel == v1, row8, SUBLANES), axis=0, keepdims=True)
    esel2 = jnp.where(row8 == i1, neg_inf, esel)
    v2 = jnp.max(esel2, axis=0, keepdims=True)
    i2 = jnp.min(jnp.where(esel2 == v2, row8, SUBLANES), axis=0, keepdims=True)
    e21 = jnp.exp(v2 - v1)
    inv = 1.0 / (1.0 + e21)
    gate1 = inv * g_gate
    gate2 = e21 * inv * g_gate
    e1 = gidx * EXPERTS_PER_GROUP + i1
    e2 = gidx * EXPERTS_PER_GROUP + i2

    rowe = lax.broadcasted_iota(jnp.int32, (N_EXPERTS, tr), 0).astype(F32)
    hit1 = rowe == e1
    hit2 = rowe == e2
    member = jnp.where(hit1, 1.0, jnp.where(hit2, 1.0, 0.0))
    before = jnp.dot(member.astype(BF16), tri_ref[...], preferred_element_type=F32) + base_sc[...]
    rank1 = jnp.sum(jnp.where(hit1, before, 0.0), axis=0, keepdims=True)
    rank2 = jnp.sum(jnp.where(hit2, before, 0.0), axis=0, keepdims=True)
    base_sc[...] = base_sc[...] + jnp.sum(member, axis=1, keepdims=True)

    zi = jnp.zeros((1, tr), jnp.int32)
    idx_ref[...] = jnp.concatenate(
        [e1.astype(jnp.int32), e2.astype(jnp.int32), rank1.astype(jnp.int32), rank2.astype(jnp.int32),
         zi, zi, zi, zi], axis=0)
    zf = jnp.zeros((1, tr), F32)
    gate_ref[...] = jnp.concatenate([gate1, gate2, zf, zf, zf, zf, zf, zf], axis=0)
    cnt_ref[...] = jnp.broadcast_to(base_sc[...], (N_EXPERTS, LANES)).astype(jnp.int32)


def _router(xf, w_rg, b_rg, w_re, b_re):
    T, D = xf.shape
    pad_rows = ROUTE_ROWS - N_EXPERTS - N_GROUPS
    w = jnp.concatenate([w_re.T, w_rg.T, jnp.zeros((pad_rows, D), F32)], axis=0)
    whi = w.astype(BF16)
    wlo = (w - whi.astype(F32)).astype(BF16)
    w_split = jnp.concatenate([whi, wlo], axis=0)
    bias = jnp.concatenate([b_re, b_rg, jnp.zeros((pad_rows,), F32)]).reshape(ROUTE_ROWS, 1)
    tri = jnp.asarray(np.triu(np.ones((ROUTE_T, ROUTE_T), np.float32), 1), BF16)
    const = lambda shape: pl.BlockSpec(shape, lambda i: (0,) * len(shape))
    return pl.pallas_call(
        _router_kernel,
        grid=(T // ROUTE_T,),
        in_specs=[
            pl.BlockSpec((ROUTE_T, D), lambda i: (i, 0)),
            const((2 * ROUTE_ROWS, D)),
            const((ROUTE_ROWS, 1)),
            const((ROUTE_T, ROUTE_T)),
        ],
        out_specs=[
            pl.BlockSpec((SUBLANES, ROUTE_T), lambda i: (0, i)),
            pl.BlockSpec((SUBLANES, ROUTE_T), lambda i: (0, i)),
            const((N_EXPERTS, LANES)),
        ],
        out_shape=[
            jax.ShapeDtypeStruct((SUBLANES, T), jnp.int32),
            jax.ShapeDtypeStruct((SUBLANES, T), F32),
            jax.ShapeDtypeStruct((N_EXPERTS, LANES), jnp.int32),
        ],
        scratch_shapes=[pltpu.VMEM((N_EXPERTS, 1), F32)],
        compiler_params=pltpu.CompilerParams(
            dimension_semantics=("arbitrary",), vmem_limit_bytes=VMEM_LIMIT),
        name="router",
    )(xf, w_split, bias, tri)


def _sc_mesh():
    return plsc.VectorSubcoreMesh(core_axis_name="c", subcore_axis_name="s",
                                  num_cores=SC_CORES, num_subcores=SC_SUBCORES)


def _sc_worker_id():
    return lax.axis_index("s") * SC_CORES + lax.axis_index("c")


def _sc_scratch(n_win, width):
    return [
        pltpu.VMEM((n_win, SC_WIN), jnp.int32),
        pltpu.VMEM((n_win, SC_WIN), jnp.int32),
        pltpu.VMEM((2, SC_WIN, width), U32),
        pltpu.SemaphoreType.DMA((2,)),
        pltpu.SemaphoreType.DMA((2,)),
    ]


def _sc_dispatch(rows, idx1, idx2, n_rows):
    _, width = rows.shape
    _, n_win, _ = idx1.shape

    @functools.partial(
        pl.kernel, mesh=_sc_mesh(), out_type=jax.ShapeDtypeStruct((n_rows, width), rows.dtype),
        scratch_types=_sc_scratch(n_win, width), name="sc_dispatch")
    def run(rows_hbm, i1_hbm, i2_hbm, o_hbm, i1_v, i2_v, buf, rsem, wsem):
        wid = _sc_worker_id()
        base = wid * (n_win * SC_WIN)
        pltpu.sync_copy(i1_hbm.at[wid], i1_v)
        pltpu.sync_copy(i2_hbm.at[wid], i2_v)

        def read(j):
            return pltpu.async_copy(rows_hbm.at[pl.ds(base + j * SC_WIN, SC_WIN)], buf.at[j % 2], rsem.at[j % 2])

        reads = {0: read(0)}
        writes = {}
        for j in range(n_win):
            if j + 1 < n_win:
                for d in writes.pop(j - 1, ()):
                    d.wait()
                reads[j + 1] = read(j + 1)
            reads.pop(j).wait()
            writes[j] = (pltpu.async_copy(buf.at[j % 2], o_hbm.at[i1_v.at[j]], wsem.at[j % 2]),
                         pltpu.async_copy(buf.at[j % 2], o_hbm.at[i2_v.at[j]], wsem.at[j % 2]))
        for j in sorted(writes):
            for d in writes[j]:
                d.wait()

    return run(rows, idx1, idx2)


def _sc_gather_pair(table, idx1, idx2):
    _, width = table.shape
    _, n_win, _ = idx1.shape
    n_tok = SC_WORKERS * n_win * SC_WIN
    out_t = jax.ShapeDtypeStruct((n_tok, width), table.dtype)

    @functools.partial(
        pl.kernel, mesh=_sc_mesh(), out_type=(out_t, out_t),
        scratch_types=_sc_scratch(n_win, width), name="sc_combine_gather")
    def run(table_hbm, i1_hbm, i2_hbm, o1_hbm, o2_hbm, i1_v, i2_v, buf, gsem, wsem):
        wid = _sc_worker_id()
        base = wid * (n_win * SC_WIN)
        pltpu.sync_copy(i1_hbm.at[wid], i1_v)
        pltpu.sync_copy(i2_hbm.at[wid], i2_v)
        work = [(i1_v, o1_hbm, j) for j in range(n_win)] + [(i2_v, o2_hbm, j) for j in range(n_win)]

        def gather(t):
            iv, _, j = work[t]
            return pltpu.async_copy(table_hbm.at[iv.at[j]], buf.at[t % 2], gsem.at[t % 2])

        def put(t):
            _, oh, j = work[t]
            return pltpu.async_copy(buf.at[t % 2], oh.at[pl.ds(base + j * SC_WIN, SC_WIN)], wsem.at[t % 2])

        gathers = {0: gather(0)}
        puts = {}
        for t in range(len(work)):
            if t + 1 < len(work):
                if t - 1 in puts:
                    puts.pop(t - 1).wait()
                gathers[t + 1] = gather(t + 1)
            gathers.pop(t).wait()
            puts[t] = put(t)
        for t in sorted(puts):
            puts[t].wait()

    return run(table, idx1, idx2)


def _moe_kernel(layer, be_ref, slot_ref, nxt_ref, valid_ref, x_ref, w1_hbm, w3_hbm, w2_hbm, o_ref,
                w1_st, w3_st, w2_st, w1_sc, w3_sc, w2_sc, sems):
    i = pl.program_id(0)
    expert = be_ref[i]
    slot = slot_ref[i]
    new_expert = jnp.logical_or(i == 0, expert != be_ref[jnp.maximum(i - 1, 0)])

    def weight_copies(e, sl):
        return [pltpu.make_async_copy(hbm.at[layer, e], stage.at[sl], sems.at[k, sl])
                for k, (hbm, stage) in enumerate(((w1_hbm, w1_st), (w3_hbm, w3_st), (w2_hbm, w2_st)))]

    @pl.when(i == 0)
    def _():
        for cp in weight_copies(expert, slot):
            cp.start()

    @pl.when(new_expert)
    def _():
        for cp in weight_copies(expert, slot):
            cp.wait()
        w1_sc[...] = w1_st[slot].astype(BF16)
        w3_sc[...] = w3_st[slot].astype(BF16)
        w2_sc[...] = w2_st[slot].astype(BF16)
        nxt = nxt_ref[i]

        @pl.when(nxt >= 0)
        def _():
            for cp in weight_copies(nxt, 1 - slot):
                cp.start()

    def up(rows):
        xa, xb = _unpack_bf16_pair(x_ref[rows, :])
        xa = xa.astype(BF16)
        xb = xb.astype(BF16)
        h1 = (jnp.dot(xa, w1_sc[:D_HALF], preferred_element_type=F32)
              + jnp.dot(xb, w1_sc[D_HALF:], preferred_element_type=F32))
        h3 = (jnp.dot(xa, w3_sc[:D_HALF], preferred_element_type=F32)
              + jnp.dot(xb, w3_sc[D_HALF:], preferred_element_type=F32))
        return h1, h3

    def down(rows, h1, h3):
        hdn = (jax.nn.silu(h1) * h3).astype(BF16)
        y = jnp.dot(hdn, w2_sc[...], preferred_element_type=F32)
        o_ref[rows, :] = _pack_row_halves(y)

    assert MOE_BM == 2 * MOE_SUB
    first, second = slice(0, MOE_SUB), slice(MOE_SUB, MOE_BM)
    valid = valid_ref[i]

    @pl.when(valid > MOE_SUB)
    def _():
        ups = [up(first), up(second)]
        down(first, *ups[0])
        down(second, *ups[1])

    @pl.when(jnp.logical_and(valid > 0, valid <= MOE_SUB))
    def _():
        down(first, *up(first))
        o_ref[second, :] = jnp.zeros((MOE_SUB, D_HALF), o_ref.dtype)

    @pl.when(valid == 0)
    def _():
        o_ref[...] = jnp.zeros(o_ref.shape, o_ref.dtype)


def _moe_blocks(xbuf, block_e, valid, w1, w3, w2, layer):
    n_rows, _ = xbuf.shape
    D = D_MODEL
    n_blocks = n_rows // MOE_BM
    pos = jnp.arange(n_blocks, dtype=jnp.int32)
    is_new = jnp.concatenate([jnp.ones((1,), bool), block_e[1:] != block_e[:-1]])
    slot = ((jnp.cumsum(is_new.astype(jnp.int32)) - 1) % 2).astype(jnp.int32)
    change_pos = jnp.where(is_new, pos, n_blocks)
    next_change = jnp.concatenate([lax.cummin(change_pos, reverse=True)[1:], jnp.full((1,), n_blocks, jnp.int32)])
    nxt = jnp.where(next_change < n_blocks, block_e[jnp.minimum(next_change, n_blocks - 1)], -1).astype(jnp.int32)
    rows = lambda i, be, sl, nx, nu: (i, 0)
    grid_spec = pltpu.PrefetchScalarGridSpec(
        num_scalar_prefetch=4,
        grid=(n_blocks,),
        in_specs=[
            pl.BlockSpec((MOE_BM, D_HALF), rows),
            pl.BlockSpec(memory_space=pl.ANY),
            pl.BlockSpec(memory_space=pl.ANY),
            pl.BlockSpec(memory_space=pl.ANY),
        ],
        out_specs=pl.BlockSpec((MOE_BM, D_HALF), rows),
        scratch_shapes=[
            pltpu.VMEM((2, D, D_EXPERT), F32),
            pltpu.VMEM((2, D, D_EXPERT), F32),
            pltpu.VMEM((2, D_EXPERT, D), F32),
            pltpu.VMEM((D, D_EXPERT), BF16),
            pltpu.VMEM((D, D_EXPERT), BF16),
            pltpu.VMEM((D_EXPERT, D), BF16),
            pltpu.SemaphoreType.DMA((3, 2)),
        ],
    )
    return pl.pallas_call(
        functools.partial(_moe_kernel, layer),
        grid_spec=grid_spec,
        out_shape=jax.ShapeDtypeStruct((n_rows, D_HALF), U32),
        compiler_params=pltpu.CompilerParams(
            dimension_semantics=("arbitrary",), vmem_limit_bytes=VMEM_LIMIT),
        name="moe_experts",
    )(block_e, slot, nxt, valid, xbuf, w1, w3, w2)


def _moe_combine_norm(x, y1, y2, gate_rows, g, b):
    n = x.shape[0]
    pad = jnp.zeros((LANES - SUBLANES, n), F32)
    gates = jnp.concatenate([gate_rows, pad], axis=0).T
    g1 = gates[:, 0:1]
    g2 = gates[:, 1:2]
    a1, b1 = _unpack_bf16_pair(y1)
    a2, b2 = _unpack_bf16_pair(y2)
    f = jnp.concatenate([g1 * a1 + g2 * a2, g1 * b1 + g2 * b2], axis=1)
    return _layer_norm_rows(ALPHA * x + f, g, b)


def _combine_kernel(x_ref, y1_ref, y2_ref, gates_ref, g_ref, b_ref, o_ref):
    o_ref[...] = _moe_combine_norm(x_ref[...], y1_ref[...], y2_ref[...], gates_ref[...], g_ref[...], b_ref[...])


def _combine_ln(xf, y1, y2, gates, ln_g, ln_b):
    T, D = xf.shape
    const = lambda shape: pl.BlockSpec(shape, lambda i: (0,) * len(shape))
    rows = lambda w: pl.BlockSpec((COMB_T, w), lambda i: (i, 0))
    return pl.pallas_call(
        _combine_kernel,
        grid=(T // COMB_T,),
        in_specs=[rows(D), rows(D_HALF), rows(D_HALF), pl.BlockSpec((SUBLANES, COMB_T), lambda i: (0, i)),
                  const((1, D)), const((1, D))],
        out_specs=rows(D),
        out_shape=jax.ShapeDtypeStruct((T, D), F32),
        compiler_params=pltpu.CompilerParams(
            dimension_semantics=("arbitrary",), vmem_limit_bytes=VMEM_LIMIT),
        name="moe_combine_ln",
    )(xf, y1, y2, gates, ln_g.reshape(1, D), ln_b.reshape(1, D))


def _hier_moe(x, xpk, w_rg, b_rg, w_re, b_re, w1, w3, w2, layer):
    B, S, D = x.shape
    T = B * S
    xf = x.reshape(T, D)
    idx, gates, cnt = _router(xf, w_rg, b_rg, w_re, b_re)
    counts = cnt[:, 0]
    padded = ((counts + MOE_BM - 1) // MOE_BM) * MOE_BM
    pends = jnp.cumsum(padded)
    pstarts = pends - padded
    experts = jnp.arange(N_EXPERTS, dtype=jnp.int32)[:, None]

    def dest(e_row, rank_row):
        return jnp.sum(jnp.where(e_row[None, :] == experts, pstarts[:, None], 0), axis=0) + rank_row

    n_win = T // (SC_WORKERS * SC_WIN)
    dest1 = dest(idx[0], idx[2]).reshape(SC_WORKERS, n_win, SC_WIN)
    dest2 = dest(idx[1], idx[3]).reshape(SC_WORKERS, n_win, SC_WIN)
    n_blocks = -(-(T * TOP_K + N_EXPERTS * (MOE_BM - 1)) // MOE_BM)
    n_rows = n_blocks * MOE_BM
    block_start = jnp.arange(n_blocks, dtype=jnp.int32) * MOE_BM
    block_e = jnp.minimum(jnp.sum(block_start[:, None] >= pends[None, :], axis=1), N_EXPERTS - 1).astype(jnp.int32)
    of_block = block_e[:, None] == experts[:, 0][None, :]
    pick = lambda v: jnp.sum(jnp.where(of_block, v[None, :], 0), axis=1)
    valid = jnp.clip(pick(counts) - (block_start - pick(pstarts)), 0, MOE_BM).astype(jnp.int32)
    xbuf = _sc_dispatch(xpk.reshape(T, D_HALF), dest1, dest2, n_rows)
    ybuf = _moe_blocks(xbuf, block_e, valid, w1, w3, w2, layer)
    y1, y2 = _sc_gather_pair(ybuf, dest1, dest2)
    return y1, y2, gates


def _att_head_order():
    order = []
    for p in range(N_HEADS // 2):
        jj, m = divmod(p, 4)
        order += [8 * jj + m, 8 * jj + 4 + m]
    return order


ATT_HEAD_ORDER = _att_head_order()


def _attn_kernel(xprev_ref, y1_ref, y2_ref, gates_a_ref, gates_b_ref, g_prev_ref, b_prev_ref, wqkv_ref, bias_ref,
                 sink_ref, wo_ref, g_ref, b_ref, o_ref, opk_ref, kv_ext, o_sc, s_sc0, s_sc1, p_sc0, p_sc1):
    s = pl.program_id(1)
    tq = ATT_TQ
    s_bufs = (s_sc0, s_sc1)
    p_bufs = (p_sc0, p_sc1)
    gate_refs = (gates_a_ref, gates_b_ref)
    assert ATT_NSEQ == len(gate_refs)

    @pl.when(s == 0)
    def _():
        kv_ext[:, 0:WINDOW, :] = jnp.zeros((ATT_NSEQ, WINDOW, 2 * KV_DIM), BF16)

    def layer_input(sq):
        return _moe_combine_norm(xprev_ref[sq], y1_ref[sq], y2_ref[sq], gate_refs[sq][...],
                                 g_prev_ref[...], b_prev_ref[...])

    def project_qkv(sq, x):
        qkv = jnp.dot(x.astype(BF16), wqkv_ref[...], preferred_element_type=F32)
        kv_ext[sq, WINDOW:WINDOW + tq, :] = qkv[:, Q_DIM:].astype(BF16)
        return (qkv[:, :Q_DIM] * (HEAD_DIM ** -0.5 * LOG2E)).astype(BF16)

    lane = lax.broadcasted_iota(jnp.int32, (WINDOW, LANES), 1)
    low = lane < HEAD_DIM
    sub = lax.broadcasted_iota(jnp.int32, (LANES, WINDOW), 0)
    top = sub < HEAD_DIM
    first = jnp.where(s == 0, 1, 0)
    nt = (((1,), (1,)), ((), ()))
    zero = jnp.zeros((), BF16)

    tiles = [(sq, n, j) for n in range(ATT_NB) for j in range(2) for sq in range(ATT_NSEQ)]

    def scores(t, qs_all):
        sq, n, j = tiles[t]
        q = qs_all[sq]
        r0 = n * WINDOW
        k_tile = kv_ext[sq, r0:r0 + 2 * WINDOW, j * LANES:(j + 1) * LANES]
        parts = []
        for m in range(4):
            p = 4 * j + m
            qt = q[r0:r0 + WINDOW, p * LANES:(p + 1) * LANES]
            parts.append(jnp.where(low, qt, zero))
            parts.append(jnp.where(low, zero, qt))
        qs = jnp.concatenate(parts, axis=0)
        bias_sel = first if n == 0 else 0
        s_bufs[t % 2][...] = (lax.dot_general(k_tile, qs, nt, preferred_element_type=F32)
                              + bias_ref[bias_sel, j])

    def softmax_pv(t):
        sq, n, j = tiles[t]
        r0 = n * WINDOW
        s_sc = s_bufs[t % 2]
        p_sc = p_bufs[t % 2]
        inv_l = []
        for h in range(8):
            hc = slice(h * WINDOW, (h + 1) * WINDOW)
            sink = sink_ref[8 * j + h] * LOG2E
            mx = jnp.maximum(jnp.max(s_sc[:, hc], axis=0, keepdims=True), sink)
            pr = jnp.exp2(s_sc[:, hc] - mx)
            p_sc[:, hc] = pr.astype(BF16)
            inv_l.append(1.0 / (jnp.sum(pr, axis=0, keepdims=True) + jnp.exp2(sink - mx)))
        v_tile = kv_ext[sq, r0:r0 + 2 * WINDOW, KV_DIM + j * LANES:KV_DIM + (j + 1) * LANES]
        v_t = v_tile.astype(F32).T.astype(BF16)
        ov = jnp.dot(v_t, p_sc[...], preferred_element_type=F32)
        for m in range(4):
            p = 4 * j + m
            o_even = ov[:, (2 * m) * WINDOW:(2 * m + 1) * WINDOW] * inv_l[2 * m]
            o_odd = ov[:, (2 * m + 1) * WINDOW:(2 * m + 2) * WINDOW] * inv_l[2 * m + 1]
            o_sc[sq, p * LANES:(p + 1) * LANES, r0:r0 + WINDOW] = jnp.where(top, o_even, o_odd).astype(BF16)

    def project_out(sq):
        return jnp.dot(wo_ref[...], o_sc[sq], preferred_element_type=F32)

    def finish(sq, x, out_t):
        xn = _layer_norm_rows(ALPHA * x + out_t.T, g_ref[...], b_ref[...])
        o_ref[sq] = xn
        opk_ref[sq] = _pack_row_halves(xn)

    xs = [layer_input(0)]
    qs_all = [project_qkv(0, xs[0])]
    xs.append(layer_input(1))
    qs_all.append(project_qkv(1, xs[1]))
    scores(0, qs_all)
    for t in range(len(tiles)):
        if t + 1 < len(tiles):
            scores(t + 1, qs_all)
        softmax_pv(t)
    kv_ext[:, 0:WINDOW, :] = kv_ext[:, tq:tq + WINDOW, :]
    outs_t = [project_out(sq) for sq in range(ATT_NSEQ)]
    for sq in range(ATT_NSEQ):
        finish(sq, xs[sq], outs_t[sq])


def _attn_bias():
    qi = np.arange(WINDOW)[:, None]
    sj = np.arange(2 * WINDOW)[None, :]
    dist = qi - sj + WINDOW
    valid = (dist >= 0) & (dist < WINDOW)
    slopes = 2.0 ** (-8.0 * np.arange(1, N_HEADS + 1, dtype=np.float32) / N_HEADS)
    slopes = slopes.astype(np.float32)[ATT_HEAD_ORDER]
    sb = -(slopes[:, None, None] * dist.astype(np.float32)[None])
    later = np.where(valid[None], sb, -np.inf)
    first = np.where((valid & (sj >= WINDOW))[None], sb, -np.inf)
    bias = np.stack([later, first]).astype(np.float32) * np.float32(LOG2E)
    bias = bias.reshape(2, 2, 8, WINDOW, 2 * WINDOW).transpose(0, 1, 4, 2, 3).reshape(2, 2, 2 * WINDOW, 8 * WINDOW)
    return jnp.asarray(np.ascontiguousarray(bias))


def _attn_layer(x_prev, y1, y2, gates, g_prev, b_prev, w_qkv, sinks, w_o, ln_g, ln_b):
    B, S, D = x_prev.shape
    steps = S // ATT_TQ
    assert ATT_HEAD_ORDER == list(np.arange(N_HEADS).reshape(2, 2, 4).transpose(0, 2, 1).reshape(-1))
    wq = w_qkv[:, :Q_DIM].reshape(D, 2, 2, 4, HEAD_DIM).transpose(0, 1, 3, 2, 4).reshape(D, Q_DIM)
    wqkv = jnp.concatenate([wq, w_qkv[:, Q_DIM:]], axis=1).astype(BF16)
    wo_t = w_o.reshape(2, 2, 4, HEAD_DIM, D).transpose(0, 2, 1, 3, 4).reshape(Q_DIM, D).T.astype(BF16)
    sink = sinks.reshape(2, 2, 4).transpose(0, 2, 1).reshape(N_HEADS, 1, 1)
    bias = _attn_bias()
    const = lambda shape: pl.BlockSpec(shape, lambda b, s: (0,) * len(shape))
    tile = lambda w: pl.BlockSpec((ATT_NSEQ, ATT_TQ, w), lambda b, s: (b, s, 0))
    gate_rows = lambda sq: pl.BlockSpec((SUBLANES, ATT_TQ), lambda b, s: (0, (ATT_NSEQ * b + sq) * steps + s))
    return pl.pallas_call(
        _attn_kernel,
        grid=(B // ATT_NSEQ, steps),
        in_specs=[
            tile(D),
            tile(D_HALF),
            tile(D_HALF),
            gate_rows(0),
            gate_rows(1),
            const((1, D)),
            const((1, D)),
            const((D, Q_DIM + 2 * KV_DIM)),
            const((2, 2, 2 * WINDOW, 8 * WINDOW)),
            const((N_HEADS, 1, 1)),
            const((D, Q_DIM)),
            const((1, D)),
            const((1, D)),
        ],
        out_specs=[tile(D), tile(D_HALF)],
        out_shape=[jax.ShapeDtypeStruct((B, S, D), F32), jax.ShapeDtypeStruct((B, S, D_HALF), U32)],
        scratch_shapes=[
            pltpu.VMEM((ATT_NSEQ, ATT_TQ + WINDOW, 2 * KV_DIM), BF16),
            pltpu.VMEM((ATT_NSEQ, Q_DIM, ATT_TQ), BF16),
            pltpu.VMEM((2 * WINDOW, 8 * WINDOW), F32),
            pltpu.VMEM((2 * WINDOW, 8 * WINDOW), F32),
            pltpu.VMEM((2 * WINDOW, 8 * WINDOW), BF16),
            pltpu.VMEM((2 * WINDOW, 8 * WINDOW), BF16),
        ],
        compiler_params=pltpu.CompilerParams(
            dimension_semantics=("arbitrary", "arbitrary"), vmem_limit_bytes=VMEM_LIMIT),
        name="swa_attn_ln",
    )(x_prev, y1.reshape(B, S, D_HALF), y2.reshape(B, S, D_HALF), gates, gates, g_prev.reshape(1, D),
      b_prev.reshape(1, D),
      wqkv, bias, sink, wo_t, ln_g.reshape(1, D), ln_b.reshape(1, D))


def kernel(x, rec_w_in, rec_conv_w, rec_conv_b, rec_w_r, rec_b_r, rec_w_i, rec_b_i, rec_lambda, rec_w_out,
           att_w_qkv, att_sinks, att_w_o, moe_w_group, moe_b_group, moe_w_expert, moe_b_expert,
           moe_w1, moe_w3, moe_w2, ln_g, ln_b):
    assert DEPTH == 2
    B, S, D = x.shape

    def moe(layer, xin, xin_pk):
        return _hier_moe(xin, xin_pk, moe_w_group[layer], moe_b_group[layer], moe_w_expert[layer],
                         moe_b_expert[layer], moe_w1, moe_w3, moe_w2, layer)

    x1, x1_pk = _rglru_layer(x, rec_w_in[0], rec_conv_w[0], rec_conv_b[0], rec_w_r[0], rec_b_r[0], rec_w_i[0],
                             rec_b_i[0], rec_lambda[0], rec_w_out[0], ln_g[0, 0], ln_b[0, 0])
    y1, y2, gates = moe(0, x1, x1_pk)
    x3, x3_pk = _attn_layer(x1, y1, y2, gates, ln_g[0, 1], ln_b[0, 1], att_w_qkv[0], att_sinks[0], att_w_o[0],
                            ln_g[1, 0], ln_b[1, 0])
    y1, y2, gates = moe(1, x3, x3_pk)
    out = _combine_ln(x3.reshape(B * S, D), y1, y2, gates, ln_g[1, 1], ln_b[1, 1])
    return out.reshape(B, S, D)
```

```python
import functools

import jax
import jax.numpy as jnp
import numpy as np
from jax import lax
from jax.experimental import pallas as pl
from jax.experimental.pallas import tpu as pltpu
from jax.experimental.pallas import tpu_sc as plsc

F32 = jnp.float32
BF16 = jnp.bfloat16
U32 = jnp.uint32

D_MODEL = 1024
DEPTH = 2
D_RNN = 1280
LRU_BLOCKS = 16
LRU_BLOCK_W = D_RNN // LRU_BLOCKS
CONV_W = 4
LRU_C = 8.0
N_HEADS = 16
N_KV_HEADS = 4
HEAD_DIM = 64
WINDOW = 128
Q_DIM = N_HEADS * HEAD_DIM
KV_DIM = N_KV_HEADS * HEAD_DIM
N_GROUPS = 4
EXPERTS_PER_GROUP = 8
N_EXPERTS = N_GROUPS * EXPERTS_PER_GROUP
TOP_K = 2
D_EXPERT = 512
ALPHA = (2 * DEPTH) ** 0.25
LN_EPS = 1e-5
LOG2E = 1.4426950408889634

LANES = 128
SUBLANES = 8
VMEM_LIMIT = 56 * 1024 * 1024

REC_TS = 256
REC_GROUPS = REC_TS // SUBLANES
REC_NSEQ = 2
GATE_TILE = 256
GATE_WIN = 512
GATE_WIN_STARTS = (0, 128, 384, 640, 768)
N_GATE_TILES = D_RNN // GATE_TILE

ROUTE_T = 512
ROUTE_ROWS = 40

MOE_BM = 1024
MOE_SUB = 256

ATT_TQ = 256
ATT_NB = ATT_TQ // WINDOW
ATT_NSEQ = 2

COMB_T = 1024

D_HALF = D_MODEL // 2

SC_CORES = 2
SC_SUBCORES = 16
SC_WORKERS = SC_CORES * SC_SUBCORES
SC_WIN = 64


def _layer_norm_rows(z, g, b):
    mu = jnp.mean(z, axis=-1, keepdims=True)
    zc = z - mu
    var = jnp.mean(zc * zc, axis=-1, keepdims=True)
    return zc * lax.rsqrt(var + LN_EPS) * g + b


def _pack_bf16_pair(a, b):
    ua = lax.bitcast_convert_type(a.astype(BF16).astype(F32), U32)
    ub = lax.bitcast_convert_type(b.astype(BF16).astype(F32), U32)
    return (ua >> 16) | (ub & jnp.uint32(0xFFFF0000))


def _unpack_bf16_pair(w):
    a = lax.bitcast_convert_type(w << 16, F32)
    b = lax.bitcast_convert_type(w & jnp.uint32(0xFFFF0000), F32)
    return a, b


def _pack_row_halves(x):
    return _pack_bf16_pair(x[:, :D_HALF], x[:, D_HALF:])


def _rglru_kernel(x_ref, perm_ref, perm_t_ref, w_in_ref, convw_ref, convb_ref, wg_ref, br_ref, bi_ref, lam_ref,
                  w_out_ref, g_ref, b_ref, o_ref, opk_ref, xr_ext, tail_sc, a_sc, u_sc, h_carry, gate_sc):
    s = pl.program_id(1)
    ts = REC_TS
    halo = (CONV_W - 1) * SUBLANES
    seqs = range(REC_NSEQ)

    @pl.when(s == 0)
    def _():
        tail_sc[...] = jnp.zeros((REC_NSEQ, halo, D_RNN), F32)
        h_carry[...] = jnp.zeros((REC_NSEQ, 1, D_RNN), F32)

    row = lax.broadcasted_iota(jnp.int32, (SUBLANES, D_RNN), 0)
    nlam = -lam_ref[...]
    sp = jnp.maximum(nlam, 0.0) + jnp.log1p(jnp.exp(-jnp.abs(nlam)))
    log2a_scale = (-LRU_C * LOG2E) * sp

    def project(q):
        xp = jnp.dot(perm_ref[...], x_ref[q].astype(BF16), preferred_element_type=F32).astype(BF16)
        proj = jnp.dot(xp, w_in_ref[...], preferred_element_type=F32)
        gate_sc[q] = proj[:, :D_RNN]
        return proj[:, D_RNN:]

    def conv_gates(q, xr):
        for k in range(CONV_W - 1):
            r0 = ts - halo + k * SUBLANES
            cur = xr[r0:r0 + SUBLANES, :]
            prev = tail_sc[q, k * SUBLANES:(k + 1) * SUBLANES, :]
            xr_ext[q, k * SUBLANES:(k + 1) * SUBLANES, :] = jnp.where(
                row == 0, pltpu.roll(prev, 1, axis=0), pltpu.roll(cur, 1, axis=0))
        tail_sc[q] = xr[ts - halo:, :]
        xr_ext[q, halo:halo + ts, :] = xr
        xc = convb_ref[...] + convw_ref[CONV_W - 1:CONV_W, :] * xr
        for k in range(CONV_W - 1):
            xc = xc + convw_ref[k:k + 1, :] * xr_ext[q, k * SUBLANES:k * SUBLANES + ts, :]
        xcb = xc.astype(BF16)
        pres = [jnp.dot(xcb[:, GATE_WIN_STARTS[j]:GATE_WIN_STARTS[j] + GATE_WIN], wg_ref[j],
                        preferred_element_type=F32) for j in range(N_GATE_TILES)]
        for j, pre in enumerate(pres):
            cs = j * GATE_TILE
            r = jax.nn.sigmoid(pre[:, :GATE_TILE] + br_ref[:, cs:cs + GATE_TILE])
            i = jax.nn.sigmoid(pre[:, GATE_TILE:] + bi_ref[:, cs:cs + GATE_TILE])
            a = jnp.exp2(r * log2a_scale[:, cs:cs + GATE_TILE])
            s1 = 1.0 - a * a
            mult = jnp.where(s1 > 0.0, s1 * lax.rsqrt(s1), 0.0)
            u = mult * (i * xc[:, cs:cs + GATE_TILE])
            a_sc[q, :, cs:cs + GATE_TILE] = a
            u_sc[q, :, cs:cs + GATE_TILE] = u

    def segment_scan(q):
        h = jnp.zeros((SUBLANES, D_RNN), F32)
        prod = jnp.ones((SUBLANES, D_RNN), F32)
        for gidx in range(REC_GROUPS):
            rows = slice(gidx * SUBLANES, (gidx + 1) * SUBLANES)
            a8 = a_sc[q, rows, :]
            h = a8 * h + u_sc[q, rows, :]
            prod = a8 * prod
            u_sc[q, rows, :] = h
            a_sc[q, rows, :] = prod
        return h, prod

    def recur_out(q, seg):
        seg_h, seg_a = seg
        for d in (1, 2, 4):
            keep = row >= d
            a_sh = jnp.where(keep, pltpu.roll(seg_a, d, axis=0), 1.0)
            h_sh = jnp.where(keep, pltpu.roll(seg_h, d, axis=0), 0.0)
            seg_h = seg_a * h_sh + seg_h
            seg_a = seg_a * a_sh
        h_in = h_carry[q]
        after = seg_a * h_in + seg_h
        enter = jnp.where(row == 0, h_in, pltpu.roll(after, 1, axis=0))
        h_carry[q] = after[SUBLANES - 1:SUBLANES, :]
        hs = (u_sc[q].reshape(REC_GROUPS, SUBLANES, D_RNN)
              + a_sc[q].reshape(REC_GROUPS, SUBLANES, D_RNN) * enter[None]).reshape(ts, D_RNN)
        y = hs * jax.nn.gelu(gate_sc[q])
        y_t = jnp.dot(perm_t_ref[...], y.astype(BF16), preferred_element_type=F32).astype(BF16)
        return jnp.dot(y_t, w_out_ref[...], preferred_element_type=F32)

    def finish(q, out):
        z = ALPHA * x_ref[q] + out
        xn = _layer_norm_rows(z, g_ref[...], b_ref[...])
        o_ref[q] = xn
        opk_ref[q] = _pack_row_halves(xn)

    assert REC_NSEQ == 2
    xr_a = project(0)
    conv_gates(0, xr_a)
    xr_b = project(1)
    out_a = recur_out(0, segment_scan(0))
    conv_gates(1, xr_b)
    finish(0, out_a)
    out_b = recur_out(1, segment_scan(1))
    finish(1, out_b)


def _band_gate_weights(w_r, w_i):
    spread = jnp.asarray(np.tile(np.eye(LRU_BLOCK_W, dtype=np.float32), (1, LRU_BLOCKS)), BF16)
    blk = np.arange(D_RNN) // LRU_BLOCK_W
    on_diag = jnp.asarray(blk[:, None] == blk[None, :])

    def dense(w):
        rows = w.reshape(D_RNN, LRU_BLOCK_W).astype(BF16)
        return jnp.where(on_diag, jnp.dot(rows, spread, preferred_element_type=F32), 0.0)

    wr, wi = dense(w_r), dense(w_i)
    tiles = []
    for j in range(N_GATE_TILES):
        ws = GATE_WIN_STARTS[j]
        cs = j * GATE_TILE
        lo_blk = cs // LRU_BLOCK_W
        hi_blk = (cs + GATE_TILE - 1) // LRU_BLOCK_W
        assert ws <= lo_blk * LRU_BLOCK_W and (hi_blk + 1) * LRU_BLOCK_W <= ws + GATE_WIN
        tiles.append(jnp.concatenate([wr[ws:ws + GATE_WIN, cs:cs + GATE_TILE],
                                      wi[ws:ws + GATE_WIN, cs:cs + GATE_TILE]], axis=1))
    return jnp.stack(tiles).astype(BF16)


def _rglru_layer(x, w_in, conv_w, conv_b, w_r, b_r, w_i, b_i, lam, w_out, ln_g, ln_b):
    B, S, D = x.shape
    wg = _band_gate_weights(w_r, w_i)
    rho = np.arange(REC_TS)
    perm_np = np.zeros((REC_TS, REC_TS), np.float32)
    perm_np[rho, (rho % SUBLANES) * REC_GROUPS + rho // SUBLANES] = 1.0
    perm = jnp.asarray(perm_np, BF16)
    perm_t = jnp.asarray(perm_np.T, BF16)
    row = lambda v: v.reshape(1, -1)
    const = lambda shape: pl.BlockSpec(shape, lambda b, s: (0,) * len(shape))
    tile = lambda w: pl.BlockSpec((REC_NSEQ, REC_TS, w), lambda b, s: (b, s, 0))
    halo = (CONV_W - 1) * SUBLANES
    return pl.pallas_call(
        _rglru_kernel,
        grid=(B // REC_NSEQ, S // REC_TS),
        in_specs=[
            tile(D),
            const((REC_TS, REC_TS)),
            const((REC_TS, REC_TS)),
            const((D, 2 * D_RNN)),
            const((CONV_W, D_RNN)),
            const((1, D_RNN)),
            const((N_GATE_TILES, GATE_WIN, 2 * GATE_TILE)),
            const((1, D_RNN)),
            const((1, D_RNN)),
            const((1, D_RNN)),
            const((D_RNN, D)),
            const((1, D)),
            const((1, D)),
        ],
        out_specs=[tile(D), tile(D_HALF)],
        out_shape=[jax.ShapeDtypeStruct((B, S, D), F32), jax.ShapeDtypeStruct((B, S, D_HALF), U32)],
        scratch_shapes=[
            pltpu.VMEM((REC_NSEQ, halo + REC_TS, D_RNN), F32),
            pltpu.VMEM((REC_NSEQ, halo, D_RNN), F32),
            pltpu.VMEM((REC_NSEQ, REC_TS, D_RNN), F32),
            pltpu.VMEM((REC_NSEQ, REC_TS, D_RNN), F32),
            pltpu.VMEM((REC_NSEQ, 1, D_RNN), F32),
            pltpu.VMEM((REC_NSEQ, REC_TS, D_RNN), F32),
        ],
        compiler_params=pltpu.CompilerParams(
            dimension_semantics=("arbitrary", "arbitrary"), vmem_limit_bytes=VMEM_LIMIT),
        name="rglru_ln",
    )(x, perm, perm_t, w_in.astype(BF16), conv_w, row(conv_b), wg, row(b_r), row(b_i), row(lam), w_out.astype(BF16),
      row(ln_g), row(ln_b))


def _router_kernel(x_ref, whi_ref, bias_ref, tri_ref, idx_ref, gate_ref, cnt_ref, base_sc):
    step = pl.program_id(0)
    tr = ROUTE_T

    @pl.when(step == 0)
    def _():
        base_sc[...] = jnp.zeros((N_EXPERTS, 1), F32)

    x = x_ref[...]
    xhi = x.astype(BF16)
    xlo = (x - xhi.astype(F32)).astype(BF16)
    nt = (((1,), (1,)), ((), ()))
    both = lax.dot_general(whi_ref[...], xhi, nt, preferred_element_type=F32)
    logits = (both[:ROUTE_ROWS] + both[ROUTE_ROWS:]
              + lax.dot_general(whi_ref[:ROUTE_ROWS], xlo, nt, preferred_element_type=F32))
    logits = logits + bias_ref[...]

    row8 = lax.broadcasted_iota(jnp.int32, (SUBLANES, tr), 0).astype(F32)
    neg_inf = -jnp.inf
    g = jnp.where(row8 < N_GROUPS, logits[N_EXPERTS:N_EXPERTS + SUBLANES, :], neg_inf)
    gmax = jnp.max(g, axis=0, keepdims=True)
    gidx = jnp.min(jnp.where(g == gmax, row8, SUBLANES), axis=0, keepdims=True)
    g_gate = 1.0 / jnp.sum(jnp.exp(g - gmax), axis=0, keepdims=True)

    esel = logits[0:EXPERTS_PER_GROUP, :]
    for grp in range(1, N_GROUPS):
        esel = jnp.where(gidx == grp, logits[grp * EXPERTS_PER_GROUP:(grp + 1) * EXPERTS_PER_GROUP, :], esel)
    v1 = jnp.max(esel, axis=0, keepdims=True)
    i1 = jnp.min(jnp.where(esel == v1, row8, SUBLANES), axis=0, keepdims=True)
    esel2 = jnp.where(row8 == i1, neg_inf, esel)
    v2 = jnp.max(esel2, axis=0, keepdims=True)
    i2 = jnp.min(jnp.where(esel2 == v2, row8, SUBLANES), axis=0, keepdims=True)
    e21 = jnp.exp(v2 - v1)
    inv = 1.0 / (1.0 + e21)
    gate1 = inv * g_gate
    gate2 = e21 * inv * g_gate
    e1 = gidx * EXPERTS_PER_GROUP + i1
    e2 = gidx * EXPERTS_PER_GROUP + i2

    rowe = lax.broadcasted_iota(jnp.int32, (N_EXPERTS, tr), 0).astype(F32)
    hit1 = rowe == e1
    hit2 = rowe == e2
    member = jnp.where(hit1, 1.0, jnp.where(hit2, 1.0, 0.0))
    before = jnp.dot(member.astype(BF16), tri_ref[...], preferred_element_type=F32) + base_sc[...]
    rank1 = jnp.sum(jnp.where(hit1, before, 0.0), axis=0, keepdims=True)
    rank2 = jnp.sum(jnp.where(hit2, before, 0.0), axis=0, keepdims=True)
    base_sc[...] = base_sc[...] + jnp.sum(member, axis=1, keepdims=True)

    zi = jnp.zeros((1, tr), jnp.int32)
    idx_ref[...] = jnp.concatenate(
        [e1.astype(jnp.int32), e2.astype(jnp.int32), rank1.astype(jnp.int32), rank2.astype(jnp.int32),
         zi, zi, zi, zi], axis=0)
    zf = jnp.zeros((1, tr), F32)
    gate_ref[...] = jnp.concatenate([gate1, gate2, zf, zf, zf, zf, zf, zf], axis=0)
    cnt_ref[...] = jnp.broadcast_to(base_sc[...], (N_EXPERTS, LANES)).astype(jnp.int32)


def _router(xf, w_rg, b_rg, w_re, b_re):
    T, D = xf.shape
    pad_rows = ROUTE_ROWS - N_EXPERTS - N_GROUPS
    w = jnp.concatenate([w_re.T, w_rg.T, jnp.zeros((pad_rows, D), F32)], axis=0)
    whi = w.astype(BF16)
    wlo = (w - whi.astype(F32)).astype(BF16)
    w_split = jnp.concatenate([whi, wlo], axis=0)
    bias = jnp.concatenate([b_re, b_rg, jnp.zeros((pad_rows,), F32)]).reshape(ROUTE_ROWS, 1)
    tri = jnp.asarray(np.triu(np.ones((ROUTE_T, ROUTE_T), np.float32), 1), BF16)
    const = lambda shape: pl.BlockSpec(shape, lambda i: (0,) * len(shape))
    return pl.pallas_call(
        _router_kernel,
        grid=(T // ROUTE_T,),
        in_specs=[
            pl.BlockSpec((ROUTE_T, D), lambda i: (i, 0)),
            const((2 * ROUTE_ROWS, D)),
            const((ROUTE_ROWS, 1)),
            const((ROUTE_T, ROUTE_T)),
        ],
        out_specs=[
            pl.BlockSpec((SUBLANES, ROUTE_T), lambda i: (0, i)),
            pl.BlockSpec((SUBLANES, ROUTE_T), lambda i: (0, i)),
            const((N_EXPERTS, LANES)),
        ],
        out_shape=[
            jax.ShapeDtypeStruct((SUBLANES, T), jnp.int32),
            jax.ShapeDtypeStruct((SUBLANES, T), F32),
            jax.ShapeDtypeStruct((N_EXPERTS, LANES), jnp.int32),
        ],
        scratch_shapes=[pltpu.VMEM((N_EXPERTS, 1), F32)],
        compiler_params=pltpu.CompilerParams(
            dimension_semantics=("arbitrary",), vmem_limit_bytes=VMEM_LIMIT),
        name="router",
    )(xf, w_split, bias, tri)


def _sc_mesh():
    return plsc.VectorSubcoreMesh(core_axis_name="c", subcore_axis_name="s",
                                  num_cores=SC_CORES, num_subcores=SC_SUBCORES)


def _sc_worker_id():
    return lax.axis_index("s") * SC_CORES + lax.axis_index("c")


def _sc_scratch(n_win, width):
    return [
        pltpu.VMEM((n_win, SC_WIN), jnp.int32),
        pltpu.VMEM((n_win, SC_WIN), jnp.int32),
        pltpu.VMEM((2, SC_WIN, width), U32),
        pltpu.SemaphoreType.DMA((2,)),
        pltpu.SemaphoreType.DMA((2,)),
    ]


def _sc_dispatch(rows, idx1, idx2, n_rows):
    _, width = rows.shape
    _, n_win, _ = idx1.shape

    @functools.partial(
        pl.kernel, mesh=_sc_mesh(), out_type=jax.ShapeDtypeStruct((n_rows, width), rows.dtype),
        scratch_types=_sc_scratch(n_win, width), name="sc_dispatch")
    def run(rows_hbm, i1_hbm, i2_hbm, o_hbm, i1_v, i2_v, buf, rsem, wsem):
        wid = _sc_worker_id()
        base = wid * (n_win * SC_WIN)
        pltpu.sync_copy(i1_hbm.at[wid], i1_v)
        pltpu.sync_copy(i2_hbm.at[wid], i2_v)

        def read(j):
            return pltpu.async_copy(rows_hbm.at[pl.ds(base + j * SC_WIN, SC_WIN)], buf.at[j % 2], rsem.at[j % 2])

        reads = {0: read(0)}
        writes = {}
        for j in range(n_win):
            if j + 1 < n_win:
                for d in writes.pop(j - 1, ()):
                    d.wait()
                reads[j + 1] = read(j + 1)
            reads.pop(j).wait()
            writes[j] = (pltpu.async_copy(buf.at[j % 2], o_hbm.at[i1_v.at[j]], wsem.at[j % 2]),
                         pltpu.async_copy(buf.at[j % 2], o_hbm.at[i2_v.at[j]], wsem.at[j % 2]))
        for j in sorted(writes):
            for d in writes[j]:
                d.wait()

    return run(rows, idx1, idx2)


def _sc_gather_pair(table, idx1, idx2):
    _, width = table.shape
    _, n_win, _ = idx1.shape
    n_tok = SC_WORKERS * n_win * SC_WIN
    out_t = jax.ShapeDtypeStruct((n_tok, width), table.dtype)

    @functools.partial(
        pl.kernel, mesh=_sc_mesh(), out_type=(out_t, out_t),
        scratch_types=_sc_scratch(n_win, width), name="sc_combine_gather")
    def run(table_hbm, i1_hbm, i2_hbm, o1_hbm, o2_hbm, i1_v, i2_v, buf, gsem, wsem):
        wid = _sc_worker_id()
        base = wid * (n_win * SC_WIN)
        pltpu.sync_copy(i1_hbm.at[wid], i1_v)
        pltpu.sync_copy(i2_hbm.at[wid], i2_v)
        work = [(i1_v, o1_hbm, j) for j in range(n_win)] + [(i2_v, o2_hbm, j) for j in range(n_win)]

        def gather(t):
            iv, _, j = work[t]
            return pltpu.async_copy(table_hbm.at[iv.at[j]], buf.at[t % 2], gsem.at[t % 2])

        def put(t):
            _, oh, j = work[t]
            return pltpu.async_copy(buf.at[t % 2], oh.at[pl.ds(base + j * SC_WIN, SC_WIN)], wsem.at[t % 2])

        gathers = {0: gather(0)}
        puts = {}
        for t in range(len(work)):
            if t + 1 < len(work):
                if t - 1 in puts:
                    puts.pop(t - 1).wait()
                gathers[t + 1] = gather(t + 1)
            gathers.pop(t).wait()
            puts[t] = put(t)
        for t in sorted(puts):
            puts[t].wait()

    return run(table, idx1, idx2)


def _moe_kernel(layer, be_ref, slot_ref, nxt_ref, valid_ref, x_ref, w1_hbm, w3_hbm, w2_hbm, o_ref,
                w1_st, w3_st, w2_st, w1_sc, w3_sc, w2_sc, sems):
    i = pl.program_id(0)
    expert = be_ref[i]
    slot = slot_ref[i]
    new_expert = jnp.logical_or(i == 0, expert != be_ref[jnp.maximum(i - 1, 0)])

    def weight_copies(e, sl):
        return [pltpu.make_async_copy(hbm.at[layer, e], stage.at[sl], sems.at[k, sl])
                for k, (hbm, stage) in enumerate(((w1_hbm, w1_st), (w3_hbm, w3_st), (w2_hbm, w2_st)))]

    @pl.when(i == 0)
    def _():
        for cp in weight_copies(expert, slot):
            cp.start()

    @pl.when(new_expert)
    def _():
        for cp in weight_copies(expert, slot):
            cp.wait()
        w1_sc[...] = w1_st[slot].astype(BF16)
        w3_sc[...] = w3_st[slot].astype(BF16)
        w2_sc[...] = w2_st[slot].astype(BF16)
        nxt = nxt_ref[i]

        @pl.when(nxt >= 0)
        def _():
            for cp in weight_copies(nxt, 1 - slot):
                cp.start()

    def up(rows):
        xa, xb = _unpack_bf16_pair(x_ref[rows, :])
        xa = xa.astype(BF16)
        xb = xb.astype(BF16)
        h1 = (jnp.dot(xa, w1_sc[:D_HALF], preferred_element_type=F32)
              + jnp.dot(xb, w1_sc[D_HALF:], preferred_element_type=F32))
        h3 = (jnp.dot(xa, w3_sc[:D_HALF], preferred_element_type=F32)
              + jnp.dot(xb, w3_sc[D_HALF:], preferred_element_type=F32))
        return h1, h3

    def down(rows, h1, h3):
        hdn = (jax.nn.silu(h1) * h3).astype(BF16)
        y = jnp.dot(hdn, w2_sc[...], preferred_element_type=F32)
        o_ref[rows, :] = _pack_row_halves(y)

    n_sub = MOE_BM // MOE_SUB
    subs = [slice(k * MOE_SUB, (k + 1) * MOE_SUB) for k in range(n_sub)]
    valid = valid_ref[i]
    chains = (valid + (MOE_SUB - 1)) // MOE_SUB

    for live in range(n_sub + 1):
        @pl.when(chains == live)
        def _(live=live):
            ups = {}
            if live:
                ups[0] = up(subs[0])
            for k in range(live):
                if k + 1 < live:
                    ups[k + 1] = up(subs[k + 1])
                down(subs[k], *ups.pop(k))
            if live < n_sub:
                o_ref[live * MOE_SUB:, :] = jnp.zeros((MOE_BM - live * MOE_SUB, D_HALF), o_ref.dtype)


def _moe_blocks(xbuf, block_e, valid, w1, w3, w2, layer):
    n_rows, _ = xbuf.shape
    D = D_MODEL
    n_blocks = n_rows // MOE_BM
    pos = jnp.arange(n_blocks, dtype=jnp.int32)
    is_new = jnp.concatenate([jnp.ones((1,), bool), block_e[1:] != block_e[:-1]])
    slot = ((jnp.cumsum(is_new.astype(jnp.int32)) - 1) % 2).astype(jnp.int32)
    change_pos = jnp.where(is_new, pos, n_blocks)
    next_change = jnp.concatenate([lax.cummin(change_pos, reverse=True)[1:], jnp.full((1,), n_blocks, jnp.int32)])
    nxt = jnp.where(next_change < n_blocks, block_e[jnp.minimum(next_change, n_blocks - 1)], -1).astype(jnp.int32)
    rows = lambda i, be, sl, nx, nu: (i, 0)
    grid_spec = pltpu.PrefetchScalarGridSpec(
        num_scalar_prefetch=4,
        grid=(n_blocks,),
        in_specs=[
            pl.BlockSpec((MOE_BM, D_HALF), rows),
            pl.BlockSpec(memory_space=pl.ANY),
            pl.BlockSpec(memory_space=pl.ANY),
            pl.BlockSpec(memory_space=pl.ANY),
        ],
        out_specs=pl.BlockSpec((MOE_BM, D_HALF), rows),
        scratch_shapes=[
            pltpu.VMEM((2, D, D_EXPERT), F32),
            pltpu.VMEM((2, D, D_EXPERT), F32),
            pltpu.VMEM((2, D_EXPERT, D), F32),
            pltpu.VMEM((D, D_EXPERT), BF16),
            pltpu.VMEM((D, D_EXPERT), BF16),
            pltpu.VMEM((D_EXPERT, D), BF16),
            pltpu.SemaphoreType.DMA((3, 2)),
        ],
    )
    return pl.pallas_call(
        functools.partial(_moe_kernel, layer),
        grid_spec=grid_spec,
        out_shape=jax.ShapeDtypeStruct((n_rows, D_HALF), U32),
        compiler_params=pltpu.CompilerParams(
            dimension_semantics=("arbitrary",), vmem_limit_bytes=VMEM_LIMIT),
        name="moe_experts",
    )(block_e, slot, nxt, valid, xbuf, w1, w3, w2)


def _moe_combine_norm(x, y1, y2, gate_rows, g, b):
    n = x.shape[0]
    pad = jnp.zeros((LANES - SUBLANES, n), F32)
    gates = jnp.concatenate([gate_rows, pad], axis=0).T
    g1 = gates[:, 0:1]
    g2 = gates[:, 1:2]
    a1, b1 = _unpack_bf16_pair(y1)
    a2, b2 = _unpack_bf16_pair(y2)
    f = jnp.concatenate([g1 * a1 + g2 * a2, g1 * b1 + g2 * b2], axis=1)
    return _layer_norm_rows(ALPHA * x + f, g, b)


def _combine_kernel(x_ref, y1_ref, y2_ref, gates_ref, g_ref, b_ref, o_ref):
    o_ref[...] = _moe_combine_norm(x_ref[...], y1_ref[...], y2_ref[...], gates_ref[...], g_ref[...], b_ref[...])


def _combine_ln(xf, y1, y2, gates, ln_g, ln_b):
    T, D = xf.shape
    const = lambda shape: pl.BlockSpec(shape, lambda i: (0,) * len(shape))
    rows = lambda w: pl.BlockSpec((COMB_T, w), lambda i: (i, 0))
    return pl.pallas_call(
        _combine_kernel,
        grid=(T // COMB_T,),
        in_specs=[rows(D), rows(D_HALF), rows(D_HALF), pl.BlockSpec((SUBLANES, COMB_T), lambda i: (0, i)),
                  const((1, D)), const((1, D))],
        out_specs=rows(D),
        out_shape=jax.ShapeDtypeStruct((T, D), F32),
        compiler_params=pltpu.CompilerParams(
            dimension_semantics=("arbitrary",), vmem_limit_bytes=VMEM_LIMIT),
        name="moe_combine_ln",
    )(xf, y1, y2, gates, ln_g.reshape(1, D), ln_b.reshape(1, D))


def _hier_moe(x, xpk, w_rg, b_rg, w_re, b_re, w1, w3, w2, layer):
    B, S, D = x.shape
    T = B * S
    xf = x.reshape(T, D)
    idx, gates, cnt = _router(xf, w_rg, b_rg, w_re, b_re)
    counts = cnt[:, 0]
    padded = ((counts + MOE_BM - 1) // MOE_BM) * MOE_BM
    pends = jnp.cumsum(padded)
    pstarts = pends - padded
    experts = jnp.arange(N_EXPERTS, dtype=jnp.int32)[:, None]

    def dest(e_row, rank_row):
        return jnp.sum(jnp.where(e_row[None, :] == experts, pstarts[:, None], 0), axis=0) + rank_row

    n_win = T // (SC_WORKERS * SC_WIN)
    dest1 = dest(idx[0], idx[2]).reshape(SC_WORKERS, n_win, SC_WIN)
    dest2 = dest(idx[1], idx[3]).reshape(SC_WORKERS, n_win, SC_WIN)
    n_blocks = -(-(T * TOP_K + N_EXPERTS * (MOE_BM - 1)) // MOE_BM)
    n_rows = n_blocks * MOE_BM
    block_start = jnp.arange(n_blocks, dtype=jnp.int32) * MOE_BM
    block_e = jnp.minimum(jnp.sum(block_start[:, None] >= pends[None, :], axis=1), N_EXPERTS - 1).astype(jnp.int32)
    of_block = block_e[:, None] == experts[:, 0][None, :]
    pick = lambda v: jnp.sum(jnp.where(of_block, v[None, :], 0), axis=1)
    valid = jnp.clip(pick(counts) - (block_start - pick(pstarts)), 0, MOE_BM).astype(jnp.int32)
    xbuf = _sc_dispatch(xpk.reshape(T, D_HALF), dest1, dest2, n_rows)
    ybuf = _moe_blocks(xbuf, block_e, valid, w1, w3, w2, layer)
    y1, y2 = _sc_gather_pair(ybuf, dest1, dest2)
    return y1, y2, gates


def _att_head_order():
    order = []
    for p in range(N_HEADS // 2):
        jj, m = divmod(p, 4)
        order += [8 * jj + m, 8 * jj + 4 + m]
    return order


ATT_HEAD_ORDER = _att_head_order()


def _attn_kernel(xprev_ref, y1_ref, y2_ref, gates_a_ref, gates_b_ref, g_prev_ref, b_prev_ref, wqkv_ref, bias_ref,
                 sink_ref, wo_ref, g_ref, b_ref, o_ref, opk_ref, kv_ext, o_sc, s_sc0, s_sc1, p_sc0, p_sc1):
    s = pl.program_id(1)
    tq = ATT_TQ
    s_bufs = (s_sc0, s_sc1)
    p_bufs = (p_sc0, p_sc1)
    gate_refs = (gates_a_ref, gates_b_ref)
    assert ATT_NSEQ == len(gate_refs)

    @pl.when(s == 0)
    def _():
        kv_ext[:, 0:WINDOW, :] = jnp.zeros((ATT_NSEQ, WINDOW, 2 * KV_DIM), BF16)

    def layer_input(sq):
        return _moe_combine_norm(xprev_ref[sq], y1_ref[sq], y2_ref[sq], gate_refs[sq][...],
                                 g_prev_ref[...], b_prev_ref[...])

    def project_qkv(sq, x):
        qkv = jnp.dot(x.astype(BF16), wqkv_ref[...], preferred_element_type=F32)
        kv_ext[sq, WINDOW:WINDOW + tq, :] = qkv[:, Q_DIM:].astype(BF16)
        return (qkv[:, :Q_DIM] * (HEAD_DIM ** -0.5 * LOG2E)).astype(BF16)

    lane = lax.broadcasted_iota(jnp.int32, (WINDOW, LANES), 1)
    low = lane < HEAD_DIM
    sub = lax.broadcasted_iota(jnp.int32, (LANES, WINDOW), 0)
    top = sub < HEAD_DIM
    first = jnp.where(s == 0, 1, 0)
    nt = (((1,), (1,)), ((), ()))
    zero = jnp.zeros((), BF16)

    tiles = [(sq, n, j) for n in range(ATT_NB) for j in range(2) for sq in range(ATT_NSEQ)]

    def scores(t, qs_all):
        sq, n, j = tiles[t]
        q = qs_all[sq]
        r0 = n * WINDOW
        k_tile = kv_ext[sq, r0:r0 + 2 * WINDOW, j * LANES:(j + 1) * LANES]
        parts = []
        for m in range(4):
            p = 4 * j + m
            qt = q[r0:r0 + WINDOW, p * LANES:(p + 1) * LANES]
            parts.append(jnp.where(low, qt, zero))
            parts.append(jnp.where(low, zero, qt))
        qs = jnp.concatenate(parts, axis=0)
        bias_sel = first if n == 0 else 0
        s_bufs[t % 2][...] = (lax.dot_general(k_tile, qs, nt, preferred_element_type=F32)
                              + bias_ref[bias_sel, j])

    def softmax_pv(t):
        sq, n, j = tiles[t]
        r0 = n * WINDOW
        s_sc = s_bufs[t % 2]
        p_sc = p_bufs[t % 2]
        inv_l = []
        for h in range(8):
            hc = slice(h * WINDOW, (h + 1) * WINDOW)
            sink = sink_ref[8 * j + h] * LOG2E
            mx = jnp.maximum(jnp.max(s_sc[:, hc], axis=0, keepdims=True), sink)
            pr = jnp.exp2(s_sc[:, hc] - mx)
            p_sc[:, hc] = pr.astype(BF16)
            inv_l.append(1.0 / (jnp.sum(pr, axis=0, keepdims=True) + jnp.exp2(sink - mx)))
        v_tile = kv_ext[sq, r0:r0 + 2 * WINDOW, KV_DIM + j * LANES:KV_DIM + (j + 1) * LANES]
        v_t = v_tile.astype(F32).T.astype(BF16)
        ov = jnp.dot(v_t, p_sc[...], preferred_element_type=F32)
        for m in range(4):
            p = 4 * j + m
            o_even = ov[:, (2 * m) * WINDOW:(2 * m + 1) * WINDOW] * inv_l[2 * m]
            o_odd = ov[:, (2 * m + 1) * WINDOW:(2 * m + 2) * WINDOW] * inv_l[2 * m + 1]
            o_sc[sq, p * LANES:(p + 1) * LANES, r0:r0 + WINDOW] = jnp.where(top, o_even, o_odd).astype(BF16)

    def project_out(sq):
        return jnp.dot(wo_ref[...], o_sc[sq], preferred_element_type=F32)

    def finish(sq, x, out_t):
        xn = _layer_norm_rows(ALPHA * x + out_t.T, g_ref[...], b_ref[...])
        o_ref[sq] = xn
        opk_ref[sq] = _pack_row_halves(xn)

    xs = [layer_input(0)]
    qs_all = [project_qkv(0, xs[0])]
    xs.append(layer_input(1))
    qs_all.append(project_qkv(1, xs[1]))
    scores(0, qs_all)
    for t in range(len(tiles)):
        if t + 1 < len(tiles):
            scores(t + 1, qs_all)
        softmax_pv(t)
    kv_ext[:, 0:WINDOW, :] = kv_ext[:, tq:tq + WINDOW, :]
    outs_t = [project_out(sq) for sq in range(ATT_NSEQ)]
    for sq in range(ATT_NSEQ):
        finish(sq, xs[sq], outs_t[sq])


def _attn_bias():
    qi = np.arange(WINDOW)[:, None]
    sj = np.arange(2 * WINDOW)[None, :]
    dist = qi - sj + WINDOW
    valid = (dist >= 0) & (dist < WINDOW)
    slopes = 2.0 ** (-8.0 * np.arange(1, N_HEADS + 1, dtype=np.float32) / N_HEADS)
    slopes = slopes.astype(np.float32)[ATT_HEAD_ORDER]
    sb = -(slopes[:, None, None] * dist.astype(np.float32)[None])
    later = np.where(valid[None], sb, -np.inf)
    first = np.where((valid & (sj >= WINDOW))[None], sb, -np.inf)
    bias = np.stack([later, first]).astype(np.float32) * np.float32(LOG2E)
    bias = bias.reshape(2, 2, 8, WINDOW, 2 * WINDOW).transpose(0, 1, 4, 2, 3).reshape(2, 2, 2 * WINDOW, 8 * WINDOW)
    return jnp.asarray(np.ascontiguousarray(bias))


def _attn_layer(x_prev, y1, y2, gates, g_prev, b_prev, w_qkv, sinks, w_o, ln_g, ln_b):
    B, S, D = x_prev.shape
    steps = S // ATT_TQ
    assert ATT_HEAD_ORDER == list(np.arange(N_HEADS).reshape(2, 2, 4).transpose(0, 2, 1).reshape(-1))
    wq = w_qkv[:, :Q_DIM].reshape(D, 2, 2, 4, HEAD_DIM).transpose(0, 1, 3, 2, 4).reshape(D, Q_DIM)
    wqkv = jnp.concatenate([wq, w_qkv[:, Q_DIM:]], axis=1).astype(BF16)
    wo_t = w_o.reshape(2, 2, 4, HEAD_DIM, D).transpose(0, 2, 1, 3, 4).reshape(Q_DIM, D).T.astype(BF16)
    sink = sinks.reshape(2, 2, 4).transpose(0, 2, 1).reshape(N_HEADS, 1, 1)
    bias = _attn_bias()
    const = lambda shape: pl.BlockSpec(shape, lambda b, s: (0,) * len(shape))
    tile = lambda w: pl.BlockSpec((ATT_NSEQ, ATT_TQ, w), lambda b, s: (b, s, 0))
    gate_rows = lambda sq: pl.BlockSpec((SUBLANES, ATT_TQ), lambda b, s: (0, (ATT_NSEQ * b + sq) * steps + s))
    return pl.pallas_call(
        _attn_kernel,
        grid=(B // ATT_NSEQ, steps),
        in_specs=[
            tile(D),
            tile(D_HALF),
            tile(D_HALF),
            gate_rows(0),
            gate_rows(1),
            const((1, D)),
            const((1, D)),
            const((D, Q_DIM + 2 * KV_DIM)),
            const((2, 2, 2 * WINDOW, 8 * WINDOW)),
            const((N_HEADS, 1, 1)),
            const((D, Q_DIM)),
            const((1, D)),
            const((1, D)),
        ],
        out_specs=[tile(D), tile(D_HALF)],
        out_shape=[jax.ShapeDtypeStruct((B, S, D), F32), jax.ShapeDtypeStruct((B, S, D_HALF), U32)],
        scratch_shapes=[
            pltpu.VMEM((ATT_NSEQ, ATT_TQ + WINDOW, 2 * KV_DIM), BF16),
            pltpu.VMEM((ATT_NSEQ, Q_DIM, ATT_TQ), BF16),
            pltpu.VMEM((2 * WINDOW, 8 * WINDOW), F32),
            pltpu.VMEM((2 * WINDOW, 8 * WINDOW), F32),
            pltpu.VMEM((2 * WINDOW, 8 * WINDOW), BF16),
            pltpu.VMEM((2 * WINDOW, 8 * WINDOW), BF16),
        ],
        compiler_params=pltpu.CompilerParams(
            dimension_semantics=("arbitrary", "arbitrary"), vmem_limit_bytes=VMEM_LIMIT),
        name="swa_attn_ln",
    )(x_prev, y1.reshape(B, S, D_HALF), y2.reshape(B, S, D_HALF), gates, gates, g_prev.reshape(1, D),
      b_prev.reshape(1, D),
      wqkv, bias, sink, wo_t, ln_g.reshape(1, D), ln_b.reshape(1, D))


def kernel(x, rec_w_in, rec_conv_w, rec_conv_b, rec_w_r, rec_b_r, rec_w_i, rec_b_i, rec_lambda, rec_w_out,
           att_w_qkv, att_sinks, att_w_o, moe_w_group, moe_b_group, moe_w_expert, moe_b_expert,
           moe_w1, moe_w3, moe_w2, ln_g, ln_b):
    assert DEPTH == 2
    B, S, D = x.shape

    def moe(layer, xin, xin_pk):
        return _hier_moe(xin, xin_pk, moe_w_group[layer], moe_b_group[layer], moe_w_expert[layer],
                         moe_b_expert[layer], moe_w1, moe_w3, moe_w2, layer)

    x1, x1_pk = _rglru_layer(x, rec_w_in[0], rec_conv_w[0], rec_conv_b[0], rec_w_r[0], rec_b_r[0], rec_w_i[0],
                             rec_b_i[0], rec_lambda[0], rec_w_out[0], ln_g[0, 0], ln_b[0, 0])
    y1, y2, gates = moe(0, x1, x1_pk)
    x3, x3_pk = _attn_layer(x1, y1, y2, gates, ln_g[0, 1], ln_b[0, 1], att_w_qkv[0], att_sinks[0], att_w_o[0],
                            ln_g[1, 0], ln_b[1, 0])
    y1, y2, gates = moe(1, x3, x3_pk)
    out = _combine_ln(x3.reshape(B * S, D), y1, y2, gates, ln_g[1, 1], ln_b[1, 1])
    return out.reshape(B, S, D)
```

```python
import functools

import jax
import jax.numpy as jnp
import numpy as np
from jax import lax
from jax.experimental import pallas as pl
from jax.experimental.pallas import tpu as pltpu
from jax.experimental.pallas import tpu_sc as plsc

F32 = jnp.float32
BF16 = jnp.bfloat16
U32 = jnp.uint32

D_MODEL = 1024
DEPTH = 2
D_RNN = 1280
LRU_BLOCKS = 16
LRU_BLOCK_W = D_RNN // LRU_BLOCKS
CONV_W = 4
LRU_C = 8.0
N_HEADS = 16
N_KV_HEADS = 4
HEAD_DIM = 64
WINDOW = 128
Q_DIM = N_HEADS * HEAD_DIM
KV_DIM = N_KV_HEADS * HEAD_DIM
N_GROUPS = 4
EXPERTS_PER_GROUP = 8
N_EXPERTS = N_GROUPS * EXPERTS_PER_GROUP
TOP_K = 2
D_EXPERT = 512
ALPHA = (2 * DEPTH) ** 0.25
LN_EPS = 1e-5
LOG2E = 1.4426950408889634

LANES = 128
SUBLANES = 8
VMEM_LIMIT = 56 * 1024 * 1024

REC_TS = 256
REC_GROUPS = REC_TS // SUBLANES
REC_NSEQ = 2
GATE_TILE = 256
GATE_WIN = 512
GATE_WIN_STARTS = (0, 128, 384, 640, 768)
N_GATE_TILES = D_RNN // GATE_TILE

ROUTE_T = 512
ROUTE_ROWS = 40

MOE_BM = 1024
MOE_SUB = 256

ATT_TQ = 256
ATT_NB = ATT_TQ // WINDOW
ATT_NSEQ = 2

COMB_T = 1024

D_HALF = D_MODEL // 2

SC_CORES = 2
SC_SUBCORES = 16
SC_WORKERS = SC_CORES * SC_SUBCORES
SC_WIN = 64


def _layer_norm_rows(z, g, b):
    mu = jnp.mean(z, axis=-1, keepdims=True)
    zc = z - mu
    var = jnp.mean(zc * zc, axis=-1, keepdims=True)
    return zc * lax.rsqrt(var + LN_EPS) * g + b


def _pack_bf16_pair(a, b):
    ua = lax.bitcast_convert_type(a.astype(BF16).astype(F32), U32)
    ub = lax.bitcast_convert_type(b.astype(BF16).astype(F32), U32)
    return (ua >> 16) | (ub & jnp.uint32(0xFFFF0000))


def _unpack_bf16_pair(w):
    a = lax.bitcast_convert_type(w << 16, F32)
    b = lax.bitcast_convert_type(w & jnp.uint32(0xFFFF0000), F32)
    return a, b


def _pack_row_halves(x):
    return _pack_bf16_pair(x[:, :D_HALF], x[:, D_HALF:])


def _router_logits_t(x, w_split, bias):
    xhi = x.astype(BF16)
    xlo = (x - xhi.astype(F32)).astype(BF16)
    nt = (((1,), (1,)), ((), ()))
    both = lax.dot_general(w_split, xhi, nt, preferred_element_type=F32)
    low = lax.dot_general(w_split[:ROUTE_ROWS], xlo, nt, preferred_element_type=F32)
    return both[:ROUTE_ROWS] + both[ROUTE_ROWS:] + low + bias


def _rglru_kernel(x_ref, perm_ref, perm_t_ref, w_in_ref, convw_ref, convb_ref, wg_ref, br_ref, bi_ref, lam_ref,
                  w_out_ref, g_ref, b_ref, wrt_ref, brt_ref, o_ref, opk_ref, lg_ref,
                  xr_ext, tail_sc, a_sc, u_sc, h_carry, gate_sc):
    s = pl.program_id(1)
    ts = REC_TS
    halo = (CONV_W - 1) * SUBLANES
    seqs = range(REC_NSEQ)

    @pl.when(s == 0)
    def _():
        tail_sc[...] = jnp.zeros((REC_NSEQ, halo, D_RNN), F32)
        h_carry[...] = jnp.zeros((REC_NSEQ, 1, D_RNN), F32)

    row = lax.broadcasted_iota(jnp.int32, (SUBLANES, D_RNN), 0)
    nlam = -lam_ref[...]
    sp = jnp.maximum(nlam, 0.0) + jnp.log1p(jnp.exp(-jnp.abs(nlam)))
    log2a_scale = (-LRU_C * LOG2E) * sp

    def project(q):
        xp = jnp.dot(perm_ref[...], x_ref[q].astype(BF16), preferred_element_type=F32).astype(BF16)
        proj = jnp.dot(xp, w_in_ref[...], preferred_element_type=F32)
        gate_sc[q] = proj[:, :D_RNN]
        return proj[:, D_RNN:]

    def conv_gates(q, xr):
        for k in range(CONV_W - 1):
            r0 = ts - halo + k * SUBLANES
            cur = xr[r0:r0 + SUBLANES, :]
            prev = tail_sc[q, k * SUBLANES:(k + 1) * SUBLANES, :]
            xr_ext[q, k * SUBLANES:(k + 1) * SUBLANES, :] = jnp.where(
                row == 0, pltpu.roll(prev, 1, axis=0), pltpu.roll(cur, 1, axis=0))
        tail_sc[q] = xr[ts - halo:, :]
        xr_ext[q, halo:halo + ts, :] = xr
        xc = convb_ref[...] + convw_ref[CONV_W - 1:CONV_W, :] * xr
        for k in range(CONV_W - 1):
            xc = xc + convw_ref[k:k + 1, :] * xr_ext[q, k * SUBLANES:k * SUBLANES + ts, :]
        xcb = xc.astype(BF16)
        pres = [jnp.dot(xcb[:, GATE_WIN_STARTS[j]:GATE_WIN_STARTS[j] + GATE_WIN], wg_ref[j],
                        preferred_element_type=F32) for j in range(N_GATE_TILES)]
        for j, pre in enumerate(pres):
            cs = j * GATE_TILE
            r = jax.nn.sigmoid(pre[:, :GATE_TILE] + br_ref[:, cs:cs + GATE_TILE])
            i = jax.nn.sigmoid(pre[:, GATE_TILE:] + bi_ref[:, cs:cs + GATE_TILE])
            a = jnp.exp2(r * log2a_scale[:, cs:cs + GATE_TILE])
            s1 = 1.0 - a * a
            mult = jnp.where(s1 > 0.0, s1 * lax.rsqrt(s1), 0.0)
            u = mult * (i * xc[:, cs:cs + GATE_TILE])
            a_sc[q, :, cs:cs + GATE_TILE] = a
            u_sc[q, :, cs:cs + GATE_TILE] = u

    def segment_scan(q):
        h = jnp.zeros((SUBLANES, D_RNN), F32)
        prod = jnp.ones((SUBLANES, D_RNN), F32)
        for gidx in range(REC_GROUPS):
            rows = slice(gidx * SUBLANES, (gidx + 1) * SUBLANES)
            a8 = a_sc[q, rows, :]
            h = a8 * h + u_sc[q, rows, :]
            prod = a8 * prod
            u_sc[q, rows, :] = h
            a_sc[q, rows, :] = prod
        return h, prod

    def recur_out(q, seg):
        seg_h, seg_a = seg
        for d in (1, 2, 4):
            keep = row >= d
            a_sh = jnp.where(keep, pltpu.roll(seg_a, d, axis=0), 1.0)
            h_sh = jnp.where(keep, pltpu.roll(seg_h, d, axis=0), 0.0)
            seg_h = seg_a * h_sh + seg_h
            seg_a = seg_a * a_sh
        h_in = h_carry[q]
        after = seg_a * h_in + seg_h
        enter = jnp.where(row == 0, h_in, pltpu.roll(after, 1, axis=0))
        h_carry[q] = after[SUBLANES - 1:SUBLANES, :]
        hs = (u_sc[q].reshape(REC_GROUPS, SUBLANES, D_RNN)
              + a_sc[q].reshape(REC_GROUPS, SUBLANES, D_RNN) * enter[None]).reshape(ts, D_RNN)
        y = hs * jax.nn.gelu(gate_sc[q])
        y_t = jnp.dot(perm_t_ref[...], y.astype(BF16), preferred_element_type=F32).astype(BF16)
        return jnp.dot(y_t, w_out_ref[...], preferred_element_type=F32)

    def finish(q, out):
        z = ALPHA * x_ref[q] + out
        xn = _layer_norm_rows(z, g_ref[...], b_ref[...])
        o_ref[q] = xn
        opk_ref[q] = _pack_row_halves(xn)
        lg_ref[q] = _router_logits_t(xn, wrt_ref[...], brt_ref[...])

    assert REC_NSEQ == 2
    xr_a = project(0)
    conv_gates(0, xr_a)
    xr_b = project(1)
    out_a = recur_out(0, segment_scan(0))
    conv_gates(1, xr_b)
    finish(0, out_a)
    out_b = recur_out(1, segment_scan(1))
    finish(1, out_b)


def _band_gate_weights(w_r, w_i):
    spread = jnp.asarray(np.tile(np.eye(LRU_BLOCK_W, dtype=np.float32), (1, LRU_BLOCKS)), BF16)
    blk = np.arange(D_RNN) // LRU_BLOCK_W
    on_diag = jnp.asarray(blk[:, None] == blk[None, :])

    def dense(w):
        rows = w.reshape(D_RNN, LRU_BLOCK_W).astype(BF16)
        return jnp.where(on_diag, jnp.dot(rows, spread, preferred_element_type=F32), 0.0)

    wr, wi = dense(w_r), dense(w_i)
    tiles = []
    for j in range(N_GATE_TILES):
        ws = GATE_WIN_STARTS[j]
        cs = j * GATE_TILE
        lo_blk = cs // LRU_BLOCK_W
        hi_blk = (cs + GATE_TILE - 1) // LRU_BLOCK_W
        assert ws <= lo_blk * LRU_BLOCK_W and (hi_blk + 1) * LRU_BLOCK_W <= ws + GATE_WIN
        tiles.append(jnp.concatenate([wr[ws:ws + GATE_WIN, cs:cs + GATE_TILE],
                                      wi[ws:ws + GATE_WIN, cs:cs + GATE_TILE]], axis=1))
    return jnp.stack(tiles).astype(BF16)


def _rglru_layer(x, w_in, conv_w, conv_b, w_r, b_r, w_i, b_i, lam, w_out, ln_g, ln_b, router_w, router_b):
    B, S, D = x.shape
    wg = _band_gate_weights(w_r, w_i)
    rho = np.arange(REC_TS)
    perm_np = np.zeros((REC_TS, REC_TS), np.float32)
    perm_np[rho, (rho % SUBLANES) * REC_GROUPS + rho // SUBLANES] = 1.0
    perm = jnp.asarray(perm_np, BF16)
    perm_t = jnp.asarray(perm_np.T, BF16)
    row = lambda v: v.reshape(1, -1)
    const = lambda shape: pl.BlockSpec(shape, lambda b, s: (0,) * len(shape))
    tile = lambda w: pl.BlockSpec((REC_NSEQ, REC_TS, w), lambda b, s: (b, s, 0))
    halo = (CONV_W - 1) * SUBLANES
    return pl.pallas_call(
        _rglru_kernel,
        grid=(B // REC_NSEQ, S // REC_TS),
        in_specs=[
            tile(D),
            const((REC_TS, REC_TS)),
            const((REC_TS, REC_TS)),
            const((D, 2 * D_RNN)),
            const((CONV_W, D_RNN)),
            const((1, D_RNN)),
            const((N_GATE_TILES, GATE_WIN, 2 * GATE_TILE)),
            const((1, D_RNN)),
            const((1, D_RNN)),
            const((1, D_RNN)),
            const((D_RNN, D)),
            const((1, D)),
            const((1, D)),
            const((2 * ROUTE_ROWS, D)),
            const((ROUTE_ROWS, 1)),
        ],
        out_specs=[tile(D), tile(D_HALF),
                   pl.BlockSpec((REC_NSEQ, ROUTE_ROWS, REC_TS), lambda b, s: (b, 0, s))],
        out_shape=[jax.ShapeDtypeStruct((B, S, D), F32), jax.ShapeDtypeStruct((B, S, D_HALF), U32),
                   jax.ShapeDtypeStruct((B, ROUTE_ROWS, S), F32)],
        scratch_shapes=[
            pltpu.VMEM((REC_NSEQ, halo + REC_TS, D_RNN), F32),
            pltpu.VMEM((REC_NSEQ, halo, D_RNN), F32),
            pltpu.VMEM((REC_NSEQ, REC_TS, D_RNN), F32),
            pltpu.VMEM((REC_NSEQ, REC_TS, D_RNN), F32),
            pltpu.VMEM((REC_NSEQ, 1, D_RNN), F32),
            pltpu.VMEM((REC_NSEQ, REC_TS, D_RNN), F32),
        ],
        compiler_params=pltpu.CompilerParams(
            dimension_semantics=("arbitrary", "arbitrary"), vmem_limit_bytes=VMEM_LIMIT),
        name="rglru_ln",
    )(x, perm, perm_t, w_in.astype(BF16), conv_w, row(conv_b), wg, row(b_r), row(b_i), row(lam), w_out.astype(BF16),
      row(ln_g), row(ln_b), router_w, router_b)


def _router_kernel(logits_ref, tri_ref, idx_ref, gate_ref, cnt_ref, base_sc):
    step = pl.program_id(0)
    tr = ROUTE_T

    @pl.when(step == 0)
    def _():
        base_sc[...] = jnp.zeros((N_EXPERTS, 1), F32)

    logits = logits_ref[0]

    row8 = lax.broadcasted_iota(jnp.int32, (SUBLANES, tr), 0).astype(F32)
    neg_inf = -jnp.inf
    g = jnp.where(row8 < N_GROUPS, logits[N_EXPERTS:N_EXPERTS + SUBLANES, :], neg_inf)
    gmax = jnp.max(g, axis=0, keepdims=True)
    gidx = jnp.min(jnp.where(g == gmax, row8, SUBLANES), axis=0, keepdims=True)
    g_gate = 1.0 / jnp.sum(jnp.exp(g - gmax), axis=0, keepdims=True)

    esel = logits[0:EXPERTS_PER_GROUP, :]
    for grp in range(1, N_GROUPS):
        esel = jnp.where(gidx == grp, logits[grp * EXPERTS_PER_GROUP:(grp + 1) * EXPERTS_PER_GROUP, :], esel)
    v1 = jnp.max(esel, axis=0, keepdims=True)
    i1 = jnp.min(jnp.where(esel == v1, row8, SUBLANES), axis=0, keepdims=True)
    esel2 = jnp.where(row8 == i1, neg_inf, esel)
    v2 = jnp.max(esel2, axis=0, keepdims=True)
    i2 = jnp.min(jnp.where(esel2 == v2, row8, SUBLANES), axis=0, keepdims=True)
    e21 = jnp.exp(v2 - v1)
    inv = 1.0 / (1.0 + e21)
    gate1 = inv * g_gate
    gate2 = e21 * inv * g_gate
    e1 = gidx * EXPERTS_PER_GROUP + i1
    e2 = gidx * EXPERTS_PER_GROUP + i2

    rowe = lax.broadcasted_iota(jnp.int32, (N_EXPERTS, tr), 0).astype(F32)
    hit1 = rowe == e1
    hit2 = rowe == e2
    member = jnp.where(hit1, 1.0, jnp.where(hit2, 1.0, 0.0))
    before = jnp.dot(member.astype(BF16), tri_ref[...], preferred_element_type=F32) + base_sc[...]
    rank1 = jnp.sum(jnp.where(hit1, before, 0.0), axis=0, keepdims=True)
    rank2 = jnp.sum(jnp.where(hit2, before, 0.0), axis=0, keepdims=True)
    base_sc[...] = base_sc[...] + jnp.sum(member, axis=1, keepdims=True)

    zi = jnp.zeros((1, tr), jnp.int32)
    idx_ref[...] = jnp.concatenate(
        [e1.astype(jnp.int32), e2.astype(jnp.int32), rank1.astype(jnp.int32), rank2.astype(jnp.int32),
         zi, zi, zi, zi], axis=0)
    zf = jnp.zeros((1, tr), F32)
    gate_ref[...] = jnp.concatenate([gate1, gate2, zf, zf, zf, zf, zf, zf], axis=0)
    cnt_ref[...] = jnp.broadcast_to(base_sc[...], (N_EXPERTS, LANES)).astype(jnp.int32)


def _router_weights(w_rg, b_rg, w_re, b_re):
    D = w_rg.shape[0]
    pad_rows = ROUTE_ROWS - N_EXPERTS - N_GROUPS
    w = jnp.concatenate([w_re.T, w_rg.T, jnp.zeros((pad_rows, D), F32)], axis=0)
    whi = w.astype(BF16)
    wlo = (w - whi.astype(F32)).astype(BF16)
    w_split = jnp.concatenate([whi, wlo], axis=0)
    bias = jnp.concatenate([b_re, b_rg, jnp.zeros((pad_rows,), F32)]).reshape(ROUTE_ROWS, 1)
    return w_split, bias


def _router(logits_t):
    B, _, S = logits_t.shape
    T = B * S
    per_row = S // ROUTE_T
    tri = jnp.asarray(np.triu(np.ones((ROUTE_T, ROUTE_T), np.float32), 1), BF16)
    const = lambda shape: pl.BlockSpec(shape, lambda i: (0,) * len(shape))
    return pl.pallas_call(
        _router_kernel,
        grid=(T // ROUTE_T,),
        in_specs=[
            pl.BlockSpec((1, ROUTE_ROWS, ROUTE_T), lambda i: (i // per_row, 0, i % per_row)),
            const((ROUTE_T, ROUTE_T)),
        ],
        out_specs=[
            pl.BlockSpec((SUBLANES, ROUTE_T), lambda i: (0, i)),
            pl.BlockSpec((SUBLANES, ROUTE_T), lambda i: (0, i)),
            const((N_EXPERTS, LANES)),
        ],
        out_shape=[
            jax.ShapeDtypeStruct((SUBLANES, T), jnp.int32),
            jax.ShapeDtypeStruct((SUBLANES, T), F32),
            jax.ShapeDtypeStruct((N_EXPERTS, LANES), jnp.int32),
        ],
        scratch_shapes=[pltpu.VMEM((N_EXPERTS, 1), F32)],
        compiler_params=pltpu.CompilerParams(
            dimension_semantics=("arbitrary",), vmem_limit_bytes=VMEM_LIMIT),
        name="router",
    )(logits_t, tri)


def _sc_mesh():
    return plsc.VectorSubcoreMesh(core_axis_name="c", subcore_axis_name="s",
                                  num_cores=SC_CORES, num_subcores=SC_SUBCORES)


def _sc_worker_id():
    return lax.axis_index("s") * SC_CORES + lax.axis_index("c")


def _sc_scratch(n_win, width):
    return [
        pltpu.VMEM((n_win, SC_WIN), jnp.int32),
        pltpu.VMEM((n_win, SC_WIN), jnp.int32),
        pltpu.VMEM((2, SC_WIN, width), U32),
        pltpu.SemaphoreType.DMA((2,)),
        pltpu.SemaphoreType.DMA((2,)),
    ]


def _sc_dispatch(rows, idx1, idx2, n_rows):
    _, width = rows.shape
    _, n_win, _ = idx1.shape

    @functools.partial(
        pl.kernel, mesh=_sc_mesh(), out_type=jax.ShapeDtypeStruct((n_rows, width), rows.dtype),
        scratch_types=_sc_scratch(n_win, width), name="sc_dispatch")
    def run(rows_hbm, i1_hbm, i2_hbm, o_hbm, i1_v, i2_v, buf, rsem, wsem):
        wid = _sc_worker_id()
        base = wid * (n_win * SC_WIN)
        pltpu.sync_copy(i1_hbm.at[wid], i1_v)
        pltpu.sync_copy(i2_hbm.at[wid], i2_v)

        def read(j):
            return pltpu.async_copy(rows_hbm.at[pl.ds(base + j * SC_WIN, SC_WIN)], buf.at[j % 2], rsem.at[j % 2])

        reads = {0: read(0)}
        writes = {}
        for j in range(n_win):
            if j + 1 < n_win:
                for d in writes.pop(j - 1, ()):
                    d.wait()
                reads[j + 1] = read(j + 1)
            reads.pop(j).wait()
            writes[j] = (pltpu.async_copy(buf.at[j % 2], o_hbm.at[i1_v.at[j]], wsem.at[j % 2]),
                         pltpu.async_copy(buf.at[j % 2], o_hbm.at[i2_v.at[j]], wsem.at[j % 2]))
        for j in sorted(writes):
            for d in writes[j]:
                d.wait()

    return run(rows, idx1, idx2)


def _sc_gather_pair(table, idx1, idx2):
    _, width = table.shape
    _, n_win, _ = idx1.shape
    n_tok = SC_WORKERS * n_win * SC_WIN
    out_t = jax.ShapeDtypeStruct((n_tok, width), table.dtype)

    @functools.partial(
        pl.kernel, mesh=_sc_mesh(), out_type=(out_t, out_t),
        scratch_types=_sc_scratch(n_win, width), name="sc_combine_gather")
    def run(table_hbm, i1_hbm, i2_hbm, o1_hbm, o2_hbm, i1_v, i2_v, buf, gsem, wsem):
        wid = _sc_worker_id()
        base = wid * (n_win * SC_WIN)
        pltpu.sync_copy(i1_hbm.at[wid], i1_v)
        pltpu.sync_copy(i2_hbm.at[wid], i2_v)
        work = [(i1_v, o1_hbm, j) for j in range(n_win)] + [(i2_v, o2_hbm, j) for j in range(n_win)]

        def gather(t):
            iv, _, j = work[t]
            return pltpu.async_copy(table_hbm.at[iv.at[j]], buf.at[t % 2], gsem.at[t % 2])

        def put(t):
            _, oh, j = work[t]
            return pltpu.async_copy(buf.at[t % 2], oh.at[pl.ds(base + j * SC_WIN, SC_WIN)], wsem.at[t % 2])

        gathers = {0: gather(0)}
        puts = {}
        for t in range(len(work)):
            if t + 1 < len(work):
                if t - 1 in puts:
                    puts.pop(t - 1).wait()
                gathers[t + 1] = gather(t + 1)
            gathers.pop(t).wait()
            puts[t] = put(t)
        for t in sorted(puts):
            puts[t].wait()

    return run(table, idx1, idx2)


def _moe_kernel(layer, be_ref, slot_ref, nxt_ref, valid_ref, x_ref, w1_hbm, w3_hbm, w2_hbm, o_ref,
                w1_st, w3_st, w2_st, w1_sc, w3_sc, w2_sc, sems):
    i = pl.program_id(0)
    expert = be_ref[i]
    slot = slot_ref[i]
    new_expert = jnp.logical_or(i == 0, expert != be_ref[jnp.maximum(i - 1, 0)])

    def weight_copies(e, sl):
        return [pltpu.make_async_copy(hbm.at[layer, e], stage.at[sl], sems.at[k, sl])
                for k, (hbm, stage) in enumerate(((w1_hbm, w1_st), (w3_hbm, w3_st), (w2_hbm, w2_st)))]

    @pl.when(i == 0)
    def _():
        for cp in weight_copies(expert, slot):
            cp.start()

    @pl.when(new_expert)
    def _():
        for cp in weight_copies(expert, slot):
            cp.wait()
        w1_sc[...] = w1_st[slot].astype(BF16)
        w3_sc[...] = w3_st[slot].astype(BF16)
        w2_sc[...] = w2_st[slot].astype(BF16)
        nxt = nxt_ref[i]

        @pl.when(nxt >= 0)
        def _():
            for cp in weight_copies(nxt, 1 - slot):
                cp.start()

    def up(rows):
        xa, xb = _unpack_bf16_pair(x_ref[rows, :])
        xa = xa.astype(BF16)
        xb = xb.astype(BF16)
        h1 = (jnp.dot(xa, w1_sc[:D_HALF], preferred_element_type=F32)
              + jnp.dot(xb, w1_sc[D_HALF:], preferred_element_type=F32))
        h3 = (jnp.dot(xa, w3_sc[:D_HALF], preferred_element_type=F32)
              + jnp.dot(xb, w3_sc[D_HALF:], preferred_element_type=F32))
        return h1, h3

    def down(rows, h1, h3):
        hdn = (jax.nn.silu(h1) * h3).astype(BF16)
        y = jnp.dot(hdn, w2_sc[...], preferred_element_type=F32)
        o_ref[rows, :] = _pack_row_halves(y)

    n_sub = MOE_BM // MOE_SUB
    subs = [slice(k * MOE_SUB, (k + 1) * MOE_SUB) for k in range(n_sub)]
    valid = valid_ref[i]
    chains = (valid + (MOE_SUB - 1)) // MOE_SUB

    for live in range(n_sub + 1):
        @pl.when(chains == live)
        def _(live=live):
            ups = {}
            if live:
                ups[0] = up(subs[0])
            for k in range(live):
                if k + 1 < live:
                    ups[k + 1] = up(subs[k + 1])
                down(subs[k], *ups.pop(k))
            if live < n_sub:
                o_ref[live * MOE_SUB:, :] = jnp.zeros((MOE_BM - live * MOE_SUB, D_HALF), o_ref.dtype)


def _moe_blocks(xbuf, block_e, valid, w1, w3, w2, layer):
    n_rows, _ = xbuf.shape
    D = D_MODEL
    n_blocks = n_rows // MOE_BM
    pos = jnp.arange(n_blocks, dtype=jnp.int32)
    is_new = jnp.concatenate([jnp.ones((1,), bool), block_e[1:] != block_e[:-1]])
    slot = ((jnp.cumsum(is_new.astype(jnp.int32)) - 1) % 2).astype(jnp.int32)
    change_pos = jnp.where(is_new, pos, n_blocks)
    next_change = jnp.concatenate([lax.cummin(change_pos, reverse=True)[1:], jnp.full((1,), n_blocks, jnp.int32)])
    nxt = jnp.where(next_change < n_blocks, block_e[jnp.minimum(next_change, n_blocks - 1)], -1).astype(jnp.int32)
    rows = lambda i, be, sl, nx, nu: (i, 0)
    grid_spec = pltpu.PrefetchScalarGridSpec(
        num_scalar_prefetch=4,
        grid=(n_blocks,),
        in_specs=[
            pl.BlockSpec((MOE_BM, D_HALF), rows),
            pl.BlockSpec(memory_space=pl.ANY),
            pl.BlockSpec(memory_space=pl.ANY),
            pl.BlockSpec(memory_space=pl.ANY),
        ],
        out_specs=pl.BlockSpec((MOE_BM, D_HALF), rows),
        scratch_shapes=[
            pltpu.VMEM((2, D, D_EXPERT), F32),
            pltpu.VMEM((2, D, D_EXPERT), F32),
            pltpu.VMEM((2, D_EXPERT, D), F32),
            pltpu.VMEM((D, D_EXPERT), BF16),
            pltpu.VMEM((D, D_EXPERT), BF16),
            pltpu.VMEM((D_EXPERT, D), BF16),
            pltpu.SemaphoreType.DMA((3, 2)),
        ],
    )
    return pl.pallas_call(
        functools.partial(_moe_kernel, layer),
        grid_spec=grid_spec,
        out_shape=jax.ShapeDtypeStruct((n_rows, D_HALF), U32),
        compiler_params=pltpu.CompilerParams(
            dimension_semantics=("arbitrary",), vmem_limit_bytes=VMEM_LIMIT),
        name="moe_experts",
    )(block_e, slot, nxt, valid, xbuf, w1, w3, w2)


def _moe_combine_norm(x, y1, y2, gate_rows, g, b):
    n = x.shape[0]
    pad = jnp.zeros((LANES - SUBLANES, n), F32)
    gates = jnp.concatenate([gate_rows, pad], axis=0).T
    g1 = gates[:, 0:1]
    g2 = gates[:, 1:2]
    a1, b1 = _unpack_bf16_pair(y1)
    a2, b2 = _unpack_bf16_pair(y2)
    f = jnp.concatenate([g1 * a1 + g2 * a2, g1 * b1 + g2 * b2], axis=1)
    return _layer_norm_rows(ALPHA * x + f, g, b)


def _combine_kernel(x_ref, y1_ref, y2_ref, gates_ref, g_ref, b_ref, o_ref):
    o_ref[...] = _moe_combine_norm(x_ref[...], y1_ref[...], y2_ref[...], gates_ref[...], g_ref[...], b_ref[...])


def _combine_ln(xf, y1, y2, gates, ln_g, ln_b):
    T, D = xf.shape
    const = lambda shape: pl.BlockSpec(shape, lambda i: (0,) * len(shape))
    rows = lambda w: pl.BlockSpec((COMB_T, w), lambda i: (i, 0))
    return pl.pallas_call(
        _combine_kernel,
        grid=(T // COMB_T,),
        in_specs=[rows(D), rows(D_HALF), rows(D_HALF), pl.BlockSpec((SUBLANES, COMB_T), lambda i: (0, i)),
                  const((1, D)), const((1, D))],
        out_specs=rows(D),
        out_shape=jax.ShapeDtypeStruct((T, D), F32),
        compiler_params=pltpu.CompilerParams(
            dimension_semantics=("arbitrary",), vmem_limit_bytes=VMEM_LIMIT),
        name="moe_combine_ln",
    )(xf, y1, y2, gates, ln_g.reshape(1, D), ln_b.reshape(1, D))


def _hier_moe(xpk, logits_t, w1, w3, w2, layer):
    B, S, _ = xpk.shape
    D = D_MODEL
    T = B * S
    idx, gates, cnt = _router(logits_t)
    counts = cnt[:, 0]
    padded = ((counts + MOE_BM - 1) // MOE_BM) * MOE_BM
    pends = jnp.cumsum(padded)
    pstarts = pends - padded
    experts = jnp.arange(N_EXPERTS, dtype=jnp.int32)[:, None]

    def dest(e_row, rank_row):
        return jnp.sum(jnp.where(e_row[None, :] == experts, pstarts[:, None], 0), axis=0) + rank_row

    n_win = T // (SC_WORKERS * SC_WIN)
    dest1 = dest(idx[0], idx[2]).reshape(SC_WORKERS, n_win, SC_WIN)
    dest2 = dest(idx[1], idx[3]).reshape(SC_WORKERS, n_win, SC_WIN)
    n_blocks = -(-(T * TOP_K + N_EXPERTS * (MOE_BM - 1)) // MOE_BM)
    n_rows = n_blocks * MOE_BM
    block_start = jnp.arange(n_blocks, dtype=jnp.int32) * MOE_BM
    block_e = jnp.minimum(jnp.sum(block_start[:, None] >= pends[None, :], axis=1), N_EXPERTS - 1).astype(jnp.int32)
    of_block = block_e[:, None] == experts[:, 0][None, :]
    pick = lambda v: jnp.sum(jnp.where(of_block, v[None, :], 0), axis=1)
    valid = jnp.clip(pick(counts) - (block_start - pick(pstarts)), 0, MOE_BM).astype(jnp.int32)
    xbuf = _sc_dispatch(xpk.reshape(T, D_HALF), dest1, dest2, n_rows)
    ybuf = _moe_blocks(xbuf, block_e, valid, w1, w3, w2, layer)
    y1, y2 = _sc_gather_pair(ybuf, dest1, dest2)
    return y1, y2, gates


def _att_head_order():
    order = []
    for p in range(N_HEADS // 2):
        jj, m = divmod(p, 4)
        order += [8 * jj + m, 8 * jj + 4 + m]
    return order


ATT_HEAD_ORDER = _att_head_order()


def _attn_kernel(xprev_ref, y1_ref, y2_ref, gates_a_ref, gates_b_ref, g_prev_ref, b_prev_ref, wqkv_ref, bias_ref,
                 sink_ref, wo_ref, g_ref, b_ref, wrt_ref, brt_ref, o_ref, opk_ref, lg_ref,
                 kv_ext, o_sc, s_sc0, s_sc1, p_sc0, p_sc1):
    s = pl.program_id(1)
    tq = ATT_TQ
    s_bufs = (s_sc0, s_sc1)
    p_bufs = (p_sc0, p_sc1)
    gate_refs = (gates_a_ref, gates_b_ref)
    assert ATT_NSEQ == len(gate_refs)

    @pl.when(s == 0)
    def _():
        kv_ext[:, 0:WINDOW, :] = jnp.zeros((ATT_NSEQ, WINDOW, 2 * KV_DIM), BF16)

    def layer_input(sq):
        return _moe_combine_norm(xprev_ref[sq], y1_ref[sq], y2_ref[sq], gate_refs[sq][...],
                                 g_prev_ref[...], b_prev_ref[...])

    def project_qkv(sq, x):
        qkv = jnp.dot(x.astype(BF16), wqkv_ref[...], preferred_element_type=F32)
        kv_ext[sq, WINDOW:WINDOW + tq, :] = qkv[:, Q_DIM:].astype(BF16)
        return (qkv[:, :Q_DIM] * (HEAD_DIM ** -0.5 * LOG2E)).astype(BF16)

    lane = lax.broadcasted_iota(jnp.int32, (WINDOW, LANES), 1)
    low = lane < HEAD_DIM
    sub = lax.broadcasted_iota(jnp.int32, (LANES, WINDOW), 0)
    top = sub < HEAD_DIM
    first = jnp.where(s == 0, 1, 0)
    nt = (((1,), (1,)), ((), ()))
    zero = jnp.zeros((), BF16)

    tiles = [(sq, n, j) for n in range(ATT_NB) for j in range(2) for sq in range(ATT_NSEQ)]

    def scores(t, qs_all):
        sq, n, j = tiles[t]
        q = qs_all[sq]
        r0 = n * WINDOW
        k_tile = kv_ext[sq, r0:r0 + 2 * WINDOW, j * LANES:(j + 1) * LANES]
        parts = []
        for m in range(4):
            p = 4 * j + m
            qt = q[r0:r0 + WINDOW, p * LANES:(p + 1) * LANES]
            parts.append(jnp.where(low, qt, zero))
            parts.append(jnp.where(low, zero, qt))
        qs = jnp.concatenate(parts, axis=0)
        bias_sel = first if n == 0 else 0
        s_bufs[t % 2][...] = (lax.dot_general(k_tile, qs, nt, preferred_element_type=F32)
                              + bias_ref[bias_sel, j])

    def softmax_pv(t):
        sq, n, j = tiles[t]
        r0 = n * WINDOW
        s_sc = s_bufs[t % 2]
        p_sc = p_bufs[t % 2]
        inv_l = []
        for h in range(8):
            hc = slice(h * WINDOW, (h + 1) * WINDOW)
            sink = sink_ref[8 * j + h] * LOG2E
            mx = jnp.maximum(jnp.max(s_sc[:, hc], axis=0, keepdims=True), sink)
            pr = jnp.exp2(s_sc[:, hc] - mx)
            p_sc[:, hc] = pr.astype(BF16)
            inv_l.append(1.0 / (jnp.sum(pr, axis=0, keepdims=True) + jnp.exp2(sink - mx)))
        v_tile = kv_ext[sq, r0:r0 + 2 * WINDOW, KV_DIM + j * LANES:KV_DIM + (j + 1) * LANES]
        v_t = v_tile.astype(F32).T.astype(BF16)
        ov = jnp.dot(v_t, p_sc[...], preferred_element_type=F32)
        for m in range(4):
            p = 4 * j + m
            o_even = ov[:, (2 * m) * WINDOW:(2 * m + 1) * WINDOW] * inv_l[2 * m]
            o_odd = ov[:, (2 * m + 1) * WINDOW:(2 * m + 2) * WINDOW] * inv_l[2 * m + 1]
            o_sc[sq, p * LANES:(p + 1) * LANES, r0:r0 + WINDOW] = jnp.where(top, o_even, o_odd).astype(BF16)

    def project_out(sq):
        return jnp.dot(wo_ref[...], o_sc[sq], preferred_element_type=F32)

    def finish(sq, x, out_t):
        xn = _layer_norm_rows(ALPHA * x + out_t.T, g_ref[...], b_ref[...])
        o_ref[sq] = xn
        opk_ref[sq] = _pack_row_halves(xn)
        lg_ref[sq] = _router_logits_t(xn, wrt_ref[...], brt_ref[...])

    xs = [layer_input(0)]
    qs_all = [project_qkv(0, xs[0])]
    xs.append(layer_input(1))
    qs_all.append(project_qkv(1, xs[1]))
    scores(0, qs_all)
    for t in range(len(tiles)):
        if t + 1 < len(tiles):
            scores(t + 1, qs_all)
        softmax_pv(t)
    kv_ext[:, 0:WINDOW, :] = kv_ext[:, tq:tq + WINDOW, :]
    outs_t = [project_out(sq) for sq in range(ATT_NSEQ)]
    for sq in range(ATT_NSEQ):
        finish(sq, xs[sq], outs_t[sq])


def _attn_bias():
    qi = np.arange(WINDOW)[:, None]
    sj = np.arange(2 * WINDOW)[None, :]
    dist = qi - sj + WINDOW
    valid = (dist >= 0) & (dist < WINDOW)
    slopes = 2.0 ** (-8.0 * np.arange(1, N_HEADS + 1, dtype=np.float32) / N_HEADS)
    slopes = slopes.astype(np.float32)[ATT_HEAD_ORDER]
    sb = -(slopes[:, None, None] * dist.astype(np.float32)[None])
    later = np.where(valid[None], sb, -np.inf)
    first = np.where((valid & (sj >= WINDOW))[None], sb, -np.inf)
    bias = np.stack([later, first]).astype(np.float32) * np.float32(LOG2E)
    bias = bias.reshape(2, 2, 8, WINDOW, 2 * WINDOW).transpose(0, 1, 4, 2, 3).reshape(2, 2, 2 * WINDOW, 8 * WINDOW)
    return jnp.asarray(np.ascontiguousarray(bias))


def _attn_layer(x_prev, y1, y2, gates, g_prev, b_prev, w_qkv, sinks, w_o, ln_g, ln_b, router_w, router_b):
    B, S, D = x_prev.shape
    steps = S // ATT_TQ
    assert ATT_HEAD_ORDER == list(np.arange(N_HEADS).reshape(2, 2, 4).transpose(0, 2, 1).reshape(-1))
    wq = w_qkv[:, :Q_DIM].reshape(D, 2, 2, 4, HEAD_DIM).transpose(0, 1, 3, 2, 4).reshape(D, Q_DIM)
    wqkv = jnp.concatenate([wq, w_qkv[:, Q_DIM:]], axis=1).astype(BF16)
    wo_t = w_o.reshape(2, 2, 4, HEAD_DIM, D).transpose(0, 2, 1, 3, 4).reshape(Q_DIM, D).T.astype(BF16)
    sink = sinks.reshape(2, 2, 4).transpose(0, 2, 1).reshape(N_HEADS, 1, 1)
    bias = _attn_bias()
    const = lambda shape: pl.BlockSpec(shape, lambda b, s: (0,) * len(shape))
    tile = lambda w: pl.BlockSpec((ATT_NSEQ, ATT_TQ, w), lambda b, s: (b, s, 0))
    gate_rows = lambda sq: pl.BlockSpec((SUBLANES, ATT_TQ), lambda b, s: (0, (ATT_NSEQ * b + sq) * steps + s))
    return pl.pallas_call(
        _attn_kernel,
        grid=(B // ATT_NSEQ, steps),
        in_specs=[
            tile(D),
            tile(D_HALF),
            tile(D_HALF),
            gate_rows(0),
            gate_rows(1),
            const((1, D)),
            const((1, D)),
            const((D, Q_DIM + 2 * KV_DIM)),
            const((2, 2, 2 * WINDOW, 8 * WINDOW)),
            const((N_HEADS, 1, 1)),
            const((D, Q_DIM)),
            const((1, D)),
            const((1, D)),
            const((2 * ROUTE_ROWS, D)),
            const((ROUTE_ROWS, 1)),
        ],
        out_specs=[tile(D), tile(D_HALF),
                   pl.BlockSpec((ATT_NSEQ, ROUTE_ROWS, ATT_TQ), lambda b, s: (b, 0, s))],
        out_shape=[jax.ShapeDtypeStruct((B, S, D), F32), jax.ShapeDtypeStruct((B, S, D_HALF), U32),
                   jax.ShapeDtypeStruct((B, ROUTE_ROWS, S), F32)],
        scratch_shapes=[
            pltpu.VMEM((ATT_NSEQ, ATT_TQ + WINDOW, 2 * KV_DIM), BF16),
            pltpu.VMEM((ATT_NSEQ, Q_DIM, ATT_TQ), BF16),
            pltpu.VMEM((2 * WINDOW, 8 * WINDOW), F32),
            pltpu.VMEM((2 * WINDOW, 8 * WINDOW), F32),
            pltpu.VMEM((2 * WINDOW, 8 * WINDOW), BF16),
            pltpu.VMEM((2 * WINDOW, 8 * WINDOW), BF16),
        ],
        compiler_params=pltpu.CompilerParams(
            dimension_semantics=("arbitrary", "arbitrary"), vmem_limit_bytes=VMEM_LIMIT),
        name="swa_attn_ln",
    )(x_prev, y1.reshape(B, S, D_HALF), y2.reshape(B, S, D_HALF), gates, gates, g_prev.reshape(1, D),
      b_prev.reshape(1, D),
      wqkv, bias, sink, wo_t, ln_g.reshape(1, D), ln_b.reshape(1, D), router_w, router_b)


def kernel(x, rec_w_in, rec_conv_w, rec_conv_b, rec_w_r, rec_b_r, rec_w_i, rec_b_i, rec_lambda, rec_w_out,
           att_w_qkv, att_sinks, att_w_o, moe_w_group, moe_b_group, moe_w_expert, moe_b_expert,
           moe_w1, moe_w3, moe_w2, ln_g, ln_b):
    assert DEPTH == 2
    B, S, D = x.shape

    router = [_router_weights(moe_w_group[layer], moe_b_group[layer], moe_w_expert[layer], moe_b_expert[layer])
              for layer in range(DEPTH)]

    x1, x1_pk, logits1 = _rglru_layer(x, rec_w_in[0], rec_conv_w[0], rec_conv_b[0], rec_w_r[0], rec_b_r[0],
                                      rec_w_i[0], rec_b_i[0], rec_lambda[0], rec_w_out[0], ln_g[0, 0], ln_b[0, 0],
                                      *router[0])
    y1, y2, gates = _hier_moe(x1_pk, logits1, moe_w1, moe_w3, moe_w2, 0)
    x3, x3_pk, logits3 = _attn_layer(x1, y1, y2, gates, ln_g[0, 1], ln_b[0, 1], att_w_qkv[0], att_sinks[0],
                                     att_w_o[0], ln_g[1, 0], ln_b[1, 0], *router[1])
    y1, y2, gates = _hier_moe(x3_pk, logits3, moe_w1, moe_w3, moe_w2, 1)
    out = _combine_ln(x3.reshape(B * S, D), y1, y2, gates, ln_g[1, 1], ln_b[1, 1])
    return out.reshape(B, S, D)
```

```python
import functools

import jax
import jax.numpy as jnp
import numpy as np
from jax import lax
from jax.experimental import pallas as pl
from jax.experimental.pallas import tpu as pltpu
from jax.experimental.pallas import tpu_sc as plsc

F32 = jnp.float32
BF16 = jnp.bfloat16
U32 = jnp.uint32

D_MODEL = 1024
DEPTH = 2
D_RNN = 1280
LRU_BLOCKS = 16
LRU_BLOCK_W = D_RNN // LRU_BLOCKS
CONV_W = 4
LRU_C = 8.0
N_HEADS = 16
N_KV_HEADS = 4
HEAD_DIM = 64
WINDOW = 128
Q_DIM = N_HEADS * HEAD_DIM
KV_DIM = N_KV_HEADS * HEAD_DIM
N_GROUPS = 4
EXPERTS_PER_GROUP = 8
N_EXPERTS = N_GROUPS * EXPERTS_PER_GROUP
TOP_K = 2
D_EXPERT = 512
ALPHA = (2 * DEPTH) ** 0.25
LN_EPS = 1e-5
LOG2E = 1.4426950408889634

LANES = 128
SUBLANES = 8
VMEM_LIMIT = 56 * 1024 * 1024

REC_TS = 256
REC_GROUPS = REC_TS // SUBLANES
REC_NSEQ = 2
GATE_TILE = 256
GATE_WIN = 512
GATE_WIN_STARTS = (0, 128, 384, 640, 768)
N_GATE_TILES = D_RNN // GATE_TILE

ROUTE_T = 512
ROUTE_STEP = 2048
ROUTE_ROWS = 40

MOE_BM = 1024
MOE_SUB = 256

ATT_TQ = 256
ATT_NB = ATT_TQ // WINDOW
ATT_NSEQ = 2

COMB_T = 1024

D_HALF = D_MODEL // 2

SC_CORES = 2
SC_SUBCORES = 16
SC_WORKERS = SC_CORES * SC_SUBCORES
SC_WIN = 64


def _layer_norm_rows(z, g, b):
    mu = jnp.mean(z, axis=-1, keepdims=True)
    zc = z - mu
    var = jnp.mean(zc * zc, axis=-1, keepdims=True)
    return zc * lax.rsqrt(var + LN_EPS) * g + b


def _pack_bf16_pair(a, b):
    ua = lax.bitcast_convert_type(a.astype(BF16).astype(F32), U32)
    ub = lax.bitcast_convert_type(b.astype(BF16).astype(F32), U32)
    return (ua >> 16) | (ub & jnp.uint32(0xFFFF0000))


def _unpack_bf16_pair(w):
    a = lax.bitcast_convert_type(w << 16, F32)
    b = lax.bitcast_convert_type(w & jnp.uint32(0xFFFF0000), F32)
    return a, b


def _pack_row_halves(x):
    return _pack_bf16_pair(x[:, :D_HALF], x[:, D_HALF:])


def _router_logits_t(x, w_split, bias):
    xhi = x.astype(BF16)
    xlo = (x - xhi.astype(F32)).astype(BF16)
    nt = (((1,), (1,)), ((), ()))
    both = lax.dot_general(w_split, xhi, nt, preferred_element_type=F32)
    low = lax.dot_general(w_split[:ROUTE_ROWS], xlo, nt, preferred_element_type=F32)
    return both[:ROUTE_ROWS] + both[ROUTE_ROWS:] + low + bias


def _rglru_kernel(x_ref, perm_ref, perm_t_ref, w_in_ref, convw_ref, convb_ref, wg_ref, br_ref, bi_ref, lam_ref,
                  w_out_ref, g_ref, b_ref, wrt_ref, brt_ref, o_ref, opk_ref, lg_ref,
                  xr_ext, tail_sc, a_sc, u_sc, h_carry, gate_sc):
    s = pl.program_id(1)
    ts = REC_TS
    halo = (CONV_W - 1) * SUBLANES
    seqs = range(REC_NSEQ)

    @pl.when(s == 0)
    def _():
        tail_sc[...] = jnp.zeros((REC_NSEQ, halo, D_RNN), F32)
        h_carry[...] = jnp.zeros((REC_NSEQ, 1, D_RNN), F32)

    row = lax.broadcasted_iota(jnp.int32, (SUBLANES, D_RNN), 0)
    nlam = -lam_ref[...]
    sp = jnp.maximum(nlam, 0.0) + jnp.log1p(jnp.exp(-jnp.abs(nlam)))
    log2a_scale = (-LRU_C * LOG2E) * sp

    def project(q):
        xp = jnp.dot(perm_ref[...], x_ref[q].astype(BF16), preferred_element_type=F32).astype(BF16)
        proj = jnp.dot(xp, w_in_ref[...], preferred_element_type=F32)
        gate_sc[q] = proj[:, :D_RNN]
        return proj[:, D_RNN:]

    def conv_gates(q, xr):
        for k in range(CONV_W - 1):
            r0 = ts - halo + k * SUBLANES
            cur = xr[r0:r0 + SUBLANES, :]
            prev = tail_sc[q, k * SUBLANES:(k + 1) * SUBLANES, :]
            xr_ext[q, k * SUBLANES:(k + 1) * SUBLANES, :] = jnp.where(
                row == 0, pltpu.roll(prev, 1, axis=0), pltpu.roll(cur, 1, axis=0))
        tail_sc[q] = xr[ts - halo:, :]
        xr_ext[q, halo:halo + ts, :] = xr
        xc = convb_ref[...] + convw_ref[CONV_W - 1:CONV_W, :] * xr
        for k in range(CONV_W - 1):
            xc = xc + convw_ref[k:k + 1, :] * xr_ext[q, k * SUBLANES:k * SUBLANES + ts, :]
        xcb = xc.astype(BF16)
        pres = [jnp.dot(xcb[:, GATE_WIN_STARTS[j]:GATE_WIN_STARTS[j] + GATE_WIN], wg_ref[j],
                        preferred_element_type=F32) for j in range(N_GATE_TILES)]
        for j, pre in enumerate(pres):
            cs = j * GATE_TILE
            r = jax.nn.sigmoid(pre[:, :GATE_TILE] + br_ref[:, cs:cs + GATE_TILE])
            i = jax.nn.sigmoid(pre[:, GATE_TILE:] + bi_ref[:, cs:cs + GATE_TILE])
            a = jnp.exp2(r * log2a_scale[:, cs:cs + GATE_TILE])
            s1 = 1.0 - a * a
            mult = jnp.where(s1 > 0.0, s1 * lax.rsqrt(s1), 0.0)
            u = mult * (i * xc[:, cs:cs + GATE_TILE])
            a_sc[q, :, cs:cs + GATE_TILE] = a
            u_sc[q, :, cs:cs + GATE_TILE] = u

    def segment_scan(q):
        h = jnp.zeros((SUBLANES, D_RNN), F32)
        prod = jnp.ones((SUBLANES, D_RNN), F32)
        for gidx in range(REC_GROUPS):
            rows = slice(gidx * SUBLANES, (gidx + 1) * SUBLANES)
            a8 = a_sc[q, rows, :]
            h = a8 * h + u_sc[q, rows, :]
            prod = a8 * prod
            u_sc[q, rows, :] = h
            a_sc[q, rows, :] = prod
        return h, prod

    def recur_out(q, seg):
        seg_h, seg_a = seg
        for d in (1, 2, 4):
            keep = row >= d
            a_sh = jnp.where(keep, pltpu.roll(seg_a, d, axis=0), 1.0)
            h_sh = jnp.where(keep, pltpu.roll(seg_h, d, axis=0), 0.0)
            seg_h = seg_a * h_sh + seg_h
            seg_a = seg_a * a_sh
        h_in = h_carry[q]
        after = seg_a * h_in + seg_h
        enter = jnp.where(row == 0, h_in, pltpu.roll(after, 1, axis=0))
        h_carry[q] = after[SUBLANES - 1:SUBLANES, :]
        hs = (u_sc[q].reshape(REC_GROUPS, SUBLANES, D_RNN)
              + a_sc[q].reshape(REC_GROUPS, SUBLANES, D_RNN) * enter[None]).reshape(ts, D_RNN)
        y = hs * jax.nn.gelu(gate_sc[q])
        y_t = jnp.dot(perm_t_ref[...], y.astype(BF16), preferred_element_type=F32).astype(BF16)
        return jnp.dot(y_t, w_out_ref[...], preferred_element_type=F32)

    def finish(q, out):
        z = ALPHA * x_ref[q] + out
        xn = _layer_norm_rows(z, g_ref[...], b_ref[...])
        o_ref[q] = xn
        opk_ref[q] = _pack_row_halves(xn)
        lg_ref[q] = _router_logits_t(xn, wrt_ref[...], brt_ref[...])

    assert REC_NSEQ == 2
    xr_a = project(0)
    conv_gates(0, xr_a)
    xr_b = project(1)
    out_a = recur_out(0, segment_scan(0))
    conv_gates(1, xr_b)
    finish(0, out_a)
    out_b = recur_out(1, segment_scan(1))
    finish(1, out_b)


def _band_gate_weights(w_r, w_i):
    spread = jnp.asarray(np.tile(np.eye(LRU_BLOCK_W, dtype=np.float32), (1, LRU_BLOCKS)), BF16)
    blk = np.arange(D_RNN) // LRU_BLOCK_W
    on_diag = jnp.asarray(blk[:, None] == blk[None, :])

    def dense(w):
        rows = w.reshape(D_RNN, LRU_BLOCK_W).astype(BF16)
        return jnp.where(on_diag, jnp.dot(rows, spread, preferred_element_type=F32), 0.0)

    wr, wi = dense(w_r), dense(w_i)
    tiles = []
    for j in range(N_GATE_TILES):
        ws = GATE_WIN_STARTS[j]
        cs = j * GATE_TILE
        lo_blk = cs // LRU_BLOCK_W
        hi_blk = (cs + GATE_TILE - 1) // LRU_BLOCK_W
        assert ws <= lo_blk * LRU_BLOCK_W and (hi_blk + 1) * LRU_BLOCK_W <= ws + GATE_WIN
        tiles.append(jnp.concatenate([wr[ws:ws + GATE_WIN, cs:cs + GATE_TILE],
                                      wi[ws:ws + GATE_WIN, cs:cs + GATE_TILE]], axis=1))
    return jnp.stack(tiles).astype(BF16)


def _rglru_layer(x, w_in, conv_w, conv_b, w_r, b_r, w_i, b_i, lam, w_out, ln_g, ln_b, router_w, router_b):
    B, S, D = x.shape
    wg = _band_gate_weights(w_r, w_i)
    rho = np.arange(REC_TS)
    perm_np = np.zeros((REC_TS, REC_TS), np.float32)
    perm_np[rho, (rho % SUBLANES) * REC_GROUPS + rho // SUBLANES] = 1.0
    perm = jnp.asarray(perm_np, BF16)
    perm_t = jnp.asarray(perm_np.T, BF16)
    row = lambda v: v.reshape(1, -1)
    const = lambda shape: pl.BlockSpec(shape, lambda b, s: (0,) * len(shape))
    tile = lambda w: pl.BlockSpec((REC_NSEQ, REC_TS, w), lambda b, s: (b, s, 0))
    halo = (CONV_W - 1) * SUBLANES
    return pl.pallas_call(
        _rglru_kernel,
        grid=(B // REC_NSEQ, S // REC_TS),
        in_specs=[
            tile(D),
            const((REC_TS, REC_TS)),
            const((REC_TS, REC_TS)),
            const((D, 2 * D_RNN)),
            const((CONV_W, D_RNN)),
            const((1, D_RNN)),
            const((N_GATE_TILES, GATE_WIN, 2 * GATE_TILE)),
            const((1, D_RNN)),
            const((1, D_RNN)),
            const((1, D_RNN)),
            const((D_RNN, D)),
            const((1, D)),
            const((1, D)),
            const((2 * ROUTE_ROWS, D)),
            const((ROUTE_ROWS, 1)),
        ],
        out_specs=[tile(D), tile(D_HALF),
                   pl.BlockSpec((REC_NSEQ, ROUTE_ROWS, REC_TS), lambda b, s: (b, 0, s))],
        out_shape=[jax.ShapeDtypeStruct((B, S, D), F32), jax.ShapeDtypeStruct((B, S, D_HALF), U32),
                   jax.ShapeDtypeStruct((B, ROUTE_ROWS, S), F32)],
        scratch_shapes=[
            pltpu.VMEM((REC_NSEQ, halo + REC_TS, D_RNN), F32),
            pltpu.VMEM((REC_NSEQ, halo, D_RNN), F32),
            pltpu.VMEM((REC_NSEQ, REC_TS, D_RNN), F32),
            pltpu.VMEM((REC_NSEQ, REC_TS, D_RNN), F32),
            pltpu.VMEM((REC_NSEQ, 1, D_RNN), F32),
            pltpu.VMEM((REC_NSEQ, REC_TS, D_RNN), F32),
        ],
        compiler_params=pltpu.CompilerParams(
            dimension_semantics=("arbitrary", "arbitrary"), vmem_limit_bytes=VMEM_LIMIT),
        name="rglru_ln",
    )(x, perm, perm_t, w_in.astype(BF16), conv_w, row(conv_b), wg, row(b_r), row(b_i), row(lam), w_out.astype(BF16),
      row(ln_g), row(ln_b), router_w, router_b)


def _router_kernel(logits_ref, tri_ref, idx_ref, gate_ref, cnt_ref, base_sc):
    step = pl.program_id(0)
    tr = ROUTE_T

    @pl.when(step == 0)
    def _():
        base_sc[...] = jnp.zeros((N_EXPERTS, 1), F32)

    row8 = lax.broadcasted_iota(jnp.int32, (SUBLANES, tr), 0).astype(F32)
    rowe = lax.broadcasted_iota(jnp.int32, (N_EXPERTS, tr), 0).astype(F32)
    neg_inf = -jnp.inf
    for sub in range(ROUTE_STEP // ROUTE_T):
        cols = slice(sub * tr, (sub + 1) * tr)
        _route_tile(logits_ref[0, :, cols], row8, rowe, neg_inf, tri_ref, idx_ref, gate_ref, base_sc, cols)
    cnt_ref[...] = jnp.broadcast_to(base_sc[...], (N_EXPERTS, LANES)).astype(jnp.int32)


def _route_tile(logits, row8, rowe, neg_inf, tri_ref, idx_ref, gate_ref, base_sc, cols):
    tr = ROUTE_T
    g = jnp.where(row8 < N_GROUPS, logits[N_EXPERTS:N_EXPERTS + SUBLANES, :], neg_inf)
    gmax = jnp.max(g, axis=0, keepdims=True)
    gidx = jnp.min(jnp.where(g == gmax, row8, SUBLANES), axis=0, keepdims=True)
    g_gate = 1.0 / jnp.sum(jnp.exp(g - gmax), axis=0, keepdims=True)

    esel = logits[0:EXPERTS_PER_GROUP, :]
    for grp in range(1, N_GROUPS):
        esel = jnp.where(gidx == grp, logits[grp * EXPERTS_PER_GROUP:(grp + 1) * EXPERTS_PER_GROUP, :], esel)
    v1 = jnp.max(esel, axis=0, keepdims=True)
    i1 = jnp.min(jnp.where(esel == v1, row8, SUBLANES), axis=0, keepdims=True)
    esel2 = jnp.where(row8 == i1, neg_inf, esel)
    v2 = jnp.max(esel2, axis=0, keepdims=True)
    i2 = jnp.min(jnp.where(esel2 == v2, row8, SUBLANES), axis=0, keepdims=True)
    e21 = jnp.exp(v2 - v1)
    inv = 1.0 / (1.0 + e21)
    gate1 = inv * g_gate
    gate2 = e21 * inv * g_gate
    e1 = gidx * EXPERTS_PER_GROUP + i1
    e2 = gidx * EXPERTS_PER_GROUP + i2

    hit1 = rowe == e1
    hit2 = rowe == e2
    member = jnp.where(hit1, 1.0, jnp.where(hit2, 1.0, 0.0))
    before = jnp.dot(member.astype(BF16), tri_ref[...], preferred_element_type=F32) + base_sc[...]
    rank1 = jnp.sum(jnp.where(hit1, before, 0.0), axis=0, keepdims=True)
    rank2 = jnp.sum(jnp.where(hit2, before, 0.0), axis=0, keepdims=True)
    base_sc[...] = base_sc[...] + jnp.sum(member, axis=1, keepdims=True)

    zi = jnp.zeros((1, tr), jnp.int32)
    idx_ref[:, cols] = jnp.concatenate(
        [e1.astype(jnp.int32), e2.astype(jnp.int32), rank1.astype(jnp.int32), rank2.astype(jnp.int32),
         zi, zi, zi, zi], axis=0)
    zf = jnp.zeros((1, tr), F32)
    gate_ref[:, cols] = jnp.concatenate([gate1, gate2, zf, zf, zf, zf, zf, zf], axis=0)


def _router_weights(w_rg, b_rg, w_re, b_re):
    D = w_rg.shape[0]
    pad_rows = ROUTE_ROWS - N_EXPERTS - N_GROUPS
    w = jnp.concatenate([w_re.T, w_rg.T, jnp.zeros((pad_rows, D), F32)], axis=0)
    whi = w.astype(BF16)
    wlo = (w - whi.astype(F32)).astype(BF16)
    w_split = jnp.concatenate([whi, wlo], axis=0)
    bias = jnp.concatenate([b_re, b_rg, jnp.zeros((pad_rows,), F32)]).reshape(ROUTE_ROWS, 1)
    return w_split, bias


def _router(logits_t):
    B, _, S = logits_t.shape
    T = B * S
    per_row = S // ROUTE_STEP
    tri = jnp.asarray(np.triu(np.ones((ROUTE_T, ROUTE_T), np.float32), 1), BF16)
    const = lambda shape: pl.BlockSpec(shape, lambda i: (0,) * len(shape))
    return pl.pallas_call(
        _router_kernel,
        grid=(T // ROUTE_STEP,),
        in_specs=[
            pl.BlockSpec((1, ROUTE_ROWS, ROUTE_STEP), lambda i: (i // per_row, 0, i % per_row)),
            const((ROUTE_T, ROUTE_T)),
        ],
        out_specs=[
            pl.BlockSpec((SUBLANES, ROUTE_STEP), lambda i: (0, i)),
            pl.BlockSpec((SUBLANES, ROUTE_STEP), lambda i: (0, i)),
            const((N_EXPERTS, LANES)),
        ],
        out_shape=[
            jax.ShapeDtypeStruct((SUBLANES, T), jnp.int32),
            jax.ShapeDtypeStruct((SUBLANES, T), F32),
            jax.ShapeDtypeStruct((N_EXPERTS, LANES), jnp.int32),
        ],
        scratch_shapes=[pltpu.VMEM((N_EXPERTS, 1), F32)],
        compiler_params=pltpu.CompilerParams(
            dimension_semantics=("arbitrary",), vmem_limit_bytes=VMEM_LIMIT),
        name="router",
    )(logits_t, tri)


def _sc_mesh():
    return plsc.VectorSubcoreMesh(core_axis_name="c", subcore_axis_name="s",
                                  num_cores=SC_CORES, num_subcores=SC_SUBCORES)


def _sc_worker_id():
    return lax.axis_index("s") * SC_CORES + lax.axis_index("c")


def _sc_scratch(n_win, width):
    return [
        pltpu.VMEM((n_win, SC_WIN), jnp.int32),
        pltpu.VMEM((n_win, SC_WIN), jnp.int32),
        pltpu.VMEM((2, SC_WIN, width), U32),
        pltpu.SemaphoreType.DMA((2,)),
        pltpu.SemaphoreType.DMA((2,)),
    ]


def _sc_dispatch(rows, idx1, idx2, n_rows):
    _, width = rows.shape
    _, n_win, _ = idx1.shape

    @functools.partial(
        pl.kernel, mesh=_sc_mesh(), out_type=jax.ShapeDtypeStruct((n_rows, width), rows.dtype),
        scratch_types=_sc_scratch(n_win, width), name="sc_dispatch")
    def run(rows_hbm, i1_hbm, i2_hbm, o_hbm, i1_v, i2_v, buf, rsem, wsem):
        wid = _sc_worker_id()
        base = wid * (n_win * SC_WIN)
        pltpu.sync_copy(i1_hbm.at[wid], i1_v)
        pltpu.sync_copy(i2_hbm.at[wid], i2_v)

        def read(j):
            return pltpu.async_copy(rows_hbm.at[pl.ds(base + j * SC_WIN, SC_WIN)], buf.at[j % 2], rsem.at[j % 2])

        reads = {0: read(0)}
        writes = {}
        for j in range(n_win):
            if j + 1 < n_win:
                for d in writes.pop(j - 1, ()):
                    d.wait()
                reads[j + 1] = read(j + 1)
            reads.pop(j).wait()
            writes[j] = (pltpu.async_copy(buf.at[j % 2], o_hbm.at[i1_v.at[j]], wsem.at[j % 2]),
                         pltpu.async_copy(buf.at[j % 2], o_hbm.at[i2_v.at[j]], wsem.at[j % 2]))
        for j in sorted(writes):
            for d in writes[j]:
                d.wait()

    return run(rows, idx1, idx2)


def _sc_gather_pair(table, idx1, idx2):
    _, width = table.shape
    _, n_win, _ = idx1.shape
    n_tok = SC_WORKERS * n_win * SC_WIN
    out_t = jax.ShapeDtypeStruct((n_tok, width), table.dtype)

    @functools.partial(
        pl.kernel, mesh=_sc_mesh(), out_type=(out_t, out_t),
        scratch_types=_sc_scratch(n_win, width), name="sc_combine_gather")
    def run(table_hbm, i1_hbm, i2_hbm, o1_hbm, o2_hbm, i1_v, i2_v, buf, gsem, wsem):
        wid = _sc_worker_id()
        base = wid * (n_win * SC_WIN)
        pltpu.sync_copy(i1_hbm.at[wid], i1_v)
        pltpu.sync_copy(i2_hbm.at[wid], i2_v)
        work = [(i1_v, o1_hbm, j) for j in range(n_win)] + [(i2_v, o2_hbm, j) for j in range(n_win)]

        def gather(t):
            iv, _, j = work[t]
            return pltpu.async_copy(table_hbm.at[iv.at[j]], buf.at[t % 2], gsem.at[t % 2])

        def put(t):
            _, oh, j = work[t]
            return pltpu.async_copy(buf.at[t % 2], oh.at[pl.ds(base + j * SC_WIN, SC_WIN)], wsem.at[t % 2])

        gathers = {0: gather(0)}
        puts = {}
        for t in range(len(work)):
            if t + 1 < len(work):
                if t - 1 in puts:
                    puts.pop(t - 1).wait()
                gathers[t + 1] = gather(t + 1)
            gathers.pop(t).wait()
            puts[t] = put(t)
        for t in sorted(puts):
            puts[t].wait()

    return run(table, idx1, idx2)


def _moe_kernel(layer, be_ref, slot_ref, nxt_ref, valid_ref, x_ref, w1_hbm, w3_hbm, w2_hbm, o_ref,
                w1_st, w3_st, w2_st, w1_sc, w3_sc, w2_sc, sems):
    i = pl.program_id(0)
    expert = be_ref[i]
    slot = slot_ref[i]
    new_expert = jnp.logical_or(i == 0, expert != be_ref[jnp.maximum(i - 1, 0)])

    def weight_copies(e, sl):
        return [pltpu.make_async_copy(hbm.at[layer, e], stage.at[sl], sems.at[k, sl])
                for k, (hbm, stage) in enumerate(((w1_hbm, w1_st), (w3_hbm, w3_st), (w2_hbm, w2_st)))]

    @pl.when(i == 0)
    def _():
        for cp in weight_copies(expert, slot):
            cp.start()

    @pl.when(new_expert)
    def _():
        for cp in weight_copies(expert, slot):
            cp.wait()
        w1_sc[...] = w1_st[slot].astype(BF16)
        w3_sc[...] = w3_st[slot].astype(BF16)
        w2_sc[...] = w2_st[slot].astype(BF16)
        nxt = nxt_ref[i]

        @pl.when(nxt >= 0)
        def _():
            for cp in weight_copies(nxt, 1 - slot):
                cp.start()

    def up(rows):
        xa, xb = _unpack_bf16_pair(x_ref[rows, :])
        xa = xa.astype(BF16)
        xb = xb.astype(BF16)
        h1 = (jnp.dot(xa, w1_sc[:D_HALF], preferred_element_type=F32)
              + jnp.dot(xb, w1_sc[D_HALF:], preferred_element_type=F32))
        h3 = (jnp.dot(xa, w3_sc[:D_HALF], preferred_element_type=F32)
              + jnp.dot(xb, w3_sc[D_HALF:], preferred_element_type=F32))
        return h1, h3

    def down(rows, h1, h3):
        hdn = (jax.nn.silu(h1) * h3).astype(BF16)
        y = jnp.dot(hdn, w2_sc[...], preferred_element_type=F32)
        o_ref[rows, :] = _pack_row_halves(y)

    n_sub = MOE_BM // MOE_SUB
    subs = [slice(k * MOE_SUB, (k + 1) * MOE_SUB) for k in range(n_sub)]
    valid = valid_ref[i]
    chains = (valid + (MOE_SUB - 1)) // MOE_SUB

    for live in range(n_sub + 1):
        @pl.when(chains == live)
        def _(live=live):
            ups = {}
            if live:
                ups[0] = up(subs[0])
            for k in range(live):
                if k + 1 < live:
                    ups[k + 1] = up(subs[k + 1])
                down(subs[k], *ups.pop(k))
            if live < n_sub:
                o_ref[live * MOE_SUB:, :] = jnp.zeros((MOE_BM - live * MOE_SUB, D_HALF), o_ref.dtype)


def _moe_blocks(xbuf, block_e, valid, w1, w3, w2, layer):
    n_rows, _ = xbuf.shape
    D = D_MODEL
    n_blocks = n_rows // MOE_BM
    pos = jnp.arange(n_blocks, dtype=jnp.int32)
    is_new = jnp.concatenate([jnp.ones((1,), bool), block_e[1:] != block_e[:-1]])
    slot = ((jnp.cumsum(is_new.astype(jnp.int32)) - 1) % 2).astype(jnp.int32)
    change_pos = jnp.where(is_new, pos, n_blocks)
    next_change = jnp.concatenate([lax.cummin(change_pos, reverse=True)[1:], jnp.full((1,), n_blocks, jnp.int32)])
    nxt = jnp.where(next_change < n_blocks, block_e[jnp.minimum(next_change, n_blocks - 1)], -1).astype(jnp.int32)
    rows = lambda i, be, sl, nx, nu: (i, 0)
    grid_spec = pltpu.PrefetchScalarGridSpec(
        num_scalar_prefetch=4,
        grid=(n_blocks,),
        in_specs=[
            pl.BlockSpec((MOE_BM, D_HALF), rows),
            pl.BlockSpec(memory_space=pl.ANY),
            pl.BlockSpec(memory_space=pl.ANY),
            pl.BlockSpec(memory_space=pl.ANY),
        ],
        out_specs=pl.BlockSpec((MOE_BM, D_HALF), rows),
        scratch_shapes=[
            pltpu.VMEM((2, D, D_EXPERT), F32),
            pltpu.VMEM((2, D, D_EXPERT), F32),
            pltpu.VMEM((2, D_EXPERT, D), F32),
            pltpu.VMEM((D, D_EXPERT), BF16),
            pltpu.VMEM((D, D_EXPERT), BF16),
            pltpu.VMEM((D_EXPERT, D), BF16),
            pltpu.SemaphoreType.DMA((3, 2)),
        ],
    )
    return pl.pallas_call(
        functools.partial(_moe_kernel, layer),
        grid_spec=grid_spec,
        out_shape=jax.ShapeDtypeStruct((n_rows, D_HALF), U32),
        compiler_params=pltpu.CompilerParams(
            dimension_semantics=("arbitrary",), vmem_limit_bytes=VMEM_LIMIT),
        name="moe_experts",
    )(block_e, slot, nxt, valid, xbuf, w1, w3, w2)


def _moe_combine_norm(x, y1, y2, gate_rows, g, b):
    n = x.shape[0]
    pad = jnp.zeros((LANES - SUBLANES, n), F32)
    gates = jnp.concatenate([gate_rows, pad], axis=0).T
    g1 = gates[:, 0:1]
    g2 = gates[:, 1:2]
    a1, b1 = _unpack_bf16_pair(y1)
    a2, b2 = _unpack_bf16_pair(y2)
    f = jnp.concatenate([g1 * a1 + g2 * a2, g1 * b1 + g2 * b2], axis=1)
    return _layer_norm_rows(ALPHA * x + f, g, b)


def _combine_kernel(x_ref, y1_ref, y2_ref, gates_ref, g_ref, b_ref, o_ref):
    o_ref[...] = _moe_combine_norm(x_ref[...], y1_ref[...], y2_ref[...], gates_ref[...], g_ref[...], b_ref[...])


def _combine_part_kernel(x_ref, y1_ref, y2_ref, gates_ref, g_ref, b_ref, prev_ref, o_ref):
    del prev_ref
    _combine_kernel(x_ref, y1_ref, y2_ref, gates_ref, g_ref, b_ref, o_ref)


def _combine_ln(xf, y_parts, gates, ln_g, ln_b):
    T, D = xf.shape
    n_parts = len(y_parts)
    steps = T // (COMB_T * n_parts)
    const = lambda shape: pl.BlockSpec(shape, lambda i: (0,) * len(shape))
    out = None
    for part, (y1, y2) in enumerate(y_parts):
        off = part * steps
        glob = lambda w, off=off: pl.BlockSpec((COMB_T, w), lambda i: (i + off, 0))
        local = lambda w: pl.BlockSpec((COMB_T, w), lambda i: (i, 0))
        in_specs = [glob(D), local(D_HALF), local(D_HALF),
                    pl.BlockSpec((SUBLANES, COMB_T), lambda i, off=off: (0, i + off)), const((1, D)), const((1, D))]
        args = [xf, y1, y2, gates, ln_g.reshape(1, D), ln_b.reshape(1, D)]
        if out is not None:
            in_specs.append(pl.BlockSpec(memory_space=pl.ANY))
            args.append(out)
        out = pl.pallas_call(
            _combine_kernel if out is None else _combine_part_kernel,
            grid=(steps,),
            in_specs=in_specs,
            out_specs=glob(D),
            out_shape=jax.ShapeDtypeStruct((T, D), F32),
            input_output_aliases={} if out is None else {len(args) - 1: 0},
            compiler_params=pltpu.CompilerParams(
                dimension_semantics=("arbitrary",), vmem_limit_bytes=VMEM_LIMIT),
            name="moe_combine_ln",
        )(*args)
    return out


def _hier_moe(xpk, logits_t, w1, w3, w2, layer, gather_parts):
    B, S, _ = xpk.shape
    D = D_MODEL
    T = B * S
    idx, gates, cnt = _router(logits_t)
    counts = cnt[:, 0]
    padded = ((counts + MOE_BM - 1) // MOE_BM) * MOE_BM
    pends = jnp.cumsum(padded)
    pstarts = pends - padded
    experts = jnp.arange(N_EXPERTS, dtype=jnp.int32)[:, None]

    def dest(e_row, rank_row):
        return jnp.sum(jnp.where(e_row[None, :] == experts, pstarts[:, None], 0), axis=0) + rank_row

    n_win = T // (SC_WORKERS * SC_WIN)
    dest1 = dest(idx[0], idx[2]).reshape(SC_WORKERS, n_win, SC_WIN)
    dest2 = dest(idx[1], idx[3]).reshape(SC_WORKERS, n_win, SC_WIN)
    n_blocks = -(-(T * TOP_K + N_EXPERTS * (MOE_BM - 1)) // MOE_BM)
    n_rows = n_blocks * MOE_BM
    block_start = jnp.arange(n_blocks, dtype=jnp.int32) * MOE_BM
    block_e = jnp.minimum(jnp.sum(block_start[:, None] >= pends[None, :], axis=1), N_EXPERTS - 1).astype(jnp.int32)
    of_block = block_e[:, None] == experts[:, 0][None, :]
    pick = lambda v: jnp.sum(jnp.where(of_block, v[None, :], 0), axis=1)
    valid = jnp.clip(pick(counts) - (block_start - pick(pstarts)), 0, MOE_BM).astype(jnp.int32)
    xbuf = _sc_dispatch(xpk.reshape(T, D_HALF), dest1, dest2, n_rows)
    ybuf = _moe_blocks(xbuf, block_e, valid, w1, w3, w2, layer)
    part_shape = (gather_parts, SC_WORKERS, n_win // gather_parts, SC_WIN)
    d1, d2 = dest1.reshape(part_shape), dest2.reshape(part_shape)
    y_parts = [_sc_gather_pair(ybuf, d1[p], d2[p]) for p in range(gather_parts)]
    return y_parts, gates


def _att_head_order():
    order = []
    for p in range(N_HEADS // 2):
        jj, m = divmod(p, 4)
        order += [8 * jj + m, 8 * jj + 4 + m]
    return order


ATT_HEAD_ORDER = _att_head_order()


def _attn_kernel(xprev_ref, y1_ref, y2_ref, gates_a_ref, gates_b_ref, g_prev_ref, b_prev_ref, wqkv_ref, bias_ref,
                 sink_ref, wo_ref, g_ref, b_ref, wrt_ref, brt_ref, o_ref, opk_ref, lg_ref,
                 kv_ext, o_sc, s_sc0, s_sc1, p_sc0, p_sc1):
    s = pl.program_id(1)
    tq = ATT_TQ
    s_bufs = (s_sc0, s_sc1)
    p_bufs = (p_sc0, p_sc1)
    gate_refs = (gates_a_ref, gates_b_ref)
    assert ATT_NSEQ == len(gate_refs)

    @pl.when(s == 0)
    def _():
        kv_ext[:, 0:WINDOW, :] = jnp.zeros((ATT_NSEQ, WINDOW, 2 * KV_DIM), BF16)

    def layer_input(sq):
        return _moe_combine_norm(xprev_ref[sq], y1_ref[sq], y2_ref[sq], gate_refs[sq][...],
                                 g_prev_ref[...], b_prev_ref[...])

    def project_qkv(sq, x):
        qkv = jnp.dot(x.astype(BF16), wqkv_ref[...], preferred_element_type=F32)
        kv_ext[sq, WINDOW:WINDOW + tq, :] = qkv[:, Q_DIM:].astype(BF16)
        return (qkv[:, :Q_DIM] * (HEAD_DIM ** -0.5 * LOG2E)).astype(BF16)

    lane = lax.broadcasted_iota(jnp.int32, (WINDOW, LANES), 1)
    low = lane < HEAD_DIM
    sub = lax.broadcasted_iota(jnp.int32, (LANES, WINDOW), 0)
    top = sub < HEAD_DIM
    first = jnp.where(s == 0, 1, 0)
    nt = (((1,), (1,)), ((), ()))
    zero = jnp.zeros((), BF16)

    tiles = [(sq, n, j) for n in range(ATT_NB) for j in range(2) for sq in range(ATT_NSEQ)]

    def scores(t, qs_all):
        sq, n, j = tiles[t]
        q = qs_all[sq]
        r0 = n * WINDOW
        k_tile = kv_ext[sq, r0:r0 + 2 * WINDOW, j * LANES:(j + 1) * LANES]
        parts = []
        for m in range(4):
            p = 4 * j + m
            qt = q[r0:r0 + WINDOW, p * LANES:(p + 1) * LANES]
            parts.append(jnp.where(low, qt, zero))
            parts.append(jnp.where(low, zero, qt))
        qs = jnp.concatenate(parts, axis=0)
        bias_sel = first if n == 0 else 0
        s_bufs[t % 2][...] = (lax.dot_general(k_tile, qs, nt, preferred_element_type=F32)
                              + bias_ref[bias_sel, j])

    def softmax_pv(t):
        sq, n, j = tiles[t]
        r0 = n * WINDOW
        s_sc = s_bufs[t % 2]
        p_sc = p_bufs[t % 2]
        inv_l = []
        for h in range(8):
            hc = slice(h * WINDOW, (h + 1) * WINDOW)
            sink = sink_ref[8 * j + h] * LOG2E
            mx = jnp.maximum(jnp.max(s_sc[:, hc], axis=0, keepdims=True), sink)
            pr = jnp.exp2(s_sc[:, hc] - mx)
            p_sc[:, hc] = pr.astype(BF16)
            inv_l.append(1.0 / (jnp.sum(pr, axis=0, keepdims=True) + jnp.exp2(sink - mx)))
        v_tile = kv_ext[sq, r0:r0 + 2 * WINDOW, KV_DIM + j * LANES:KV_DIM + (j + 1) * LANES]
        v_t = v_tile.astype(F32).T.astype(BF16)
        ov = jnp.dot(v_t, p_sc[...], preferred_element_type=F32)
        for m in range(4):
            p = 4 * j + m
            o_even = ov[:, (2 * m) * WINDOW:(2 * m + 1) * WINDOW] * inv_l[2 * m]
            o_odd = ov[:, (2 * m + 1) * WINDOW:(2 * m + 2) * WINDOW] * inv_l[2 * m + 1]
            o_sc[sq, p * LANES:(p + 1) * LANES, r0:r0 + WINDOW] = jnp.where(top, o_even, o_odd).astype(BF16)

    def project_out(sq):
        return jnp.dot(wo_ref[...], o_sc[sq], preferred_element_type=F32)

    def finish(sq, x, out_t):
        xn = _layer_norm_rows(ALPHA * x + out_t.T, g_ref[...], b_ref[...])
        o_ref[sq] = xn
        opk_ref[sq] = _pack_row_halves(xn)
        lg_ref[sq] = _router_logits_t(xn, wrt_ref[...], brt_ref[...])

    xs = [layer_input(0)]
    qs_all = [project_qkv(0, xs[0])]
    xs.append(layer_input(1))
    qs_all.append(project_qkv(1, xs[1]))
    scores(0, qs_all)
    for t in range(len(tiles)):
        if t + 1 < len(tiles):
            scores(t + 1, qs_all)
        softmax_pv(t)
    kv_ext[:, 0:WINDOW, :] = kv_ext[:, tq:tq + WINDOW, :]
    outs_t = [project_out(sq) for sq in range(ATT_NSEQ)]
    for sq in range(ATT_NSEQ):
        finish(sq, xs[sq], outs_t[sq])


def _attn_bias():
    qi = np.arange(WINDOW)[:, None]
    sj = np.arange(2 * WINDOW)[None, :]
    dist = qi - sj + WINDOW
    valid = (dist >= 0) & (dist < WINDOW)
    slopes = 2.0 ** (-8.0 * np.arange(1, N_HEADS + 1, dtype=np.float32) / N_HEADS)
    slopes = slopes.astype(np.float32)[ATT_HEAD_ORDER]
    sb = -(slopes[:, None, None] * dist.astype(np.float32)[None])
    later = np.where(valid[None], sb, -np.inf)
    first = np.where((valid & (sj >= WINDOW))[None], sb, -np.inf)
    bias = np.stack([later, first]).astype(np.float32) * np.float32(LOG2E)
    bias = bias.reshape(2, 2, 8, WINDOW, 2 * WINDOW).transpose(0, 1, 4, 2, 3).reshape(2, 2, 2 * WINDOW, 8 * WINDOW)
    return jnp.asarray(np.ascontiguousarray(bias))


def _attn_layer(x_prev, y1, y2, gates, g_prev, b_prev, w_qkv, sinks, w_o, ln_g, ln_b, router_w, router_b):
    B, S, D = x_prev.shape
    steps = S // ATT_TQ
    assert ATT_HEAD_ORDER == list(np.arange(N_HEADS).reshape(2, 2, 4).transpose(0, 2, 1).reshape(-1))
    wq = w_qkv[:, :Q_DIM].reshape(D, 2, 2, 4, HEAD_DIM).transpose(0, 1, 3, 2, 4).reshape(D, Q_DIM)
    wqkv = jnp.concatenate([wq, w_qkv[:, Q_DIM:]], axis=1).astype(BF16)
    wo_t = w_o.reshape(2, 2, 4, HEAD_DIM, D).transpose(0, 2, 1, 3, 4).reshape(Q_DIM, D).T.astype(BF16)
    sink = sinks.reshape(2, 2, 4).transpose(0, 2, 1).reshape(N_HEADS, 1, 1)
    bias = _attn_bias()
    const = lambda shape: pl.BlockSpec(shape, lambda b, s: (0,) * len(shape))
    tile = lambda w: pl.BlockSpec((ATT_NSEQ, ATT_TQ, w), lambda b, s: (b, s, 0))
    gate_rows = lambda sq: pl.BlockSpec((SUBLANES, ATT_TQ), lambda b, s: (0, (ATT_NSEQ * b + sq) * steps + s))
    return pl.pallas_call(
        _attn_kernel,
        grid=(B // ATT_NSEQ, steps),
        in_specs=[
            tile(D),
            tile(D_HALF),
            tile(D_HALF),
            gate_rows(0),
            gate_rows(1),
            const((1, D)),
            const((1, D)),
            const((D, Q_DIM + 2 * KV_DIM)),
            const((2, 2, 2 * WINDOW, 8 * WINDOW)),
            const((N_HEADS, 1, 1)),
            const((D, Q_DIM)),
            const((1, D)),
            const((1, D)),
            const((2 * ROUTE_ROWS, D)),
            const((ROUTE_ROWS, 1)),
        ],
        out_specs=[tile(D), tile(D_HALF),
                   pl.BlockSpec((ATT_NSEQ, ROUTE_ROWS, ATT_TQ), lambda b, s: (b, 0, s))],
        out_shape=[jax.ShapeDtypeStruct((B, S, D), F32), jax.ShapeDtypeStruct((B, S, D_HALF), U32),
                   jax.ShapeDtypeStruct((B, ROUTE_ROWS, S), F32)],
        scratch_shapes=[
            pltpu.VMEM((ATT_NSEQ, ATT_TQ + WINDOW, 2 * KV_DIM), BF16),
            pltpu.VMEM((ATT_NSEQ, Q_DIM, ATT_TQ), BF16),
            pltpu.VMEM((2 * WINDOW, 8 * WINDOW), F32),
            pltpu.VMEM((2 * WINDOW, 8 * WINDOW), F32),
            pltpu.VMEM((2 * WINDOW, 8 * WINDOW), BF16),
            pltpu.VMEM((2 * WINDOW, 8 * WINDOW), BF16),
        ],
        compiler_params=pltpu.CompilerParams(
            dimension_semantics=("arbitrary", "arbitrary"), vmem_limit_bytes=VMEM_LIMIT),
        name="swa_attn_ln",
    )(x_prev, y1.reshape(B, S, D_HALF), y2.reshape(B, S, D_HALF), gates, gates, g_prev.reshape(1, D),
      b_prev.reshape(1, D),
      wqkv, bias, sink, wo_t, ln_g.reshape(1, D), ln_b.reshape(1, D), router_w, router_b)


def kernel(x, rec_w_in, rec_conv_w, rec_conv_b, rec_w_r, rec_b_r, rec_w_i, rec_b_i, rec_lambda, rec_w_out,
           att_w_qkv, att_sinks, att_w_o, moe_w_group, moe_b_group, moe_w_expert, moe_b_expert,
           moe_w1, moe_w3, moe_w2, ln_g, ln_b):
    assert DEPTH == 2
    B, S, D = x.shape

    router = [_router_weights(moe_w_group[layer], moe_b_group[layer], moe_w_expert[layer], moe_b_expert[layer])
              for layer in range(DEPTH)]

    x1, x1_pk, logits1 = _rglru_layer(x, rec_w_in[0], rec_conv_w[0], rec_conv_b[0], rec_w_r[0], rec_b_r[0],
                                      rec_w_i[0], rec_b_i[0], rec_lambda[0], rec_w_out[0], ln_g[0, 0], ln_b[0, 0],
                                      *router[0])
    ((y1, y2),), gates = _hier_moe(x1_pk, logits1, moe_w1, moe_w3, moe_w2, 0, gather_parts=1)
    x3, x3_pk, logits3 = _attn_layer(x1, y1, y2, gates, ln_g[0, 1], ln_b[0, 1], att_w_qkv[0], att_sinks[0],
                                     att_w_o[0], ln_g[1, 0], ln_b[1, 0], *router[1])
    y_parts, gates = _hier_moe(x3_pk, logits3, moe_w1, moe_w3, moe_w2, 1, gather_parts=2)
    out = _combine_ln(x3.reshape(B * S, D), y_parts, gates, ln_g[1, 1], ln_b[1, 1])
    return out.reshape(B, S, D)
```

```python
import functools

import jax
import jax.numpy as jnp
import numpy as np
from jax import lax
from jax.experimental import pallas as pl
from jax.experimental.pallas import tpu as pltpu
from jax.experimental.pallas import tpu_sc as plsc

F32 = jnp.float32
BF16 = jnp.bfloat16
U32 = jnp.uint32

D_MODEL = 1024
DEPTH = 2
D_RNN = 1280
LRU_BLOCKS = 16
LRU_BLOCK_W = D_RNN // LRU_BLOCKS
CONV_W = 4
LRU_C = 8.0
N_HEADS = 16
N_KV_HEADS = 4
HEAD_DIM = 64
WINDOW = 128
Q_DIM = N_HEADS * HEAD_DIM
KV_DIM = N_KV_HEADS * HEAD_DIM
N_GROUPS = 4
EXPERTS_PER_GROUP = 8
N_EXPERTS = N_GROUPS * EXPERTS_PER_GROUP
TOP_K = 2
D_EXPERT = 512
ALPHA = (2 * DEPTH) ** 0.25
LN_EPS = 1e-5
LOG2E = 1.4426950408889634

LANES = 128
SUBLANES = 8
VMEM_LIMIT = 56 * 1024 * 1024

REC_TS = 256
REC_GROUPS = REC_TS // SUBLANES
REC_NSEQ = 2
GATE_TILE = 256
GATE_WIN = 512
GATE_WIN_STARTS = (0, 128, 384, 640, 768)
N_GATE_TILES = D_RNN // GATE_TILE

ROUTE_T = 512
ROUTE_STEP = 2048
ROUTE_ROWS = 40

MOE_BM = 1024
MOE_SUB = 256

ATT_TQ = 256
ATT_NB = ATT_TQ // WINDOW
ATT_NSEQ = 2
ATT_PARTS = 4

COMB_T = 1024

D_HALF = D_MODEL // 2

SC_CORES = 2
SC_SUBCORES = 16
SC_WORKERS = SC_CORES * SC_SUBCORES
SC_WIN = 64


def _layer_norm_rows(z, g, b):
    mu = jnp.mean(z, axis=-1, keepdims=True)
    zc = z - mu
    var = jnp.mean(zc * zc, axis=-1, keepdims=True)
    return zc * lax.rsqrt(var + LN_EPS) * g + b


def _pack_bf16_pair(a, b):
    ua = lax.bitcast_convert_type(a.astype(BF16).astype(F32), U32)
    ub = lax.bitcast_convert_type(b.astype(BF16).astype(F32), U32)
    return (ua >> 16) | (ub & jnp.uint32(0xFFFF0000))


def _unpack_bf16_pair(w):
    a = lax.bitcast_convert_type(w << 16, F32)
    b = lax.bitcast_convert_type(w & jnp.uint32(0xFFFF0000), F32)
    return a, b


def _pack_row_halves(x):
    return _pack_bf16_pair(x[:, :D_HALF], x[:, D_HALF:])


def _router_logits_t(x, w_split, bias):
    xhi = x.astype(BF16)
    xlo = (x - xhi.astype(F32)).astype(BF16)
    nt = (((1,), (1,)), ((), ()))
    both = lax.dot_general(w_split, xhi, nt, preferred_element_type=F32)
    low = lax.dot_general(w_split[:ROUTE_ROWS], xlo, nt, preferred_element_type=F32)
    return both[:ROUTE_ROWS] + both[ROUTE_ROWS:] + low + bias


def _rglru_kernel(x_ref, perm_ref, perm_t_ref, w_in_ref, convw_ref, convb_ref, wg_ref, br_ref, bi_ref, lam_ref,
                  w_out_ref, g_ref, b_ref, wrt_ref, brt_ref, o_ref, opk_ref, lg_ref,
                  xr_ext, tail_sc, a_sc, u_sc, h_carry, gate_sc):
    s = pl.program_id(1)
    ts = REC_TS
    halo = (CONV_W - 1) * SUBLANES
    seqs = range(REC_NSEQ)

    @pl.when(s == 0)
    def _():
        tail_sc[...] = jnp.zeros((REC_NSEQ, halo, D_RNN), F32)
        h_carry[...] = jnp.zeros((REC_NSEQ, 1, D_RNN), F32)

    row = lax.broadcasted_iota(jnp.int32, (SUBLANES, D_RNN), 0)
    nlam = -lam_ref[...]
    sp = jnp.maximum(nlam, 0.0) + jnp.log1p(jnp.exp(-jnp.abs(nlam)))
    log2a_scale = (-LRU_C * LOG2E) * sp

    def project(q):
        xp = jnp.dot(perm_ref[...], x_ref[q].astype(BF16), preferred_element_type=F32).astype(BF16)
        proj = jnp.dot(xp, w_in_ref[...], preferred_element_type=F32)
        gate_sc[q] = proj[:, :D_RNN]
        return proj[:, D_RNN:]

    def conv_gates(q, xr):
        for k in range(CONV_W - 1):
            r0 = ts - halo + k * SUBLANES
            cur = xr[r0:r0 + SUBLANES, :]
            prev = tail_sc[q, k * SUBLANES:(k + 1) * SUBLANES, :]
            xr_ext[q, k * SUBLANES:(k + 1) * SUBLANES, :] = jnp.where(
                row == 0, pltpu.roll(prev, 1, axis=0), pltpu.roll(cur, 1, axis=0))
        tail_sc[q] = xr[ts - halo:, :]
        xr_ext[q, halo:halo + ts, :] = xr
        xc = convb_ref[...] + convw_ref[CONV_W - 1:CONV_W, :] * xr
        for k in range(CONV_W - 1):
            xc = xc + convw_ref[k:k + 1, :] * xr_ext[q, k * SUBLANES:k * SUBLANES + ts, :]
        xcb = xc.astype(BF16)
        pres = [jnp.dot(xcb[:, GATE_WIN_STARTS[j]:GATE_WIN_STARTS[j] + GATE_WIN], wg_ref[j],
                        preferred_element_type=F32) for j in range(N_GATE_TILES)]
        for j, pre in enumerate(pres):
            cs = j * GATE_TILE
            r = jax.nn.sigmoid(pre[:, :GATE_TILE] + br_ref[:, cs:cs + GATE_TILE])
            i = jax.nn.sigmoid(pre[:, GATE_TILE:] + bi_ref[:, cs:cs + GATE_TILE])
            a = jnp.exp2(r * log2a_scale[:, cs:cs + GATE_TILE])
            s1 = 1.0 - a * a
            mult = jnp.where(s1 > 0.0, s1 * lax.rsqrt(s1), 0.0)
            u = mult * (i * xc[:, cs:cs + GATE_TILE])
            a_sc[q, :, cs:cs + GATE_TILE] = a
            u_sc[q, :, cs:cs + GATE_TILE] = u

    def segment_scan(q):
        h = jnp.zeros((SUBLANES, D_RNN), F32)
        prod = jnp.ones((SUBLANES, D_RNN), F32)
        for gidx in range(REC_GROUPS):
            rows = slice(gidx * SUBLANES, (gidx + 1) * SUBLANES)
            a8 = a_sc[q, rows, :]
            h = a8 * h + u_sc[q, rows, :]
            prod = a8 * prod
            u_sc[q, rows, :] = h
            a_sc[q, rows, :] = prod
        return h, prod

    def recur_out(q, seg):
        seg_h, seg_a = seg
        for d in (1, 2, 4):
            keep = row >= d
            a_sh = jnp.where(keep, pltpu.roll(seg_a, d, axis=0), 1.0)
            h_sh = jnp.where(keep, pltpu.roll(seg_h, d, axis=0), 0.0)
            seg_h = seg_a * h_sh + seg_h
            seg_a = seg_a * a_sh
        h_in = h_carry[q]
        after = seg_a * h_in + seg_h
        enter = jnp.where(row == 0, h_in, pltpu.roll(after, 1, axis=0))
        h_carry[q] = after[SUBLANES - 1:SUBLANES, :]
        hs = (u_sc[q].reshape(REC_GROUPS, SUBLANES, D_RNN)
              + a_sc[q].reshape(REC_GROUPS, SUBLANES, D_RNN) * enter[None]).reshape(ts, D_RNN)
        y = hs * jax.nn.gelu(gate_sc[q])
        y_t = jnp.dot(perm_t_ref[...], y.astype(BF16), preferred_element_type=F32).astype(BF16)
        return jnp.dot(y_t, w_out_ref[...], preferred_element_type=F32)

    def finish(q, out):
        z = ALPHA * x_ref[q] + out
        xn = _layer_norm_rows(z, g_ref[...], b_ref[...])
        o_ref[q] = xn
        opk_ref[q] = _pack_row_halves(xn)
        lg_ref[q] = _router_logits_t(xn, wrt_ref[...], brt_ref[...])

    assert REC_NSEQ == 2
    xr_a = project(0)
    conv_gates(0, xr_a)
    xr_b = project(1)
    out_a = recur_out(0, segment_scan(0))
    conv_gates(1, xr_b)
    finish(0, out_a)
    out_b = recur_out(1, segment_scan(1))
    finish(1, out_b)


def _band_gate_weights(w_r, w_i):
    spread = jnp.asarray(np.tile(np.eye(LRU_BLOCK_W, dtype=np.float32), (1, LRU_BLOCKS)), BF16)
    blk = np.arange(D_RNN) // LRU_BLOCK_W
    on_diag = jnp.asarray(blk[:, None] == blk[None, :])

    def dense(w):
        rows = w.reshape(D_RNN, LRU_BLOCK_W).astype(BF16)
        return jnp.where(on_diag, jnp.dot(rows, spread, preferred_element_type=F32), 0.0)

    wr, wi = dense(w_r), dense(w_i)
    tiles = []
    for j in range(N_GATE_TILES):
        ws = GATE_WIN_STARTS[j]
        cs = j * GATE_TILE
        lo_blk = cs // LRU_BLOCK_W
        hi_blk = (cs + GATE_TILE - 1) // LRU_BLOCK_W
        assert ws <= lo_blk * LRU_BLOCK_W and (hi_blk + 1) * LRU_BLOCK_W <= ws + GATE_WIN
        tiles.append(jnp.concatenate([wr[ws:ws + GATE_WIN, cs:cs + GATE_TILE],
                                      wi[ws:ws + GATE_WIN, cs:cs + GATE_TILE]], axis=1))
    return jnp.stack(tiles).astype(BF16)


def _rglru_layer(x, w_in, conv_w, conv_b, w_r, b_r, w_i, b_i, lam, w_out, ln_g, ln_b, router_w, router_b):
    B, S, D = x.shape
    wg = _band_gate_weights(w_r, w_i)
    rho = np.arange(REC_TS)
    perm_np = np.zeros((REC_TS, REC_TS), np.float32)
    perm_np[rho, (rho % SUBLANES) * REC_GROUPS + rho // SUBLANES] = 1.0
    perm = jnp.asarray(perm_np, BF16)
    perm_t = jnp.asarray(perm_np.T, BF16)
    row = lambda v: v.reshape(1, -1)
    const = lambda shape: pl.BlockSpec(shape, lambda b, s: (0,) * len(shape))
    tile = lambda w: pl.BlockSpec((REC_NSEQ, REC_TS, w), lambda b, s: (b, s, 0))
    halo = (CONV_W - 1) * SUBLANES
    return pl.pallas_call(
        _rglru_kernel,
        grid=(B // REC_NSEQ, S // REC_TS),
        in_specs=[
            tile(D),
            const((REC_TS, REC_TS)),
            const((REC_TS, REC_TS)),
            const((D, 2 * D_RNN)),
            const((CONV_W, D_RNN)),
            const((1, D_RNN)),
            const((N_GATE_TILES, GATE_WIN, 2 * GATE_TILE)),
            const((1, D_RNN)),
            const((1, D_RNN)),
            const((1, D_RNN)),
            const((D_RNN, D)),
            const((1, D)),
            const((1, D)),
            const((2 * ROUTE_ROWS, D)),
            const((ROUTE_ROWS, 1)),
        ],
        out_specs=[tile(D), tile(D_HALF),
                   pl.BlockSpec((REC_NSEQ, ROUTE_ROWS, REC_TS), lambda b, s: (b, 0, s))],
        out_shape=[jax.ShapeDtypeStruct((B, S, D), F32), jax.ShapeDtypeStruct((B, S, D_HALF), U32),
                   jax.ShapeDtypeStruct((B, ROUTE_ROWS, S), F32)],
        scratch_shapes=[
            pltpu.VMEM((REC_NSEQ, halo + REC_TS, D_RNN), F32),
            pltpu.VMEM((REC_NSEQ, halo, D_RNN), F32),
            pltpu.VMEM((REC_NSEQ, REC_TS, D_RNN), F32),
            pltpu.VMEM((REC_NSEQ, REC_TS, D_RNN), F32),
            pltpu.VMEM((REC_NSEQ, 1, D_RNN), F32),
            pltpu.VMEM((REC_NSEQ, REC_TS, D_RNN), F32),
        ],
        compiler_params=pltpu.CompilerParams(
            dimension_semantics=("arbitrary", "arbitrary"), vmem_limit_bytes=VMEM_LIMIT),
        name="rglru_ln",
    )(x, perm, perm_t, w_in.astype(BF16), conv_w, row(conv_b), wg, row(b_r), row(b_i), row(lam), w_out.astype(BF16),
      row(ln_g), row(ln_b), router_w, router_b)


def _router_kernel(logits_ref, tri_ref, idx_ref, gate_ref, cnt_ref, base_sc):
    step = pl.program_id(0)
    tr = ROUTE_T

    @pl.when(step == 0)
    def _():
        base_sc[...] = jnp.zeros((N_EXPERTS, 1), F32)

    row8 = lax.broadcasted_iota(jnp.int32, (SUBLANES, tr), 0).astype(F32)
    rowe = lax.broadcasted_iota(jnp.int32, (N_EXPERTS, tr), 0).astype(F32)
    neg_inf = -jnp.inf
    for sub in range(ROUTE_STEP // ROUTE_T):
        cols = slice(sub * tr, (sub + 1) * tr)
        _route_tile(logits_ref[0, :, cols], row8, rowe, neg_inf, tri_ref, idx_ref, gate_ref, base_sc, cols)
    cnt_ref[...] = jnp.broadcast_to(base_sc[...], (N_EXPERTS, LANES)).astype(jnp.int32)


def _route_tile(logits, row8, rowe, neg_inf, tri_ref, idx_ref, gate_ref, base_sc, cols):
    tr = ROUTE_T
    g = jnp.where(row8 < N_GROUPS, logits[N_EXPERTS:N_EXPERTS + SUBLANES, :], neg_inf)
    gmax = jnp.max(g, axis=0, keepdims=True)
    gidx = jnp.min(jnp.where(g == gmax, row8, SUBLANES), axis=0, keepdims=True)
    g_gate = 1.0 / jnp.sum(jnp.exp(g - gmax), axis=0, keepdims=True)

    esel = logits[0:EXPERTS_PER_GROUP, :]
    for grp in range(1, N_GROUPS):
        esel = jnp.where(gidx == grp, logits[grp * EXPERTS_PER_GROUP:(grp + 1) * EXPERTS_PER_GROUP, :], esel)
    v1 = jnp.max(esel, axis=0, keepdims=True)
    i1 = jnp.min(jnp.where(esel == v1, row8, SUBLANES), axis=0, keepdims=True)
    esel2 = jnp.where(row8 == i1, neg_inf, esel)
    v2 = jnp.max(esel2, axis=0, keepdims=True)
    i2 = jnp.min(jnp.where(esel2 == v2, row8, SUBLANES), axis=0, keepdims=True)
    e21 = jnp.exp(v2 - v1)
    inv = 1.0 / (1.0 + e21)
    gate1 = inv * g_gate
    gate2 = e21 * inv * g_gate
    e1 = gidx * EXPERTS_PER_GROUP + i1
    e2 = gidx * EXPERTS_PER_GROUP + i2

    hit1 = rowe == e1
    hit2 = rowe == e2
    member = jnp.where(hit1, 1.0, jnp.where(hit2, 1.0, 0.0))
    before = jnp.dot(member.astype(BF16), tri_ref[...], preferred_element_type=F32) + base_sc[...]
    rank1 = jnp.sum(jnp.where(hit1, before, 0.0), axis=0, keepdims=True)
    rank2 = jnp.sum(jnp.where(hit2, before, 0.0), axis=0, keepdims=True)
    base_sc[...] = base_sc[...] + jnp.sum(member, axis=1, keepdims=True)

    zi = jnp.zeros((1, tr), jnp.int32)
    idx_ref[:, cols] = jnp.concatenate(
        [e1.astype(jnp.int32), e2.astype(jnp.int32), rank1.astype(jnp.int32), rank2.astype(jnp.int32),
         zi, zi, zi, zi], axis=0)
    zf = jnp.zeros((1, tr), F32)
    gate_ref[:, cols] = jnp.concatenate([gate1, gate2, zf, zf, zf, zf, zf, zf], axis=0)


def _router_weights(w_rg, b_rg, w_re, b_re):
    D = w_rg.shape[0]
    pad_rows = ROUTE_ROWS - N_EXPERTS - N_GROUPS
    w = jnp.concatenate([w_re.T, w_rg.T, jnp.zeros((pad_rows, D), F32)], axis=0)
    whi = w.astype(BF16)
    wlo = (w - whi.astype(F32)).astype(BF16)
    w_split = jnp.concatenate([whi, wlo], axis=0)
    bias = jnp.concatenate([b_re, b_rg, jnp.zeros((pad_rows,), F32)]).reshape(ROUTE_ROWS, 1)
    return w_split, bias


def _router(logits_t):
    B, _, S = logits_t.shape
    T = B * S
    per_row = S // ROUTE_STEP
    tri = jnp.asarray(np.triu(np.ones((ROUTE_T, ROUTE_T), np.float32), 1), BF16)
    const = lambda shape: pl.BlockSpec(shape, lambda i: (0,) * len(shape))
    return pl.pallas_call(
        _router_kernel,
        grid=(T // ROUTE_STEP,),
        in_specs=[
            pl.BlockSpec((1, ROUTE_ROWS, ROUTE_STEP), lambda i: (i // per_row, 0, i % per_row)),
            const((ROUTE_T, ROUTE_T)),
        ],
        out_specs=[
            pl.BlockSpec((SUBLANES, ROUTE_STEP), lambda i: (0, i)),
            pl.BlockSpec((SUBLANES, ROUTE_STEP), lambda i: (0, i)),
            const((N_EXPERTS, LANES)),
        ],
        out_shape=[
            jax.ShapeDtypeStruct((SUBLANES, T), jnp.int32),
            jax.ShapeDtypeStruct((SUBLANES, T), F32),
            jax.ShapeDtypeStruct((N_EXPERTS, LANES), jnp.int32),
        ],
        scratch_shapes=[pltpu.VMEM((N_EXPERTS, 1), F32)],
        compiler_params=pltpu.CompilerParams(
            dimension_semantics=("arbitrary",), vmem_limit_bytes=VMEM_LIMIT),
        name="router",
    )(logits_t, tri)


def _sc_mesh():
    return plsc.VectorSubcoreMesh(core_axis_name="c", subcore_axis_name="s",
                                  num_cores=SC_CORES, num_subcores=SC_SUBCORES)


def _sc_worker_id():
    return lax.axis_index("s") * SC_CORES + lax.axis_index("c")


def _sc_scratch(n_win, width):
    return [
        pltpu.VMEM((n_win, SC_WIN), jnp.int32),
        pltpu.VMEM((n_win, SC_WIN), jnp.int32),
        pltpu.VMEM((2, SC_WIN, width), U32),
        pltpu.SemaphoreType.DMA((2,)),
        pltpu.SemaphoreType.DMA((2,)),
    ]


def _sc_dispatch(rows, idx1, idx2, n_rows):
    _, width = rows.shape
    _, n_win, _ = idx1.shape

    @functools.partial(
        pl.kernel, mesh=_sc_mesh(), out_type=jax.ShapeDtypeStruct((n_rows, width), rows.dtype),
        scratch_types=_sc_scratch(n_win, width), name="sc_dispatch")
    def run(rows_hbm, i1_hbm, i2_hbm, o_hbm, i1_v, i2_v, buf, rsem, wsem):
        wid = _sc_worker_id()
        base = wid * (n_win * SC_WIN)
        pltpu.sync_copy(i1_hbm.at[wid], i1_v)
        pltpu.sync_copy(i2_hbm.at[wid], i2_v)

        def read(j):
            return pltpu.async_copy(rows_hbm.at[pl.ds(base + j * SC_WIN, SC_WIN)], buf.at[j % 2], rsem.at[j % 2])

        reads = {0: read(0)}
        writes = {}
        for j in range(n_win):
            if j + 1 < n_win:
                for d in writes.pop(j - 1, ()):
                    d.wait()
                reads[j + 1] = read(j + 1)
            reads.pop(j).wait()
            writes[j] = (pltpu.async_copy(buf.at[j % 2], o_hbm.at[i1_v.at[j]], wsem.at[j % 2]),
                         pltpu.async_copy(buf.at[j % 2], o_hbm.at[i2_v.at[j]], wsem.at[j % 2]))
        for j in sorted(writes):
            for d in writes[j]:
                d.wait()

    return run(rows, idx1, idx2)


def _sc_gather_pair(table, idx1, idx2):
    _, width = table.shape
    _, n_win, _ = idx1.shape
    n_tok = SC_WORKERS * n_win * SC_WIN
    out_t = jax.ShapeDtypeStruct((n_tok, width), table.dtype)

    @functools.partial(
        pl.kernel, mesh=_sc_mesh(), out_type=(out_t, out_t),
        scratch_types=_sc_scratch(n_win, width), name="sc_combine_gather")
    def run(table_hbm, i1_hbm, i2_hbm, o1_hbm, o2_hbm, i1_v, i2_v, buf, gsem, wsem):
        wid = _sc_worker_id()
        base = wid * (n_win * SC_WIN)
        pltpu.sync_copy(i1_hbm.at[wid], i1_v)
        pltpu.sync_copy(i2_hbm.at[wid], i2_v)
        work = [(i1_v, o1_hbm, j) for j in range(n_win)] + [(i2_v, o2_hbm, j) for j in range(n_win)]

        def gather(t):
            iv, _, j = work[t]
            return pltpu.async_copy(table_hbm.at[iv.at[j]], buf.at[t % 2], gsem.at[t % 2])

        def put(t):
            _, oh, j = work[t]
            return pltpu.async_copy(buf.at[t % 2], oh.at[pl.ds(base + j * SC_WIN, SC_WIN)], wsem.at[t % 2])

        gathers = {0: gather(0)}
        puts = {}
        for t in range(len(work)):
            if t + 1 < len(work):
                if t - 1 in puts:
                    puts.pop(t - 1).wait()
                gathers[t + 1] = gather(t + 1)
            gathers.pop(t).wait()
            puts[t] = put(t)
        for t in sorted(puts):
            puts[t].wait()

    return run(table, idx1, idx2)


def _moe_kernel(layer, be_ref, slot_ref, nxt_ref, valid_ref, x_ref, w1_hbm, w3_hbm, w2_hbm, o_ref,
                w1_st, w3_st, w2_st, w1_sc, w3_sc, w2_sc, sems):
    i = pl.program_id(0)
    expert = be_ref[i]
    slot = slot_ref[i]
    new_expert = jnp.logical_or(i == 0, expert != be_ref[jnp.maximum(i - 1, 0)])

    def weight_copies(e, sl):
        return [pltpu.make_async_copy(hbm.at[layer, e], stage.at[sl], sems.at[k, sl])
                for k, (hbm, stage) in enumerate(((w1_hbm, w1_st), (w3_hbm, w3_st), (w2_hbm, w2_st)))]

    @pl.when(i == 0)
    def _():
        for cp in weight_copies(expert, slot):
            cp.start()

    @pl.when(new_expert)
    def _():
        for cp in weight_copies(expert, slot):
            cp.wait()
        w1_sc[...] = w1_st[slot].astype(BF16)
        w3_sc[...] = w3_st[slot].astype(BF16)
        w2_sc[...] = w2_st[slot].astype(BF16)
        nxt = nxt_ref[i]

        @pl.when(nxt >= 0)
        def _():
            for cp in weight_copies(nxt, 1 - slot):
                cp.start()

    def up(rows):
        xa, xb = _unpack_bf16_pair(x_ref[rows, :])
        xa = xa.astype(BF16)
        xb = xb.astype(BF16)
        h1 = (jnp.dot(xa, w1_sc[:D_HALF], preferred_element_type=F32)
              + jnp.dot(xb, w1_sc[D_HALF:], preferred_element_type=F32))
        h3 = (jnp.dot(xa, w3_sc[:D_HALF], preferred_element_type=F32)
              + jnp.dot(xb, w3_sc[D_HALF:], preferred_element_type=F32))
        return h1, h3

    def down(rows, h1, h3):
        hdn = (jax.nn.silu(h1) * h3).astype(BF16)
        y = jnp.dot(hdn, w2_sc[...], preferred_element_type=F32)
        o_ref[rows, :] = _pack_row_halves(y)

    n_sub = MOE_BM // MOE_SUB
    subs = [slice(k * MOE_SUB, (k + 1) * MOE_SUB) for k in range(n_sub)]
    valid = valid_ref[i]
    chains = (valid + (MOE_SUB - 1)) // MOE_SUB

    for live in range(n_sub + 1):
        @pl.when(chains == live)
        def _(live=live):
            ups = {}
            if live:
                ups[0] = up(subs[0])
            for k in range(live):
                if k + 1 < live:
                    ups[k + 1] = up(subs[k + 1])
                down(subs[k], *ups.pop(k))
            if live < n_sub:
                o_ref[live * MOE_SUB:, :] = jnp.zeros((MOE_BM - live * MOE_SUB, D_HALF), o_ref.dtype)


def _moe_blocks(xbuf, block_e, valid, w1, w3, w2, layer):
    n_rows, _ = xbuf.shape
    D = D_MODEL
    n_blocks = n_rows // MOE_BM
    pos = jnp.arange(n_blocks, dtype=jnp.int32)
    is_new = jnp.concatenate([jnp.ones((1,), bool), block_e[1:] != block_e[:-1]])
    slot = ((jnp.cumsum(is_new.astype(jnp.int32)) - 1) % 2).astype(jnp.int32)
    change_pos = jnp.where(is_new, pos, n_blocks)
    next_change = jnp.concatenate([lax.cummin(change_pos, reverse=True)[1:], jnp.full((1,), n_blocks, jnp.int32)])
    nxt = jnp.where(next_change < n_blocks, block_e[jnp.minimum(next_change, n_blocks - 1)], -1).astype(jnp.int32)
    rows = lambda i, be, sl, nx, nu: (i, 0)
    grid_spec = pltpu.PrefetchScalarGridSpec(
        num_scalar_prefetch=4,
        grid=(n_blocks,),
        in_specs=[
            pl.BlockSpec((MOE_BM, D_HALF), rows),
            pl.BlockSpec(memory_space=pl.ANY),
            pl.BlockSpec(memory_space=pl.ANY),
            pl.BlockSpec(memory_space=pl.ANY),
        ],
        out_specs=pl.BlockSpec((MOE_BM, D_HALF), rows),
        scratch_shapes=[
            pltpu.VMEM((2, D, D_EXPERT), F32),
            pltpu.VMEM((2, D, D_EXPERT), F32),
            pltpu.VMEM((2, D_EXPERT, D), F32),
            pltpu.VMEM((D, D_EXPERT), BF16),
            pltpu.VMEM((D, D_EXPERT), BF16),
            pltpu.VMEM((D_EXPERT, D), BF16),
            pltpu.SemaphoreType.DMA((3, 2)),
        ],
    )
    return pl.pallas_call(
        functools.partial(_moe_kernel, layer),
        grid_spec=grid_spec,
        out_shape=jax.ShapeDtypeStruct((n_rows, D_HALF), U32),
        compiler_params=pltpu.CompilerParams(
            dimension_semantics=("arbitrary",), vmem_limit_bytes=VMEM_LIMIT),
        name="moe_experts",
    )(block_e, slot, nxt, valid, xbuf, w1, w3, w2)


def _moe_combine_norm(x, y1, y2, gate_rows, g, b):
    n = x.shape[0]
    pad = jnp.zeros((LANES - SUBLANES, n), F32)
    gates = jnp.concatenate([gate_rows, pad], axis=0).T
    g1 = gates[:, 0:1]
    g2 = gates[:, 1:2]
    a1, b1 = _unpack_bf16_pair(y1)
    a2, b2 = _unpack_bf16_pair(y2)
    f = jnp.concatenate([g1 * a1 + g2 * a2, g1 * b1 + g2 * b2], axis=1)
    return _layer_norm_rows(ALPHA * x + f, g, b)


def _combine_kernel(x_ref, y1_ref, y2_ref, gates_ref, g_ref, b_ref, o_ref):
    o_ref[...] = _moe_combine_norm(x_ref[...], y1_ref[...], y2_ref[...], gates_ref[...], g_ref[...], b_ref[...])


def _combine_part_kernel(x_ref, y1_ref, y2_ref, gates_ref, g_ref, b_ref, prev_ref, o_ref):
    del prev_ref
    _combine_kernel(x_ref, y1_ref, y2_ref, gates_ref, g_ref, b_ref, o_ref)


def _combine_ln(xf, y_parts, gates, ln_g, ln_b):
    T, D = xf.shape
    n_parts = len(y_parts)
    steps = T // (COMB_T * n_parts)
    const = lambda shape: pl.BlockSpec(shape, lambda i: (0,) * len(shape))
    out = None
    for part, (y1, y2) in enumerate(y_parts):
        off = part * steps
        glob = lambda w, off=off: pl.BlockSpec((COMB_T, w), lambda i: (i + off, 0))
        local = lambda w: pl.BlockSpec((COMB_T, w), lambda i: (i, 0))
        in_specs = [glob(D), local(D_HALF), local(D_HALF),
                    pl.BlockSpec((SUBLANES, COMB_T), lambda i, off=off: (0, i + off)), const((1, D)), const((1, D))]
        args = [xf, y1, y2, gates, ln_g.reshape(1, D), ln_b.reshape(1, D)]
        if out is not None:
            in_specs.append(pl.BlockSpec(memory_space=pl.ANY))
            args.append(out)
        out = pl.pallas_call(
            _combine_kernel if out is None else _combine_part_kernel,
            grid=(steps,),
            in_specs=in_specs,
            out_specs=glob(D),
            out_shape=jax.ShapeDtypeStruct((T, D), F32),
            input_output_aliases={} if out is None else {len(args) - 1: 0},
            compiler_params=pltpu.CompilerParams(
                dimension_semantics=("arbitrary",), vmem_limit_bytes=VMEM_LIMIT),
            name="moe_combine_ln",
        )(*args)
    return out


def _hier_moe(xpk, logits_t, w1, w3, w2, layer, gather_parts):
    B, S, _ = xpk.shape
    D = D_MODEL
    T = B * S
    idx, gates, cnt = _router(logits_t)
    counts = cnt[:, 0]
    padded = ((counts + MOE_BM - 1) // MOE_BM) * MOE_BM
    pends = jnp.cumsum(padded)
    pstarts = pends - padded
    experts = jnp.arange(N_EXPERTS, dtype=jnp.int32)[:, None]

    def dest(e_row, rank_row):
        return jnp.sum(jnp.where(e_row[None, :] == experts, pstarts[:, None], 0), axis=0) + rank_row

    n_win = T // (SC_WORKERS * SC_WIN)
    dest1 = dest(idx[0], idx[2]).reshape(SC_WORKERS, n_win, SC_WIN)
    dest2 = dest(idx[1], idx[3]).reshape(SC_WORKERS, n_win, SC_WIN)
    n_blocks = -(-(T * TOP_K + N_EXPERTS * (MOE_BM - 1)) // MOE_BM)
    n_rows = n_blocks * MOE_BM
    block_start = jnp.arange(n_blocks, dtype=jnp.int32) * MOE_BM
    block_e = jnp.minimum(jnp.sum(block_start[:, None] >= pends[None, :], axis=1), N_EXPERTS - 1).astype(jnp.int32)
    of_block = block_e[:, None] == experts[:, 0][None, :]
    pick = lambda v: jnp.sum(jnp.where(of_block, v[None, :], 0), axis=1)
    valid = jnp.clip(pick(counts) - (block_start - pick(pstarts)), 0, MOE_BM).astype(jnp.int32)
    xbuf = _sc_dispatch(xpk.reshape(T, D_HALF), dest1, dest2, n_rows)
    ybuf = _moe_blocks(xbuf, block_e, valid, w1, w3, w2, layer)
    part_shape = (gather_parts, SC_WORKERS, n_win // gather_parts, SC_WIN)
    d1, d2 = dest1.reshape(part_shape), dest2.reshape(part_shape)
    y_parts = [_sc_gather_pair(ybuf, d1[p], d2[p]) for p in range(gather_parts)]
    return y_parts, gates


def _att_head_order():
    order = []
    for p in range(N_HEADS // 2):
        jj, m = divmod(p, 4)
        order += [8 * jj + m, 8 * jj + 4 + m]
    return order


ATT_HEAD_ORDER = _att_head_order()


def _attn_kernel(xprev_ref, y1_ref, y2_ref, gates_a_ref, gates_b_ref, g_prev_ref, b_prev_ref, wqkv_ref, bias_ref,
                 sink_ref, wo_ref, g_ref, b_ref, wrt_ref, brt_ref, o_ref, opk_ref, lg_ref,
                 kv_ext, o_sc, s_sc0, s_sc1, p_sc0, p_sc1):
    s = pl.program_id(1)
    tq = ATT_TQ
    s_bufs = (s_sc0, s_sc1)
    p_bufs = (p_sc0, p_sc1)
    gate_refs = (gates_a_ref, gates_b_ref)
    assert ATT_NSEQ == len(gate_refs)

    @pl.when(s == 0)
    def _():
        kv_ext[:, 0:WINDOW, :] = jnp.zeros((ATT_NSEQ, WINDOW, 2 * KV_DIM), BF16)

    def layer_input(sq):
        return _moe_combine_norm(xprev_ref[sq], y1_ref[sq], y2_ref[sq], gate_refs[sq][...],
                                 g_prev_ref[...], b_prev_ref[...])

    def project_qkv(sq, x):
        qkv = jnp.dot(x.astype(BF16), wqkv_ref[...], preferred_element_type=F32)
        kv_ext[sq, WINDOW:WINDOW + tq, :] = qkv[:, Q_DIM:].astype(BF16)
        return (qkv[:, :Q_DIM] * (HEAD_DIM ** -0.5 * LOG2E)).astype(BF16)

    lane = lax.broadcasted_iota(jnp.int32, (WINDOW, LANES), 1)
    low = lane < HEAD_DIM
    sub = lax.broadcasted_iota(jnp.int32, (LANES, WINDOW), 0)
    top = sub < HEAD_DIM
    first = jnp.where(s == 0, 1, 0)
    nt = (((1,), (1,)), ((), ()))
    zero = jnp.zeros((), BF16)

    tiles = [(sq, n, j) for n in range(ATT_NB) for j in range(2) for sq in range(ATT_NSEQ)]

    def scores(t, qs_all):
        sq, n, j = tiles[t]
        q = qs_all[sq]
        r0 = n * WINDOW
        k_tile = kv_ext[sq, r0:r0 + 2 * WINDOW, j * LANES:(j + 1) * LANES]
        parts = []
        for m in range(4):
            p = 4 * j + m
            qt = q[r0:r0 + WINDOW, p * LANES:(p + 1) * LANES]
            parts.append(jnp.where(low, qt, zero))
            parts.append(jnp.where(low, zero, qt))
        qs = jnp.concatenate(parts, axis=0)
        bias_sel = first if n == 0 else 0
        s_bufs[t % 2][...] = (lax.dot_general(k_tile, qs, nt, preferred_element_type=F32)
                              + bias_ref[bias_sel, j])

    def softmax_pv(t):
        sq, n, j = tiles[t]
        r0 = n * WINDOW
        s_sc = s_bufs[t % 2]
        p_sc = p_bufs[t % 2]
        inv_l = []
        for h in range(8):
            hc = slice(h * WINDOW, (h + 1) * WINDOW)
            sink = sink_ref[8 * j + h] * LOG2E
            mx = jnp.maximum(jnp.max(s_sc[:, hc], axis=0, keepdims=True), sink)
            pr = jnp.exp2(s_sc[:, hc] - mx)
            p_sc[:, hc] = pr.astype(BF16)
            inv_l.append(1.0 / (jnp.sum(pr, axis=0, keepdims=True) + jnp.exp2(sink - mx)))
        v_tile = kv_ext[sq, r0:r0 + 2 * WINDOW, KV_DIM + j * LANES:KV_DIM + (j + 1) * LANES]
        v_t = v_tile.astype(F32).T.astype(BF16)
        ov = jnp.dot(v_t, p_sc[...], preferred_element_type=F32)
        for m in range(4):
            p = 4 * j + m
            o_even = ov[:, (2 * m) * WINDOW:(2 * m + 1) * WINDOW] * inv_l[2 * m]
            o_odd = ov[:, (2 * m + 1) * WINDOW:(2 * m + 2) * WINDOW] * inv_l[2 * m + 1]
            o_sc[sq, p * LANES:(p + 1) * LANES, r0:r0 + WINDOW] = jnp.where(top, o_even, o_odd).astype(BF16)

    def project_out(sq):
        return jnp.dot(wo_ref[...], o_sc[sq], preferred_element_type=F32)

    def finish(sq, x, out_t):
        xn = _layer_norm_rows(ALPHA * x + out_t.T, g_ref[...], b_ref[...])
        o_ref[sq] = xn
        opk_ref[sq] = _pack_row_halves(xn)
        lg_ref[sq] = _router_logits_t(xn, wrt_ref[...], brt_ref[...])

    xs = [layer_input(0)]
    qs_all = [project_qkv(0, xs[0])]
    xs.append(layer_input(1))
    qs_all.append(project_qkv(1, xs[1]))
    scores(0, qs_all)
    for t in range(len(tiles)):
        if t + 1 < len(tiles):
            scores(t + 1, qs_all)
        softmax_pv(t)
    kv_ext[:, 0:WINDOW, :] = kv_ext[:, tq:tq + WINDOW, :]
    outs_t = [project_out(sq) for sq in range(ATT_NSEQ)]
    for sq in range(ATT_NSEQ):
        finish(sq, xs[sq], outs_t[sq])


def _attn_bias():
    qi = np.arange(WINDOW)[:, None]
    sj = np.arange(2 * WINDOW)[None, :]
    dist = qi - sj + WINDOW
    valid = (dist >= 0) & (dist < WINDOW)
    slopes = 2.0 ** (-8.0 * np.arange(1, N_HEADS + 1, dtype=np.float32) / N_HEADS)
    slopes = slopes.astype(np.float32)[ATT_HEAD_ORDER]
    sb = -(slopes[:, None, None] * dist.astype(np.float32)[None])
    later = np.where(valid[None], sb, -np.inf)
    first = np.where((valid & (sj >= WINDOW))[None], sb, -np.inf)
    bias = np.stack([later, first]).astype(np.float32) * np.float32(LOG2E)
    bias = bias.reshape(2, 2, 8, WINDOW, 2 * WINDOW).transpose(0, 1, 4, 2, 3).reshape(2, 2, 2 * WINDOW, 8 * WINDOW)
    return jnp.asarray(np.ascontiguousarray(bias))


ATT_N_INPUTS = 15


def _attn_part_kernel(*refs):
    _attn_kernel(*refs[:ATT_N_INPUTS], *refs[ATT_N_INPUTS + 3:])


def _attn_layer(x_prev, y_parts, gates, g_prev, b_prev, w_qkv, sinks, w_o, ln_g, ln_b, router_w, router_b):
    B, S, D = x_prev.shape
    steps = S // ATT_TQ
    assert ATT_HEAD_ORDER == list(np.arange(N_HEADS).reshape(2, 2, 4).transpose(0, 2, 1).reshape(-1))
    wq = w_qkv[:, :Q_DIM].reshape(D, 2, 2, 4, HEAD_DIM).transpose(0, 1, 3, 2, 4).reshape(D, Q_DIM)
    wqkv = jnp.concatenate([wq, w_qkv[:, Q_DIM:]], axis=1).astype(BF16)
    wo_t = w_o.reshape(2, 2, 4, HEAD_DIM, D).transpose(0, 2, 1, 3, 4).reshape(Q_DIM, D).T.astype(BF16)
    sink = sinks.reshape(2, 2, 4).transpose(0, 2, 1).reshape(N_HEADS, 1, 1)
    bias = _attn_bias()
    const = lambda shape: pl.BlockSpec(shape, lambda b, s: (0,) * len(shape))
    n_parts = len(y_parts)
    rows_per_part = B // n_parts
    pairs = rows_per_part // ATT_NSEQ
    outs = None
    for part, (y1, y2) in enumerate(y_parts):
        off = part * pairs
        glob = lambda w, off=off: pl.BlockSpec((ATT_NSEQ, ATT_TQ, w), lambda b, s: (b + off, s, 0))
        local = lambda w: pl.BlockSpec((ATT_NSEQ, ATT_TQ, w), lambda b, s: (b, s, 0))
        gate_rows = lambda sq, off=off: pl.BlockSpec(
            (SUBLANES, ATT_TQ), lambda b, s: (0, (ATT_NSEQ * (b + off) + sq) * steps + s))
        in_specs = [
            glob(D),
            local(D_HALF),
            local(D_HALF),
            gate_rows(0),
            gate_rows(1),
            const((1, D)),
            const((1, D)),
            const((D, Q_DIM + 2 * KV_DIM)),
            const((2, 2, 2 * WINDOW, 8 * WINDOW)),
            const((N_HEADS, 1, 1)),
            const((D, Q_DIM)),
            const((1, D)),
            const((1, D)),
            const((2 * ROUTE_ROWS, D)),
            const((ROUTE_ROWS, 1)),
        ]
        args = [x_prev, y1.reshape(rows_per_part, S, D_HALF), y2.reshape(rows_per_part, S, D_HALF), gates, gates,
                g_prev.reshape(1, D), b_prev.reshape(1, D), wqkv, bias, sink, wo_t,
                ln_g.reshape(1, D), ln_b.reshape(1, D), router_w, router_b]
        aliases = {}
        body = _attn_kernel
        if outs is not None:
            aliases = {len(args) + k: k for k in range(3)}
            in_specs += [pl.BlockSpec(memory_space=pl.ANY)] * 3
            args += list(outs)
            body = _attn_part_kernel
        outs = pl.pallas_call(
            body,
            grid=(pairs, steps),
            in_specs=in_specs,
            out_specs=[glob(D), glob(D_HALF),
                       pl.BlockSpec((ATT_NSEQ, ROUTE_ROWS, ATT_TQ), lambda b, s, off=off: (b + off, 0, s))],
            out_shape=[jax.ShapeDtypeStruct((B, S, D), F32), jax.ShapeDtypeStruct((B, S, D_HALF), U32),
                       jax.ShapeDtypeStruct((B, ROUTE_ROWS, S), F32)],
            input_output_aliases=aliases,
            scratch_shapes=[
                pltpu.VMEM((ATT_NSEQ, ATT_TQ + WINDOW, 2 * KV_DIM), BF16),
                pltpu.VMEM((ATT_NSEQ, Q_DIM, ATT_TQ), BF16),
                pltpu.VMEM((2 * WINDOW, 8 * WINDOW), F32),
                pltpu.VMEM((2 * WINDOW, 8 * WINDOW), F32),
                pltpu.VMEM((2 * WINDOW, 8 * WINDOW), BF16),
                pltpu.VMEM((2 * WINDOW, 8 * WINDOW), BF16),
            ],
            compiler_params=pltpu.CompilerParams(
                dimension_semantics=("arbitrary", "arbitrary"), vmem_limit_bytes=VMEM_LIMIT),
            name="swa_attn_ln",
        )(*args)
    return outs


def kernel(x, rec_w_in, rec_conv_w, rec_conv_b, rec_w_r, rec_b_r, rec_w_i, rec_b_i, rec_lambda, rec_w_out,
           att_w_qkv, att_sinks, att_w_o, moe_w_group, moe_b_group, moe_w_expert, moe_b_expert,
           moe_w1, moe_w3, moe_w2, ln_g, ln_b):
    assert DEPTH == 2
    B, S, D = x.shape

    router = [_router_weights(moe_w_group[layer], moe_b_group[layer], moe_w_expert[layer], moe_b_expert[layer])
              for layer in range(DEPTH)]

    x1, x1_pk, logits1 = _rglru_layer(x, rec_w_in[0], rec_conv_w[0], rec_conv_b[0], rec_w_r[0], rec_b_r[0],
                                      rec_w_i[0], rec_b_i[0], rec_lambda[0], rec_w_out[0], ln_g[0, 0], ln_b[0, 0],
                                      *router[0])
    y_parts, gates = _hier_moe(x1_pk, logits1, moe_w1, moe_w3, moe_w2, 0, gather_parts=ATT_PARTS)
    x3, x3_pk, logits3 = _attn_layer(x1, y_parts, gates, ln_g[0, 1], ln_b[0, 1], att_w_qkv[0], att_sinks[0],
                                     att_w_o[0], ln_g[1, 0], ln_b[1, 0], *router[1])
    y_parts, gates = _hier_moe(x3_pk, logits3, moe_w1, moe_w3, moe_w2, 1, gather_parts=1)
    out = _combine_ln(x3.reshape(B * S, D), y_parts, gates, ln_g[1, 1], ln_b[1, 1])
    return out.reshape(B, S, D)
```

```python
import functools

import jax
import jax.numpy as jnp
import numpy as np
from jax import lax
from jax.experimental import pallas as pl
from jax.experimental.pallas import tpu as pltpu
from jax.experimental.pallas import tpu_sc as plsc

F32 = jnp.float32
BF16 = jnp.bfloat16
U32 = jnp.uint32

D_MODEL = 1024
DEPTH = 2
D_RNN = 1280
LRU_BLOCKS = 16
LRU_BLOCK_W = D_RNN // LRU_BLOCKS
CONV_W = 4
LRU_C = 8.0
N_HEADS = 16
N_KV_HEADS = 4
HEAD_DIM = 64
WINDOW = 128
Q_DIM = N_HEADS * HEAD_DIM
KV_DIM = N_KV_HEADS * HEAD_DIM
N_GROUPS = 4
EXPERTS_PER_GROUP = 8
N_EXPERTS = N_GROUPS * EXPERTS_PER_GROUP
TOP_K = 2
D_EXPERT = 512
ALPHA = (2 * DEPTH) ** 0.25
LN_EPS = 1e-5
LOG2E = 1.4426950408889634

LANES = 128
SUBLANES = 8
VMEM_LIMIT = 56 * 1024 * 1024

REC_TS = 256
REC_GROUPS = REC_TS // SUBLANES
REC_NSEQ = 2
GATE_TILE = 256
GATE_WIN = 512
GATE_WIN_STARTS = (0, 128, 384, 640, 768)
N_GATE_TILES = D_RNN // GATE_TILE

ROUTE_T = 512
ROUTE_STEP = 2048
ROUTE_ROWS = 40

MOE_BM = 1024
MOE_SUB = 256

ATT_TQ = 256
ATT_NB = ATT_TQ // WINDOW
ATT_NSEQ = 2
ATT_PARTS = 4

COMB_T = 1024

D_HALF = D_MODEL // 2

SC_CORES = 2
SC_SUBCORES = 16
SC_WORKERS = SC_CORES * SC_SUBCORES
SC_WIN = 64


def _layer_norm_rows(z, g, b):
    mu = jnp.mean(z, axis=-1, keepdims=True)
    zc = z - mu
    var = jnp.mean(zc * zc, axis=-1, keepdims=True)
    return zc * lax.rsqrt(var + LN_EPS) * g + b


def _pack_bf16_pair(a, b):
    ua = lax.bitcast_convert_type(a.astype(BF16).astype(F32), U32)
    ub = lax.bitcast_convert_type(b.astype(BF16).astype(F32), U32)
    return (ua >> 16) | (ub & jnp.uint32(0xFFFF0000))


def _unpack_bf16_pair(w):
    a = lax.bitcast_convert_type(w << 16, F32)
    b = lax.bitcast_convert_type(w & jnp.uint32(0xFFFF0000), F32)
    return a, b


def _pack_row_halves(x):
    return _pack_bf16_pair(x[:, :D_HALF], x[:, D_HALF:])


def _router_logits_t(x, w_split, bias):
    xhi = x.astype(BF16)
    xlo = (x - xhi.astype(F32)).astype(BF16)
    nt = (((1,), (1,)), ((), ()))
    both = lax.dot_general(w_split, xhi, nt, preferred_element_type=F32)
    low = lax.dot_general(w_split[:ROUTE_ROWS], xlo, nt, preferred_element_type=F32)
    return both[:ROUTE_ROWS] + both[ROUTE_ROWS:] + low + bias


def _rglru_kernel(x_ref, perm_ref, perm_t_ref, w_in_ref, convw_ref, convb_ref, wg_ref, br_ref, bi_ref, lam_ref,
                  w_out_ref, g_ref, b_ref, wrt_ref, brt_ref, o_ref, opk_ref, lg_ref,
                  xr_ext, tail_sc, a_sc, u_sc, h_carry):
    s = pl.program_id(1)
    ts = REC_TS
    halo = (CONV_W - 1) * SUBLANES
    seqs = range(REC_NSEQ)

    @pl.when(s == 0)
    def _():
        tail_sc[...] = jnp.zeros((REC_NSEQ, halo, D_RNN), F32)
        h_carry[...] = jnp.zeros((REC_NSEQ, 1, D_RNN), F32)

    row = lax.broadcasted_iota(jnp.int32, (SUBLANES, D_RNN), 0)
    nlam = -lam_ref[...]
    sp = jnp.maximum(nlam, 0.0) + jnp.log1p(jnp.exp(-jnp.abs(nlam)))
    log2a_scale = (-LRU_C * LOG2E) * sp

    def project(q):
        xp = jnp.dot(perm_ref[...], x_ref[q].astype(BF16), preferred_element_type=F32).astype(BF16)
        proj = jnp.dot(xp, w_in_ref[...], preferred_element_type=F32)
        return proj[:, :D_RNN], proj[:, D_RNN:]

    def conv_gates(q, xr):
        for k in range(CONV_W - 1):
            r0 = ts - halo + k * SUBLANES
            cur = xr[r0:r0 + SUBLANES, :]
            prev = tail_sc[q, k * SUBLANES:(k + 1) * SUBLANES, :]
            xr_ext[q, k * SUBLANES:(k + 1) * SUBLANES, :] = jnp.where(
                row == 0, pltpu.roll(prev, 1, axis=0), pltpu.roll(cur, 1, axis=0))
        tail_sc[q] = xr[ts - halo:, :]
        xr_ext[q, halo:halo + ts, :] = xr
        xc = convb_ref[...] + convw_ref[CONV_W - 1:CONV_W, :] * xr
        for k in range(CONV_W - 1):
            xc = xc + convw_ref[k:k + 1, :] * xr_ext[q, k * SUBLANES:k * SUBLANES + ts, :]
        xcb = xc.astype(BF16)
        pres = [jnp.dot(xcb[:, GATE_WIN_STARTS[j]:GATE_WIN_STARTS[j] + GATE_WIN], wg_ref[j],
                        preferred_element_type=F32) for j in range(N_GATE_TILES)]
        for j, pre in enumerate(pres):
            cs = j * GATE_TILE
            r = jax.nn.sigmoid(pre[:, :GATE_TILE] + br_ref[:, cs:cs + GATE_TILE])
            i = jax.nn.sigmoid(pre[:, GATE_TILE:] + bi_ref[:, cs:cs + GATE_TILE])
            a = jnp.exp2(r * log2a_scale[:, cs:cs + GATE_TILE])
            s1 = 1.0 - a * a
            mult = jnp.where(s1 > 0.0, s1 * lax.rsqrt(s1), 0.0)
            u = mult * (i * xc[:, cs:cs + GATE_TILE])
            a_sc[q, :, cs:cs + GATE_TILE] = a
            u_sc[q, :, cs:cs + GATE_TILE] = u

    def segment_scan(q):
        h = jnp.zeros((SUBLANES, D_RNN), F32)
        prod = jnp.ones((SUBLANES, D_RNN), F32)
        for gidx in range(REC_GROUPS):
            rows = slice(gidx * SUBLANES, (gidx + 1) * SUBLANES)
            a8 = a_sc[q, rows, :]
            h = a8 * h + u_sc[q, rows, :]
            prod = a8 * prod
            u_sc[q, rows, :] = h
            a_sc[q, rows, :] = prod
        return h, prod

    def recur_out(q, gate, seg):
        seg_h, seg_a = seg
        for d in (1, 2, 4):
            keep = row >= d
            a_sh = jnp.where(keep, pltpu.roll(seg_a, d, axis=0), 1.0)
            h_sh = jnp.where(keep, pltpu.roll(seg_h, d, axis=0), 0.0)
            seg_h = seg_a * h_sh + seg_h
            seg_a = seg_a * a_sh
        h_in = h_carry[q]
        after = seg_a * h_in + seg_h
        enter = jnp.where(row == 0, h_in, pltpu.roll(after, 1, axis=0))
        h_carry[q] = after[SUBLANES - 1:SUBLANES, :]
        hs = (u_sc[q].reshape(REC_GROUPS, SUBLANES, D_RNN)
              + a_sc[q].reshape(REC_GROUPS, SUBLANES, D_RNN) * enter[None]).reshape(ts, D_RNN)
        y = hs * jax.nn.gelu(gate)
        y_t = jnp.dot(perm_t_ref[...], y.astype(BF16), preferred_element_type=F32).astype(BF16)
        return jnp.dot(y_t, w_out_ref[...], preferred_element_type=F32)

    def finish(q, out):
        z = ALPHA * x_ref[q] + out
        xn = _layer_norm_rows(z, g_ref[...], b_ref[...])
        o_ref[q] = xn
        opk_ref[q] = _pack_row_halves(xn)
        lg_ref[q] = _router_logits_t(xn, wrt_ref[...], brt_ref[...])

    assert REC_NSEQ == 2
    gate_a, xr_a = project(0)
    conv_gates(0, xr_a)
    gate_b, xr_b = project(1)
    conv_gates(1, xr_b)
    seg_a = segment_scan(0)
    seg_b = segment_scan(1)
    out_a = recur_out(0, gate_a, seg_a)
    out_b = recur_out(1, gate_b, seg_b)
    finish(0, out_a)
    finish(1, out_b)


def _band_gate_weights(w_r, w_i):
    spread = jnp.asarray(np.tile(np.eye(LRU_BLOCK_W, dtype=np.float32), (1, LRU_BLOCKS)), BF16)
    blk = np.arange(D_RNN) // LRU_BLOCK_W
    on_diag = jnp.asarray(blk[:, None] == blk[None, :])

    def dense(w):
        rows = w.reshape(D_RNN, LRU_BLOCK_W).astype(BF16)
        return jnp.where(on_diag, jnp.dot(rows, spread, preferred_element_type=F32), 0.0)

    wr, wi = dense(w_r), dense(w_i)
    tiles = []
    for j in range(N_GATE_TILES):
        ws = GATE_WIN_STARTS[j]
        cs = j * GATE_TILE
        lo_blk = cs // LRU_BLOCK_W
        hi_blk = (cs + GATE_TILE - 1) // LRU_BLOCK_W
        assert ws <= lo_blk * LRU_BLOCK_W and (hi_blk + 1) * LRU_BLOCK_W <= ws + GATE_WIN
        tiles.append(jnp.concatenate([wr[ws:ws + GATE_WIN, cs:cs + GATE_TILE],
                                      wi[ws:ws + GATE_WIN, cs:cs + GATE_TILE]], axis=1))
    return jnp.stack(tiles).astype(BF16)


def _rglru_layer(x, w_in, conv_w, conv_b, w_r, b_r, w_i, b_i, lam, w_out, ln_g, ln_b, router_w, router_b):
    B, S, D = x.shape
    wg = _band_gate_weights(w_r, w_i)
    rho = np.arange(REC_TS)
    perm_np = np.zeros((REC_TS, REC_TS), np.float32)
    perm_np[rho, (rho % SUBLANES) * REC_GROUPS + rho // SUBLANES] = 1.0
    perm = jnp.asarray(perm_np, BF16)
    perm_t = jnp.asarray(perm_np.T, BF16)
    row = lambda v: v.reshape(1, -1)
    const = lambda shape: pl.BlockSpec(shape, lambda b, s: (0,) * len(shape))
    tile = lambda w: pl.BlockSpec((REC_NSEQ, REC_TS, w), lambda b, s: (b, s, 0))
    halo = (CONV_W - 1) * SUBLANES
    return pl.pallas_call(
        _rglru_kernel,
        grid=(B // REC_NSEQ, S // REC_TS),
        in_specs=[
            tile(D),
            const((REC_TS, REC_TS)),
            const((REC_TS, REC_TS)),
            const((D, 2 * D_RNN)),
            const((CONV_W, D_RNN)),
            const((1, D_RNN)),
            const((N_GATE_TILES, GATE_WIN, 2 * GATE_TILE)),
            const((1, D_RNN)),
            const((1, D_RNN)),
            const((1, D_RNN)),
            const((D_RNN, D)),
            const((1, D)),
            const((1, D)),
            const((2 * ROUTE_ROWS, D)),
            const((ROUTE_ROWS, 1)),
        ],
        out_specs=[tile(D), tile(D_HALF),
                   pl.BlockSpec((REC_NSEQ, ROUTE_ROWS, REC_TS), lambda b, s: (b, 0, s))],
        out_shape=[jax.ShapeDtypeStruct((B, S, D), F32), jax.ShapeDtypeStruct((B, S, D_HALF), U32),
                   jax.ShapeDtypeStruct((B, ROUTE_ROWS, S), F32)],
        scratch_shapes=[
            pltpu.VMEM((REC_NSEQ, halo + REC_TS, D_RNN), F32),
            pltpu.VMEM((REC_NSEQ, halo, D_RNN), F32),
            pltpu.VMEM((REC_NSEQ, REC_TS, D_RNN), F32),
            pltpu.VMEM((REC_NSEQ, REC_TS, D_RNN), F32),
            pltpu.VMEM((REC_NSEQ, 1, D_RNN), F32),
        ],
        compiler_params=pltpu.CompilerParams(
            dimension_semantics=("arbitrary", "arbitrary"), vmem_limit_bytes=VMEM_LIMIT),
        name="rglru_ln",
    )(x, perm, perm_t, w_in.astype(BF16), conv_w, row(conv_b), wg, row(b_r), row(b_i), row(lam), w_out.astype(BF16),
      row(ln_g), row(ln_b), router_w, router_b)


def _router_kernel(logits_ref, tri_ref, idx_ref, gate_ref, cnt_ref, base_sc):
    step = pl.program_id(0)
    tr = ROUTE_T

    @pl.when(step == 0)
    def _():
        base_sc[...] = jnp.zeros((N_EXPERTS, 1), F32)

    row8 = lax.broadcasted_iota(jnp.int32, (SUBLANES, tr), 0).astype(F32)
    rowe = lax.broadcasted_iota(jnp.int32, (N_EXPERTS, tr), 0).astype(F32)
    neg_inf = -jnp.inf
    for sub in range(ROUTE_STEP // ROUTE_T):
        cols = slice(sub * tr, (sub + 1) * tr)
        _route_tile(logits_ref[0, :, cols], row8, rowe, neg_inf, tri_ref, idx_ref, gate_ref, base_sc, cols)
    cnt_ref[...] = jnp.broadcast_to(base_sc[...], (N_EXPERTS, LANES)).astype(jnp.int32)


def _route_tile(logits, row8, rowe, neg_inf, tri_ref, idx_ref, gate_ref, base_sc, cols):
    tr = ROUTE_T
    g = jnp.where(row8 < N_GROUPS, logits[N_EXPERTS:N_EXPERTS + SUBLANES, :], neg_inf)
    gmax = jnp.max(g, axis=0, keepdims=True)
    gidx = jnp.min(jnp.where(g == gmax, row8, SUBLANES), axis=0, keepdims=True)
    g_gate = 1.0 / jnp.sum(jnp.exp(g - gmax), axis=0, keepdims=True)

    esel = logits[0:EXPERTS_PER_GROUP, :]
    for grp in range(1, N_GROUPS):
        esel = jnp.where(gidx == grp, logits[grp * EXPERTS_PER_GROUP:(grp + 1) * EXPERTS_PER_GROUP, :], esel)
    v1 = jnp.max(esel, axis=0, keepdims=True)
    i1 = jnp.min(jnp.where(esel == v1, row8, SUBLANES), axis=0, keepdims=True)
    esel2 = jnp.where(row8 == i1, neg_inf, esel)
    v2 = jnp.max(esel2, axis=0, keepdims=True)
    i2 = jnp.min(jnp.where(esel2 == v2, row8, SUBLANES), axis=0, keepdims=True)
    e21 = jnp.exp(v2 - v1)
    inv = 1.0 / (1.0 + e21)
    gate1 = inv * g_gate
    gate2 = e21 * inv * g_gate
    e1 = gidx * EXPERTS_PER_GROUP + i1
    e2 = gidx * EXPERTS_PER_GROUP + i2

    hit1 = rowe == e1
    hit2 = rowe == e2
    member = jnp.where(hit1, 1.0, jnp.where(hit2, 1.0, 0.0))
    before = jnp.dot(member.astype(BF16), tri_ref[...], preferred_element_type=F32) + base_sc[...]
    rank1 = jnp.sum(jnp.where(hit1, before, 0.0), axis=0, keepdims=True)
    rank2 = jnp.sum(jnp.where(hit2, before, 0.0), axis=0, keepdims=True)
    base_sc[...] = base_sc[...] + jnp.sum(member, axis=1, keepdims=True)

    zi = jnp.zeros((1, tr), jnp.int32)
    idx_ref[:, cols] = jnp.concatenate(
        [e1.astype(jnp.int32), e2.astype(jnp.int32), rank1.astype(jnp.int32), rank2.astype(jnp.int32),
         zi, zi, zi, zi], axis=0)
    zf = jnp.zeros((1, tr), F32)
    gate_ref[:, cols] = jnp.concatenate([gate1, gate2, zf, zf, zf, zf, zf, zf], axis=0)


def _router_weights(w_rg, b_rg, w_re, b_re):
    D = w_rg.shape[0]
    pad_rows = ROUTE_ROWS - N_EXPERTS - N_GROUPS
    w = jnp.concatenate([w_re.T, w_rg.T, jnp.zeros((pad_rows, D), F32)], axis=0)
    whi = w.astype(BF16)
    wlo = (w - whi.astype(F32)).astype(BF16)
    w_split = jnp.concatenate([whi, wlo], axis=0)
    bias = jnp.concatenate([b_re, b_rg, jnp.zeros((pad_rows,), F32)]).reshape(ROUTE_ROWS, 1)
    return w_split, bias


def _router(logits_t):
    B, _, S = logits_t.shape
    T = B * S
    per_row = S // ROUTE_STEP
    tri = jnp.asarray(np.triu(np.ones((ROUTE_T, ROUTE_T), np.float32), 1), BF16)
    const = lambda shape: pl.BlockSpec(shape, lambda i: (0,) * len(shape))
    return pl.pallas_call(
        _router_kernel,
        grid=(T // ROUTE_STEP,),
        in_specs=[
            pl.BlockSpec((1, ROUTE_ROWS, ROUTE_STEP), lambda i: (i // per_row, 0, i % per_row)),
            const((ROUTE_T, ROUTE_T)),
        ],
        out_specs=[
            pl.BlockSpec((SUBLANES, ROUTE_STEP), lambda i: (0, i)),
            pl.BlockSpec((SUBLANES, ROUTE_STEP), lambda i: (0, i)),
            const((N_EXPERTS, LANES)),
        ],
        out_shape=[
            jax.ShapeDtypeStruct((SUBLANES, T), jnp.int32),
            jax.ShapeDtypeStruct((SUBLANES, T), F32),
            jax.ShapeDtypeStruct((N_EXPERTS, LANES), jnp.int32),
        ],
        scratch_shapes=[pltpu.VMEM((N_EXPERTS, 1), F32)],
        compiler_params=pltpu.CompilerParams(
            dimension_semantics=("arbitrary",), vmem_limit_bytes=VMEM_LIMIT),
        name="router",
    )(logits_t, tri)


def _sc_mesh():
    return plsc.VectorSubcoreMesh(core_axis_name="c", subcore_axis_name="s",
                                  num_cores=SC_CORES, num_subcores=SC_SUBCORES)


def _sc_worker_id():
    return lax.axis_index("s") * SC_CORES + lax.axis_index("c")


def _sc_scratch(n_win, width):
    return [
        pltpu.VMEM((n_win, SC_WIN), jnp.int32),
        pltpu.VMEM((n_win, SC_WIN), jnp.int32),
        pltpu.VMEM((2, SC_WIN, width), U32),
        pltpu.SemaphoreType.DMA((2,)),
        pltpu.SemaphoreType.DMA((2,)),
    ]


def _sc_dispatch(rows, idx1, idx2, n_rows):
    _, width = rows.shape
    _, n_win, _ = idx1.shape

    @functools.partial(
        pl.kernel, mesh=_sc_mesh(), out_type=jax.ShapeDtypeStruct((n_rows, width), rows.dtype),
        scratch_types=_sc_scratch(n_win, width), name="sc_dispatch")
    def run(rows_hbm, i1_hbm, i2_hbm, o_hbm, i1_v, i2_v, buf, rsem, wsem):
        wid = _sc_worker_id()
        base = wid * (n_win * SC_WIN)
        pltpu.sync_copy(i1_hbm.at[wid], i1_v)
        pltpu.sync_copy(i2_hbm.at[wid], i2_v)

        def read(j):
            return pltpu.async_copy(rows_hbm.at[pl.ds(base + j * SC_WIN, SC_WIN)], buf.at[j % 2], rsem.at[j % 2])

        reads = {0: read(0)}
        writes = {}
        for j in range(n_win):
            if j + 1 < n_win:
                for d in writes.pop(j - 1, ()):
                    d.wait()
                reads[j + 1] = read(j + 1)
            reads.pop(j).wait()
            writes[j] = (pltpu.async_copy(buf.at[j % 2], o_hbm.at[i1_v.at[j]], wsem.at[j % 2]),
                         pltpu.async_copy(buf.at[j % 2], o_hbm.at[i2_v.at[j]], wsem.at[j % 2]))
        for j in sorted(writes):
            for d in writes[j]:
                d.wait()

    return run(rows, idx1, idx2)


def _sc_gather_pair(table, idx1, idx2):
    _, width = table.shape
    _, n_win, _ = idx1.shape
    n_tok = SC_WORKERS * n_win * SC_WIN
    out_t = jax.ShapeDtypeStruct((n_tok, width), table.dtype)

    @functools.partial(
        pl.kernel, mesh=_sc_mesh(), out_type=(out_t, out_t),
        scratch_types=_sc_scratch(n_win, width), name="sc_combine_gather")
    def run(table_hbm, i1_hbm, i2_hbm, o1_hbm, o2_hbm, i1_v, i2_v, buf, gsem, wsem):
        wid = _sc_worker_id()
        base = wid * (n_win * SC_WIN)
        pltpu.sync_copy(i1_hbm.at[wid], i1_v)
        pltpu.sync_copy(i2_hbm.at[wid], i2_v)
        work = [(i1_v, o1_hbm, j) for j in range(n_win)] + [(i2_v, o2_hbm, j) for j in range(n_win)]

        def gather(t):
            iv, _, j = work[t]
            return pltpu.async_copy(table_hbm.at[iv.at[j]], buf.at[t % 2], gsem.at[t % 2])

        def put(t):
            _, oh, j = work[t]
            return pltpu.async_copy(buf.at[t % 2], oh.at[pl.ds(base + j * SC_WIN, SC_WIN)], wsem.at[t % 2])

        gathers = {0: gather(0)}
        puts = {}
        for t in range(len(work)):
            if t + 1 < len(work):
                if t - 1 in puts:
                    puts.pop(t - 1).wait()
                gathers[t + 1] = gather(t + 1)
            gathers.pop(t).wait()
            puts[t] = put(t)
        for t in sorted(puts):
            puts[t].wait()

    return run(table, idx1, idx2)


def _moe_kernel(layer, be_ref, slot_ref, nxt_ref, valid_ref, x_ref, w1_hbm, w3_hbm, w2_hbm, o_ref,
                w1_st, w3_st, w2_st, w1_sc, w3_sc, w2_sc, sems):
    i = pl.program_id(0)
    expert = be_ref[i]
    slot = slot_ref[i]
    new_expert = jnp.logical_or(i == 0, expert != be_ref[jnp.maximum(i - 1, 0)])

    def weight_copies(e, sl):
        return [pltpu.make_async_copy(hbm.at[layer, e], stage.at[sl], sems.at[k, sl])
                for k, (hbm, stage) in enumerate(((w1_hbm, w1_st), (w3_hbm, w3_st), (w2_hbm, w2_st)))]

    @pl.when(i == 0)
    def _():
        for cp in weight_copies(expert, slot):
            cp.start()

    @pl.when(new_expert)
    def _():
        for cp in weight_copies(expert, slot):
            cp.wait()
        w1_sc[...] = w1_st[slot].astype(BF16)
        w3_sc[...] = w3_st[slot].astype(BF16)
        w2_sc[...] = w2_st[slot].astype(BF16)
        nxt = nxt_ref[i]

        @pl.when(nxt >= 0)
        def _():
            for cp in weight_copies(nxt, 1 - slot):
                cp.start()

    def up(rows):
        xa, xb = _unpack_bf16_pair(x_ref[rows, :])
        xa = xa.astype(BF16)
        xb = xb.astype(BF16)
        h1 = (jnp.dot(xa, w1_sc[:D_HALF], preferred_element_type=F32)
              + jnp.dot(xb, w1_sc[D_HALF:], preferred_element_type=F32))
        h3 = (jnp.dot(xa, w3_sc[:D_HALF], preferred_element_type=F32)
              + jnp.dot(xb, w3_sc[D_HALF:], preferred_element_type=F32))
        return h1, h3

    def down(rows, h1, h3):
        hdn = (jax.nn.silu(h1) * h3).astype(BF16)
        y = jnp.dot(hdn, w2_sc[...], preferred_element_type=F32)
        o_ref[rows, :] = _pack_row_halves(y)

    n_sub = MOE_BM // MOE_SUB
    subs = [slice(k * MOE_SUB, (k + 1) * MOE_SUB) for k in range(n_sub)]
    valid = valid_ref[i]
    chains = (valid + (MOE_SUB - 1)) // MOE_SUB

    for live in range(n_sub + 1):
        @pl.when(chains == live)
        def _(live=live):
            ups = {}
            if live:
                ups[0] = up(subs[0])
            for k in range(live):
                if k + 1 < live:
                    ups[k + 1] = up(subs[k + 1])
                down(subs[k], *ups.pop(k))
            if live < n_sub:
                o_ref[live * MOE_SUB:, :] = jnp.zeros((MOE_BM - live * MOE_SUB, D_HALF), o_ref.dtype)


def _moe_blocks(xbuf, block_e, valid, w1, w3, w2, layer):
    n_rows, _ = xbuf.shape
    D = D_MODEL
    n_blocks = n_rows // MOE_BM
    pos = jnp.arange(n_blocks, dtype=jnp.int32)
    is_new = jnp.concatenate([jnp.ones((1,), bool), block_e[1:] != block_e[:-1]])
    slot = ((jnp.cumsum(is_new.astype(jnp.int32)) - 1) % 2).astype(jnp.int32)
    change_pos = jnp.where(is_new, pos, n_blocks)
    next_change = jnp.concatenate([lax.cummin(change_pos, reverse=True)[1:], jnp.full((1,), n_blocks, jnp.int32)])
    nxt = jnp.where(next_change < n_blocks, block_e[jnp.minimum(next_change, n_blocks - 1)], -1).astype(jnp.int32)
    rows = lambda i, be, sl, nx, nu: (i, 0)
    grid_spec = pltpu.PrefetchScalarGridSpec(
        num_scalar_prefetch=4,
        grid=(n_blocks,),
        in_specs=[
            pl.BlockSpec((MOE_BM, D_HALF), rows),
            pl.BlockSpec(memory_space=pl.ANY),
            pl.BlockSpec(memory_space=pl.ANY),
            pl.BlockSpec(memory_space=pl.ANY),
        ],
        out_specs=pl.BlockSpec((MOE_BM, D_HALF), rows),
        scratch_shapes=[
            pltpu.VMEM((2, D, D_EXPERT), F32),
            pltpu.VMEM((2, D, D_EXPERT), F32),
            pltpu.VMEM((2, D_EXPERT, D), F32),
            pltpu.VMEM((D, D_EXPERT), BF16),
            pltpu.VMEM((D, D_EXPERT), BF16),
            pltpu.VMEM((D_EXPERT, D), BF16),
            pltpu.SemaphoreType.DMA((3, 2)),
        ],
    )
    return pl.pallas_call(
        functools.partial(_moe_kernel, layer),
        grid_spec=grid_spec,
        out_shape=jax.ShapeDtypeStruct((n_rows, D_HALF), U32),
        compiler_params=pltpu.CompilerParams(
            dimension_semantics=("arbitrary",), vmem_limit_bytes=VMEM_LIMIT),
        name="moe_experts",
    )(block_e, slot, nxt, valid, xbuf, w1, w3, w2)


def _moe_combine_norm(x, y1, y2, gate_rows, g, b):
    n = x.shape[0]
    pad = jnp.zeros((LANES - SUBLANES, n), F32)
    gates = jnp.concatenate([gate_rows, pad], axis=0).T
    g1 = gates[:, 0:1]
    g2 = gates[:, 1:2]
    a1, b1 = _unpack_bf16_pair(y1)
    a2, b2 = _unpack_bf16_pair(y2)
    f = jnp.concatenate([g1 * a1 + g2 * a2, g1 * b1 + g2 * b2], axis=1)
    return _layer_norm_rows(ALPHA * x + f, g, b)


def _combine_kernel(x_ref, y1_ref, y2_ref, gates_ref, g_ref, b_ref, o_ref):
    o_ref[...] = _moe_combine_norm(x_ref[...], y1_ref[...], y2_ref[...], gates_ref[...], g_ref[...], b_ref[...])


def _combine_part_kernel(x_ref, y1_ref, y2_ref, gates_ref, g_ref, b_ref, prev_ref, o_ref):
    del prev_ref
    _combine_kernel(x_ref, y1_ref, y2_ref, gates_ref, g_ref, b_ref, o_ref)


def _combine_ln(xf, y_parts, gates, ln_g, ln_b):
    T, D = xf.shape
    n_parts = len(y_parts)
    steps = T // (COMB_T * n_parts)
    const = lambda shape: pl.BlockSpec(shape, lambda i: (0,) * len(shape))
    out = None
    for part, (y1, y2) in enumerate(y_parts):
        off = part * steps
        glob = lambda w, off=off: pl.BlockSpec((COMB_T, w), lambda i: (i + off, 0))
        local = lambda w: pl.BlockSpec((COMB_T, w), lambda i: (i, 0))
        in_specs = [glob(D), local(D_HALF), local(D_HALF),
                    pl.BlockSpec((SUBLANES, COMB_T), lambda i, off=off: (0, i + off)), const((1, D)), const((1, D))]
        args = [xf, y1, y2, gates, ln_g.reshape(1, D), ln_b.reshape(1, D)]
        if out is not None:
            in_specs.append(pl.BlockSpec(memory_space=pl.ANY))
            args.append(out)
        out = pl.pallas_call(
            _combine_kernel if out is None else _combine_part_kernel,
            grid=(steps,),
            in_specs=in_specs,
            out_specs=glob(D),
            out_shape=jax.ShapeDtypeStruct((T, D), F32),
            input_output_aliases={} if out is None else {len(args) - 1: 0},
            compiler_params=pltpu.CompilerParams(
                dimension_semantics=("arbitrary",), vmem_limit_bytes=VMEM_LIMIT),
            name="moe_combine_ln",
        )(*args)
    return out


def _hier_moe(xpk, logits_t, w1, w3, w2, layer, gather_parts):
    B, S, _ = xpk.shape
    D = D_MODEL
    T = B * S
    idx, gates, cnt = _router(logits_t)
    counts = cnt[:, 0]
    padded = ((counts + MOE_BM - 1) // MOE_BM) * MOE_BM
    pends = jnp.cumsum(padded)
    pstarts = pends - padded
    experts = jnp.arange(N_EXPERTS, dtype=jnp.int32)[:, None]

    def dest(e_row, rank_row):
        return jnp.sum(jnp.where(e_row[None, :] == experts, pstarts[:, None], 0), axis=0) + rank_row

    n_win = T // (SC_WORKERS * SC_WIN)
    dest1 = dest(idx[0], idx[2]).reshape(SC_WORKERS, n_win, SC_WIN)
    dest2 = dest(idx[1], idx[3]).reshape(SC_WORKERS, n_win, SC_WIN)
    n_blocks = -(-(T * TOP_K + N_EXPERTS * (MOE_BM - 1)) // MOE_BM)
    n_rows = n_blocks * MOE_BM
    block_start = jnp.arange(n_blocks, dtype=jnp.int32) * MOE_BM
    block_e = jnp.minimum(jnp.sum(block_start[:, None] >= pends[None, :], axis=1), N_EXPERTS - 1).astype(jnp.int32)
    of_block = block_e[:, None] == experts[:, 0][None, :]
    pick = lambda v: jnp.sum(jnp.where(of_block, v[None, :], 0), axis=1)
    valid = jnp.clip(pick(counts) - (block_start - pick(pstarts)), 0, MOE_BM).astype(jnp.int32)
    xbuf = _sc_dispatch(xpk.reshape(T, D_HALF), dest1, dest2, n_rows)
    ybuf = _moe_blocks(xbuf, block_e, valid, w1, w3, w2, layer)
    part_shape = (gather_parts, SC_WORKERS, n_win // gather_parts, SC_WIN)
    d1, d2 = dest1.reshape(part_shape), dest2.reshape(part_shape)
    y_parts = [_sc_gather_pair(ybuf, d1[p], d2[p]) for p in range(gather_parts)]
    return y_parts, gates


def _att_head_order():
    order = []
    for p in range(N_HEADS // 2):
        jj, m = divmod(p, 4)
        order += [8 * jj + m, 8 * jj + 4 + m]
    return order


ATT_HEAD_ORDER = _att_head_order()


def _attn_kernel(xprev_ref, y1_ref, y2_ref, gates_a_ref, gates_b_ref, g_prev_ref, b_prev_ref, wqkv_ref, bias_ref,
                 sink_ref, wo_ref, g_ref, b_ref, wrt_ref, brt_ref, o_ref, opk_ref, lg_ref,
                 kv_ext, o_sc, s_sc0, s_sc1, p_sc0, p_sc1):
    s = pl.program_id(1)
    tq = ATT_TQ
    s_bufs = (s_sc0, s_sc1)
    p_bufs = (p_sc0, p_sc1)
    gate_refs = (gates_a_ref, gates_b_ref)
    assert ATT_NSEQ == len(gate_refs)

    @pl.when(s == 0)
    def _():
        kv_ext[:, 0:WINDOW, :] = jnp.zeros((ATT_NSEQ, WINDOW, 2 * KV_DIM), BF16)

    def layer_input(sq):
        return _moe_combine_norm(xprev_ref[sq], y1_ref[sq], y2_ref[sq], gate_refs[sq][...],
                                 g_prev_ref[...], b_prev_ref[...])

    def project_qkv(sq, x):
        qkv = jnp.dot(x.astype(BF16), wqkv_ref[...], preferred_element_type=F32)
        kv_ext[sq, WINDOW:WINDOW + tq, :] = qkv[:, Q_DIM:].astype(BF16)
        return (qkv[:, :Q_DIM] * (HEAD_DIM ** -0.5 * LOG2E)).astype(BF16)

    lane = lax.broadcasted_iota(jnp.int32, (WINDOW, LANES), 1)
    low = lane < HEAD_DIM
    sub = lax.broadcasted_iota(jnp.int32, (LANES, WINDOW), 0)
    top = sub < HEAD_DIM
    first = jnp.where(s == 0, 1, 0)
    nt = (((1,), (1,)), ((), ()))
    zero = jnp.zeros((), BF16)

    tiles = [(sq, n, j) for n in range(ATT_NB) for j in range(2) for sq in range(ATT_NSEQ)]

    def scores(t, qs_all):
        sq, n, j = tiles[t]
        q = qs_all[sq]
        r0 = n * WINDOW
        k_tile = kv_ext[sq, r0:r0 + 2 * WINDOW, j * LANES:(j + 1) * LANES]
        parts = []
        for m in range(4):
            p = 4 * j + m
            qt = q[r0:r0 + WINDOW, p * LANES:(p + 1) * LANES]
            parts.append(jnp.where(low, qt, zero))
            parts.append(jnp.where(low, zero, qt))
        qs = jnp.concatenate(parts, axis=0)
        bias_sel = first if n == 0 else 0
        s_bufs[t % 2][...] = (lax.dot_general(k_tile, qs, nt, preferred_element_type=F32)
                              + bias_ref[bias_sel, j])

    def softmax_pv(t):
        sq, n, j = tiles[t]
        r0 = n * WINDOW
        s_sc = s_bufs[t % 2]
        p_sc = p_bufs[t % 2]
        inv_l = []
        for h in range(8):
            hc = slice(h * WINDOW, (h + 1) * WINDOW)
            sink = sink_ref[8 * j + h] * LOG2E
            mx = jnp.maximum(jnp.max(s_sc[:, hc], axis=0, keepdims=True), sink)
            pr = jnp.exp2(s_sc[:, hc] - mx)
            p_sc[:, hc] = pr.astype(BF16)
            inv_l.append(1.0 / (jnp.sum(pr, axis=0, keepdims=True) + jnp.exp2(sink - mx)))
        v_tile = kv_ext[sq, r0:r0 + 2 * WINDOW, KV_DIM + j * LANES:KV_DIM + (j + 1) * LANES]
        v_t = v_tile.astype(F32).T.astype(BF16)
        ov = jnp.dot(v_t, p_sc[...], preferred_element_type=F32)
        for m in range(4):
            p = 4 * j + m
            o_even = ov[:, (2 * m) * WINDOW:(2 * m + 1) * WINDOW] * inv_l[2 * m]
            o_odd = ov[:, (2 * m + 1) * WINDOW:(2 * m + 2) * WINDOW] * inv_l[2 * m + 1]
            o_sc[sq, p * LANES:(p + 1) * LANES, r0:r0 + WINDOW] = jnp.where(top, o_even, o_odd).astype(BF16)

    def project_out(sq):
        return jnp.dot(wo_ref[...], o_sc[sq], preferred_element_type=F32)

    def finish(sq, x, out_t):
        xn = _layer_norm_rows(ALPHA * x + out_t.T, g_ref[...], b_ref[...])
        o_ref[sq] = xn
        opk_ref[sq] = _pack_row_halves(xn)
        lg_ref[sq] = _router_logits_t(xn, wrt_ref[...], brt_ref[...])

    xs = [layer_input(0)]
    qs_all = [project_qkv(0, xs[0])]
    xs.append(layer_input(1))
    qs_all.append(project_qkv(1, xs[1]))
    scores(0, qs_all)
    for t in range(len(tiles)):
        if t + 1 < len(tiles):
            scores(t + 1, qs_all)
        softmax_pv(t)
    kv_ext[:, 0:WINDOW, :] = kv_ext[:, tq:tq + WINDOW, :]
    outs_t = [project_out(sq) for sq in range(ATT_NSEQ)]
    for sq in range(ATT_NSEQ):
        finish(sq, xs[sq], outs_t[sq])


def _attn_bias():
    qi = np.arange(WINDOW)[:, None]
    sj = np.arange(2 * WINDOW)[None, :]
    dist = qi - sj + WINDOW
    valid = (dist >= 0) & (dist < WINDOW)
    slopes = 2.0 ** (-8.0 * np.arange(1, N_HEADS + 1, dtype=np.float32) / N_HEADS)
    slopes = slopes.astype(np.float32)[ATT_HEAD_ORDER]
    sb = -(slopes[:, None, None] * dist.astype(np.float32)[None])
    later = np.where(valid[None], sb, -np.inf)
    first = np.where((valid & (sj >= WINDOW))[None], sb, -np.inf)
    bias = np.stack([later, first]).astype(np.float32) * np.float32(LOG2E)
    bias = bias.reshape(2, 2, 8, WINDOW, 2 * WINDOW).transpose(0, 1, 4, 2, 3).reshape(2, 2, 2 * WINDOW, 8 * WINDOW)
    return jnp.asarray(np.ascontiguousarray(bias))


ATT_N_INPUTS = 15


def _attn_part_kernel(*refs):
    _attn_kernel(*refs[:ATT_N_INPUTS], *refs[ATT_N_INPUTS + 3:])


def _attn_layer(x_prev, y_parts, gates, g_prev, b_prev, w_qkv, sinks, w_o, ln_g, ln_b, router_w, router_b):
    B, S, D = x_prev.shape
    steps = S // ATT_TQ
    assert ATT_HEAD_ORDER == list(np.arange(N_HEADS).reshape(2, 2, 4).transpose(0, 2, 1).reshape(-1))
    wq = w_qkv[:, :Q_DIM].reshape(D, 2, 2, 4, HEAD_DIM).transpose(0, 1, 3, 2, 4).reshape(D, Q_DIM)
    wqkv = jnp.concatenate([wq, w_qkv[:, Q_DIM:]], axis=1).astype(BF16)
    wo_t = w_o.reshape(2, 2, 4, HEAD_DIM, D).transpose(0, 2, 1, 3, 4).reshape(Q_DIM, D).T.astype(BF16)
    sink = sinks.reshape(2, 2, 4).transpose(0, 2, 1).reshape(N_HEADS, 1, 1)
    bias = _attn_bias()
    const = lambda shape: pl.BlockSpec(shape, lambda b, s: (0,) * len(shape))
    n_parts = len(y_parts)
    rows_per_part = B // n_parts
    pairs = rows_per_part // ATT_NSEQ
    outs = None
    for part, (y1, y2) in enumerate(y_parts):
        off = part * pairs
        glob = lambda w, off=off: pl.BlockSpec((ATT_NSEQ, ATT_TQ, w), lambda b, s: (b + off, s, 0))
        local = lambda w: pl.BlockSpec((ATT_NSEQ, ATT_TQ, w), lambda b, s: (b, s, 0))
        gate_rows = lambda sq, off=off: pl.BlockSpec(
            (SUBLANES, ATT_TQ), lambda b, s: (0, (ATT_NSEQ * (b + off) + sq) * steps + s))
        in_specs = [
            glob(D),
            local(D_HALF),
            local(D_HALF),
            gate_rows(0),
            gate_rows(1),
            const((1, D)),
            const((1, D)),
            const((D, Q_DIM + 2 * KV_DIM)),
            const((2, 2, 2 * WINDOW, 8 * WINDOW)),
            const((N_HEADS, 1, 1)),
            const((D, Q_DIM)),
            const((1, D)),
            const((1, D)),
            const((2 * ROUTE_ROWS, D)),
            const((ROUTE_ROWS, 1)),
        ]
        args = [x_prev, y1.reshape(rows_per_part, S, D_HALF), y2.reshape(rows_per_part, S, D_HALF), gates, gates,
                g_prev.reshape(1, D), b_prev.reshape(1, D), wqkv, bias, sink, wo_t,
                ln_g.reshape(1, D), ln_b.reshape(1, D), router_w, router_b]
        aliases = {}
        body = _attn_kernel
        if outs is not None:
            aliases = {len(args) + k: k for k in range(3)}
            in_specs += [pl.BlockSpec(memory_space=pl.ANY)] * 3
            args += list(outs)
            body = _attn_part_kernel
        outs = pl.pallas_call(
            body,
            grid=(pairs, steps),
            in_specs=in_specs,
            out_specs=[glob(D), glob(D_HALF),
                       pl.BlockSpec((ATT_NSEQ, ROUTE_ROWS, ATT_TQ), lambda b, s, off=off: (b + off, 0, s))],
            out_shape=[jax.ShapeDtypeStruct((B, S, D), F32), jax.ShapeDtypeStruct((B, S, D_HALF), U32),
                       jax.ShapeDtypeStruct((B, ROUTE_ROWS, S), F32)],
            input_output_aliases=aliases,
            scratch_shapes=[
                pltpu.VMEM((ATT_NSEQ, ATT_TQ + WINDOW, 2 * KV_DIM), BF16),
                pltpu.VMEM((ATT_NSEQ, Q_DIM, ATT_TQ), BF16),
                pltpu.VMEM((2 * WINDOW, 8 * WINDOW), F32),
                pltpu.VMEM((2 * WINDOW, 8 * WINDOW), F32),
                pltpu.VMEM((2 * WINDOW, 8 * WINDOW), BF16),
                pltpu.VMEM((2 * WINDOW, 8 * WINDOW), BF16),
            ],
            compiler_params=pltpu.CompilerParams(
                dimension_semantics=("arbitrary", "arbitrary"), vmem_limit_bytes=VMEM_LIMIT),
            name="swa_attn_ln",
        )(*args)
    return outs


def kernel(x, rec_w_in, rec_conv_w, rec_conv_b, rec_w_r, rec_b_r, rec_w_i, rec_b_i, rec_lambda, rec_w_out,
           att_w_qkv, att_sinks, att_w_o, moe_w_group, moe_b_group, moe_w_expert, moe_b_expert,
           moe_w1, moe_w3, moe_w2, ln_g, ln_b):
    assert DEPTH == 2
    B, S, D = x.shape

    router = [_router_weights(moe_w_group[layer], moe_b_group[layer], moe_w_expert[layer], moe_b_expert[layer])
              for layer in range(DEPTH)]

    x1, x1_pk, logits1 = _rglru_layer(x, rec_w_in[0], rec_conv_w[0], rec_conv_b[0], rec_w_r[0], rec_b_r[0],
                                      rec_w_i[0], rec_b_i[0], rec_lambda[0], rec_w_out[0], ln_g[0, 0], ln_b[0, 0],
                                      *router[0])
    y_parts, gates = _hier_moe(x1_pk, logits1, moe_w1, moe_w3, moe_w2, 0, gather_parts=ATT_PARTS)
    x3, x3_pk, logits3 = _attn_layer(x1, y_parts, gates, ln_g[0, 1], ln_b[0, 1], att_w_qkv[0], att_sinks[0],
                                     att_w_o[0], ln_g[1, 0], ln_b[1, 0], *router[1])
    y_parts, gates = _hier_moe(x3_pk, logits3, moe_w1, moe_w3, moe_w2, 1, gather_parts=1)
    out = _combine_ln(x3.reshape(B * S, D), y_parts, gates, ln_g[1, 1], ln_b[1, 1])
    return out.reshape(B, S, D)
```

```python
import functools

import jax
import jax.numpy as jnp
import numpy as np
from jax import lax
from jax.experimental import pallas as pl
from jax.experimental.pallas import tpu as pltpu
from jax.experimental.pallas import tpu_sc as plsc

F32 = jnp.float32
BF16 = jnp.bfloat16
U32 = jnp.uint32

D_MODEL = 1024
DEPTH = 2
D_RNN = 1280
LRU_BLOCKS = 16
LRU_BLOCK_W = D_RNN // LRU_BLOCKS
CONV_W = 4
LRU_C = 8.0
N_HEADS = 16
N_KV_HEADS = 4
HEAD_DIM = 64
WINDOW = 128
Q_DIM = N_HEADS * HEAD_DIM
KV_DIM = N_KV_HEADS * HEAD_DIM
N_GROUPS = 4
EXPERTS_PER_GROUP = 8
N_EXPERTS = N_GROUPS * EXPERTS_PER_GROUP
TOP_K = 2
D_EXPERT = 512
ALPHA = (2 * DEPTH) ** 0.25
LN_EPS = 1e-5
LOG2E = 1.4426950408889634

LANES = 128
SUBLANES = 8
VMEM_LIMIT = 56 * 1024 * 1024

REC_TS = 256
REC_GROUPS = REC_TS // SUBLANES
REC_NSEQ = 2
GATE_TILE = 256
GATE_WIN = 512
GATE_WIN_STARTS = (0, 128, 384, 640, 768)
N_GATE_TILES = D_RNN // GATE_TILE

ROUTE_T = 512
ROUTE_STEP = 2048
ROUTE_ROWS = 40

MOE_BM = 1024
MOE_SUB = 256

ATT_TQ = 256
ATT_NB = ATT_TQ // WINDOW
ATT_NSEQ = 2
ATT_PARTS = 4

COMB_T = 1024

D_HALF = D_MODEL // 2

SC_CORES = 2
SC_SUBCORES = 16
SC_WORKERS = SC_CORES * SC_SUBCORES
SC_WIN = 64


def _layer_norm_rows(z, g, b):
    mu = jnp.mean(z, axis=-1, keepdims=True)
    zc = z - mu
    var = jnp.mean(zc * zc, axis=-1, keepdims=True)
    return zc * lax.rsqrt(var + LN_EPS) * g + b


def _pack_bf16_pair(a, b):
    ua = lax.bitcast_convert_type(a.astype(BF16).astype(F32), U32)
    ub = lax.bitcast_convert_type(b.astype(BF16).astype(F32), U32)
    return (ua >> 16) | (ub & jnp.uint32(0xFFFF0000))


def _unpack_bf16_pair(w):
    a = lax.bitcast_convert_type(w << 16, F32)
    b = lax.bitcast_convert_type(w & jnp.uint32(0xFFFF0000), F32)
    return a, b


def _pack_row_halves(x):
    return _pack_bf16_pair(x[:, :D_HALF], x[:, D_HALF:])


def _router_logits_t(x, w_split, bias):
    xhi = x.astype(BF16)
    xlo = (x - xhi.astype(F32)).astype(BF16)
    nt = (((1,), (1,)), ((), ()))
    both = lax.dot_general(w_split, xhi, nt, preferred_element_type=F32)
    low = lax.dot_general(w_split[:ROUTE_ROWS], xlo, nt, preferred_element_type=F32)
    return both[:ROUTE_ROWS] + both[ROUTE_ROWS:] + low + bias


def _rglru_kernel(x_ref, perm_ref, perm_t_ref, w_in_ref, convw_ref, convb_ref, wg_ref, br_ref, bi_ref, lam_ref,
                  w_out_ref, g_ref, b_ref, wrt_ref, brt_ref, o_ref, opk_ref, lg_ref,
                  xr_ext, tail_sc, a_sc, u_sc, h_carry):
    s = pl.program_id(1)
    ts = REC_TS
    halo = (CONV_W - 1) * SUBLANES
    seqs = range(REC_NSEQ)

    @pl.when(s == 0)
    def _():
        tail_sc[...] = jnp.zeros((REC_NSEQ, halo, D_RNN), F32)
        h_carry[...] = jnp.zeros((REC_NSEQ, 1, D_RNN), F32)

    row = lax.broadcasted_iota(jnp.int32, (SUBLANES, D_RNN), 0)
    nlam = -lam_ref[...]
    sp = jnp.maximum(nlam, 0.0) + jnp.log1p(jnp.exp(-jnp.abs(nlam)))
    log2a_scale = (-LRU_C * LOG2E) * sp

    def project(q):
        xp = jnp.dot(perm_ref[...], x_ref[q].astype(BF16), preferred_element_type=F32).astype(BF16)
        proj = jnp.dot(xp, w_in_ref[...], preferred_element_type=F32)
        return proj[:, :D_RNN], proj[:, D_RNN:]

    def conv_gates(q, xr):
        for k in range(CONV_W - 1):
            r0 = ts - halo + k * SUBLANES
            cur = xr[r0:r0 + SUBLANES, :]
            prev = tail_sc[q, k * SUBLANES:(k + 1) * SUBLANES, :]
            xr_ext[q, k * SUBLANES:(k + 1) * SUBLANES, :] = jnp.where(
                row == 0, pltpu.roll(prev, 1, axis=0), pltpu.roll(cur, 1, axis=0))
        tail_sc[q] = xr[ts - halo:, :]
        xr_ext[q, halo:halo + ts, :] = xr
        xc = convb_ref[...] + convw_ref[CONV_W - 1:CONV_W, :] * xr
        for k in range(CONV_W - 1):
            xc = xc + convw_ref[k:k + 1, :] * xr_ext[q, k * SUBLANES:k * SUBLANES + ts, :]
        xcb = xc.astype(BF16)
        pres = [jnp.dot(xcb[:, GATE_WIN_STARTS[j]:GATE_WIN_STARTS[j] + GATE_WIN], wg_ref[j],
                        preferred_element_type=F32) for j in range(N_GATE_TILES)]
        for j, pre in enumerate(pres):
            cs = j * GATE_TILE
            r = jax.nn.sigmoid(pre[:, :GATE_TILE] + br_ref[:, cs:cs + GATE_TILE])
            i = jax.nn.sigmoid(pre[:, GATE_TILE:] + bi_ref[:, cs:cs + GATE_TILE])
            a = jnp.exp2(r * log2a_scale[:, cs:cs + GATE_TILE])
            s1 = 1.0 - a * a
            mult = jnp.where(s1 > 0.0, s1 * lax.rsqrt(s1), 0.0)
            u = mult * (i * xc[:, cs:cs + GATE_TILE])
            a_sc[q, :, cs:cs + GATE_TILE] = a
            u_sc[q, :, cs:cs + GATE_TILE] = u

    def segment_scan(q):
        h = jnp.zeros((SUBLANES, D_RNN), F32)
        prod = jnp.ones((SUBLANES, D_RNN), F32)
        for gidx in range(REC_GROUPS):
            rows = slice(gidx * SUBLANES, (gidx + 1) * SUBLANES)
            a8 = a_sc[q, rows, :]
            h = a8 * h + u_sc[q, rows, :]
            prod = a8 * prod
            u_sc[q, rows, :] = h
            a_sc[q, rows, :] = prod
        return h, prod

    def recur_out(q, gate, seg):
        seg_h, seg_a = seg
        for d in (1, 2, 4):
            keep = row >= d
            a_sh = jnp.where(keep, pltpu.roll(seg_a, d, axis=0), 1.0)
            h_sh = jnp.where(keep, pltpu.roll(seg_h, d, axis=0), 0.0)
            seg_h = seg_a * h_sh + seg_h
            seg_a = seg_a * a_sh
        h_in = h_carry[q]
        after = seg_a * h_in + seg_h
        enter = jnp.where(row == 0, h_in, pltpu.roll(after, 1, axis=0))
        h_carry[q] = after[SUBLANES - 1:SUBLANES, :]
        hs = (u_sc[q].reshape(REC_GROUPS, SUBLANES, D_RNN)
              + a_sc[q].reshape(REC_GROUPS, SUBLANES, D_RNN) * enter[None]).reshape(ts, D_RNN)
        y = hs * jax.nn.gelu(gate)
        y_t = jnp.dot(perm_t_ref[...], y.astype(BF16), preferred_element_type=F32).astype(BF16)
        return jnp.dot(y_t, w_out_ref[...], preferred_element_type=F32)

    def finish(q, out):
        z = ALPHA * x_ref[q] + out
        xn = _layer_norm_rows(z, g_ref[...], b_ref[...])
        o_ref[q] = xn
        opk_ref[q] = _pack_row_halves(xn)
        lg_ref[q] = _router_logits_t(xn, wrt_ref[...], brt_ref[...])

    assert REC_NSEQ == 2
    gate_a, xr_a = project(0)
    conv_gates(0, xr_a)
    gate_b, xr_b = project(1)
    conv_gates(1, xr_b)
    seg_a = segment_scan(0)
    seg_b = segment_scan(1)
    out_a = recur_out(0, gate_a, seg_a)
    out_b = recur_out(1, gate_b, seg_b)
    finish(0, out_a)
    finish(1, out_b)


def _band_gate_weights(w_r, w_i):
    spread = jnp.asarray(np.tile(np.eye(LRU_BLOCK_W, dtype=np.float32), (1, LRU_BLOCKS)), BF16)
    blk = np.arange(D_RNN) // LRU_BLOCK_W
    on_diag = jnp.asarray(blk[:, None] == blk[None, :])

    def dense(w):
        rows = w.reshape(D_RNN, LRU_BLOCK_W).astype(BF16)
        return jnp.where(on_diag, jnp.dot(rows, spread, preferred_element_type=F32), 0.0)

    wr, wi = dense(w_r), dense(w_i)
    tiles = []
    for j in range(N_GATE_TILES):
        ws = GATE_WIN_STARTS[j]
        cs = j * GATE_TILE
        lo_blk = cs // LRU_BLOCK_W
        hi_blk = (cs + GATE_TILE - 1) // LRU_BLOCK_W
        assert ws <= lo_blk * LRU_BLOCK_W and (hi_blk + 1) * LRU_BLOCK_W <= ws + GATE_WIN
        tiles.append(jnp.concatenate([wr[ws:ws + GATE_WIN, cs:cs + GATE_TILE],
                                      wi[ws:ws + GATE_WIN, cs:cs + GATE_TILE]], axis=1))
    return jnp.stack(tiles).astype(BF16)


def _rglru_layer(x, w_in, conv_w, conv_b, w_r, b_r, w_i, b_i, lam, w_out, ln_g, ln_b, router_w, router_b):
    B, S, D = x.shape
    wg = _band_gate_weights(w_r, w_i)
    rho = np.arange(REC_TS)
    perm_np = np.zeros((REC_TS, REC_TS), np.float32)
    perm_np[rho, (rho % SUBLANES) * REC_GROUPS + rho // SUBLANES] = 1.0
    perm = jnp.asarray(perm_np, BF16)
    perm_t = jnp.asarray(perm_np.T, BF16)
    row = lambda v: v.reshape(1, -1)
    const = lambda shape: pl.BlockSpec(shape, lambda b, s: (0,) * len(shape))
    tile = lambda w: pl.BlockSpec((REC_NSEQ, REC_TS, w), lambda b, s: (b, s, 0))
    halo = (CONV_W - 1) * SUBLANES
    return pl.pallas_call(
        _rglru_kernel,
        grid=(B // REC_NSEQ, S // REC_TS),
        in_specs=[
            tile(D),
            const((REC_TS, REC_TS)),
            const((REC_TS, REC_TS)),
            const((D, 2 * D_RNN)),
            const((CONV_W, D_RNN)),
            const((1, D_RNN)),
            const((N_GATE_TILES, GATE_WIN, 2 * GATE_TILE)),
            const((1, D_RNN)),
            const((1, D_RNN)),
            const((1, D_RNN)),
            const((D_RNN, D)),
            const((1, D)),
            const((1, D)),
            const((2 * ROUTE_ROWS, D)),
            const((ROUTE_ROWS, 1)),
        ],
        out_specs=[tile(D), tile(D_HALF),
                   pl.BlockSpec((REC_NSEQ, ROUTE_ROWS, REC_TS), lambda b, s: (b, 0, s))],
        out_shape=[jax.ShapeDtypeStruct((B, S, D), F32), jax.ShapeDtypeStruct((B, S, D_HALF), U32),
                   jax.ShapeDtypeStruct((B, ROUTE_ROWS, S), F32)],
        scratch_shapes=[
            pltpu.VMEM((REC_NSEQ, halo + REC_TS, D_RNN), F32),
            pltpu.VMEM((REC_NSEQ, halo, D_RNN), F32),
            pltpu.VMEM((REC_NSEQ, REC_TS, D_RNN), F32),
            pltpu.VMEM((REC_NSEQ, REC_TS, D_RNN), F32),
            pltpu.VMEM((REC_NSEQ, 1, D_RNN), F32),
        ],
        compiler_params=pltpu.CompilerParams(
            dimension_semantics=("arbitrary", "arbitrary"), vmem_limit_bytes=VMEM_LIMIT),
        name="rglru_ln",
    )(x, perm, perm_t, w_in.astype(BF16), conv_w, row(conv_b), wg, row(b_r), row(b_i), row(lam), w_out.astype(BF16),
      row(ln_g), row(ln_b), router_w, router_b)


def _router_kernel(logits_ref, tri_ref, idx_ref, gate_ref, cnt_ref, base_sc):
    step = pl.program_id(0)
    tr = ROUTE_T

    @pl.when(step == 0)
    def _():
        base_sc[...] = jnp.zeros((N_EXPERTS, 1), F32)

    row8 = lax.broadcasted_iota(jnp.int32, (SUBLANES, tr), 0).astype(F32)
    rowe = lax.broadcasted_iota(jnp.int32, (N_EXPERTS, tr), 0).astype(F32)
    neg_inf = -jnp.inf
    for sub in range(ROUTE_STEP // ROUTE_T):
        cols = slice(sub * tr, (sub + 1) * tr)
        _route_tile(logits_ref[0, :, cols], row8, rowe, neg_inf, tri_ref, idx_ref, gate_ref, base_sc, cols)
    cnt_ref[...] = jnp.broadcast_to(base_sc[...], (N_EXPERTS, LANES)).astype(jnp.int32)


def _route_tile(logits, row8, rowe, neg_inf, tri_ref, idx_ref, gate_ref, base_sc, cols):
    tr = ROUTE_T
    g = jnp.where(row8 < N_GROUPS, logits[N_EXPERTS:N_EXPERTS + SUBLANES, :], neg_inf)
    gmax = jnp.max(g, axis=0, keepdims=True)
    gidx = jnp.min(jnp.where(g == gmax, row8, SUBLANES), axis=0, keepdims=True)
    g_gate = 1.0 / jnp.sum(jnp.exp(g - gmax), axis=0, keepdims=True)

    esel = logits[0:EXPERTS_PER_GROUP, :]
    for grp in range(1, N_GROUPS):
        esel = jnp.where(gidx == grp, logits[grp * EXPERTS_PER_GROUP:(grp + 1) * EXPERTS_PER_GROUP, :], esel)
    v1 = jnp.max(esel, axis=0, keepdims=True)
    i1 = jnp.min(jnp.where(esel == v1, row8, SUBLANES), axis=0, keepdims=True)
    esel2 = jnp.where(row8 == i1, neg_inf, esel)
    v2 = jnp.max(esel2, axis=0, keepdims=True)
    i2 = jnp.min(jnp.where(esel2 == v2, row8, SUBLANES), axis=0, keepdims=True)
    e21 = jnp.exp(v2 - v1)
    inv = 1.0 / (1.0 + e21)
    gate1 = inv * g_gate
    gate2 = e21 * inv * g_gate
    e1 = gidx * EXPERTS_PER_GROUP + i1
    e2 = gidx * EXPERTS_PER_GROUP + i2

    hit1 = rowe == e1
    hit2 = rowe == e2
    member = jnp.where(hit1, 1.0, jnp.where(hit2, 1.0, 0.0))
    before = jnp.dot(member.astype(BF16), tri_ref[...], preferred_element_type=F32) + base_sc[...]
    rank1 = jnp.sum(jnp.where(hit1, before, 0.0), axis=0, keepdims=True)
    rank2 = jnp.sum(jnp.where(hit2, before, 0.0), axis=0, keepdims=True)
    base_sc[...] = base_sc[...] + jnp.sum(member, axis=1, keepdims=True)

    zi = jnp.zeros((1, tr), jnp.int32)
    idx_ref[:, cols] = jnp.concatenate(
        [e1.astype(jnp.int32), e2.astype(jnp.int32), rank1.astype(jnp.int32), rank2.astype(jnp.int32),
         zi, zi, zi, zi], axis=0)
    zf = jnp.zeros((1, tr), F32)
    gate_ref[:, cols] = jnp.concatenate([gate1, gate2, zf, zf, zf, zf, zf, zf], axis=0)


def _router_weights(w_rg, b_rg, w_re, b_re):
    D = w_rg.shape[0]
    pad_rows = ROUTE_ROWS - N_EXPERTS - N_GROUPS
    w = jnp.concatenate([w_re.T, w_rg.T, jnp.zeros((pad_rows, D), F32)], axis=0)
    whi = w.astype(BF16)
    wlo = (w - whi.astype(F32)).astype(BF16)
    w_split = jnp.concatenate([whi, wlo], axis=0)
    bias = jnp.concatenate([b_re, b_rg, jnp.zeros((pad_rows,), F32)]).reshape(ROUTE_ROWS, 1)
    return w_split, bias


def _router(logits_t):
    B, _, S = logits_t.shape
    T = B * S
    per_row = S // ROUTE_STEP
    tri = jnp.asarray(np.triu(np.ones((ROUTE_T, ROUTE_T), np.float32), 1), BF16)
    const = lambda shape: pl.BlockSpec(shape, lambda i: (0,) * len(shape))
    return pl.pallas_call(
        _router_kernel,
        grid=(T // ROUTE_STEP,),
        in_specs=[
            pl.BlockSpec((1, ROUTE_ROWS, ROUTE_STEP), lambda i: (i // per_row, 0, i % per_row)),
            const((ROUTE_T, ROUTE_T)),
        ],
        out_specs=[
            pl.BlockSpec((SUBLANES, ROUTE_STEP), lambda i: (0, i)),
            pl.BlockSpec((SUBLANES, ROUTE_STEP), lambda i: (0, i)),
            const((N_EXPERTS, LANES)),
        ],
        out_shape=[
            jax.ShapeDtypeStruct((SUBLANES, T), jnp.int32),
            jax.ShapeDtypeStruct((SUBLANES, T), F32),
            jax.ShapeDtypeStruct((N_EXPERTS, LANES), jnp.int32),
        ],
        scratch_shapes=[pltpu.VMEM((N_EXPERTS, 1), F32)],
        compiler_params=pltpu.CompilerParams(
            dimension_semantics=("arbitrary",), vmem_limit_bytes=VMEM_LIMIT),
        name="router",
    )(logits_t, tri)


def _sc_mesh():
    return plsc.VectorSubcoreMesh(core_axis_name="c", subcore_axis_name="s",
                                  num_cores=SC_CORES, num_subcores=SC_SUBCORES)


def _sc_worker_id():
    return lax.axis_index("s") * SC_CORES + lax.axis_index("c")


def _sc_scratch(n_win, width):
    return [
        pltpu.VMEM((n_win, SC_WIN), jnp.int32),
        pltpu.VMEM((n_win, SC_WIN), jnp.int32),
        pltpu.VMEM((2, SC_WIN, width), U32),
        pltpu.SemaphoreType.DMA((2,)),
        pltpu.SemaphoreType.DMA((2,)),
    ]


def _sc_dispatch(rows, idx1, idx2, n_rows):
    _, width = rows.shape
    _, n_win, _ = idx1.shape

    @functools.partial(
        pl.kernel, mesh=_sc_mesh(), out_type=jax.ShapeDtypeStruct((n_rows, width), rows.dtype),
        scratch_types=_sc_scratch(n_win, width), name="sc_dispatch")
    def run(rows_hbm, i1_hbm, i2_hbm, o_hbm, i1_v, i2_v, buf, rsem, wsem):
        wid = _sc_worker_id()
        base = wid * (n_win * SC_WIN)
        pltpu.sync_copy(i1_hbm.at[wid], i1_v)
        pltpu.sync_copy(i2_hbm.at[wid], i2_v)

        def read(j):
            return pltpu.async_copy(rows_hbm.at[pl.ds(base + j * SC_WIN, SC_WIN)], buf.at[j % 2], rsem.at[j % 2])

        reads = {0: read(0)}
        writes = {}
        for j in range(n_win):
            if j + 1 < n_win:
                for d in writes.pop(j - 1, ()):
                    d.wait()
                reads[j + 1] = read(j + 1)
            reads.pop(j).wait()
            writes[j] = (pltpu.async_copy(buf.at[j % 2], o_hbm.at[i1_v.at[j]], wsem.at[j % 2]),
                         pltpu.async_copy(buf.at[j % 2], o_hbm.at[i2_v.at[j]], wsem.at[j % 2]))
        for j in sorted(writes):
            for d in writes[j]:
                d.wait()

    return run(rows, idx1, idx2)


def _sc_gather_pair(table, idx1, idx2):
    _, width = table.shape
    _, n_win, _ = idx1.shape
    n_tok = SC_WORKERS * n_win * SC_WIN
    out_t = jax.ShapeDtypeStruct((n_tok, width), table.dtype)

    @functools.partial(
        pl.kernel, mesh=_sc_mesh(), out_type=(out_t, out_t),
        scratch_types=_sc_scratch(n_win, width), name="sc_combine_gather")
    def run(table_hbm, i1_hbm, i2_hbm, o1_hbm, o2_hbm, i1_v, i2_v, buf, gsem, wsem):
        wid = _sc_worker_id()
        base = wid * (n_win * SC_WIN)
        pltpu.sync_copy(i1_hbm.at[wid], i1_v)
        pltpu.sync_copy(i2_hbm.at[wid], i2_v)
        work = [(i1_v, o1_hbm, j) for j in range(n_win)] + [(i2_v, o2_hbm, j) for j in range(n_win)]

        def gather(t):
            iv, _, j = work[t]
            return pltpu.async_copy(table_hbm.at[iv.at[j]], buf.at[t % 2], gsem.at[t % 2])

        def put(t):
            _, oh, j = work[t]
            return pltpu.async_copy(buf.at[t % 2], oh.at[pl.ds(base + j * SC_WIN, SC_WIN)], wsem.at[t % 2])

        gathers = {0: gather(0)}
        puts = {}
        for t in range(len(work)):
            if t + 1 < len(work):
                if t - 1 in puts:
                    puts.pop(t - 1).wait()
                gathers[t + 1] = gather(t + 1)
            gathers.pop(t).wait()
            puts[t] = put(t)
        for t in sorted(puts):
            puts[t].wait()

    return run(table, idx1, idx2)


def _moe_kernel(layer, be_ref, slot_ref, nxt_ref, valid_ref, x_ref, w1_hbm, w3_hbm, w2_hbm, o_ref,
                w1_st, w3_st, w2_st, w1_sc, w3_sc, w2_sc, sems):
    i = pl.program_id(0)
    expert = be_ref[i]
    slot = slot_ref[i]
    new_expert = jnp.logical_or(i == 0, expert != be_ref[jnp.maximum(i - 1, 0)])

    def weight_copies(e, sl):
        return [pltpu.make_async_copy(hbm.at[layer, e], stage.at[sl], sems.at[k, sl])
                for k, (hbm, stage) in enumerate(((w1_hbm, w1_st), (w3_hbm, w3_st), (w2_hbm, w2_st)))]

    @pl.when(i == 0)
    def _():
        for cp in weight_copies(expert, slot):
            cp.start()

    @pl.when(new_expert)
    def _():
        for cp in weight_copies(expert, slot):
            cp.wait()
        w1_sc[...] = w1_st[slot].astype(BF16)
        w3_sc[...] = w3_st[slot].astype(BF16)
        w2_sc[...] = w2_st[slot].astype(BF16)
        nxt = nxt_ref[i]

        @pl.when(nxt >= 0)
        def _():
            for cp in weight_copies(nxt, 1 - slot):
                cp.start()

    def up(rows):
        xa, xb = _unpack_bf16_pair(x_ref[rows, :])
        xa = xa.astype(BF16)
        xb = xb.astype(BF16)
        h1 = (jnp.dot(xa, w1_sc[:D_HALF], preferred_element_type=F32)
              + jnp.dot(xb, w1_sc[D_HALF:], preferred_element_type=F32))
        h3 = (jnp.dot(xa, w3_sc[:D_HALF], preferred_element_type=F32)
              + jnp.dot(xb, w3_sc[D_HALF:], preferred_element_type=F32))
        return h1, h3

    def down(rows, h1, h3):
        hdn = (jax.nn.silu(h1) * h3).astype(BF16)
        y = jnp.dot(hdn, w2_sc[...], preferred_element_type=F32)
        o_ref[rows, :] = _pack_row_halves(y)

    n_sub = MOE_BM // MOE_SUB
    subs = [slice(k * MOE_SUB, (k + 1) * MOE_SUB) for k in range(n_sub)]
    valid = valid_ref[i]
    chains = (valid + (MOE_SUB - 1)) // MOE_SUB

    for live in range(n_sub + 1):
        @pl.when(chains == live)
        def _(live=live):
            ups = {}
            if live:
                ups[0] = up(subs[0])
            for k in range(live):
                if k + 1 < live:
                    ups[k + 1] = up(subs[k + 1])
                down(subs[k], *ups.pop(k))
            if live < n_sub:
                o_ref[live * MOE_SUB:, :] = jnp.zeros((MOE_BM - live * MOE_SUB, D_HALF), o_ref.dtype)


def _moe_blocks(xbuf, block_e, valid, w1, w3, w2, layer):
    n_rows, _ = xbuf.shape
    D = D_MODEL
    n_blocks = n_rows // MOE_BM
    pos = jnp.arange(n_blocks, dtype=jnp.int32)
    is_new = jnp.concatenate([jnp.ones((1,), bool), block_e[1:] != block_e[:-1]])
    slot = ((jnp.cumsum(is_new.astype(jnp.int32)) - 1) % 2).astype(jnp.int32)
    change_pos = jnp.where(is_new, pos, n_blocks)
    next_change = jnp.concatenate([lax.cummin(change_pos, reverse=True)[1:], jnp.full((1,), n_blocks, jnp.int32)])
    nxt = jnp.where(next_change < n_blocks, block_e[jnp.minimum(next_change, n_blocks - 1)], -1).astype(jnp.int32)
    rows = lambda i, be, sl, nx, nu: (i, 0)
    grid_spec = pltpu.PrefetchScalarGridSpec(
        num_scalar_prefetch=4,
        grid=(n_blocks,),
        in_specs=[
            pl.BlockSpec((MOE_BM, D_HALF), rows),
            pl.BlockSpec(memory_space=pl.ANY),
            pl.BlockSpec(memory_space=pl.ANY),
            pl.BlockSpec(memory_space=pl.ANY),
        ],
        out_specs=pl.BlockSpec((MOE_BM, D_HALF), rows),
        scratch_shapes=[
            pltpu.VMEM((2, D, D_EXPERT), F32),
            pltpu.VMEM((2, D, D_EXPERT), F32),
            pltpu.VMEM((2, D_EXPERT, D), F32),
            pltpu.VMEM((D, D_EXPERT), BF16),
            pltpu.VMEM((D, D_EXPERT), BF16),
            pltpu.VMEM((D_EXPERT, D), BF16),
            pltpu.SemaphoreType.DMA((3, 2)),
        ],
    )
    return pl.pallas_call(
        functools.partial(_moe_kernel, layer),
        grid_spec=grid_spec,
        out_shape=jax.ShapeDtypeStruct((n_rows, D_HALF), U32),
        compiler_params=pltpu.CompilerParams(
            dimension_semantics=("arbitrary",), vmem_limit_bytes=VMEM_LIMIT),
        name="moe_experts",
    )(block_e, slot, nxt, valid, xbuf, w1, w3, w2)


def _moe_combine_norm(x, y1, y2, gate_rows, g, b):
    n = x.shape[0]
    pad = jnp.zeros((LANES - SUBLANES, n), F32)
    gates = jnp.concatenate([gate_rows, pad], axis=0).T
    g1 = gates[:, 0:1]
    g2 = gates[:, 1:2]
    a1, b1 = _unpack_bf16_pair(y1)
    a2, b2 = _unpack_bf16_pair(y2)
    f = jnp.concatenate([g1 * a1 + g2 * a2, g1 * b1 + g2 * b2], axis=1)
    return _layer_norm_rows(ALPHA * x + f, g, b)


def _combine_kernel(x_ref, y1_ref, y2_ref, gates_ref, g_ref, b_ref, o_ref):
    o_ref[...] = _moe_combine_norm(x_ref[...], y1_ref[...], y2_ref[...], gates_ref[...], g_ref[...], b_ref[...])


def _combine_part_kernel(x_ref, y1_ref, y2_ref, gates_ref, g_ref, b_ref, prev_ref, o_ref):
    del prev_ref
    _combine_kernel(x_ref, y1_ref, y2_ref, gates_ref, g_ref, b_ref, o_ref)


def _combine_ln(xf, y_parts, gates, ln_g, ln_b):
    T, D = xf.shape
    n_parts = len(y_parts)
    steps = T // (COMB_T * n_parts)
    const = lambda shape: pl.BlockSpec(shape, lambda i: (0,) * len(shape))
    out = None
    for part, (y1, y2) in enumerate(y_parts):
        off = part * steps
        glob = lambda w, off=off: pl.BlockSpec((COMB_T, w), lambda i: (i + off, 0))
        local = lambda w: pl.BlockSpec((COMB_T, w), lambda i: (i, 0))
        in_specs = [glob(D), local(D_HALF), local(D_HALF),
                    pl.BlockSpec((SUBLANES, COMB_T), lambda i, off=off: (0, i + off)), const((1, D)), const((1, D))]
        args = [xf, y1, y2, gates, ln_g.reshape(1, D), ln_b.reshape(1, D)]
        if out is not None:
            in_specs.append(pl.BlockSpec(memory_space=pl.ANY))
            args.append(out)
        out = pl.pallas_call(
            _combine_kernel if out is None else _combine_part_kernel,
            grid=(steps,),
            in_specs=in_specs,
            out_specs=glob(D),
            out_shape=jax.ShapeDtypeStruct((T, D), F32),
            input_output_aliases={} if out is None else {len(args) - 1: 0},
            compiler_params=pltpu.CompilerParams(
                dimension_semantics=("arbitrary",), vmem_limit_bytes=VMEM_LIMIT),
            name="moe_combine_ln",
        )(*args)
    return out


def _hier_moe(xpk, logits_t, w1, w3, w2, layer, gather_parts):
    B, S, _ = xpk.shape
    D = D_MODEL
    T = B * S
    idx, gates, cnt = _router(logits_t)
    counts = cnt[:, 0]
    padded = ((counts + MOE_BM - 1) // MOE_BM) * MOE_BM
    pends = jnp.cumsum(padded)
    pstarts = pends - padded
    experts = jnp.arange(N_EXPERTS, dtype=jnp.int32)[:, None]

    def dest(e_row, rank_row):
        return jnp.sum(jnp.where(e_row[None, :] == experts, pstarts[:, None], 0), axis=0) + rank_row

    n_win = T // (SC_WORKERS * SC_WIN)
    dest1 = dest(idx[0], idx[2]).reshape(SC_WORKERS, n_win, SC_WIN)
    dest2 = dest(idx[1], idx[3]).reshape(SC_WORKERS, n_win, SC_WIN)
    n_blocks = -(-(T * TOP_K + N_EXPERTS * (MOE_BM - 1)) // MOE_BM)
    n_rows = n_blocks * MOE_BM
    block_start = jnp.arange(n_blocks, dtype=jnp.int32) * MOE_BM
    block_e = jnp.minimum(jnp.sum(block_start[:, None] >= pends[None, :], axis=1), N_EXPERTS - 1).astype(jnp.int32)
    of_block = block_e[:, None] == experts[:, 0][None, :]
    pick = lambda v: jnp.sum(jnp.where(of_block, v[None, :], 0), axis=1)
    valid = jnp.clip(pick(counts) - (block_start - pick(pstarts)), 0, MOE_BM).astype(jnp.int32)
    xbuf = _sc_dispatch(xpk.reshape(T, D_HALF), dest1, dest2, n_rows)
    ybuf = _moe_blocks(xbuf, block_e, valid, w1, w3, w2, layer)
    part_shape = (gather_parts, SC_WORKERS, n_win // gather_parts, SC_WIN)
    d1, d2 = dest1.reshape(part_shape), dest2.reshape(part_shape)
    y_parts = [_sc_gather_pair(ybuf, d1[p], d2[p]) for p in range(gather_parts)]
    return y_parts, gates


def _att_head_order():
    order = []
    for p in range(N_HEADS // 2):
        jj, m = divmod(p, 4)
        order += [8 * jj + m, 8 * jj + 4 + m]
    return order


ATT_HEAD_ORDER = _att_head_order()


def _attn_kernel(xprev_ref, y1_ref, y2_ref, gates_a_ref, gates_b_ref, g_prev_ref, b_prev_ref, wqkv_ref, bias_ref,
                 sink_ref, wo_ref, g_ref, b_ref, wrt_ref, brt_ref, o_ref, opk_ref, lg_ref,
                 kv_ext, o_sc, s_sc0, s_sc1, p_sc0, p_sc1):
    s = pl.program_id(1)
    tq = ATT_TQ
    s_bufs = (s_sc0, s_sc1)
    p_bufs = (p_sc0, p_sc1)
    gate_refs = (gates_a_ref, gates_b_ref)
    assert ATT_NSEQ == len(gate_refs)

    @pl.when(s == 0)
    def _():
        kv_ext[:, 0:WINDOW, :] = jnp.zeros((ATT_NSEQ, WINDOW, 2 * KV_DIM), BF16)

    def layer_input(sq):
        return _moe_combine_norm(xprev_ref[sq], y1_ref[sq], y2_ref[sq], gate_refs[sq][...],
                                 g_prev_ref[...], b_prev_ref[...])

    def project_qkv(sq, x):
        qkv = jnp.dot(x.astype(BF16), wqkv_ref[...], preferred_element_type=F32)
        kv_ext[sq, WINDOW:WINDOW + tq, :] = qkv[:, Q_DIM:].astype(BF16)
        return (qkv[:, :Q_DIM] * (HEAD_DIM ** -0.5 * LOG2E)).astype(BF16)

    lane = lax.broadcasted_iota(jnp.int32, (WINDOW, LANES), 1)
    low = lane < HEAD_DIM
    sub = lax.broadcasted_iota(jnp.int32, (LANES, WINDOW), 0)
    top = sub < HEAD_DIM
    first = jnp.where(s == 0, 1, 0)
    nt = (((1,), (1,)), ((), ()))
    zero = jnp.zeros((), BF16)

    tiles = [(sq, n, j) for n in range(ATT_NB) for j in range(2) for sq in range(ATT_NSEQ)]

    def scores(t, qs_all):
        sq, n, j = tiles[t]
        q = qs_all[sq]
        r0 = n * WINDOW
        k_tile = kv_ext[sq, r0:r0 + 2 * WINDOW, j * LANES:(j + 1) * LANES]
        parts = []
        for m in range(4):
            p = 4 * j + m
            qt = q[r0:r0 + WINDOW, p * LANES:(p + 1) * LANES]
            parts.append(jnp.where(low, qt, zero))
            parts.append(jnp.where(low, zero, qt))
        qs = jnp.concatenate(parts, axis=0)
        bias_sel = first if n == 0 else 0
        s_bufs[t % 2][...] = (lax.dot_general(k_tile, qs, nt, preferred_element_type=F32)
                              + bias_ref[bias_sel, j])

    def softmax_pv(t):
        sq, n, j = tiles[t]
        r0 = n * WINDOW
        s_sc = s_bufs[t % 2]
        p_sc = p_bufs[t % 2]
        inv_l = []
        for h in range(8):
            hc = slice(h * WINDOW, (h + 1) * WINDOW)
            sink = sink_ref[8 * j + h] * LOG2E
            mx = jnp.maximum(jnp.max(s_sc[:, hc], axis=0, keepdims=True), sink)
            pr = jnp.exp2(s_sc[:, hc] - mx)
            p_sc[:, hc] = pr.astype(BF16)
            inv_l.append(1.0 / (jnp.sum(pr, axis=0, keepdims=True) + jnp.exp2(sink - mx)))
        v_tile = kv_ext[sq, r0:r0 + 2 * WINDOW, KV_DIM + j * LANES:KV_DIM + (j + 1) * LANES]
        tn = (((0,), (0,)), ((), ()))
        ov = lax.dot_general(v_tile, p_sc[...], tn, preferred_element_type=F32)
        for m in range(4):
            p = 4 * j + m
            o_even = ov[:, (2 * m) * WINDOW:(2 * m + 1) * WINDOW] * inv_l[2 * m]
            o_odd = ov[:, (2 * m + 1) * WINDOW:(2 * m + 2) * WINDOW] * inv_l[2 * m + 1]
            o_sc[sq, p * LANES:(p + 1) * LANES, r0:r0 + WINDOW] = jnp.where(top, o_even, o_odd).astype(BF16)

    def project_out(sq):
        return jnp.dot(wo_ref[...], o_sc[sq], preferred_element_type=F32)

    def finish(sq, x, out_t):
        xn = _layer_norm_rows(ALPHA * x + out_t.T, g_ref[...], b_ref[...])
        o_ref[sq] = xn
        opk_ref[sq] = _pack_row_halves(xn)
        lg_ref[sq] = _router_logits_t(xn, wrt_ref[...], brt_ref[...])

    xs = [layer_input(0)]
    qs_all = [project_qkv(0, xs[0])]
    xs.append(layer_input(1))
    qs_all.append(project_qkv(1, xs[1]))
    scores(0, qs_all)
    for t in range(len(tiles)):
        if t + 1 < len(tiles):
            scores(t + 1, qs_all)
        softmax_pv(t)
    kv_ext[:, 0:WINDOW, :] = kv_ext[:, tq:tq + WINDOW, :]
    outs_t = [project_out(sq) for sq in range(ATT_NSEQ)]
    for sq in range(ATT_NSEQ):
        finish(sq, xs[sq], outs_t[sq])


def _attn_bias():
    qi = np.arange(WINDOW)[:, None]
    sj = np.arange(2 * WINDOW)[None, :]
    dist = qi - sj + WINDOW
    valid = (dist >= 0) & (dist < WINDOW)
    slopes = 2.0 ** (-8.0 * np.arange(1, N_HEADS + 1, dtype=np.float32) / N_HEADS)
    slopes = slopes.astype(np.float32)[ATT_HEAD_ORDER]
    sb = -(slopes[:, None, None] * dist.astype(np.float32)[None])
    later = np.where(valid[None], sb, -np.inf)
    first = np.where((valid & (sj >= WINDOW))[None], sb, -np.inf)
    bias = np.stack([later, first]).astype(np.float32) * np.float32(LOG2E)
    bias = bias.reshape(2, 2, 8, WINDOW, 2 * WINDOW).transpose(0, 1, 4, 2, 3).reshape(2, 2, 2 * WINDOW, 8 * WINDOW)
    return jnp.asarray(np.ascontiguousarray(bias))


ATT_N_INPUTS = 15


def _attn_part_kernel(*refs):
    _attn_kernel(*refs[:ATT_N_INPUTS], *refs[ATT_N_INPUTS + 3:])


def _attn_layer(x_prev, y_parts, gates, g_prev, b_prev, w_qkv, sinks, w_o, ln_g, ln_b, router_w, router_b):
    B, S, D = x_prev.shape
    steps = S // ATT_TQ
    assert ATT_HEAD_ORDER == list(np.arange(N_HEADS).reshape(2, 2, 4).transpose(0, 2, 1).reshape(-1))
    wq = w_qkv[:, :Q_DIM].reshape(D, 2, 2, 4, HEAD_DIM).transpose(0, 1, 3, 2, 4).reshape(D, Q_DIM)
    wqkv = jnp.concatenate([wq, w_qkv[:, Q_DIM:]], axis=1).astype(BF16)
    wo_t = w_o.reshape(2, 2, 4, HEAD_DIM, D).transpose(0, 2, 1, 3, 4).reshape(Q_DIM, D).T.astype(BF16)
    sink = sinks.reshape(2, 2, 4).transpose(0, 2, 1).reshape(N_HEADS, 1, 1)
    bias = _attn_bias()
    const = lambda shape: pl.BlockSpec(shape, lambda b, s: (0,) * len(shape))
    n_parts = len(y_parts)
    rows_per_part = B // n_parts
    pairs = rows_per_part // ATT_NSEQ
    outs = None
    for part, (y1, y2) in enumerate(y_parts):
        off = part * pairs
        glob = lambda w, off=off: pl.BlockSpec((ATT_NSEQ, ATT_TQ, w), lambda b, s: (b + off, s, 0))
        local = lambda w: pl.BlockSpec((ATT_NSEQ, ATT_TQ, w), lambda b, s: (b, s, 0))
        gate_rows = lambda sq, off=off: pl.BlockSpec(
            (SUBLANES, ATT_TQ), lambda b, s: (0, (ATT_NSEQ * (b + off) + sq) * steps + s))
        in_specs = [
            glob(D),
            local(D_HALF),
            local(D_HALF),
            gate_rows(0),
            gate_rows(1),
            const((1, D)),
            const((1, D)),
            const((D, Q_DIM + 2 * KV_DIM)),
            const((2, 2, 2 * WINDOW, 8 * WINDOW)),
            const((N_HEADS, 1, 1)),
            const((D, Q_DIM)),
            const((1, D)),
            const((1, D)),
            const((2 * ROUTE_ROWS, D)),
            const((ROUTE_ROWS, 1)),
        ]
        args = [x_prev, y1.reshape(rows_per_part, S, D_HALF), y2.reshape(rows_per_part, S, D_HALF), gates, gates,
                g_prev.reshape(1, D), b_prev.reshape(1, D), wqkv, bias, sink, wo_t,
                ln_g.reshape(1, D), ln_b.reshape(1, D), router_w, router_b]
        aliases = {}
        body = _attn_kernel
        if outs is not None:
            aliases = {len(args) + k: k for k in range(3)}
            in_specs += [pl.BlockSpec(memory_space=pl.ANY)] * 3
            args += list(outs)
            body = _attn_part_kernel
        outs = pl.pallas_call(
            body,
            grid=(pairs, steps),
            in_specs=in_specs,
            out_specs=[glob(D), glob(D_HALF),
                       pl.BlockSpec((ATT_NSEQ, ROUTE_ROWS, ATT_TQ), lambda b, s, off=off: (b + off, 0, s))],
            out_shape=[jax.ShapeDtypeStruct((B, S, D), F32), jax.ShapeDtypeStruct((B, S, D_HALF), U32),
                       jax.ShapeDtypeStruct((B, ROUTE_ROWS, S), F32)],
            input_output_aliases=aliases,
            scratch_shapes=[
                pltpu.VMEM((ATT_NSEQ, ATT_TQ + WINDOW, 2 * KV_DIM), BF16),
                pltpu.VMEM((ATT_NSEQ, Q_DIM, ATT_TQ), BF16),
                pltpu.VMEM((2 * WINDOW, 8 * WINDOW), F32),
                pltpu.VMEM((2 * WINDOW, 8 * WINDOW), F32),
                pltpu.VMEM((2 * WINDOW, 8 * WINDOW), BF16),
                pltpu.VMEM((2 * WINDOW, 8 * WINDOW), BF16),
            ],
            compiler_params=pltpu.CompilerParams(
                dimension_semantics=("arbitrary", "arbitrary"), vmem_limit_bytes=VMEM_LIMIT),
            name="swa_attn_ln",
        )(*args)
    return outs


def kernel(x, rec_w_in, rec_conv_w, rec_conv_b, rec_w_r, rec_b_r, rec_w_i, rec_b_i, rec_lambda, rec_w_out,
           att_w_qkv, att_sinks, att_w_o, moe_w_group, moe_b_group, moe_w_expert, moe_b_expert,
           moe_w1, moe_w3, moe_w2, ln_g, ln_b):
    assert DEPTH == 2
    B, S, D = x.shape

    router = [_router_weights(moe_w_group[layer], moe_b_group[layer], moe_w_expert[layer], moe_b_expert[layer])
              for layer in range(DEPTH)]

    x1, x1_pk, logits1 = _rglru_layer(x, rec_w_in[0], rec_conv_w[0], rec_conv_b[0], rec_w_r[0], rec_b_r[0],
                                      rec_w_i[0], rec_b_i[0], rec_lambda[0], rec_w_out[0], ln_g[0, 0], ln_b[0, 0],
                                      *router[0])
    y_parts, gates = _hier_moe(x1_pk, logits1, moe_w1, moe_w3, moe_w2, 0, gather_parts=ATT_PARTS)
    x3, x3_pk, logits3 = _attn_layer(x1, y_parts, gates, ln_g[0, 1], ln_b[0, 1], att_w_qkv[0], att_sinks[0],
                                     att_w_o[0], ln_g[1, 0], ln_b[1, 0], *router[1])
    y_parts, gates = _hier_moe(x3_pk, logits3, moe_w1, moe_w3, moe_w2, 1, gather_parts=1)
    out = _combine_ln(x3.reshape(B * S, D), y_parts, gates, ln_g[1, 1], ln_b[1, 1])
    return out.reshape(B, S, D)
```

```python
import functools

import jax
import jax.numpy as jnp
import numpy as np
from jax import lax
from jax.experimental import pallas as pl
from jax.experimental.pallas import tpu as pltpu
from jax.experimental.pallas import tpu_sc as plsc

F32 = jnp.float32
BF16 = jnp.bfloat16
U32 = jnp.uint32

D_MODEL = 1024
DEPTH = 2
D_RNN = 1280
LRU_BLOCKS = 16
LRU_BLOCK_W = D_RNN // LRU_BLOCKS
CONV_W = 4
LRU_C = 8.0
N_HEADS = 16
N_KV_HEADS = 4
HEAD_DIM = 64
WINDOW = 128
Q_DIM = N_HEADS * HEAD_DIM
KV_DIM = N_KV_HEADS * HEAD_DIM
N_GROUPS = 4
EXPERTS_PER_GROUP = 8
N_EXPERTS = N_GROUPS * EXPERTS_PER_GROUP
TOP_K = 2
D_EXPERT = 512
ALPHA = (2 * DEPTH) ** 0.25
LN_EPS = 1e-5
LOG2E = 1.4426950408889634

LANES = 128
SUBLANES = 8
VMEM_LIMIT = 56 * 1024 * 1024

REC_TS = 256
REC_GROUPS = REC_TS // SUBLANES
REC_NSEQ = 2
GATE_TILE = 256
GATE_WIN = 512
GATE_WIN_STARTS = (0, 128, 384, 640, 768)
N_GATE_TILES = D_RNN // GATE_TILE

ROUTE_T = 512
ROUTE_STEP = 2048
ROUTE_ROWS = 40

MOE_BM = 1024
MOE_SUB = 256

ATT_TQ = 256
ATT_NB = ATT_TQ // WINDOW
ATT_NSEQ = 2
ATT_PARTS = 4

COMB_T = 1024

D_HALF = D_MODEL // 2

SC_CORES = 2
SC_SUBCORES = 16
SC_WORKERS = SC_CORES * SC_SUBCORES
SC_WIN = 64


def _layer_norm_rows(z, g, b):
    mu = jnp.mean(z, axis=-1, keepdims=True)
    zc = z - mu
    var = jnp.mean(zc * zc, axis=-1, keepdims=True)
    return zc * lax.rsqrt(var + LN_EPS) * g + b


def _pack_bf16_pair(a, b):
    ua = lax.bitcast_convert_type(a.astype(BF16).astype(F32), U32)
    ub = lax.bitcast_convert_type(b.astype(BF16).astype(F32), U32)
    return (ua >> 16) | (ub & jnp.uint32(0xFFFF0000))


def _unpack_bf16_pair(w):
    a = lax.bitcast_convert_type(w << 16, F32)
    b = lax.bitcast_convert_type(w & jnp.uint32(0xFFFF0000), F32)
    return a, b


def _pack_row_halves(x):
    return _pack_bf16_pair(x[:, :D_HALF], x[:, D_HALF:])


def _router_logits_t(x, w_split, bias):
    xhi = x.astype(BF16)
    xlo = (x - xhi.astype(F32)).astype(BF16)
    nt = (((1,), (1,)), ((), ()))
    both = lax.dot_general(w_split, xhi, nt, preferred_element_type=F32)
    low = lax.dot_general(w_split[:ROUTE_ROWS], xlo, nt, preferred_element_type=F32)
    return both[:ROUTE_ROWS] + both[ROUTE_ROWS:] + low + bias


def _rglru_kernel(x_ref, perm_ref, perm_t_ref, w_in_ref, convw_ref, convb_ref, wg_ref, br_ref, bi_ref, lam_ref,
                  w_out_ref, g_ref, b_ref, wrt_ref, brt_ref, o_ref, opk_ref, lg_ref,
                  xr_ext, tail_sc, a_sc, u_sc, h_carry):
    s = pl.program_id(1)
    ts = REC_TS
    halo = (CONV_W - 1) * SUBLANES
    seqs = range(REC_NSEQ)

    @pl.when(s == 0)
    def _():
        tail_sc[...] = jnp.zeros((REC_NSEQ, halo, D_RNN), F32)
        h_carry[...] = jnp.zeros((REC_NSEQ, 1, D_RNN), F32)

    row = lax.broadcasted_iota(jnp.int32, (SUBLANES, D_RNN), 0)
    nlam = -lam_ref[...]
    sp = jnp.maximum(nlam, 0.0) + jnp.log1p(jnp.exp(-jnp.abs(nlam)))
    log2a_scale = (-LRU_C * LOG2E) * sp

    def project(q):
        xp = jnp.dot(perm_ref[...], x_ref[q].astype(BF16), preferred_element_type=F32).astype(BF16)
        proj = jnp.dot(xp, w_in_ref[...], preferred_element_type=F32)
        return proj[:, :D_RNN], proj[:, D_RNN:]

    def conv_gates(q, xr):
        for k in range(CONV_W - 1):
            r0 = ts - halo + k * SUBLANES
            cur = xr[r0:r0 + SUBLANES, :]
            prev = tail_sc[q, k * SUBLANES:(k + 1) * SUBLANES, :]
            xr_ext[q, k * SUBLANES:(k + 1) * SUBLANES, :] = jnp.where(
                row == 0, pltpu.roll(prev, 1, axis=0), pltpu.roll(cur, 1, axis=0))
        tail_sc[q] = xr[ts - halo:, :]
        xr_ext[q, halo:halo + ts, :] = xr
        xc = convb_ref[...] + convw_ref[CONV_W - 1:CONV_W, :] * xr
        for k in range(CONV_W - 1):
            xc = xc + convw_ref[k:k + 1, :] * xr_ext[q, k * SUBLANES:k * SUBLANES + ts, :]
        xcb = xc.astype(BF16)
        pres = [jnp.dot(xcb[:, GATE_WIN_STARTS[j]:GATE_WIN_STARTS[j] + GATE_WIN], wg_ref[j],
                        preferred_element_type=F32) for j in range(N_GATE_TILES)]
        for j, pre in enumerate(pres):
            cs = j * GATE_TILE
            r = jax.nn.sigmoid(pre[:, :GATE_TILE] + br_ref[:, cs:cs + GATE_TILE])
            i = jax.nn.sigmoid(pre[:, GATE_TILE:] + bi_ref[:, cs:cs + GATE_TILE])
            a = jnp.exp2(r * log2a_scale[:, cs:cs + GATE_TILE])
            s1 = 1.0 - a * a
            mult = jnp.where(s1 > 0.0, s1 * lax.rsqrt(s1), 0.0)
            u = mult * (i * xc[:, cs:cs + GATE_TILE])
            a_sc[q, :, cs:cs + GATE_TILE] = a
            u_sc[q, :, cs:cs + GATE_TILE] = u

    def segment_scan(q):
        h = jnp.zeros((SUBLANES, D_RNN), F32)
        prod = jnp.ones((SUBLANES, D_RNN), F32)
        for gidx in range(REC_GROUPS):
            rows = slice(gidx * SUBLANES, (gidx + 1) * SUBLANES)
            a8 = a_sc[q, rows, :]
            h = a8 * h + u_sc[q, rows, :]
            prod = a8 * prod
            u_sc[q, rows, :] = h
            a_sc[q, rows, :] = prod
        return h, prod

    def recur_out(q, gate, seg):
        seg_h, seg_a = seg
        for d in (1, 2, 4):
            keep = row >= d
            a_sh = jnp.where(keep, pltpu.roll(seg_a, d, axis=0), 1.0)
            h_sh = jnp.where(keep, pltpu.roll(seg_h, d, axis=0), 0.0)
            seg_h = seg_a * h_sh + seg_h
            seg_a = seg_a * a_sh
        h_in = h_carry[q]
        after = seg_a * h_in + seg_h
        enter = jnp.where(row == 0, h_in, pltpu.roll(after, 1, axis=0))
        h_carry[q] = after[SUBLANES - 1:SUBLANES, :]
        hs = (u_sc[q].reshape(REC_GROUPS, SUBLANES, D_RNN)
              + a_sc[q].reshape(REC_GROUPS, SUBLANES, D_RNN) * enter[None]).reshape(ts, D_RNN)
        y = hs * jax.nn.gelu(gate)
        y_t = jnp.dot(perm_t_ref[...], y.astype(BF16), preferred_element_type=F32).astype(BF16)
        return jnp.dot(y_t, w_out_ref[...], preferred_element_type=F32)

    def finish(q, out):
        z = ALPHA * x_ref[q] + out
        xn = _layer_norm_rows(z, g_ref[...], b_ref[...])
        o_ref[q] = xn
        opk_ref[q] = _pack_row_halves(xn)
        lg_ref[q] = _router_logits_t(xn, wrt_ref[...], brt_ref[...])

    assert REC_NSEQ == 2
    gate_a, xr_a = project(0)
    conv_gates(0, xr_a)
    gate_b, xr_b = project(1)
    conv_gates(1, xr_b)
    seg_a = segment_scan(0)
    seg_b = segment_scan(1)
    out_a = recur_out(0, gate_a, seg_a)
    out_b = recur_out(1, gate_b, seg_b)
    finish(0, out_a)
    finish(1, out_b)


def _band_gate_weights(w_r, w_i):
    spread = jnp.asarray(np.tile(np.eye(LRU_BLOCK_W, dtype=np.float32), (1, LRU_BLOCKS)), BF16)
    blk = np.arange(D_RNN) // LRU_BLOCK_W
    on_diag = jnp.asarray(blk[:, None] == blk[None, :])

    def dense(w):
        rows = w.reshape(D_RNN, LRU_BLOCK_W).astype(BF16)
        return jnp.where(on_diag, jnp.dot(rows, spread, preferred_element_type=F32), 0.0)

    wr, wi = dense(w_r), dense(w_i)
    tiles = []
    for j in range(N_GATE_TILES):
        ws = GATE_WIN_STARTS[j]
        cs = j * GATE_TILE
        lo_blk = cs // LRU_BLOCK_W
        hi_blk = (cs + GATE_TILE - 1) // LRU_BLOCK_W
        assert ws <= lo_blk * LRU_BLOCK_W and (hi_blk + 1) * LRU_BLOCK_W <= ws + GATE_WIN
        tiles.append(jnp.concatenate([wr[ws:ws + GATE_WIN, cs:cs + GATE_TILE],
                                      wi[ws:ws + GATE_WIN, cs:cs + GATE_TILE]], axis=1))
    return jnp.stack(tiles).astype(BF16)


def _rglru_layer(x, w_in, conv_w, conv_b, w_r, b_r, w_i, b_i, lam, w_out, ln_g, ln_b, router_w, router_b):
    B, S, D = x.shape
    wg = _band_gate_weights(w_r, w_i)
    rho = np.arange(REC_TS)
    perm_np = np.zeros((REC_TS, REC_TS), np.float32)
    perm_np[rho, (rho % SUBLANES) * REC_GROUPS + rho // SUBLANES] = 1.0
    perm = jnp.asarray(perm_np, BF16)
    perm_t = jnp.asarray(perm_np.T, BF16)
    row = lambda v: v.reshape(1, -1)
    const = lambda shape: pl.BlockSpec(shape, lambda b, s: (0,) * len(shape))
    tile = lambda w: pl.BlockSpec((REC_NSEQ, REC_TS, w), lambda b, s: (b, s, 0))
    halo = (CONV_W - 1) * SUBLANES
    return pl.pallas_call(
        _rglru_kernel,
        grid=(B // REC_NSEQ, S // REC_TS),
        in_specs=[
            tile(D),
            const((REC_TS, REC_TS)),
            const((REC_TS, REC_TS)),
            const((D, 2 * D_RNN)),
            const((CONV_W, D_RNN)),
            const((1, D_RNN)),
            const((N_GATE_TILES, GATE_WIN, 2 * GATE_TILE)),
            const((1, D_RNN)),
            const((1, D_RNN)),
            const((1, D_RNN)),
            const((D_RNN, D)),
            const((1, D)),
            const((1, D)),
            const((2 * ROUTE_ROWS, D)),
            const((ROUTE_ROWS, 1)),
        ],
        out_specs=[tile(D), tile(D_HALF),
                   pl.BlockSpec((REC_NSEQ, ROUTE_ROWS, REC_TS), lambda b, s: (b, 0, s))],
        out_shape=[jax.ShapeDtypeStruct((B, S, D), F32), jax.ShapeDtypeStruct((B, S, D_HALF), U32),
                   jax.ShapeDtypeStruct((B, ROUTE_ROWS, S), F32)],
        scratch_shapes=[
            pltpu.VMEM((REC_NSEQ, halo + REC_TS, D_RNN), F32),
            pltpu.VMEM((REC_NSEQ, halo, D_RNN), F32),
            pltpu.VMEM((REC_NSEQ, REC_TS, D_RNN), F32),
            pltpu.VMEM((REC_NSEQ, REC_TS, D_RNN), F32),
            pltpu.VMEM((REC_NSEQ, 1, D_RNN), F32),
        ],
        compiler_params=pltpu.CompilerParams(
            dimension_semantics=("arbitrary", "arbitrary"), vmem_limit_bytes=VMEM_LIMIT),
        name="rglru_ln",
    )(x, perm, perm_t, w_in.astype(BF16), conv_w, row(conv_b), wg, row(b_r), row(b_i), row(lam), w_out.astype(BF16),
      row(ln_g), row(ln_b), router_w, router_b)


def _router_kernel(logits_ref, tri_ref, idx_ref, gate_ref, cnt_ref, base_sc):
    step = pl.program_id(0)
    tr = ROUTE_T

    @pl.when(step == 0)
    def _():
        base_sc[...] = jnp.zeros((N_EXPERTS, 1), F32)

    row8 = lax.broadcasted_iota(jnp.int32, (SUBLANES, tr), 0).astype(F32)
    rowe = lax.broadcasted_iota(jnp.int32, (N_EXPERTS, tr), 0).astype(F32)
    neg_inf = -jnp.inf
    for sub in range(ROUTE_STEP // ROUTE_T):
        cols = slice(sub * tr, (sub + 1) * tr)
        _route_tile(logits_ref[0, :, cols], row8, rowe, neg_inf, tri_ref, idx_ref, gate_ref, base_sc, cols)
    cnt_ref[...] = jnp.broadcast_to(base_sc[...], (N_EXPERTS, LANES)).astype(jnp.int32)


def _route_tile(logits, row8, rowe, neg_inf, tri_ref, idx_ref, gate_ref, base_sc, cols):
    tr = ROUTE_T
    g = jnp.where(row8 < N_GROUPS, logits[N_EXPERTS:N_EXPERTS + SUBLANES, :], neg_inf)
    gmax = jnp.max(g, axis=0, keepdims=True)
    gidx = jnp.min(jnp.where(g == gmax, row8, SUBLANES), axis=0, keepdims=True)
    g_gate = 1.0 / jnp.sum(jnp.exp(g - gmax), axis=0, keepdims=True)

    esel = logits[0:EXPERTS_PER_GROUP, :]
    for grp in range(1, N_GROUPS):
        esel = jnp.where(gidx == grp, logits[grp * EXPERTS_PER_GROUP:(grp + 1) * EXPERTS_PER_GROUP, :], esel)
    v1 = jnp.max(esel, axis=0, keepdims=True)
    i1 = jnp.min(jnp.where(esel == v1, row8, SUBLANES), axis=0, keepdims=True)
    esel2 = jnp.where(row8 == i1, neg_inf, esel)
    v2 = jnp.max(esel2, axis=0, keepdims=True)
    i2 = jnp.min(jnp.where(esel2 == v2, row8, SUBLANES), axis=0, keepdims=True)
    e21 = jnp.exp(v2 - v1)
    inv = 1.0 / (1.0 + e21)
    gate1 = inv * g_gate
    gate2 = e21 * inv * g_gate
    e1 = gidx * EXPERTS_PER_GROUP + i1
    e2 = gidx * EXPERTS_PER_GROUP + i2

    hit1 = rowe == e1
    hit2 = rowe == e2
    member = jnp.where(hit1, 1.0, jnp.where(hit2, 1.0, 0.0))
    before = jnp.dot(member.astype(BF16), tri_ref[...], preferred_element_type=F32) + base_sc[...]
    rank1 = jnp.sum(jnp.where(hit1, before, 0.0), axis=0, keepdims=True)
    rank2 = jnp.sum(jnp.where(hit2, before, 0.0), axis=0, keepdims=True)
    base_sc[...] = base_sc[...] + jnp.sum(member, axis=1, keepdims=True)

    zi = jnp.zeros((1, tr), jnp.int32)
    idx_ref[:, cols] = jnp.concatenate(
        [e1.astype(jnp.int32), e2.astype(jnp.int32), rank1.astype(jnp.int32), rank2.astype(jnp.int32),
         zi, zi, zi, zi], axis=0)
    zf = jnp.zeros((1, tr), F32)
    gate_ref[:, cols] = jnp.concatenate([gate1, gate2, zf, zf, zf, zf, zf, zf], axis=0)


def _router_weights(w_rg, b_rg, w_re, b_re):
    D = w_rg.shape[0]
    pad_rows = ROUTE_ROWS - N_EXPERTS - N_GROUPS
    w = jnp.concatenate([w_re.T, w_rg.T, jnp.zeros((pad_rows, D), F32)], axis=0)
    whi = w.astype(BF16)
    wlo = (w - whi.astype(F32)).astype(BF16)
    w_split = jnp.concatenate([whi, wlo], axis=0)
    bias = jnp.concatenate([b_re, b_rg, jnp.zeros((pad_rows,), F32)]).reshape(ROUTE_ROWS, 1)
    return w_split, bias


def _router(logits_t):
    B, _, S = logits_t.shape
    T = B * S
    per_row = S // ROUTE_STEP
    tri = jnp.asarray(np.triu(np.ones((ROUTE_T, ROUTE_T), np.float32), 1), BF16)
    const = lambda shape: pl.BlockSpec(shape, lambda i: (0,) * len(shape))
    return pl.pallas_call(
        _router_kernel,
        grid=(T // ROUTE_STEP,),
        in_specs=[
            pl.BlockSpec((1, ROUTE_ROWS, ROUTE_STEP), lambda i: (i // per_row, 0, i % per_row)),
            const((ROUTE_T, ROUTE_T)),
        ],
        out_specs=[
            pl.BlockSpec((SUBLANES, ROUTE_STEP), lambda i: (0, i)),
            pl.BlockSpec((SUBLANES, ROUTE_STEP), lambda i: (0, i)),
            const((N_EXPERTS, LANES)),
        ],
        out_shape=[
            jax.ShapeDtypeStruct((SUBLANES, T), jnp.int32),
            jax.ShapeDtypeStruct((SUBLANES, T), F32),
            jax.ShapeDtypeStruct((N_EXPERTS, LANES), jnp.int32),
        ],
        scratch_shapes=[pltpu.VMEM((N_EXPERTS, 1), F32)],
        compiler_params=pltpu.CompilerParams(
            dimension_semantics=("arbitrary",), vmem_limit_bytes=VMEM_LIMIT),
        name="router",
    )(logits_t, tri)


def _sc_mesh():
    return plsc.VectorSubcoreMesh(core_axis_name="c", subcore_axis_name="s",
                                  num_cores=SC_CORES, num_subcores=SC_SUBCORES)


def _sc_worker_id():
    return lax.axis_index("s") * SC_CORES + lax.axis_index("c")


def _sc_scratch(n_win, width):
    return [
        pltpu.VMEM((n_win, SC_WIN), jnp.int32),
        pltpu.VMEM((n_win, SC_WIN), jnp.int32),
        pltpu.VMEM((2, SC_WIN, width), U32),
        pltpu.SemaphoreType.DMA((2,)),
        pltpu.SemaphoreType.DMA((2,)),
    ]


def _sc_dispatch(rows, idx1, idx2, n_rows):
    _, width = rows.shape
    _, n_win, _ = idx1.shape

    @functools.partial(
        pl.kernel, mesh=_sc_mesh(), out_type=jax.ShapeDtypeStruct((n_rows, width), rows.dtype),
        scratch_types=_sc_scratch(n_win, width), name="sc_dispatch")
    def run(rows_hbm, i1_hbm, i2_hbm, o_hbm, i1_v, i2_v, buf, rsem, wsem):
        wid = _sc_worker_id()
        base = wid * (n_win * SC_WIN)
        pltpu.sync_copy(i1_hbm.at[wid], i1_v)
        pltpu.sync_copy(i2_hbm.at[wid], i2_v)

        def read(j):
            return pltpu.async_copy(rows_hbm.at[pl.ds(base + j * SC_WIN, SC_WIN)], buf.at[j % 2], rsem.at[j % 2])

        reads = {0: read(0)}
        writes = {}
        for j in range(n_win):
            if j + 1 < n_win:
                for d in writes.pop(j - 1, ()):
                    d.wait()
                reads[j + 1] = read(j + 1)
            reads.pop(j).wait()
            writes[j] = (pltpu.async_copy(buf.at[j % 2], o_hbm.at[i1_v.at[j]], wsem.at[j % 2]),
                         pltpu.async_copy(buf.at[j % 2], o_hbm.at[i2_v.at[j]], wsem.at[j % 2]))
        for j in sorted(writes):
            for d in writes[j]:
                d.wait()

    return run(rows, idx1, idx2)


def _sc_gather_pair(table, idx1, idx2):
    _, width = table.shape
    _, n_win, _ = idx1.shape
    n_tok = SC_WORKERS * n_win * SC_WIN
    out_t = jax.ShapeDtypeStruct((n_tok, width), table.dtype)

    @functools.partial(
        pl.kernel, mesh=_sc_mesh(), out_type=(out_t, out_t),
        scratch_types=_sc_scratch(n_win, width), name="sc_combine_gather")
    def run(table_hbm, i1_hbm, i2_hbm, o1_hbm, o2_hbm, i1_v, i2_v, buf, gsem, wsem):
        wid = _sc_worker_id()
        base = wid * (n_win * SC_WIN)
        pltpu.sync_copy(i1_hbm.at[wid], i1_v)
        pltpu.sync_copy(i2_hbm.at[wid], i2_v)
        work = [(i1_v, o1_hbm, j) for j in range(n_win)] + [(i2_v, o2_hbm, j) for j in range(n_win)]

        def gather(t):
            iv, _, j = work[t]
            return pltpu.async_copy(table_hbm.at[iv.at[j]], buf.at[t % 2], gsem.at[t % 2])

        def put(t):
            _, oh, j = work[t]
            return pltpu.async_copy(buf.at[t % 2], oh.at[pl.ds(base + j * SC_WIN, SC_WIN)], wsem.at[t % 2])

        gathers = {0: gather(0)}
        puts = {}
        for t in range(len(work)):
            if t + 1 < len(work):
                if t - 1 in puts:
                    puts.pop(t - 1).wait()
                gathers[t + 1] = gather(t + 1)
            gathers.pop(t).wait()
            puts[t] = put(t)
        for t in sorted(puts):
            puts[t].wait()

    return run(table, idx1, idx2)


def _moe_kernel(layer, be_ref, slot_ref, nxt_ref, valid_ref, x_ref, w1_hbm, w3_hbm, w2_hbm, o_ref,
                w1_st, w3_st, w2_st, w1_sc, w3_sc, w2_sc, sems):
    i = pl.program_id(0)
    expert = be_ref[i]
    slot = slot_ref[i]
    new_expert = jnp.logical_or(i == 0, expert != be_ref[jnp.maximum(i - 1, 0)])

    def weight_copies(e, sl):
        return [pltpu.make_async_copy(hbm.at[layer, e], stage.at[sl], sems.at[k, sl])
                for k, (hbm, stage) in enumerate(((w1_hbm, w1_st), (w3_hbm, w3_st), (w2_hbm, w2_st)))]

    @pl.when(i == 0)
    def _():
        for cp in weight_copies(expert, slot):
            cp.start()

    @pl.when(new_expert)
    def _():
        for cp in weight_copies(expert, slot):
            cp.wait()
        w1_sc[...] = w1_st[slot].astype(BF16)
        w3_sc[...] = w3_st[slot].astype(BF16)
        w2_sc[...] = w2_st[slot].astype(BF16)
        nxt = nxt_ref[i]

        @pl.when(nxt >= 0)
        def _():
            for cp in weight_copies(nxt, 1 - slot):
                cp.start()

    def up(rows):
        xa, xb = _unpack_bf16_pair(x_ref[rows, :])
        xa = xa.astype(BF16)
        xb = xb.astype(BF16)
        h1 = (jnp.dot(xa, w1_sc[:D_HALF], preferred_element_type=F32)
              + jnp.dot(xb, w1_sc[D_HALF:], preferred_element_type=F32))
        h3 = (jnp.dot(xa, w3_sc[:D_HALF], preferred_element_type=F32)
              + jnp.dot(xb, w3_sc[D_HALF:], preferred_element_type=F32))
        return h1, h3

    def down(rows, h1, h3):
        hdn = (jax.nn.silu(h1) * h3).astype(BF16)
        y = jnp.dot(hdn, w2_sc[...], preferred_element_type=F32)
        o_ref[rows, :] = _pack_row_halves(y)

    n_sub = MOE_BM // MOE_SUB
    subs = [slice(k * MOE_SUB, (k + 1) * MOE_SUB) for k in range(n_sub)]
    valid = valid_ref[i]
    chains = (valid + (MOE_SUB - 1)) // MOE_SUB

    for live in range(n_sub + 1):
        @pl.when(chains == live)
        def _(live=live):
            ups = {}
            if live:
                ups[0] = up(subs[0])
            for k in range(live):
                if k + 1 < live:
                    ups[k + 1] = up(subs[k + 1])
                down(subs[k], *ups.pop(k))
            if live < n_sub:
                o_ref[live * MOE_SUB:, :] = jnp.zeros((MOE_BM - live * MOE_SUB, D_HALF), o_ref.dtype)


def _moe_blocks(xbuf, block_e, valid, w1, w3, w2, layer):
    n_rows, _ = xbuf.shape
    D = D_MODEL
    n_blocks = n_rows // MOE_BM
    pos = jnp.arange(n_blocks, dtype=jnp.int32)
    is_new = jnp.concatenate([jnp.ones((1,), bool), block_e[1:] != block_e[:-1]])
    slot = ((jnp.cumsum(is_new.astype(jnp.int32)) - 1) % 2).astype(jnp.int32)
    change_pos = jnp.where(is_new, pos, n_blocks)
    next_change = jnp.concatenate([lax.cummin(change_pos, reverse=True)[1:], jnp.full((1,), n_blocks, jnp.int32)])
    nxt = jnp.where(next_change < n_blocks, block_e[jnp.minimum(next_change, n_blocks - 1)], -1).astype(jnp.int32)
    rows = lambda i, be, sl, nx, nu: (i, 0)
    grid_spec = pltpu.PrefetchScalarGridSpec(
        num_scalar_prefetch=4,
        grid=(n_blocks,),
        in_specs=[
            pl.BlockSpec((MOE_BM, D_HALF), rows),
            pl.BlockSpec(memory_space=pl.ANY),
            pl.BlockSpec(memory_space=pl.ANY),
            pl.BlockSpec(memory_space=pl.ANY),
        ],
        out_specs=pl.BlockSpec((MOE_BM, D_HALF), rows),
        scratch_shapes=[
            pltpu.VMEM((2, D, D_EXPERT), F32),
            pltpu.VMEM((2, D, D_EXPERT), F32),
            pltpu.VMEM((2, D_EXPERT, D), F32),
            pltpu.VMEM((D, D_EXPERT), BF16),
            pltpu.VMEM((D, D_EXPERT), BF16),
            pltpu.VMEM((D_EXPERT, D), BF16),
            pltpu.SemaphoreType.DMA((3, 2)),
        ],
    )
    return pl.pallas_call(
        functools.partial(_moe_kernel, layer),
        grid_spec=grid_spec,
        out_shape=jax.ShapeDtypeStruct((n_rows, D_HALF), U32),
        compiler_params=pltpu.CompilerParams(
            dimension_semantics=("arbitrary",), vmem_limit_bytes=VMEM_LIMIT),
        name="moe_experts",
    )(block_e, slot, nxt, valid, xbuf, w1, w3, w2)


def _moe_combine_norm(x, y1, y2, gate_rows, g, b):
    n = x.shape[0]
    pad = jnp.zeros((LANES - SUBLANES, n), F32)
    gates = jnp.concatenate([gate_rows, pad], axis=0).T
    g1 = gates[:, 0:1]
    g2 = gates[:, 1:2]
    a1, b1 = _unpack_bf16_pair(y1)
    a2, b2 = _unpack_bf16_pair(y2)
    f = jnp.concatenate([g1 * a1 + g2 * a2, g1 * b1 + g2 * b2], axis=1)
    return _layer_norm_rows(ALPHA * x + f, g, b)


def _combine_kernel(x_ref, y1_ref, y2_ref, gates_ref, g_ref, b_ref, o_ref):
    o_ref[...] = _moe_combine_norm(x_ref[...], y1_ref[...], y2_ref[...], gates_ref[...], g_ref[...], b_ref[...])


def _combine_part_kernel(x_ref, y1_ref, y2_ref, gates_ref, g_ref, b_ref, prev_ref, o_ref):
    del prev_ref
    _combine_kernel(x_ref, y1_ref, y2_ref, gates_ref, g_ref, b_ref, o_ref)


def _combine_ln(xf, y_parts, gates, ln_g, ln_b):
    T, D = xf.shape
    n_parts = len(y_parts)
    steps = T // (COMB_T * n_parts)
    const = lambda shape: pl.BlockSpec(shape, lambda i: (0,) * len(shape))
    out = None
    for part, (y1, y2) in enumerate(y_parts):
        off = part * steps
        glob = lambda w, off=off: pl.BlockSpec((COMB_T, w), lambda i: (i + off, 0))
        local = lambda w: pl.BlockSpec((COMB_T, w), lambda i: (i, 0))
        in_specs = [glob(D), local(D_HALF), local(D_HALF),
                    pl.BlockSpec((SUBLANES, COMB_T), lambda i, off=off: (0, i + off)), const((1, D)), const((1, D))]
        args = [xf, y1, y2, gates, ln_g.reshape(1, D), ln_b.reshape(1, D)]
        if out is not None:
            in_specs.append(pl.BlockSpec(memory_space=pl.ANY))
            args.append(out)
        out = pl.pallas_call(
            _combine_kernel if out is None else _combine_part_kernel,
            grid=(steps,),
            in_specs=in_specs,
            out_specs=glob(D),
            out_shape=jax.ShapeDtypeStruct((T, D), F32),
            input_output_aliases={} if out is None else {len(args) - 1: 0},
            compiler_params=pltpu.CompilerParams(
                dimension_semantics=("arbitrary",), vmem_limit_bytes=VMEM_LIMIT),
            name="moe_combine_ln",
        )(*args)
    return out


def _hier_moe(xpk, logits_t, w1, w3, w2, layer, gather_parts):
    B, S, _ = xpk.shape
    D = D_MODEL
    T = B * S
    idx, gates, cnt = _router(logits_t)
    counts = cnt[:, 0]
    padded = ((counts + MOE_BM - 1) // MOE_BM) * MOE_BM
    pends = jnp.cumsum(padded)
    pstarts = pends - padded
    experts = jnp.arange(N_EXPERTS, dtype=jnp.int32)[:, None]

    def dest(e_row, rank_row):
        return jnp.sum(jnp.where(e_row[None, :] == experts, pstarts[:, None], 0), axis=0) + rank_row

    n_win = T // (SC_WORKERS * SC_WIN)
    dest1 = dest(idx[0], idx[2]).reshape(SC_WORKERS, n_win, SC_WIN)
    dest2 = dest(idx[1], idx[3]).reshape(SC_WORKERS, n_win, SC_WIN)
    n_blocks = -(-(T * TOP_K + N_EXPERTS * (MOE_BM - 1)) // MOE_BM)
    n_rows = n_blocks * MOE_BM
    block_start = jnp.arange(n_blocks, dtype=jnp.int32) * MOE_BM
    block_e = jnp.minimum(jnp.sum(block_start[:, None] >= pends[None, :], axis=1), N_EXPERTS - 1).astype(jnp.int32)
    of_block = block_e[:, None] == experts[:, 0][None, :]
    pick = lambda v: jnp.sum(jnp.where(of_block, v[None, :], 0), axis=1)
    valid = jnp.clip(pick(counts) - (block_start - pick(pstarts)), 0, MOE_BM).astype(jnp.int32)
    xbuf = _sc_dispatch(xpk.reshape(T, D_HALF), dest1, dest2, n_rows)
    ybuf = _moe_blocks(xbuf, block_e, valid, w1, w3, w2, layer)
    part_shape = (gather_parts, SC_WORKERS, n_win // gather_parts, SC_WIN)
    d1, d2 = dest1.reshape(part_shape), dest2.reshape(part_shape)
    y_parts = [_sc_gather_pair(ybuf, d1[p], d2[p]) for p in range(gather_parts)]
    return y_parts, gates


def _att_head_order():
    order = []
    for p in range(N_HEADS // 2):
        jj, m = divmod(p, 4)
        order += [8 * jj + m, 8 * jj + 4 + m]
    return order


ATT_HEAD_ORDER = _att_head_order()


def _attn_kernel(xprev_ref, y1_ref, y2_ref, gates_a_ref, gates_b_ref, g_prev_ref, b_prev_ref, wqkv_ref, bias_ref,
                 sink_ref, wo_ref, g_ref, b_ref, wrt_ref, brt_ref, o_ref, opk_ref, lg_ref,
                 kv_ext, o_sc, s_sc0, s_sc1, p_sc0, p_sc1):
    s = pl.program_id(1)
    tq = ATT_TQ
    s_bufs = (s_sc0, s_sc1)
    p_bufs = (p_sc0, p_sc1)
    gate_refs = (gates_a_ref, gates_b_ref)
    assert ATT_NSEQ == len(gate_refs)

    @pl.when(s == 0)
    def _():
        kv_ext[:, 0:WINDOW, :] = jnp.zeros((ATT_NSEQ, WINDOW, 2 * KV_DIM), BF16)

    def layer_input(sq):
        return _moe_combine_norm(xprev_ref[sq], y1_ref[sq], y2_ref[sq], gate_refs[sq][...],
                                 g_prev_ref[...], b_prev_ref[...])

    def project_qkv(sq, x):
        qkv = jnp.dot(x.astype(BF16), wqkv_ref[...], preferred_element_type=F32)
        kv_ext[sq, WINDOW:WINDOW + tq, :] = qkv[:, Q_DIM:].astype(BF16)
        return (qkv[:, :Q_DIM] * (HEAD_DIM ** -0.5 * LOG2E)).astype(BF16)

    lane = lax.broadcasted_iota(jnp.int32, (WINDOW, LANES), 1)
    low = lane < HEAD_DIM
    sub = lax.broadcasted_iota(jnp.int32, (LANES, WINDOW), 0)
    top = sub < HEAD_DIM
    first = jnp.where(s == 0, 1, 0)
    nt = (((1,), (1,)), ((), ()))
    zero = jnp.zeros((), BF16)

    tiles = [(sq, n, j) for n in range(ATT_NB) for j in range(2) for sq in range(ATT_NSEQ)]

    def scores(t, qs_all):
        sq, n, j = tiles[t]
        q = qs_all[sq]
        r0 = n * WINDOW
        k_tile = kv_ext[sq, r0:r0 + 2 * WINDOW, j * LANES:(j + 1) * LANES]
        parts = []
        for m in range(4):
            p = 4 * j + m
            qt = q[r0:r0 + WINDOW, p * LANES:(p + 1) * LANES]
            parts.append(jnp.where(low, qt, zero))
            parts.append(jnp.where(low, zero, qt))
        qs = jnp.concatenate(parts, axis=0)
        bias_sel = first if n == 0 else 0
        s_bufs[t % 2][...] = (lax.dot_general(k_tile, qs, nt, preferred_element_type=F32)
                              + bias_ref[bias_sel, j])

    def softmax_pv(t):
        sq, n, j = tiles[t]
        r0 = n * WINDOW
        s_sc = s_bufs[t % 2]
        p_sc = p_bufs[t % 2]
        inv_l = []
        for h in range(8):
            hc = slice(h * WINDOW, (h + 1) * WINDOW)
            sink = sink_ref[8 * j + h] * LOG2E
            mx = jnp.maximum(jnp.max(s_sc[:, hc], axis=0, keepdims=True), sink)
            pr = jnp.exp2(s_sc[:, hc] - mx)
            p_sc[:, hc] = pr.astype(BF16)
            inv_l.append(1.0 / (jnp.sum(pr, axis=0, keepdims=True) + jnp.exp2(sink - mx)))
        v_tile = kv_ext[sq, r0:r0 + 2 * WINDOW, KV_DIM + j * LANES:KV_DIM + (j + 1) * LANES]
        tn = (((0,), (0,)), ((), ()))
        ov = lax.dot_general(v_tile, p_sc[...], tn, preferred_element_type=F32)
        for m in range(4):
            p = 4 * j + m
            o_even = ov[:, (2 * m) * WINDOW:(2 * m + 1) * WINDOW] * inv_l[2 * m]
            o_odd = ov[:, (2 * m + 1) * WINDOW:(2 * m + 2) * WINDOW] * inv_l[2 * m + 1]
            o_sc[sq, p * LANES:(p + 1) * LANES, r0:r0 + WINDOW] = jnp.where(top, o_even, o_odd).astype(BF16)

    def project_out(sq):
        tn = (((0,), (0,)), ((), ()))
        return lax.dot_general(o_sc[sq], wo_ref[...], tn, preferred_element_type=F32)

    def finish(sq, x, out):
        xn = _layer_norm_rows(ALPHA * x + out, g_ref[...], b_ref[...])
        o_ref[sq] = xn
        opk_ref[sq] = _pack_row_halves(xn)
        lg_ref[sq] = _router_logits_t(xn, wrt_ref[...], brt_ref[...])

    xs = [layer_input(0)]
    qs_all = [project_qkv(0, xs[0])]
    xs.append(layer_input(1))
    qs_all.append(project_qkv(1, xs[1]))
    scores(0, qs_all)
    for t in range(len(tiles)):
        if t + 1 < len(tiles):
            scores(t + 1, qs_all)
        softmax_pv(t)
    kv_ext[:, 0:WINDOW, :] = kv_ext[:, tq:tq + WINDOW, :]
    outs = [project_out(sq) for sq in range(ATT_NSEQ)]
    for sq in range(ATT_NSEQ):
        finish(sq, xs[sq], outs[sq])


def _attn_bias():
    qi = np.arange(WINDOW)[:, None]
    sj = np.arange(2 * WINDOW)[None, :]
    dist = qi - sj + WINDOW
    valid = (dist >= 0) & (dist < WINDOW)
    slopes = 2.0 ** (-8.0 * np.arange(1, N_HEADS + 1, dtype=np.float32) / N_HEADS)
    slopes = slopes.astype(np.float32)[ATT_HEAD_ORDER]
    sb = -(slopes[:, None, None] * dist.astype(np.float32)[None])
    later = np.where(valid[None], sb, -np.inf)
    first = np.where((valid & (sj >= WINDOW))[None], sb, -np.inf)
    bias = np.stack([later, first]).astype(np.float32) * np.float32(LOG2E)
    bias = bias.reshape(2, 2, 8, WINDOW, 2 * WINDOW).transpose(0, 1, 4, 2, 3).reshape(2, 2, 2 * WINDOW, 8 * WINDOW)
    return jnp.asarray(np.ascontiguousarray(bias))


ATT_N_INPUTS = 15


def _attn_part_kernel(*refs):
    _attn_kernel(*refs[:ATT_N_INPUTS], *refs[ATT_N_INPUTS + 3:])


def _attn_layer(x_prev, y_parts, gates, g_prev, b_prev, w_qkv, sinks, w_o, ln_g, ln_b, router_w, router_b):
    B, S, D = x_prev.shape
    steps = S // ATT_TQ
    assert ATT_HEAD_ORDER == list(np.arange(N_HEADS).reshape(2, 2, 4).transpose(0, 2, 1).reshape(-1))
    wq = w_qkv[:, :Q_DIM].reshape(D, 2, 2, 4, HEAD_DIM).transpose(0, 1, 3, 2, 4).reshape(D, Q_DIM)
    wqkv = jnp.concatenate([wq, w_qkv[:, Q_DIM:]], axis=1).astype(BF16)
    wo_t = w_o.reshape(2, 2, 4, HEAD_DIM, D).transpose(0, 2, 1, 3, 4).reshape(Q_DIM, D).astype(BF16)
    sink = sinks.reshape(2, 2, 4).transpose(0, 2, 1).reshape(N_HEADS, 1, 1)
    bias = _attn_bias()
    const = lambda shape: pl.BlockSpec(shape, lambda b, s: (0,) * len(shape))
    n_parts = len(y_parts)
    rows_per_part = B // n_parts
    pairs = rows_per_part // ATT_NSEQ
    outs = None
    for part, (y1, y2) in enumerate(y_parts):
        off = part * pairs
        glob = lambda w, off=off: pl.BlockSpec((ATT_NSEQ, ATT_TQ, w), lambda b, s: (b + off, s, 0))
        local = lambda w: pl.BlockSpec((ATT_NSEQ, ATT_TQ, w), lambda b, s: (b, s, 0))
        gate_rows = lambda sq, off=off: pl.BlockSpec(
            (SUBLANES, ATT_TQ), lambda b, s: (0, (ATT_NSEQ * (b + off) + sq) * steps + s))
        in_specs = [
            glob(D),
            local(D_HALF),
            local(D_HALF),
            gate_rows(0),
            gate_rows(1),
            const((1, D)),
            const((1, D)),
            const((D, Q_DIM + 2 * KV_DIM)),
            const((2, 2, 2 * WINDOW, 8 * WINDOW)),
            const((N_HEADS, 1, 1)),
            const((Q_DIM, D)),
            const((1, D)),
            const((1, D)),
            const((2 * ROUTE_ROWS, D)),
            const((ROUTE_ROWS, 1)),
        ]
        args = [x_prev, y1.reshape(rows_per_part, S, D_HALF), y2.reshape(rows_per_part, S, D_HALF), gates, gates,
                g_prev.reshape(1, D), b_prev.reshape(1, D), wqkv, bias, sink, wo_t,
                ln_g.reshape(1, D), ln_b.reshape(1, D), router_w, router_b]
        aliases = {}
        body = _attn_kernel
        if outs is not None:
            aliases = {len(args) + k: k for k in range(3)}
            in_specs += [pl.BlockSpec(memory_space=pl.ANY)] * 3
            args += list(outs)
            body = _attn_part_kernel
        outs = pl.pallas_call(
            body,
            grid=(pairs, steps),
            in_specs=in_specs,
            out_specs=[glob(D), glob(D_HALF),
                       pl.BlockSpec((ATT_NSEQ, ROUTE_ROWS, ATT_TQ), lambda b, s, off=off: (b + off, 0, s))],
            out_shape=[jax.ShapeDtypeStruct((B, S, D), F32), jax.ShapeDtypeStruct((B, S, D_HALF), U32),
                       jax.ShapeDtypeStruct((B, ROUTE_ROWS, S), F32)],
            input_output_aliases=aliases,
            scratch_shapes=[
                pltpu.VMEM((ATT_NSEQ, ATT_TQ + WINDOW, 2 * KV_DIM), BF16),
                pltpu.VMEM((ATT_NSEQ, Q_DIM, ATT_TQ), BF16),
                pltpu.VMEM((2 * WINDOW, 8 * WINDOW), F32),
                pltpu.VMEM((2 * WINDOW, 8 * WINDOW), F32),
                pltpu.VMEM((2 * WINDOW, 8 * WINDOW), BF16),
                pltpu.VMEM((2 * WINDOW, 8 * WINDOW), BF16),
            ],
            compiler_params=pltpu.CompilerParams(
                dimension_semantics=("arbitrary", "arbitrary"), vmem_limit_bytes=VMEM_LIMIT),
            name="swa_attn_ln",
        )(*args)
    return outs


def kernel(x, rec_w_in, rec_conv_w, rec_conv_b, rec_w_r, rec_b_r, rec_w_i, rec_b_i, rec_lambda, rec_w_out,
           att_w_qkv, att_sinks, att_w_o, moe_w_group, moe_b_group, moe_w_expert, moe_b_expert,
           moe_w1, moe_w3, moe_w2, ln_g, ln_b):
    assert DEPTH == 2
    B, S, D = x.shape

    router = [_router_weights(moe_w_group[layer], moe_b_group[layer], moe_w_expert[layer], moe_b_expert[layer])
              for layer in range(DEPTH)]

    x1, x1_pk, logits1 = _rglru_layer(x, rec_w_in[0], rec_conv_w[0], rec_conv_b[0], rec_w_r[0], rec_b_r[0],
                                      rec_w_i[0], rec_b_i[0], rec_lambda[0], rec_w_out[0], ln_g[0, 0], ln_b[0, 0],
                                      *router[0])
    y_parts, gates = _hier_moe(x1_pk, logits1, moe_w1, moe_w3, moe_w2, 0, gather_parts=ATT_PARTS)
    x3, x3_pk, logits3 = _attn_layer(x1, y_parts, gates, ln_g[0, 1], ln_b[0, 1], att_w_qkv[0], att_sinks[0],
                                     att_w_o[0], ln_g[1, 0], ln_b[1, 0], *router[1])
    y_parts, gates = _hier_moe(x3_pk, logits3, moe_w1, moe_w3, moe_w2, 1, gather_parts=1)
    out = _combine_ln(x3.reshape(B * S, D), y_parts, gates, ln_g[1, 1], ln_b[1, 1])
    return out.reshape(B, S, D)
```

```python
import functools

import jax
import jax.numpy as jnp
import numpy as np
from jax import lax
from jax.experimental import pallas as pl
from jax.experimental.pallas import tpu as pltpu
from jax.experimental.pallas import tpu_sc as plsc

F32 = jnp.float32
BF16 = jnp.bfloat16
U32 = jnp.uint32

D_MODEL = 1024
DEPTH = 2
D_RNN = 1280
LRU_BLOCKS = 16
LRU_BLOCK_W = D_RNN // LRU_BLOCKS
CONV_W = 4
LRU_C = 8.0
N_HEADS = 16
N_KV_HEADS = 4
HEAD_DIM = 64
WINDOW = 128
Q_DIM = N_HEADS * HEAD_DIM
KV_DIM = N_KV_HEADS * HEAD_DIM
N_GROUPS = 4
EXPERTS_PER_GROUP = 8
N_EXPERTS = N_GROUPS * EXPERTS_PER_GROUP
TOP_K = 2
D_EXPERT = 512
ALPHA = (2 * DEPTH) ** 0.25
LN_EPS = 1e-5
LOG2E = 1.4426950408889634

LANES = 128
SUBLANES = 8
VMEM_LIMIT = 56 * 1024 * 1024

REC_TS = 256
REC_GROUPS = REC_TS // SUBLANES
REC_NSEQ = 2
GATE_TILE = 256
GATE_WIN = 512
GATE_WIN_STARTS = (0, 128, 384, 640, 768)
N_GATE_TILES = D_RNN // GATE_TILE

ROUTE_T = 512
ROUTE_STEP = 2048
ROUTE_ROWS = 40

MOE_BM = 1024
MOE_SUB = 256

ATT_TQ = 256
ATT_NB = ATT_TQ // WINDOW
ATT_NSEQ = 2
ATT_PARTS = 4

COMB_T = 1024

D_HALF = D_MODEL // 2

SC_CORES = 2
SC_SUBCORES = 16
SC_WORKERS = SC_CORES * SC_SUBCORES
SC_WIN = 64


def _layer_norm_rows(z, g, b):
    mu = jnp.mean(z, axis=-1, keepdims=True)
    zc = z - mu
    var = jnp.mean(zc * zc, axis=-1, keepdims=True)
    return zc * lax.rsqrt(var + LN_EPS) * g + b


def _pack_bf16_pair(a, b):
    ua = lax.bitcast_convert_type(a.astype(BF16).astype(F32), U32)
    ub = lax.bitcast_convert_type(b.astype(BF16).astype(F32), U32)
    return (ua >> 16) | (ub & jnp.uint32(0xFFFF0000))


def _unpack_bf16_pair(w):
    a = lax.bitcast_convert_type(w << 16, F32)
    b = lax.bitcast_convert_type(w & jnp.uint32(0xFFFF0000), F32)
    return a, b


def _pack_row_halves(x):
    return _pack_bf16_pair(x[:, :D_HALF], x[:, D_HALF:])


def _router_logits_t(x, w_split, bias):
    xhi = x.astype(BF16)
    xlo = (x - xhi.astype(F32)).astype(BF16)
    nt = (((1,), (1,)), ((), ()))
    both = lax.dot_general(w_split, xhi, nt, preferred_element_type=F32)
    low = lax.dot_general(w_split[:ROUTE_ROWS], xlo, nt, preferred_element_type=F32)
    return both[:ROUTE_ROWS] + both[ROUTE_ROWS:] + low + bias


def _rglru_kernel(x_ref, perm_ref, perm_t_ref, w_in_ref, convw_ref, convb_ref, wg_ref, br_ref, bi_ref, lam_ref,
                  w_out_ref, g_ref, b_ref, wrt_ref, brt_ref, o_ref, opk_ref, lg_ref,
                  xr_ext, tail_sc, a_sc, u_sc, h_carry):
    s = pl.program_id(1)
    ts = REC_TS
    halo = (CONV_W - 1) * SUBLANES
    seqs = range(REC_NSEQ)

    @pl.when(s == 0)
    def _():
        tail_sc[...] = jnp.zeros((REC_NSEQ, halo, D_RNN), F32)
        h_carry[...] = jnp.zeros((REC_NSEQ, 1, D_RNN), F32)

    row = lax.broadcasted_iota(jnp.int32, (SUBLANES, D_RNN), 0)
    nlam = -lam_ref[...]
    sp = jnp.maximum(nlam, 0.0) + jnp.log1p(jnp.exp(-jnp.abs(nlam)))
    log2a_scale = (-LRU_C * LOG2E) * sp

    def project(q):
        xp = jnp.dot(perm_ref[...], x_ref[q].astype(BF16), preferred_element_type=F32).astype(BF16)
        proj = jnp.dot(xp, w_in_ref[...], preferred_element_type=F32)
        return proj[:, :D_RNN], proj[:, D_RNN:]

    def conv_gates(q, xr):
        for k in range(CONV_W - 1):
            r0 = ts - halo + k * SUBLANES
            cur = xr[r0:r0 + SUBLANES, :]
            prev = tail_sc[q, k * SUBLANES:(k + 1) * SUBLANES, :]
            xr_ext[q, k * SUBLANES:(k + 1) * SUBLANES, :] = jnp.where(
                row == 0, pltpu.roll(prev, 1, axis=0), pltpu.roll(cur, 1, axis=0))
        tail_sc[q] = xr[ts - halo:, :]
        xr_ext[q, halo:halo + ts, :] = xr
        xc = convb_ref[...] + convw_ref[CONV_W - 1:CONV_W, :] * xr
        for k in range(CONV_W - 1):
            xc = xc + convw_ref[k:k + 1, :] * xr_ext[q, k * SUBLANES:k * SUBLANES + ts, :]
        xcb = xc.astype(BF16)
        pres = [jnp.dot(xcb[:, GATE_WIN_STARTS[j]:GATE_WIN_STARTS[j] + GATE_WIN], wg_ref[j],
                        preferred_element_type=F32) for j in range(N_GATE_TILES)]
        for j, pre in enumerate(pres):
            cs = j * GATE_TILE
            r = jax.nn.sigmoid(pre[:, :GATE_TILE] + br_ref[:, cs:cs + GATE_TILE])
            i = jax.nn.sigmoid(pre[:, GATE_TILE:] + bi_ref[:, cs:cs + GATE_TILE])
            a = jnp.exp2(r * log2a_scale[:, cs:cs + GATE_TILE])
            s1 = 1.0 - a * a
            mult = jnp.where(s1 > 0.0, s1 * lax.rsqrt(s1), 0.0)
            u = mult * (i * xc[:, cs:cs + GATE_TILE])
            a_sc[q, :, cs:cs + GATE_TILE] = a
            u_sc[q, :, cs:cs + GATE_TILE] = u

    def segment_scan(q):
        h = jnp.zeros((SUBLANES, D_RNN), F32)
        prod = jnp.ones((SUBLANES, D_RNN), F32)
        for gidx in range(REC_GROUPS):
            rows = slice(gidx * SUBLANES, (gidx + 1) * SUBLANES)
            a8 = a_sc[q, rows, :]
            h = a8 * h + u_sc[q, rows, :]
            prod = a8 * prod
            u_sc[q, rows, :] = h
            a_sc[q, rows, :] = prod
        return h, prod

    def recur_out(q, gate, seg):
        seg_h, seg_a = seg
        for d in (1, 2, 4):
            keep = row >= d
            a_sh = jnp.where(keep, pltpu.roll(seg_a, d, axis=0), 1.0)
            h_sh = jnp.where(keep, pltpu.roll(seg_h, d, axis=0), 0.0)
            seg_h = seg_a * h_sh + seg_h
            seg_a = seg_a * a_sh
        h_in = h_carry[q]
        after = seg_a * h_in + seg_h
        enter = jnp.where(row == 0, h_in, pltpu.roll(after, 1, axis=0))
        h_carry[q] = after[SUBLANES - 1:SUBLANES, :]
        hs = (u_sc[q].reshape(REC_GROUPS, SUBLANES, D_RNN)
              + a_sc[q].reshape(REC_GROUPS, SUBLANES, D_RNN) * enter[None]).reshape(ts, D_RNN)
        y = hs * jax.nn.gelu(gate)
        y_t = jnp.dot(perm_t_ref[...], y.astype(BF16), preferred_element_type=F32).astype(BF16)
        return jnp.dot(y_t, w_out_ref[...], preferred_element_type=F32)

    def finish(q, out):
        z = ALPHA * x_ref[q] + out
        xn = _layer_norm_rows(z, g_ref[...], b_ref[...])
        o_ref[q] = xn
        opk_ref[q] = _pack_row_halves(xn)
        lg_ref[q] = _router_logits_t(xn, wrt_ref[...], brt_ref[...])

    assert REC_NSEQ == 2
    gate_a, xr_a = project(0)
    conv_gates(0, xr_a)
    gate_b, xr_b = project(1)
    conv_gates(1, xr_b)
    seg_a = segment_scan(0)
    seg_b = segment_scan(1)
    out_a = recur_out(0, gate_a, seg_a)
    out_b = recur_out(1, gate_b, seg_b)
    finish(0, out_a)
    finish(1, out_b)


def _band_gate_weights(w_r, w_i):
    spread = jnp.asarray(np.tile(np.eye(LRU_BLOCK_W, dtype=np.float32), (1, LRU_BLOCKS)), BF16)
    blk = np.arange(D_RNN) // LRU_BLOCK_W
    on_diag = jnp.asarray(blk[:, None] == blk[None, :])

    def dense(w):
        rows = w.reshape(D_RNN, LRU_BLOCK_W).astype(BF16)
        return jnp.where(on_diag, jnp.dot(rows, spread, preferred_element_type=F32), 0.0)

    wr, wi = dense(w_r), dense(w_i)
    tiles = []
    for j in range(N_GATE_TILES):
        ws = GATE_WIN_STARTS[j]
        cs = j * GATE_TILE
        lo_blk = cs // LRU_BLOCK_W
        hi_blk = (cs + GATE_TILE - 1) // LRU_BLOCK_W
        assert ws <= lo_blk * LRU_BLOCK_W and (hi_blk + 1) * LRU_BLOCK_W <= ws + GATE_WIN
        tiles.append(jnp.concatenate([wr[ws:ws + GATE_WIN, cs:cs + GATE_TILE],
                                      wi[ws:ws + GATE_WIN, cs:cs + GATE_TILE]], axis=1))
    return jnp.stack(tiles).astype(BF16)


def _rglru_layer(x, w_in, conv_w, conv_b, w_r, b_r, w_i, b_i, lam, w_out, ln_g, ln_b, router_w, router_b):
    B, S, D = x.shape
    wg = _band_gate_weights(w_r, w_i)
    rho = np.arange(REC_TS)
    perm_np = np.zeros((REC_TS, REC_TS), np.float32)
    perm_np[rho, (rho % SUBLANES) * REC_GROUPS + rho // SUBLANES] = 1.0
    perm = jnp.asarray(perm_np, BF16)
    perm_t = jnp.asarray(perm_np.T, BF16)
    row = lambda v: v.reshape(1, -1)
    const = lambda shape: pl.BlockSpec(shape, lambda b, s: (0,) * len(shape))
    tile = lambda w: pl.BlockSpec((REC_NSEQ, REC_TS, w), lambda b, s: (b, s, 0))
    halo = (CONV_W - 1) * SUBLANES
    return pl.pallas_call(
        _rglru_kernel,
        grid=(B // REC_NSEQ, S // REC_TS),
        in_specs=[
            tile(D),
            const((REC_TS, REC_TS)),
            const((REC_TS, REC_TS)),
            const((D, 2 * D_RNN)),
            const((CONV_W, D_RNN)),
            const((1, D_RNN)),
            const((N_GATE_TILES, GATE_WIN, 2 * GATE_TILE)),
            const((1, D_RNN)),
            const((1, D_RNN)),
            const((1, D_RNN)),
            const((D_RNN, D)),
            const((1, D)),
            const((1, D)),
            const((2 * ROUTE_ROWS, D)),
            const((ROUTE_ROWS, 1)),
        ],
        out_specs=[tile(D), tile(D_HALF),
                   pl.BlockSpec((REC_NSEQ, ROUTE_ROWS, REC_TS), lambda b, s: (b, 0, s))],
        out_shape=[jax.ShapeDtypeStruct((B, S, D), F32), jax.ShapeDtypeStruct((B, S, D_HALF), U32),
                   jax.ShapeDtypeStruct((B, ROUTE_ROWS, S), F32)],
        scratch_shapes=[
            pltpu.VMEM((REC_NSEQ, halo + REC_TS, D_RNN), F32),
            pltpu.VMEM((REC_NSEQ, halo, D_RNN), F32),
            pltpu.VMEM((REC_NSEQ, REC_TS, D_RNN), F32),
            pltpu.VMEM((REC_NSEQ, REC_TS, D_RNN), F32),
            pltpu.VMEM((REC_NSEQ, 1, D_RNN), F32),
        ],
        compiler_params=pltpu.CompilerParams(
            dimension_semantics=("arbitrary", "arbitrary"), vmem_limit_bytes=VMEM_LIMIT),
        name="rglru_ln",
    )(x, perm, perm_t, w_in.astype(BF16), conv_w, row(conv_b), wg, row(b_r), row(b_i), row(lam), w_out.astype(BF16),
      row(ln_g), row(ln_b), router_w, router_b)


def _router_kernel(logits_ref, tri_ref, idx_ref, gate_ref, cnt_ref, base_sc):
    step = pl.program_id(0)
    tr = ROUTE_T

    @pl.when(step == 0)
    def _():
        base_sc[...] = jnp.zeros((N_EXPERTS, 1), F32)

    row8 = lax.broadcasted_iota(jnp.int32, (SUBLANES, tr), 0).astype(F32)
    rowe = lax.broadcasted_iota(jnp.int32, (N_EXPERTS, tr), 0).astype(F32)
    neg_inf = -jnp.inf
    for sub in range(ROUTE_STEP // ROUTE_T):
        cols = slice(sub * tr, (sub + 1) * tr)
        _route_tile(logits_ref[0, :, cols], row8, rowe, neg_inf, tri_ref, idx_ref, gate_ref, base_sc, cols)
    cnt_ref[...] = jnp.broadcast_to(base_sc[...], (N_EXPERTS, LANES)).astype(jnp.int32)


def _route_tile(logits, row8, rowe, neg_inf, tri_ref, idx_ref, gate_ref, base_sc, cols):
    tr = ROUTE_T
    g = jnp.where(row8 < N_GROUPS, logits[N_EXPERTS:N_EXPERTS + SUBLANES, :], neg_inf)
    gmax = jnp.max(g, axis=0, keepdims=True)
    gidx = jnp.min(jnp.where(g == gmax, row8, SUBLANES), axis=0, keepdims=True)
    g_gate = 1.0 / jnp.sum(jnp.exp(g - gmax), axis=0, keepdims=True)

    esel = logits[0:EXPERTS_PER_GROUP, :]
    for grp in range(1, N_GROUPS):
        esel = jnp.where(gidx == grp, logits[grp * EXPERTS_PER_GROUP:(grp + 1) * EXPERTS_PER_GROUP, :], esel)
    v1 = jnp.max(esel, axis=0, keepdims=True)
    i1 = jnp.min(jnp.where(esel == v1, row8, SUBLANES), axis=0, keepdims=True)
    esel2 = jnp.where(row8 == i1, neg_inf, esel)
    v2 = jnp.max(esel2, axis=0, keepdims=True)
    i2 = jnp.min(jnp.where(esel2 == v2, row8, SUBLANES), axis=0, keepdims=True)
    e21 = jnp.exp(v2 - v1)
    inv = 1.0 / (1.0 + e21)
    gate1 = inv * g_gate
    gate2 = e21 * inv * g_gate
    e1 = gidx * EXPERTS_PER_GROUP + i1
    e2 = gidx * EXPERTS_PER_GROUP + i2

    hit1 = rowe == e1
    hit2 = rowe == e2
    member = jnp.where(hit1, 1.0, jnp.where(hit2, 1.0, 0.0))
    before = jnp.dot(member.astype(BF16), tri_ref[...], preferred_element_type=F32) + base_sc[...]
    rank1 = jnp.sum(jnp.where(hit1, before, 0.0), axis=0, keepdims=True)
    rank2 = jnp.sum(jnp.where(hit2, before, 0.0), axis=0, keepdims=True)
    base_sc[...] = base_sc[...] + jnp.sum(member, axis=1, keepdims=True)

    zi = jnp.zeros((1, tr), jnp.int32)
    idx_ref[:, cols] = jnp.concatenate(
        [e1.astype(jnp.int32), e2.astype(jnp.int32), rank1.astype(jnp.int32), rank2.astype(jnp.int32),
         zi, zi, zi, zi], axis=0)
    zf = jnp.zeros((1, tr), F32)
    gate_ref[:, cols] = jnp.concatenate([gate1, gate2, zf, zf, zf, zf, zf, zf], axis=0)


def _router_weights(w_rg, b_rg, w_re, b_re):
    D = w_rg.shape[0]
    pad_rows = ROUTE_ROWS - N_EXPERTS - N_GROUPS
    w = jnp.concatenate([w_re.T, w_rg.T, jnp.zeros((pad_rows, D), F32)], axis=0)
    whi = w.astype(BF16)
    wlo = (w - whi.astype(F32)).astype(BF16)
    w_split = jnp.concatenate([whi, wlo], axis=0)
    bias = jnp.concatenate([b_re, b_rg, jnp.zeros((pad_rows,), F32)]).reshape(ROUTE_ROWS, 1)
    return w_split, bias


def _router(logits_t):
    B, _, S = logits_t.shape
    T = B * S
    per_row = S // ROUTE_STEP
    tri = jnp.asarray(np.triu(np.ones((ROUTE_T, ROUTE_T), np.float32), 1), BF16)
    const = lambda shape: pl.BlockSpec(shape, lambda i: (0,) * len(shape))
    return pl.pallas_call(
        _router_kernel,
        grid=(T // ROUTE_STEP,),
        in_specs=[
            pl.BlockSpec((1, ROUTE_ROWS, ROUTE_STEP), lambda i: (i // per_row, 0, i % per_row)),
            const((ROUTE_T, ROUTE_T)),
        ],
        out_specs=[
            pl.BlockSpec((SUBLANES, ROUTE_STEP), lambda i: (0, i)),
            pl.BlockSpec((SUBLANES, ROUTE_STEP), lambda i: (0, i)),
            const((N_EXPERTS, LANES)),
        ],
        out_shape=[
            jax.ShapeDtypeStruct((SUBLANES, T), jnp.int32),
            jax.ShapeDtypeStruct((SUBLANES, T), F32),
            jax.ShapeDtypeStruct((N_EXPERTS, LANES), jnp.int32),
        ],
        scratch_shapes=[pltpu.VMEM((N_EXPERTS, 1), F32)],
        compiler_params=pltpu.CompilerParams(
            dimension_semantics=("arbitrary",), vmem_limit_bytes=VMEM_LIMIT),
        name="router",
    )(logits_t, tri)


def _sc_mesh():
    return plsc.VectorSubcoreMesh(core_axis_name="c", subcore_axis_name="s",
                                  num_cores=SC_CORES, num_subcores=SC_SUBCORES)


def _sc_worker_id():
    return lax.axis_index("s") * SC_CORES + lax.axis_index("c")


def _sc_scratch(n_win, width):
    return [
        pltpu.VMEM((n_win, SC_WIN), jnp.int32),
        pltpu.VMEM((n_win, SC_WIN), jnp.int32),
        pltpu.VMEM((2, SC_WIN, width), U32),
        pltpu.SemaphoreType.DMA((2,)),
        pltpu.SemaphoreType.DMA((2,)),
    ]


def _sc_dispatch(rows, idx1, idx2, n_rows):
    _, width = rows.shape
    _, n_win, _ = idx1.shape

    @functools.partial(
        pl.kernel, mesh=_sc_mesh(), out_type=jax.ShapeDtypeStruct((n_rows, width), rows.dtype),
        scratch_types=_sc_scratch(n_win, width), name="sc_dispatch")
    def run(rows_hbm, i1_hbm, i2_hbm, o_hbm, i1_v, i2_v, buf, rsem, wsem):
        wid = _sc_worker_id()
        base = wid * (n_win * SC_WIN)
        pltpu.sync_copy(i1_hbm.at[wid], i1_v)
        pltpu.sync_copy(i2_hbm.at[wid], i2_v)

        def read(j):
            return pltpu.async_copy(rows_hbm.at[pl.ds(base + j * SC_WIN, SC_WIN)], buf.at[j % 2], rsem.at[j % 2])

        reads = {0: read(0)}
        writes = {}
        for j in range(n_win):
            if j + 1 < n_win:
                for d in writes.pop(j - 1, ()):
                    d.wait()
                reads[j + 1] = read(j + 1)
            reads.pop(j).wait()
            writes[j] = (pltpu.async_copy(buf.at[j % 2], o_hbm.at[i1_v.at[j]], wsem.at[j % 2]),
                         pltpu.async_copy(buf.at[j % 2], o_hbm.at[i2_v.at[j]], wsem.at[j % 2]))
        for j in sorted(writes):
            for d in writes[j]:
                d.wait()

    return run(rows, idx1, idx2)


def _sc_gather_pair(table, idx1, idx2):
    _, width = table.shape
    _, n_win, _ = idx1.shape
    n_tok = SC_WORKERS * n_win * SC_WIN
    out_t = jax.ShapeDtypeStruct((n_tok, width), table.dtype)

    @functools.partial(
        pl.kernel, mesh=_sc_mesh(), out_type=(out_t, out_t),
        scratch_types=_sc_scratch(n_win, width), name="sc_combine_gather")
    def run(table_hbm, i1_hbm, i2_hbm, o1_hbm, o2_hbm, i1_v, i2_v, buf, gsem, wsem):
        wid = _sc_worker_id()
        base = wid * (n_win * SC_WIN)
        pltpu.sync_copy(i1_hbm.at[wid], i1_v)
        pltpu.sync_copy(i2_hbm.at[wid], i2_v)
        work = [(i1_v, o1_hbm, j) for j in range(n_win)] + [(i2_v, o2_hbm, j) for j in range(n_win)]

        def gather(t):
            iv, _, j = work[t]
            return pltpu.async_copy(table_hbm.at[iv.at[j]], buf.at[t % 2], gsem.at[t % 2])

        def put(t):
            _, oh, j = work[t]
            return pltpu.async_copy(buf.at[t % 2], oh.at[pl.ds(base + j * SC_WIN, SC_WIN)], wsem.at[t % 2])

        gathers = {0: gather(0)}
        puts = {}
        for t in range(len(work)):
            if t + 1 < len(work):
                if t - 1 in puts:
                    puts.pop(t - 1).wait()
                gathers[t + 1] = gather(t + 1)
            gathers.pop(t).wait()
            puts[t] = put(t)
        for t in sorted(puts):
            puts[t].wait()

    return run(table, idx1, idx2)


def _moe_kernel(layer, n_blocks, be_ref, valid_ref, x_ref, w1_hbm, w3_hbm, w2_hbm, o_ref,
                w1_st, w3_st, w2_st, w1_sc, w3_sc, w2_sc, sems, ordinal_sm):
    i = pl.program_id(0)
    expert = be_ref[i]
    new_expert = jnp.logical_or(i == 0, expert != be_ref[jnp.maximum(i - 1, 0)])

    def weight_copies(e, sl):
        return [pltpu.make_async_copy(hbm.at[layer, e], stage.at[sl], sems.at[k, sl])
                for k, (hbm, stage) in enumerate(((w1_hbm, w1_st), (w3_hbm, w3_st), (w2_hbm, w2_st)))]

    @pl.when(i == 0)
    def _():
        ordinal_sm[0] = -1
        for cp in weight_copies(expert, 0):
            cp.start()

    @pl.when(new_expert)
    def _():
        ordinal = ordinal_sm[0] + 1
        ordinal_sm[0] = ordinal
        slot = ordinal & 1
        for cp in weight_copies(expert, slot):
            cp.wait()
        w1_sc[...] = w1_st[slot].astype(BF16)
        w3_sc[...] = w3_st[slot].astype(BF16)
        w2_sc[...] = w2_st[slot].astype(BF16)
        nxt_pos = lax.while_loop(
            lambda p: jnp.logical_and(p < n_blocks, be_ref[jnp.minimum(p, n_blocks - 1)] == expert),
            lambda p: p + 1, i + 1)

        @pl.when(nxt_pos < n_blocks)
        def _():
            for cp in weight_copies(be_ref[jnp.minimum(nxt_pos, n_blocks - 1)], 1 - slot):
                cp.start()

    def up(rows):
        xa, xb = _unpack_bf16_pair(x_ref[rows, :])
        xa = xa.astype(BF16)
        xb = xb.astype(BF16)
        h1 = (jnp.dot(xa, w1_sc[:D_HALF], preferred_element_type=F32)
              + jnp.dot(xb, w1_sc[D_HALF:], preferred_element_type=F32))
        h3 = (jnp.dot(xa, w3_sc[:D_HALF], preferred_element_type=F32)
              + jnp.dot(xb, w3_sc[D_HALF:], preferred_element_type=F32))
        return h1, h3

    def down(rows, h1, h3):
        hdn = (jax.nn.silu(h1) * h3).astype(BF16)
        y = jnp.dot(hdn, w2_sc[...], preferred_element_type=F32)
        o_ref[rows, :] = _pack_row_halves(y)

    n_sub = MOE_BM // MOE_SUB
    subs = [slice(k * MOE_SUB, (k + 1) * MOE_SUB) for k in range(n_sub)]
    valid = valid_ref[i]
    chains = (valid + (MOE_SUB - 1)) // MOE_SUB

    for live in range(n_sub + 1):
        @pl.when(chains == live)
        def _(live=live):
            ups = {}
            if live:
                ups[0] = up(subs[0])
            for k in range(live):
                if k + 1 < live:
                    ups[k + 1] = up(subs[k + 1])
                down(subs[k], *ups.pop(k))
            if live < n_sub:
                o_ref[live * MOE_SUB:, :] = jnp.zeros((MOE_BM - live * MOE_SUB, D_HALF), o_ref.dtype)


def _moe_blocks(xbuf, block_e, valid, w1, w3, w2, layer):
    n_rows, _ = xbuf.shape
    D = D_MODEL
    n_blocks = n_rows // MOE_BM
    rows = lambda i, be, nu: (i, 0)
    grid_spec = pltpu.PrefetchScalarGridSpec(
        num_scalar_prefetch=2,
        grid=(n_blocks,),
        in_specs=[
            pl.BlockSpec((MOE_BM, D_HALF), rows),
            pl.BlockSpec(memory_space=pl.ANY),
            pl.BlockSpec(memory_space=pl.ANY),
            pl.BlockSpec(memory_space=pl.ANY),
        ],
        out_specs=pl.BlockSpec((MOE_BM, D_HALF), rows),
        scratch_shapes=[
            pltpu.VMEM((2, D, D_EXPERT), F32),
            pltpu.VMEM((2, D, D_EXPERT), F32),
            pltpu.VMEM((2, D_EXPERT, D), F32),
            pltpu.VMEM((D, D_EXPERT), BF16),
            pltpu.VMEM((D, D_EXPERT), BF16),
            pltpu.VMEM((D_EXPERT, D), BF16),
            pltpu.SemaphoreType.DMA((3, 2)),
            pltpu.SMEM((1,), jnp.int32),
        ],
    )
    return pl.pallas_call(
        functools.partial(_moe_kernel, layer, n_blocks),
        grid_spec=grid_spec,
        out_shape=jax.ShapeDtypeStruct((n_rows, D_HALF), U32),
        compiler_params=pltpu.CompilerParams(
            dimension_semantics=("arbitrary",), vmem_limit_bytes=VMEM_LIMIT),
        name="moe_experts",
    )(block_e, valid, xbuf, w1, w3, w2)


def _moe_combine_norm(x, y1, y2, gate_rows, g, b):
    n = x.shape[0]
    pad = jnp.zeros((LANES - SUBLANES, n), F32)
    gates = jnp.concatenate([gate_rows, pad], axis=0).T
    g1 = gates[:, 0:1]
    g2 = gates[:, 1:2]
    a1, b1 = _unpack_bf16_pair(y1)
    a2, b2 = _unpack_bf16_pair(y2)
    f = jnp.concatenate([g1 * a1 + g2 * a2, g1 * b1 + g2 * b2], axis=1)
    return _layer_norm_rows(ALPHA * x + f, g, b)


def _combine_kernel(x_ref, y1_ref, y2_ref, gates_ref, g_ref, b_ref, o_ref):
    o_ref[...] = _moe_combine_norm(x_ref[...], y1_ref[...], y2_ref[...], gates_ref[...], g_ref[...], b_ref[...])


def _combine_part_kernel(x_ref, y1_ref, y2_ref, gates_ref, g_ref, b_ref, prev_ref, o_ref):
    del prev_ref
    _combine_kernel(x_ref, y1_ref, y2_ref, gates_ref, g_ref, b_ref, o_ref)


def _combine_ln(xf, y_parts, gates, ln_g, ln_b):
    T, D = xf.shape
    n_parts = len(y_parts)
    steps = T // (COMB_T * n_parts)
    const = lambda shape: pl.BlockSpec(shape, lambda i: (0,) * len(shape))
    out = None
    for part, (y1, y2) in enumerate(y_parts):
        off = part * steps
        glob = lambda w, off=off: pl.BlockSpec((COMB_T, w), lambda i: (i + off, 0))
        local = lambda w: pl.BlockSpec((COMB_T, w), lambda i: (i, 0))
        in_specs = [glob(D), local(D_HALF), local(D_HALF),
                    pl.BlockSpec((SUBLANES, COMB_T), lambda i, off=off: (0, i + off)), const((1, D)), const((1, D))]
        args = [xf, y1, y2, gates, ln_g.reshape(1, D), ln_b.reshape(1, D)]
        if out is not None:
            in_specs.append(pl.BlockSpec(memory_space=pl.ANY))
            args.append(out)
        out = pl.pallas_call(
            _combine_kernel if out is None else _combine_part_kernel,
            grid=(steps,),
            in_specs=in_specs,
            out_specs=glob(D),
            out_shape=jax.ShapeDtypeStruct((T, D), F32),
            input_output_aliases={} if out is None else {len(args) - 1: 0},
            compiler_params=pltpu.CompilerParams(
                dimension_semantics=("arbitrary",), vmem_limit_bytes=VMEM_LIMIT),
            name="moe_combine_ln",
        )(*args)
    return out


def _hier_moe(xpk, logits_t, w1, w3, w2, layer, gather_parts):
    B, S, _ = xpk.shape
    D = D_MODEL
    T = B * S
    idx, gates, cnt = _router(logits_t)
    counts = cnt[:, 0]
    padded = ((counts + MOE_BM - 1) // MOE_BM) * MOE_BM
    pends = jnp.cumsum(padded)
    pstarts = pends - padded
    experts = jnp.arange(N_EXPERTS, dtype=jnp.int32)[:, None]

    def dest(e_row, rank_row):
        return jnp.sum(jnp.where(e_row[None, :] == experts, pstarts[:, None], 0), axis=0) + rank_row

    n_win = T // (SC_WORKERS * SC_WIN)
    dest1 = dest(idx[0], idx[2]).reshape(SC_WORKERS, n_win, SC_WIN)
    dest2 = dest(idx[1], idx[3]).reshape(SC_WORKERS, n_win, SC_WIN)
    n_blocks = -(-(T * TOP_K + N_EXPERTS * (MOE_BM - 1)) // MOE_BM)
    n_rows = n_blocks * MOE_BM
    block_start = jnp.arange(n_blocks, dtype=jnp.int32) * MOE_BM
    block_e = jnp.minimum(jnp.sum(block_start[:, None] >= pends[None, :], axis=1), N_EXPERTS - 1).astype(jnp.int32)
    of_block = block_e[:, None] == experts[:, 0][None, :]
    pick = lambda v: jnp.sum(jnp.where(of_block, v[None, :], 0), axis=1)
    valid = jnp.clip(pick(counts) - (block_start - pick(pstarts)), 0, MOE_BM).astype(jnp.int32)
    xbuf = _sc_dispatch(xpk.reshape(T, D_HALF), dest1, dest2, n_rows)
    ybuf = _moe_blocks(xbuf, block_e, valid, w1, w3, w2, layer)
    part_shape = (gather_parts, SC_WORKERS, n_win // gather_parts, SC_WIN)
    d1, d2 = dest1.reshape(part_shape), dest2.reshape(part_shape)
    y_parts = [_sc_gather_pair(ybuf, d1[p], d2[p]) for p in range(gather_parts)]
    return y_parts, gates


def _att_head_order():
    order = []
    for p in range(N_HEADS // 2):
        jj, m = divmod(p, 4)
        order += [8 * jj + m, 8 * jj + 4 + m]
    return order


ATT_HEAD_ORDER = _att_head_order()


def _attn_kernel(xprev_ref, y1_ref, y2_ref, gates_a_ref, gates_b_ref, g_prev_ref, b_prev_ref, wqkv_ref, bias_ref,
                 sink_ref, wo_ref, g_ref, b_ref, wrt_ref, brt_ref, o_ref, opk_ref, lg_ref,
                 kv_ext, o_sc, s_sc0, s_sc1, p_sc0, p_sc1):
    s = pl.program_id(1)
    tq = ATT_TQ
    s_bufs = (s_sc0, s_sc1)
    p_bufs = (p_sc0, p_sc1)
    gate_refs = (gates_a_ref, gates_b_ref)
    assert ATT_NSEQ == len(gate_refs)

    @pl.when(s == 0)
    def _():
        kv_ext[:, 0:WINDOW, :] = jnp.zeros((ATT_NSEQ, WINDOW, 2 * KV_DIM), BF16)

    def layer_input(sq):
        return _moe_combine_norm(xprev_ref[sq], y1_ref[sq], y2_ref[sq], gate_refs[sq][...],
                                 g_prev_ref[...], b_prev_ref[...])

    def project_qkv(sq, x):
        qkv = jnp.dot(x.astype(BF16), wqkv_ref[...], preferred_element_type=F32)
        kv_ext[sq, WINDOW:WINDOW + tq, :] = qkv[:, Q_DIM:].astype(BF16)
        return (qkv[:, :Q_DIM] * (HEAD_DIM ** -0.5 * LOG2E)).astype(BF16)

    lane = lax.broadcasted_iota(jnp.int32, (WINDOW, LANES), 1)
    low = lane < HEAD_DIM
    sub = lax.broadcasted_iota(jnp.int32, (LANES, WINDOW), 0)
    top = sub < HEAD_DIM
    first = jnp.where(s == 0, 1, 0)
    nt = (((1,), (1,)), ((), ()))
    zero = jnp.zeros((), BF16)

    tiles = [(sq, n, j) for n in range(ATT_NB) for j in range(2) for sq in range(ATT_NSEQ)]

    def scores(t, qs_all):
        sq, n, j = tiles[t]
        q = qs_all[sq]
        r0 = n * WINDOW
        k_tile = kv_ext[sq, r0:r0 + 2 * WINDOW, j * LANES:(j + 1) * LANES]
        parts = []
        for m in range(4):
            p = 4 * j + m
            qt = q[r0:r0 + WINDOW, p * LANES:(p + 1) * LANES]
            parts.append(jnp.where(low, qt, zero))
            parts.append(jnp.where(low, zero, qt))
        qs = jnp.concatenate(parts, axis=0)
        bias_sel = first if n == 0 else 0
        s_bufs[t % 2][...] = (lax.dot_general(k_tile, qs, nt, preferred_element_type=F32)
                              + bias_ref[bias_sel, j])

    def softmax_pv(t):
        sq, n, j = tiles[t]
        r0 = n * WINDOW
        s_sc = s_bufs[t % 2]
        p_sc = p_bufs[t % 2]
        inv_l = []
        for h in range(8):
            hc = slice(h * WINDOW, (h + 1) * WINDOW)
            sink = sink_ref[8 * j + h] * LOG2E
            mx = jnp.maximum(jnp.max(s_sc[:, hc], axis=0, keepdims=True), sink)
            pr = jnp.exp2(s_sc[:, hc] - mx)
            p_sc[:, hc] = pr.astype(BF16)
            inv_l.append(1.0 / (jnp.sum(pr, axis=0, keepdims=True) + jnp.exp2(sink - mx)))
        v_tile = kv_ext[sq, r0:r0 + 2 * WINDOW, KV_DIM + j * LANES:KV_DIM + (j + 1) * LANES]
        tn = (((0,), (0,)), ((), ()))
        ov = lax.dot_general(v_tile, p_sc[...], tn, preferred_element_type=F32)
        for m in range(4):
            p = 4 * j + m
            o_even = ov[:, (2 * m) * WINDOW:(2 * m + 1) * WINDOW] * inv_l[2 * m]
            o_odd = ov[:, (2 * m + 1) * WINDOW:(2 * m + 2) * WINDOW] * inv_l[2 * m + 1]
            o_sc[sq, p * LANES:(p + 1) * LANES, r0:r0 + WINDOW] = jnp.where(top, o_even, o_odd).astype(BF16)

    def project_out(sq):
        tn = (((0,), (0,)), ((), ()))
        return lax.dot_general(o_sc[sq], wo_ref[...], tn, preferred_element_type=F32)

    def finish(sq, x, out):
        xn = _layer_norm_rows(ALPHA * x + out, g_ref[...], b_ref[...])
        o_ref[sq] = xn
        opk_ref[sq] = _pack_row_halves(xn)
        lg_ref[sq] = _router_logits_t(xn, wrt_ref[...], brt_ref[...])

    xs = [layer_input(0)]
    qs_all = [project_qkv(0, xs[0])]
    xs.append(layer_input(1))
    qs_all.append(project_qkv(1, xs[1]))
    scores(0, qs_all)
    for t in range(len(tiles)):
        if t + 1 < len(tiles):
            scores(t + 1, qs_all)
        softmax_pv(t)
    kv_ext[:, 0:WINDOW, :] = kv_ext[:, tq:tq + WINDOW, :]
    outs = [project_out(sq) for sq in range(ATT_NSEQ)]
    for sq in range(ATT_NSEQ):
        finish(sq, xs[sq], outs[sq])


def _attn_bias():
    qi = np.arange(WINDOW)[:, None]
    sj = np.arange(2 * WINDOW)[None, :]
    dist = qi - sj + WINDOW
    valid = (dist >= 0) & (dist < WINDOW)
    slopes = 2.0 ** (-8.0 * np.arange(1, N_HEADS + 1, dtype=np.float32) / N_HEADS)
    slopes = slopes.astype(np.float32)[ATT_HEAD_ORDER]
    sb = -(slopes[:, None, None] * dist.astype(np.float32)[None])
    later = np.where(valid[None], sb, -np.inf)
    first = np.where((valid & (sj >= WINDOW))[None], sb, -np.inf)
    bias = np.stack([later, first]).astype(np.float32) * np.float32(LOG2E)
    bias = bias.reshape(2, 2, 8, WINDOW, 2 * WINDOW).transpose(0, 1, 4, 2, 3).reshape(2, 2, 2 * WINDOW, 8 * WINDOW)
    return jnp.asarray(np.ascontiguousarray(bias))


ATT_N_INPUTS = 15


def _attn_part_kernel(*refs):
    _attn_kernel(*refs[:ATT_N_INPUTS], *refs[ATT_N_INPUTS + 3:])


def _attn_layer(x_prev, y_parts, gates, g_prev, b_prev, w_qkv, sinks, w_o, ln_g, ln_b, router_w, router_b):
    B, S, D = x_prev.shape
    steps = S // ATT_TQ
    assert ATT_HEAD_ORDER == list(np.arange(N_HEADS).reshape(2, 2, 4).transpose(0, 2, 1).reshape(-1))
    wq = w_qkv[:, :Q_DIM].reshape(D, 2, 2, 4, HEAD_DIM).transpose(0, 1, 3, 2, 4).reshape(D, Q_DIM)
    wqkv = jnp.concatenate([wq, w_qkv[:, Q_DIM:]], axis=1).astype(BF16)
    wo_t = w_o.reshape(2, 2, 4, HEAD_DIM, D).transpose(0, 2, 1, 3, 4).reshape(Q_DIM, D).astype(BF16)
    sink = sinks.reshape(2, 2, 4).transpose(0, 2, 1).reshape(N_HEADS, 1, 1)
    bias = _attn_bias()
    const = lambda shape: pl.BlockSpec(shape, lambda b, s: (0,) * len(shape))
    n_parts = len(y_parts)
    rows_per_part = B // n_parts
    pairs = rows_per_part // ATT_NSEQ
    outs = None
    for part, (y1, y2) in enumerate(y_parts):
        off = part * pairs
        glob = lambda w, off=off: pl.BlockSpec((ATT_NSEQ, ATT_TQ, w), lambda b, s: (b + off, s, 0))
        local = lambda w: pl.BlockSpec((ATT_NSEQ, ATT_TQ, w), lambda b, s: (b, s, 0))
        gate_rows = lambda sq, off=off: pl.BlockSpec(
            (SUBLANES, ATT_TQ), lambda b, s: (0, (ATT_NSEQ * (b + off) + sq) * steps + s))
        in_specs = [
            glob(D),
            local(D_HALF),
            local(D_HALF),
            gate_rows(0),
            gate_rows(1),
            const((1, D)),
            const((1, D)),
            const((D, Q_DIM + 2 * KV_DIM)),
            const((2, 2, 2 * WINDOW, 8 * WINDOW)),
            const((N_HEADS, 1, 1)),
            const((Q_DIM, D)),
            const((1, D)),
            const((1, D)),
            const((2 * ROUTE_ROWS, D)),
            const((ROUTE_ROWS, 1)),
        ]
        args = [x_prev, y1.reshape(rows_per_part, S, D_HALF), y2.reshape(rows_per_part, S, D_HALF), gates, gates,
                g_prev.reshape(1, D), b_prev.reshape(1, D), wqkv, bias, sink, wo_t,
                ln_g.reshape(1, D), ln_b.reshape(1, D), router_w, router_b]
        aliases = {}
        body = _attn_kernel
        if outs is not None:
            aliases = {len(args) + k: k for k in range(3)}
            in_specs += [pl.BlockSpec(memory_space=pl.ANY)] * 3
            args += list(outs)
            body = _attn_part_kernel
        outs = pl.pallas_call(
            body,
            grid=(pairs, steps),
            in_specs=in_specs,
            out_specs=[glob(D), glob(D_HALF),
                       pl.BlockSpec((ATT_NSEQ, ROUTE_ROWS, ATT_TQ), lambda b, s, off=off: (b + off, 0, s))],
            out_shape=[jax.ShapeDtypeStruct((B, S, D), F32), jax.ShapeDtypeStruct((B, S, D_HALF), U32),
                       jax.ShapeDtypeStruct((B, ROUTE_ROWS, S), F32)],
            input_output_aliases=aliases,
            scratch_shapes=[
                pltpu.VMEM((ATT_NSEQ, ATT_TQ + WINDOW, 2 * KV_DIM), BF16),
                pltpu.VMEM((ATT_NSEQ, Q_DIM, ATT_TQ), BF16),
                pltpu.VMEM((2 * WINDOW, 8 * WINDOW), F32),
                pltpu.VMEM((2 * WINDOW, 8 * WINDOW), F32),
                pltpu.VMEM((2 * WINDOW, 8 * WINDOW), BF16),
                pltpu.VMEM((2 * WINDOW, 8 * WINDOW), BF16),
            ],
            compiler_params=pltpu.CompilerParams(
                dimension_semantics=("arbitrary", "arbitrary"), vmem_limit_bytes=VMEM_LIMIT),
            name="swa_attn_ln",
        )(*args)
    return outs


def kernel(x, rec_w_in, rec_conv_w, rec_conv_b, rec_w_r, rec_b_r, rec_w_i, rec_b_i, rec_lambda, rec_w_out,
           att_w_qkv, att_sinks, att_w_o, moe_w_group, moe_b_group, moe_w_expert, moe_b_expert,
           moe_w1, moe_w3, moe_w2, ln_g, ln_b):
    assert DEPTH == 2
    B, S, D = x.shape

    router = [_router_weights(moe_w_group[layer], moe_b_group[layer], moe_w_expert[layer], moe_b_expert[layer])
              for layer in range(DEPTH)]

    x1, x1_pk, logits1 = _rglru_layer(x, rec_w_in[0], rec_conv_w[0], rec_conv_b[0], rec_w_r[0], rec_b_r[0],
                                      rec_w_i[0], rec_b_i[0], rec_lambda[0], rec_w_out[0], ln_g[0, 0], ln_b[0, 0],
                                      *router[0])
    y_parts, gates = _hier_moe(x1_pk, logits1, moe_w1, moe_w3, moe_w2, 0, gather_parts=ATT_PARTS)
    x3, x3_pk, logits3 = _attn_layer(x1, y_parts, gates, ln_g[0, 1], ln_b[0, 1], att_w_qkv[0], att_sinks[0],
                                     att_w_o[0], ln_g[1, 0], ln_b[1, 0], *router[1])
    y_parts, gates = _hier_moe(x3_pk, logits3, moe_w1, moe_w3, moe_w2, 1, gather_parts=1)
    out = _combine_ln(x3.reshape(B * S, D), y_parts, gates, ln_g[1, 1], ln_b[1, 1])
    return out.reshape(B, S, D)
```

```python
import functools

import jax
import jax.numpy as jnp
import numpy as np
from jax import lax
from jax.experimental import pallas as pl
from jax.experimental.pallas import tpu as pltpu
from jax.experimental.pallas import tpu_sc as plsc

F32 = jnp.float32
BF16 = jnp.bfloat16
U32 = jnp.uint32

D_MODEL = 1024
DEPTH = 2
D_RNN = 1280
LRU_BLOCKS = 16
LRU_BLOCK_W = D_RNN // LRU_BLOCKS
CONV_W = 4
LRU_C = 8.0
N_HEADS = 16
N_KV_HEADS = 4
HEAD_DIM = 64
WINDOW = 128
Q_DIM = N_HEADS * HEAD_DIM
KV_DIM = N_KV_HEADS * HEAD_DIM
N_GROUPS = 4
EXPERTS_PER_GROUP = 8
N_EXPERTS = N_GROUPS * EXPERTS_PER_GROUP
TOP_K = 2
D_EXPERT = 512
ALPHA = (2 * DEPTH) ** 0.25
LN_EPS = 1e-5
LOG2E = 1.4426950408889634

LANES = 128
SUBLANES = 8
VMEM_LIMIT = 56 * 1024 * 1024

REC_TS = 256
REC_GROUPS = REC_TS // SUBLANES
REC_NSEQ = 2
GATE_TILE = 256
GATE_WIN = 512
GATE_WIN_STARTS = (0, 128, 384, 640, 768)
N_GATE_TILES = D_RNN // GATE_TILE

ROUTE_T = 512
ROUTE_STEP = 2048
ROUTE_ROWS = 40

MOE_BM = 1024
MOE_SUB = 256

ATT_TQ = 256
ATT_NB = ATT_TQ // WINDOW
ATT_NSEQ = 2
ATT_PARTS = 4

COMB_T = 1024

D_HALF = D_MODEL // 2

SC_CORES = 2
SC_SUBCORES = 16
SC_WORKERS = SC_CORES * SC_SUBCORES
SC_WIN = 64


def _layer_norm_rows(z, g, b):
    mu = jnp.mean(z, axis=-1, keepdims=True)
    zc = z - mu
    var = jnp.mean(zc * zc, axis=-1, keepdims=True)
    return zc * lax.rsqrt(var + LN_EPS) * g + b


def _pack_bf16_pair(a, b):
    ua = lax.bitcast_convert_type(a.astype(BF16).astype(F32), U32)
    ub = lax.bitcast_convert_type(b.astype(BF16).astype(F32), U32)
    return (ua >> 16) | (ub & jnp.uint32(0xFFFF0000))


def _unpack_bf16_pair(w):
    a = lax.bitcast_convert_type(w << 16, F32)
    b = lax.bitcast_convert_type(w & jnp.uint32(0xFFFF0000), F32)
    return a, b


def _pack_row_halves(x):
    return _pack_bf16_pair(x[:, :D_HALF], x[:, D_HALF:])


def _router_logits_t(x, w_split, bias):
    xhi = x.astype(BF16)
    xlo = (x - xhi.astype(F32)).astype(BF16)
    nt = (((1,), (1,)), ((), ()))
    both = lax.dot_general(w_split, xhi, nt, preferred_element_type=F32)
    low = lax.dot_general(w_split[:ROUTE_ROWS], xlo, nt, preferred_element_type=F32)
    return both[:ROUTE_ROWS] + both[ROUTE_ROWS:] + low + bias


def _rglru_kernel(x_ref, perm_ref, perm_t_ref, w_in_ref, convw_ref, convb_ref, wg_ref, br_ref, bi_ref, lam_ref,
                  w_out_ref, g_ref, b_ref, wrt_ref, brt_ref, o_ref, opk_ref, lg_ref,
                  xr_ext, tail_sc, a_sc, u_sc, h_carry):
    s = pl.program_id(1)
    ts = REC_TS
    halo = (CONV_W - 1) * SUBLANES
    seqs = range(REC_NSEQ)

    @pl.when(s == 0)
    def _():
        tail_sc[...] = jnp.zeros((REC_NSEQ, halo, D_RNN), F32)
        h_carry[...] = jnp.zeros((REC_NSEQ, 1, D_RNN), F32)

    row = lax.broadcasted_iota(jnp.int32, (SUBLANES, D_RNN), 0)
    nlam = -lam_ref[...]
    sp = jnp.maximum(nlam, 0.0) + jnp.log1p(jnp.exp(-jnp.abs(nlam)))
    log2a_scale = (-LRU_C * LOG2E) * sp

    def project(q):
        xp = jnp.dot(perm_ref[...], x_ref[q].astype(BF16), preferred_element_type=F32).astype(BF16)
        proj = jnp.dot(xp, w_in_ref[...], preferred_element_type=F32)
        return proj[:, :D_RNN], proj[:, D_RNN:]

    def conv_gates(q, xr):
        for k in range(CONV_W - 1):
            r0 = ts - halo + k * SUBLANES
            cur = xr[r0:r0 + SUBLANES, :]
            prev = tail_sc[q, k * SUBLANES:(k + 1) * SUBLANES, :]
            xr_ext[q, k * SUBLANES:(k + 1) * SUBLANES, :] = jnp.where(
                row == 0, pltpu.roll(prev, 1, axis=0), pltpu.roll(cur, 1, axis=0))
        tail_sc[q] = xr[ts - halo:, :]
        xr_ext[q, halo:halo + ts, :] = xr
        xc = convb_ref[...] + convw_ref[CONV_W - 1:CONV_W, :] * xr
        for k in range(CONV_W - 1):
            xc = xc + convw_ref[k:k + 1, :] * xr_ext[q, k * SUBLANES:k * SUBLANES + ts, :]
        xcb = xc.astype(BF16)
        pres = [jnp.dot(xcb[:, GATE_WIN_STARTS[j]:GATE_WIN_STARTS[j] + GATE_WIN], wg_ref[j],
                        preferred_element_type=F32) for j in range(N_GATE_TILES)]
        for j, pre in enumerate(pres):
            cs = j * GATE_TILE
            r = jax.nn.sigmoid(pre[:, :GATE_TILE] + br_ref[:, cs:cs + GATE_TILE])
            i = jax.nn.sigmoid(pre[:, GATE_TILE:] + bi_ref[:, cs:cs + GATE_TILE])
            a = jnp.exp2(r * log2a_scale[:, cs:cs + GATE_TILE])
            s1 = 1.0 - a * a
            mult = jnp.where(s1 > 0.0, s1 * lax.rsqrt(s1), 0.0)
            u = mult * (i * xc[:, cs:cs + GATE_TILE])
            a_sc[q, :, cs:cs + GATE_TILE] = a
            u_sc[q, :, cs:cs + GATE_TILE] = u

    def segment_scan(q):
        h = jnp.zeros((SUBLANES, D_RNN), F32)
        prod = jnp.ones((SUBLANES, D_RNN), F32)
        for gidx in range(REC_GROUPS):
            rows = slice(gidx * SUBLANES, (gidx + 1) * SUBLANES)
            a8 = a_sc[q, rows, :]
            h = a8 * h + u_sc[q, rows, :]
            prod = a8 * prod
            u_sc[q, rows, :] = h
            a_sc[q, rows, :] = prod
        return h, prod

    def recur_out(q, gate, seg):
        seg_h, seg_a = seg
        for d in (1, 2, 4):
            keep = row >= d
            a_sh = jnp.where(keep, pltpu.roll(seg_a, d, axis=0), 1.0)
            h_sh = jnp.where(keep, pltpu.roll(seg_h, d, axis=0), 0.0)
            seg_h = seg_a * h_sh + seg_h
            seg_a = seg_a * a_sh
        h_in = h_carry[q]
        after = seg_a * h_in + seg_h
        enter = jnp.where(row == 0, h_in, pltpu.roll(after, 1, axis=0))
        h_carry[q] = after[SUBLANES - 1:SUBLANES, :]
        hs = (u_sc[q].reshape(REC_GROUPS, SUBLANES, D_RNN)
              + a_sc[q].reshape(REC_GROUPS, SUBLANES, D_RNN) * enter[None]).reshape(ts, D_RNN)
        y = hs * jax.nn.gelu(gate)
        y_t = jnp.dot(perm_t_ref[...], y.astype(BF16), preferred_element_type=F32).astype(BF16)
        return jnp.dot(y_t, w_out_ref[...], preferred_element_type=F32)

    def finish(q, out):
        z = ALPHA * x_ref[q] + out
        xn = _layer_norm_rows(z, g_ref[...], b_ref[...])
        o_ref[q] = xn
        opk_ref[q] = _pack_row_halves(xn)
        lg_ref[q] = _router_logits_t(xn, wrt_ref[...], brt_ref[...])

    assert REC_NSEQ == 2
    gate_a, xr_a = project(0)
    conv_gates(0, xr_a)
    gate_b, xr_b = project(1)
    conv_gates(1, xr_b)
    seg_a = segment_scan(0)
    seg_b = segment_scan(1)
    out_a = recur_out(0, gate_a, seg_a)
    out_b = recur_out(1, gate_b, seg_b)
    finish(0, out_a)
    finish(1, out_b)


def _band_gate_weights(w_r, w_i):
    spread = jnp.asarray(np.tile(np.eye(LRU_BLOCK_W, dtype=np.float32), (1, LRU_BLOCKS)), BF16)
    blk = np.arange(D_RNN) // LRU_BLOCK_W
    on_diag = jnp.asarray(blk[:, None] == blk[None, :])

    def dense(w):
        rows = w.reshape(D_RNN, LRU_BLOCK_W).astype(BF16)
        return jnp.where(on_diag, jnp.dot(rows, spread, preferred_element_type=F32), 0.0)

    wr, wi = dense(w_r), dense(w_i)
    tiles = []
    for j in range(N_GATE_TILES):
        ws = GATE_WIN_STARTS[j]
        cs = j * GATE_TILE
        lo_blk = cs // LRU_BLOCK_W
        hi_blk = (cs + GATE_TILE - 1) // LRU_BLOCK_W
        assert ws <= lo_blk * LRU_BLOCK_W and (hi_blk + 1) * LRU_BLOCK_W <= ws + GATE_WIN
        tiles.append(jnp.concatenate([wr[ws:ws + GATE_WIN, cs:cs + GATE_TILE],
                                      wi[ws:ws + GATE_WIN, cs:cs + GATE_TILE]], axis=1))
    return jnp.stack(tiles).astype(BF16)


def _rglru_layer(x, w_in, conv_w, conv_b, w_r, b_r, w_i, b_i, lam, w_out, ln_g, ln_b, router_w, router_b):
    B, S, D = x.shape
    wg = _band_gate_weights(w_r, w_i)
    rho = np.arange(REC_TS)
    perm_np = np.zeros((REC_TS, REC_TS), np.float32)
    perm_np[rho, (rho % SUBLANES) * REC_GROUPS + rho // SUBLANES] = 1.0
    perm = jnp.asarray(perm_np, BF16)
    perm_t = jnp.asarray(perm_np.T, BF16)
    row = lambda v: v.reshape(1, -1)
    const = lambda shape: pl.BlockSpec(shape, lambda b, s: (0,) * len(shape))
    tile = lambda w: pl.BlockSpec((REC_NSEQ, REC_TS, w), lambda b, s: (b, s, 0))
    halo = (CONV_W - 1) * SUBLANES
    return pl.pallas_call(
        _rglru_kernel,
        grid=(B // REC_NSEQ, S // REC_TS),
        in_specs=[
            tile(D),
            const((REC_TS, REC_TS)),
            const((REC_TS, REC_TS)),
            const((D, 2 * D_RNN)),
            const((CONV_W, D_RNN)),
            const((1, D_RNN)),
            const((N_GATE_TILES, GATE_WIN, 2 * GATE_TILE)),
            const((1, D_RNN)),
            const((1, D_RNN)),
            const((1, D_RNN)),
            const((D_RNN, D)),
            const((1, D)),
            const((1, D)),
            const((2 * ROUTE_ROWS, D)),
            const((ROUTE_ROWS, 1)),
        ],
        out_specs=[tile(D), tile(D_HALF),
                   pl.BlockSpec((REC_NSEQ, ROUTE_ROWS, REC_TS), lambda b, s: (b, 0, s))],
        out_shape=[jax.ShapeDtypeStruct((B, S, D), F32), jax.ShapeDtypeStruct((B, S, D_HALF), U32),
                   jax.ShapeDtypeStruct((B, ROUTE_ROWS, S), F32)],
        scratch_shapes=[
            pltpu.VMEM((REC_NSEQ, halo + REC_TS, D_RNN), F32),
            pltpu.VMEM((REC_NSEQ, halo, D_RNN), F32),
            pltpu.VMEM((REC_NSEQ, REC_TS, D_RNN), F32),
            pltpu.VMEM((REC_NSEQ, REC_TS, D_RNN), F32),
            pltpu.VMEM((REC_NSEQ, 1, D_RNN), F32),
        ],
        compiler_params=pltpu.CompilerParams(
            dimension_semantics=("arbitrary", "arbitrary"), vmem_limit_bytes=VMEM_LIMIT),
        name="rglru_ln",
    )(x, perm, perm_t, w_in.astype(BF16), conv_w, row(conv_b), wg, row(b_r), row(b_i), row(lam), w_out.astype(BF16),
      row(ln_g), row(ln_b), router_w, router_b)


def _router_kernel(logits_ref, tri_ref, idx_ref, gate_ref, cnt_ref, base_sc):
    step = pl.program_id(0)
    tr = ROUTE_T

    @pl.when(step == 0)
    def _():
        base_sc[...] = jnp.zeros((N_EXPERTS, 1), F32)

    row8 = lax.broadcasted_iota(jnp.int32, (SUBLANES, tr), 0).astype(F32)
    rowe = lax.broadcasted_iota(jnp.int32, (N_EXPERTS, tr), 0).astype(F32)
    neg_inf = -jnp.inf
    for sub in range(ROUTE_STEP // ROUTE_T):
        cols = slice(sub * tr, (sub + 1) * tr)
        _route_tile(logits_ref[0, :, cols], row8, rowe, neg_inf, tri_ref, idx_ref, gate_ref, base_sc, cols)
    cnt_ref[...] = jnp.broadcast_to(base_sc[...], (N_EXPERTS, LANES)).astype(jnp.int32)


def _route_tile(logits, row8, rowe, neg_inf, tri_ref, idx_ref, gate_ref, base_sc, cols):
    tr = ROUTE_T
    g = jnp.where(row8 < N_GROUPS, logits[N_EXPERTS:N_EXPERTS + SUBLANES, :], neg_inf)
    gmax = jnp.max(g, axis=0, keepdims=True)
    gidx = jnp.min(jnp.where(g == gmax, row8, SUBLANES), axis=0, keepdims=True)
    g_gate = 1.0 / jnp.sum(jnp.exp(g - gmax), axis=0, keepdims=True)

    esel = logits[0:EXPERTS_PER_GROUP, :]
    for grp in range(1, N_GROUPS):
        esel = jnp.where(gidx == grp, logits[grp * EXPERTS_PER_GROUP:(grp + 1) * EXPERTS_PER_GROUP, :], esel)
    v1 = jnp.max(esel, axis=0, keepdims=True)
    i1 = jnp.min(jnp.where(esel == v1, row8, SUBLANES), axis=0, keepdims=True)
    esel2 = jnp.where(row8 == i1, neg_inf, esel)
    v2 = jnp.max(esel2, axis=0, keepdims=True)
    i2 = jnp.min(jnp.where(esel2 == v2, row8, SUBLANES), axis=0, keepdims=True)
    e21 = jnp.exp(v2 - v1)
    inv = 1.0 / (1.0 + e21)
    gate1 = inv * g_gate
    gate2 = e21 * inv * g_gate
    e1 = gidx * EXPERTS_PER_GROUP + i1
    e2 = gidx * EXPERTS_PER_GROUP + i2

    hit1 = rowe == e1
    hit2 = rowe == e2
    member = jnp.where(hit1, 1.0, jnp.where(hit2, 1.0, 0.0))
    before = jnp.dot(member.astype(BF16), tri_ref[...], preferred_element_type=F32) + base_sc[...]
    rank1 = jnp.sum(jnp.where(hit1, before, 0.0), axis=0, keepdims=True)
    rank2 = jnp.sum(jnp.where(hit2, before, 0.0), axis=0, keepdims=True)
    base_sc[...] = base_sc[...] + jnp.sum(member, axis=1, keepdims=True)

    zi = jnp.zeros((1, tr), jnp.int32)
    idx_ref[:, cols] = jnp.concatenate(
        [e1.astype(jnp.int32), e2.astype(jnp.int32), rank1.astype(jnp.int32), rank2.astype(jnp.int32),
         zi, zi, zi, zi], axis=0)
    zf = jnp.zeros((1, tr), F32)
    gate_ref[:, cols] = jnp.concatenate([gate1, gate2, zf, zf, zf, zf, zf, zf], axis=0)


def _router_weights(w_rg, b_rg, w_re, b_re):
    D = w_rg.shape[0]
    pad_rows = ROUTE_ROWS - N_EXPERTS - N_GROUPS
    w = jnp.concatenate([w_re.T, w_rg.T, jnp.zeros((pad_rows, D), F32)], axis=0)
    whi = w.astype(BF16)
    wlo = (w - whi.astype(F32)).astype(BF16)
    w_split = jnp.concatenate([whi, wlo], axis=0)
    bias = jnp.concatenate([b_re, b_rg, jnp.zeros((pad_rows,), F32)]).reshape(ROUTE_ROWS, 1)
    return w_split, bias


def _router(logits_t):
    B, _, S = logits_t.shape
    T = B * S
    per_row = S // ROUTE_STEP
    tri = jnp.asarray(np.triu(np.ones((ROUTE_T, ROUTE_T), np.float32), 1), BF16)
    const = lambda shape: pl.BlockSpec(shape, lambda i: (0,) * len(shape))
    return pl.pallas_call(
        _router_kernel,
        grid=(T // ROUTE_STEP,),
        in_specs=[
            pl.BlockSpec((1, ROUTE_ROWS, ROUTE_STEP), lambda i: (i // per_row, 0, i % per_row)),
            const((ROUTE_T, ROUTE_T)),
        ],
        out_specs=[
            pl.BlockSpec((SUBLANES, ROUTE_STEP), lambda i: (0, i)),
            pl.BlockSpec((SUBLANES, ROUTE_STEP), lambda i: (0, i)),
            const((N_EXPERTS, LANES)),
        ],
        out_shape=[
            jax.ShapeDtypeStruct((SUBLANES, T), jnp.int32),
            jax.ShapeDtypeStruct((SUBLANES, T), F32),
            jax.ShapeDtypeStruct((N_EXPERTS, LANES), jnp.int32),
        ],
        scratch_shapes=[pltpu.VMEM((N_EXPERTS, 1), F32)],
        compiler_params=pltpu.CompilerParams(
            dimension_semantics=("arbitrary",), vmem_limit_bytes=VMEM_LIMIT),
        name="router",
    )(logits_t, tri)


def _sc_mesh():
    return plsc.VectorSubcoreMesh(core_axis_name="c", subcore_axis_name="s",
                                  num_cores=SC_CORES, num_subcores=SC_SUBCORES)


def _sc_worker_id():
    return lax.axis_index("s") * SC_CORES + lax.axis_index("c")


def _sc_scratch(n_win, width):
    return [
        pltpu.VMEM((n_win, SC_WIN), jnp.int32),
        pltpu.VMEM((n_win, SC_WIN), jnp.int32),
        pltpu.VMEM((2, SC_WIN, width), U32),
        pltpu.SemaphoreType.DMA((2,)),
        pltpu.SemaphoreType.DMA((2,)),
    ]


def _sc_dispatch(rows, idx1, idx2, n_rows):
    _, width = rows.shape
    _, n_win, _ = idx1.shape

    @functools.partial(
        pl.kernel, mesh=_sc_mesh(), out_type=jax.ShapeDtypeStruct((n_rows, width), rows.dtype),
        scratch_types=_sc_scratch(n_win, width), name="sc_dispatch")
    def run(rows_hbm, i1_hbm, i2_hbm, o_hbm, i1_v, i2_v, buf, rsem, wsem):
        wid = _sc_worker_id()
        base = wid * (n_win * SC_WIN)
        pltpu.sync_copy(i1_hbm.at[wid], i1_v)
        pltpu.sync_copy(i2_hbm.at[wid], i2_v)

        def read(j):
            return pltpu.async_copy(rows_hbm.at[pl.ds(base + j * SC_WIN, SC_WIN)], buf.at[j % 2], rsem.at[j % 2])

        reads = {0: read(0)}
        writes = {}
        for j in range(n_win):
            if j + 1 < n_win:
                for d in writes.pop(j - 1, ()):
                    d.wait()
                reads[j + 1] = read(j + 1)
            reads.pop(j).wait()
            writes[j] = (pltpu.async_copy(buf.at[j % 2], o_hbm.at[i1_v.at[j]], wsem.at[j % 2]),
                         pltpu.async_copy(buf.at[j % 2], o_hbm.at[i2_v.at[j]], wsem.at[j % 2]))
        for j in sorted(writes):
            for d in writes[j]:
                d.wait()

    return run(rows, idx1, idx2)


def _sc_gather_pair(table, idx1, idx2):
    _, width = table.shape
    _, n_win, _ = idx1.shape
    n_tok = SC_WORKERS * n_win * SC_WIN
    out_t = jax.ShapeDtypeStruct((n_tok, width), table.dtype)

    @functools.partial(
        pl.kernel, mesh=_sc_mesh(), out_type=(out_t, out_t),
        scratch_types=_sc_scratch(n_win, width), name="sc_combine_gather")
    def run(table_hbm, i1_hbm, i2_hbm, o1_hbm, o2_hbm, i1_v, i2_v, buf, gsem, wsem):
        wid = _sc_worker_id()
        base = wid * (n_win * SC_WIN)
        pltpu.sync_copy(i1_hbm.at[wid], i1_v)
        pltpu.sync_copy(i2_hbm.at[wid], i2_v)
        work = [(i1_v, o1_hbm, j) for j in range(n_win)] + [(i2_v, o2_hbm, j) for j in range(n_win)]

        def gather(t):
            iv, _, j = work[t]
            return pltpu.async_copy(table_hbm.at[iv.at[j]], buf.at[t % 2], gsem.at[t % 2])

        def put(t):
            _, oh, j = work[t]
            return pltpu.async_copy(buf.at[t % 2], oh.at[pl.ds(base + j * SC_WIN, SC_WIN)], wsem.at[t % 2])

        gathers = {0: gather(0)}
        puts = {}
        for t in range(len(work)):
            if t + 1 < len(work):
                if t - 1 in puts:
                    puts.pop(t - 1).wait()
                gathers[t + 1] = gather(t + 1)
            gathers.pop(t).wait()
            puts[t] = put(t)
        for t in sorted(puts):
            puts[t].wait()

    return run(table, idx1, idx2)


def _moe_kernel(layer, n_blocks, be_ref, valid_ref, x_ref, w1_hbm, w3_hbm, w2_hbm, o_ref,
                w1_st, w3_st, w2_st, w1_sc, w3_sc, w2_sc, sems, ordinal_sm):
    i = pl.program_id(0)
    expert = be_ref[i]
    new_expert = jnp.logical_or(i == 0, expert != be_ref[jnp.maximum(i - 1, 0)])

    def weight_copies(e, sl):
        return [pltpu.make_async_copy(hbm.at[layer, e], stage.at[sl], sems.at[k, sl])
                for k, (hbm, stage) in enumerate(((w1_hbm, w1_st), (w3_hbm, w3_st), (w2_hbm, w2_st)))]

    @pl.when(i == 0)
    def _():
        ordinal_sm[0] = -1
        for cp in weight_copies(expert, 0):
            cp.start()

    @pl.when(new_expert)
    def _():
        ordinal = ordinal_sm[0] + 1
        ordinal_sm[0] = ordinal
        slot = ordinal & 1
        for cp in weight_copies(expert, slot):
            cp.wait()
        w1_sc[...] = w1_st[slot].astype(BF16)
        w3_sc[...] = w3_st[slot].astype(BF16)
        w2_sc[...] = w2_st[slot].astype(BF16)
        nxt_pos = lax.while_loop(
            lambda p: jnp.logical_and(p < n_blocks, be_ref[jnp.minimum(p, n_blocks - 1)] == expert),
            lambda p: p + 1, i + 1)

        @pl.when(nxt_pos < n_blocks)
        def _():
            for cp in weight_copies(be_ref[jnp.minimum(nxt_pos, n_blocks - 1)], 1 - slot):
                cp.start()

    def up(rows):
        xa, xb = _unpack_bf16_pair(x_ref[rows, :])
        xa = xa.astype(BF16)
        xb = xb.astype(BF16)
        h1 = (jnp.dot(xa, w1_sc[:D_HALF], preferred_element_type=F32)
              + jnp.dot(xb, w1_sc[D_HALF:], preferred_element_type=F32))
        h3 = (jnp.dot(xa, w3_sc[:D_HALF], preferred_element_type=F32)
              + jnp.dot(xb, w3_sc[D_HALF:], preferred_element_type=F32))
        return h1, h3

    def down(rows, h1, h3):
        hdn = (jax.nn.silu(h1) * h3).astype(BF16)
        y = jnp.dot(hdn, w2_sc[...], preferred_element_type=F32)
        o_ref[rows, :] = _pack_row_halves(y)

    n_sub = MOE_BM // MOE_SUB
    subs = [slice(k * MOE_SUB, (k + 1) * MOE_SUB) for k in range(n_sub)]
    valid = valid_ref[i]
    chains = (valid + (MOE_SUB - 1)) // MOE_SUB

    for live in range(n_sub + 1):
        @pl.when(chains == live)
        def _(live=live):
            ups = {}
            if live:
                ups[0] = up(subs[0])
            for k in range(live):
                if k + 1 < live:
                    ups[k + 1] = up(subs[k + 1])
                down(subs[k], *ups.pop(k))
            if live < n_sub:
                o_ref[live * MOE_SUB:, :] = jnp.zeros((MOE_BM - live * MOE_SUB, D_HALF), o_ref.dtype)


def _moe_blocks(xbuf, block_e, valid, w1, w3, w2, layer):
    n_rows, _ = xbuf.shape
    D = D_MODEL
    n_blocks = n_rows // MOE_BM
    rows = lambda i, be, nu: (i, 0)
    grid_spec = pltpu.PrefetchScalarGridSpec(
        num_scalar_prefetch=2,
        grid=(n_blocks,),
        in_specs=[
            pl.BlockSpec((MOE_BM, D_HALF), rows),
            pl.BlockSpec(memory_space=pl.ANY),
            pl.BlockSpec(memory_space=pl.ANY),
            pl.BlockSpec(memory_space=pl.ANY),
        ],
        out_specs=pl.BlockSpec((MOE_BM, D_HALF), rows),
        scratch_shapes=[
            pltpu.VMEM((2, D, D_EXPERT), F32),
            pltpu.VMEM((2, D, D_EXPERT), F32),
            pltpu.VMEM((2, D_EXPERT, D), F32),
            pltpu.VMEM((D, D_EXPERT), BF16),
            pltpu.VMEM((D, D_EXPERT), BF16),
            pltpu.VMEM((D_EXPERT, D), BF16),
            pltpu.SemaphoreType.DMA((3, 2)),
            pltpu.SMEM((1,), jnp.int32),
        ],
    )
    return pl.pallas_call(
        functools.partial(_moe_kernel, layer, n_blocks),
        grid_spec=grid_spec,
        out_shape=jax.ShapeDtypeStruct((n_rows, D_HALF), U32),
        compiler_params=pltpu.CompilerParams(
            dimension_semantics=("arbitrary",), vmem_limit_bytes=VMEM_LIMIT),
        name="moe_experts",
    )(block_e, valid, xbuf, w1, w3, w2)


def _moe_combine_norm(x, y1, y2, gate_rows, g, b):
    n = x.shape[0]
    pad = jnp.zeros((LANES - SUBLANES, n), F32)
    gates = jnp.concatenate([gate_rows, pad], axis=0).T
    g1 = gates[:, 0:1]
    g2 = gates[:, 1:2]
    a1, b1 = _unpack_bf16_pair(y1)
    a2, b2 = _unpack_bf16_pair(y2)
    f = jnp.concatenate([g1 * a1 + g2 * a2, g1 * b1 + g2 * b2], axis=1)
    return _layer_norm_rows(ALPHA * x + f, g, b)


def _combine_kernel(x_ref, y1_ref, y2_ref, gates_ref, g_ref, b_ref, o_ref):
    o_ref[...] = _moe_combine_norm(x_ref[...], y1_ref[...], y2_ref[...], gates_ref[...], g_ref[...], b_ref[...])


def _combine_part_kernel(x_ref, y1_ref, y2_ref, gates_ref, g_ref, b_ref, prev_ref, o_ref):
    del prev_ref
    _combine_kernel(x_ref, y1_ref, y2_ref, gates_ref, g_ref, b_ref, o_ref)


def _combine_ln(xf, y_parts, gates, ln_g, ln_b):
    T, D = xf.shape
    n_parts = len(y_parts)
    steps = T // (COMB_T * n_parts)
    const = lambda shape: pl.BlockSpec(shape, lambda i: (0,) * len(shape))
    out = None
    for part, (y1, y2) in enumerate(y_parts):
        off = part * steps
        glob = lambda w, off=off: pl.BlockSpec((COMB_T, w), lambda i: (i + off, 0))
        local = lambda w: pl.BlockSpec((COMB_T, w), lambda i: (i, 0))
        in_specs = [glob(D), local(D_HALF), local(D_HALF),
                    pl.BlockSpec((SUBLANES, COMB_T), lambda i, off=off: (0, i + off)), const((1, D)), const((1, D))]
        args = [xf, y1, y2, gates, ln_g.reshape(1, D), ln_b.reshape(1, D)]
        if out is not None:
            in_specs.append(pl.BlockSpec(memory_space=pl.ANY))
            args.append(out)
        out = pl.pallas_call(
            _combine_kernel if out is None else _combine_part_kernel,
            grid=(steps,),
            in_specs=in_specs,
            out_specs=glob(D),
            out_shape=jax.ShapeDtypeStruct((T, D), F32),
            input_output_aliases={} if out is None else {len(args) - 1: 0},
            compiler_params=pltpu.CompilerParams(
                dimension_semantics=("arbitrary",), vmem_limit_bytes=VMEM_LIMIT),
            name="moe_combine_ln",
        )(*args)
    return out


def _hier_moe(xpk, logits_t, w1, w3, w2, layer, gather_parts):
    B, S, _ = xpk.shape
    D = D_MODEL
    T = B * S
    idx, gates, cnt = _router(logits_t)
    counts = cnt[:, 0]
    padded = ((counts + MOE_BM - 1) // MOE_BM) * MOE_BM
    pends = jnp.cumsum(padded)
    pstarts = pends - padded
    experts = jnp.arange(N_EXPERTS, dtype=jnp.int32)[:, None]

    def dest(e_row, rank_row):
        return jnp.sum(jnp.where(e_row[None, :] == experts, pstarts[:, None], 0), axis=0) + rank_row

    n_win = T // (SC_WORKERS * SC_WIN)
    dest1 = dest(idx[0], idx[2]).reshape(SC_WORKERS, n_win, SC_WIN)
    dest2 = dest(idx[1], idx[3]).reshape(SC_WORKERS, n_win, SC_WIN)
    n_blocks = -(-(T * TOP_K + N_EXPERTS * (MOE_BM - 1)) // MOE_BM)
    n_rows = n_blocks * MOE_BM
    block_start = jnp.arange(n_blocks, dtype=jnp.int32) * MOE_BM
    block_e = jnp.minimum(jnp.sum(block_start[:, None] >= pends[None, :], axis=1), N_EXPERTS - 1).astype(jnp.int32)
    of_block = block_e[:, None] == experts[:, 0][None, :]
    pick = lambda v: jnp.sum(jnp.where(of_block, v[None, :], 0), axis=1)
    valid = jnp.clip(pick(counts) - (block_start - pick(pstarts)), 0, MOE_BM).astype(jnp.int32)
    xbuf = _sc_dispatch(xpk.reshape(T, D_HALF), dest1, dest2, n_rows)
    ybuf = _moe_blocks(xbuf, block_e, valid, w1, w3, w2, layer)
    part_shape = (gather_parts, SC_WORKERS, n_win // gather_parts, SC_WIN)
    d1, d2 = dest1.reshape(part_shape), dest2.reshape(part_shape)
    y_parts = [_sc_gather_pair(ybuf, d1[p], d2[p]) for p in range(gather_parts)]
    return y_parts, gates


def _att_head_order():
    order = []
    for p in range(N_HEADS // 2):
        jj, m = divmod(p, 4)
        order += [8 * jj + m, 8 * jj + 4 + m]
    return order


ATT_HEAD_ORDER = _att_head_order()


def _attn_kernel(xprev_ref, y1_ref, y2_ref, gates_a_ref, gates_b_ref, g_prev_ref, b_prev_ref, wqkv_ref, bias_ref,
                 sink_ref, wo_ref, g_ref, b_ref, wrt_ref, brt_ref, o_ref, opk_ref, lg_ref,
                 kv_ext, o_sc, s_sc0, s_sc1, p_sc0, p_sc1):
    s = pl.program_id(1)
    tq = ATT_TQ
    s_bufs = (s_sc0, s_sc1)
    p_bufs = (p_sc0, p_sc1)
    gate_refs = (gates_a_ref, gates_b_ref)
    assert ATT_NSEQ == len(gate_refs)

    @pl.when(s == 0)
    def _():
        kv_ext[:, 0:WINDOW, :] = jnp.zeros((ATT_NSEQ, WINDOW, 2 * KV_DIM), BF16)

    def layer_input(sq):
        return _moe_combine_norm(xprev_ref[sq], y1_ref[sq], y2_ref[sq], gate_refs[sq][...],
                                 g_prev_ref[...], b_prev_ref[...])

    def project_qkv(sq, x):
        qkv = jnp.dot(x.astype(BF16), wqkv_ref[...], preferred_element_type=F32)
        kv_ext[sq, WINDOW:WINDOW + tq, :] = qkv[:, Q_DIM:].astype(BF16)
        return (qkv[:, :Q_DIM] * (HEAD_DIM ** -0.5 * LOG2E)).astype(BF16)

    lane = lax.broadcasted_iota(jnp.int32, (WINDOW, LANES), 1)
    low = lane < HEAD_DIM
    sub = lax.broadcasted_iota(jnp.int32, (LANES, WINDOW), 0)
    top = sub < HEAD_DIM
    first = jnp.where(s == 0, 1, 0)
    nt = (((1,), (1,)), ((), ()))
    zero = jnp.zeros((), BF16)

    tiles = [(sq, n, j) for n in range(ATT_NB) for j in range(2) for sq in range(ATT_NSEQ)]

    def scores(t, qs_all):
        sq, n, j = tiles[t]
        q = qs_all[sq]
        r0 = n * WINDOW
        k_tile = kv_ext[sq, r0:r0 + 2 * WINDOW, j * LANES:(j + 1) * LANES]
        parts = []
        for m in range(4):
            p = 4 * j + m
            qt = q[r0:r0 + WINDOW, p * LANES:(p + 1) * LANES]
            parts.append(jnp.where(low, qt, zero))
            parts.append(jnp.where(low, zero, qt))
        qs = jnp.concatenate(parts, axis=0)
        bias_sel = first if n == 0 else 0
        s_bufs[t % 2][...] = (lax.dot_general(k_tile, qs, nt, preferred_element_type=F32)
                              + bias_ref[bias_sel, j])

    def softmax_pv(t):
        sq, n, j = tiles[t]
        r0 = n * WINDOW
        s_sc = s_bufs[t % 2]
        p_sc = p_bufs[t % 2]
        inv_l = []
        for h in range(8):
            hc = slice(h * WINDOW, (h + 1) * WINDOW)
            sink = sink_ref[8 * j + h] * LOG2E
            mx = jnp.maximum(jnp.max(s_sc[:, hc], axis=0, keepdims=True), sink)
            pr = jnp.exp2(s_sc[:, hc] - mx)
            p_sc[:, hc] = pr.astype(BF16)
            inv_l.append(1.0 / (jnp.sum(pr, axis=0, keepdims=True) + jnp.exp2(sink - mx)))
        v_tile = kv_ext[sq, r0:r0 + 2 * WINDOW, KV_DIM + j * LANES:KV_DIM + (j + 1) * LANES]
        tn = (((0,), (0,)), ((), ()))
        ov = lax.dot_general(v_tile, p_sc[...], tn, preferred_element_type=F32)
        for m in range(4):
            p = 4 * j + m
            o_even = ov[:, (2 * m) * WINDOW:(2 * m + 1) * WINDOW] * inv_l[2 * m]
            o_odd = ov[:, (2 * m + 1) * WINDOW:(2 * m + 2) * WINDOW] * inv_l[2 * m + 1]
            o_sc[sq, p * LANES:(p + 1) * LANES, r0:r0 + WINDOW] = jnp.where(top, o_even, o_odd).astype(BF16)

    def project_out(sq):
        tn = (((0,), (0,)), ((), ()))
        return lax.dot_general(o_sc[sq], wo_ref[...], tn, preferred_element_type=F32)

    def finish(sq, x, out):
        xn = _layer_norm_rows(ALPHA * x + out, g_ref[...], b_ref[...])
        o_ref[sq] = xn
        opk_ref[sq] = _pack_row_halves(xn)
        lg_ref[sq] = _router_logits_t(xn, wrt_ref[...], brt_ref[...])

    xs = [layer_input(0), layer_input(1)]
    qs_all = [project_qkv(0, xs[0]), project_qkv(1, xs[1])]
    scores(0, qs_all)
    for t in range(len(tiles)):
        if t + 1 < len(tiles):
            scores(t + 1, qs_all)
        softmax_pv(t)
        if tiles[t][0] == 0 and all(sq != 0 for sq, _, _ in tiles[t + 1:]):
            out_first = project_out(0)
    kv_ext[:, 0:WINDOW, :] = kv_ext[:, tq:tq + WINDOW, :]
    out_second = project_out(1)
    finish(0, xs[0], out_first)
    finish(1, xs[1], out_second)


def _attn_bias():
    qi = np.arange(WINDOW)[:, None]
    sj = np.arange(2 * WINDOW)[None, :]
    dist = qi - sj + WINDOW
    valid = (dist >= 0) & (dist < WINDOW)
    slopes = 2.0 ** (-8.0 * np.arange(1, N_HEADS + 1, dtype=np.float32) / N_HEADS)
    slopes = slopes.astype(np.float32)[ATT_HEAD_ORDER]
    sb = -(slopes[:, None, None] * dist.astype(np.float32)[None])
    later = np.where(valid[None], sb, -np.inf)
    first = np.where((valid & (sj >= WINDOW))[None], sb, -np.inf)
    bias = np.stack([later, first]).astype(np.float32) * np.float32(LOG2E)
    bias = bias.reshape(2, 2, 8, WINDOW, 2 * WINDOW).transpose(0, 1, 4, 2, 3).reshape(2, 2, 2 * WINDOW, 8 * WINDOW)
    return jnp.asarray(np.ascontiguousarray(bias))


ATT_N_INPUTS = 15


def _attn_part_kernel(*refs):
    _attn_kernel(*refs[:ATT_N_INPUTS], *refs[ATT_N_INPUTS + 3:])


def _attn_layer(x_prev, y_parts, gates, g_prev, b_prev, w_qkv, sinks, w_o, ln_g, ln_b, router_w, router_b):
    B, S, D = x_prev.shape
    steps = S // ATT_TQ
    assert ATT_HEAD_ORDER == list(np.arange(N_HEADS).reshape(2, 2, 4).transpose(0, 2, 1).reshape(-1))
    wq = w_qkv[:, :Q_DIM].reshape(D, 2, 2, 4, HEAD_DIM).transpose(0, 1, 3, 2, 4).reshape(D, Q_DIM)
    wqkv = jnp.concatenate([wq, w_qkv[:, Q_DIM:]], axis=1).astype(BF16)
    wo_t = w_o.reshape(2, 2, 4, HEAD_DIM, D).transpose(0, 2, 1, 3, 4).reshape(Q_DIM, D).astype(BF16)
    sink = sinks.reshape(2, 2, 4).transpose(0, 2, 1).reshape(N_HEADS, 1, 1)
    bias = _attn_bias()
    const = lambda shape: pl.BlockSpec(shape, lambda b, s: (0,) * len(shape))
    n_parts = len(y_parts)
    rows_per_part = B // n_parts
    pairs = rows_per_part // ATT_NSEQ
    outs = None
    for part, (y1, y2) in enumerate(y_parts):
        off = part * pairs
        glob = lambda w, off=off: pl.BlockSpec((ATT_NSEQ, ATT_TQ, w), lambda b, s: (b + off, s, 0))
        local = lambda w: pl.BlockSpec((ATT_NSEQ, ATT_TQ, w), lambda b, s: (b, s, 0))
        gate_rows = lambda sq, off=off: pl.BlockSpec(
            (SUBLANES, ATT_TQ), lambda b, s: (0, (ATT_NSEQ * (b + off) + sq) * steps + s))
        in_specs = [
            glob(D),
            local(D_HALF),
            local(D_HALF),
            gate_rows(0),
            gate_rows(1),
            const((1, D)),
            const((1, D)),
            const((D, Q_DIM + 2 * KV_DIM)),
            const((2, 2, 2 * WINDOW, 8 * WINDOW)),
            const((N_HEADS, 1, 1)),
            const((Q_DIM, D)),
            const((1, D)),
            const((1, D)),
            const((2 * ROUTE_ROWS, D)),
            const((ROUTE_ROWS, 1)),
        ]
        args = [x_prev, y1.reshape(rows_per_part, S, D_HALF), y2.reshape(rows_per_part, S, D_HALF), gates, gates,
                g_prev.reshape(1, D), b_prev.reshape(1, D), wqkv, bias, sink, wo_t,
                ln_g.reshape(1, D), ln_b.reshape(1, D), router_w, router_b]
        aliases = {}
        body = _attn_kernel
        if outs is not None:
            aliases = {len(args) + k: k for k in range(3)}
            in_specs += [pl.BlockSpec(memory_space=pl.ANY)] * 3
            args += list(outs)
            body = _attn_part_kernel
        outs = pl.pallas_call(
            body,
            grid=(pairs, steps),
            in_specs=in_specs,
            out_specs=[glob(D), glob(D_HALF),
                       pl.BlockSpec((ATT_NSEQ, ROUTE_ROWS, ATT_TQ), lambda b, s, off=off: (b + off, 0, s))],
            out_shape=[jax.ShapeDtypeStruct((B, S, D), F32), jax.ShapeDtypeStruct((B, S, D_HALF), U32),
                       jax.ShapeDtypeStruct((B, ROUTE_ROWS, S), F32)],
            input_output_aliases=aliases,
            scratch_shapes=[
                pltpu.VMEM((ATT_NSEQ, ATT_TQ + WINDOW, 2 * KV_DIM), BF16),
                pltpu.VMEM((ATT_NSEQ, Q_DIM, ATT_TQ), BF16),
                pltpu.VMEM((2 * WINDOW, 8 * WINDOW), F32),
                pltpu.VMEM((2 * WINDOW, 8 * WINDOW), F32),
                pltpu.VMEM((2 * WINDOW, 8 * WINDOW), BF16),
                pltpu.VMEM((2 * WINDOW, 8 * WINDOW), BF16),
            ],
            compiler_params=pltpu.CompilerParams(
                dimension_semantics=("arbitrary", "arbitrary"), vmem_limit_bytes=VMEM_LIMIT),
            name="swa_attn_ln",
        )(*args)
    return outs


def kernel(x, rec_w_in, rec_conv_w, rec_conv_b, rec_w_r, rec_b_r, rec_w_i, rec_b_i, rec_lambda, rec_w_out,
           att_w_qkv, att_sinks, att_w_o, moe_w_group, moe_b_group, moe_w_expert, moe_b_expert,
           moe_w1, moe_w3, moe_w2, ln_g, ln_b):
    assert DEPTH == 2
    B, S, D = x.shape

    router = [_router_weights(moe_w_group[layer], moe_b_group[layer], moe_w_expert[layer], moe_b_expert[layer])
              for layer in range(DEPTH)]

    x1, x1_pk, logits1 = _rglru_layer(x, rec_w_in[0], rec_conv_w[0], rec_conv_b[0], rec_w_r[0], rec_b_r[0],
                                      rec_w_i[0], rec_b_i[0], rec_lambda[0], rec_w_out[0], ln_g[0, 0], ln_b[0, 0],
                                      *router[0])
    y_parts, gates = _hier_moe(x1_pk, logits1, moe_w1, moe_w3, moe_w2, 0, gather_parts=ATT_PARTS)
    x3, x3_pk, logits3 = _attn_layer(x1, y_parts, gates, ln_g[0, 1], ln_b[0, 1], att_w_qkv[0], att_sinks[0],
                                     att_w_o[0], ln_g[1, 0], ln_b[1, 0], *router[1])
    y_parts, gates = _hier_moe(x3_pk, logits3, moe_w1, moe_w3, moe_w2, 1, gather_parts=1)
    out = _combine_ln(x3.reshape(B * S, D), y_parts, gates, ln_g[1, 1], ln_b[1, 1])
    return out.reshape(B, S, D)
```

```python
import functools

import jax
import jax.numpy as jnp
import numpy as np
from jax import lax
from jax.experimental import pallas as pl
from jax.experimental.pallas import tpu as pltpu
from jax.experimental.pallas import tpu_sc as plsc

F32 = jnp.float32
BF16 = jnp.bfloat16
U32 = jnp.uint32

D_MODEL = 1024
DEPTH = 2
D_RNN = 1280
LRU_BLOCKS = 16
LRU_BLOCK_W = D_RNN // LRU_BLOCKS
CONV_W = 4
LRU_C = 8.0
N_HEADS = 16
N_KV_HEADS = 4
HEAD_DIM = 64
WINDOW = 128
Q_DIM = N_HEADS * HEAD_DIM
KV_DIM = N_KV_HEADS * HEAD_DIM
N_GROUPS = 4
EXPERTS_PER_GROUP = 8
N_EXPERTS = N_GROUPS * EXPERTS_PER_GROUP
TOP_K = 2
D_EXPERT = 512
ALPHA = (2 * DEPTH) ** 0.25
LN_EPS = 1e-5
LOG2E = 1.4426950408889634

LANES = 128
SUBLANES = 8
VMEM_LIMIT = 56 * 1024 * 1024

REC_TS = 256
REC_GROUPS = REC_TS // SUBLANES
REC_NSEQ = 2
GATE_TILE = 256
GATE_WIN = 512
GATE_WIN_STARTS = (0, 128, 384, 640, 768)
N_GATE_TILES = D_RNN // GATE_TILE

ROUTE_T = 512
ROUTE_STEP = 2048
ROUTE_ROWS = 40

MOE_BM = 1024
MOE_SUB = 256

ATT_TQ = 256
ATT_NB = ATT_TQ // WINDOW
ATT_NSEQ = 2
ATT_PARTS = 4

COMB_T = 1024

D_HALF = D_MODEL // 2

SC_CORES = 2
SC_SUBCORES = 16
SC_WORKERS = SC_CORES * SC_SUBCORES
SC_WIN = 64


def _layer_norm_rows(z, g, b):
    mu = jnp.mean(z, axis=-1, keepdims=True)
    zc = z - mu
    var = jnp.mean(zc * zc, axis=-1, keepdims=True)
    return zc * lax.rsqrt(var + LN_EPS) * g + b


def _pack_bf16_pair(a, b):
    ua = lax.bitcast_convert_type(a.astype(BF16).astype(F32), U32)
    ub = lax.bitcast_convert_type(b.astype(BF16).astype(F32), U32)
    return (ua >> 16) | (ub & jnp.uint32(0xFFFF0000))


def _unpack_bf16_pair(w):
    a = lax.bitcast_convert_type(w << 16, F32)
    b = lax.bitcast_convert_type(w & jnp.uint32(0xFFFF0000), F32)
    return a, b


def _pack_row_halves(x):
    return _pack_bf16_pair(x[:, :D_HALF], x[:, D_HALF:])


def _router_logits_t(x, w_split, bias):
    xhi = x.astype(BF16)
    xlo = (x - xhi.astype(F32)).astype(BF16)
    nt = (((1,), (1,)), ((), ()))
    both = lax.dot_general(w_split, xhi, nt, preferred_element_type=F32)
    low = lax.dot_general(w_split[:ROUTE_ROWS], xlo, nt, preferred_element_type=F32)
    return both[:ROUTE_ROWS] + both[ROUTE_ROWS:] + low + bias


def _rglru_kernel(x_ref, perm_ref, perm_t_ref, w_in_ref, convw_ref, convb_ref, wg_ref, br_ref, bi_ref, lam_ref,
                  w_out_ref, g_ref, b_ref, wrt_ref, brt_ref, o_ref, opk_ref, lg_ref,
                  xr_ext, tail_sc, a_sc, u_sc, h_carry):
    s = pl.program_id(1)
    ts = REC_TS
    halo = (CONV_W - 1) * SUBLANES

    @pl.when(s == 0)
    def _():
        tail_sc[...] = jnp.zeros((REC_NSEQ, halo, D_RNN), F32)
        h_carry[...] = jnp.zeros((REC_NSEQ, 1, D_RNN), F32)

    row = lax.broadcasted_iota(jnp.int32, (SUBLANES, D_RNN), 0)
    nlam = -lam_ref[...]
    sp = jnp.maximum(nlam, 0.0) + jnp.log1p(jnp.exp(-jnp.abs(nlam)))
    log2a_scale = (-LRU_C * LOG2E) * sp

    def project(q):
        xp = jnp.dot(perm_ref[...], x_ref[q].astype(BF16), preferred_element_type=F32).astype(BF16)
        proj = jnp.dot(xp, w_in_ref[...], preferred_element_type=F32)
        return proj[:, :D_RNN], proj[:, D_RNN:]

    def conv_gates(q, xr):
        for k in range(CONV_W - 1):
            r0 = ts - halo + k * SUBLANES
            cur = xr[r0:r0 + SUBLANES, :]
            prev = tail_sc[q, k * SUBLANES:(k + 1) * SUBLANES, :]
            xr_ext[q, k * SUBLANES:(k + 1) * SUBLANES, :] = jnp.where(
                row == 0, pltpu.roll(prev, 1, axis=0), pltpu.roll(cur, 1, axis=0))
        tail_sc[q] = xr[ts - halo:, :]
        xr_ext[q, halo:halo + ts, :] = xr
        xc = convb_ref[...] + convw_ref[CONV_W - 1:CONV_W, :] * xr
        for k in range(CONV_W - 1):
            xc = xc + convw_ref[k:k + 1, :] * xr_ext[q, k * SUBLANES:k * SUBLANES + ts, :]
        xcb = xc.astype(BF16)
        pres = [jnp.dot(xcb[:, GATE_WIN_STARTS[j]:GATE_WIN_STARTS[j] + GATE_WIN], wg_ref[j],
                        preferred_element_type=F32) for j in range(N_GATE_TILES)]
        for j, pre in enumerate(pres):
            cs = j * GATE_TILE
            r = jax.nn.sigmoid(pre[:, :GATE_TILE] + br_ref[:, cs:cs + GATE_TILE])
            i = jax.nn.sigmoid(pre[:, GATE_TILE:] + bi_ref[:, cs:cs + GATE_TILE])
            a = jnp.exp2(r * log2a_scale[:, cs:cs + GATE_TILE])
            s1 = 1.0 - a * a
            mult = jnp.where(s1 > 0.0, s1 * lax.rsqrt(s1), 0.0)
            u = mult * (i * xc[:, cs:cs + GATE_TILE])
            a_sc[q, :, cs:cs + GATE_TILE] = a
            u_sc[q, :, cs:cs + GATE_TILE] = u

    def segment_scan(q):
        h = jnp.zeros((SUBLANES, D_RNN), F32)
        prod = jnp.ones((SUBLANES, D_RNN), F32)
        for gidx in range(REC_GROUPS):
            rows = slice(gidx * SUBLANES, (gidx + 1) * SUBLANES)
            a8 = a_sc[q, rows, :]
            h = a8 * h + u_sc[q, rows, :]
            prod = a8 * prod
            u_sc[q, rows, :] = h
            a_sc[q, rows, :] = prod
        return h, prod

    def recur_out(q, gate, seg):
        seg_h, seg_a = seg
        for d in (1, 2, 4):
            keep = row >= d
            a_sh = jnp.where(keep, pltpu.roll(seg_a, d, axis=0), 1.0)
            h_sh = jnp.where(keep, pltpu.roll(seg_h, d, axis=0), 0.0)
            seg_h = seg_a * h_sh + seg_h
            seg_a = seg_a * a_sh
        h_in = h_carry[q]
        after = seg_a * h_in + seg_h
        enter = jnp.where(row == 0, h_in, pltpu.roll(after, 1, axis=0))
        h_carry[q] = after[SUBLANES - 1:SUBLANES, :]
        hs = (u_sc[q].reshape(REC_GROUPS, SUBLANES, D_RNN)
              + a_sc[q].reshape(REC_GROUPS, SUBLANES, D_RNN) * enter[None]).reshape(ts, D_RNN)
        y = hs * jax.nn.gelu(gate)
        y_t = jnp.dot(perm_t_ref[...], y.astype(BF16), preferred_element_type=F32).astype(BF16)
        return jnp.dot(y_t, w_out_ref[...], preferred_element_type=F32)

    def finish(q, out):
        z = ALPHA * x_ref[q] + out
        xn = _layer_norm_rows(z, g_ref[...], b_ref[...])
        o_ref[q] = xn
        opk_ref[q] = _pack_row_halves(xn)
        lg_ref[q] = _router_logits_t(xn, wrt_ref[...], brt_ref[...])

    assert REC_NSEQ == 2
    gate_a, xr_a = project(0)
    conv_gates(0, xr_a)
    gate_b, xr_b = project(1)
    conv_gates(1, xr_b)
    seg_a = segment_scan(0)
    seg_b = segment_scan(1)
    out_a = recur_out(0, gate_a, seg_a)
    out_b = recur_out(1, gate_b, seg_b)
    finish(0, out_a)
    finish(1, out_b)


def _band_gate_weights(w_r, w_i):
    spread = jnp.asarray(np.tile(np.eye(LRU_BLOCK_W, dtype=np.float32), (1, LRU_BLOCKS)), BF16)
    blk = np.arange(D_RNN) // LRU_BLOCK_W
    on_diag = jnp.asarray(blk[:, None] == blk[None, :])

    def dense(w):
        rows = w.reshape(D_RNN, LRU_BLOCK_W).astype(BF16)
        return jnp.where(on_diag, jnp.dot(rows, spread, preferred_element_type=F32), 0.0)

    wr, wi = dense(w_r), dense(w_i)
    tiles = []
    for j in range(N_GATE_TILES):
        ws = GATE_WIN_STARTS[j]
        cs = j * GATE_TILE
        lo_blk = cs // LRU_BLOCK_W
        hi_blk = (cs + GATE_TILE - 1) // LRU_BLOCK_W
        assert ws <= lo_blk * LRU_BLOCK_W and (hi_blk + 1) * LRU_BLOCK_W <= ws + GATE_WIN
        tiles.append(jnp.concatenate([wr[ws:ws + GATE_WIN, cs:cs + GATE_TILE],
                                      wi[ws:ws + GATE_WIN, cs:cs + GATE_TILE]], axis=1))
    return jnp.stack(tiles).astype(BF16)


def _rglru_layer(x, w_in, conv_w, conv_b, w_r, b_r, w_i, b_i, lam, w_out, ln_g, ln_b, router_w, router_b):
    B, S, D = x.shape
    wg = _band_gate_weights(w_r, w_i)
    rho = np.arange(REC_TS)
    perm_np = np.zeros((REC_TS, REC_TS), np.float32)
    perm_np[rho, (rho % SUBLANES) * REC_GROUPS + rho // SUBLANES] = 1.0
    perm = jnp.asarray(perm_np, BF16)
    perm_t = jnp.asarray(perm_np.T, BF16)
    row = lambda v: v.reshape(1, -1)
    const = lambda shape: pl.BlockSpec(shape, lambda b, s: (0,) * len(shape))
    tile = lambda w: pl.BlockSpec((REC_NSEQ, REC_TS, w), lambda b, s: (b, s, 0))
    halo = (CONV_W - 1) * SUBLANES
    return pl.pallas_call(
        _rglru_kernel,
        grid=(B // REC_NSEQ, S // REC_TS),
        in_specs=[
            tile(D),
            const((REC_TS, REC_TS)),
            const((REC_TS, REC_TS)),
            const((D, 2 * D_RNN)),
            const((CONV_W, D_RNN)),
            const((1, D_RNN)),
            const((N_GATE_TILES, GATE_WIN, 2 * GATE_TILE)),
            const((1, D_RNN)),
            const((1, D_RNN)),
            const((1, D_RNN)),
            const((D_RNN, D)),
            const((1, D)),
            const((1, D)),
            const((2 * ROUTE_ROWS, D)),
            const((ROUTE_ROWS, 1)),
        ],
        out_specs=[tile(D), tile(D_HALF),
                   pl.BlockSpec((REC_NSEQ, ROUTE_ROWS, REC_TS), lambda b, s: (b, 0, s))],
        out_shape=[jax.ShapeDtypeStruct((B, S, D), F32), jax.ShapeDtypeStruct((B, S, D_HALF), U32),
                   jax.ShapeDtypeStruct((B, ROUTE_ROWS, S), F32)],
        scratch_shapes=[
            pltpu.VMEM((REC_NSEQ, halo + REC_TS, D_RNN), F32),
            pltpu.VMEM((REC_NSEQ, halo, D_RNN), F32),
            pltpu.VMEM((REC_NSEQ, REC_TS, D_RNN), F32),
            pltpu.VMEM((REC_NSEQ, REC_TS, D_RNN), F32),
            pltpu.VMEM((REC_NSEQ, 1, D_RNN), F32),
        ],
        compiler_params=pltpu.CompilerParams(
            dimension_semantics=("arbitrary", "arbitrary"), vmem_limit_bytes=VMEM_LIMIT),
        name="rglru_ln",
    )(x, perm, perm_t, w_in.astype(BF16), conv_w, row(conv_b), wg, row(b_r), row(b_i), row(lam), w_out.astype(BF16),
      row(ln_g), row(ln_b), router_w, router_b)


def _router_kernel(logits_ref, tri_ref, idx_ref, gate_ref, cnt_ref, base_sc):
    step = pl.program_id(0)
    tr = ROUTE_T

    @pl.when(step == 0)
    def _():
        base_sc[...] = jnp.zeros((N_EXPERTS, 1), F32)

    row8 = lax.broadcasted_iota(jnp.int32, (SUBLANES, tr), 0).astype(F32)
    rowe = lax.broadcasted_iota(jnp.int32, (N_EXPERTS, tr), 0).astype(F32)
    neg_inf = -jnp.inf
    for sub in range(ROUTE_STEP // ROUTE_T):
        cols = slice(sub * tr, (sub + 1) * tr)
        _route_tile(logits_ref[0, :, cols], row8, rowe, neg_inf, tri_ref, idx_ref, gate_ref, base_sc, cols)
    cnt_ref[...] = jnp.broadcast_to(base_sc[...], (N_EXPERTS, LANES)).astype(jnp.int32)


def _route_tile(logits, row8, rowe, neg_inf, tri_ref, idx_ref, gate_ref, base_sc, cols):
    tr = ROUTE_T
    g = jnp.where(row8 < N_GROUPS, logits[N_EXPERTS:N_EXPERTS + SUBLANES, :], neg_inf)
    gmax = jnp.max(g, axis=0, keepdims=True)
    gidx = jnp.min(jnp.where(g == gmax, row8, SUBLANES), axis=0, keepdims=True)
    g_gate = 1.0 / jnp.sum(jnp.exp(g - gmax), axis=0, keepdims=True)

    esel = logits[0:EXPERTS_PER_GROUP, :]
    for grp in range(1, N_GROUPS):
        esel = jnp.where(gidx == grp, logits[grp * EXPERTS_PER_GROUP:(grp + 1) * EXPERTS_PER_GROUP, :], esel)
    v1 = jnp.max(esel, axis=0, keepdims=True)
    i1 = jnp.min(jnp.where(esel == v1, row8, SUBLANES), axis=0, keepdims=True)
    esel2 = jnp.where(row8 == i1, neg_inf, esel)
    v2 = jnp.max(esel2, axis=0, keepdims=True)
    i2 = jnp.min(jnp.where(esel2 == v2, row8, SUBLANES), axis=0, keepdims=True)
    e21 = jnp.exp(v2 - v1)
    inv = 1.0 / (1.0 + e21)
    gate1 = inv * g_gate
    gate2 = e21 * inv * g_gate
    e1 = gidx * EXPERTS_PER_GROUP + i1
    e2 = gidx * EXPERTS_PER_GROUP + i2

    hit1 = rowe == e1
    hit2 = rowe == e2
    member = jnp.where(hit1, 1.0, jnp.where(hit2, 1.0, 0.0))
    before = jnp.dot(member.astype(BF16), tri_ref[...], preferred_element_type=F32) + base_sc[...]
    rank1 = jnp.sum(jnp.where(hit1, before, 0.0), axis=0, keepdims=True)
    rank2 = jnp.sum(jnp.where(hit2, before, 0.0), axis=0, keepdims=True)
    base_sc[...] = base_sc[...] + jnp.sum(member, axis=1, keepdims=True)

    zi = jnp.zeros((1, tr), jnp.int32)
    idx_ref[:, cols] = jnp.concatenate(
        [e1.astype(jnp.int32), e2.astype(jnp.int32), rank1.astype(jnp.int32), rank2.astype(jnp.int32),
         zi, zi, zi, zi], axis=0)
    zf = jnp.zeros((1, tr), F32)
    gate_ref[:, cols] = jnp.concatenate([gate1, gate2, zf, zf, zf, zf, zf, zf], axis=0)


def _router_weights(w_rg, b_rg, w_re, b_re):
    D = w_rg.shape[0]
    pad_rows = ROUTE_ROWS - N_EXPERTS - N_GROUPS
    w = jnp.concatenate([w_re.T, w_rg.T, jnp.zeros((pad_rows, D), F32)], axis=0)
    whi = w.astype(BF16)
    wlo = (w - whi.astype(F32)).astype(BF16)
    w_split = jnp.concatenate([whi, wlo], axis=0)
    bias = jnp.concatenate([b_re, b_rg, jnp.zeros((pad_rows,), F32)]).reshape(ROUTE_ROWS, 1)
    return w_split, bias


def _router(logits_t):
    B, _, S = logits_t.shape
    T = B * S
    per_row = S // ROUTE_STEP
    tri = jnp.asarray(np.triu(np.ones((ROUTE_T, ROUTE_T), np.float32), 1), BF16)
    const = lambda shape: pl.BlockSpec(shape, lambda i: (0,) * len(shape))
    return pl.pallas_call(
        _router_kernel,
        grid=(T // ROUTE_STEP,),
        in_specs=[
            pl.BlockSpec((1, ROUTE_ROWS, ROUTE_STEP), lambda i: (i // per_row, 0, i % per_row)),
            const((ROUTE_T, ROUTE_T)),
        ],
        out_specs=[
            pl.BlockSpec((SUBLANES, ROUTE_STEP), lambda i: (0, i)),
            pl.BlockSpec((SUBLANES, ROUTE_STEP), lambda i: (0, i)),
            const((N_EXPERTS, LANES)),
        ],
        out_shape=[
            jax.ShapeDtypeStruct((SUBLANES, T), jnp.int32),
            jax.ShapeDtypeStruct((SUBLANES, T), F32),
            jax.ShapeDtypeStruct((N_EXPERTS, LANES), jnp.int32),
        ],
        scratch_shapes=[pltpu.VMEM((N_EXPERTS, 1), F32)],
        compiler_params=pltpu.CompilerParams(
            dimension_semantics=("arbitrary",), vmem_limit_bytes=VMEM_LIMIT),
        name="router",
    )(logits_t, tri)


def _sc_mesh():
    return plsc.VectorSubcoreMesh(core_axis_name="c", subcore_axis_name="s",
                                  num_cores=SC_CORES, num_subcores=SC_SUBCORES)


def _sc_worker_id():
    return lax.axis_index("s") * SC_CORES + lax.axis_index("c")


def _sc_scratch(n_win, width):
    return [
        pltpu.VMEM((n_win, SC_WIN), jnp.int32),
        pltpu.VMEM((n_win, SC_WIN), jnp.int32),
        pltpu.VMEM((2, SC_WIN, width), U32),
        pltpu.SemaphoreType.DMA((2,)),
        pltpu.SemaphoreType.DMA((2,)),
    ]


def _sc_dispatch(rows, idx1, idx2, n_rows):
    _, width = rows.shape
    _, n_win, _ = idx1.shape

    @functools.partial(
        pl.kernel, mesh=_sc_mesh(), out_type=jax.ShapeDtypeStruct((n_rows, width), rows.dtype),
        scratch_types=_sc_scratch(n_win, width), name="sc_dispatch")
    def run(rows_hbm, i1_hbm, i2_hbm, o_hbm, i1_v, i2_v, buf, rsem, wsem):
        wid = _sc_worker_id()
        base = wid * (n_win * SC_WIN)
        pltpu.sync_copy(i1_hbm.at[wid], i1_v)
        pltpu.sync_copy(i2_hbm.at[wid], i2_v)

        def read(j):
            return pltpu.async_copy(rows_hbm.at[pl.ds(base + j * SC_WIN, SC_WIN)], buf.at[j % 2], rsem.at[j % 2])

        reads = {0: read(0)}
        writes = {}
        for j in range(n_win):
            if j + 1 < n_win:
                for d in writes.pop(j - 1, ()):
                    d.wait()
                reads[j + 1] = read(j + 1)
            reads.pop(j).wait()
            writes[j] = (pltpu.async_copy(buf.at[j % 2], o_hbm.at[i1_v.at[j]], wsem.at[j % 2]),
                         pltpu.async_copy(buf.at[j % 2], o_hbm.at[i2_v.at[j]], wsem.at[j % 2]))
        for j in sorted(writes):
            for d in writes[j]:
                d.wait()

    return run(rows, idx1, idx2)


def _sc_gather_pair(table, idx1, idx2):
    _, width = table.shape
    _, n_win, _ = idx1.shape
    n_tok = SC_WORKERS * n_win * SC_WIN
    out_t = jax.ShapeDtypeStruct((n_tok, width), table.dtype)

    @functools.partial(
        pl.kernel, mesh=_sc_mesh(), out_type=(out_t, out_t),
        scratch_types=_sc_scratch(n_win, width), name="sc_combine_gather")
    def run(table_hbm, i1_hbm, i2_hbm, o1_hbm, o2_hbm, i1_v, i2_v, buf, gsem, wsem):
        wid = _sc_worker_id()
        base = wid * (n_win * SC_WIN)
        pltpu.sync_copy(i1_hbm.at[wid], i1_v)
        pltpu.sync_copy(i2_hbm.at[wid], i2_v)
        work = [(i1_v, o1_hbm, j) for j in range(n_win)] + [(i2_v, o2_hbm, j) for j in range(n_win)]

        def gather(t):
            iv, _, j = work[t]
            return pltpu.async_copy(table_hbm.at[iv.at[j]], buf.at[t % 2], gsem.at[t % 2])

        def put(t):
            _, oh, j = work[t]
            return pltpu.async_copy(buf.at[t % 2], oh.at[pl.ds(base + j * SC_WIN, SC_WIN)], wsem.at[t % 2])

        gathers = {0: gather(0)}
        puts = {}
        for t in range(len(work)):
            if t + 1 < len(work):
                if t - 1 in puts:
                    puts.pop(t - 1).wait()
                gathers[t + 1] = gather(t + 1)
            gathers.pop(t).wait()
            puts[t] = put(t)
        for t in sorted(puts):
            puts[t].wait()

    return run(table, idx1, idx2)


def _moe_kernel(layer, n_blocks, be_ref, valid_ref, x_ref, w1_hbm, w3_hbm, w2_hbm, o_ref,
                w1_st, w3_st, w2_st, w1_sc, w3_sc, w2_sc, sems, ordinal_sm):
    i = pl.program_id(0)
    expert = be_ref[i]
    new_expert = jnp.logical_or(i == 0, expert != be_ref[jnp.maximum(i - 1, 0)])

    def weight_copies(e, sl):
        return [pltpu.make_async_copy(hbm.at[layer, e], stage.at[sl], sems.at[k, sl])
                for k, (hbm, stage) in enumerate(((w1_hbm, w1_st), (w3_hbm, w3_st), (w2_hbm, w2_st)))]

    @pl.when(i == 0)
    def _():
        ordinal_sm[0] = -1
        for cp in weight_copies(expert, 0):
            cp.start()

    @pl.when(new_expert)
    def _():
        ordinal = ordinal_sm[0] + 1
        ordinal_sm[0] = ordinal
        slot = ordinal & 1
        for cp in weight_copies(expert, slot):
            cp.wait()
        w1_sc[...] = w1_st[slot].astype(BF16)
        w3_sc[...] = w3_st[slot].astype(BF16)
        w2_sc[...] = w2_st[slot].astype(BF16)
        nxt_pos = lax.while_loop(
            lambda p: jnp.logical_and(p < n_blocks, be_ref[jnp.minimum(p, n_blocks - 1)] == expert),
            lambda p: p + 1, i + 1)

        @pl.when(nxt_pos < n_blocks)
        def _():
            for cp in weight_copies(be_ref[jnp.minimum(nxt_pos, n_blocks - 1)], 1 - slot):
                cp.start()

    def up(rows):
        xa, xb = _unpack_bf16_pair(x_ref[rows, :])
        xa = xa.astype(BF16)
        xb = xb.astype(BF16)
        h1 = (jnp.dot(xa, w1_sc[:D_HALF], preferred_element_type=F32)
              + jnp.dot(xb, w1_sc[D_HALF:], preferred_element_type=F32))
        h3 = (jnp.dot(xa, w3_sc[:D_HALF], preferred_element_type=F32)
              + jnp.dot(xb, w3_sc[D_HALF:], preferred_element_type=F32))
        return h1, h3

    def down(rows, h1, h3):
        hdn = (jax.nn.silu(h1) * h3).astype(BF16)
        y = jnp.dot(hdn, w2_sc[...], preferred_element_type=F32)
        o_ref[rows, :] = _pack_row_halves(y)

    n_sub = MOE_BM // MOE_SUB
    subs = [slice(k * MOE_SUB, (k + 1) * MOE_SUB) for k in range(n_sub)]
    valid = valid_ref[i]
    chains = (valid + (MOE_SUB - 1)) // MOE_SUB

    for live in range(n_sub + 1):
        @pl.when(chains == live)
        def _(live=live):
            ups = {}
            if live:
                ups[0] = up(subs[0])
            for k in range(live):
                if k + 1 < live:
                    ups[k + 1] = up(subs[k + 1])
                down(subs[k], *ups.pop(k))
            if live < n_sub:
                o_ref[live * MOE_SUB:, :] = jnp.zeros((MOE_BM - live * MOE_SUB, D_HALF), o_ref.dtype)


def _moe_blocks(xbuf, block_e, valid, w1, w3, w2, layer):
    n_rows, _ = xbuf.shape
    D = D_MODEL
    n_blocks = n_rows // MOE_BM
    rows = lambda i, be, nu: (i, 0)
    grid_spec = pltpu.PrefetchScalarGridSpec(
        num_scalar_prefetch=2,
        grid=(n_blocks,),
        in_specs=[
            pl.BlockSpec((MOE_BM, D_HALF), rows),
            pl.BlockSpec(memory_space=pl.ANY),
            pl.BlockSpec(memory_space=pl.ANY),
            pl.BlockSpec(memory_space=pl.ANY),
        ],
        out_specs=pl.BlockSpec((MOE_BM, D_HALF), rows),
        scratch_shapes=[
            pltpu.VMEM((2, D, D_EXPERT), F32),
            pltpu.VMEM((2, D, D_EXPERT), F32),
            pltpu.VMEM((2, D_EXPERT, D), F32),
            pltpu.VMEM((D, D_EXPERT), BF16),
            pltpu.VMEM((D, D_EXPERT), BF16),
            pltpu.VMEM((D_EXPERT, D), BF16),
            pltpu.SemaphoreType.DMA((3, 2)),
            pltpu.SMEM((1,), jnp.int32),
        ],
    )
    return pl.pallas_call(
        functools.partial(_moe_kernel, layer, n_blocks),
        grid_spec=grid_spec,
        out_shape=jax.ShapeDtypeStruct((n_rows, D_HALF), U32),
        compiler_params=pltpu.CompilerParams(
            dimension_semantics=("arbitrary",), vmem_limit_bytes=VMEM_LIMIT),
        name="moe_experts",
    )(block_e, valid, xbuf, w1, w3, w2)


def _moe_combine_norm(x, y1, y2, gate_rows, g, b):
    n = x.shape[0]
    pad = jnp.zeros((LANES - SUBLANES, n), F32)
    gates = jnp.concatenate([gate_rows, pad], axis=0).T
    g1 = gates[:, 0:1]
    g2 = gates[:, 1:2]
    a1, b1 = _unpack_bf16_pair(y1)
    a2, b2 = _unpack_bf16_pair(y2)
    f = jnp.concatenate([g1 * a1 + g2 * a2, g1 * b1 + g2 * b2], axis=1)
    return _layer_norm_rows(ALPHA * x + f, g, b)


def _combine_kernel(x_ref, y1_ref, y2_ref, gates_ref, g_ref, b_ref, o_ref):
    o_ref[...] = _moe_combine_norm(x_ref[...], y1_ref[...], y2_ref[...], gates_ref[...], g_ref[...], b_ref[...])


def _combine_part_kernel(x_ref, y1_ref, y2_ref, gates_ref, g_ref, b_ref, prev_ref, o_ref):
    del prev_ref
    _combine_kernel(x_ref, y1_ref, y2_ref, gates_ref, g_ref, b_ref, o_ref)


def _combine_ln(xf, y_parts, gates, ln_g, ln_b):
    T, D = xf.shape
    n_parts = len(y_parts)
    steps = T // (COMB_T * n_parts)
    const = lambda shape: pl.BlockSpec(shape, lambda i: (0,) * len(shape))
    out = None
    for part, (y1, y2) in enumerate(y_parts):
        off = part * steps
        glob = lambda w, off=off: pl.BlockSpec((COMB_T, w), lambda i: (i + off, 0))
        local = lambda w: pl.BlockSpec((COMB_T, w), lambda i: (i, 0))
        in_specs = [glob(D), local(D_HALF), local(D_HALF),
                    pl.BlockSpec((SUBLANES, COMB_T), lambda i, off=off: (0, i + off)), const((1, D)), const((1, D))]
        args = [xf, y1, y2, gates, ln_g.reshape(1, D), ln_b.reshape(1, D)]
        if out is not None:
            in_specs.append(pl.BlockSpec(memory_space=pl.ANY))
            args.append(out)
        out = pl.pallas_call(
            _combine_kernel if out is None else _combine_part_kernel,
            grid=(steps,),
            in_specs=in_specs,
            out_specs=glob(D),
            out_shape=jax.ShapeDtypeStruct((T, D), F32),
            input_output_aliases={} if out is None else {len(args) - 1: 0},
            compiler_params=pltpu.CompilerParams(
                dimension_semantics=("arbitrary",), vmem_limit_bytes=VMEM_LIMIT),
            name="moe_combine_ln",
        )(*args)
    return out


def _hier_moe(xpk, logits_t, w1, w3, w2, layer, gather_parts):
    B, S, _ = xpk.shape
    T = B * S
    idx, gates, cnt = _router(logits_t)
    counts = cnt[:, 0]
    padded = ((counts + MOE_BM - 1) // MOE_BM) * MOE_BM
    pends = jnp.cumsum(padded)
    pstarts = pends - padded
    experts = jnp.arange(N_EXPERTS, dtype=jnp.int32)[:, None]

    def dest(e_row, rank_row):
        return jnp.sum(jnp.where(e_row[None, :] == experts, pstarts[:, None], 0), axis=0) + rank_row

    n_win = T // (SC_WORKERS * SC_WIN)
    dest1 = dest(idx[0], idx[2]).reshape(SC_WORKERS, n_win, SC_WIN)
    dest2 = dest(idx[1], idx[3]).reshape(SC_WORKERS, n_win, SC_WIN)
    n_blocks = -(-(T * TOP_K + N_EXPERTS * (MOE_BM - 1)) // MOE_BM)
    n_rows = n_blocks * MOE_BM
    block_start = jnp.arange(n_blocks, dtype=jnp.int32) * MOE_BM
    block_e = jnp.minimum(jnp.sum(block_start[:, None] >= pends[None, :], axis=1), N_EXPERTS - 1).astype(jnp.int32)
    of_block = block_e[:, None] == experts[:, 0][None, :]
    pick = lambda v: jnp.sum(jnp.where(of_block, v[None, :], 0), axis=1)
    valid = jnp.clip(pick(counts) - (block_start - pick(pstarts)), 0, MOE_BM).astype(jnp.int32)
    xbuf = _sc_dispatch(xpk.reshape(T, D_HALF), dest1, dest2, n_rows)
    ybuf = _moe_blocks(xbuf, block_e, valid, w1, w3, w2, layer)
    part_shape = (gather_parts, SC_WORKERS, n_win // gather_parts, SC_WIN)
    d1, d2 = dest1.reshape(part_shape), dest2.reshape(part_shape)
    y_parts = [_sc_gather_pair(ybuf, d1[p], d2[p]) for p in range(gather_parts)]
    return y_parts, gates


def _att_head_order():
    order = []
    for p in range(N_HEADS // 2):
        jj, m = divmod(p, 4)
        order += [8 * jj + m, 8 * jj + 4 + m]
    return order


ATT_HEAD_ORDER = _att_head_order()


def _attn_kernel(xprev_ref, y1_ref, y2_ref, gates_a_ref, gates_b_ref, g_prev_ref, b_prev_ref, wqkv_ref, bias_ref,
                 sink_ref, wo_ref, g_ref, b_ref, wrt_ref, brt_ref, o_ref, opk_ref, lg_ref,
                 kv_ext, o_sc, s_sc0, s_sc1, p_sc0, p_sc1):
    s = pl.program_id(1)
    tq = ATT_TQ
    s_bufs = (s_sc0, s_sc1)
    p_bufs = (p_sc0, p_sc1)
    gate_refs = (gates_a_ref, gates_b_ref)
    assert ATT_NSEQ == len(gate_refs)

    @pl.when(s == 0)
    def _():
        kv_ext[:, 0:WINDOW, :] = jnp.zeros((ATT_NSEQ, WINDOW, 2 * KV_DIM), BF16)

    def layer_input(sq):
        return _moe_combine_norm(xprev_ref[sq], y1_ref[sq], y2_ref[sq], gate_refs[sq][...],
                                 g_prev_ref[...], b_prev_ref[...])

    def project_qkv(sq, x):
        qkv = jnp.dot(x.astype(BF16), wqkv_ref[...], preferred_element_type=F32)
        kv_ext[sq, WINDOW:WINDOW + tq, :] = qkv[:, Q_DIM:].astype(BF16)
        return (qkv[:, :Q_DIM] * (HEAD_DIM ** -0.5 * LOG2E)).astype(BF16)

    lane = lax.broadcasted_iota(jnp.int32, (WINDOW, LANES), 1)
    low = lane < HEAD_DIM
    sub = lax.broadcasted_iota(jnp.int32, (LANES, WINDOW), 0)
    top = sub < HEAD_DIM
    first = jnp.where(s == 0, 1, 0)
    nt = (((1,), (1,)), ((), ()))
    zero = jnp.zeros((), BF16)

    tiles = [(sq, n, j) for n in range(ATT_NB) for j in range(2) for sq in range(ATT_NSEQ)]

    def scores(t, qs_all):
        sq, n, j = tiles[t]
        q = qs_all[sq]
        r0 = n * WINDOW
        k_tile = kv_ext[sq, r0:r0 + 2 * WINDOW, j * LANES:(j + 1) * LANES]
        parts = []
        for m in range(4):
            p = 4 * j + m
            qt = q[r0:r0 + WINDOW, p * LANES:(p + 1) * LANES]
            parts.append(jnp.where(low, qt, zero))
            parts.append(jnp.where(low, zero, qt))
        qs = jnp.concatenate(parts, axis=0)
        bias_sel = first if n == 0 else 0
        s_bufs[t % 2][...] = (lax.dot_general(k_tile, qs, nt, preferred_element_type=F32)
                              + bias_ref[bias_sel, j])

    def softmax_pv(t):
        sq, n, j = tiles[t]
        r0 = n * WINDOW
        s_sc = s_bufs[t % 2]
        p_sc = p_bufs[t % 2]
        inv_l = []
        for h in range(8):
            hc = slice(h * WINDOW, (h + 1) * WINDOW)
            sink = sink_ref[8 * j + h] * LOG2E
            mx = jnp.maximum(jnp.max(s_sc[:, hc], axis=0, keepdims=True), sink)
            pr = jnp.exp2(s_sc[:, hc] - mx)
            p_sc[:, hc] = pr.astype(BF16)
            inv_l.append(1.0 / (jnp.sum(pr, axis=0, keepdims=True) + jnp.exp2(sink - mx)))
        v_tile = kv_ext[sq, r0:r0 + 2 * WINDOW, KV_DIM + j * LANES:KV_DIM + (j + 1) * LANES]
        tn = (((0,), (0,)), ((), ()))
        ov = lax.dot_general(v_tile, p_sc[...], tn, preferred_element_type=F32)
        for m in range(4):
            p = 4 * j + m
            o_even = ov[:, (2 * m) * WINDOW:(2 * m + 1) * WINDOW] * inv_l[2 * m]
            o_odd = ov[:, (2 * m + 1) * WINDOW:(2 * m + 2) * WINDOW] * inv_l[2 * m + 1]
            o_sc[sq, p * LANES:(p + 1) * LANES, r0:r0 + WINDOW] = jnp.where(top, o_even, o_odd).astype(BF16)

    def project_out(sq):
        tn = (((0,), (0,)), ((), ()))
        return lax.dot_general(o_sc[sq], wo_ref[...], tn, preferred_element_type=F32)

    def finish(sq, x, out):
        xn = _layer_norm_rows(ALPHA * x + out, g_ref[...], b_ref[...])
        o_ref[sq] = xn
        opk_ref[sq] = _pack_row_halves(xn)
        lg_ref[sq] = _router_logits_t(xn, wrt_ref[...], brt_ref[...])

    xs = [layer_input(0), layer_input(1)]
    qs_all = [project_qkv(0, xs[0]), project_qkv(1, xs[1])]
    scores(0, qs_all)
    for t in range(len(tiles)):
        if t + 1 < len(tiles):
            scores(t + 1, qs_all)
        softmax_pv(t)
        if tiles[t][0] == 0 and all(sq != 0 for sq, _, _ in tiles[t + 1:]):
            out_first = project_out(0)
    kv_ext[:, 0:WINDOW, :] = kv_ext[:, tq:tq + WINDOW, :]
    out_second = project_out(1)
    finish(0, xs[0], out_first)
    finish(1, xs[1], out_second)


def _attn_bias():
    qi = np.arange(WINDOW)[:, None]
    sj = np.arange(2 * WINDOW)[None, :]
    dist = qi - sj + WINDOW
    valid = (dist >= 0) & (dist < WINDOW)
    slopes = 2.0 ** (-8.0 * np.arange(1, N_HEADS + 1, dtype=np.float32) / N_HEADS)
    slopes = slopes.astype(np.float32)[ATT_HEAD_ORDER]
    sb = -(slopes[:, None, None] * dist.astype(np.float32)[None])
    later = np.where(valid[None], sb, -np.inf)
    first = np.where((valid & (sj >= WINDOW))[None], sb, -np.inf)
    bias = np.stack([later, first]).astype(np.float32) * np.float32(LOG2E)
    bias = bias.reshape(2, 2, 8, WINDOW, 2 * WINDOW).transpose(0, 1, 4, 2, 3).reshape(2, 2, 2 * WINDOW, 8 * WINDOW)
    return jnp.asarray(np.ascontiguousarray(bias))


ATT_N_INPUTS = 15


def _attn_part_kernel(*refs):
    _attn_kernel(*refs[:ATT_N_INPUTS], *refs[ATT_N_INPUTS + 3:])


def _attn_layer(x_prev, y_parts, gates, g_prev, b_prev, w_qkv, sinks, w_o, ln_g, ln_b, router_w, router_b):
    B, S, D = x_prev.shape
    steps = S // ATT_TQ
    assert ATT_HEAD_ORDER == list(np.arange(N_HEADS).reshape(2, 2, 4).transpose(0, 2, 1).reshape(-1))
    wq = w_qkv[:, :Q_DIM].reshape(D, 2, 2, 4, HEAD_DIM).transpose(0, 1, 3, 2, 4).reshape(D, Q_DIM)
    wqkv = jnp.concatenate([wq, w_qkv[:, Q_DIM:]], axis=1).astype(BF16)
    wo_t = w_o.reshape(2, 2, 4, HEAD_DIM, D).transpose(0, 2, 1, 3, 4).reshape(Q_DIM, D).astype(BF16)
    sink = sinks.reshape(2, 2, 4).transpose(0, 2, 1).reshape(N_HEADS, 1, 1)
    bias = _attn_bias()
    const = lambda shape: pl.BlockSpec(shape, lambda b, s: (0,) * len(shape))
    n_parts = len(y_parts)
    rows_per_part = B // n_parts
    pairs = rows_per_part // ATT_NSEQ
    outs = None
    for part, (y1, y2) in enumerate(y_parts):
        off = part * pairs
        glob = lambda w, off=off: pl.BlockSpec((ATT_NSEQ, ATT_TQ, w), lambda b, s: (b + off, s, 0))
        local = lambda w: pl.BlockSpec((ATT_NSEQ, ATT_TQ, w), lambda b, s: (b, s, 0))
        gate_rows = lambda sq, off=off: pl.BlockSpec(
            (SUBLANES, ATT_TQ), lambda b, s: (0, (ATT_NSEQ * (b + off) + sq) * steps + s))
        in_specs = [
            glob(D),
            local(D_HALF),
            local(D_HALF),
            gate_rows(0),
            gate_rows(1),
            const((1, D)),
            const((1, D)),
            const((D, Q_DIM + 2 * KV_DIM)),
            const((2, 2, 2 * WINDOW, 8 * WINDOW)),
            const((N_HEADS, 1, 1)),
            const((Q_DIM, D)),
            const((1, D)),
            const((1, D)),
            const((2 * ROUTE_ROWS, D)),
            const((ROUTE_ROWS, 1)),
        ]
        args = [x_prev, y1.reshape(rows_per_part, S, D_HALF), y2.reshape(rows_per_part, S, D_HALF), gates, gates,
                g_prev.reshape(1, D), b_prev.reshape(1, D), wqkv, bias, sink, wo_t,
                ln_g.reshape(1, D), ln_b.reshape(1, D), router_w, router_b]
        aliases = {}
        body = _attn_kernel
        if outs is not None:
            aliases = {len(args) + k: k for k in range(3)}
            in_specs += [pl.BlockSpec(memory_space=pl.ANY)] * 3
            args += list(outs)
            body = _attn_part_kernel
        outs = pl.pallas_call(
            body,
            grid=(pairs, steps),
            in_specs=in_specs,
            out_specs=[glob(D), glob(D_HALF),
                       pl.BlockSpec((ATT_NSEQ, ROUTE_ROWS, ATT_TQ), lambda b, s, off=off: (b + off, 0, s))],
            out_shape=[jax.ShapeDtypeStruct((B, S, D), F32), jax.ShapeDtypeStruct((B, S, D_HALF), U32),
                       jax.ShapeDtypeStruct((B, ROUTE_ROWS, S), F32)],
            input_output_aliases=aliases,
            scratch_shapes=[
                pltpu.VMEM((ATT_NSEQ, ATT_TQ + WINDOW, 2 * KV_DIM), BF16),
                pltpu.VMEM((ATT_NSEQ, Q_DIM, ATT_TQ), BF16),
                pltpu.VMEM((2 * WINDOW, 8 * WINDOW), F32),
                pltpu.VMEM((2 * WINDOW, 8 * WINDOW), F32),
                pltpu.VMEM((2 * WINDOW, 8 * WINDOW), BF16),
                pltpu.VMEM((2 * WINDOW, 8 * WINDOW), BF16),
            ],
            compiler_params=pltpu.CompilerParams(
                dimension_semantics=("arbitrary", "arbitrary"), vmem_limit_bytes=VMEM_LIMIT),
            name="swa_attn_ln",
        )(*args)
    return outs


def kernel(x, rec_w_in, rec_conv_w, rec_conv_b, rec_w_r, rec_b_r, rec_w_i, rec_b_i, rec_lambda, rec_w_out,
           att_w_qkv, att_sinks, att_w_o, moe_w_group, moe_b_group, moe_w_expert, moe_b_expert,
           moe_w1, moe_w3, moe_w2, ln_g, ln_b):
    assert DEPTH == 2
    B, S, D = x.shape

    router = [_router_weights(moe_w_group[layer], moe_b_group[layer], moe_w_expert[layer], moe_b_expert[layer])
              for layer in range(DEPTH)]

    x1, x1_pk, logits1 = _rglru_layer(x, rec_w_in[0], rec_conv_w[0], rec_conv_b[0], rec_w_r[0], rec_b_r[0],
                                      rec_w_i[0], rec_b_i[0], rec_lambda[0], rec_w_out[0], ln_g[0, 0], ln_b[0, 0],
                                      *router[0])
    y_parts, gates = _hier_moe(x1_pk, logits1, moe_w1, moe_w3, moe_w2, 0, gather_parts=ATT_PARTS)
    x3, x3_pk, logits3 = _attn_layer(x1, y_parts, gates, ln_g[0, 1], ln_b[0, 1], att_w_qkv[0], att_sinks[0],
                                     att_w_o[0], ln_g[1, 0], ln_b[1, 0], *router[1])
    y_parts, gates = _hier_moe(x3_pk, logits3, moe_w1, moe_w3, moe_w2, 1, gather_parts=1)
    out = _combine_ln(x3.reshape(B * S, D), y_parts, gates, ln_g[1, 1], ln_b[1, 1])
    return out.reshape(B, S, D)
```

```python
import functools

import jax
import jax.numpy as jnp
import numpy as np
from jax import lax
from jax.experimental import pallas as pl
from jax.experimental.pallas import tpu as pltpu
from jax.experimental.pallas import tpu_sc as plsc

F32 = jnp.float32
BF16 = jnp.bfloat16
U32 = jnp.uint32

D_MODEL = 1024
DEPTH = 2
D_RNN = 1280
LRU_BLOCKS = 16
LRU_BLOCK_W = D_RNN // LRU_BLOCKS
CONV_W = 4
LRU_C = 8.0
N_HEADS = 16
N_KV_HEADS = 4
HEAD_DIM = 64
WINDOW = 128
Q_DIM = N_HEADS * HEAD_DIM
KV_DIM = N_KV_HEADS * HEAD_DIM
N_GROUPS = 4
EXPERTS_PER_GROUP = 8
N_EXPERTS = N_GROUPS * EXPERTS_PER_GROUP
TOP_K = 2
D_EXPERT = 512
ALPHA = (2 * DEPTH) ** 0.25
LN_EPS = 1e-5
LOG2E = 1.4426950408889634

LANES = 128
SUBLANES = 8
VMEM_LIMIT = 56 * 1024 * 1024

REC_TS = 256
REC_GROUPS = REC_TS // SUBLANES
REC_NSEQ = 2
GATE_TILE = 256
GATE_WIN = 512
GATE_WIN_STARTS = (0, 128, 384, 640, 768)
N_GATE_TILES = D_RNN // GATE_TILE

ROUTE_T = 512
ROUTE_STEP = 2048
ROUTE_ROWS = 40

MOE_BM = 1024
MOE_SUB = 256

ATT_TQ = 256
ATT_NB = ATT_TQ // WINDOW
ATT_NSEQ = 2
ATT_PARTS = 2

COMB_T = 1024

D_HALF = D_MODEL // 2

SC_CORES = 2
SC_SUBCORES = 16
SC_WORKERS = SC_CORES * SC_SUBCORES
SC_WIN = 64


def _layer_norm_rows(z, g, b):
    mu = jnp.mean(z, axis=-1, keepdims=True)
    zc = z - mu
    var = jnp.mean(zc * zc, axis=-1, keepdims=True)
    return zc * lax.rsqrt(var + LN_EPS) * g + b


def _pack_bf16_pair(a, b):
    ua = lax.bitcast_convert_type(a.astype(BF16).astype(F32), U32)
    ub = lax.bitcast_convert_type(b.astype(BF16).astype(F32), U32)
    return (ua >> 16) | (ub & jnp.uint32(0xFFFF0000))


def _unpack_bf16_pair(w):
    a = lax.bitcast_convert_type(w << 16, F32)
    b = lax.bitcast_convert_type(w & jnp.uint32(0xFFFF0000), F32)
    return a, b


def _pack_row_halves(x):
    return _pack_bf16_pair(x[:, :D_HALF], x[:, D_HALF:])


def _router_logits_t(x, w_split, bias):
    xhi = x.astype(BF16)
    xlo = (x - xhi.astype(F32)).astype(BF16)
    nt = (((1,), (1,)), ((), ()))
    both = lax.dot_general(w_split, xhi, nt, preferred_element_type=F32)
    low = lax.dot_general(w_split[:ROUTE_ROWS], xlo, nt, preferred_element_type=F32)
    return both[:ROUTE_ROWS] + both[ROUTE_ROWS:] + low + bias


def _rglru_kernel(x_ref, perm_ref, perm_t_ref, w_in_ref, convw_ref, convb_ref, wg_ref, br_ref, bi_ref, lam_ref,
                  w_out_ref, g_ref, b_ref, wrt_ref, brt_ref, o_ref, opk_ref, lg_ref,
                  xr_ext, tail_sc, a_sc, u_sc, h_carry):
    s = pl.program_id(1)
    ts = REC_TS
    halo = (CONV_W - 1) * SUBLANES

    @pl.when(s == 0)
    def _():
        tail_sc[...] = jnp.zeros((REC_NSEQ, halo, D_RNN), F32)
        h_carry[...] = jnp.zeros((REC_NSEQ, 1, D_RNN), F32)

    row = lax.broadcasted_iota(jnp.int32, (SUBLANES, D_RNN), 0)
    nlam = -lam_ref[...]
    sp = jnp.maximum(nlam, 0.0) + jnp.log1p(jnp.exp(-jnp.abs(nlam)))
    log2a_scale = (-LRU_C * LOG2E) * sp

    def project(q):
        xp = jnp.dot(perm_ref[...], x_ref[q].astype(BF16), preferred_element_type=F32).astype(BF16)
        proj = jnp.dot(xp, w_in_ref[...], preferred_element_type=F32)
        return proj[:, :D_RNN], proj[:, D_RNN:]

    def conv_gates(q, xr):
        for k in range(CONV_W - 1):
            r0 = ts - halo + k * SUBLANES
            cur = xr[r0:r0 + SUBLANES, :]
            prev = tail_sc[q, k * SUBLANES:(k + 1) * SUBLANES, :]
            xr_ext[q, k * SUBLANES:(k + 1) * SUBLANES, :] = jnp.where(
                row == 0, pltpu.roll(prev, 1, axis=0), pltpu.roll(cur, 1, axis=0))
        tail_sc[q] = xr[ts - halo:, :]
        xr_ext[q, halo:halo + ts, :] = xr
        xc = convb_ref[...] + convw_ref[CONV_W - 1:CONV_W, :] * xr
        for k in range(CONV_W - 1):
            xc = xc + convw_ref[k:k + 1, :] * xr_ext[q, k * SUBLANES:k * SUBLANES + ts, :]
        xcb = xc.astype(BF16)
        pres = [jnp.dot(xcb[:, GATE_WIN_STARTS[j]:GATE_WIN_STARTS[j] + GATE_WIN], wg_ref[j],
                        preferred_element_type=F32) for j in range(N_GATE_TILES)]
        for j, pre in enumerate(pres):
            cs = j * GATE_TILE
            r = jax.nn.sigmoid(pre[:, :GATE_TILE] + br_ref[:, cs:cs + GATE_TILE])
            i = jax.nn.sigmoid(pre[:, GATE_TILE:] + bi_ref[:, cs:cs + GATE_TILE])
            a = jnp.exp2(r * log2a_scale[:, cs:cs + GATE_TILE])
            s1 = 1.0 - a * a
            mult = jnp.where(s1 > 0.0, s1 * lax.rsqrt(s1), 0.0)
            u = mult * (i * xc[:, cs:cs + GATE_TILE])
            a_sc[q, :, cs:cs + GATE_TILE] = a
            u_sc[q, :, cs:cs + GATE_TILE] = u

    def segment_scan(q):
        h = jnp.zeros((SUBLANES, D_RNN), F32)
        prod = jnp.ones((SUBLANES, D_RNN), F32)
        for gidx in range(REC_GROUPS):
            rows = slice(gidx * SUBLANES, (gidx + 1) * SUBLANES)
            a8 = a_sc[q, rows, :]
            h = a8 * h + u_sc[q, rows, :]
            prod = a8 * prod
            u_sc[q, rows, :] = h
            a_sc[q, rows, :] = prod
        return h, prod

    def recur_out(q, gate, seg):
        seg_h, seg_a = seg
        for d in (1, 2, 4):
            keep = row >= d
            a_sh = jnp.where(keep, pltpu.roll(seg_a, d, axis=0), 1.0)
            h_sh = jnp.where(keep, pltpu.roll(seg_h, d, axis=0), 0.0)
            seg_h = seg_a * h_sh + seg_h
            seg_a = seg_a * a_sh
        h_in = h_carry[q]
        after = seg_a * h_in + seg_h
        enter = jnp.where(row == 0, h_in, pltpu.roll(after, 1, axis=0))
        h_carry[q] = after[SUBLANES - 1:SUBLANES, :]
        hs = (u_sc[q].reshape(REC_GROUPS, SUBLANES, D_RNN)
              + a_sc[q].reshape(REC_GROUPS, SUBLANES, D_RNN) * enter[None]).reshape(ts, D_RNN)
        y = hs * jax.nn.gelu(gate)
        y_t = jnp.dot(perm_t_ref[...], y.astype(BF16), preferred_element_type=F32).astype(BF16)
        return jnp.dot(y_t, w_out_ref[...], preferred_element_type=F32)

    def finish(q, out):
        z = ALPHA * x_ref[q] + out
        xn = _layer_norm_rows(z, g_ref[...], b_ref[...])
        o_ref[q] = xn
        opk_ref[q] = _pack_row_halves(xn)
        lg_ref[q] = _router_logits_t(xn, wrt_ref[...], brt_ref[...])

    assert REC_NSEQ == 2
    gate_a, xr_a = project(0)
    conv_gates(0, xr_a)
    gate_b, xr_b = project(1)
    conv_gates(1, xr_b)
    seg_a = segment_scan(0)
    seg_b = segment_scan(1)
    out_a = recur_out(0, gate_a, seg_a)
    out_b = recur_out(1, gate_b, seg_b)
    finish(0, out_a)
    finish(1, out_b)


def _band_gate_weights(w_r, w_i):
    spread = jnp.asarray(np.tile(np.eye(LRU_BLOCK_W, dtype=np.float32), (1, LRU_BLOCKS)), BF16)
    blk = np.arange(D_RNN) // LRU_BLOCK_W
    on_diag = jnp.asarray(blk[:, None] == blk[None, :])

    def dense(w):
        rows = w.reshape(D_RNN, LRU_BLOCK_W).astype(BF16)
        return jnp.where(on_diag, jnp.dot(rows, spread, preferred_element_type=F32), 0.0)

    wr, wi = dense(w_r), dense(w_i)
    tiles = []
    for j in range(N_GATE_TILES):
        ws = GATE_WIN_STARTS[j]
        cs = j * GATE_TILE
        lo_blk = cs // LRU_BLOCK_W
        hi_blk = (cs + GATE_TILE - 1) // LRU_BLOCK_W
        assert ws <= lo_blk * LRU_BLOCK_W and (hi_blk + 1) * LRU_BLOCK_W <= ws + GATE_WIN
        tiles.append(jnp.concatenate([wr[ws:ws + GATE_WIN, cs:cs + GATE_TILE],
                                      wi[ws:ws + GATE_WIN, cs:cs + GATE_TILE]], axis=1))
    return jnp.stack(tiles).astype(BF16)


def _rglru_layer(x, w_in, conv_w, conv_b, w_r, b_r, w_i, b_i, lam, w_out, ln_g, ln_b, router_w, router_b):
    B, S, D = x.shape
    wg = _band_gate_weights(w_r, w_i)
    rho = np.arange(REC_TS)
    perm_np = np.zeros((REC_TS, REC_TS), np.float32)
    perm_np[rho, (rho % SUBLANES) * REC_GROUPS + rho // SUBLANES] = 1.0
    perm = jnp.asarray(perm_np, BF16)
    perm_t = jnp.asarray(perm_np.T, BF16)
    row = lambda v: v.reshape(1, -1)
    const = lambda shape: pl.BlockSpec(shape, lambda b, s: (0,) * len(shape))
    tile = lambda w: pl.BlockSpec((REC_NSEQ, REC_TS, w), lambda b, s: (b, s, 0))
    halo = (CONV_W - 1) * SUBLANES
    return pl.pallas_call(
        _rglru_kernel,
        grid=(B // REC_NSEQ, S // REC_TS),
        in_specs=[
            tile(D),
            const((REC_TS, REC_TS)),
            const((REC_TS, REC_TS)),
            const((D, 2 * D_RNN)),
            const((CONV_W, D_RNN)),
            const((1, D_RNN)),
            const((N_GATE_TILES, GATE_WIN, 2 * GATE_TILE)),
            const((1, D_RNN)),
            const((1, D_RNN)),
            const((1, D_RNN)),
            const((D_RNN, D)),
            const((1, D)),
            const((1, D)),
            const((2 * ROUTE_ROWS, D)),
            const((ROUTE_ROWS, 1)),
        ],
        out_specs=[tile(D), tile(D_HALF),
                   pl.BlockSpec((REC_NSEQ, ROUTE_ROWS, REC_TS), lambda b, s: (b, 0, s))],
        out_shape=[jax.ShapeDtypeStruct((B, S, D), F32), jax.ShapeDtypeStruct((B, S, D_HALF), U32),
                   jax.ShapeDtypeStruct((B, ROUTE_ROWS, S), F32)],
        scratch_shapes=[
            pltpu.VMEM((REC_NSEQ, halo + REC_TS, D_RNN), F32),
            pltpu.VMEM((REC_NSEQ, halo, D_RNN), F32),
            pltpu.VMEM((REC_NSEQ, REC_TS, D_RNN), F32),
            pltpu.VMEM((REC_NSEQ, REC_TS, D_RNN), F32),
            pltpu.VMEM((REC_NSEQ, 1, D_RNN), F32),
        ],
        compiler_params=pltpu.CompilerParams(
            dimension_semantics=("arbitrary", "arbitrary"), vmem_limit_bytes=VMEM_LIMIT),
        name="rglru_ln",
    )(x, perm, perm_t, w_in.astype(BF16), conv_w, row(conv_b), wg, row(b_r), row(b_i), row(lam), w_out.astype(BF16),
      row(ln_g), row(ln_b), router_w, router_b)


def _router_kernel(logits_ref, tri_ref, idx_ref, gate_ref, cnt_ref, base_sc):
    step = pl.program_id(0)
    tr = ROUTE_T

    @pl.when(step == 0)
    def _():
        base_sc[...] = jnp.zeros((N_EXPERTS, 1), F32)

    row8 = lax.broadcasted_iota(jnp.int32, (SUBLANES, tr), 0).astype(F32)
    rowe = lax.broadcasted_iota(jnp.int32, (N_EXPERTS, tr), 0).astype(F32)
    neg_inf = -jnp.inf
    for sub in range(ROUTE_STEP // ROUTE_T):
        cols = slice(sub * tr, (sub + 1) * tr)
        _route_tile(logits_ref[0, :, cols], row8, rowe, neg_inf, tri_ref, idx_ref, gate_ref, base_sc, cols)
    cnt_ref[...] = jnp.broadcast_to(base_sc[...], (N_EXPERTS, LANES)).astype(jnp.int32)


def _route_tile(logits, row8, rowe, neg_inf, tri_ref, idx_ref, gate_ref, base_sc, cols):
    tr = ROUTE_T
    g = jnp.where(row8 < N_GROUPS, logits[N_EXPERTS:N_EXPERTS + SUBLANES, :], neg_inf)
    gmax = jnp.max(g, axis=0, keepdims=True)
    gidx = jnp.min(jnp.where(g == gmax, row8, SUBLANES), axis=0, keepdims=True)
    g_gate = 1.0 / jnp.sum(jnp.exp(g - gmax), axis=0, keepdims=True)

    esel = logits[0:EXPERTS_PER_GROUP, :]
    for grp in range(1, N_GROUPS):
        esel = jnp.where(gidx == grp, logits[grp * EXPERTS_PER_GROUP:(grp + 1) * EXPERTS_PER_GROUP, :], esel)
    v1 = jnp.max(esel, axis=0, keepdims=True)
    i1 = jnp.min(jnp.where(esel == v1, row8, SUBLANES), axis=0, keepdims=True)
    esel2 = jnp.where(row8 == i1, neg_inf, esel)
    v2 = jnp.max(esel2, axis=0, keepdims=True)
    i2 = jnp.min(jnp.where(esel2 == v2, row8, SUBLANES), axis=0, keepdims=True)
    e21 = jnp.exp(v2 - v1)
    inv = 1.0 / (1.0 + e21)
    gate1 = inv * g_gate
    gate2 = e21 * inv * g_gate
    e1 = gidx * EXPERTS_PER_GROUP + i1
    e2 = gidx * EXPERTS_PER_GROUP + i2

    hit1 = rowe == e1
    hit2 = rowe == e2
    member = jnp.where(hit1, 1.0, jnp.where(hit2, 1.0, 0.0))
    before = jnp.dot(member.astype(BF16), tri_ref[...], preferred_element_type=F32) + base_sc[...]
    rank1 = jnp.sum(jnp.where(hit1, before, 0.0), axis=0, keepdims=True)
    rank2 = jnp.sum(jnp.where(hit2, before, 0.0), axis=0, keepdims=True)
    base_sc[...] = base_sc[...] + jnp.sum(member, axis=1, keepdims=True)

    zi = jnp.zeros((1, tr), jnp.int32)
    idx_ref[:, cols] = jnp.concatenate(
        [e1.astype(jnp.int32), e2.astype(jnp.int32), rank1.astype(jnp.int32), rank2.astype(jnp.int32),
         zi, zi, zi, zi], axis=0)
    zf = jnp.zeros((1, tr), F32)
    gate_ref[:, cols] = jnp.concatenate([gate1, gate2, zf, zf, zf, zf, zf, zf], axis=0)


def _router_weights(w_rg, b_rg, w_re, b_re):
    D = w_rg.shape[0]
    pad_rows = ROUTE_ROWS - N_EXPERTS - N_GROUPS
    w = jnp.concatenate([w_re.T, w_rg.T, jnp.zeros((pad_rows, D), F32)], axis=0)
    whi = w.astype(BF16)
    wlo = (w - whi.astype(F32)).astype(BF16)
    w_split = jnp.concatenate([whi, wlo], axis=0)
    bias = jnp.concatenate([b_re, b_rg, jnp.zeros((pad_rows,), F32)]).reshape(ROUTE_ROWS, 1)
    return w_split, bias


def _router(logits_t):
    B, _, S = logits_t.shape
    T = B * S
    per_row = S // ROUTE_STEP
    tri = jnp.asarray(np.triu(np.ones((ROUTE_T, ROUTE_T), np.float32), 1), BF16)
    const = lambda shape: pl.BlockSpec(shape, lambda i: (0,) * len(shape))
    return pl.pallas_call(
        _router_kernel,
        grid=(T // ROUTE_STEP,),
        in_specs=[
            pl.BlockSpec((1, ROUTE_ROWS, ROUTE_STEP), lambda i: (i // per_row, 0, i % per_row)),
            const((ROUTE_T, ROUTE_T)),
        ],
        out_specs=[
            pl.BlockSpec((SUBLANES, ROUTE_STEP), lambda i: (0, i)),
            pl.BlockSpec((SUBLANES, ROUTE_STEP), lambda i: (0, i)),
            const((N_EXPERTS, LANES)),
        ],
        out_shape=[
            jax.ShapeDtypeStruct((SUBLANES, T), jnp.int32),
            jax.ShapeDtypeStruct((SUBLANES, T), F32),
            jax.ShapeDtypeStruct((N_EXPERTS, LANES), jnp.int32),
        ],
        scratch_shapes=[pltpu.VMEM((N_EXPERTS, 1), F32)],
        compiler_params=pltpu.CompilerParams(
            dimension_semantics=("arbitrary",), vmem_limit_bytes=VMEM_LIMIT),
        name="router",
    )(logits_t, tri)


def _sc_mesh():
    return plsc.VectorSubcoreMesh(core_axis_name="c", subcore_axis_name="s",
                                  num_cores=SC_CORES, num_subcores=SC_SUBCORES)


def _sc_worker_id():
    return lax.axis_index("s") * SC_CORES + lax.axis_index("c")


def _sc_scratch(n_win, width):
    return [
        pltpu.VMEM((n_win, SC_WIN), jnp.int32),
        pltpu.VMEM((n_win, SC_WIN), jnp.int32),
        pltpu.VMEM((2, SC_WIN, width), U32),
        pltpu.SemaphoreType.DMA((2,)),
        pltpu.SemaphoreType.DMA((2,)),
    ]


def _sc_dispatch(rows, idx1, idx2, n_rows):
    _, width = rows.shape
    _, n_win, _ = idx1.shape

    @functools.partial(
        pl.kernel, mesh=_sc_mesh(), out_type=jax.ShapeDtypeStruct((n_rows, width), rows.dtype),
        scratch_types=_sc_scratch(n_win, width), name="sc_dispatch")
    def run(rows_hbm, i1_hbm, i2_hbm, o_hbm, i1_v, i2_v, buf, rsem, wsem):
        wid = _sc_worker_id()
        base = wid * (n_win * SC_WIN)
        pltpu.sync_copy(i1_hbm.at[wid], i1_v)
        pltpu.sync_copy(i2_hbm.at[wid], i2_v)

        def read(j):
            return pltpu.async_copy(rows_hbm.at[pl.ds(base + j * SC_WIN, SC_WIN)], buf.at[j % 2], rsem.at[j % 2])

        reads = {0: read(0)}
        writes = {}
        for j in range(n_win):
            if j + 1 < n_win:
                for d in writes.pop(j - 1, ()):
                    d.wait()
                reads[j + 1] = read(j + 1)
            reads.pop(j).wait()
            writes[j] = (pltpu.async_copy(buf.at[j % 2], o_hbm.at[i1_v.at[j]], wsem.at[j % 2]),
                         pltpu.async_copy(buf.at[j % 2], o_hbm.at[i2_v.at[j]], wsem.at[j % 2]))
        for j in sorted(writes):
            for d in writes[j]:
                d.wait()

    return run(rows, idx1, idx2)


def _sc_gather_pair(table, idx1, idx2):
    _, width = table.shape
    _, n_win, _ = idx1.shape
    n_tok = SC_WORKERS * n_win * SC_WIN
    out_t = jax.ShapeDtypeStruct((n_tok, width), table.dtype)

    @functools.partial(
        pl.kernel, mesh=_sc_mesh(), out_type=(out_t, out_t),
        scratch_types=_sc_scratch(n_win, width), name="sc_combine_gather")
    def run(table_hbm, i1_hbm, i2_hbm, o1_hbm, o2_hbm, i1_v, i2_v, buf, gsem, wsem):
        wid = _sc_worker_id()
        base = wid * (n_win * SC_WIN)
        pltpu.sync_copy(i1_hbm.at[wid], i1_v)
        pltpu.sync_copy(i2_hbm.at[wid], i2_v)
        work = [(i1_v, o1_hbm, j) for j in range(n_win)] + [(i2_v, o2_hbm, j) for j in range(n_win)]

        def gather(t):
            iv, _, j = work[t]
            return pltpu.async_copy(table_hbm.at[iv.at[j]], buf.at[t % 2], gsem.at[t % 2])

        def put(t):
            _, oh, j = work[t]
            return pltpu.async_copy(buf.at[t % 2], oh.at[pl.ds(base + j * SC_WIN, SC_WIN)], wsem.at[t % 2])

        gathers = {0: gather(0)}
        puts = {}
        for t in range(len(work)):
            if t + 1 < len(work):
                if t - 1 in puts:
                    puts.pop(t - 1).wait()
                gathers[t + 1] = gather(t + 1)
            gathers.pop(t).wait()
            puts[t] = put(t)
        for t in sorted(puts):
            puts[t].wait()

    return run(table, idx1, idx2)


def _moe_kernel(layer, n_blocks, be_ref, valid_ref, x_ref, w1_hbm, w3_hbm, w2_hbm, o_ref,
                w1_st, w3_st, w2_st, w1_sc, w3_sc, w2_sc, sems, ordinal_sm):
    i = pl.program_id(0)
    expert = be_ref[i]
    new_expert = jnp.logical_or(i == 0, expert != be_ref[jnp.maximum(i - 1, 0)])

    def weight_copies(e, sl):
        return [pltpu.make_async_copy(hbm.at[layer, e], stage.at[sl], sems.at[k, sl])
                for k, (hbm, stage) in enumerate(((w1_hbm, w1_st), (w3_hbm, w3_st), (w2_hbm, w2_st)))]

    @pl.when(i == 0)
    def _():
        ordinal_sm[0] = -1
        for cp in weight_copies(expert, 0):
            cp.start()

    @pl.when(new_expert)
    def _():
        ordinal = ordinal_sm[0] + 1
        ordinal_sm[0] = ordinal
        slot = ordinal & 1
        for cp in weight_copies(expert, slot):
            cp.wait()
        w1_sc[...] = w1_st[slot].astype(BF16)
        w3_sc[...] = w3_st[slot].astype(BF16)
        w2_sc[...] = w2_st[slot].astype(BF16)
        nxt_pos = lax.while_loop(
            lambda p: jnp.logical_and(p < n_blocks, be_ref[jnp.minimum(p, n_blocks - 1)] == expert),
            lambda p: p + 1, i + 1)

        @pl.when(nxt_pos < n_blocks)
        def _():
            for cp in weight_copies(be_ref[jnp.minimum(nxt_pos, n_blocks - 1)], 1 - slot):
                cp.start()

    def up(rows):
        xa, xb = _unpack_bf16_pair(x_ref[rows, :])
        xa = xa.astype(BF16)
        xb = xb.astype(BF16)
        h1 = (jnp.dot(xa, w1_sc[:D_HALF], preferred_element_type=F32)
              + jnp.dot(xb, w1_sc[D_HALF:], preferred_element_type=F32))
        h3 = (jnp.dot(xa, w3_sc[:D_HALF], preferred_element_type=F32)
              + jnp.dot(xb, w3_sc[D_HALF:], preferred_element_type=F32))
        return h1, h3

    def down(rows, h1, h3):
        hdn = (jax.nn.silu(h1) * h3).astype(BF16)
        y = jnp.dot(hdn, w2_sc[...], preferred_element_type=F32)
        o_ref[rows, :] = _pack_row_halves(y)

    n_sub = MOE_BM // MOE_SUB
    subs = [slice(k * MOE_SUB, (k + 1) * MOE_SUB) for k in range(n_sub)]
    valid = valid_ref[i]
    chains = (valid + (MOE_SUB - 1)) // MOE_SUB

    for live in range(n_sub + 1):
        @pl.when(chains == live)
        def _(live=live):
            ups = {}
            if live:
                ups[0] = up(subs[0])
            for k in range(live):
                if k + 1 < live:
                    ups[k + 1] = up(subs[k + 1])
                down(subs[k], *ups.pop(k))
            if live < n_sub:
                o_ref[live * MOE_SUB:, :] = jnp.zeros((MOE_BM - live * MOE_SUB, D_HALF), o_ref.dtype)


def _moe_blocks(xbuf, block_e, valid, w1, w3, w2, layer):
    n_rows, _ = xbuf.shape
    D = D_MODEL
    n_blocks = n_rows // MOE_BM
    rows = lambda i, be, nu: (i, 0)
    grid_spec = pltpu.PrefetchScalarGridSpec(
        num_scalar_prefetch=2,
        grid=(n_blocks,),
        in_specs=[
            pl.BlockSpec((MOE_BM, D_HALF), rows),
            pl.BlockSpec(memory_space=pl.ANY),
            pl.BlockSpec(memory_space=pl.ANY),
            pl.BlockSpec(memory_space=pl.ANY),
        ],
        out_specs=pl.BlockSpec((MOE_BM, D_HALF), rows),
        scratch_shapes=[
            pltpu.VMEM((2, D, D_EXPERT), F32),
            pltpu.VMEM((2, D, D_EXPERT), F32),
            pltpu.VMEM((2, D_EXPERT, D), F32),
            pltpu.VMEM((D, D_EXPERT), BF16),
            pltpu.VMEM((D, D_EXPERT), BF16),
            pltpu.VMEM((D_EXPERT, D), BF16),
            pltpu.SemaphoreType.DMA((3, 2)),
            pltpu.SMEM((1,), jnp.int32),
        ],
    )
    return pl.pallas_call(
        functools.partial(_moe_kernel, layer, n_blocks),
        grid_spec=grid_spec,
        out_shape=jax.ShapeDtypeStruct((n_rows, D_HALF), U32),
        compiler_params=pltpu.CompilerParams(
            dimension_semantics=("arbitrary",), vmem_limit_bytes=VMEM_LIMIT),
        name="moe_experts",
    )(block_e, valid, xbuf, w1, w3, w2)


def _moe_combine_norm(x, y1, y2, gate_rows, g, b):
    n = x.shape[0]
    pad = jnp.zeros((LANES - SUBLANES, n), F32)
    gates = jnp.concatenate([gate_rows, pad], axis=0).T
    g1 = gates[:, 0:1]
    g2 = gates[:, 1:2]
    a1, b1 = _unpack_bf16_pair(y1)
    a2, b2 = _unpack_bf16_pair(y2)
    f = jnp.concatenate([g1 * a1 + g2 * a2, g1 * b1 + g2 * b2], axis=1)
    return _layer_norm_rows(ALPHA * x + f, g, b)


def _combine_kernel(x_ref, y1_ref, y2_ref, gates_ref, g_ref, b_ref, o_ref):
    o_ref[...] = _moe_combine_norm(x_ref[...], y1_ref[...], y2_ref[...], gates_ref[...], g_ref[...], b_ref[...])


def _combine_part_kernel(x_ref, y1_ref, y2_ref, gates_ref, g_ref, b_ref, prev_ref, o_ref):
    del prev_ref
    _combine_kernel(x_ref, y1_ref, y2_ref, gates_ref, g_ref, b_ref, o_ref)


def _combine_ln(xf, y_parts, gates, ln_g, ln_b):
    T, D = xf.shape
    n_parts = len(y_parts)
    steps = T // (COMB_T * n_parts)
    const = lambda shape: pl.BlockSpec(shape, lambda i: (0,) * len(shape))
    out = None
    for part, (y1, y2) in enumerate(y_parts):
        off = part * steps
        glob = lambda w, off=off: pl.BlockSpec((COMB_T, w), lambda i: (i + off, 0))
        local = lambda w: pl.BlockSpec((COMB_T, w), lambda i: (i, 0))
        in_specs = [glob(D), local(D_HALF), local(D_HALF),
                    pl.BlockSpec((SUBLANES, COMB_T), lambda i, off=off: (0, i + off)), const((1, D)), const((1, D))]
        args = [xf, y1, y2, gates, ln_g.reshape(1, D), ln_b.reshape(1, D)]
        if out is not None:
            in_specs.append(pl.BlockSpec(memory_space=pl.ANY))
            args.append(out)
        out = pl.pallas_call(
            _combine_kernel if out is None else _combine_part_kernel,
            grid=(steps,),
            in_specs=in_specs,
            out_specs=glob(D),
            out_shape=jax.ShapeDtypeStruct((T, D), F32),
            input_output_aliases={} if out is None else {len(args) - 1: 0},
            compiler_params=pltpu.CompilerParams(
                dimension_semantics=("arbitrary",), vmem_limit_bytes=VMEM_LIMIT),
            name="moe_combine_ln",
        )(*args)
    return out


def _hier_moe(xpk, logits_t, w1, w3, w2, layer, gather_parts):
    B, S, _ = xpk.shape
    T = B * S
    idx, gates, cnt = _router(logits_t)
    counts = cnt[:, 0]
    padded = ((counts + MOE_BM - 1) // MOE_BM) * MOE_BM
    pends = jnp.cumsum(padded)
    pstarts = pends - padded
    experts = jnp.arange(N_EXPERTS, dtype=jnp.int32)[:, None]

    def dest(e_row, rank_row):
        return jnp.sum(jnp.where(e_row[None, :] == experts, pstarts[:, None], 0), axis=0) + rank_row

    n_win = T // (SC_WORKERS * SC_WIN)
    dest1 = dest(idx[0], idx[2]).reshape(SC_WORKERS, n_win, SC_WIN)
    dest2 = dest(idx[1], idx[3]).reshape(SC_WORKERS, n_win, SC_WIN)
    n_blocks = -(-(T * TOP_K + N_EXPERTS * (MOE_BM - 1)) // MOE_BM)
    n_rows = n_blocks * MOE_BM
    block_start = jnp.arange(n_blocks, dtype=jnp.int32) * MOE_BM
    block_e = jnp.minimum(jnp.sum(block_start[:, None] >= pends[None, :], axis=1), N_EXPERTS - 1).astype(jnp.int32)
    of_block = block_e[:, None] == experts[:, 0][None, :]
    pick = lambda v: jnp.sum(jnp.where(of_block, v[None, :], 0), axis=1)
    valid = jnp.clip(pick(counts) - (block_start - pick(pstarts)), 0, MOE_BM).astype(jnp.int32)
    xbuf = _sc_dispatch(xpk.reshape(T, D_HALF), dest1, dest2, n_rows)
    ybuf = _moe_blocks(xbuf, block_e, valid, w1, w3, w2, layer)
    part_shape = (gather_parts, SC_WORKERS, n_win // gather_parts, SC_WIN)
    d1, d2 = dest1.reshape(part_shape), dest2.reshape(part_shape)
    y_parts = [_sc_gather_pair(ybuf, d1[p], d2[p]) for p in range(gather_parts)]
    return y_parts, gates


def _att_head_order():
    order = []
    for p in range(N_HEADS // 2):
        jj, m = divmod(p, 4)
        order += [8 * jj + m, 8 * jj + 4 + m]
    return order


ATT_HEAD_ORDER = _att_head_order()


def _attn_kernel(xprev_ref, y1_ref, y2_ref, gates_a_ref, gates_b_ref, g_prev_ref, b_prev_ref, wqkv_ref, bias_ref,
                 sink_ref, wo_ref, g_ref, b_ref, wrt_ref, brt_ref, o_ref, opk_ref, lg_ref,
                 kv_ext, o_sc, s_sc0, s_sc1, p_sc0, p_sc1):
    s = pl.program_id(1)
    tq = ATT_TQ
    s_bufs = (s_sc0, s_sc1)
    p_bufs = (p_sc0, p_sc1)
    gate_refs = (gates_a_ref, gates_b_ref)
    assert ATT_NSEQ == len(gate_refs)

    @pl.when(s == 0)
    def _():
        kv_ext[:, 0:WINDOW, :] = jnp.zeros((ATT_NSEQ, WINDOW, 2 * KV_DIM), BF16)

    def layer_input(sq):
        return _moe_combine_norm(xprev_ref[sq], y1_ref[sq], y2_ref[sq], gate_refs[sq][...],
                                 g_prev_ref[...], b_prev_ref[...])

    def project_qkv(sq, x):
        qkv = jnp.dot(x.astype(BF16), wqkv_ref[...], preferred_element_type=F32)
        kv_ext[sq, WINDOW:WINDOW + tq, :] = qkv[:, Q_DIM:].astype(BF16)
        return (qkv[:, :Q_DIM] * (HEAD_DIM ** -0.5 * LOG2E)).astype(BF16)

    lane = lax.broadcasted_iota(jnp.int32, (WINDOW, LANES), 1)
    low = lane < HEAD_DIM
    sub = lax.broadcasted_iota(jnp.int32, (LANES, WINDOW), 0)
    top = sub < HEAD_DIM
    first = jnp.where(s == 0, 1, 0)
    nt = (((1,), (1,)), ((), ()))
    zero = jnp.zeros((), BF16)

    tiles = [(sq, n, j) for n in range(ATT_NB) for j in range(2) for sq in range(ATT_NSEQ)]

    def scores(t, qs_all):
        sq, n, j = tiles[t]
        q = qs_all[sq]
        r0 = n * WINDOW
        k_tile = kv_ext[sq, r0:r0 + 2 * WINDOW, j * LANES:(j + 1) * LANES]
        parts = []
        for m in range(4):
            p = 4 * j + m
            qt = q[r0:r0 + WINDOW, p * LANES:(p + 1) * LANES]
            parts.append(jnp.where(low, qt, zero))
            parts.append(jnp.where(low, zero, qt))
        qs = jnp.concatenate(parts, axis=0)
        bias_sel = first if n == 0 else 0
        s_bufs[t % 2][...] = (lax.dot_general(k_tile, qs, nt, preferred_element_type=F32)
                              + bias_ref[bias_sel, j])

    def softmax_pv(t):
        sq, n, j = tiles[t]
        r0 = n * WINDOW
        s_sc = s_bufs[t % 2]
        p_sc = p_bufs[t % 2]
        inv_l = []
        for h in range(8):
            hc = slice(h * WINDOW, (h + 1) * WINDOW)
            sink = sink_ref[8 * j + h] * LOG2E
            mx = jnp.maximum(jnp.max(s_sc[:, hc], axis=0, keepdims=True), sink)
            pr = jnp.exp2(s_sc[:, hc] - mx)
            p_sc[:, hc] = pr.astype(BF16)
            inv_l.append(1.0 / (jnp.sum(pr, axis=0, keepdims=True) + jnp.exp2(sink - mx)))
        v_tile = kv_ext[sq, r0:r0 + 2 * WINDOW, KV_DIM + j * LANES:KV_DIM + (j + 1) * LANES]
        tn = (((0,), (0,)), ((), ()))
        ov = lax.dot_general(v_tile, p_sc[...], tn, preferred_element_type=F32)
        for m in range(4):
            p = 4 * j + m
            o_even = ov[:, (2 * m) * WINDOW:(2 * m + 1) * WINDOW] * inv_l[2 * m]
            o_odd = ov[:, (2 * m + 1) * WINDOW:(2 * m + 2) * WINDOW] * inv_l[2 * m + 1]
            o_sc[sq, p * LANES:(p + 1) * LANES, r0:r0 + WINDOW] = jnp.where(top, o_even, o_odd).astype(BF16)

    def project_out(sq):
        tn = (((0,), (0,)), ((), ()))
        return lax.dot_general(o_sc[sq], wo_ref[...], tn, preferred_element_type=F32)

    def finish(sq, x, out):
        xn = _layer_norm_rows(ALPHA * x + out, g_ref[...], b_ref[...])
        o_ref[sq] = xn
        opk_ref[sq] = _pack_row_halves(xn)
        lg_ref[sq] = _router_logits_t(xn, wrt_ref[...], brt_ref[...])

    xs = [layer_input(0), layer_input(1)]
    qs_all = [project_qkv(0, xs[0]), project_qkv(1, xs[1])]
    scores(0, qs_all)
    for t in range(len(tiles)):
        if t + 1 < len(tiles):
            scores(t + 1, qs_all)
        softmax_pv(t)
        if tiles[t][0] == 0 and all(sq != 0 for sq, _, _ in tiles[t + 1:]):
            out_first = project_out(0)
    kv_ext[:, 0:WINDOW, :] = kv_ext[:, tq:tq + WINDOW, :]
    out_second = project_out(1)
    finish(0, xs[0], out_first)
    finish(1, xs[1], out_second)


def _attn_bias():
    qi = np.arange(WINDOW)[:, None]
    sj = np.arange(2 * WINDOW)[None, :]
    dist = qi - sj + WINDOW
    valid = (dist >= 0) & (dist < WINDOW)
    slopes = 2.0 ** (-8.0 * np.arange(1, N_HEADS + 1, dtype=np.float32) / N_HEADS)
    slopes = slopes.astype(np.float32)[ATT_HEAD_ORDER]
    sb = -(slopes[:, None, None] * dist.astype(np.float32)[None])
    later = np.where(valid[None], sb, -np.inf)
    first = np.where((valid & (sj >= WINDOW))[None], sb, -np.inf)
    bias = np.stack([later, first]).astype(np.float32) * np.float32(LOG2E)
    bias = bias.reshape(2, 2, 8, WINDOW, 2 * WINDOW).transpose(0, 1, 4, 2, 3).reshape(2, 2, 2 * WINDOW, 8 * WINDOW)
    return jnp.asarray(np.ascontiguousarray(bias))


ATT_N_INPUTS = 15


def _attn_part_kernel(*refs):
    _attn_kernel(*refs[:ATT_N_INPUTS], *refs[ATT_N_INPUTS + 3:])


def _attn_layer(x_prev, y_parts, gates, g_prev, b_prev, w_qkv, sinks, w_o, ln_g, ln_b, router_w, router_b):
    B, S, D = x_prev.shape
    steps = S // ATT_TQ
    assert ATT_HEAD_ORDER == list(np.arange(N_HEADS).reshape(2, 2, 4).transpose(0, 2, 1).reshape(-1))
    wq = w_qkv[:, :Q_DIM].reshape(D, 2, 2, 4, HEAD_DIM).transpose(0, 1, 3, 2, 4).reshape(D, Q_DIM)
    wqkv = jnp.concatenate([wq, w_qkv[:, Q_DIM:]], axis=1).astype(BF16)
    wo_t = w_o.reshape(2, 2, 4, HEAD_DIM, D).transpose(0, 2, 1, 3, 4).reshape(Q_DIM, D).astype(BF16)
    sink = sinks.reshape(2, 2, 4).transpose(0, 2, 1).reshape(N_HEADS, 1, 1)
    bias = _attn_bias()
    const = lambda shape: pl.BlockSpec(shape, lambda b, s: (0,) * len(shape))
    n_parts = len(y_parts)
    rows_per_part = B // n_parts
    pairs = rows_per_part // ATT_NSEQ
    outs = None
    for part, (y1, y2) in enumerate(y_parts):
        off = part * pairs
        glob = lambda w, off=off: pl.BlockSpec((ATT_NSEQ, ATT_TQ, w), lambda b, s: (b + off, s, 0))
        local = lambda w: pl.BlockSpec((ATT_NSEQ, ATT_TQ, w), lambda b, s: (b, s, 0))
        gate_rows = lambda sq, off=off: pl.BlockSpec(
            (SUBLANES, ATT_TQ), lambda b, s: (0, (ATT_NSEQ * (b + off) + sq) * steps + s))
        in_specs = [
            glob(D),
            local(D_HALF),
            local(D_HALF),
            gate_rows(0),
            gate_rows(1),
            const((1, D)),
            const((1, D)),
            const((D, Q_DIM + 2 * KV_DIM)),
            const((2, 2, 2 * WINDOW, 8 * WINDOW)),
            const((N_HEADS, 1, 1)),
            const((Q_DIM, D)),
            const((1, D)),
            const((1, D)),
            const((2 * ROUTE_ROWS, D)),
            const((ROUTE_ROWS, 1)),
        ]
        args = [x_prev, y1.reshape(rows_per_part, S, D_HALF), y2.reshape(rows_per_part, S, D_HALF), gates, gates,
                g_prev.reshape(1, D), b_prev.reshape(1, D), wqkv, bias, sink, wo_t,
                ln_g.reshape(1, D), ln_b.reshape(1, D), router_w, router_b]
        aliases = {}
        body = _attn_kernel
        if outs is not None:
            aliases = {len(args) + k: k for k in range(3)}
            in_specs += [pl.BlockSpec(memory_space=pl.ANY)] * 3
            args += list(outs)
            body = _attn_part_kernel
        outs = pl.pallas_call(
            body,
            grid=(pairs, steps),
            in_specs=in_specs,
            out_specs=[glob(D), glob(D_HALF),
                       pl.BlockSpec((ATT_NSEQ, ROUTE_ROWS, ATT_TQ), lambda b, s, off=off: (b + off, 0, s))],
            out_shape=[jax.ShapeDtypeStruct((B, S, D), F32), jax.ShapeDtypeStruct((B, S, D_HALF), U32),
                       jax.ShapeDtypeStruct((B, ROUTE_ROWS, S), F32)],
            input_output_aliases=aliases,
            scratch_shapes=[
                pltpu.VMEM((ATT_NSEQ, ATT_TQ + WINDOW, 2 * KV_DIM), BF16),
                pltpu.VMEM((ATT_NSEQ, Q_DIM, ATT_TQ), BF16),
                pltpu.VMEM((2 * WINDOW, 8 * WINDOW), F32),
                pltpu.VMEM((2 * WINDOW, 8 * WINDOW), F32),
                pltpu.VMEM((2 * WINDOW, 8 * WINDOW), BF16),
                pltpu.VMEM((2 * WINDOW, 8 * WINDOW), BF16),
            ],
            compiler_params=pltpu.CompilerParams(
                dimension_semantics=("arbitrary", "arbitrary"), vmem_limit_bytes=VMEM_LIMIT),
            name="swa_attn_ln",
        )(*args)
    return outs


def kernel(x, rec_w_in, rec_conv_w, rec_conv_b, rec_w_r, rec_b_r, rec_w_i, rec_b_i, rec_lambda, rec_w_out,
           att_w_qkv, att_sinks, att_w_o, moe_w_group, moe_b_group, moe_w_expert, moe_b_expert,
           moe_w1, moe_w3, moe_w2, ln_g, ln_b):
    assert DEPTH == 2
    B, S, D = x.shape

    router = [_router_weights(moe_w_group[layer], moe_b_group[layer], moe_w_expert[layer], moe_b_expert[layer])
              for layer in range(DEPTH)]

    x1, x1_pk, logits1 = _rglru_layer(x, rec_w_in[0], rec_conv_w[0], rec_conv_b[0], rec_w_r[0], rec_b_r[0],
                                      rec_w_i[0], rec_b_i[0], rec_lambda[0], rec_w_out[0], ln_g[0, 0], ln_b[0, 0],
                                      *router[0])
    y_parts, gates = _hier_moe(x1_pk, logits1, moe_w1, moe_w3, moe_w2, 0, gather_parts=ATT_PARTS)
    x3, x3_pk, logits3 = _attn_layer(x1, y_parts, gates, ln_g[0, 1], ln_b[0, 1], att_w_qkv[0], att_sinks[0],
                                     att_w_o[0], ln_g[1, 0], ln_b[1, 0], *router[1])
    y_parts, gates = _hier_moe(x3_pk, logits3, moe_w1, moe_w3, moe_w2, 1, gather_parts=1)
    out = _combine_ln(x3.reshape(B * S, D), y_parts, gates, ln_g[1, 1], ln_b[1, 1])
    return out.reshape(B, S, D)
```

```python
import functools

import jax
import jax.numpy as jnp
import numpy as np
from jax import lax
from jax.experimental import pallas as pl
from jax.experimental.pallas import tpu as pltpu
from jax.experimental.pallas import tpu_sc as plsc

F32 = jnp.float32
BF16 = jnp.bfloat16
U32 = jnp.uint32

D_MODEL = 1024
DEPTH = 2
D_RNN = 1280
LRU_BLOCKS = 16
LRU_BLOCK_W = D_RNN // LRU_BLOCKS
CONV_W = 4
LRU_C = 8.0
N_HEADS = 16
N_KV_HEADS = 4
HEAD_DIM = 64
WINDOW = 128
Q_DIM = N_HEADS * HEAD_DIM
KV_DIM = N_KV_HEADS * HEAD_DIM
N_GROUPS = 4
EXPERTS_PER_GROUP = 8
N_EXPERTS = N_GROUPS * EXPERTS_PER_GROUP
TOP_K = 2
D_EXPERT = 512
ALPHA = (2 * DEPTH) ** 0.25
LN_EPS = 1e-5
LOG2E = 1.4426950408889634

LANES = 128
SUBLANES = 8
VMEM_LIMIT = 56 * 1024 * 1024

REC_TS = 256
REC_GROUPS = REC_TS // SUBLANES
REC_NSEQ = 2
GATE_TILE = 256
GATE_WIN = 512
GATE_WIN_STARTS = (0, 128, 384, 640, 768)
N_GATE_TILES = D_RNN // GATE_TILE

ROUTE_T = 512
ROUTE_STEP = 2048
ROUTE_ROWS = 40

MOE_BM = 1024
MOE_SUB = 256

ATT_TQ = 256
ATT_NB = ATT_TQ // WINDOW
ATT_NSEQ = 2
ATT_PARTS = 4

COMB_T = 1024
COMB_SLOTS = 3

D_HALF = D_MODEL // 2

SC_CORES = 2
SC_SUBCORES = 16
SC_WORKERS = SC_CORES * SC_SUBCORES
SC_WIN = 64


def _layer_norm_rows(z, g, b):
    mu = jnp.mean(z, axis=-1, keepdims=True)
    zc = z - mu
    var = jnp.mean(zc * zc, axis=-1, keepdims=True)
    return zc * lax.rsqrt(var + LN_EPS) * g + b


def _pack_bf16_pair(a, b):
    ua = lax.bitcast_convert_type(a.astype(BF16).astype(F32), U32)
    ub = lax.bitcast_convert_type(b.astype(BF16).astype(F32), U32)
    return (ua >> 16) | (ub & jnp.uint32(0xFFFF0000))


def _unpack_bf16_pair(w):
    a = lax.bitcast_convert_type(w << 16, F32)
    b = lax.bitcast_convert_type(w & jnp.uint32(0xFFFF0000), F32)
    return a, b


def _pack_row_halves(x):
    return _pack_bf16_pair(x[:, :D_HALF], x[:, D_HALF:])


def _router_logits_t(x, w_split, bias):
    xhi = x.astype(BF16)
    xlo = (x - xhi.astype(F32)).astype(BF16)
    nt = (((1,), (1,)), ((), ()))
    both = lax.dot_general(w_split, xhi, nt, preferred_element_type=F32)
    low = lax.dot_general(w_split[:ROUTE_ROWS], xlo, nt, preferred_element_type=F32)
    return both[:ROUTE_ROWS] + both[ROUTE_ROWS:] + low + bias


def _rglru_kernel(x_ref, perm_ref, perm_t_ref, w_in_ref, convw_ref, convb_ref, wg_ref, br_ref, bi_ref, lam_ref,
                  w_out_ref, g_ref, b_ref, wrt_ref, brt_ref, o_ref, opk_ref, lg_ref,
                  xr_ext, tail_sc, a_sc, u_sc, h_carry):
    s = pl.program_id(1)
    ts = REC_TS
    halo = (CONV_W - 1) * SUBLANES

    @pl.when(s == 0)
    def _():
        tail_sc[...] = jnp.zeros((REC_NSEQ, halo, D_RNN), F32)
        h_carry[...] = jnp.zeros((REC_NSEQ, 1, D_RNN), F32)

    row = lax.broadcasted_iota(jnp.int32, (SUBLANES, D_RNN), 0)
    nlam = -lam_ref[...]
    sp = jnp.maximum(nlam, 0.0) + jnp.log1p(jnp.exp(-jnp.abs(nlam)))
    log2a_scale = (-LRU_C * LOG2E) * sp

    def project(q):
        xp = jnp.dot(perm_ref[...], x_ref[q].astype(BF16), preferred_element_type=F32).astype(BF16)
        proj = jnp.dot(xp, w_in_ref[...], preferred_element_type=F32)
        return proj[:, :D_RNN], proj[:, D_RNN:]

    def conv_gates(q, xr):
        for k in range(CONV_W - 1):
            r0 = ts - halo + k * SUBLANES
            cur = xr[r0:r0 + SUBLANES, :]
            prev = tail_sc[q, k * SUBLANES:(k + 1) * SUBLANES, :]
            xr_ext[q, k * SUBLANES:(k + 1) * SUBLANES, :] = jnp.where(
                row == 0, pltpu.roll(prev, 1, axis=0), pltpu.roll(cur, 1, axis=0))
        tail_sc[q] = xr[ts - halo:, :]
        xr_ext[q, halo:halo + ts, :] = xr
        xc = convb_ref[...] + convw_ref[CONV_W - 1:CONV_W, :] * xr
        for k in range(CONV_W - 1):
            xc = xc + convw_ref[k:k + 1, :] * xr_ext[q, k * SUBLANES:k * SUBLANES + ts, :]
        xcb = xc.astype(BF16)
        pres = [jnp.dot(xcb[:, GATE_WIN_STARTS[j]:GATE_WIN_STARTS[j] + GATE_WIN], wg_ref[j],
                        preferred_element_type=F32) for j in range(N_GATE_TILES)]
        for j, pre in enumerate(pres):
            cs = j * GATE_TILE
            r = jax.nn.sigmoid(pre[:, :GATE_TILE] + br_ref[:, cs:cs + GATE_TILE])
            i = jax.nn.sigmoid(pre[:, GATE_TILE:] + bi_ref[:, cs:cs + GATE_TILE])
            a = jnp.exp2(r * log2a_scale[:, cs:cs + GATE_TILE])
            s1 = 1.0 - a * a
            mult = jnp.where(s1 > 0.0, s1 * lax.rsqrt(s1), 0.0)
            u = mult * (i * xc[:, cs:cs + GATE_TILE])
            a_sc[q, :, cs:cs + GATE_TILE] = a
            u_sc[q, :, cs:cs + GATE_TILE] = u

    def segment_scan(q):
        h = jnp.zeros((SUBLANES, D_RNN), F32)
        prod = jnp.ones((SUBLANES, D_RNN), F32)
        for gidx in range(REC_GROUPS):
            rows = slice(gidx * SUBLANES, (gidx + 1) * SUBLANES)
            a8 = a_sc[q, rows, :]
            h = a8 * h + u_sc[q, rows, :]
            prod = a8 * prod
            u_sc[q, rows, :] = h
            a_sc[q, rows, :] = prod
        return h, prod

    def recur_out(q, gate, seg):
        seg_h, seg_a = seg
        for d in (1, 2, 4):
            keep = row >= d
            a_sh = jnp.where(keep, pltpu.roll(seg_a, d, axis=0), 1.0)
            h_sh = jnp.where(keep, pltpu.roll(seg_h, d, axis=0), 0.0)
            seg_h = seg_a * h_sh + seg_h
            seg_a = seg_a * a_sh
        h_in = h_carry[q]
        after = seg_a * h_in + seg_h
        enter = jnp.where(row == 0, h_in, pltpu.roll(after, 1, axis=0))
        h_carry[q] = after[SUBLANES - 1:SUBLANES, :]
        hs = (u_sc[q].reshape(REC_GROUPS, SUBLANES, D_RNN)
              + a_sc[q].reshape(REC_GROUPS, SUBLANES, D_RNN) * enter[None]).reshape(ts, D_RNN)
        y = hs * jax.nn.gelu(gate)
        y_t = jnp.dot(perm_t_ref[...], y.astype(BF16), preferred_element_type=F32).astype(BF16)
        return jnp.dot(y_t, w_out_ref[...], preferred_element_type=F32)

    def finish(q, out):
        z = ALPHA * x_ref[q] + out
        xn = _layer_norm_rows(z, g_ref[...], b_ref[...])
        o_ref[q] = xn
        opk_ref[q] = _pack_row_halves(xn)
        lg_ref[q] = _router_logits_t(xn, wrt_ref[...], brt_ref[...])

    assert REC_NSEQ == 2
    gate_a, xr_a = project(0)
    conv_gates(0, xr_a)
    gate_b, xr_b = project(1)
    conv_gates(1, xr_b)
    seg_a = segment_scan(0)
    seg_b = segment_scan(1)
    out_a = recur_out(0, gate_a, seg_a)
    out_b = recur_out(1, gate_b, seg_b)
    finish(0, out_a)
    finish(1, out_b)


def _band_gate_weights(w_r, w_i):
    spread = jnp.asarray(np.tile(np.eye(LRU_BLOCK_W, dtype=np.float32), (1, LRU_BLOCKS)), BF16)
    blk = np.arange(D_RNN) // LRU_BLOCK_W
    on_diag = jnp.asarray(blk[:, None] == blk[None, :])

    def dense(w):
        rows = w.reshape(D_RNN, LRU_BLOCK_W).astype(BF16)
        return jnp.where(on_diag, jnp.dot(rows, spread, preferred_element_type=F32), 0.0)

    wr, wi = dense(w_r), dense(w_i)
    tiles = []
    for j in range(N_GATE_TILES):
        ws = GATE_WIN_STARTS[j]
        cs = j * GATE_TILE
        lo_blk = cs // LRU_BLOCK_W
        hi_blk = (cs + GATE_TILE - 1) // LRU_BLOCK_W
        assert ws <= lo_blk * LRU_BLOCK_W and (hi_blk + 1) * LRU_BLOCK_W <= ws + GATE_WIN
        tiles.append(jnp.concatenate([wr[ws:ws + GATE_WIN, cs:cs + GATE_TILE],
                                      wi[ws:ws + GATE_WIN, cs:cs + GATE_TILE]], axis=1))
    return jnp.stack(tiles).astype(BF16)


def _rglru_layer(x, w_in, conv_w, conv_b, w_r, b_r, w_i, b_i, lam, w_out, ln_g, ln_b, router_w, router_b):
    B, S, D = x.shape
    wg = _band_gate_weights(w_r, w_i)
    rho = np.arange(REC_TS)
    perm_np = np.zeros((REC_TS, REC_TS), np.float32)
    perm_np[rho, (rho % SUBLANES) * REC_GROUPS + rho // SUBLANES] = 1.0
    perm = jnp.asarray(perm_np, BF16)
    perm_t = jnp.asarray(perm_np.T, BF16)
    row = lambda v: v.reshape(1, -1)
    const = lambda shape: pl.BlockSpec(shape, lambda b, s: (0,) * len(shape))
    tile = lambda w: pl.BlockSpec((REC_NSEQ, REC_TS, w), lambda b, s: (b, s, 0))
    halo = (CONV_W - 1) * SUBLANES
    return pl.pallas_call(
        _rglru_kernel,
        grid=(B // REC_NSEQ, S // REC_TS),
        in_specs=[
            tile(D),
            const((REC_TS, REC_TS)),
            const((REC_TS, REC_TS)),
            const((D, 2 * D_RNN)),
            const((CONV_W, D_RNN)),
            const((1, D_RNN)),
            const((N_GATE_TILES, GATE_WIN, 2 * GATE_TILE)),
            const((1, D_RNN)),
            const((1, D_RNN)),
            const((1, D_RNN)),
            const((D_RNN, D)),
            const((1, D)),
            const((1, D)),
            const((2 * ROUTE_ROWS, D)),
            const((ROUTE_ROWS, 1)),
        ],
        out_specs=[tile(D), tile(D_HALF),
                   pl.BlockSpec((REC_NSEQ, ROUTE_ROWS, REC_TS), lambda b, s: (b, 0, s))],
        out_shape=[jax.ShapeDtypeStruct((B, S, D), F32), jax.ShapeDtypeStruct((B, S, D_HALF), U32),
                   jax.ShapeDtypeStruct((B, ROUTE_ROWS, S), F32)],
        scratch_shapes=[
            pltpu.VMEM((REC_NSEQ, halo + REC_TS, D_RNN), F32),
            pltpu.VMEM((REC_NSEQ, halo, D_RNN), F32),
            pltpu.VMEM((REC_NSEQ, REC_TS, D_RNN), F32),
            pltpu.VMEM((REC_NSEQ, REC_TS, D_RNN), F32),
            pltpu.VMEM((REC_NSEQ, 1, D_RNN), F32),
        ],
        compiler_params=pltpu.CompilerParams(
            dimension_semantics=("arbitrary", "arbitrary"), vmem_limit_bytes=VMEM_LIMIT),
        name="rglru_ln",
    )(x, perm, perm_t, w_in.astype(BF16), conv_w, row(conv_b), wg, row(b_r), row(b_i), row(lam), w_out.astype(BF16),
      row(ln_g), row(ln_b), router_w, router_b)


def _router_kernel(logits_ref, tri_ref, idx_ref, gate_ref, cnt_ref, base_sc):
    step = pl.program_id(0)
    tr = ROUTE_T

    @pl.when(step == 0)
    def _():
        base_sc[...] = jnp.zeros((N_EXPERTS, 1), F32)

    row8 = lax.broadcasted_iota(jnp.int32, (SUBLANES, tr), 0).astype(F32)
    rowe = lax.broadcasted_iota(jnp.int32, (N_EXPERTS, tr), 0).astype(F32)
    neg_inf = -jnp.inf
    for sub in range(ROUTE_STEP // ROUTE_T):
        cols = slice(sub * tr, (sub + 1) * tr)
        _route_tile(logits_ref[0, :, cols], row8, rowe, neg_inf, tri_ref, idx_ref, gate_ref, base_sc, cols)
    cnt_ref[...] = jnp.broadcast_to(base_sc[...], (N_EXPERTS, LANES)).astype(jnp.int32)


def _route_tile(logits, row8, rowe, neg_inf, tri_ref, idx_ref, gate_ref, base_sc, cols):
    tr = ROUTE_T
    g = jnp.where(row8 < N_GROUPS, logits[N_EXPERTS:N_EXPERTS + SUBLANES, :], neg_inf)
    gmax = jnp.max(g, axis=0, keepdims=True)
    gidx = jnp.min(jnp.where(g == gmax, row8, SUBLANES), axis=0, keepdims=True)
    g_gate = 1.0 / jnp.sum(jnp.exp(g - gmax), axis=0, keepdims=True)

    esel = logits[0:EXPERTS_PER_GROUP, :]
    for grp in range(1, N_GROUPS):
        esel = jnp.where(gidx == grp, logits[grp * EXPERTS_PER_GROUP:(grp + 1) * EXPERTS_PER_GROUP, :], esel)
    v1 = jnp.max(esel, axis=0, keepdims=True)
    i1 = jnp.min(jnp.where(esel == v1, row8, SUBLANES), axis=0, keepdims=True)
    esel2 = jnp.where(row8 == i1, neg_inf, esel)
    v2 = jnp.max(esel2, axis=0, keepdims=True)
    i2 = jnp.min(jnp.where(esel2 == v2, row8, SUBLANES), axis=0, keepdims=True)
    e21 = jnp.exp(v2 - v1)
    inv = 1.0 / (1.0 + e21)
    gate1 = inv * g_gate
    gate2 = e21 * inv * g_gate
    e1 = gidx * EXPERTS_PER_GROUP + i1
    e2 = gidx * EXPERTS_PER_GROUP + i2

    hit1 = rowe == e1
    hit2 = rowe == e2
    member = jnp.where(hit1, 1.0, jnp.where(hit2, 1.0, 0.0))
    before = jnp.dot(member.astype(BF16), tri_ref[...], preferred_element_type=F32) + base_sc[...]
    rank1 = jnp.sum(jnp.where(hit1, before, 0.0), axis=0, keepdims=True)
    rank2 = jnp.sum(jnp.where(hit2, before, 0.0), axis=0, keepdims=True)
    base_sc[...] = base_sc[...] + jnp.sum(member, axis=1, keepdims=True)

    zi = jnp.zeros((1, tr), jnp.int32)
    idx_ref[:, cols] = jnp.concatenate(
        [e1.astype(jnp.int32), e2.astype(jnp.int32), rank1.astype(jnp.int32), rank2.astype(jnp.int32),
         zi, zi, zi, zi], axis=0)
    zf = jnp.zeros((1, tr), F32)
    gate_ref[:, cols] = jnp.concatenate([gate1, gate2, zf, zf, zf, zf, zf, zf], axis=0)


def _router_weights(w_rg, b_rg, w_re, b_re):
    D = w_rg.shape[0]
    pad_rows = ROUTE_ROWS - N_EXPERTS - N_GROUPS
    w = jnp.concatenate([w_re.T, w_rg.T, jnp.zeros((pad_rows, D), F32)], axis=0)
    whi = w.astype(BF16)
    wlo = (w - whi.astype(F32)).astype(BF16)
    w_split = jnp.concatenate([whi, wlo], axis=0)
    bias = jnp.concatenate([b_re, b_rg, jnp.zeros((pad_rows,), F32)]).reshape(ROUTE_ROWS, 1)
    return w_split, bias


def _router(logits_t):
    B, _, S = logits_t.shape
    T = B * S
    per_row = S // ROUTE_STEP
    tri = jnp.asarray(np.triu(np.ones((ROUTE_T, ROUTE_T), np.float32), 1), BF16)
    const = lambda shape: pl.BlockSpec(shape, lambda i: (0,) * len(shape))
    return pl.pallas_call(
        _router_kernel,
        grid=(T // ROUTE_STEP,),
        in_specs=[
            pl.BlockSpec((1, ROUTE_ROWS, ROUTE_STEP), lambda i: (i // per_row, 0, i % per_row)),
            const((ROUTE_T, ROUTE_T)),
        ],
        out_specs=[
            pl.BlockSpec((SUBLANES, ROUTE_STEP), lambda i: (0, i)),
            pl.BlockSpec((SUBLANES, ROUTE_STEP), lambda i: (0, i)),
            const((N_EXPERTS, LANES)),
        ],
        out_shape=[
            jax.ShapeDtypeStruct((SUBLANES, T), jnp.int32),
            jax.ShapeDtypeStruct((SUBLANES, T), F32),
            jax.ShapeDtypeStruct((N_EXPERTS, LANES), jnp.int32),
        ],
        scratch_shapes=[pltpu.VMEM((N_EXPERTS, 1), F32)],
        compiler_params=pltpu.CompilerParams(
            dimension_semantics=("arbitrary",), vmem_limit_bytes=VMEM_LIMIT),
        name="router",
    )(logits_t, tri)


def _sc_mesh():
    return plsc.VectorSubcoreMesh(core_axis_name="c", subcore_axis_name="s",
                                  num_cores=SC_CORES, num_subcores=SC_SUBCORES)


def _sc_worker_id():
    return lax.axis_index("s") * SC_CORES + lax.axis_index("c")


def _sc_scratch(n_win, width):
    return [
        pltpu.VMEM((n_win, SC_WIN), jnp.int32),
        pltpu.VMEM((n_win, SC_WIN), jnp.int32),
        pltpu.VMEM((2, SC_WIN, width), U32),
        pltpu.SemaphoreType.DMA((2,)),
        pltpu.SemaphoreType.DMA((2,)),
    ]


def _sc_dispatch(rows, idx1, idx2, n_rows):
    _, width = rows.shape
    _, n_win, _ = idx1.shape

    @functools.partial(
        pl.kernel, mesh=_sc_mesh(), out_type=jax.ShapeDtypeStruct((n_rows, width), rows.dtype),
        scratch_types=_sc_scratch(n_win, width), name="sc_dispatch")
    def run(rows_hbm, i1_hbm, i2_hbm, o_hbm, i1_v, i2_v, buf, rsem, wsem):
        wid = _sc_worker_id()
        base = wid * (n_win * SC_WIN)
        pltpu.sync_copy(i1_hbm.at[wid], i1_v)
        pltpu.sync_copy(i2_hbm.at[wid], i2_v)

        def read(j):
            return pltpu.async_copy(rows_hbm.at[pl.ds(base + j * SC_WIN, SC_WIN)], buf.at[j % 2], rsem.at[j % 2])

        reads = {0: read(0)}
        writes = {}
        for j in range(n_win):
            if j + 1 < n_win:
                for d in writes.pop(j - 1, ()):
                    d.wait()
                reads[j + 1] = read(j + 1)
            reads.pop(j).wait()
            writes[j] = (pltpu.async_copy(buf.at[j % 2], o_hbm.at[i1_v.at[j]], wsem.at[j % 2]),
                         pltpu.async_copy(buf.at[j % 2], o_hbm.at[i2_v.at[j]], wsem.at[j % 2]))
        for j in sorted(writes):
            for d in writes[j]:
                d.wait()

    return run(rows, idx1, idx2)


def _sc_gather_pair(table, idx1, idx2):
    _, width = table.shape
    _, n_win, _ = idx1.shape
    n_tok = SC_WORKERS * n_win * SC_WIN
    out_t = jax.ShapeDtypeStruct((n_tok, width), table.dtype)

    @functools.partial(
        pl.kernel, mesh=_sc_mesh(), out_type=(out_t, out_t),
        scratch_types=_sc_scratch(n_win, width), name="sc_combine_gather")
    def run(table_hbm, i1_hbm, i2_hbm, o1_hbm, o2_hbm, i1_v, i2_v, buf, gsem, wsem):
        wid = _sc_worker_id()
        base = wid * (n_win * SC_WIN)
        pltpu.sync_copy(i1_hbm.at[wid], i1_v)
        pltpu.sync_copy(i2_hbm.at[wid], i2_v)
        work = [(i1_v, o1_hbm, j) for j in range(n_win)] + [(i2_v, o2_hbm, j) for j in range(n_win)]

        def gather(t):
            iv, _, j = work[t]
            return pltpu.async_copy(table_hbm.at[iv.at[j]], buf.at[t % 2], gsem.at[t % 2])

        def put(t):
            _, oh, j = work[t]
            return pltpu.async_copy(buf.at[t % 2], oh.at[pl.ds(base + j * SC_WIN, SC_WIN)], wsem.at[t % 2])

        gathers = {0: gather(0)}
        puts = {}
        for t in range(len(work)):
            if t + 1 < len(work):
                if t - 1 in puts:
                    puts.pop(t - 1).wait()
                gathers[t + 1] = gather(t + 1)
            gathers.pop(t).wait()
            puts[t] = put(t)
        for t in sorted(puts):
            puts[t].wait()

    return run(table, idx1, idx2)


def _moe_kernel(layer, n_blocks, be_ref, valid_ref, x_ref, w1_hbm, w3_hbm, w2_hbm, o_ref,
                w1_st, w3_st, w2_st, w1_sc, w3_sc, w2_sc, sems, ordinal_sm):
    i = pl.program_id(0)
    expert = be_ref[i]
    new_expert = jnp.logical_or(i == 0, expert != be_ref[jnp.maximum(i - 1, 0)])

    def weight_copies(e, sl):
        return [pltpu.make_async_copy(hbm.at[layer, e], stage.at[sl], sems.at[k, sl])
                for k, (hbm, stage) in enumerate(((w1_hbm, w1_st), (w3_hbm, w3_st), (w2_hbm, w2_st)))]

    @pl.when(i == 0)
    def _():
        ordinal_sm[0] = -1
        for cp in weight_copies(expert, 0):
            cp.start()

    @pl.when(new_expert)
    def _():
        ordinal = ordinal_sm[0] + 1
        ordinal_sm[0] = ordinal
        slot = ordinal & 1
        for cp in weight_copies(expert, slot):
            cp.wait()
        w1_sc[...] = w1_st[slot].astype(BF16)
        w3_sc[...] = w3_st[slot].astype(BF16)
        w2_sc[...] = w2_st[slot].astype(BF16)
        nxt_pos = lax.while_loop(
            lambda p: jnp.logical_and(p < n_blocks, be_ref[jnp.minimum(p, n_blocks - 1)] == expert),
            lambda p: p + 1, i + 1)

        @pl.when(nxt_pos < n_blocks)
        def _():
            for cp in weight_copies(be_ref[jnp.minimum(nxt_pos, n_blocks - 1)], 1 - slot):
                cp.start()

    def up(rows):
        xa, xb = _unpack_bf16_pair(x_ref[rows, :])
        xa = xa.astype(BF16)
        xb = xb.astype(BF16)
        h1 = (jnp.dot(xa, w1_sc[:D_HALF], preferred_element_type=F32)
              + jnp.dot(xb, w1_sc[D_HALF:], preferred_element_type=F32))
        h3 = (jnp.dot(xa, w3_sc[:D_HALF], preferred_element_type=F32)
              + jnp.dot(xb, w3_sc[D_HALF:], preferred_element_type=F32))
        return h1, h3

    def down(rows, h1, h3):
        hdn = (jax.nn.silu(h1) * h3).astype(BF16)
        y = jnp.dot(hdn, w2_sc[...], preferred_element_type=F32)
        o_ref[rows, :] = _pack_row_halves(y)

    n_sub = MOE_BM // MOE_SUB
    subs = [slice(k * MOE_SUB, (k + 1) * MOE_SUB) for k in range(n_sub)]
    valid = valid_ref[i]
    chains = (valid + (MOE_SUB - 1)) // MOE_SUB

    for live in range(n_sub + 1):
        @pl.when(chains == live)
        def _(live=live):
            ups = {}
            if live:
                ups[0] = up(subs[0])
            for k in range(live):
                if k + 1 < live:
                    ups[k + 1] = up(subs[k + 1])
                down(subs[k], *ups.pop(k))
            if live < n_sub:
                o_ref[live * MOE_SUB:, :] = jnp.zeros((MOE_BM - live * MOE_SUB, D_HALF), o_ref.dtype)


def _moe_blocks(xbuf, block_e, valid, w1, w3, w2, layer):
    n_rows, _ = xbuf.shape
    D = D_MODEL
    n_blocks = n_rows // MOE_BM
    rows = lambda i, be, nu: (i, 0)
    grid_spec = pltpu.PrefetchScalarGridSpec(
        num_scalar_prefetch=2,
        grid=(n_blocks,),
        in_specs=[
            pl.BlockSpec((MOE_BM, D_HALF), rows),
            pl.BlockSpec(memory_space=pl.ANY),
            pl.BlockSpec(memory_space=pl.ANY),
            pl.BlockSpec(memory_space=pl.ANY),
        ],
        out_specs=pl.BlockSpec((MOE_BM, D_HALF), rows),
        scratch_shapes=[
            pltpu.VMEM((2, D, D_EXPERT), F32),
            pltpu.VMEM((2, D, D_EXPERT), F32),
            pltpu.VMEM((2, D_EXPERT, D), F32),
            pltpu.VMEM((D, D_EXPERT), BF16),
            pltpu.VMEM((D, D_EXPERT), BF16),
            pltpu.VMEM((D_EXPERT, D), BF16),
            pltpu.SemaphoreType.DMA((3, 2)),
            pltpu.SMEM((1,), jnp.int32),
        ],
    )
    return pl.pallas_call(
        functools.partial(_moe_kernel, layer, n_blocks),
        grid_spec=grid_spec,
        out_shape=jax.ShapeDtypeStruct((n_rows, D_HALF), U32),
        compiler_params=pltpu.CompilerParams(
            dimension_semantics=("arbitrary",), vmem_limit_bytes=VMEM_LIMIT),
        name="moe_experts",
    )(block_e, valid, xbuf, w1, w3, w2)


def _moe_combine_norm(x, y1, y2, gate_rows, g, b):
    n = x.shape[0]
    pad = jnp.zeros((LANES - SUBLANES, n), F32)
    gates = jnp.concatenate([gate_rows, pad], axis=0).T
    g1 = gates[:, 0:1]
    g2 = gates[:, 1:2]
    a1, b1 = _unpack_bf16_pair(y1)
    a2, b2 = _unpack_bf16_pair(y2)
    f = jnp.concatenate([g1 * a1 + g2 * a2, g1 * b1 + g2 * b2], axis=1)
    return _layer_norm_rows(ALPHA * x + f, g, b)


def _combine_kernel(steps, x_hbm, y1_hbm, y2_hbm, gates_ref, g_ref, b_ref, o_ref, x_buf, y1_buf, y2_buf, sems):
    s = pl.program_id(0)

    def fetch(step, slot):
        rows = pl.ds(pl.multiple_of(step * COMB_T, COMB_T), COMB_T)
        return [pltpu.make_async_copy(hbm.at[rows], buf.at[slot], sems.at[k, slot])
                for k, (hbm, buf) in enumerate(((x_hbm, x_buf), (y1_hbm, y1_buf), (y2_hbm, y2_buf)))]

    @pl.when(s == 0)
    def _():
        for ahead in range(COMB_SLOTS - 1):
            for cp in fetch(ahead, ahead):
                cp.start()

    ahead = s + (COMB_SLOTS - 1)

    @pl.when(ahead < steps)
    def _():
        for cp in fetch(ahead, lax.rem(ahead, COMB_SLOTS)):
            cp.start()

    slot = lax.rem(s, COMB_SLOTS)
    for cp in fetch(s, slot):
        cp.wait()
    o_ref[...] = _moe_combine_norm(x_buf[slot], y1_buf[slot], y2_buf[slot], gates_ref[...], g_ref[...], b_ref[...])


def _combine_ln(xf, y1, y2, gates, ln_g, ln_b):
    T, D = xf.shape
    steps = T // COMB_T
    assert steps >= COMB_SLOTS
    const = lambda shape: pl.BlockSpec(shape, lambda i: (0,) * len(shape))
    hbm = pl.BlockSpec(memory_space=pl.ANY)
    return pl.pallas_call(
        functools.partial(_combine_kernel, steps),
        grid=(steps,),
        in_specs=[hbm, hbm, hbm, pl.BlockSpec((SUBLANES, COMB_T), lambda i: (0, i)), const((1, D)), const((1, D))],
        out_specs=pl.BlockSpec((COMB_T, D), lambda i: (i, 0)),
        out_shape=jax.ShapeDtypeStruct((T, D), F32),
        scratch_shapes=[
            pltpu.VMEM((COMB_SLOTS, COMB_T, D), F32),
            pltpu.VMEM((COMB_SLOTS, COMB_T, D_HALF), U32),
            pltpu.VMEM((COMB_SLOTS, COMB_T, D_HALF), U32),
            pltpu.SemaphoreType.DMA((3, COMB_SLOTS)),
        ],
        compiler_params=pltpu.CompilerParams(
            dimension_semantics=("arbitrary",), vmem_limit_bytes=VMEM_LIMIT),
        name="moe_combine_ln",
    )(xf, y1, y2, gates, ln_g.reshape(1, D), ln_b.reshape(1, D))


def _hier_moe(xpk, logits_t, w1, w3, w2, layer, gather_parts):
    B, S, _ = xpk.shape
    T = B * S
    idx, gates, cnt = _router(logits_t)
    counts = cnt[:, 0]
    padded = ((counts + MOE_BM - 1) // MOE_BM) * MOE_BM
    pends = jnp.cumsum(padded)
    pstarts = pends - padded
    experts = jnp.arange(N_EXPERTS, dtype=jnp.int32)[:, None]

    def dest(e_row, rank_row):
        return jnp.sum(jnp.where(e_row[None, :] == experts, pstarts[:, None], 0), axis=0) + rank_row

    n_win = T // (SC_WORKERS * SC_WIN)
    dest1 = dest(idx[0], idx[2]).reshape(SC_WORKERS, n_win, SC_WIN)
    dest2 = dest(idx[1], idx[3]).reshape(SC_WORKERS, n_win, SC_WIN)
    n_blocks = -(-(T * TOP_K + N_EXPERTS * (MOE_BM - 1)) // MOE_BM)
    n_rows = n_blocks * MOE_BM
    block_start = jnp.arange(n_blocks, dtype=jnp.int32) * MOE_BM
    block_e = jnp.minimum(jnp.sum(block_start[:, None] >= pends[None, :], axis=1), N_EXPERTS - 1).astype(jnp.int32)
    of_block = block_e[:, None] == experts[:, 0][None, :]
    pick = lambda v: jnp.sum(jnp.where(of_block, v[None, :], 0), axis=1)
    valid = jnp.clip(pick(counts) - (block_start - pick(pstarts)), 0, MOE_BM).astype(jnp.int32)
    xbuf = _sc_dispatch(xpk.reshape(T, D_HALF), dest1, dest2, n_rows)
    ybuf = _moe_blocks(xbuf, block_e, valid, w1, w3, w2, layer)
    part_shape = (gather_parts, SC_WORKERS, n_win // gather_parts, SC_WIN)
    d1, d2 = dest1.reshape(part_shape), dest2.reshape(part_shape)
    y_parts = [_sc_gather_pair(ybuf, d1[p], d2[p]) for p in range(gather_parts)]
    return y_parts, gates


def _att_head_order():
    order = []
    for p in range(N_HEADS // 2):
        jj, m = divmod(p, 4)
        order += [8 * jj + m, 8 * jj + 4 + m]
    return order


ATT_HEAD_ORDER = _att_head_order()


def _attn_kernel(xprev_ref, y1_ref, y2_ref, gates_a_ref, gates_b_ref, g_prev_ref, b_prev_ref, wqkv_ref, bias_ref,
                 sink_ref, wo_ref, g_ref, b_ref, wrt_ref, brt_ref, o_ref, opk_ref, lg_ref,
                 kv_ext, o_sc, s_sc0, s_sc1, p_sc0, p_sc1):
    s = pl.program_id(1)
    tq = ATT_TQ
    s_bufs = (s_sc0, s_sc1)
    p_bufs = (p_sc0, p_sc1)
    gate_refs = (gates_a_ref, gates_b_ref)
    assert ATT_NSEQ == len(gate_refs)

    @pl.when(s == 0)
    def _():
        kv_ext[:, 0:WINDOW, :] = jnp.zeros((ATT_NSEQ, WINDOW, 2 * KV_DIM), BF16)

    def layer_input(sq):
        return _moe_combine_norm(xprev_ref[sq], y1_ref[sq], y2_ref[sq], gate_refs[sq][...],
                                 g_prev_ref[...], b_prev_ref[...])

    def project_qkv(sq, x):
        qkv = jnp.dot(x.astype(BF16), wqkv_ref[...], preferred_element_type=F32)
        kv_ext[sq, WINDOW:WINDOW + tq, :] = qkv[:, Q_DIM:].astype(BF16)
        return (qkv[:, :Q_DIM] * (HEAD_DIM ** -0.5 * LOG2E)).astype(BF16)

    lane = lax.broadcasted_iota(jnp.int32, (WINDOW, LANES), 1)
    low = lane < HEAD_DIM
    sub = lax.broadcasted_iota(jnp.int32, (LANES, WINDOW), 0)
    top = sub < HEAD_DIM
    first = jnp.where(s == 0, 1, 0)
    nt = (((1,), (1,)), ((), ()))
    zero = jnp.zeros((), BF16)

    tiles = [(sq, n, j) for n in range(ATT_NB) for j in range(2) for sq in range(ATT_NSEQ)]

    def scores(t, qs_all):
        sq, n, j = tiles[t]
        q = qs_all[sq]
        r0 = n * WINDOW
        k_tile = kv_ext[sq, r0:r0 + 2 * WINDOW, j * LANES:(j + 1) * LANES]
        parts = []
        for m in range(4):
            p = 4 * j + m
            qt = q[r0:r0 + WINDOW, p * LANES:(p + 1) * LANES]
            parts.append(jnp.where(low, qt, zero))
            parts.append(jnp.where(low, zero, qt))
        qs = jnp.concatenate(parts, axis=0)
        bias_sel = first if n == 0 else 0
        s_bufs[t % 2][...] = (lax.dot_general(k_tile, qs, nt, preferred_element_type=F32)
                              + bias_ref[bias_sel, j])

    def softmax_pv(t):
        sq, n, j = tiles[t]
        r0 = n * WINDOW
        s_sc = s_bufs[t % 2]
        p_sc = p_bufs[t % 2]
        inv_l = []
        for h in range(8):
            hc = slice(h * WINDOW, (h + 1) * WINDOW)
            sink = sink_ref[8 * j + h] * LOG2E
            mx = jnp.maximum(jnp.max(s_sc[:, hc], axis=0, keepdims=True), sink)
            pr = jnp.exp2(s_sc[:, hc] - mx)
            p_sc[:, hc] = pr.astype(BF16)
            inv_l.append(1.0 / (jnp.sum(pr, axis=0, keepdims=True) + jnp.exp2(sink - mx)))
        v_tile = kv_ext[sq, r0:r0 + 2 * WINDOW, KV_DIM + j * LANES:KV_DIM + (j + 1) * LANES]
        tn = (((0,), (0,)), ((), ()))
        ov = lax.dot_general(v_tile, p_sc[...], tn, preferred_element_type=F32)
        for m in range(4):
            p = 4 * j + m
            o_even = ov[:, (2 * m) * WINDOW:(2 * m + 1) * WINDOW] * inv_l[2 * m]
            o_odd = ov[:, (2 * m + 1) * WINDOW:(2 * m + 2) * WINDOW] * inv_l[2 * m + 1]
            o_sc[sq, p * LANES:(p + 1) * LANES, r0:r0 + WINDOW] = jnp.where(top, o_even, o_odd).astype(BF16)

    def project_out(sq):
        tn = (((0,), (0,)), ((), ()))
        return lax.dot_general(o_sc[sq], wo_ref[...], tn, preferred_element_type=F32)

    def finish(sq, x, out):
        xn = _layer_norm_rows(ALPHA * x + out, g_ref[...], b_ref[...])
        o_ref[sq] = xn
        opk_ref[sq] = _pack_row_halves(xn)
        lg_ref[sq] = _router_logits_t(xn, wrt_ref[...], brt_ref[...])

    xs = [layer_input(0), layer_input(1)]
    qs_all = [project_qkv(0, xs[0]), project_qkv(1, xs[1])]
    scores(0, qs_all)
    for t in range(len(tiles)):
        if t + 1 < len(tiles):
            scores(t + 1, qs_all)
        softmax_pv(t)
        if tiles[t][0] == 0 and all(sq != 0 for sq, _, _ in tiles[t + 1:]):
            out_first = project_out(0)
    kv_ext[:, 0:WINDOW, :] = kv_ext[:, tq:tq + WINDOW, :]
    out_second = project_out(1)
    finish(0, xs[0], out_first)
    finish(1, xs[1], out_second)


def _attn_bias():
    qi = np.arange(WINDOW)[:, None]
    sj = np.arange(2 * WINDOW)[None, :]
    dist = qi - sj + WINDOW
    valid = (dist >= 0) & (dist < WINDOW)
    slopes = 2.0 ** (-8.0 * np.arange(1, N_HEADS + 1, dtype=np.float32) / N_HEADS)
    slopes = slopes.astype(np.float32)[ATT_HEAD_ORDER]
    sb = -(slopes[:, None, None] * dist.astype(np.float32)[None])
    later = np.where(valid[None], sb, -np.inf)
    first = np.where((valid & (sj >= WINDOW))[None], sb, -np.inf)
    bias = np.stack([later, first]).astype(np.float32) * np.float32(LOG2E)
    bias = bias.reshape(2, 2, 8, WINDOW, 2 * WINDOW).transpose(0, 1, 4, 2, 3).reshape(2, 2, 2 * WINDOW, 8 * WINDOW)
    return jnp.asarray(np.ascontiguousarray(bias))


ATT_N_INPUTS = 15


def _attn_part_kernel(*refs):
    _attn_kernel(*refs[:ATT_N_INPUTS], *refs[ATT_N_INPUTS + 3:])


def _attn_layer(x_prev, y_parts, gates, g_prev, b_prev, w_qkv, sinks, w_o, ln_g, ln_b, router_w, router_b):
    B, S, D = x_prev.shape
    steps = S // ATT_TQ
    assert ATT_HEAD_ORDER == list(np.arange(N_HEADS).reshape(2, 2, 4).transpose(0, 2, 1).reshape(-1))
    wq = w_qkv[:, :Q_DIM].reshape(D, 2, 2, 4, HEAD_DIM).transpose(0, 1, 3, 2, 4).reshape(D, Q_DIM)
    wqkv = jnp.concatenate([wq, w_qkv[:, Q_DIM:]], axis=1).astype(BF16)
    wo_t = w_o.reshape(2, 2, 4, HEAD_DIM, D).transpose(0, 2, 1, 3, 4).reshape(Q_DIM, D).astype(BF16)
    sink = sinks.reshape(2, 2, 4).transpose(0, 2, 1).reshape(N_HEADS, 1, 1)
    bias = _attn_bias()
    const = lambda shape: pl.BlockSpec(shape, lambda b, s: (0,) * len(shape))
    n_parts = len(y_parts)
    rows_per_part = B // n_parts
    pairs = rows_per_part // ATT_NSEQ
    outs = None
    for part, (y1, y2) in enumerate(y_parts):
        off = part * pairs
        glob = lambda w, off=off: pl.BlockSpec((ATT_NSEQ, ATT_TQ, w), lambda b, s: (b + off, s, 0))
        local = lambda w: pl.BlockSpec((ATT_NSEQ, ATT_TQ, w), lambda b, s: (b, s, 0))
        gate_rows = lambda sq, off=off: pl.BlockSpec(
            (SUBLANES, ATT_TQ), lambda b, s: (0, (ATT_NSEQ * (b + off) + sq) * steps + s))
        in_specs = [
            glob(D),
            local(D_HALF),
            local(D_HALF),
            gate_rows(0),
            gate_rows(1),
            const((1, D)),
            const((1, D)),
            const((D, Q_DIM + 2 * KV_DIM)),
            const((2, 2, 2 * WINDOW, 8 * WINDOW)),
            const((N_HEADS, 1, 1)),
            const((Q_DIM, D)),
            const((1, D)),
            const((1, D)),
            const((2 * ROUTE_ROWS, D)),
            const((ROUTE_ROWS, 1)),
        ]
        args = [x_prev, y1.reshape(rows_per_part, S, D_HALF), y2.reshape(rows_per_part, S, D_HALF), gates, gates,
                g_prev.reshape(1, D), b_prev.reshape(1, D), wqkv, bias, sink, wo_t,
                ln_g.reshape(1, D), ln_b.reshape(1, D), router_w, router_b]
        aliases = {}
        body = _attn_kernel
        if outs is not None:
            aliases = {len(args) + k: k for k in range(3)}
            in_specs += [pl.BlockSpec(memory_space=pl.ANY)] * 3
            args += list(outs)
            body = _attn_part_kernel
        outs = pl.pallas_call(
            body,
            grid=(pairs, steps),
            in_specs=in_specs,
            out_specs=[glob(D), glob(D_HALF),
                       pl.BlockSpec((ATT_NSEQ, ROUTE_ROWS, ATT_TQ), lambda b, s, off=off: (b + off, 0, s))],
            out_shape=[jax.ShapeDtypeStruct((B, S, D), F32), jax.ShapeDtypeStruct((B, S, D_HALF), U32),
                       jax.ShapeDtypeStruct((B, ROUTE_ROWS, S), F32)],
            input_output_aliases=aliases,
            scratch_shapes=[
                pltpu.VMEM((ATT_NSEQ, ATT_TQ + WINDOW, 2 * KV_DIM), BF16),
                pltpu.VMEM((ATT_NSEQ, Q_DIM, ATT_TQ), BF16),
                pltpu.VMEM((2 * WINDOW, 8 * WINDOW), F32),
                pltpu.VMEM((2 * WINDOW, 8 * WINDOW), F32),
                pltpu.VMEM((2 * WINDOW, 8 * WINDOW), BF16),
                pltpu.VMEM((2 * WINDOW, 8 * WINDOW), BF16),
            ],
            compiler_params=pltpu.CompilerParams(
                dimension_semantics=("arbitrary", "arbitrary"), vmem_limit_bytes=VMEM_LIMIT),
            name="swa_attn_ln",
        )(*args)
    return outs


def kernel(x, rec_w_in, rec_conv_w, rec_conv_b, rec_w_r, rec_b_r, rec_w_i, rec_b_i, rec_lambda, rec_w_out,
           att_w_qkv, att_sinks, att_w_o, moe_w_group, moe_b_group, moe_w_expert, moe_b_expert,
           moe_w1, moe_w3, moe_w2, ln_g, ln_b):
    assert DEPTH == 2
    B, S, D = x.shape

    router = [_router_weights(moe_w_group[layer], moe_b_group[layer], moe_w_expert[layer], moe_b_expert[layer])
              for layer in range(DEPTH)]

    x1, x1_pk, logits1 = _rglru_layer(x, rec_w_in[0], rec_conv_w[0], rec_conv_b[0], rec_w_r[0], rec_b_r[0],
                                      rec_w_i[0], rec_b_i[0], rec_lambda[0], rec_w_out[0], ln_g[0, 0], ln_b[0, 0],
                                      *router[0])
    y_parts, gates = _hier_moe(x1_pk, logits1, moe_w1, moe_w3, moe_w2, 0, gather_parts=ATT_PARTS)
    x3, x3_pk, logits3 = _attn_layer(x1, y_parts, gates, ln_g[0, 1], ln_b[0, 1], att_w_qkv[0], att_sinks[0],
                                     att_w_o[0], ln_g[1, 0], ln_b[1, 0], *router[1])
    ((y1, y2),), gates = _hier_moe(x3_pk, logits3, moe_w1, moe_w3, moe_w2, 1, gather_parts=1)
    out = _combine_ln(x3.reshape(B * S, D), y1, y2, gates, ln_g[1, 1], ln_b[1, 1])
    return out.reshape(B, S, D)
```

```python
import functools

import jax
import jax.numpy as jnp
import numpy as np
from jax import lax
from jax.experimental import pallas as pl
from jax.experimental.pallas import tpu as pltpu
from jax.experimental.pallas import tpu_sc as plsc

F32 = jnp.float32
BF16 = jnp.bfloat16
U32 = jnp.uint32

D_MODEL = 1024
DEPTH = 2
D_RNN = 1280
LRU_BLOCKS = 16
LRU_BLOCK_W = D_RNN // LRU_BLOCKS
CONV_W = 4
LRU_C = 8.0
N_HEADS = 16
N_KV_HEADS = 4
HEAD_DIM = 64
WINDOW = 128
Q_DIM = N_HEADS * HEAD_DIM
KV_DIM = N_KV_HEADS * HEAD_DIM
N_GROUPS = 4
EXPERTS_PER_GROUP = 8
N_EXPERTS = N_GROUPS * EXPERTS_PER_GROUP
TOP_K = 2
D_EXPERT = 512
ALPHA = (2 * DEPTH) ** 0.25
LN_EPS = 1e-5
LOG2E = 1.4426950408889634

LANES = 128
SUBLANES = 8
VMEM_LIMIT = 56 * 1024 * 1024

REC_TS = 256
REC_GROUPS = REC_TS // SUBLANES
REC_NSEQ = 2
GATE_TILE = 256
GATE_WIN = 512
GATE_WIN_STARTS = (0, 128, 384, 640, 768)
N_GATE_TILES = D_RNN // GATE_TILE

ROUTE_T = 512
ROUTE_STEP = 2048
ROUTE_ROWS = 40

MOE_BM = 1024
MOE_SUB = 256

ATT_TQ = 256
ATT_NB = ATT_TQ // WINDOW
ATT_NSEQ = 2
ATT_PARTS = 4

COMB_T = 1024
COMB_SLOTS = 3

D_HALF = D_MODEL // 2

SC_CORES = 2
SC_SUBCORES = 16
SC_WORKERS = SC_CORES * SC_SUBCORES
SC_WIN = 64
SC_BUFS = 3


def _layer_norm_rows(z, g, b):
    mu = jnp.mean(z, axis=-1, keepdims=True)
    zc = z - mu
    var = jnp.mean(zc * zc, axis=-1, keepdims=True)
    return zc * lax.rsqrt(var + LN_EPS) * g + b


def _pack_bf16_pair(a, b):
    ua = lax.bitcast_convert_type(a.astype(BF16).astype(F32), U32)
    ub = lax.bitcast_convert_type(b.astype(BF16).astype(F32), U32)
    return (ua >> 16) | (ub & jnp.uint32(0xFFFF0000))


def _unpack_bf16_pair(w):
    a = lax.bitcast_convert_type(w << 16, F32)
    b = lax.bitcast_convert_type(w & jnp.uint32(0xFFFF0000), F32)
    return a, b


def _pack_row_halves(x):
    return _pack_bf16_pair(x[:, :D_HALF], x[:, D_HALF:])


def _router_logits_t(x, w_split, bias):
    xhi = x.astype(BF16)
    xlo = (x - xhi.astype(F32)).astype(BF16)
    nt = (((1,), (1,)), ((), ()))
    both = lax.dot_general(w_split, xhi, nt, preferred_element_type=F32)
    low = lax.dot_general(w_split[:ROUTE_ROWS], xlo, nt, preferred_element_type=F32)
    return both[:ROUTE_ROWS] + both[ROUTE_ROWS:] + low + bias


def _rglru_kernel(x_ref, perm_ref, perm_t_ref, w_in_ref, convw_ref, convb_ref, wg_ref, br_ref, bi_ref, lam_ref,
                  w_out_ref, g_ref, b_ref, wrt_ref, brt_ref, o_ref, opk_ref, lg_ref,
                  xr_ext, tail_sc, a_sc, u_sc, h_carry):
    s = pl.program_id(1)
    ts = REC_TS
    halo = (CONV_W - 1) * SUBLANES

    @pl.when(s == 0)
    def _():
        tail_sc[...] = jnp.zeros((REC_NSEQ, halo, D_RNN), F32)
        h_carry[...] = jnp.zeros((REC_NSEQ, 1, D_RNN), F32)

    row = lax.broadcasted_iota(jnp.int32, (SUBLANES, D_RNN), 0)
    nlam = -lam_ref[...]
    sp = jnp.maximum(nlam, 0.0) + jnp.log1p(jnp.exp(-jnp.abs(nlam)))
    log2a_scale = (-LRU_C * LOG2E) * sp

    def project(q):
        xp = jnp.dot(perm_ref[...], x_ref[q].astype(BF16), preferred_element_type=F32).astype(BF16)
        proj = jnp.dot(xp, w_in_ref[...], preferred_element_type=F32)
        return proj[:, :D_RNN], proj[:, D_RNN:]

    def conv_gates(q, xr):
        for k in range(CONV_W - 1):
            r0 = ts - halo + k * SUBLANES
            cur = xr[r0:r0 + SUBLANES, :]
            prev = tail_sc[q, k * SUBLANES:(k + 1) * SUBLANES, :]
            xr_ext[q, k * SUBLANES:(k + 1) * SUBLANES, :] = jnp.where(
                row == 0, pltpu.roll(prev, 1, axis=0), pltpu.roll(cur, 1, axis=0))
        tail_sc[q] = xr[ts - halo:, :]
        xr_ext[q, halo:halo + ts, :] = xr
        xc = convb_ref[...] + convw_ref[CONV_W - 1:CONV_W, :] * xr
        for k in range(CONV_W - 1):
            xc = xc + convw_ref[k:k + 1, :] * xr_ext[q, k * SUBLANES:k * SUBLANES + ts, :]
        xcb = xc.astype(BF16)
        pres = [jnp.dot(xcb[:, GATE_WIN_STARTS[j]:GATE_WIN_STARTS[j] + GATE_WIN], wg_ref[j],
                        preferred_element_type=F32) for j in range(N_GATE_TILES)]
        for j, pre in enumerate(pres):
            cs = j * GATE_TILE
            r = jax.nn.sigmoid(pre[:, :GATE_TILE] + br_ref[:, cs:cs + GATE_TILE])
            i = jax.nn.sigmoid(pre[:, GATE_TILE:] + bi_ref[:, cs:cs + GATE_TILE])
            a = jnp.exp2(r * log2a_scale[:, cs:cs + GATE_TILE])
            s1 = 1.0 - a * a
            mult = jnp.where(s1 > 0.0, s1 * lax.rsqrt(s1), 0.0)
            u = mult * (i * xc[:, cs:cs + GATE_TILE])
            a_sc[q, :, cs:cs + GATE_TILE] = a
            u_sc[q, :, cs:cs + GATE_TILE] = u

    def segment_scan(q):
        h = jnp.zeros((SUBLANES, D_RNN), F32)
        prod = jnp.ones((SUBLANES, D_RNN), F32)
        for gidx in range(REC_GROUPS):
            rows = slice(gidx * SUBLANES, (gidx + 1) * SUBLANES)
            a8 = a_sc[q, rows, :]
            h = a8 * h + u_sc[q, rows, :]
            prod = a8 * prod
            u_sc[q, rows, :] = h
            a_sc[q, rows, :] = prod
        return h, prod

    def recur_out(q, gate, seg):
        seg_h, seg_a = seg
        for d in (1, 2, 4):
            keep = row >= d
            a_sh = jnp.where(keep, pltpu.roll(seg_a, d, axis=0), 1.0)
            h_sh = jnp.where(keep, pltpu.roll(seg_h, d, axis=0), 0.0)
            seg_h = seg_a * h_sh + seg_h
            seg_a = seg_a * a_sh
        h_in = h_carry[q]
        after = seg_a * h_in + seg_h
        enter = jnp.where(row == 0, h_in, pltpu.roll(after, 1, axis=0))
        h_carry[q] = after[SUBLANES - 1:SUBLANES, :]
        hs = (u_sc[q].reshape(REC_GROUPS, SUBLANES, D_RNN)
              + a_sc[q].reshape(REC_GROUPS, SUBLANES, D_RNN) * enter[None]).reshape(ts, D_RNN)
        y = hs * jax.nn.gelu(gate)
        y_t = jnp.dot(perm_t_ref[...], y.astype(BF16), preferred_element_type=F32).astype(BF16)
        return jnp.dot(y_t, w_out_ref[...], preferred_element_type=F32)

    def finish(q, out):
        z = ALPHA * x_ref[q] + out
        xn = _layer_norm_rows(z, g_ref[...], b_ref[...])
        o_ref[q] = xn
        opk_ref[q] = _pack_row_halves(xn)
        lg_ref[q] = _router_logits_t(xn, wrt_ref[...], brt_ref[...])

    assert REC_NSEQ == 2
    gate_a, xr_a = project(0)
    conv_gates(0, xr_a)
    gate_b, xr_b = project(1)
    conv_gates(1, xr_b)
    seg_a = segment_scan(0)
    seg_b = segment_scan(1)
    out_a = recur_out(0, gate_a, seg_a)
    out_b = recur_out(1, gate_b, seg_b)
    finish(0, out_a)
    finish(1, out_b)


def _band_gate_weights(w_r, w_i):
    spread = jnp.asarray(np.tile(np.eye(LRU_BLOCK_W, dtype=np.float32), (1, LRU_BLOCKS)), BF16)
    blk = np.arange(D_RNN) // LRU_BLOCK_W
    on_diag = jnp.asarray(blk[:, None] == blk[None, :])

    def dense(w):
        rows = w.reshape(D_RNN, LRU_BLOCK_W).astype(BF16)
        return jnp.where(on_diag, jnp.dot(rows, spread, preferred_element_type=F32), 0.0)

    wr, wi = dense(w_r), dense(w_i)
    tiles = []
    for j in range(N_GATE_TILES):
        ws = GATE_WIN_STARTS[j]
        cs = j * GATE_TILE
        lo_blk = cs // LRU_BLOCK_W
        hi_blk = (cs + GATE_TILE - 1) // LRU_BLOCK_W
        assert ws <= lo_blk * LRU_BLOCK_W and (hi_blk + 1) * LRU_BLOCK_W <= ws + GATE_WIN
        tiles.append(jnp.concatenate([wr[ws:ws + GATE_WIN, cs:cs + GATE_TILE],
                                      wi[ws:ws + GATE_WIN, cs:cs + GATE_TILE]], axis=1))
    return jnp.stack(tiles).astype(BF16)


def _rglru_layer(x, w_in, conv_w, conv_b, w_r, b_r, w_i, b_i, lam, w_out, ln_g, ln_b, router_w, router_b):
    B, S, D = x.shape
    wg = _band_gate_weights(w_r, w_i)
    rho = np.arange(REC_TS)
    perm_np = np.zeros((REC_TS, REC_TS), np.float32)
    perm_np[rho, (rho % SUBLANES) * REC_GROUPS + rho // SUBLANES] = 1.0
    perm = jnp.asarray(perm_np, BF16)
    perm_t = jnp.asarray(perm_np.T, BF16)
    row = lambda v: v.reshape(1, -1)
    const = lambda shape: pl.BlockSpec(shape, lambda b, s: (0,) * len(shape))
    tile = lambda w: pl.BlockSpec((REC_NSEQ, REC_TS, w), lambda b, s: (b, s, 0))
    halo = (CONV_W - 1) * SUBLANES
    return pl.pallas_call(
        _rglru_kernel,
        grid=(B // REC_NSEQ, S // REC_TS),
        in_specs=[
            tile(D),
            const((REC_TS, REC_TS)),
            const((REC_TS, REC_TS)),
            const((D, 2 * D_RNN)),
            const((CONV_W, D_RNN)),
            const((1, D_RNN)),
            const((N_GATE_TILES, GATE_WIN, 2 * GATE_TILE)),
            const((1, D_RNN)),
            const((1, D_RNN)),
            const((1, D_RNN)),
            const((D_RNN, D)),
            const((1, D)),
            const((1, D)),
            const((2 * ROUTE_ROWS, D)),
            const((ROUTE_ROWS, 1)),
        ],
        out_specs=[tile(D), tile(D_HALF),
                   pl.BlockSpec((REC_NSEQ, ROUTE_ROWS, REC_TS), lambda b, s: (b, 0, s))],
        out_shape=[jax.ShapeDtypeStruct((B, S, D), F32), jax.ShapeDtypeStruct((B, S, D_HALF), U32),
                   jax.ShapeDtypeStruct((B, ROUTE_ROWS, S), F32)],
        scratch_shapes=[
            pltpu.VMEM((REC_NSEQ, halo + REC_TS, D_RNN), F32),
            pltpu.VMEM((REC_NSEQ, halo, D_RNN), F32),
            pltpu.VMEM((REC_NSEQ, REC_TS, D_RNN), F32),
            pltpu.VMEM((REC_NSEQ, REC_TS, D_RNN), F32),
            pltpu.VMEM((REC_NSEQ, 1, D_RNN), F32),
        ],
        compiler_params=pltpu.CompilerParams(
            dimension_semantics=("arbitrary", "arbitrary"), vmem_limit_bytes=VMEM_LIMIT),
        name="rglru_ln",
    )(x, perm, perm_t, w_in.astype(BF16), conv_w, row(conv_b), wg, row(b_r), row(b_i), row(lam), w_out.astype(BF16),
      row(ln_g), row(ln_b), router_w, router_b)


def _router_kernel(logits_ref, tri_ref, idx_ref, gate_ref, cnt_ref, base_sc):
    step = pl.program_id(0)
    tr = ROUTE_T

    @pl.when(step == 0)
    def _():
        base_sc[...] = jnp.zeros((N_EXPERTS, 1), F32)

    row8 = lax.broadcasted_iota(jnp.int32, (SUBLANES, tr), 0).astype(F32)
    rowe = lax.broadcasted_iota(jnp.int32, (N_EXPERTS, tr), 0).astype(F32)
    neg_inf = -jnp.inf
    for sub in range(ROUTE_STEP // ROUTE_T):
        cols = slice(sub * tr, (sub + 1) * tr)
        _route_tile(logits_ref[0, :, cols], row8, rowe, neg_inf, tri_ref, idx_ref, gate_ref, base_sc, cols)
    cnt_ref[...] = jnp.broadcast_to(base_sc[...], (N_EXPERTS, LANES)).astype(jnp.int32)


def _route_tile(logits, row8, rowe, neg_inf, tri_ref, idx_ref, gate_ref, base_sc, cols):
    tr = ROUTE_T
    g = jnp.where(row8 < N_GROUPS, logits[N_EXPERTS:N_EXPERTS + SUBLANES, :], neg_inf)
    gmax = jnp.max(g, axis=0, keepdims=True)
    gidx = jnp.min(jnp.where(g == gmax, row8, SUBLANES), axis=0, keepdims=True)
    g_gate = 1.0 / jnp.sum(jnp.exp(g - gmax), axis=0, keepdims=True)

    esel = logits[0:EXPERTS_PER_GROUP, :]
    for grp in range(1, N_GROUPS):
        esel = jnp.where(gidx == grp, logits[grp * EXPERTS_PER_GROUP:(grp + 1) * EXPERTS_PER_GROUP, :], esel)
    v1 = jnp.max(esel, axis=0, keepdims=True)
    i1 = jnp.min(jnp.where(esel == v1, row8, SUBLANES), axis=0, keepdims=True)
    esel2 = jnp.where(row8 == i1, neg_inf, esel)
    v2 = jnp.max(esel2, axis=0, keepdims=True)
    i2 = jnp.min(jnp.where(esel2 == v2, row8, SUBLANES), axis=0, keepdims=True)
    e21 = jnp.exp(v2 - v1)
    inv = 1.0 / (1.0 + e21)
    gate1 = inv * g_gate
    gate2 = e21 * inv * g_gate
    e1 = gidx * EXPERTS_PER_GROUP + i1
    e2 = gidx * EXPERTS_PER_GROUP + i2

    hit1 = rowe == e1
    hit2 = rowe == e2
    member = jnp.where(hit1, 1.0, jnp.where(hit2, 1.0, 0.0))
    before = jnp.dot(member.astype(BF16), tri_ref[...], preferred_element_type=F32) + base_sc[...]
    rank1 = jnp.sum(jnp.where(hit1, before, 0.0), axis=0, keepdims=True)
    rank2 = jnp.sum(jnp.where(hit2, before, 0.0), axis=0, keepdims=True)
    base_sc[...] = base_sc[...] + jnp.sum(member, axis=1, keepdims=True)

    zi = jnp.zeros((1, tr), jnp.int32)
    idx_ref[:, cols] = jnp.concatenate(
        [e1.astype(jnp.int32), e2.astype(jnp.int32), rank1.astype(jnp.int32), rank2.astype(jnp.int32),
         zi, zi, zi, zi], axis=0)
    zf = jnp.zeros((1, tr), F32)
    gate_ref[:, cols] = jnp.concatenate([gate1, gate2, zf, zf, zf, zf, zf, zf], axis=0)


def _router_weights(w_rg, b_rg, w_re, b_re):
    D = w_rg.shape[0]
    pad_rows = ROUTE_ROWS - N_EXPERTS - N_GROUPS
    w = jnp.concatenate([w_re.T, w_rg.T, jnp.zeros((pad_rows, D), F32)], axis=0)
    whi = w.astype(BF16)
    wlo = (w - whi.astype(F32)).astype(BF16)
    w_split = jnp.concatenate([whi, wlo], axis=0)
    bias = jnp.concatenate([b_re, b_rg, jnp.zeros((pad_rows,), F32)]).reshape(ROUTE_ROWS, 1)
    return w_split, bias


def _router(logits_t):
    B, _, S = logits_t.shape
    T = B * S
    per_row = S // ROUTE_STEP
    tri = jnp.asarray(np.triu(np.ones((ROUTE_T, ROUTE_T), np.float32), 1), BF16)
    const = lambda shape: pl.BlockSpec(shape, lambda i: (0,) * len(shape))
    return pl.pallas_call(
        _router_kernel,
        grid=(T // ROUTE_STEP,),
        in_specs=[
            pl.BlockSpec((1, ROUTE_ROWS, ROUTE_STEP), lambda i: (i // per_row, 0, i % per_row)),
            const((ROUTE_T, ROUTE_T)),
        ],
        out_specs=[
            pl.BlockSpec((SUBLANES, ROUTE_STEP), lambda i: (0, i)),
            pl.BlockSpec((SUBLANES, ROUTE_STEP), lambda i: (0, i)),
            const((N_EXPERTS, LANES)),
        ],
        out_shape=[
            jax.ShapeDtypeStruct((SUBLANES, T), jnp.int32),
            jax.ShapeDtypeStruct((SUBLANES, T), F32),
            jax.ShapeDtypeStruct((N_EXPERTS, LANES), jnp.int32),
        ],
        scratch_shapes=[pltpu.VMEM((N_EXPERTS, 1), F32)],
        compiler_params=pltpu.CompilerParams(
            dimension_semantics=("arbitrary",), vmem_limit_bytes=VMEM_LIMIT),
        name="router",
    )(logits_t, tri)


def _sc_mesh():
    return plsc.VectorSubcoreMesh(core_axis_name="c", subcore_axis_name="s",
                                  num_cores=SC_CORES, num_subcores=SC_SUBCORES)


def _sc_worker_id():
    return lax.axis_index("s") * SC_CORES + lax.axis_index("c")


def _sc_scratch(n_win, width):
    return [
        pltpu.VMEM((n_win, SC_WIN), jnp.int32),
        pltpu.VMEM((n_win, SC_WIN), jnp.int32),
        pltpu.VMEM((SC_BUFS, SC_WIN, width), U32),
        pltpu.SemaphoreType.DMA((SC_BUFS,)),
        pltpu.SemaphoreType.DMA((SC_BUFS,)),
    ]


def _sc_dispatch(rows, idx1, idx2, n_rows):
    _, width = rows.shape
    _, n_win, _ = idx1.shape

    @functools.partial(
        pl.kernel, mesh=_sc_mesh(), out_type=jax.ShapeDtypeStruct((n_rows, width), rows.dtype),
        scratch_types=_sc_scratch(n_win, width), name="sc_dispatch")
    def run(rows_hbm, i1_hbm, i2_hbm, o_hbm, i1_v, i2_v, buf, rsem, wsem):
        wid = _sc_worker_id()
        base = wid * (n_win * SC_WIN)
        pltpu.sync_copy(i1_hbm.at[wid], i1_v)
        pltpu.sync_copy(i2_hbm.at[wid], i2_v)

        def read(j):
            b = j % SC_BUFS
            return pltpu.async_copy(rows_hbm.at[pl.ds(base + j * SC_WIN, SC_WIN)], buf.at[b], rsem.at[b])

        ahead = SC_BUFS - 1
        reads = {j: read(j) for j in range(min(ahead, n_win))}
        writes = {}
        for j in range(n_win):
            if j + ahead < n_win:
                for d in writes.pop(j - 1, ()):
                    d.wait()
                reads[j + ahead] = read(j + ahead)
            reads.pop(j).wait()
            b = j % SC_BUFS
            writes[j] = (pltpu.async_copy(buf.at[b], o_hbm.at[i1_v.at[j]], wsem.at[b]),
                         pltpu.async_copy(buf.at[b], o_hbm.at[i2_v.at[j]], wsem.at[b]))
        for j in sorted(writes):
            for d in writes[j]:
                d.wait()

    return run(rows, idx1, idx2)


def _sc_gather_pair(table, idx1, idx2):
    _, width = table.shape
    _, n_win, _ = idx1.shape
    n_tok = SC_WORKERS * n_win * SC_WIN
    out_t = jax.ShapeDtypeStruct((n_tok, width), table.dtype)

    @functools.partial(
        pl.kernel, mesh=_sc_mesh(), out_type=(out_t, out_t),
        scratch_types=_sc_scratch(n_win, width), name="sc_combine_gather")
    def run(table_hbm, i1_hbm, i2_hbm, o1_hbm, o2_hbm, i1_v, i2_v, buf, gsem, wsem):
        wid = _sc_worker_id()
        base = wid * (n_win * SC_WIN)
        pltpu.sync_copy(i1_hbm.at[wid], i1_v)
        pltpu.sync_copy(i2_hbm.at[wid], i2_v)
        work = [(i1_v, o1_hbm, j) for j in range(n_win)] + [(i2_v, o2_hbm, j) for j in range(n_win)]

        def gather(t):
            iv, _, j = work[t]
            b = t % SC_BUFS
            return pltpu.async_copy(table_hbm.at[iv.at[j]], buf.at[b], gsem.at[b])

        def put(t):
            _, oh, j = work[t]
            b = t % SC_BUFS
            return pltpu.async_copy(buf.at[b], oh.at[pl.ds(base + j * SC_WIN, SC_WIN)], wsem.at[b])

        ahead = SC_BUFS - 1
        gathers = {t: gather(t) for t in range(min(ahead, len(work)))}
        puts = {}
        for t in range(len(work)):
            if t + ahead < len(work):
                if t - 1 in puts:
                    puts.pop(t - 1).wait()
                gathers[t + ahead] = gather(t + ahead)
            gathers.pop(t).wait()
            puts[t] = put(t)
        for t in sorted(puts):
            puts[t].wait()

    return run(table, idx1, idx2)


def _moe_kernel(layer, n_blocks, be_ref, valid_ref, x_ref, w1_hbm, w3_hbm, w2_hbm, o_ref,
                w1_st, w3_st, w2_st, w1_sc, w3_sc, w2_sc, sems, ordinal_sm):
    i = pl.program_id(0)
    expert = be_ref[i]
    new_expert = jnp.logical_or(i == 0, expert != be_ref[jnp.maximum(i - 1, 0)])

    def weight_copies(e, sl):
        return [pltpu.make_async_copy(hbm.at[layer, e], stage.at[sl], sems.at[k, sl])
                for k, (hbm, stage) in enumerate(((w1_hbm, w1_st), (w3_hbm, w3_st), (w2_hbm, w2_st)))]

    @pl.when(i == 0)
    def _():
        ordinal_sm[0] = -1
        for cp in weight_copies(expert, 0):
            cp.start()

    @pl.when(new_expert)
    def _():
        ordinal = ordinal_sm[0] + 1
        ordinal_sm[0] = ordinal
        slot = ordinal & 1
        for cp in weight_copies(expert, slot):
            cp.wait()
        w1_sc[...] = w1_st[slot].astype(BF16)
        w3_sc[...] = w3_st[slot].astype(BF16)
        w2_sc[...] = w2_st[slot].astype(BF16)
        nxt_pos = lax.while_loop(
            lambda p: jnp.logical_and(p < n_blocks, be_ref[jnp.minimum(p, n_blocks - 1)] == expert),
            lambda p: p + 1, i + 1)

        @pl.when(nxt_pos < n_blocks)
        def _():
            for cp in weight_copies(be_ref[jnp.minimum(nxt_pos, n_blocks - 1)], 1 - slot):
                cp.start()

    def up(rows):
        xa, xb = _unpack_bf16_pair(x_ref[rows, :])
        xa = xa.astype(BF16)
        xb = xb.astype(BF16)
        h1 = (jnp.dot(xa, w1_sc[:D_HALF], preferred_element_type=F32)
              + jnp.dot(xb, w1_sc[D_HALF:], preferred_element_type=F32))
        h3 = (jnp.dot(xa, w3_sc[:D_HALF], preferred_element_type=F32)
              + jnp.dot(xb, w3_sc[D_HALF:], preferred_element_type=F32))
        return h1, h3

    def down(rows, h1, h3):
        hdn = (jax.nn.silu(h1) * h3).astype(BF16)
        y = jnp.dot(hdn, w2_sc[...], preferred_element_type=F32)
        o_ref[rows, :] = _pack_row_halves(y)

    n_sub = MOE_BM // MOE_SUB
    subs = [slice(k * MOE_SUB, (k + 1) * MOE_SUB) for k in range(n_sub)]
    valid = valid_ref[i]
    chains = (valid + (MOE_SUB - 1)) // MOE_SUB

    for live in range(n_sub + 1):
        @pl.when(chains == live)
        def _(live=live):
            ups = {}
            if live:
                ups[0] = up(subs[0])
            for k in range(live):
                if k + 1 < live:
                    ups[k + 1] = up(subs[k + 1])
                down(subs[k], *ups.pop(k))
            if live < n_sub:
                o_ref[live * MOE_SUB:, :] = jnp.zeros((MOE_BM - live * MOE_SUB, D_HALF), o_ref.dtype)


def _moe_blocks(xbuf, block_e, valid, w1, w3, w2, layer):
    n_rows, _ = xbuf.shape
    D = D_MODEL
    n_blocks = n_rows // MOE_BM
    rows = lambda i, be, nu: (i, 0)
    grid_spec = pltpu.PrefetchScalarGridSpec(
        num_scalar_prefetch=2,
        grid=(n_blocks,),
        in_specs=[
            pl.BlockSpec((MOE_BM, D_HALF), rows),
            pl.BlockSpec(memory_space=pl.ANY),
            pl.BlockSpec(memory_space=pl.ANY),
            pl.BlockSpec(memory_space=pl.ANY),
        ],
        out_specs=pl.BlockSpec((MOE_BM, D_HALF), rows),
        scratch_shapes=[
            pltpu.VMEM((2, D, D_EXPERT), F32),
            pltpu.VMEM((2, D, D_EXPERT), F32),
            pltpu.VMEM((2, D_EXPERT, D), F32),
            pltpu.VMEM((D, D_EXPERT), BF16),
            pltpu.VMEM((D, D_EXPERT), BF16),
            pltpu.VMEM((D_EXPERT, D), BF16),
            pltpu.SemaphoreType.DMA((3, 2)),
            pltpu.SMEM((1,), jnp.int32),
        ],
    )
    return pl.pallas_call(
        functools.partial(_moe_kernel, layer, n_blocks),
        grid_spec=grid_spec,
        out_shape=jax.ShapeDtypeStruct((n_rows, D_HALF), U32),
        compiler_params=pltpu.CompilerParams(
            dimension_semantics=("arbitrary",), vmem_limit_bytes=VMEM_LIMIT),
        name="moe_experts",
    )(block_e, valid, xbuf, w1, w3, w2)


def _moe_combine_norm(x, y1, y2, gate_rows, g, b):
    n = x.shape[0]
    pad = jnp.zeros((LANES - SUBLANES, n), F32)
    gates = jnp.concatenate([gate_rows, pad], axis=0).T
    g1 = gates[:, 0:1]
    g2 = gates[:, 1:2]
    a1, b1 = _unpack_bf16_pair(y1)
    a2, b2 = _unpack_bf16_pair(y2)
    f = jnp.concatenate([g1 * a1 + g2 * a2, g1 * b1 + g2 * b2], axis=1)
    return _layer_norm_rows(ALPHA * x + f, g, b)


def _combine_kernel(steps, x_hbm, y1_hbm, y2_hbm, gates_ref, g_ref, b_ref, o_ref, x_buf, y1_buf, y2_buf, sems):
    s = pl.program_id(0)

    def fetch(step, slot):
        rows = pl.ds(pl.multiple_of(step * COMB_T, COMB_T), COMB_T)
        return [pltpu.make_async_copy(hbm.at[rows], buf.at[slot], sems.at[k, slot])
                for k, (hbm, buf) in enumerate(((x_hbm, x_buf), (y1_hbm, y1_buf), (y2_hbm, y2_buf)))]

    @pl.when(s == 0)
    def _():
        for ahead in range(COMB_SLOTS - 1):
            for cp in fetch(ahead, ahead):
                cp.start()

    ahead = s + (COMB_SLOTS - 1)

    @pl.when(ahead < steps)
    def _():
        for cp in fetch(ahead, lax.rem(ahead, COMB_SLOTS)):
            cp.start()

    slot = lax.rem(s, COMB_SLOTS)
    for cp in fetch(s, slot):
        cp.wait()
    o_ref[...] = _moe_combine_norm(x_buf[slot], y1_buf[slot], y2_buf[slot], gates_ref[...], g_ref[...], b_ref[...])


def _combine_ln(xf, y1, y2, gates, ln_g, ln_b):
    T, D = xf.shape
    steps = T // COMB_T
    assert steps >= COMB_SLOTS
    const = lambda shape: pl.BlockSpec(shape, lambda i: (0,) * len(shape))
    hbm = pl.BlockSpec(memory_space=pl.ANY)
    return pl.pallas_call(
        functools.partial(_combine_kernel, steps),
        grid=(steps,),
        in_specs=[hbm, hbm, hbm, pl.BlockSpec((SUBLANES, COMB_T), lambda i: (0, i)), const((1, D)), const((1, D))],
        out_specs=pl.BlockSpec((COMB_T, D), lambda i: (i, 0)),
        out_shape=jax.ShapeDtypeStruct((T, D), F32),
        scratch_shapes=[
            pltpu.VMEM((COMB_SLOTS, COMB_T, D), F32),
            pltpu.VMEM((COMB_SLOTS, COMB_T, D_HALF), U32),
            pltpu.VMEM((COMB_SLOTS, COMB_T, D_HALF), U32),
            pltpu.SemaphoreType.DMA((3, COMB_SLOTS)),
        ],
        compiler_params=pltpu.CompilerParams(
            dimension_semantics=("arbitrary",), vmem_limit_bytes=VMEM_LIMIT),
        name="moe_combine_ln",
    )(xf, y1, y2, gates, ln_g.reshape(1, D), ln_b.reshape(1, D))


def _hier_moe(xpk, logits_t, w1, w3, w2, layer, gather_parts):
    B, S, _ = xpk.shape
    T = B * S
    idx, gates, cnt = _router(logits_t)
    counts = cnt[:, 0]
    padded = ((counts + MOE_BM - 1) // MOE_BM) * MOE_BM
    pends = jnp.cumsum(padded)
    pstarts = pends - padded
    experts = jnp.arange(N_EXPERTS, dtype=jnp.int32)[:, None]

    def dest(e_row, rank_row):
        return jnp.sum(jnp.where(e_row[None, :] == experts, pstarts[:, None], 0), axis=0) + rank_row

    n_win = T // (SC_WORKERS * SC_WIN)
    dest1 = dest(idx[0], idx[2]).reshape(SC_WORKERS, n_win, SC_WIN)
    dest2 = dest(idx[1], idx[3]).reshape(SC_WORKERS, n_win, SC_WIN)
    n_blocks = -(-(T * TOP_K + N_EXPERTS * (MOE_BM - 1)) // MOE_BM)
    n_rows = n_blocks * MOE_BM
    block_start = jnp.arange(n_blocks, dtype=jnp.int32) * MOE_BM
    block_e = jnp.minimum(jnp.sum(block_start[:, None] >= pends[None, :], axis=1), N_EXPERTS - 1).astype(jnp.int32)
    of_block = block_e[:, None] == experts[:, 0][None, :]
    pick = lambda v: jnp.sum(jnp.where(of_block, v[None, :], 0), axis=1)
    valid = jnp.clip(pick(counts) - (block_start - pick(pstarts)), 0, MOE_BM).astype(jnp.int32)
    xbuf = _sc_dispatch(xpk.reshape(T, D_HALF), dest1, dest2, n_rows)
    ybuf = _moe_blocks(xbuf, block_e, valid, w1, w3, w2, layer)
    part_shape = (gather_parts, SC_WORKERS, n_win // gather_parts, SC_WIN)
    d1, d2 = dest1.reshape(part_shape), dest2.reshape(part_shape)
    y_parts = [_sc_gather_pair(ybuf, d1[p], d2[p]) for p in range(gather_parts)]
    return y_parts, gates


def _att_head_order():
    order = []
    for p in range(N_HEADS // 2):
        jj, m = divmod(p, 4)
        order += [8 * jj + m, 8 * jj + 4 + m]
    return order


ATT_HEAD_ORDER = _att_head_order()


def _attn_kernel(xprev_ref, y1_ref, y2_ref, gates_a_ref, gates_b_ref, g_prev_ref, b_prev_ref, wqkv_ref, bias_ref,
                 sink_ref, wo_ref, g_ref, b_ref, wrt_ref, brt_ref, o_ref, opk_ref, lg_ref,
                 kv_ext, o_sc, s_sc0, s_sc1, p_sc0, p_sc1):
    s = pl.program_id(1)
    tq = ATT_TQ
    s_bufs = (s_sc0, s_sc1)
    p_bufs = (p_sc0, p_sc1)
    gate_refs = (gates_a_ref, gates_b_ref)
    assert ATT_NSEQ == len(gate_refs)

    @pl.when(s == 0)
    def _():
        kv_ext[:, 0:WINDOW, :] = jnp.zeros((ATT_NSEQ, WINDOW, 2 * KV_DIM), BF16)

    def layer_input(sq):
        return _moe_combine_norm(xprev_ref[sq], y1_ref[sq], y2_ref[sq], gate_refs[sq][...],
                                 g_prev_ref[...], b_prev_ref[...])

    def project_qkv(sq, x):
        qkv = jnp.dot(x.astype(BF16), wqkv_ref[...], preferred_element_type=F32)
        kv_ext[sq, WINDOW:WINDOW + tq, :] = qkv[:, Q_DIM:].astype(BF16)
        return (qkv[:, :Q_DIM] * (HEAD_DIM ** -0.5 * LOG2E)).astype(BF16)

    lane = lax.broadcasted_iota(jnp.int32, (WINDOW, LANES), 1)
    low = lane < HEAD_DIM
    sub = lax.broadcasted_iota(jnp.int32, (LANES, WINDOW), 0)
    top = sub < HEAD_DIM
    first = jnp.where(s == 0, 1, 0)
    nt = (((1,), (1,)), ((), ()))
    zero = jnp.zeros((), BF16)

    tiles = [(sq, n, j) for n in range(ATT_NB) for j in range(2) for sq in range(ATT_NSEQ)]

    def scores(t, qs_all):
        sq, n, j = tiles[t]
        q = qs_all[sq]
        r0 = n * WINDOW
        k_tile = kv_ext[sq, r0:r0 + 2 * WINDOW, j * LANES:(j + 1) * LANES]
        parts = []
        for m in range(4):
            p = 4 * j + m
            qt = q[r0:r0 + WINDOW, p * LANES:(p + 1) * LANES]
            parts.append(jnp.where(low, qt, zero))
            parts.append(jnp.where(low, zero, qt))
        qs = jnp.concatenate(parts, axis=0)
        bias_sel = first if n == 0 else 0
        s_bufs[t % 2][...] = (lax.dot_general(k_tile, qs, nt, preferred_element_type=F32)
                              + bias_ref[bias_sel, j])

    def softmax_pv(t):
        sq, n, j = tiles[t]
        r0 = n * WINDOW
        s_sc = s_bufs[t % 2]
        p_sc = p_bufs[t % 2]
        inv_l = []
        for h in range(8):
            hc = slice(h * WINDOW, (h + 1) * WINDOW)
            sink = sink_ref[8 * j + h] * LOG2E
            mx = jnp.maximum(jnp.max(s_sc[:, hc], axis=0, keepdims=True), sink)
            pr = jnp.exp2(s_sc[:, hc] - mx)
            p_sc[:, hc] = pr.astype(BF16)
            inv_l.append(1.0 / (jnp.sum(pr, axis=0, keepdims=True) + jnp.exp2(sink - mx)))
        v_tile = kv_ext[sq, r0:r0 + 2 * WINDOW, KV_DIM + j * LANES:KV_DIM + (j + 1) * LANES]
        tn = (((0,), (0,)), ((), ()))
        ov = lax.dot_general(v_tile, p_sc[...], tn, preferred_element_type=F32)
        for m in range(4):
            p = 4 * j + m
            o_even = ov[:, (2 * m) * WINDOW:(2 * m + 1) * WINDOW] * inv_l[2 * m]
            o_odd = ov[:, (2 * m + 1) * WINDOW:(2 * m + 2) * WINDOW] * inv_l[2 * m + 1]
            o_sc[sq, p * LANES:(p + 1) * LANES, r0:r0 + WINDOW] = jnp.where(top, o_even, o_odd).astype(BF16)

    def project_out(sq):
        tn = (((0,), (0,)), ((), ()))
        return lax.dot_general(o_sc[sq], wo_ref[...], tn, preferred_element_type=F32)

    def finish(sq, x, out):
        xn = _layer_norm_rows(ALPHA * x + out, g_ref[...], b_ref[...])
        o_ref[sq] = xn
        opk_ref[sq] = _pack_row_halves(xn)
        lg_ref[sq] = _router_logits_t(xn, wrt_ref[...], brt_ref[...])

    xs = [layer_input(0), layer_input(1)]
    qs_all = [project_qkv(0, xs[0]), project_qkv(1, xs[1])]
    scores(0, qs_all)
    for t in range(len(tiles)):
        if t + 1 < len(tiles):
            scores(t + 1, qs_all)
        softmax_pv(t)
        if tiles[t][0] == 0 and all(sq != 0 for sq, _, _ in tiles[t + 1:]):
            out_first = project_out(0)
    kv_ext[:, 0:WINDOW, :] = kv_ext[:, tq:tq + WINDOW, :]
    out_second = project_out(1)
    finish(0, xs[0], out_first)
    finish(1, xs[1], out_second)


def _attn_bias():
    qi = np.arange(WINDOW)[:, None]
    sj = np.arange(2 * WINDOW)[None, :]
    dist = qi - sj + WINDOW
    valid = (dist >= 0) & (dist < WINDOW)
    slopes = 2.0 ** (-8.0 * np.arange(1, N_HEADS + 1, dtype=np.float32) / N_HEADS)
    slopes = slopes.astype(np.float32)[ATT_HEAD_ORDER]
    sb = -(slopes[:, None, None] * dist.astype(np.float32)[None])
    later = np.where(valid[None], sb, -np.inf)
    first = np.where((valid & (sj >= WINDOW))[None], sb, -np.inf)
    bias = np.stack([later, first]).astype(np.float32) * np.float32(LOG2E)
    bias = bias.reshape(2, 2, 8, WINDOW, 2 * WINDOW).transpose(0, 1, 4, 2, 3).reshape(2, 2, 2 * WINDOW, 8 * WINDOW)
    return jnp.asarray(np.ascontiguousarray(bias))


ATT_N_INPUTS = 15


def _attn_part_kernel(*refs):
    _attn_kernel(*refs[:ATT_N_INPUTS], *refs[ATT_N_INPUTS + 3:])


def _attn_layer(x_prev, y_parts, gates, g_prev, b_prev, w_qkv, sinks, w_o, ln_g, ln_b, router_w, router_b):
    B, S, D = x_prev.shape
    steps = S // ATT_TQ
    assert ATT_HEAD_ORDER == list(np.arange(N_HEADS).reshape(2, 2, 4).transpose(0, 2, 1).reshape(-1))
    wq = w_qkv[:, :Q_DIM].reshape(D, 2, 2, 4, HEAD_DIM).transpose(0, 1, 3, 2, 4).reshape(D, Q_DIM)
    wqkv = jnp.concatenate([wq, w_qkv[:, Q_DIM:]], axis=1).astype(BF16)
    wo_t = w_o.reshape(2, 2, 4, HEAD_DIM, D).transpose(0, 2, 1, 3, 4).reshape(Q_DIM, D).astype(BF16)
    sink = sinks.reshape(2, 2, 4).transpose(0, 2, 1).reshape(N_HEADS, 1, 1)
    bias = _attn_bias()
    const = lambda shape: pl.BlockSpec(shape, lambda b, s: (0,) * len(shape))
    n_parts = len(y_parts)
    rows_per_part = B // n_parts
    pairs = rows_per_part // ATT_NSEQ
    outs = None
    for part, (y1, y2) in enumerate(y_parts):
        off = part * pairs
        glob = lambda w, off=off: pl.BlockSpec((ATT_NSEQ, ATT_TQ, w), lambda b, s: (b + off, s, 0))
        local = lambda w: pl.BlockSpec((ATT_NSEQ, ATT_TQ, w), lambda b, s: (b, s, 0))
        gate_rows = lambda sq, off=off: pl.BlockSpec(
            (SUBLANES, ATT_TQ), lambda b, s: (0, (ATT_NSEQ * (b + off) + sq) * steps + s))
        in_specs = [
            glob(D),
            local(D_HALF),
            local(D_HALF),
            gate_rows(0),
            gate_rows(1),
            const((1, D)),
            const((1, D)),
            const((D, Q_DIM + 2 * KV_DIM)),
            const((2, 2, 2 * WINDOW, 8 * WINDOW)),
            const((N_HEADS, 1, 1)),
            const((Q_DIM, D)),
            const((1, D)),
            const((1, D)),
            const((2 * ROUTE_ROWS, D)),
            const((ROUTE_ROWS, 1)),
        ]
        args = [x_prev, y1.reshape(rows_per_part, S, D_HALF), y2.reshape(rows_per_part, S, D_HALF), gates, gates,
                g_prev.reshape(1, D), b_prev.reshape(1, D), wqkv, bias, sink, wo_t,
                ln_g.reshape(1, D), ln_b.reshape(1, D), router_w, router_b]
        aliases = {}
        body = _attn_kernel
        if outs is not None:
            aliases = {len(args) + k: k for k in range(3)}
            in_specs += [pl.BlockSpec(memory_space=pl.ANY)] * 3
            args += list(outs)
            body = _attn_part_kernel
        outs = pl.pallas_call(
            body,
            grid=(pairs, steps),
            in_specs=in_specs,
            out_specs=[glob(D), glob(D_HALF),
                       pl.BlockSpec((ATT_NSEQ, ROUTE_ROWS, ATT_TQ), lambda b, s, off=off: (b + off, 0, s))],
            out_shape=[jax.ShapeDtypeStruct((B, S, D), F32), jax.ShapeDtypeStruct((B, S, D_HALF), U32),
                       jax.ShapeDtypeStruct((B, ROUTE_ROWS, S), F32)],
            input_output_aliases=aliases,
            scratch_shapes=[
                pltpu.VMEM((ATT_NSEQ, ATT_TQ + WINDOW, 2 * KV_DIM), BF16),
                pltpu.VMEM((ATT_NSEQ, Q_DIM, ATT_TQ), BF16),
                pltpu.VMEM((2 * WINDOW, 8 * WINDOW), F32),
                pltpu.VMEM((2 * WINDOW, 8 * WINDOW), F32),
                pltpu.VMEM((2 * WINDOW, 8 * WINDOW), BF16),
                pltpu.VMEM((2 * WINDOW, 8 * WINDOW), BF16),
            ],
            compiler_params=pltpu.CompilerParams(
                dimension_semantics=("arbitrary", "arbitrary"), vmem_limit_bytes=VMEM_LIMIT),
            name="swa_attn_ln",
        )(*args)
    return outs


def kernel(x, rec_w_in, rec_conv_w, rec_conv_b, rec_w_r, rec_b_r, rec_w_i, rec_b_i, rec_lambda, rec_w_out,
           att_w_qkv, att_sinks, att_w_o, moe_w_group, moe_b_group, moe_w_expert, moe_b_expert,
           moe_w1, moe_w3, moe_w2, ln_g, ln_b):
    assert DEPTH == 2
    B, S, D = x.shape

    router = [_router_weights(moe_w_group[layer], moe_b_group[layer], moe_w_expert[layer], moe_b_expert[layer])
              for layer in range(DEPTH)]

    x1, x1_pk, logits1 = _rglru_layer(x, rec_w_in[0], rec_conv_w[0], rec_conv_b[0], rec_w_r[0], rec_b_r[0],
                                      rec_w_i[0], rec_b_i[0], rec_lambda[0], rec_w_out[0], ln_g[0, 0], ln_b[0, 0],
                                      *router[0])
    y_parts, gates = _hier_moe(x1_pk, logits1, moe_w1, moe_w3, moe_w2, 0, gather_parts=ATT_PARTS)
    x3, x3_pk, logits3 = _attn_layer(x1, y_parts, gates, ln_g[0, 1], ln_b[0, 1], att_w_qkv[0], att_sinks[0],
                                     att_w_o[0], ln_g[1, 0], ln_b[1, 0], *router[1])
    ((y1, y2),), gates = _hier_moe(x3_pk, logits3, moe_w1, moe_w3, moe_w2, 1, gather_parts=1)
    out = _combine_ln(x3.reshape(B * S, D), y1, y2, gates, ln_g[1, 1], ln_b[1, 1])
    return out.reshape(B, S, D)
```

```python
import functools

import jax
import jax.numpy as jnp
import numpy as np
from jax import lax
from jax.experimental import pallas as pl
from jax.experimental.pallas import tpu as pltpu
from jax.experimental.pallas import tpu_sc as plsc

F32 = jnp.float32
BF16 = jnp.bfloat16
U32 = jnp.uint32

D_MODEL = 1024
DEPTH = 2
D_RNN = 1280
LRU_BLOCKS = 16
LRU_BLOCK_W = D_RNN // LRU_BLOCKS
CONV_W = 4
LRU_C = 8.0
N_HEADS = 16
N_KV_HEADS = 4
HEAD_DIM = 64
WINDOW = 128
Q_DIM = N_HEADS * HEAD_DIM
KV_DIM = N_KV_HEADS * HEAD_DIM
N_GROUPS = 4
EXPERTS_PER_GROUP = 8
N_EXPERTS = N_GROUPS * EXPERTS_PER_GROUP
TOP_K = 2
D_EXPERT = 512
ALPHA = (2 * DEPTH) ** 0.25
LN_EPS = 1e-5
LOG2E = 1.4426950408889634

LANES = 128
SUBLANES = 8
VMEM_LIMIT = 56 * 1024 * 1024

REC_TS = 256
REC_GROUPS = REC_TS // SUBLANES
REC_NSEQ = 2
GATE_TILE = 256
GATE_WIN = 512
GATE_WIN_STARTS = (0, 128, 384, 640, 768)
N_GATE_TILES = D_RNN // GATE_TILE

ROUTE_T = 512
ROUTE_STEP = 2048
ROUTE_ROWS = 40

MOE_BM = 1024
MOE_SUB = 256

ATT_TQ = 256
ATT_NB = ATT_TQ // WINDOW
ATT_NSEQ = 2
ATT_PARTS = 4

COMB_T = 1024
COMB_SLOTS = 4

D_HALF = D_MODEL // 2

SC_CORES = 2
SC_SUBCORES = 16
SC_WORKERS = SC_CORES * SC_SUBCORES
SC_WIN = 64


def _layer_norm_rows(z, g, b):
    mu = jnp.mean(z, axis=-1, keepdims=True)
    zc = z - mu
    var = jnp.mean(zc * zc, axis=-1, keepdims=True)
    return zc * lax.rsqrt(var + LN_EPS) * g + b


def _pack_bf16_pair(a, b):
    ua = lax.bitcast_convert_type(a.astype(BF16).astype(F32), U32)
    ub = lax.bitcast_convert_type(b.astype(BF16).astype(F32), U32)
    return (ua >> 16) | (ub & jnp.uint32(0xFFFF0000))


def _unpack_bf16_pair(w):
    a = lax.bitcast_convert_type(w << 16, F32)
    b = lax.bitcast_convert_type(w & jnp.uint32(0xFFFF0000), F32)
    return a, b


def _pack_row_halves(x):
    return _pack_bf16_pair(x[:, :D_HALF], x[:, D_HALF:])


def _router_logits_t(x, w_split, bias):
    xhi = x.astype(BF16)
    xlo = (x - xhi.astype(F32)).astype(BF16)
    nt = (((1,), (1,)), ((), ()))
    both = lax.dot_general(w_split, xhi, nt, preferred_element_type=F32)
    low = lax.dot_general(w_split[:ROUTE_ROWS], xlo, nt, preferred_element_type=F32)
    return both[:ROUTE_ROWS] + both[ROUTE_ROWS:] + low + bias


def _rglru_kernel(x_ref, perm_ref, perm_t_ref, w_in_ref, convw_ref, convb_ref, wg_ref, br_ref, bi_ref, lam_ref,
                  w_out_ref, g_ref, b_ref, wrt_ref, brt_ref, o_ref, opk_ref, lg_ref,
                  xr_ext, tail_sc, a_sc, u_sc, h_carry):
    s = pl.program_id(1)
    ts = REC_TS
    halo = (CONV_W - 1) * SUBLANES

    @pl.when(s == 0)
    def _():
        tail_sc[...] = jnp.zeros((REC_NSEQ, halo, D_RNN), F32)
        h_carry[...] = jnp.zeros((REC_NSEQ, 1, D_RNN), F32)

    row = lax.broadcasted_iota(jnp.int32, (SUBLANES, D_RNN), 0)
    nlam = -lam_ref[...]
    sp = jnp.maximum(nlam, 0.0) + jnp.log1p(jnp.exp(-jnp.abs(nlam)))
    log2a_scale = (-LRU_C * LOG2E) * sp

    def project(q):
        xp = jnp.dot(perm_ref[...], x_ref[q].astype(BF16), preferred_element_type=F32).astype(BF16)
        proj = jnp.dot(xp, w_in_ref[...], preferred_element_type=F32)
        return proj[:, :D_RNN], proj[:, D_RNN:]

    def conv_gates(q, xr):
        for k in range(CONV_W - 1):
            r0 = ts - halo + k * SUBLANES
            cur = xr[r0:r0 + SUBLANES, :]
            prev = tail_sc[q, k * SUBLANES:(k + 1) * SUBLANES, :]
            xr_ext[q, k * SUBLANES:(k + 1) * SUBLANES, :] = jnp.where(
                row == 0, pltpu.roll(prev, 1, axis=0), pltpu.roll(cur, 1, axis=0))
        tail_sc[q] = xr[ts - halo:, :]
        xr_ext[q, halo:halo + ts, :] = xr
        xc = convb_ref[...] + convw_ref[CONV_W - 1:CONV_W, :] * xr
        for k in range(CONV_W - 1):
            xc = xc + convw_ref[k:k + 1, :] * xr_ext[q, k * SUBLANES:k * SUBLANES + ts, :]
        xcb = xc.astype(BF16)
        pres = [jnp.dot(xcb[:, GATE_WIN_STARTS[j]:GATE_WIN_STARTS[j] + GATE_WIN], wg_ref[j],
                        preferred_element_type=F32) for j in range(N_GATE_TILES)]
        for j, pre in enumerate(pres):
            cs = j * GATE_TILE
            r = jax.nn.sigmoid(pre[:, :GATE_TILE] + br_ref[:, cs:cs + GATE_TILE])
            i = jax.nn.sigmoid(pre[:, GATE_TILE:] + bi_ref[:, cs:cs + GATE_TILE])
            a = jnp.exp2(r * log2a_scale[:, cs:cs + GATE_TILE])
            s1 = 1.0 - a * a
            mult = jnp.where(s1 > 0.0, s1 * lax.rsqrt(s1), 0.0)
            u = mult * (i * xc[:, cs:cs + GATE_TILE])
            a_sc[q, :, cs:cs + GATE_TILE] = a
            u_sc[q, :, cs:cs + GATE_TILE] = u

    def segment_scan(q):
        h = jnp.zeros((SUBLANES, D_RNN), F32)
        prod = jnp.ones((SUBLANES, D_RNN), F32)
        for gidx in range(REC_GROUPS):
            rows = slice(gidx * SUBLANES, (gidx + 1) * SUBLANES)
            a8 = a_sc[q, rows, :]
            h = a8 * h + u_sc[q, rows, :]
            prod = a8 * prod
            u_sc[q, rows, :] = h
            a_sc[q, rows, :] = prod
        return h, prod

    def recur_out(q, gate, seg):
        seg_h, seg_a = seg
        for d in (1, 2, 4):
            keep = row >= d
            a_sh = jnp.where(keep, pltpu.roll(seg_a, d, axis=0), 1.0)
            h_sh = jnp.where(keep, pltpu.roll(seg_h, d, axis=0), 0.0)
            seg_h = seg_a * h_sh + seg_h
            seg_a = seg_a * a_sh
        h_in = h_carry[q]
        after = seg_a * h_in + seg_h
        enter = jnp.where(row == 0, h_in, pltpu.roll(after, 1, axis=0))
        h_carry[q] = after[SUBLANES - 1:SUBLANES, :]
        hs = (u_sc[q].reshape(REC_GROUPS, SUBLANES, D_RNN)
              + a_sc[q].reshape(REC_GROUPS, SUBLANES, D_RNN) * enter[None]).reshape(ts, D_RNN)
        y = hs * jax.nn.gelu(gate)
        y_t = jnp.dot(perm_t_ref[...], y.astype(BF16), preferred_element_type=F32).astype(BF16)
        return jnp.dot(y_t, w_out_ref[...], preferred_element_type=F32)

    def finish(q, out):
        z = ALPHA * x_ref[q] + out
        xn = _layer_norm_rows(z, g_ref[...], b_ref[...])
        o_ref[q] = xn
        opk_ref[q] = _pack_row_halves(xn)
        lg_ref[q] = _router_logits_t(xn, wrt_ref[...], brt_ref[...])

    assert REC_NSEQ == 2
    gate_a, xr_a = project(0)
    conv_gates(0, xr_a)
    gate_b, xr_b = project(1)
    conv_gates(1, xr_b)
    seg_a = segment_scan(0)
    seg_b = segment_scan(1)
    out_a = recur_out(0, gate_a, seg_a)
    out_b = recur_out(1, gate_b, seg_b)
    finish(0, out_a)
    finish(1, out_b)


def _band_gate_weights(w_r, w_i):
    spread = jnp.asarray(np.tile(np.eye(LRU_BLOCK_W, dtype=np.float32), (1, LRU_BLOCKS)), BF16)
    blk = np.arange(D_RNN) // LRU_BLOCK_W
    on_diag = jnp.asarray(blk[:, None] == blk[None, :])

    def dense(w):
        rows = w.reshape(D_RNN, LRU_BLOCK_W).astype(BF16)
        return jnp.where(on_diag, jnp.dot(rows, spread, preferred_element_type=F32), 0.0)

    wr, wi = dense(w_r), dense(w_i)
    tiles = []
    for j in range(N_GATE_TILES):
        ws = GATE_WIN_STARTS[j]
        cs = j * GATE_TILE
        lo_blk = cs // LRU_BLOCK_W
        hi_blk = (cs + GATE_TILE - 1) // LRU_BLOCK_W
        assert ws <= lo_blk * LRU_BLOCK_W and (hi_blk + 1) * LRU_BLOCK_W <= ws + GATE_WIN
        tiles.append(jnp.concatenate([wr[ws:ws + GATE_WIN, cs:cs + GATE_TILE],
                                      wi[ws:ws + GATE_WIN, cs:cs + GATE_TILE]], axis=1))
    return jnp.stack(tiles).astype(BF16)


def _rglru_layer(x, w_in, conv_w, conv_b, w_r, b_r, w_i, b_i, lam, w_out, ln_g, ln_b, router_w, router_b):
    B, S, D = x.shape
    wg = _band_gate_weights(w_r, w_i)
    rho = np.arange(REC_TS)
    perm_np = np.zeros((REC_TS, REC_TS), np.float32)
    perm_np[rho, (rho % SUBLANES) * REC_GROUPS + rho // SUBLANES] = 1.0
    perm = jnp.asarray(perm_np, BF16)
    perm_t = jnp.asarray(perm_np.T, BF16)
    row = lambda v: v.reshape(1, -1)
    const = lambda shape: pl.BlockSpec(shape, lambda b, s: (0,) * len(shape))
    tile = lambda w: pl.BlockSpec((REC_NSEQ, REC_TS, w), lambda b, s: (b, s, 0))
    halo = (CONV_W - 1) * SUBLANES
    return pl.pallas_call(
        _rglru_kernel,
        grid=(B // REC_NSEQ, S // REC_TS),
        in_specs=[
            tile(D),
            const((REC_TS, REC_TS)),
            const((REC_TS, REC_TS)),
            const((D, 2 * D_RNN)),
            const((CONV_W, D_RNN)),
            const((1, D_RNN)),
            const((N_GATE_TILES, GATE_WIN, 2 * GATE_TILE)),
            const((1, D_RNN)),
            const((1, D_RNN)),
            const((1, D_RNN)),
            const((D_RNN, D)),
            const((1, D)),
            const((1, D)),
            const((2 * ROUTE_ROWS, D)),
            const((ROUTE_ROWS, 1)),
        ],
        out_specs=[tile(D), tile(D_HALF),
                   pl.BlockSpec((REC_NSEQ, ROUTE_ROWS, REC_TS), lambda b, s: (b, 0, s))],
        out_shape=[jax.ShapeDtypeStruct((B, S, D), F32), jax.ShapeDtypeStruct((B, S, D_HALF), U32),
                   jax.ShapeDtypeStruct((B, ROUTE_ROWS, S), F32)],
        scratch_shapes=[
            pltpu.VMEM((REC_NSEQ, halo + REC_TS, D_RNN), F32),
            pltpu.VMEM((REC_NSEQ, halo, D_RNN), F32),
            pltpu.VMEM((REC_NSEQ, REC_TS, D_RNN), F32),
            pltpu.VMEM((REC_NSEQ, REC_TS, D_RNN), F32),
            pltpu.VMEM((REC_NSEQ, 1, D_RNN), F32),
        ],
        compiler_params=pltpu.CompilerParams(
            dimension_semantics=("arbitrary", "arbitrary"), vmem_limit_bytes=VMEM_LIMIT),
        name="rglru_ln",
    )(x, perm, perm_t, w_in.astype(BF16), conv_w, row(conv_b), wg, row(b_r), row(b_i), row(lam), w_out.astype(BF16),
      row(ln_g), row(ln_b), router_w, router_b)


def _router_kernel(logits_ref, tri_ref, idx_ref, gate_ref, cnt_ref, base_sc):
    step = pl.program_id(0)
    tr = ROUTE_T

    @pl.when(step == 0)
    def _():
        base_sc[...] = jnp.zeros((N_EXPERTS, 1), F32)

    row8 = lax.broadcasted_iota(jnp.int32, (SUBLANES, tr), 0).astype(F32)
    rowe = lax.broadcasted_iota(jnp.int32, (N_EXPERTS, tr), 0).astype(F32)
    neg_inf = -jnp.inf
    for sub in range(ROUTE_STEP // ROUTE_T):
        cols = slice(sub * tr, (sub + 1) * tr)
        _route_tile(logits_ref[0, :, cols], row8, rowe, neg_inf, tri_ref, idx_ref, gate_ref, base_sc, cols)
    cnt_ref[...] = jnp.broadcast_to(base_sc[...], (N_EXPERTS, LANES)).astype(jnp.int32)


def _route_tile(logits, row8, rowe, neg_inf, tri_ref, idx_ref, gate_ref, base_sc, cols):
    tr = ROUTE_T
    g = jnp.where(row8 < N_GROUPS, logits[N_EXPERTS:N_EXPERTS + SUBLANES, :], neg_inf)
    gmax = jnp.max(g, axis=0, keepdims=True)
    gidx = jnp.min(jnp.where(g == gmax, row8, SUBLANES), axis=0, keepdims=True)
    g_gate = 1.0 / jnp.sum(jnp.exp(g - gmax), axis=0, keepdims=True)

    esel = logits[0:EXPERTS_PER_GROUP, :]
    for grp in range(1, N_GROUPS):
        esel = jnp.where(gidx == grp, logits[grp * EXPERTS_PER_GROUP:(grp + 1) * EXPERTS_PER_GROUP, :], esel)
    v1 = jnp.max(esel, axis=0, keepdims=True)
    i1 = jnp.min(jnp.where(esel == v1, row8, SUBLANES), axis=0, keepdims=True)
    esel2 = jnp.where(row8 == i1, neg_inf, esel)
    v2 = jnp.max(esel2, axis=0, keepdims=True)
    i2 = jnp.min(jnp.where(esel2 == v2, row8, SUBLANES), axis=0, keepdims=True)
    e21 = jnp.exp(v2 - v1)
    inv = 1.0 / (1.0 + e21)
    gate1 = inv * g_gate
    gate2 = e21 * inv * g_gate
    e1 = gidx * EXPERTS_PER_GROUP + i1
    e2 = gidx * EXPERTS_PER_GROUP + i2

    hit1 = rowe == e1
    hit2 = rowe == e2
    member = jnp.where(hit1, 1.0, jnp.where(hit2, 1.0, 0.0))
    before = jnp.dot(member.astype(BF16), tri_ref[...], preferred_element_type=F32) + base_sc[...]
    rank1 = jnp.sum(jnp.where(hit1, before, 0.0), axis=0, keepdims=True)
    rank2 = jnp.sum(jnp.where(hit2, before, 0.0), axis=0, keepdims=True)
    base_sc[...] = base_sc[...] + jnp.sum(member, axis=1, keepdims=True)

    zi = jnp.zeros((1, tr), jnp.int32)
    idx_ref[:, cols] = jnp.concatenate(
        [e1.astype(jnp.int32), e2.astype(jnp.int32), rank1.astype(jnp.int32), rank2.astype(jnp.int32),
         zi, zi, zi, zi], axis=0)
    zf = jnp.zeros((1, tr), F32)
    gate_ref[:, cols] = jnp.concatenate([gate1, gate2, zf, zf, zf, zf, zf, zf], axis=0)


def _router_weights(w_rg, b_rg, w_re, b_re):
    D = w_rg.shape[0]
    pad_rows = ROUTE_ROWS - N_EXPERTS - N_GROUPS
    w = jnp.concatenate([w_re.T, w_rg.T, jnp.zeros((pad_rows, D), F32)], axis=0)
    whi = w.astype(BF16)
    wlo = (w - whi.astype(F32)).astype(BF16)
    w_split = jnp.concatenate([whi, wlo], axis=0)
    bias = jnp.concatenate([b_re, b_rg, jnp.zeros((pad_rows,), F32)]).reshape(ROUTE_ROWS, 1)
    return w_split, bias


def _router(logits_t):
    B, _, S = logits_t.shape
    T = B * S
    per_row = S // ROUTE_STEP
    tri = jnp.asarray(np.triu(np.ones((ROUTE_T, ROUTE_T), np.float32), 1), BF16)
    const = lambda shape: pl.BlockSpec(shape, lambda i: (0,) * len(shape))
    return pl.pallas_call(
        _router_kernel,
        grid=(T // ROUTE_STEP,),
        in_specs=[
            pl.BlockSpec((1, ROUTE_ROWS, ROUTE_STEP), lambda i: (i // per_row, 0, i % per_row)),
            const((ROUTE_T, ROUTE_T)),
        ],
        out_specs=[
            pl.BlockSpec((SUBLANES, ROUTE_STEP), lambda i: (0, i)),
            pl.BlockSpec((SUBLANES, ROUTE_STEP), lambda i: (0, i)),
            const((N_EXPERTS, LANES)),
        ],
        out_shape=[
            jax.ShapeDtypeStruct((SUBLANES, T), jnp.int32),
            jax.ShapeDtypeStruct((SUBLANES, T), F32),
            jax.ShapeDtypeStruct((N_EXPERTS, LANES), jnp.int32),
        ],
        scratch_shapes=[pltpu.VMEM((N_EXPERTS, 1), F32)],
        compiler_params=pltpu.CompilerParams(
            dimension_semantics=("arbitrary",), vmem_limit_bytes=VMEM_LIMIT),
        name="router",
    )(logits_t, tri)


def _sc_mesh():
    return plsc.VectorSubcoreMesh(core_axis_name="c", subcore_axis_name="s",
                                  num_cores=SC_CORES, num_subcores=SC_SUBCORES)


def _sc_worker_id():
    return lax.axis_index("s") * SC_CORES + lax.axis_index("c")


def _sc_scratch(n_win, width):
    return [
        pltpu.VMEM((n_win, SC_WIN), jnp.int32),
        pltpu.VMEM((n_win, SC_WIN), jnp.int32),
        pltpu.VMEM((2, SC_WIN, width), U32),
        pltpu.SemaphoreType.DMA((2,)),
        pltpu.SemaphoreType.DMA((2,)),
    ]


def _sc_dispatch(rows, idx1, idx2, n_rows):
    _, width = rows.shape
    _, n_win, _ = idx1.shape

    @functools.partial(
        pl.kernel, mesh=_sc_mesh(), out_type=jax.ShapeDtypeStruct((n_rows, width), rows.dtype),
        scratch_types=_sc_scratch(n_win, width), name="sc_dispatch")
    def run(rows_hbm, i1_hbm, i2_hbm, o_hbm, i1_v, i2_v, buf, rsem, wsem):
        wid = _sc_worker_id()
        base = wid * (n_win * SC_WIN)
        pltpu.sync_copy(i1_hbm.at[wid], i1_v)
        pltpu.sync_copy(i2_hbm.at[wid], i2_v)

        def read(j):
            return pltpu.async_copy(rows_hbm.at[pl.ds(base + j * SC_WIN, SC_WIN)], buf.at[j % 2], rsem.at[j % 2])

        reads = {0: read(0)}
        writes = {}
        for j in range(n_win):
            if j + 1 < n_win:
                for d in writes.pop(j - 1, ()):
                    d.wait()
                reads[j + 1] = read(j + 1)
            reads.pop(j).wait()
            writes[j] = (pltpu.async_copy(buf.at[j % 2], o_hbm.at[i1_v.at[j]], wsem.at[j % 2]),
                         pltpu.async_copy(buf.at[j % 2], o_hbm.at[i2_v.at[j]], wsem.at[j % 2]))
        for j in sorted(writes):
            for d in writes[j]:
                d.wait()

    return run(rows, idx1, idx2)


def _sc_gather_pair(table, idx1, idx2):
    _, width = table.shape
    _, n_win, _ = idx1.shape
    n_tok = SC_WORKERS * n_win * SC_WIN
    out_t = jax.ShapeDtypeStruct((n_tok, width), table.dtype)

    @functools.partial(
        pl.kernel, mesh=_sc_mesh(), out_type=(out_t, out_t),
        scratch_types=_sc_scratch(n_win, width), name="sc_combine_gather")
    def run(table_hbm, i1_hbm, i2_hbm, o1_hbm, o2_hbm, i1_v, i2_v, buf, gsem, wsem):
        wid = _sc_worker_id()
        base = wid * (n_win * SC_WIN)
        pltpu.sync_copy(i1_hbm.at[wid], i1_v)
        pltpu.sync_copy(i2_hbm.at[wid], i2_v)
        work = [(i1_v, o1_hbm, j) for j in range(n_win)] + [(i2_v, o2_hbm, j) for j in range(n_win)]

        def gather(t):
            iv, _, j = work[t]
            return pltpu.async_copy(table_hbm.at[iv.at[j]], buf.at[t % 2], gsem.at[t % 2])

        def put(t):
            _, oh, j = work[t]
            return pltpu.async_copy(buf.at[t % 2], oh.at[pl.ds(base + j * SC_WIN, SC_WIN)], wsem.at[t % 2])

        gathers = {0: gather(0)}
        puts = {}
        for t in range(len(work)):
            if t + 1 < len(work):
                if t - 1 in puts:
                    puts.pop(t - 1).wait()
                gathers[t + 1] = gather(t + 1)
            gathers.pop(t).wait()
            puts[t] = put(t)
        for t in sorted(puts):
            puts[t].wait()

    return run(table, idx1, idx2)


def _moe_kernel(layer, n_blocks, be_ref, valid_ref, x_ref, w1_hbm, w3_hbm, w2_hbm, o_ref,
                w1_st, w3_st, w2_st, w1_sc, w3_sc, w2_sc, sems, ordinal_sm):
    i = pl.program_id(0)
    expert = be_ref[i]
    new_expert = jnp.logical_or(i == 0, expert != be_ref[jnp.maximum(i - 1, 0)])

    def weight_copies(e, sl):
        return [pltpu.make_async_copy(hbm.at[layer, e], stage.at[sl], sems.at[k, sl])
                for k, (hbm, stage) in enumerate(((w1_hbm, w1_st), (w3_hbm, w3_st), (w2_hbm, w2_st)))]

    @pl.when(i == 0)
    def _():
        ordinal_sm[0] = -1
        for cp in weight_copies(expert, 0):
            cp.start()

    @pl.when(new_expert)
    def _():
        ordinal = ordinal_sm[0] + 1
        ordinal_sm[0] = ordinal
        slot = ordinal & 1
        for cp in weight_copies(expert, slot):
            cp.wait()
        w1_sc[...] = w1_st[slot].astype(BF16)
        w3_sc[...] = w3_st[slot].astype(BF16)
        w2_sc[...] = w2_st[slot].astype(BF16)
        nxt_pos = lax.while_loop(
            lambda p: jnp.logical_and(p < n_blocks, be_ref[jnp.minimum(p, n_blocks - 1)] == expert),
            lambda p: p + 1, i + 1)

        @pl.when(nxt_pos < n_blocks)
        def _():
            for cp in weight_copies(be_ref[jnp.minimum(nxt_pos, n_blocks - 1)], 1 - slot):
                cp.start()

    def up(rows):
        xa, xb = _unpack_bf16_pair(x_ref[rows, :])
        xa = xa.astype(BF16)
        xb = xb.astype(BF16)
        h1 = (jnp.dot(xa, w1_sc[:D_HALF], preferred_element_type=F32)
              + jnp.dot(xb, w1_sc[D_HALF:], preferred_element_type=F32))
        h3 = (jnp.dot(xa, w3_sc[:D_HALF], preferred_element_type=F32)
              + jnp.dot(xb, w3_sc[D_HALF:], preferred_element_type=F32))
        return h1, h3

    def down(rows, h1, h3):
        hdn = (jax.nn.silu(h1) * h3).astype(BF16)
        y = jnp.dot(hdn, w2_sc[...], preferred_element_type=F32)
        o_ref[rows, :] = _pack_row_halves(y)

    n_sub = MOE_BM // MOE_SUB
    subs = [slice(k * MOE_SUB, (k + 1) * MOE_SUB) for k in range(n_sub)]
    valid = valid_ref[i]
    chains = (valid + (MOE_SUB - 1)) // MOE_SUB

    for live in range(n_sub + 1):
        @pl.when(chains == live)
        def _(live=live):
            ups = {}
            if live:
                ups[0] = up(subs[0])
            for k in range(live):
                if k + 1 < live:
                    ups[k + 1] = up(subs[k + 1])
                down(subs[k], *ups.pop(k))
            if live < n_sub:
                o_ref[live * MOE_SUB:, :] = jnp.zeros((MOE_BM - live * MOE_SUB, D_HALF), o_ref.dtype)


def _moe_blocks(xbuf, block_e, valid, w1, w3, w2, layer):
    n_rows, _ = xbuf.shape
    D = D_MODEL
    n_blocks = n_rows // MOE_BM
    rows = lambda i, be, nu: (i, 0)
    grid_spec = pltpu.PrefetchScalarGridSpec(
        num_scalar_prefetch=2,
        grid=(n_blocks,),
        in_specs=[
            pl.BlockSpec((MOE_BM, D_HALF), rows),
            pl.BlockSpec(memory_space=pl.ANY),
            pl.BlockSpec(memory_space=pl.ANY),
            pl.BlockSpec(memory_space=pl.ANY),
        ],
        out_specs=pl.BlockSpec((MOE_BM, D_HALF), rows),
        scratch_shapes=[
            pltpu.VMEM((2, D, D_EXPERT), F32),
            pltpu.VMEM((2, D, D_EXPERT), F32),
            pltpu.VMEM((2, D_EXPERT, D), F32),
            pltpu.VMEM((D, D_EXPERT), BF16),
            pltpu.VMEM((D, D_EXPERT), BF16),
            pltpu.VMEM((D_EXPERT, D), BF16),
            pltpu.SemaphoreType.DMA((3, 2)),
            pltpu.SMEM((1,), jnp.int32),
        ],
    )
    return pl.pallas_call(
        functools.partial(_moe_kernel, layer, n_blocks),
        grid_spec=grid_spec,
        out_shape=jax.ShapeDtypeStruct((n_rows, D_HALF), U32),
        compiler_params=pltpu.CompilerParams(
            dimension_semantics=("arbitrary",), vmem_limit_bytes=VMEM_LIMIT),
        name="moe_experts",
    )(block_e, valid, xbuf, w1, w3, w2)


def _moe_combine_norm(x, y1, y2, gate_rows, g, b):
    n = x.shape[0]
    pad = jnp.zeros((LANES - SUBLANES, n), F32)
    gates = jnp.concatenate([gate_rows, pad], axis=0).T
    g1 = gates[:, 0:1]
    g2 = gates[:, 1:2]
    a1, b1 = _unpack_bf16_pair(y1)
    a2, b2 = _unpack_bf16_pair(y2)
    f = jnp.concatenate([g1 * a1 + g2 * a2, g1 * b1 + g2 * b2], axis=1)
    return _layer_norm_rows(ALPHA * x + f, g, b)


def _combine_kernel(steps, x_hbm, y1_hbm, y2_hbm, gates_ref, g_ref, b_ref, o_ref, x_buf, y1_buf, y2_buf, sems):
    s = pl.program_id(0)

    def fetch(step, slot):
        rows = pl.ds(pl.multiple_of(step * COMB_T, COMB_T), COMB_T)
        return [pltpu.make_async_copy(hbm.at[rows], buf.at[slot], sems.at[k, slot])
                for k, (hbm, buf) in enumerate(((x_hbm, x_buf), (y1_hbm, y1_buf), (y2_hbm, y2_buf)))]

    @pl.when(s == 0)
    def _():
        for ahead in range(COMB_SLOTS - 1):
            for cp in fetch(ahead, ahead):
                cp.start()

    ahead = s + (COMB_SLOTS - 1)

    @pl.when(ahead < steps)
    def _():
        for cp in fetch(ahead, lax.rem(ahead, COMB_SLOTS)):
            cp.start()

    slot = lax.rem(s, COMB_SLOTS)
    for cp in fetch(s, slot):
        cp.wait()
    o_ref[...] = _moe_combine_norm(x_buf[slot], y1_buf[slot], y2_buf[slot], gates_ref[...], g_ref[...], b_ref[...])


def _combine_ln(xf, y1, y2, gates, ln_g, ln_b):
    T, D = xf.shape
    steps = T // COMB_T
    assert steps >= COMB_SLOTS
    const = lambda shape: pl.BlockSpec(shape, lambda i: (0,) * len(shape))
    hbm = pl.BlockSpec(memory_space=pl.ANY)
    return pl.pallas_call(
        functools.partial(_combine_kernel, steps),
        grid=(steps,),
        in_specs=[hbm, hbm, hbm, pl.BlockSpec((SUBLANES, COMB_T), lambda i: (0, i)), const((1, D)), const((1, D))],
        out_specs=pl.BlockSpec((COMB_T, D), lambda i: (i, 0)),
        out_shape=jax.ShapeDtypeStruct((T, D), F32),
        scratch_shapes=[
            pltpu.VMEM((COMB_SLOTS, COMB_T, D), F32),
            pltpu.VMEM((COMB_SLOTS, COMB_T, D_HALF), U32),
            pltpu.VMEM((COMB_SLOTS, COMB_T, D_HALF), U32),
            pltpu.SemaphoreType.DMA((3, COMB_SLOTS)),
        ],
        compiler_params=pltpu.CompilerParams(
            dimension_semantics=("arbitrary",), vmem_limit_bytes=VMEM_LIMIT),
        name="moe_combine_ln",
    )(xf, y1, y2, gates, ln_g.reshape(1, D), ln_b.reshape(1, D))


def _hier_moe(xpk, logits_t, w1, w3, w2, layer, gather_parts):
    B, S, _ = xpk.shape
    T = B * S
    idx, gates, cnt = _router(logits_t)
    counts = cnt[:, 0]
    padded = ((counts + MOE_BM - 1) // MOE_BM) * MOE_BM
    pends = jnp.cumsum(padded)
    pstarts = pends - padded
    experts = jnp.arange(N_EXPERTS, dtype=jnp.int32)[:, None]

    def dest(e_row, rank_row):
        return jnp.sum(jnp.where(e_row[None, :] == experts, pstarts[:, None], 0), axis=0) + rank_row

    n_win = T // (SC_WORKERS * SC_WIN)
    dest1 = dest(idx[0], idx[2]).reshape(SC_WORKERS, n_win, SC_WIN)
    dest2 = dest(idx[1], idx[3]).reshape(SC_WORKERS, n_win, SC_WIN)
    n_blocks = -(-(T * TOP_K + N_EXPERTS * (MOE_BM - 1)) // MOE_BM)
    n_rows = n_blocks * MOE_BM
    block_start = jnp.arange(n_blocks, dtype=jnp.int32) * MOE_BM
    block_e = jnp.minimum(jnp.sum(block_start[:, None] >= pends[None, :], axis=1), N_EXPERTS - 1).astype(jnp.int32)
    of_block = block_e[:, None] == experts[:, 0][None, :]
    pick = lambda v: jnp.sum(jnp.where(of_block, v[None, :], 0), axis=1)
    valid = jnp.clip(pick(counts) - (block_start - pick(pstarts)), 0, MOE_BM).astype(jnp.int32)
    xbuf = _sc_dispatch(xpk.reshape(T, D_HALF), dest1, dest2, n_rows)
    ybuf = _moe_blocks(xbuf, block_e, valid, w1, w3, w2, layer)
    part_shape = (gather_parts, SC_WORKERS, n_win // gather_parts, SC_WIN)
    d1, d2 = dest1.reshape(part_shape), dest2.reshape(part_shape)
    y_parts = [_sc_gather_pair(ybuf, d1[p], d2[p]) for p in range(gather_parts)]
    return y_parts, gates


def _att_head_order():
    order = []
    for p in range(N_HEADS // 2):
        jj, m = divmod(p, 4)
        order += [8 * jj + m, 8 * jj + 4 + m]
    return order


ATT_HEAD_ORDER = _att_head_order()


def _attn_kernel(xprev_ref, y1_ref, y2_ref, gates_a_ref, gates_b_ref, g_prev_ref, b_prev_ref, wqkv_ref, bias_ref,
                 sink_ref, wo_ref, g_ref, b_ref, wrt_ref, brt_ref, o_ref, opk_ref, lg_ref,
                 kv_ext, o_sc, s_sc0, s_sc1, p_sc0, p_sc1):
    s = pl.program_id(1)
    tq = ATT_TQ
    s_bufs = (s_sc0, s_sc1)
    p_bufs = (p_sc0, p_sc1)
    gate_refs = (gates_a_ref, gates_b_ref)
    assert ATT_NSEQ == len(gate_refs)

    @pl.when(s == 0)
    def _():
        kv_ext[:, 0:WINDOW, :] = jnp.zeros((ATT_NSEQ, WINDOW, 2 * KV_DIM), BF16)

    def layer_input(sq):
        return _moe_combine_norm(xprev_ref[sq], y1_ref[sq], y2_ref[sq], gate_refs[sq][...],
                                 g_prev_ref[...], b_prev_ref[...])

    def project_qkv(sq, x):
        qkv = jnp.dot(x.astype(BF16), wqkv_ref[...], preferred_element_type=F32)
        kv_ext[sq, WINDOW:WINDOW + tq, :] = qkv[:, Q_DIM:].astype(BF16)
        return (qkv[:, :Q_DIM] * (HEAD_DIM ** -0.5 * LOG2E)).astype(BF16)

    lane = lax.broadcasted_iota(jnp.int32, (WINDOW, LANES), 1)
    low = lane < HEAD_DIM
    sub = lax.broadcasted_iota(jnp.int32, (LANES, WINDOW), 0)
    top = sub < HEAD_DIM
    first = jnp.where(s == 0, 1, 0)
    nt = (((1,), (1,)), ((), ()))
    zero = jnp.zeros((), BF16)

    tiles = [(sq, n, j) for n in range(ATT_NB) for j in range(2) for sq in range(ATT_NSEQ)]

    def scores(t, qs_all):
        sq, n, j = tiles[t]
        q = qs_all[sq]
        r0 = n * WINDOW
        k_tile = kv_ext[sq, r0:r0 + 2 * WINDOW, j * LANES:(j + 1) * LANES]
        parts = []
        for m in range(4):
            p = 4 * j + m
            qt = q[r0:r0 + WINDOW, p * LANES:(p + 1) * LANES]
            parts.append(jnp.where(low, qt, zero))
            parts.append(jnp.where(low, zero, qt))
        qs = jnp.concatenate(parts, axis=0)
        bias_sel = first if n == 0 else 0
        s_bufs[t % 2][...] = (lax.dot_general(k_tile, qs, nt, preferred_element_type=F32)
                              + bias_ref[bias_sel, j])

    def softmax_pv(t):
        sq, n, j = tiles[t]
        r0 = n * WINDOW
        s_sc = s_bufs[t % 2]
        p_sc = p_bufs[t % 2]
        inv_l = []
        for h in range(8):
            hc = slice(h * WINDOW, (h + 1) * WINDOW)
            sink = sink_ref[8 * j + h] * LOG2E
            mx = jnp.maximum(jnp.max(s_sc[:, hc], axis=0, keepdims=True), sink)
            pr = jnp.exp2(s_sc[:, hc] - mx)
            p_sc[:, hc] = pr.astype(BF16)
            inv_l.append(1.0 / (jnp.sum(pr, axis=0, keepdims=True) + jnp.exp2(sink - mx)))
        v_tile = kv_ext[sq, r0:r0 + 2 * WINDOW, KV_DIM + j * LANES:KV_DIM + (j + 1) * LANES]
        tn = (((0,), (0,)), ((), ()))
        ov = lax.dot_general(v_tile, p_sc[...], tn, preferred_element_type=F32)
        for m in range(4):
            p = 4 * j + m
            o_even = ov[:, (2 * m) * WINDOW:(2 * m + 1) * WINDOW] * inv_l[2 * m]
            o_odd = ov[:, (2 * m + 1) * WINDOW:(2 * m + 2) * WINDOW] * inv_l[2 * m + 1]
            o_sc[sq, p * LANES:(p + 1) * LANES, r0:r0 + WINDOW] = jnp.where(top, o_even, o_odd).astype(BF16)

    def project_out(sq):
        tn = (((0,), (0,)), ((), ()))
        return lax.dot_general(o_sc[sq], wo_ref[...], tn, preferred_element_type=F32)

    def finish(sq, x, out):
        xn = _layer_norm_rows(ALPHA * x + out, g_ref[...], b_ref[...])
        o_ref[sq] = xn
        opk_ref[sq] = _pack_row_halves(xn)
        lg_ref[sq] = _router_logits_t(xn, wrt_ref[...], brt_ref[...])

    xs = [layer_input(0), layer_input(1)]
    qs_all = [project_qkv(0, xs[0]), project_qkv(1, xs[1])]
    scores(0, qs_all)
    for t in range(len(tiles)):
        if t + 1 < len(tiles):
            scores(t + 1, qs_all)
        softmax_pv(t)
        if tiles[t][0] == 0 and all(sq != 0 for sq, _, _ in tiles[t + 1:]):
            out_first = project_out(0)
    kv_ext[:, 0:WINDOW, :] = kv_ext[:, tq:tq + WINDOW, :]
    out_second = project_out(1)
    finish(0, xs[0], out_first)
    finish(1, xs[1], out_second)


def _attn_bias():
    qi = np.arange(WINDOW)[:, None]
    sj = np.arange(2 * WINDOW)[None, :]
    dist = qi - sj + WINDOW
    valid = (dist >= 0) & (dist < WINDOW)
    slopes = 2.0 ** (-8.0 * np.arange(1, N_HEADS + 1, dtype=np.float32) / N_HEADS)
    slopes = slopes.astype(np.float32)[ATT_HEAD_ORDER]
    sb = -(slopes[:, None, None] * dist.astype(np.float32)[None])
    later = np.where(valid[None], sb, -np.inf)
    first = np.where((valid & (sj >= WINDOW))[None], sb, -np.inf)
    bias = np.stack([later, first]).astype(np.float32) * np.float32(LOG2E)
    bias = bias.reshape(2, 2, 8, WINDOW, 2 * WINDOW).transpose(0, 1, 4, 2, 3).reshape(2, 2, 2 * WINDOW, 8 * WINDOW)
    return jnp.asarray(np.ascontiguousarray(bias))


ATT_N_INPUTS = 15


def _attn_part_kernel(*refs):
    _attn_kernel(*refs[:ATT_N_INPUTS], *refs[ATT_N_INPUTS + 3:])


def _attn_layer(x_prev, y_parts, gates, g_prev, b_prev, w_qkv, sinks, w_o, ln_g, ln_b, router_w, router_b):
    B, S, D = x_prev.shape
    steps = S // ATT_TQ
    assert ATT_HEAD_ORDER == list(np.arange(N_HEADS).reshape(2, 2, 4).transpose(0, 2, 1).reshape(-1))
    wq = w_qkv[:, :Q_DIM].reshape(D, 2, 2, 4, HEAD_DIM).transpose(0, 1, 3, 2, 4).reshape(D, Q_DIM)
    wqkv = jnp.concatenate([wq, w_qkv[:, Q_DIM:]], axis=1).astype(BF16)
    wo_t = w_o.reshape(2, 2, 4, HEAD_DIM, D).transpose(0, 2, 1, 3, 4).reshape(Q_DIM, D).astype(BF16)
    sink = sinks.reshape(2, 2, 4).transpose(0, 2, 1).reshape(N_HEADS, 1, 1)
    bias = _attn_bias()
    const = lambda shape: pl.BlockSpec(shape, lambda b, s: (0,) * len(shape))
    n_parts = len(y_parts)
    rows_per_part = B // n_parts
    pairs = rows_per_part // ATT_NSEQ
    outs = None
    for part, (y1, y2) in enumerate(y_parts):
        off = part * pairs
        glob = lambda w, off=off: pl.BlockSpec((ATT_NSEQ, ATT_TQ, w), lambda b, s: (b + off, s, 0))
        local = lambda w: pl.BlockSpec((ATT_NSEQ, ATT_TQ, w), lambda b, s: (b, s, 0))
        gate_rows = lambda sq, off=off: pl.BlockSpec(
            (SUBLANES, ATT_TQ), lambda b, s: (0, (ATT_NSEQ * (b + off) + sq) * steps + s))
        in_specs = [
            glob(D),
            local(D_HALF),
            local(D_HALF),
            gate_rows(0),
            gate_rows(1),
            const((1, D)),
            const((1, D)),
            const((D, Q_DIM + 2 * KV_DIM)),
            const((2, 2, 2 * WINDOW, 8 * WINDOW)),
            const((N_HEADS, 1, 1)),
            const((Q_DIM, D)),
            const((1, D)),
            const((1, D)),
            const((2 * ROUTE_ROWS, D)),
            const((ROUTE_ROWS, 1)),
        ]
        args = [x_prev, y1.reshape(rows_per_part, S, D_HALF), y2.reshape(rows_per_part, S, D_HALF), gates, gates,
                g_prev.reshape(1, D), b_prev.reshape(1, D), wqkv, bias, sink, wo_t,
                ln_g.reshape(1, D), ln_b.reshape(1, D), router_w, router_b]
        aliases = {}
        body = _attn_kernel
        if outs is not None:
            aliases = {len(args) + k: k for k in range(3)}
            in_specs += [pl.BlockSpec(memory_space=pl.ANY)] * 3
            args += list(outs)
            body = _attn_part_kernel
        outs = pl.pallas_call(
            body,
            grid=(pairs, steps),
            in_specs=in_specs,
            out_specs=[glob(D), glob(D_HALF),
                       pl.BlockSpec((ATT_NSEQ, ROUTE_ROWS, ATT_TQ), lambda b, s, off=off: (b + off, 0, s))],
            out_shape=[jax.ShapeDtypeStruct((B, S, D), F32), jax.ShapeDtypeStruct((B, S, D_HALF), U32),
                       jax.ShapeDtypeStruct((B, ROUTE_ROWS, S), F32)],
            input_output_aliases=aliases,
            scratch_shapes=[
                pltpu.VMEM((ATT_NSEQ, ATT_TQ + WINDOW, 2 * KV_DIM), BF16),
                pltpu.VMEM((ATT_NSEQ, Q_DIM, ATT_TQ), BF16),
                pltpu.VMEM((2 * WINDOW, 8 * WINDOW), F32),
                pltpu.VMEM((2 * WINDOW, 8 * WINDOW), F32),
                pltpu.VMEM((2 * WINDOW, 8 * WINDOW), BF16),
                pltpu.VMEM((2 * WINDOW, 8 * WINDOW), BF16),
            ],
            compiler_params=pltpu.CompilerParams(
                dimension_semantics=("arbitrary", "arbitrary"), vmem_limit_bytes=VMEM_LIMIT),
            name="swa_attn_ln",
        )(*args)
    return outs


def kernel(x, rec_w_in, rec_conv_w, rec_conv_b, rec_w_r, rec_b_r, rec_w_i, rec_b_i, rec_lambda, rec_w_out,
           att_w_qkv, att_sinks, att_w_o, moe_w_group, moe_b_group, moe_w_expert, moe_b_expert,
           moe_w1, moe_w3, moe_w2, ln_g, ln_b):
    assert DEPTH == 2
    B, S, D = x.shape

    router = [_router_weights(moe_w_group[layer], moe_b_group[layer], moe_w_expert[layer], moe_b_expert[layer])
              for layer in range(DEPTH)]

    x1, x1_pk, logits1 = _rglru_layer(x, rec_w_in[0], rec_conv_w[0], rec_conv_b[0], rec_w_r[0], rec_b_r[0],
                                      rec_w_i[0], rec_b_i[0], rec_lambda[0], rec_w_out[0], ln_g[0, 0], ln_b[0, 0],
                                      *router[0])
    y_parts, gates = _hier_moe(x1_pk, logits1, moe_w1, moe_w3, moe_w2, 0, gather_parts=ATT_PARTS)
    x3, x3_pk, logits3 = _attn_layer(x1, y_parts, gates, ln_g[0, 1], ln_b[0, 1], att_w_qkv[0], att_sinks[0],
                                     att_w_o[0], ln_g[1, 0], ln_b[1, 0], *router[1])
    ((y1, y2),), gates = _hier_moe(x3_pk, logits3, moe_w1, moe_w3, moe_w2, 1, gather_parts=1)
    out = _combine_ln(x3.reshape(B * S, D), y1, y2, gates, ln_g[1, 1], ln_b[1, 1])
    return out.reshape(B, S, D)
```

```python
import functools

import jax
import jax.numpy as jnp
import numpy as np
from jax import lax
from jax.experimental import pallas as pl
from jax.experimental.pallas import tpu as pltpu
from jax.experimental.pallas import tpu_sc as plsc

F32 = jnp.float32
BF16 = jnp.bfloat16
U32 = jnp.uint32

D_MODEL = 1024
DEPTH = 2
D_RNN = 1280
LRU_BLOCKS = 16
LRU_BLOCK_W = D_RNN // LRU_BLOCKS
CONV_W = 4
LRU_C = 8.0
N_HEADS = 16
N_KV_HEADS = 4
HEAD_DIM = 64
WINDOW = 128
Q_DIM = N_HEADS * HEAD_DIM
KV_DIM = N_KV_HEADS * HEAD_DIM
N_GROUPS = 4
EXPERTS_PER_GROUP = 8
N_EXPERTS = N_GROUPS * EXPERTS_PER_GROUP
TOP_K = 2
D_EXPERT = 512
ALPHA = (2 * DEPTH) ** 0.25
LN_EPS = 1e-5
LOG2E = 1.4426950408889634

LANES = 128
SUBLANES = 8
VMEM_LIMIT = 56 * 1024 * 1024

REC_TS = 256
REC_GROUPS = REC_TS // SUBLANES
REC_NSEQ = 2
GATE_TILE = 256
GATE_WIN = 512
GATE_WIN_STARTS = (0, 128, 384, 640, 768)
N_GATE_TILES = D_RNN // GATE_TILE

ROUTE_T = 512
ROUTE_STEP = 2048
ROUTE_ROWS = 40

MOE_BM = 1024
MOE_SUB = 256

ATT_TQ = 256
ATT_NB = ATT_TQ // WINDOW
ATT_NSEQ = 2
ATT_PARTS = 4

COMB_T = 1024
COMB_SLOTS = 3

D_HALF = D_MODEL // 2

SC_CORES = 2
SC_SUBCORES = 16
SC_WORKERS = SC_CORES * SC_SUBCORES
SC_WIN = 64


def _layer_norm_rows(z, g, b):
    mu = jnp.mean(z, axis=-1, keepdims=True)
    zc = z - mu
    var = jnp.mean(zc * zc, axis=-1, keepdims=True)
    return zc * lax.rsqrt(var + LN_EPS) * g + b


def _pack_bf16_pair(a, b):
    ua = lax.bitcast_convert_type(a.astype(BF16).astype(F32), U32)
    ub = lax.bitcast_convert_type(b.astype(BF16).astype(F32), U32)
    return (ua >> 16) | (ub & jnp.uint32(0xFFFF0000))


def _unpack_bf16_pair(w):
    a = lax.bitcast_convert_type(w << 16, F32)
    b = lax.bitcast_convert_type(w & jnp.uint32(0xFFFF0000), F32)
    return a, b


def _pack_row_halves(x):
    return _pack_bf16_pair(x[:, :D_HALF], x[:, D_HALF:])


def _router_logits_t(x, w_split, bias):
    xhi = x.astype(BF16)
    xlo = (x - xhi.astype(F32)).astype(BF16)
    nt = (((1,), (1,)), ((), ()))
    both = lax.dot_general(w_split, xhi, nt, preferred_element_type=F32)
    low = lax.dot_general(w_split[:ROUTE_ROWS], xlo, nt, preferred_element_type=F32)
    return both[:ROUTE_ROWS] + both[ROUTE_ROWS:] + low + bias


def _rglru_kernel(x_ref, perm_ref, perm_t_ref, w_in_ref, convw_ref, convb_ref, wg_ref, br_ref, bi_ref, lam_ref,
                  w_out_ref, g_ref, b_ref, wrt_ref, brt_ref, o_ref, opk_ref, lg_ref,
                  xr_ext, tail_sc, a_sc, u_sc, h_carry):
    s = pl.program_id(1)
    ts = REC_TS
    halo = (CONV_W - 1) * SUBLANES

    @pl.when(s == 0)
    def _():
        tail_sc[...] = jnp.zeros((REC_NSEQ, halo, D_RNN), F32)
        h_carry[...] = jnp.zeros((REC_NSEQ, 1, D_RNN), F32)

    row = lax.broadcasted_iota(jnp.int32, (SUBLANES, D_RNN), 0)
    nlam = -lam_ref[...]
    sp = jnp.maximum(nlam, 0.0) + jnp.log1p(jnp.exp(-jnp.abs(nlam)))
    log2a_scale = (-LRU_C * LOG2E) * sp

    def project(q):
        xp = jnp.dot(perm_ref[...], x_ref[q].astype(BF16), preferred_element_type=F32).astype(BF16)
        proj = jnp.dot(xp, w_in_ref[...], preferred_element_type=F32)
        return proj[:, :D_RNN], proj[:, D_RNN:]

    def conv_gates(q, xr):
        for k in range(CONV_W - 1):
            r0 = ts - halo + k * SUBLANES
            cur = xr[r0:r0 + SUBLANES, :]
            prev = tail_sc[q, k * SUBLANES:(k + 1) * SUBLANES, :]
            xr_ext[q, k * SUBLANES:(k + 1) * SUBLANES, :] = jnp.where(
                row == 0, pltpu.roll(prev, 1, axis=0), pltpu.roll(cur, 1, axis=0))
        tail_sc[q] = xr[ts - halo:, :]
        xr_ext[q, halo:halo + ts, :] = xr
        xc = convb_ref[...] + convw_ref[CONV_W - 1:CONV_W, :] * xr
        for k in range(CONV_W - 1):
            xc = xc + convw_ref[k:k + 1, :] * xr_ext[q, k * SUBLANES:k * SUBLANES + ts, :]
        xcb = xc.astype(BF16)
        pres = [jnp.dot(xcb[:, GATE_WIN_STARTS[j]:GATE_WIN_STARTS[j] + GATE_WIN], wg_ref[j],
                        preferred_element_type=F32) for j in range(N_GATE_TILES)]
        for j, pre in enumerate(pres):
            cs = j * GATE_TILE
            r = jax.nn.sigmoid(pre[:, :GATE_TILE] + br_ref[:, cs:cs + GATE_TILE])
            i = jax.nn.sigmoid(pre[:, GATE_TILE:] + bi_ref[:, cs:cs + GATE_TILE])
            a = jnp.exp2(r * log2a_scale[:, cs:cs + GATE_TILE])
            s1 = 1.0 - a * a
            mult = jnp.where(s1 > 0.0, s1 * lax.rsqrt(s1), 0.0)
            u = mult * (i * xc[:, cs:cs + GATE_TILE])
            a_sc[q, :, cs:cs + GATE_TILE] = a
            u_sc[q, :, cs:cs + GATE_TILE] = u

    def segment_scan(q):
        h = jnp.zeros((SUBLANES, D_RNN), F32)
        prod = jnp.ones((SUBLANES, D_RNN), F32)
        for gidx in range(REC_GROUPS):
            rows = slice(gidx * SUBLANES, (gidx + 1) * SUBLANES)
            a8 = a_sc[q, rows, :]
            h = a8 * h + u_sc[q, rows, :]
            prod = a8 * prod
            u_sc[q, rows, :] = h
            a_sc[q, rows, :] = prod
        return h, prod

    def recur_out(q, gate, seg):
        seg_h, seg_a = seg
        for d in (1, 2, 4):
            keep = row >= d
            a_sh = jnp.where(keep, pltpu.roll(seg_a, d, axis=0), 1.0)
            h_sh = jnp.where(keep, pltpu.roll(seg_h, d, axis=0), 0.0)
            seg_h = seg_a * h_sh + seg_h
            seg_a = seg_a * a_sh
        h_in = h_carry[q]
        after = seg_a * h_in + seg_h
        enter = jnp.where(row == 0, h_in, pltpu.roll(after, 1, axis=0))
        h_carry[q] = after[SUBLANES - 1:SUBLANES, :]
        hs = (u_sc[q].reshape(REC_GROUPS, SUBLANES, D_RNN)
              + a_sc[q].reshape(REC_GROUPS, SUBLANES, D_RNN) * enter[None]).reshape(ts, D_RNN)
        y = hs * jax.nn.gelu(gate)
        y_t = jnp.dot(perm_t_ref[...], y.astype(BF16), preferred_element_type=F32).astype(BF16)
        return jnp.dot(y_t, w_out_ref[...], preferred_element_type=F32)

    def finish(q, out):
        z = ALPHA * x_ref[q] + out
        xn = _layer_norm_rows(z, g_ref[...], b_ref[...])
        o_ref[q] = xn
        opk_ref[q] = _pack_row_halves(xn)
        lg_ref[q] = _router_logits_t(xn, wrt_ref[...], brt_ref[...])

    assert REC_NSEQ == 2
    gate_a, xr_a = project(0)
    conv_gates(0, xr_a)
    gate_b, xr_b = project(1)
    conv_gates(1, xr_b)
    seg_a = segment_scan(0)
    seg_b = segment_scan(1)
    out_a = recur_out(0, gate_a, seg_a)
    out_b = recur_out(1, gate_b, seg_b)
    finish(0, out_a)
    finish(1, out_b)


def _band_gate_weights(w_r, w_i):
    spread = jnp.asarray(np.tile(np.eye(LRU_BLOCK_W, dtype=np.float32), (1, LRU_BLOCKS)), BF16)
    blk = np.arange(D_RNN) // LRU_BLOCK_W
    on_diag = jnp.asarray(blk[:, None] == blk[None, :])

    def dense(w):
        rows = w.reshape(D_RNN, LRU_BLOCK_W).astype(BF16)
        return jnp.where(on_diag, jnp.dot(rows, spread, preferred_element_type=F32), 0.0)

    wr, wi = dense(w_r), dense(w_i)
    tiles = []
    for j in range(N_GATE_TILES):
        ws = GATE_WIN_STARTS[j]
        cs = j * GATE_TILE
        lo_blk = cs // LRU_BLOCK_W
        hi_blk = (cs + GATE_TILE - 1) // LRU_BLOCK_W
        assert ws <= lo_blk * LRU_BLOCK_W and (hi_blk + 1) * LRU_BLOCK_W <= ws + GATE_WIN
        tiles.append(jnp.concatenate([wr[ws:ws + GATE_WIN, cs:cs + GATE_TILE],
                                      wi[ws:ws + GATE_WIN, cs:cs + GATE_TILE]], axis=1))
    return jnp.stack(tiles).astype(BF16)


def _rglru_layer(x, w_in, conv_w, conv_b, w_r, b_r, w_i, b_i, lam, w_out, ln_g, ln_b, router_w, router_b):
    B, S, D = x.shape
    wg = _band_gate_weights(w_r, w_i)
    rho = np.arange(REC_TS)
    perm_np = np.zeros((REC_TS, REC_TS), np.float32)
    perm_np[rho, (rho % SUBLANES) * REC_GROUPS + rho // SUBLANES] = 1.0
    perm = jnp.asarray(perm_np, BF16)
    perm_t = jnp.asarray(perm_np.T, BF16)
    row = lambda v: v.reshape(1, -1)
    const = lambda shape: pl.BlockSpec(shape, lambda b, s: (0,) * len(shape))
    tile = lambda w: pl.BlockSpec((REC_NSEQ, REC_TS, w), lambda b, s: (b, s, 0))
    halo = (CONV_W - 1) * SUBLANES
    return pl.pallas_call(
        _rglru_kernel,
        grid=(B // REC_NSEQ, S // REC_TS),
        in_specs=[
            tile(D),
            const((REC_TS, REC_TS)),
            const((REC_TS, REC_TS)),
            const((D, 2 * D_RNN)),
            const((CONV_W, D_RNN)),
            const((1, D_RNN)),
            const((N_GATE_TILES, GATE_WIN, 2 * GATE_TILE)),
            const((1, D_RNN)),
            const((1, D_RNN)),
            const((1, D_RNN)),
            const((D_RNN, D)),
            const((1, D)),
            const((1, D)),
            const((2 * ROUTE_ROWS, D)),
            const((ROUTE_ROWS, 1)),
        ],
        out_specs=[tile(D), tile(D_HALF),
                   pl.BlockSpec((REC_NSEQ, ROUTE_ROWS, REC_TS), lambda b, s: (b, 0, s))],
        out_shape=[jax.ShapeDtypeStruct((B, S, D), F32), jax.ShapeDtypeStruct((B, S, D_HALF), U32),
                   jax.ShapeDtypeStruct((B, ROUTE_ROWS, S), F32)],
        scratch_shapes=[
            pltpu.VMEM((REC_NSEQ, halo + REC_TS, D_RNN), F32),
            pltpu.VMEM((REC_NSEQ, halo, D_RNN), F32),
            pltpu.VMEM((REC_NSEQ, REC_TS, D_RNN), F32),
            pltpu.VMEM((REC_NSEQ, REC_TS, D_RNN), F32),
            pltpu.VMEM((REC_NSEQ, 1, D_RNN), F32),
        ],
        compiler_params=pltpu.CompilerParams(
            dimension_semantics=("arbitrary", "arbitrary"), vmem_limit_bytes=VMEM_LIMIT),
        name="rglru_ln",
    )(x, perm, perm_t, w_in.astype(BF16), conv_w, row(conv_b), wg, row(b_r), row(b_i), row(lam), w_out.astype(BF16),
      row(ln_g), row(ln_b), router_w, router_b)


def _router_kernel(logits_ref, tri_ref, idx_ref, gate_ref, cnt_ref, base_sc):
    step = pl.program_id(0)
    tr = ROUTE_T

    @pl.when(step == 0)
    def _():
        base_sc[...] = jnp.zeros((N_EXPERTS, 1), F32)

    row8 = lax.broadcasted_iota(jnp.int32, (SUBLANES, tr), 0).astype(F32)
    rowe = lax.broadcasted_iota(jnp.int32, (N_EXPERTS, tr), 0).astype(F32)
    neg_inf = -jnp.inf
    for sub in range(ROUTE_STEP // ROUTE_T):
        cols = slice(sub * tr, (sub + 1) * tr)
        _route_tile(logits_ref[0, :, cols], row8, rowe, neg_inf, tri_ref, idx_ref, gate_ref, base_sc, cols)
    cnt_ref[...] = jnp.broadcast_to(base_sc[...], (N_EXPERTS, LANES)).astype(jnp.int32)


def _route_tile(logits, row8, rowe, neg_inf, tri_ref, idx_ref, gate_ref, base_sc, cols):
    tr = ROUTE_T
    g = jnp.where(row8 < N_GROUPS, logits[N_EXPERTS:N_EXPERTS + SUBLANES, :], neg_inf)
    gmax = jnp.max(g, axis=0, keepdims=True)
    gidx = jnp.min(jnp.where(g == gmax, row8, SUBLANES), axis=0, keepdims=True)
    g_gate = 1.0 / jnp.sum(jnp.exp(g - gmax), axis=0, keepdims=True)

    esel = logits[0:EXPERTS_PER_GROUP, :]
    for grp in range(1, N_GROUPS):
        esel = jnp.where(gidx == grp, logits[grp * EXPERTS_PER_GROUP:(grp + 1) * EXPERTS_PER_GROUP, :], esel)
    v1 = jnp.max(esel, axis=0, keepdims=True)
    i1 = jnp.min(jnp.where(esel == v1, row8, SUBLANES), axis=0, keepdims=True)
    esel2 = jnp.where(row8 == i1, neg_inf, esel)
    v2 = jnp.max(esel2, axis=0, keepdims=True)
    i2 = jnp.min(jnp.where(esel2 == v2, row8, SUBLANES), axis=0, keepdims=True)
    e21 = jnp.exp(v2 - v1)
    inv = 1.0 / (1.0 + e21)
    gate1 = inv * g_gate
    gate2 = e21 * inv * g_gate
    e1 = gidx * EXPERTS_PER_GROUP + i1
    e2 = gidx * EXPERTS_PER_GROUP + i2

    hit1 = rowe == e1
    hit2 = rowe == e2
    member = jnp.where(hit1, 1.0, jnp.where(hit2, 1.0, 0.0))
    before = jnp.dot(member.astype(BF16), tri_ref[...], preferred_element_type=F32) + base_sc[...]
    rank1 = jnp.sum(jnp.where(hit1, before, 0.0), axis=0, keepdims=True)
    rank2 = jnp.sum(jnp.where(hit2, before, 0.0), axis=0, keepdims=True)
    base_sc[...] = base_sc[...] + jnp.sum(member, axis=1, keepdims=True)

    zi = jnp.zeros((1, tr), jnp.int32)
    idx_ref[:, cols] = jnp.concatenate(
        [e1.astype(jnp.int32), e2.astype(jnp.int32), rank1.astype(jnp.int32), rank2.astype(jnp.int32),
         zi, zi, zi, zi], axis=0)
    zf = jnp.zeros((1, tr), F32)
    gate_ref[:, cols] = jnp.concatenate([gate1, gate2, zf, zf, zf, zf, zf, zf], axis=0)


def _router_weights(w_rg, b_rg, w_re, b_re):
    D = w_rg.shape[0]
    pad_rows = ROUTE_ROWS - N_EXPERTS - N_GROUPS
    w = jnp.concatenate([w_re.T, w_rg.T, jnp.zeros((pad_rows, D), F32)], axis=0)
    whi = w.astype(BF16)
    wlo = (w - whi.astype(F32)).astype(BF16)
    w_split = jnp.concatenate([whi, wlo], axis=0)
    bias = jnp.concatenate([b_re, b_rg, jnp.zeros((pad_rows,), F32)]).reshape(ROUTE_ROWS, 1)
    return w_split, bias


def _router(logits_t):
    B, _, S = logits_t.shape
    T = B * S
    per_row = S // ROUTE_STEP
    tri = jnp.asarray(np.triu(np.ones((ROUTE_T, ROUTE_T), np.float32), 1), BF16)
    const = lambda shape: pl.BlockSpec(shape, lambda i: (0,) * len(shape))
    return pl.pallas_call(
        _router_kernel,
        grid=(T // ROUTE_STEP,),
        in_specs=[
            pl.BlockSpec((1, ROUTE_ROWS, ROUTE_STEP), lambda i: (i // per_row, 0, i % per_row)),
            const((ROUTE_T, ROUTE_T)),
        ],
        out_specs=[
            pl.BlockSpec((SUBLANES, ROUTE_STEP), lambda i: (0, i)),
            pl.BlockSpec((SUBLANES, ROUTE_STEP), lambda i: (0, i)),
            const((N_EXPERTS, LANES)),
        ],
        out_shape=[
            jax.ShapeDtypeStruct((SUBLANES, T), jnp.int32),
            jax.ShapeDtypeStruct((SUBLANES, T), F32),
            jax.ShapeDtypeStruct((N_EXPERTS, LANES), jnp.int32),
        ],
        scratch_shapes=[pltpu.VMEM((N_EXPERTS, 1), F32)],
        compiler_params=pltpu.CompilerParams(
            dimension_semantics=("arbitrary",), vmem_limit_bytes=VMEM_LIMIT),
        name="router",
    )(logits_t, tri)


def _sc_mesh():
    return plsc.VectorSubcoreMesh(core_axis_name="c", subcore_axis_name="s",
                                  num_cores=SC_CORES, num_subcores=SC_SUBCORES)


def _sc_worker_id():
    return lax.axis_index("s") * SC_CORES + lax.axis_index("c")


def _sc_scratch(n_win, width):
    return [
        pltpu.VMEM((n_win, SC_WIN), jnp.int32),
        pltpu.VMEM((n_win, SC_WIN), jnp.int32),
        pltpu.VMEM((2, SC_WIN, width), U32),
        pltpu.SemaphoreType.DMA((2,)),
        pltpu.SemaphoreType.DMA((2,)),
    ]


def _sc_dispatch(rows, idx1, idx2, n_rows):
    _, width = rows.shape
    _, n_win, _ = idx1.shape

    @functools.partial(
        pl.kernel, mesh=_sc_mesh(), out_type=jax.ShapeDtypeStruct((n_rows, width), rows.dtype),
        scratch_types=_sc_scratch(n_win, width), name="sc_dispatch")
    def run(rows_hbm, i1_hbm, i2_hbm, o_hbm, i1_v, i2_v, buf, rsem, wsem):
        wid = _sc_worker_id()
        base = wid * (n_win * SC_WIN)
        pltpu.sync_copy(i1_hbm.at[wid], i1_v)
        pltpu.sync_copy(i2_hbm.at[wid], i2_v)

        def read(j):
            return pltpu.async_copy(rows_hbm.at[pl.ds(base + j * SC_WIN, SC_WIN)], buf.at[j % 2], rsem.at[j % 2])

        reads = {0: read(0)}
        writes = {}
        for j in range(n_win):
            if j + 1 < n_win:
                for d in writes.pop(j - 1, ()):
                    d.wait()
                reads[j + 1] = read(j + 1)
            reads.pop(j).wait()
            writes[j] = (pltpu.async_copy(buf.at[j % 2], o_hbm.at[i1_v.at[j]], wsem.at[j % 2]),
                         pltpu.async_copy(buf.at[j % 2], o_hbm.at[i2_v.at[j]], wsem.at[j % 2]))
        for j in sorted(writes):
            for d in writes[j]:
                d.wait()

    return run(rows, idx1, idx2)


def _sc_gather_pair(table, idx1, idx2):
    _, width = table.shape
    _, n_win, _ = idx1.shape
    n_tok = SC_WORKERS * n_win * SC_WIN
    out_t = jax.ShapeDtypeStruct((n_tok, width), table.dtype)

    @functools.partial(
        pl.kernel, mesh=_sc_mesh(), out_type=(out_t, out_t),
        scratch_types=_sc_scratch(n_win, width), name="sc_combine_gather")
    def run(table_hbm, i1_hbm, i2_hbm, o1_hbm, o2_hbm, i1_v, i2_v, buf, gsem, wsem):
        wid = _sc_worker_id()
        base = wid * (n_win * SC_WIN)
        pltpu.sync_copy(i1_hbm.at[wid], i1_v)
        pltpu.sync_copy(i2_hbm.at[wid], i2_v)
        work = [(i1_v, o1_hbm, j) for j in range(n_win)] + [(i2_v, o2_hbm, j) for j in range(n_win)]

        def gather(t):
            iv, _, j = work[t]
            return pltpu.async_copy(table_hbm.at[iv.at[j]], buf.at[t % 2], gsem.at[t % 2])

        def put(t):
            _, oh, j = work[t]
            return pltpu.async_copy(buf.at[t % 2], oh.at[pl.ds(base + j * SC_WIN, SC_WIN)], wsem.at[t % 2])

        gathers = {0: gather(0)}
        puts = {}
        for t in range(len(work)):
            if t + 1 < len(work):
                if t - 1 in puts:
                    puts.pop(t - 1).wait()
                gathers[t + 1] = gather(t + 1)
            gathers.pop(t).wait()
            puts[t] = put(t)
        for t in sorted(puts):
            puts[t].wait()

    return run(table, idx1, idx2)


def _moe_kernel(layer, n_blocks, be_ref, valid_ref, x_ref, w1_hbm, w3_hbm, w2_hbm, o_ref,
                w1_st, w3_st, w2_st, w1_sc, w3_sc, w2_sc, sems, ordinal_sm):
    i = pl.program_id(0)
    expert = be_ref[i]
    new_expert = jnp.logical_or(i == 0, expert != be_ref[jnp.maximum(i - 1, 0)])

    def weight_copies(e, sl):
        return [pltpu.make_async_copy(hbm.at[layer, e], stage.at[sl], sems.at[k, sl])
                for k, (hbm, stage) in enumerate(((w1_hbm, w1_st), (w3_hbm, w3_st), (w2_hbm, w2_st)))]

    @pl.when(i == 0)
    def _():
        ordinal_sm[0] = -1
        for cp in weight_copies(expert, 0):
            cp.start()

    @pl.when(new_expert)
    def _():
        ordinal = ordinal_sm[0] + 1
        ordinal_sm[0] = ordinal
        slot = ordinal & 1
        for cp in weight_copies(expert, slot):
            cp.wait()
        w1_sc[...] = w1_st[slot].astype(BF16)
        w3_sc[...] = w3_st[slot].astype(BF16)
        w2_sc[...] = w2_st[slot].astype(BF16)
        nxt_pos = lax.while_loop(
            lambda p: jnp.logical_and(p < n_blocks, be_ref[jnp.minimum(p, n_blocks - 1)] == expert),
            lambda p: p + 1, i + 1)

        @pl.when(nxt_pos < n_blocks)
        def _():
            for cp in weight_copies(be_ref[jnp.minimum(nxt_pos, n_blocks - 1)], 1 - slot):
                cp.start(priority=1)

    def up(rows):
        xa, xb = _unpack_bf16_pair(x_ref[rows, :])
        xa = xa.astype(BF16)
        xb = xb.astype(BF16)
        h1 = (jnp.dot(xa, w1_sc[:D_HALF], preferred_element_type=F32)
              + jnp.dot(xb, w1_sc[D_HALF:], preferred_element_type=F32))
        h3 = (jnp.dot(xa, w3_sc[:D_HALF], preferred_element_type=F32)
              + jnp.dot(xb, w3_sc[D_HALF:], preferred_element_type=F32))
        return h1, h3

    def down(rows, h1, h3):
        hdn = (jax.nn.silu(h1) * h3).astype(BF16)
        y = jnp.dot(hdn, w2_sc[...], preferred_element_type=F32)
        o_ref[rows, :] = _pack_row_halves(y)

    n_sub = MOE_BM // MOE_SUB
    subs = [slice(k * MOE_SUB, (k + 1) * MOE_SUB) for k in range(n_sub)]
    valid = valid_ref[i]
    chains = (valid + (MOE_SUB - 1)) // MOE_SUB

    for live in range(n_sub + 1):
        @pl.when(chains == live)
        def _(live=live):
            ups = {}
            if live:
                ups[0] = up(subs[0])
            for k in range(live):
                if k + 1 < live:
                    ups[k + 1] = up(subs[k + 1])
                down(subs[k], *ups.pop(k))
            if live < n_sub:
                o_ref[live * MOE_SUB:, :] = jnp.zeros((MOE_BM - live * MOE_SUB, D_HALF), o_ref.dtype)


def _moe_blocks(xbuf, block_e, valid, w1, w3, w2, layer):
    n_rows, _ = xbuf.shape
    D = D_MODEL
    n_blocks = n_rows // MOE_BM
    rows = lambda i, be, nu: (i, 0)
    grid_spec = pltpu.PrefetchScalarGridSpec(
        num_scalar_prefetch=2,
        grid=(n_blocks,),
        in_specs=[
            pl.BlockSpec((MOE_BM, D_HALF), rows),
            pl.BlockSpec(memory_space=pl.ANY),
            pl.BlockSpec(memory_space=pl.ANY),
            pl.BlockSpec(memory_space=pl.ANY),
        ],
        out_specs=pl.BlockSpec((MOE_BM, D_HALF), rows),
        scratch_shapes=[
            pltpu.VMEM((2, D, D_EXPERT), F32),
            pltpu.VMEM((2, D, D_EXPERT), F32),
            pltpu.VMEM((2, D_EXPERT, D), F32),
            pltpu.VMEM((D, D_EXPERT), BF16),
            pltpu.VMEM((D, D_EXPERT), BF16),
            pltpu.VMEM((D_EXPERT, D), BF16),
            pltpu.SemaphoreType.DMA((3, 2)),
            pltpu.SMEM((1,), jnp.int32),
        ],
    )
    return pl.pallas_call(
        functools.partial(_moe_kernel, layer, n_blocks),
        grid_spec=grid_spec,
        out_shape=jax.ShapeDtypeStruct((n_rows, D_HALF), U32),
        compiler_params=pltpu.CompilerParams(
            dimension_semantics=("arbitrary",), vmem_limit_bytes=VMEM_LIMIT),
        name="moe_experts",
    )(block_e, valid, xbuf, w1, w3, w2)


def _moe_combine_norm(x, y1, y2, gate_rows, g, b):
    n = x.shape[0]
    pad = jnp.zeros((LANES - SUBLANES, n), F32)
    gates = jnp.concatenate([gate_rows, pad], axis=0).T
    g1 = gates[:, 0:1]
    g2 = gates[:, 1:2]
    a1, b1 = _unpack_bf16_pair(y1)
    a2, b2 = _unpack_bf16_pair(y2)
    f = jnp.concatenate([g1 * a1 + g2 * a2, g1 * b1 + g2 * b2], axis=1)
    return _layer_norm_rows(ALPHA * x + f, g, b)


def _combine_kernel(steps, x_hbm, y1_hbm, y2_hbm, gates_ref, g_ref, b_ref, o_ref, x_buf, y1_buf, y2_buf, sems):
    s = pl.program_id(0)

    def fetch(step, slot):
        rows = pl.ds(pl.multiple_of(step * COMB_T, COMB_T), COMB_T)
        return [pltpu.make_async_copy(hbm.at[rows], buf.at[slot], sems.at[k, slot])
                for k, (hbm, buf) in enumerate(((x_hbm, x_buf), (y1_hbm, y1_buf), (y2_hbm, y2_buf)))]

    @pl.when(s == 0)
    def _():
        for ahead in range(COMB_SLOTS - 1):
            for cp in fetch(ahead, ahead):
                cp.start()

    ahead = s + (COMB_SLOTS - 1)

    @pl.when(ahead < steps)
    def _():
        for cp in fetch(ahead, lax.rem(ahead, COMB_SLOTS)):
            cp.start()

    slot = lax.rem(s, COMB_SLOTS)
    for cp in fetch(s, slot):
        cp.wait()
    o_ref[...] = _moe_combine_norm(x_buf[slot], y1_buf[slot], y2_buf[slot], gates_ref[...], g_ref[...], b_ref[...])


def _combine_ln(xf, y1, y2, gates, ln_g, ln_b):
    T, D = xf.shape
    steps = T // COMB_T
    assert steps >= COMB_SLOTS
    const = lambda shape: pl.BlockSpec(shape, lambda i: (0,) * len(shape))
    hbm = pl.BlockSpec(memory_space=pl.ANY)
    return pl.pallas_call(
        functools.partial(_combine_kernel, steps),
        grid=(steps,),
        in_specs=[hbm, hbm, hbm, pl.BlockSpec((SUBLANES, COMB_T), lambda i: (0, i)), const((1, D)), const((1, D))],
        out_specs=pl.BlockSpec((COMB_T, D), lambda i: (i, 0)),
        out_shape=jax.ShapeDtypeStruct((T, D), F32),
        scratch_shapes=[
            pltpu.VMEM((COMB_SLOTS, COMB_T, D), F32),
            pltpu.VMEM((COMB_SLOTS, COMB_T, D_HALF), U32),
            pltpu.VMEM((COMB_SLOTS, COMB_T, D_HALF), U32),
            pltpu.SemaphoreType.DMA((3, COMB_SLOTS)),
        ],
        compiler_params=pltpu.CompilerParams(
            dimension_semantics=("arbitrary",), vmem_limit_bytes=VMEM_LIMIT),
        name="moe_combine_ln",
    )(xf, y1, y2, gates, ln_g.reshape(1, D), ln_b.reshape(1, D))


def _hier_moe(xpk, logits_t, w1, w3, w2, layer, gather_parts):
    B, S, _ = xpk.shape
    T = B * S
    idx, gates, cnt = _router(logits_t)
    counts = cnt[:, 0]
    padded = ((counts + MOE_BM - 1) // MOE_BM) * MOE_BM
    pends = jnp.cumsum(padded)
    pstarts = pends - padded
    experts = jnp.arange(N_EXPERTS, dtype=jnp.int32)[:, None]

    def dest(e_row, rank_row):
        return jnp.sum(jnp.where(e_row[None, :] == experts, pstarts[:, None], 0), axis=0) + rank_row

    n_win = T // (SC_WORKERS * SC_WIN)
    dest1 = dest(idx[0], idx[2]).reshape(SC_WORKERS, n_win, SC_WIN)
    dest2 = dest(idx[1], idx[3]).reshape(SC_WORKERS, n_win, SC_WIN)
    n_blocks = -(-(T * TOP_K + N_EXPERTS * (MOE_BM - 1)) // MOE_BM)
    n_rows = n_blocks * MOE_BM
    block_start = jnp.arange(n_blocks, dtype=jnp.int32) * MOE_BM
    block_e = jnp.minimum(jnp.sum(block_start[:, None] >= pends[None, :], axis=1), N_EXPERTS - 1).astype(jnp.int32)
    of_block = block_e[:, None] == experts[:, 0][None, :]
    pick = lambda v: jnp.sum(jnp.where(of_block, v[None, :], 0), axis=1)
    valid = jnp.clip(pick(counts) - (block_start - pick(pstarts)), 0, MOE_BM).astype(jnp.int32)
    xbuf = _sc_dispatch(xpk.reshape(T, D_HALF), dest1, dest2, n_rows)
    ybuf = _moe_blocks(xbuf, block_e, valid, w1, w3, w2, layer)
    part_shape = (gather_parts, SC_WORKERS, n_win // gather_parts, SC_WIN)
    d1, d2 = dest1.reshape(part_shape), dest2.reshape(part_shape)
    y_parts = [_sc_gather_pair(ybuf, d1[p], d2[p]) for p in range(gather_parts)]
    return y_parts, gates


def _att_head_order():
    order = []
    for p in range(N_HEADS // 2):
        jj, m = divmod(p, 4)
        order += [8 * jj + m, 8 * jj + 4 + m]
    return order


ATT_HEAD_ORDER = _att_head_order()


def _attn_kernel(xprev_ref, y1_ref, y2_ref, gates_a_ref, gates_b_ref, g_prev_ref, b_prev_ref, wqkv_ref, bias_ref,
                 sink_ref, wo_ref, g_ref, b_ref, wrt_ref, brt_ref, o_ref, opk_ref, lg_ref,
                 kv_ext, o_sc, s_sc0, s_sc1, p_sc0, p_sc1):
    s = pl.program_id(1)
    tq = ATT_TQ
    s_bufs = (s_sc0, s_sc1)
    p_bufs = (p_sc0, p_sc1)
    gate_refs = (gates_a_ref, gates_b_ref)
    assert ATT_NSEQ == len(gate_refs)

    @pl.when(s == 0)
    def _():
        kv_ext[:, 0:WINDOW, :] = jnp.zeros((ATT_NSEQ, WINDOW, 2 * KV_DIM), BF16)

    def layer_input(sq):
        return _moe_combine_norm(xprev_ref[sq], y1_ref[sq], y2_ref[sq], gate_refs[sq][...],
                                 g_prev_ref[...], b_prev_ref[...])

    def project_qkv(sq, x):
        qkv = jnp.dot(x.astype(BF16), wqkv_ref[...], preferred_element_type=F32)
        kv_ext[sq, WINDOW:WINDOW + tq, :] = qkv[:, Q_DIM:].astype(BF16)
        return (qkv[:, :Q_DIM] * (HEAD_DIM ** -0.5 * LOG2E)).astype(BF16)

    lane = lax.broadcasted_iota(jnp.int32, (WINDOW, LANES), 1)
    low = lane < HEAD_DIM
    sub = lax.broadcasted_iota(jnp.int32, (LANES, WINDOW), 0)
    top = sub < HEAD_DIM
    first = jnp.where(s == 0, 1, 0)
    nt = (((1,), (1,)), ((), ()))
    zero = jnp.zeros((), BF16)

    tiles = [(sq, n, j) for n in range(ATT_NB) for j in range(2) for sq in range(ATT_NSEQ)]

    def scores(t, qs_all):
        sq, n, j = tiles[t]
        q = qs_all[sq]
        r0 = n * WINDOW
        k_tile = kv_ext[sq, r0:r0 + 2 * WINDOW, j * LANES:(j + 1) * LANES]
        parts = []
        for m in range(4):
            p = 4 * j + m
            qt = q[r0:r0 + WINDOW, p * LANES:(p + 1) * LANES]
            parts.append(jnp.where(low, qt, zero))
            parts.append(jnp.where(low, zero, qt))
        qs = jnp.concatenate(parts, axis=0)
        bias_sel = first if n == 0 else 0
        s_bufs[t % 2][...] = (lax.dot_general(k_tile, qs, nt, preferred_element_type=F32)
                              + bias_ref[bias_sel, j])

    def softmax_pv(t):
        sq, n, j = tiles[t]
        r0 = n * WINDOW
        s_sc = s_bufs[t % 2]
        p_sc = p_bufs[t % 2]
        inv_l = []
        for h in range(8):
            hc = slice(h * WINDOW, (h + 1) * WINDOW)
            sink = sink_ref[8 * j + h] * LOG2E
            mx = jnp.maximum(jnp.max(s_sc[:, hc], axis=0, keepdims=True), sink)
            pr = jnp.exp2(s_sc[:, hc] - mx)
            p_sc[:, hc] = pr.astype(BF16)
            inv_l.append(1.0 / (jnp.sum(pr, axis=0, keepdims=True) + jnp.exp2(sink - mx)))
        v_tile = kv_ext[sq, r0:r0 + 2 * WINDOW, KV_DIM + j * LANES:KV_DIM + (j + 1) * LANES]
        tn = (((0,), (0,)), ((), ()))
        ov = lax.dot_general(v_tile, p_sc[...], tn, preferred_element_type=F32)
        for m in range(4):
            p = 4 * j + m
            o_even = ov[:, (2 * m) * WINDOW:(2 * m + 1) * WINDOW] * inv_l[2 * m]
            o_odd = ov[:, (2 * m + 1) * WINDOW:(2 * m + 2) * WINDOW] * inv_l[2 * m + 1]
            o_sc[sq, p * LANES:(p + 1) * LANES, r0:r0 + WINDOW] = jnp.where(top, o_even, o_odd).astype(BF16)

    def project_out(sq):
        tn = (((0,), (0,)), ((), ()))
        return lax.dot_general(o_sc[sq], wo_ref[...], tn, preferred_element_type=F32)

    def finish(sq, x, out):
        xn = _layer_norm_rows(ALPHA * x + out, g_ref[...], b_ref[...])
        o_ref[sq] = xn
        opk_ref[sq] = _pack_row_halves(xn)
        lg_ref[sq] = _router_logits_t(xn, wrt_ref[...], brt_ref[...])

    xs = [layer_input(0), layer_input(1)]
    qs_all = [project_qkv(0, xs[0]), project_qkv(1, xs[1])]
    scores(0, qs_all)
    for t in range(len(tiles)):
        if t + 1 < len(tiles):
            scores(t + 1, qs_all)
        softmax_pv(t)
        if tiles[t][0] == 0 and all(sq != 0 for sq, _, _ in tiles[t + 1:]):
            out_first = project_out(0)
    kv_ext[:, 0:WINDOW, :] = kv_ext[:, tq:tq + WINDOW, :]
    out_second = project_out(1)
    finish(0, xs[0], out_first)
    finish(1, xs[1], out_second)


def _attn_bias():
    qi = np.arange(WINDOW)[:, None]
    sj = np.arange(2 * WINDOW)[None, :]
    dist = qi - sj + WINDOW
    valid = (dist >= 0) & (dist < WINDOW)
    slopes = 2.0 ** (-8.0 * np.arange(1, N_HEADS + 1, dtype=np.float32) / N_HEADS)
    slopes = slopes.astype(np.float32)[ATT_HEAD_ORDER]
    sb = -(slopes[:, None, None] * dist.astype(np.float32)[None])
    later = np.where(valid[None], sb, -np.inf)
    first = np.where((valid & (sj >= WINDOW))[None], sb, -np.inf)
    bias = np.stack([later, first]).astype(np.float32) * np.float32(LOG2E)
    bias = bias.reshape(2, 2, 8, WINDOW, 2 * WINDOW).transpose(0, 1, 4, 2, 3).reshape(2, 2, 2 * WINDOW, 8 * WINDOW)
    return jnp.asarray(np.ascontiguousarray(bias))


ATT_N_INPUTS = 15


def _attn_part_kernel(*refs):
    _attn_kernel(*refs[:ATT_N_INPUTS], *refs[ATT_N_INPUTS + 3:])


def _attn_layer(x_prev, y_parts, gates, g_prev, b_prev, w_qkv, sinks, w_o, ln_g, ln_b, router_w, router_b):
    B, S, D = x_prev.shape
    steps = S // ATT_TQ
    assert ATT_HEAD_ORDER == list(np.arange(N_HEADS).reshape(2, 2, 4).transpose(0, 2, 1).reshape(-1))
    wq = w_qkv[:, :Q_DIM].reshape(D, 2, 2, 4, HEAD_DIM).transpose(0, 1, 3, 2, 4).reshape(D, Q_DIM)
    wqkv = jnp.concatenate([wq, w_qkv[:, Q_DIM:]], axis=1).astype(BF16)
    wo_t = w_o.reshape(2, 2, 4, HEAD_DIM, D).transpose(0, 2, 1, 3, 4).reshape(Q_DIM, D).astype(BF16)
    sink = sinks.reshape(2, 2, 4).transpose(0, 2, 1).reshape(N_HEADS, 1, 1)
    bias = _attn_bias()
    const = lambda shape: pl.BlockSpec(shape, lambda b, s: (0,) * len(shape))
    n_parts = len(y_parts)
    rows_per_part = B // n_parts
    pairs = rows_per_part // ATT_NSEQ
    outs = None
    for part, (y1, y2) in enumerate(y_parts):
        off = part * pairs
        glob = lambda w, off=off: pl.BlockSpec((ATT_NSEQ, ATT_TQ, w), lambda b, s: (b + off, s, 0))
        local = lambda w: pl.BlockSpec((ATT_NSEQ, ATT_TQ, w), lambda b, s: (b, s, 0))
        gate_rows = lambda sq, off=off: pl.BlockSpec(
            (SUBLANES, ATT_TQ), lambda b, s: (0, (ATT_NSEQ * (b + off) + sq) * steps + s))
        in_specs = [
            glob(D),
            local(D_HALF),
            local(D_HALF),
            gate_rows(0),
            gate_rows(1),
            const((1, D)),
            const((1, D)),
            const((D, Q_DIM + 2 * KV_DIM)),
            const((2, 2, 2 * WINDOW, 8 * WINDOW)),
            const((N_HEADS, 1, 1)),
            const((Q_DIM, D)),
            const((1, D)),
            const((1, D)),
            const((2 * ROUTE_ROWS, D)),
            const((ROUTE_ROWS, 1)),
        ]
        args = [x_prev, y1.reshape(rows_per_part, S, D_HALF), y2.reshape(rows_per_part, S, D_HALF), gates, gates,
                g_prev.reshape(1, D), b_prev.reshape(1, D), wqkv, bias, sink, wo_t,
                ln_g.reshape(1, D), ln_b.reshape(1, D), router_w, router_b]
        aliases = {}
        body = _attn_kernel
        if outs is not None:
            aliases = {len(args) + k: k for k in range(3)}
            in_specs += [pl.BlockSpec(memory_space=pl.ANY)] * 3
            args += list(outs)
            body = _attn_part_kernel
        outs = pl.pallas_call(
            body,
            grid=(pairs, steps),
            in_specs=in_specs,
            out_specs=[glob(D), glob(D_HALF),
                       pl.BlockSpec((ATT_NSEQ, ROUTE_ROWS, ATT_TQ), lambda b, s, off=off: (b + off, 0, s))],
            out_shape=[jax.ShapeDtypeStruct((B, S, D), F32), jax.ShapeDtypeStruct((B, S, D_HALF), U32),
                       jax.ShapeDtypeStruct((B, ROUTE_ROWS, S), F32)],
            input_output_aliases=aliases,
            scratch_shapes=[
                pltpu.VMEM((ATT_NSEQ, ATT_TQ + WINDOW, 2 * KV_DIM), BF16),
                pltpu.VMEM((ATT_NSEQ, Q_DIM, ATT_TQ), BF16),
                pltpu.VMEM((2 * WINDOW, 8 * WINDOW), F32),
                pltpu.VMEM((2 * WINDOW, 8 * WINDOW), F32),
                pltpu.VMEM((2 * WINDOW, 8 * WINDOW), BF16),
                pltpu.VMEM((2 * WINDOW, 8 * WINDOW), BF16),
            ],
            compiler_params=pltpu.CompilerParams(
                dimension_semantics=("arbitrary", "arbitrary"), vmem_limit_bytes=VMEM_LIMIT),
            name="swa_attn_ln",
        )(*args)
    return outs


def kernel(x, rec_w_in, rec_conv_w, rec_conv_b, rec_w_r, rec_b_r, rec_w_i, rec_b_i, rec_lambda, rec_w_out,
           att_w_qkv, att_sinks, att_w_o, moe_w_group, moe_b_group, moe_w_expert, moe_b_expert,
           moe_w1, moe_w3, moe_w2, ln_g, ln_b):
    assert DEPTH == 2
    B, S, D = x.shape

    router = [_router_weights(moe_w_group[layer], moe_b_group[layer], moe_w_expert[layer], moe_b_expert[layer])
              for layer in range(DEPTH)]

    x1, x1_pk, logits1 = _rglru_layer(x, rec_w_in[0], rec_conv_w[0], rec_conv_b[0], rec_w_r[0], rec_b_r[0],
                                      rec_w_i[0], rec_b_i[0], rec_lambda[0], rec_w_out[0], ln_g[0, 0], ln_b[0, 0],
                                      *router[0])
    y_parts, gates = _hier_moe(x1_pk, logits1, moe_w1, moe_w3, moe_w2, 0, gather_parts=ATT_PARTS)
    x3, x3_pk, logits3 = _attn_layer(x1, y_parts, gates, ln_g[0, 1], ln_b[0, 1], att_w_qkv[0], att_sinks[0],
                                     att_w_o[0], ln_g[1, 0], ln_b[1, 0], *router[1])
    ((y1, y2),), gates = _hier_moe(x3_pk, logits3, moe_w1, moe_w3, moe_w2, 1, gather_parts=1)
    out = _combine_ln(x3.reshape(B * S, D), y1, y2, gates, ln_g[1, 1], ln_b[1, 1])
    return out.reshape(B, S, D)
```
